```python
import jax, jax.numpy as jnp
from jax import lax
import numpy as np

D_MODEL = 1024
BATCH = 8
SEQ = 8192
DEPTH = 1

N_META = 16
RET_HEADS = 8
RET_QK_DIM = 128
RET_V_DIM = 128
RET_CHUNK = 128
ROPE_BASE = 10000.0
LRU_WIDTH = D_MODEL
LRU_BLOCKS = 4
LRU_BLOCK = LRU_WIDTH // LRU_BLOCKS
CONV_WIDTH = 4
LRU_C = 8.0
FFN_HIDDEN = ((8 * D_MODEL // 3 + 255) // 256) * 256
RET_QK = RET_HEADS * RET_QK_DIM
RET_V = RET_HEADS * RET_V_DIM
IN_COLS = 2 * RET_QK + 2 * RET_V + 2 * LRU_WIDTH + 2 * D_MODEL
NORM_EPS = 1e-6

kernel_name = 'hybrid_retention_rglru_gated_block'


def rmsnorm(x, w):
    xf = x.astype(jnp.float32)
    y = xf * lax.rsqrt(jnp.mean(xf * xf, axis=-1, keepdims=True) + NORM_EPS)
    return (y * w.astype(jnp.float32)).astype(x.dtype)


def rotary(x, pos):
    d = x.shape[-1]
    inv_freq = ROPE_BASE ** (-jnp.arange(0, d, 2, dtype=jnp.float32) / d)
    ang = pos.astype(jnp.float32)[:, None] * inv_freq[None, :]
    cos = jnp.cos(ang)[None, :, None, :]
    sin = jnp.sin(ang)[None, :, None, :]
    x1, x2 = x[..., : d // 2], x[..., d // 2:]
    return jnp.concatenate([x1 * cos - x2 * sin, x2 * cos + x1 * sin], axis=-1)


def chunk_retention(q, k, v):
    B, T, H, dk = q.shape
    dv = v.shape[-1]
    C = RET_CHUNK
    pad = C - N_META
    padw = ((0, 0), (pad, 0), (0, 0), (0, 0))
    q, k, v = jnp.pad(q, padw), jnp.pad(k, padw), jnp.pad(v, padw)
    Tp = T + pad
    n_chunks = Tp // C
    log_g = jnp.log(1.0 - 2.0 ** (-5.0 - jnp.arange(H, dtype=jnp.float32)))
    idx = jnp.arange(C, dtype=jnp.float32)
    diff = idx[:, None] - idx[None, :]
    intra = jnp.where(diff[None] >= 0,
                      jnp.exp(jnp.maximum(diff, 0.0)[None] * log_g[:, None, None]), 0.0)
    q_decay = jnp.exp((idx + 1.0)[:, None] * log_g[None, :])
    k_decay = jnp.exp((C - 1.0 - idx)[:, None] * log_g[None, :])
    chunk_decay = jnp.exp(C * log_g)

    def to_chunks(a):
        return a.reshape(B, n_chunks, C, H, a.shape[-1]).transpose(1, 0, 2, 3, 4)

    def step(state, qkv):
        qc, kc, vc = qkv
        s = jnp.einsum('bchd,bmhd->bhcm', qc, kc) * intra[None]
        inner = jnp.einsum('bhcm,bmhe->bche', s, vc)
        cross = jnp.einsum('bchd,bhde->bche', qc, state) * q_decay[None, :, :, None]
        state = state * chunk_decay[None, :, None, None] + jnp.einsum(
            'bmhd,bmhe->bhde', kc * k_decay[None, :, :, None], vc)
        return state, inner + cross

    s0 = jnp.zeros((B, H, dk, dv), jnp.float32)
    _, ys = lax.scan(step, s0, (to_chunks(q), to_chunks(k), to_chunks(v)))
    out = ys.transpose(1, 0, 2, 3, 4).reshape(B, Tp, H, dv)
    return out[:, pad:]


def causal_depthwise_conv(x, w, b):
    T = x.shape[1]
    xp = jnp.pad(x, ((0, 0), (CONV_WIDTH - 1, 0), (0, 0)))
    y = b[None, None, :]
    for j in range(CONV_WIDTH):
        y = y + xp[:, j:j + T] * w[j][None, None, :]
    return y


def rg_lru(x, wa, ba, wx, bx, lam):
    B, T, W = x.shape
    xb = x.reshape(B, T, LRU_BLOCKS, LRU_BLOCK)
    r = jax.nn.sigmoid(jnp.einsum('btgi,gij->btgj', xb, wa).reshape(B, T, W) + ba)
    i = jax.nn.sigmoid(jnp.einsum('btgi,gij->btgj', xb, wx).reshape(B, T, W) + bx)
    log_a = -LRU_C * r * jax.nn.softplus(-lam)
    a = jnp.exp(log_a)
    u = jnp.sqrt(-jnp.expm1(2.0 * log_a)) * (i * x)

    def step(h, au):
        a_t, u_t = au
        h = a_t * h + u_t
        return h, h

    _, hs = lax.scan(step, jnp.zeros((B, W), jnp.float32),
                     (a.transpose(1, 0, 2), u.transpose(1, 0, 2)))
    return hs.transpose(1, 0, 2)


def _fwd_setup_inputs(seed: int = 0) -> dict:
    key = jax.random.key(seed)
    ks = jax.random.split(key, 20)
    f32 = jnp.float32
    nrm = lambda k, s, sc: jax.random.normal(k, s, f32) * sc
    a0 = jax.random.uniform(ks[9], (DEPTH, LRU_WIDTH), f32, minval=0.9, maxval=0.999)
    a0r = a0 ** (1.0 / LRU_C)
    return {
        'x': nrm(ks[0], (BATCH, SEQ, D_MODEL), 1.0),
        'meta_tokens': nrm(ks[1], (N_META, D_MODEL), 1.0),
        'mix_norm_w': 1.0 + nrm(ks[2], (DEPTH, D_MODEL), 0.02),
        'w_in': nrm(ks[3], (DEPTH, D_MODEL, IN_COLS), D_MODEL ** -0.5),
        'conv_w': nrm(ks[4], (DEPTH, CONV_WIDTH, LRU_WIDTH), CONV_WIDTH ** -0.5),
        'conv_b': nrm(ks[5], (DEPTH, LRU_WIDTH), 0.01),
        'lru_wa': nrm(ks[6], (DEPTH, LRU_BLOCKS, LRU_BLOCK, LRU_BLOCK), LRU_BLOCK ** -0.5),
        'lru_ba': nrm(ks[7], (DEPTH, LRU_WIDTH), 0.01),
        'lru_wx': nrm(ks[8], (DEPTH, LRU_BLOCKS, LRU_BLOCK, LRU_BLOCK), LRU_BLOCK ** -0.5),
        'lru_bx': nrm(ks[10], (DEPTH, LRU_WIDTH), 0.01),
        'lru_lambda': jnp.log(a0r) - jnp.log1p(-a0r),
        'w_branch_ret': nrm(ks[11], (DEPTH, RET_V, D_MODEL), RET_V ** -0.5),
        'w_branch_lru': nrm(ks[12], (DEPTH, LRU_WIDTH, D_MODEL), LRU_WIDTH ** -0.5),
        'w_out': nrm(ks[13], (DEPTH, D_MODEL, D_MODEL), D_MODEL ** -0.5),
        'ffn_norm_w': 1.0 + nrm(ks[14], (DEPTH, D_MODEL), 0.02),
        'w_ffn_in': nrm(ks[15], (DEPTH, D_MODEL, 2 * FFN_HIDDEN), D_MODEL ** -0.5),
        'w_ffn_out': nrm(ks[16], (DEPTH, FFN_HIDDEN, D_MODEL), FFN_HIDDEN ** -0.5),
        'final_norm_w': 1.0 + nrm(ks[17], (D_MODEL,), 0.02),
    }


def _fwd_reference(x, meta_tokens, mix_norm_w, w_in, conv_w, conv_b, lru_wa, lru_ba, lru_wx,
              lru_bx, lru_lambda, w_branch_ret, w_branch_lru, w_out, ffn_norm_w,
              w_ffn_in, w_ffn_out, final_norm_w):
    B = x.shape[0]
    f32 = jnp.float32
    meta = jnp.broadcast_to(meta_tokens.astype(x.dtype)[None], (B, N_META, D_MODEL))
    h = jnp.concatenate([meta, x], axis=1)
    T = h.shape[1]
    pos = jnp.arange(T)
    sizes = (RET_QK, RET_QK, RET_V, RET_V, LRU_WIDTH, LRU_WIDTH, D_MODEL, D_MODEL)
    split_at = [int(s) for s in np.cumsum(sizes)[:-1]]
    for l in range(DEPTH):
        u = rmsnorm(h, mix_norm_w[l])
        proj = u @ w_in[l]
        q, k, v, g_ret, lru_in, lru_gate, gate_a, gate_b = jnp.split(proj, split_at, axis=-1)
        q = rotary(q.reshape(B, T, RET_HEADS, RET_QK_DIM).astype(f32), pos)
        k = rotary(k.reshape(B, T, RET_HEADS, RET_QK_DIM).astype(f32), pos) * (RET_QK_DIM ** -0.5)
        v = v.reshape(B, T, RET_HEADS, RET_V_DIM).astype(f32)
        o = chunk_retention(q, k, v)
        o = o * lax.rsqrt(jnp.mean(o * o, axis=-1, keepdims=True) + NORM_EPS)
        o = o.reshape(B, T, RET_V).astype(h.dtype)
        y_ret = (jax.nn.silu(g_ret) * o) @ w_branch_ret[l]
        c = causal_depthwise_conv(lru_in.astype(f32), conv_w[l].astype(f32), conv_b[l].astype(f32))
        r = rg_lru(c, lru_wa[l].astype(f32), lru_ba[l].astype(f32), lru_wx[l].astype(f32),
                   lru_bx[l].astype(f32), lru_lambda[l].astype(f32)).astype(h.dtype)
        y_lru = (jax.nn.gelu(lru_gate) * r) @ w_branch_lru[l]
        mixed = jax.nn.sigmoid(gate_a) * y_ret + jax.nn.sigmoid(gate_b) * y_lru
        h = h + mixed @ w_out[l]
        u = rmsnorm(h, ffn_norm_w[l])
        gu = u @ w_ffn_in[l]
        g, up = gu[..., :FFN_HIDDEN], gu[..., FFN_HIDDEN:]
        h = h + (jax.nn.silu(g) * up) @ w_ffn_out[l]
    h = rmsnorm(h, final_norm_w)
    return h[:, N_META:]


import jax as _jax
import jax.numpy as _jnp

TWIN_FORMAT = 'train_step'
FWD_PARAMS = ['x', 'meta_tokens', 'mix_norm_w', 'w_in', 'conv_w', 'conv_b', 'lru_wa', 'lru_ba', 'lru_wx', 'lru_bx', 'lru_lambda', 'w_branch_ret', 'w_branch_lru', 'w_out', 'ffn_norm_w', 'w_ffn_in', 'w_ffn_out', 'final_norm_w']
TWIN_WEIGHTS = ['meta_tokens', 'mix_norm_w', 'w_in', 'conv_w', 'conv_b', 'lru_wa', 'lru_ba', 'lru_wx', 'lru_bx', 'lru_lambda', 'w_branch_ret', 'w_branch_lru', 'w_out', 'ffn_norm_w', 'w_ffn_in', 'w_ffn_out', 'final_norm_w']
TWIN_DIFF_INPUT = 'x'
TWIN_INPUTS = ['x', 'meta_tokens', 'mix_norm_w', 'w_in', 'conv_w', 'conv_b', 'lru_wa', 'lru_ba', 'lru_wx', 'lru_bx', 'lru_lambda', 'w_branch_ret', 'w_branch_lru', 'w_out', 'ffn_norm_w', 'w_ffn_in', 'w_ffn_out', 'final_norm_w', 'loss_target', 'm_meta_tokens', 'm_mix_norm_w', 'm_w_in', 'm_conv_w', 'm_conv_b', 'm_lru_wa', 'm_lru_ba', 'm_lru_wx', 'm_lru_bx', 'm_lru_lambda', 'm_w_branch_ret', 'm_w_branch_lru', 'm_w_out', 'm_ffn_norm_w', 'm_w_ffn_in', 'm_w_ffn_out', 'm_final_norm_w', 'v_meta_tokens', 'v_mix_norm_w', 'v_w_in', 'v_conv_w', 'v_conv_b', 'v_lru_wa', 'v_lru_ba', 'v_lru_wx', 'v_lru_bx', 'v_lru_lambda', 'v_w_branch_ret', 'v_w_branch_lru', 'v_w_out', 'v_ffn_norm_w', 'v_w_ffn_in', 'v_w_ffn_out', 'v_final_norm_w']
TWIN_OUTPUTS = ['loss', 'grad_x', 'grad_meta_tokens', 'grad_mix_norm_w', 'grad_w_in', 'grad_conv_w', 'grad_conv_b', 'grad_lru_wa', 'grad_lru_ba', 'grad_lru_wx', 'grad_lru_bx', 'grad_lru_lambda', 'grad_w_branch_ret', 'grad_w_branch_lru', 'grad_w_out', 'grad_ffn_norm_w', 'grad_w_ffn_in', 'grad_w_ffn_out', 'grad_final_norm_w', 'delta_meta_tokens', 'delta_mix_norm_w', 'delta_w_in', 'delta_conv_w', 'delta_conv_b', 'delta_lru_wa', 'delta_lru_ba', 'delta_lru_wx', 'delta_lru_bx', 'delta_lru_lambda', 'delta_w_branch_ret', 'delta_w_branch_lru', 'delta_w_out', 'delta_ffn_norm_w', 'delta_w_ffn_in', 'delta_w_ffn_out', 'delta_final_norm_w', 'new_m_meta_tokens', 'new_m_mix_norm_w', 'new_m_w_in', 'new_m_conv_w', 'new_m_conv_b', 'new_m_lru_wa', 'new_m_lru_ba', 'new_m_lru_wx', 'new_m_lru_bx', 'new_m_lru_lambda', 'new_m_w_branch_ret', 'new_m_w_branch_lru', 'new_m_w_out', 'new_m_ffn_norm_w', 'new_m_w_ffn_in', 'new_m_w_ffn_out', 'new_m_final_norm_w', 'new_v_meta_tokens', 'new_v_mix_norm_w', 'new_v_w_in', 'new_v_conv_w', 'new_v_conv_b', 'new_v_lru_wa', 'new_v_lru_ba', 'new_v_lru_wx', 'new_v_lru_bx', 'new_v_lru_lambda', 'new_v_w_branch_ret', 'new_v_w_branch_lru', 'new_v_w_out', 'new_v_ffn_norm_w', 'new_v_w_ffn_in', 'new_v_w_ffn_out', 'new_v_final_norm_w']
TWIN_LEAF_KINDS = {'loss': 'loss', 'grad_x': 'grad_x', 'grad_meta_tokens': 'grad_w', 'grad_mix_norm_w': 'grad_w', 'grad_w_in': 'grad_w', 'grad_conv_w': 'grad_w', 'grad_conv_b': 'grad_w', 'grad_lru_wa': 'grad_w', 'grad_lru_ba': 'grad_w', 'grad_lru_wx': 'grad_w', 'grad_lru_bx': 'grad_w', 'grad_lru_lambda': 'grad_w', 'grad_w_branch_ret': 'grad_w', 'grad_w_branch_lru': 'grad_w', 'grad_w_out': 'grad_w', 'grad_ffn_norm_w': 'grad_w', 'grad_w_ffn_in': 'grad_w', 'grad_w_ffn_out': 'grad_w', 'grad_final_norm_w': 'grad_w', 'delta_meta_tokens': 'delta_w', 'delta_mix_norm_w': 'delta_w', 'delta_w_in': 'delta_w', 'delta_conv_w': 'delta_w', 'delta_conv_b': 'delta_w', 'delta_lru_wa': 'delta_w', 'delta_lru_ba': 'delta_w', 'delta_lru_wx': 'delta_w', 'delta_lru_bx': 'delta_w', 'delta_lru_lambda': 'delta_w', 'delta_w_branch_ret': 'delta_w', 'delta_w_branch_lru': 'delta_w', 'delta_w_out': 'delta_w', 'delta_ffn_norm_w': 'delta_w', 'delta_w_ffn_in': 'delta_w', 'delta_w_ffn_out': 'delta_w', 'delta_final_norm_w': 'delta_w', 'new_m_meta_tokens': 'new_m', 'new_m_mix_norm_w': 'new_m', 'new_m_w_in': 'new_m', 'new_m_conv_w': 'new_m', 'new_m_conv_b': 'new_m', 'new_m_lru_wa': 'new_m', 'new_m_lru_ba': 'new_m', 'new_m_lru_wx': 'new_m', 'new_m_lru_bx': 'new_m', 'new_m_lru_lambda': 'new_m', 'new_m_w_branch_ret': 'new_m', 'new_m_w_branch_lru': 'new_m', 'new_m_w_out': 'new_m', 'new_m_ffn_norm_w': 'new_m', 'new_m_w_ffn_in': 'new_m', 'new_m_w_ffn_out': 'new_m', 'new_m_final_norm_w': 'new_m', 'new_v_meta_tokens': 'new_v', 'new_v_mix_norm_w': 'new_v', 'new_v_w_in': 'new_v', 'new_v_conv_w': 'new_v', 'new_v_conv_b': 'new_v', 'new_v_lru_wa': 'new_v', 'new_v_lru_ba': 'new_v', 'new_v_lru_wx': 'new_v', 'new_v_lru_bx': 'new_v', 'new_v_lru_lambda': 'new_v', 'new_v_w_branch_ret': 'new_v', 'new_v_w_branch_lru': 'new_v', 'new_v_w_out': 'new_v', 'new_v_ffn_norm_w': 'new_v', 'new_v_w_ffn_in': 'new_v', 'new_v_w_ffn_out': 'new_v', 'new_v_final_norm_w': 'new_v'}


def _forward(args):
    return _fwd_reference(*[args[k] for k in FWD_PARAMS])


def _output_shape():
    out = _jax.eval_shape(lambda: _forward(_fwd_setup_inputs(0)))
    return out.shape, out.dtype

N_MICROBATCH = 1
ADAM_LR = 0.001
ADAM_B1 = 0.9
ADAM_B2 = 0.999
ADAM_EPS = 1e-08
ADAM_WD = 0.01
ADAM_STEP = 10
PER_EXAMPLE_BATCH_AXIS = {'x': 0, 'loss_target': 0}
SHARED_INPUTS = []
_WEIGHT_DTYPES = {'meta_tokens': _jnp.float32, 'mix_norm_w': _jnp.float32, 'w_in': _jnp.float32, 'conv_w': _jnp.float32, 'conv_b': _jnp.float32, 'lru_wa': _jnp.float32, 'lru_ba': _jnp.float32, 'lru_wx': _jnp.float32, 'lru_bx': _jnp.float32, 'lru_lambda': _jnp.float32, 'w_branch_ret': _jnp.float32, 'w_branch_lru': _jnp.float32, 'w_out': _jnp.float32, 'ffn_norm_w': _jnp.float32, 'w_ffn_in': _jnp.float32, 'w_ffn_out': _jnp.float32, 'final_norm_w': _jnp.float32}
MOMENT_SCALE = {'meta_tokens': 1.099513e-02, 'mix_norm_w': 1.918462e-01, 'w_in': 6.872838e-02, 'conv_w': 6.713436e-02, 'conv_b': 7.676715e-01, 'lru_wa': 1.798613e-02, 'lru_ba': 1.884773e-02, 'lru_wx': 3.225532e-02, 'lru_bx': 2.250529e-02, 'lru_lambda': 3.821707e-02, 'w_branch_ret': 8.464297e-02, 'w_branch_lru': 6.116056e-02, 'w_out': 1.032831e-01, 'ffn_norm_w': 1.807560e-01, 'w_ffn_in': 7.491673e-02, 'w_ffn_out': 1.224321e-01, 'final_norm_w': 6.392210e+01}


def _to_microbatches(a, axis):
    t = _jnp.moveaxis(a, axis, 0)
    t = t.reshape((N_MICROBATCH, t.shape[0] // N_MICROBATCH) + t.shape[1:])
    return _jnp.moveaxis(t, 1, axis + 1)


def setup_inputs(seed: int = 0) -> dict:
    inp = _fwd_setup_inputs(seed)
    key = _jax.random.fold_in(_jax.random.key(seed), 7919)
    shape, _ = _output_shape()
    out = dict(inp)
    out["loss_target"] = _jax.random.normal(_jax.random.fold_in(key, 0), shape, _jnp.float32)
    for i, name in enumerate(TWIN_WEIGHTS):
        w = inp[name].astype(_jnp.float32)
        if MOMENT_SCALE is None:
            s = _jnp.sqrt(_jnp.mean(_jnp.square(w)) + 1e-30)
        else:
            s = MOMENT_SCALE[name]
        km, kv = _jax.random.split(_jax.random.fold_in(key, i + 1))
        out[name] = w
        out["m_" + name] = s * _jax.random.normal(km, w.shape, _jnp.float32)
        out["v_" + name] = (s * s) * _jax.random.uniform(kv, w.shape, _jnp.float32, 0.5, 1.5)
    if N_MICROBATCH > 1:
        for name, axis in PER_EXAMPLE_BATCH_AXIS.items():
            out[name] = _to_microbatches(out[name], axis)
    return {'x': out['x'], 'meta_tokens': out['meta_tokens'], 'mix_norm_w': out['mix_norm_w'], 'w_in': out['w_in'], 'conv_w': out['conv_w'], 'conv_b': out['conv_b'], 'lru_wa': out['lru_wa'], 'lru_ba': out['lru_ba'], 'lru_wx': out['lru_wx'], 'lru_bx': out['lru_bx'], 'lru_lambda': out['lru_lambda'], 'w_branch_ret': out['w_branch_ret'], 'w_branch_lru': out['w_branch_lru'], 'w_out': out['w_out'], 'ffn_norm_w': out['ffn_norm_w'], 'w_ffn_in': out['w_ffn_in'], 'w_ffn_out': out['w_ffn_out'], 'final_norm_w': out['final_norm_w'], 'loss_target': out['loss_target'], 'm_meta_tokens': out['m_meta_tokens'], 'm_mix_norm_w': out['m_mix_norm_w'], 'm_w_in': out['m_w_in'], 'm_conv_w': out['m_conv_w'], 'm_conv_b': out['m_conv_b'], 'm_lru_wa': out['m_lru_wa'], 'm_lru_ba': out['m_lru_ba'], 'm_lru_wx': out['m_lru_wx'], 'm_lru_bx': out['m_lru_bx'], 'm_lru_lambda': out['m_lru_lambda'], 'm_w_branch_ret': out['m_w_branch_ret'], 'm_w_branch_lru': out['m_w_branch_lru'], 'm_w_out': out['m_w_out'], 'm_ffn_norm_w': out['m_ffn_norm_w'], 'm_w_ffn_in': out['m_w_ffn_in'], 'm_w_ffn_out': out['m_w_ffn_out'], 'm_final_norm_w': out['m_final_norm_w'], 'v_meta_tokens': out['v_meta_tokens'], 'v_mix_norm_w': out['v_mix_norm_w'], 'v_w_in': out['v_w_in'], 'v_conv_w': out['v_conv_w'], 'v_conv_b': out['v_conv_b'], 'v_lru_wa': out['v_lru_wa'], 'v_lru_ba': out['v_lru_ba'], 'v_lru_wx': out['v_lru_wx'], 'v_lru_bx': out['v_lru_bx'], 'v_lru_lambda': out['v_lru_lambda'], 'v_w_branch_ret': out['v_w_branch_ret'], 'v_w_branch_lru': out['v_w_branch_lru'], 'v_w_out': out['v_w_out'], 'v_ffn_norm_w': out['v_ffn_norm_w'], 'v_w_ffn_in': out['v_w_ffn_in'], 'v_w_ffn_out': out['v_w_ffn_out'], 'v_final_norm_w': out['v_final_norm_w']}


def _loss(weights, diff, rest, loss_target):
    with _jax.named_scope("forward"):
        args = {**rest, TWIN_DIFF_INPUT: diff, **{k: w.astype(_WEIGHT_DTYPES[k]) for k, w in weights.items()}}
        y = _forward(args)
    with _jax.named_scope("loss_head"):
        err = _jnp.square(y.astype(_jnp.float32) - loss_target)
        return 0.5 * _jnp.sum(_jnp.mean(err, axis=-1)) if err.ndim else 0.5 * err


def _adamw(w, g, m, v):
    m = ADAM_B1 * m + (1.0 - ADAM_B1) * g
    v = ADAM_B2 * v + (1.0 - ADAM_B2) * _jnp.square(g)
    m_hat = m / (1.0 - ADAM_B1 ** ADAM_STEP)
    v_hat = v / (1.0 - ADAM_B2 ** ADAM_STEP)
    delta = -ADAM_LR * (m_hat / (_jnp.sqrt(v_hat) + ADAM_EPS) + ADAM_WD * w)
    return delta, m, v


def reference(x, meta_tokens, mix_norm_w, w_in, conv_w, conv_b, lru_wa, lru_ba, lru_wx, lru_bx, lru_lambda, w_branch_ret, w_branch_lru, w_out, ffn_norm_w, w_ffn_in, w_ffn_out, final_norm_w, loss_target, m_meta_tokens, m_mix_norm_w, m_w_in, m_conv_w, m_conv_b, m_lru_wa, m_lru_ba, m_lru_wx, m_lru_bx, m_lru_lambda, m_w_branch_ret, m_w_branch_lru, m_w_out, m_ffn_norm_w, m_w_ffn_in, m_w_ffn_out, m_final_norm_w, v_meta_tokens, v_mix_norm_w, v_w_in, v_conv_w, v_conv_b, v_lru_wa, v_lru_ba, v_lru_wx, v_lru_bx, v_lru_lambda, v_w_branch_ret, v_w_branch_lru, v_w_out, v_ffn_norm_w, v_w_ffn_in, v_w_ffn_out, v_final_norm_w):
    given = dict(x=x, meta_tokens=meta_tokens, mix_norm_w=mix_norm_w, w_in=w_in, conv_w=conv_w, conv_b=conv_b, lru_wa=lru_wa, lru_ba=lru_ba, lru_wx=lru_wx, lru_bx=lru_bx, lru_lambda=lru_lambda, w_branch_ret=w_branch_ret, w_branch_lru=w_branch_lru, w_out=w_out, ffn_norm_w=ffn_norm_w, w_ffn_in=w_ffn_in, w_ffn_out=w_ffn_out, final_norm_w=final_norm_w, loss_target=loss_target, m_meta_tokens=m_meta_tokens, m_mix_norm_w=m_mix_norm_w, m_w_in=m_w_in, m_conv_w=m_conv_w, m_conv_b=m_conv_b, m_lru_wa=m_lru_wa, m_lru_ba=m_lru_ba, m_lru_wx=m_lru_wx, m_lru_bx=m_lru_bx, m_lru_lambda=m_lru_lambda, m_w_branch_ret=m_w_branch_ret, m_w_branch_lru=m_w_branch_lru, m_w_out=m_w_out, m_ffn_norm_w=m_ffn_norm_w, m_w_ffn_in=m_w_ffn_in, m_w_ffn_out=m_w_ffn_out, m_final_norm_w=m_final_norm_w, v_meta_tokens=v_meta_tokens, v_mix_norm_w=v_mix_norm_w, v_w_in=v_w_in, v_conv_w=v_conv_w, v_conv_b=v_conv_b, v_lru_wa=v_lru_wa, v_lru_ba=v_lru_ba, v_lru_wx=v_lru_wx, v_lru_bx=v_lru_bx, v_lru_lambda=v_lru_lambda, v_w_branch_ret=v_w_branch_ret, v_w_branch_lru=v_w_branch_lru, v_w_out=v_w_out, v_ffn_norm_w=v_ffn_norm_w, v_w_ffn_in=v_w_ffn_in, v_w_ffn_out=v_w_ffn_out, v_final_norm_w=v_final_norm_w)
    weights = {n: given[n] for n in TWIN_WEIGHTS}
    shared = {n: given[n] for n in SHARED_INPUTS}
    per_example = {n: given[n] for n in ['x']}
    grad_fn = _jax.value_and_grad(_loss, argnums=(0, 1))

    def one_microbatch(ex, loss_target):
        ex = dict(ex)
        diff = ex.pop(TWIN_DIFF_INPUT)
        return grad_fn(weights, diff, {**shared, **ex}, loss_target)

    if N_MICROBATCH == 1:
        loss, (grad_w, grad_x) = one_microbatch(per_example, given["loss_target"])
    else:
        def body(carry, xs):
            loss_sum, grad_sum = carry
            l_k, (gw_k, gx_k) = one_microbatch(xs[0], xs[1])
            with _jax.named_scope("update"):
                return (loss_sum + l_k, _jax.tree.map(_jnp.add, grad_sum, gw_k)), gx_k

        init = (_jnp.zeros((), _jnp.float32), _jax.tree.map(_jnp.zeros_like, weights))
        (loss, grad_w), grad_x = _jax.lax.scan(body, init, (per_example, given["loss_target"]))
    with _jax.named_scope("update"):
        delta_w, new_m, new_v = {}, {}, {}
        for n in TWIN_WEIGHTS:
            delta_w[n], new_m[n], new_v[n] = _adamw(weights[n], grad_w[n], given["m_" + n], given["v_" + n])
    return (loss, grad_x, *[grad_w[n] for n in TWIN_WEIGHTS], *[delta_w[n] for n in TWIN_WEIGHTS],
            *[new_m[n] for n in TWIN_WEIGHTS], *[new_v[n] for n in TWIN_WEIGHTS])
```

```python
import functools

import numpy as np
import jax
import jax.numpy as jnp
from jax import lax
from jax.experimental import pallas as pl
from jax.experimental.pallas import tpu as pltpu

F32 = jnp.float32
BF16 = jnp.bfloat16

D_MODEL = 1024
N_META = 16
CHUNK = 128
PAD_ROWS = CHUNK - N_META
HEADS = 8
HEAD_DIM = 128
ROPE_BASE = 10000.0
QK_SCALE = HEAD_DIM ** -0.5
LRU_BLOCKS = 4
LRU_BLOCK = 256
LRU_C = 8.0
FFN_HIDDEN = 2816
N_DEV = 8
FFN_SHARD = 2 * FFN_HIDDEN // N_DEV
FFN_GROUP = 768
FFN_GROUPS = 4
FFN_OUT_SHARD = FFN_HIDDEN // N_DEV
NORM_EPS = 1e-6

ADAM_LR = 0.001
ADAM_B1 = 0.9
ADAM_B2 = 0.999
ADAM_EPS = 1e-08
ADAM_WD = 0.01
ADAM_STEP = 10

VMEM_LIMIT = 56 * 1024 * 1024
MESH_ID = pl.DeviceIdType.MESH
ANY = pl.BlockSpec(memory_space=pl.ANY)


def _cparams(sem):
    return pltpu.CompilerParams(dimension_semantics=sem, vmem_limit_bytes=VMEM_LIMIT)


def _tile(rows, cap):
    t = cap - cap % 64
    while rows % t:
        t -= 64
    return t


def _dot(a, b):
    return jnp.dot(a, b, preferred_element_type=F32)


def _dot_nt(a, b):
    return lax.dot_general(a, b, (((1,), (1,)), ((), ())), preferred_element_type=F32)


def _dot_tn(a, b):
    return lax.dot_general(a, b, (((0,), (0,)), ((), ())), preferred_element_type=F32)


def _sigmoid(x):
    return 1.0 / (1.0 + jnp.exp(-x))


def _gelu_parts(x):
    k = 0.7978845608028654
    inner = k * (x + 0.044715 * x * x * x)
    t = jnp.tanh(inner)
    g = 0.5 * x * (1.0 + t)
    dg = 0.5 * (1.0 + t) + 0.5 * x * (1.0 - t * t) * k * (1.0 + 3.0 * 0.044715 * x * x)
    return g, dg


def _rot(x, cos2, sin2):
    return x * cos2 + pltpu.roll(x, HEAD_DIM // 2, 1) * sin2


def _rot_t(dx, cos2, sin2):
    return dx * cos2 - pltpu.roll(dx, HEAD_DIM // 2, 1) * sin2


def _rms_bwd(x, w, dy):
    rs = lax.rsqrt(jnp.mean(x * x, axis=-1, keepdims=True) + NORM_EPS)
    nh = x * rs
    dw = jnp.sum(dy * nh, axis=0, keepdims=True)
    dn = dy * w
    dx = rs * (dn - nh * jnp.mean(dn * nh, axis=-1, keepdims=True))
    return dx, dw


def _retention_consts():
    h = jnp.arange(HEADS, dtype=F32)
    log_g = jnp.log(1.0 - 2.0 ** (-5.0 - h))
    idx = jnp.arange(CHUNK, dtype=F32)
    diff = idx[:, None] - idx[None, :]
    intra = jnp.where(diff[None] >= 0, jnp.exp(jnp.maximum(diff, 0.0)[None] * log_g[:, None, None]), 0.0)
    q_decay = jnp.exp((idx + 1.0)[:, None] * log_g[None, :])
    k_decay = jnp.exp((CHUNK - 1.0 - idx)[:, None] * log_g[None, :])
    chunk_decay = jnp.exp(CHUNK * log_g)
    shape = (HEADS, CHUNK, CHUNK)
    qd = jnp.broadcast_to(q_decay.T[:, :, None], shape)
    kd = jnp.broadcast_to(k_decay.T[:, :, None], shape)
    cd = jnp.broadcast_to(chunk_decay[:, None, None], shape)
    return jnp.stack([intra, qd, kd, cd])


def _rope_tables(rows):
    pos = jnp.maximum(jnp.arange(rows) - PAD_ROWS, 0).astype(F32)
    inv_freq = ROPE_BASE ** (-jnp.arange(0, HEAD_DIM, 2, dtype=F32) / HEAD_DIM)
    ang = pos[:, None] * inv_freq[None, :]
    cos, sin = jnp.cos(ang), jnp.sin(ang)
    return jnp.concatenate([cos, cos], axis=1), jnp.concatenate([-sin, sin], axis=1)


def _in_proj(h0, norm_w, win_g):
    rows = h0.shape[0]
    tm = _tile(rows, 640)

    def body(h_ref, nw_ref, w_ref, proj_ref, u_ref, u_s):
        @pl.when(pl.program_id(1) == 0)
        def _():
            x = h_ref[...]
            rs = lax.rsqrt(jnp.mean(x * x, axis=-1, keepdims=True) + NORM_EPS)
            u = (x * rs * nw_ref[...]).astype(BF16)
            u_s[...] = u
            u_ref[...] = u
        proj_ref[...] = _dot(u_s[...], w_ref[...])

    return pl.pallas_call(
        body, name="in_proj", grid=(rows // tm, N_DEV),
        in_specs=[pl.BlockSpec((tm, D_MODEL), lambda i, j: (i, 0)),
                  pl.BlockSpec((1, D_MODEL), lambda i, j: (0, 0)),
                  pl.BlockSpec((None, D_MODEL, D_MODEL), lambda i, j: (j, 0, 0))],
        out_specs=[pl.BlockSpec((tm, D_MODEL), lambda i, j: (i, j)),
                   pl.BlockSpec((tm, D_MODEL), lambda i, j: (i, 0))],
        out_shape=[jax.ShapeDtypeStruct((rows, N_DEV * D_MODEL), F32),
                   jax.ShapeDtypeStruct((rows, D_MODEL), BF16)],
        scratch_shapes=[pltpu.VMEM((tm, D_MODEL), BF16)],
        compiler_params=_cparams(("parallel", "arbitrary")),
    )(h0, norm_w, win_g)


def _seg_spec(rows_per_block, seg):
    return pl.BlockSpec((rows_per_block, D_MODEL), lambda n, seg=seg: (n, seg))


def _retention_fwd(proj, cos2, sin2, dec):
    rows = proj.shape[0]
    n_chunks = rows // CHUNK

    def body(q_ref, k_ref, v_ref, g_ref, c_ref, s_ref, dec_ref, o_ref, zr_ref, st_ref, state):
        @pl.when(pl.program_id(0) == 0)
        def _():
            state[...] = jnp.zeros_like(state)
        cos_t, sin_t = c_ref[...], s_ref[...]
        for h in range(HEADS):
            sl = slice(HEAD_DIM * h, HEAD_DIM * (h + 1))
            qh = _rot(q_ref[:, sl], cos_t, sin_t)
            kh = _rot(k_ref[:, sl], cos_t, sin_t) * QK_SCALE
            qb, kb, vb = qh.astype(BF16), kh.astype(BF16), v_ref[:, sl].astype(BF16)
            s = _dot_nt(qb, kb) * dec_ref[0, h]
            st = state[h]
            st_ref[h] = st
            o = _dot(s.astype(BF16), vb) + _dot(qb, st.astype(BF16)) * dec_ref[1, h]
            state[h] = st * dec_ref[3, h] + _dot_tn((kh * dec_ref[2, h]).astype(BF16), vb)
            o_ref[:, sl] = o
            r = lax.rsqrt(jnp.mean(o * o, axis=-1, keepdims=True) + NORM_EPS)
            g = g_ref[:, sl]
            zr_ref[:, sl] = (g * _sigmoid(g) * (o * r)).astype(BF16)

    tab = pl.BlockSpec((CHUNK, HEAD_DIM), lambda n: (n, 0))
    return pl.pallas_call(
        body, name="retention_fwd", grid=(n_chunks,),
        in_specs=[_seg_spec(CHUNK, 0), _seg_spec(CHUNK, 1), _seg_spec(CHUNK, 2), _seg_spec(CHUNK, 3), tab, tab,
                  pl.BlockSpec((4, HEADS, CHUNK, CHUNK), lambda n: (0, 0, 0, 0))],
        out_specs=[pl.BlockSpec((CHUNK, D_MODEL), lambda n: (n, 0)),
                   pl.BlockSpec((CHUNK, D_MODEL), lambda n: (n, 0)),
                   pl.BlockSpec((None, HEADS, HEAD_DIM, HEAD_DIM), lambda n: (n, 0, 0, 0))],
        out_shape=[jax.ShapeDtypeStruct((rows, D_MODEL), F32),
                   jax.ShapeDtypeStruct((rows, D_MODEL), BF16),
                   jax.ShapeDtypeStruct((n_chunks, HEADS, HEAD_DIM, HEAD_DIM), F32)],
        scratch_shapes=[pltpu.VMEM((HEADS, HEAD_DIM, HEAD_DIM), F32)],
        compiler_params=_cparams(("arbitrary",)),
    )(proj, proj, proj, proj, cos2, sin2, dec)


def _lru_gates(c, ba, bx, lam, wa_ref, wx_ref):
    pre_r, pre_i = [], []
    for g in range(LRU_BLOCKS):
        cg = c[:, LRU_BLOCK * g:LRU_BLOCK * (g + 1)].astype(BF16)
        pre_r.append(_dot(cg, wa_ref[g]))
        pre_i.append(_dot(cg, wx_ref[g]))
    r = _sigmoid(jnp.concatenate(pre_r, axis=1) + ba)
    ig = _sigmoid(jnp.concatenate(pre_i, axis=1) + bx)
    sp = jnp.maximum(-lam, 0.0) + jnp.log(1.0 + jnp.exp(-jnp.abs(lam)))
    log_a = -LRU_C * r * sp
    a = jnp.exp(log_a)
    mult = jnp.sqrt(-jnp.tanh(log_a) * (a * a + 1.0))
    return r, ig, a, mult, sp


def _conv_taps(xbuf, tm, cw_ref, cb_ref):
    c = cb_ref[...] + cw_ref[3:4, :] * xbuf[8:8 + tm, :]
    for back in (1, 2, 3):
        c = c + cw_ref[3 - back:4 - back, :] * xbuf[8 - back:8 - back + tm, :]
    return c


def _lru_fwd(proj, conv_w, conv_b, ba, bx, lam, wa_g, wx_g):
    rows = proj.shape[0]
    tm = _tile(rows, 320)

    def body(x_ref, gt_ref, cw_ref, cb_ref, ba_ref, bx_ref, lam_ref, wa_ref, wx_ref, hs_ref, zl_ref,
             xbuf, abuf, ubuf, hcar):
        i = pl.program_id(0)

        @pl.when(i == 0)
        def _():
            xbuf[0:8, :] = jnp.zeros((8, D_MODEL), F32)
            hcar[...] = jnp.zeros_like(hcar)

        xbuf[8:8 + tm, :] = x_ref[...]
        c = _conv_taps(xbuf, tm, cw_ref, cb_ref)
        xbuf[0:8, :] = xbuf[tm:tm + 8, :]
        r, ig, a, mult, _ = _lru_gates(c, ba_ref[...], bx_ref[...], lam_ref[...], wa_ref, wx_ref)
        row = i * tm + lax.broadcasted_iota(jnp.int32, (tm, 1), 0)
        abuf[...] = a
        ubuf[...] = jnp.where(row >= PAD_ROWS, mult * (ig * c), 0.0)

        sub = lax.broadcasted_iota(jnp.int32, (8, D_MODEL), 0)

        def block(b, carry):
            off = pl.multiple_of(b * 8, 8)
            av, uv = abuf[pl.ds(off, 8), :], ubuf[pl.ds(off, 8), :]
            for s in (1, 2, 4):
                us = jnp.where(sub >= s, pltpu.roll(uv, s, 0), 0.0)
                as_ = jnp.where(sub >= s, pltpu.roll(av, s, 0), 1.0)
                uv = uv + av * us
                av = av * as_
            hv = uv + av * carry
            hs_ref[pl.ds(off, 8), :] = hv
            return hv[7:8, :]

        hcar[...] = lax.fori_loop(0, tm // 8, block, hcar[...])
        gl, _ = _gelu_parts(gt_ref[...])
        zl_ref[...] = (gl * hs_ref[...]).astype(BF16)

    vec = pl.BlockSpec((1, D_MODEL), lambda i: (0, 0))
    mat = pl.BlockSpec((LRU_BLOCKS, LRU_BLOCK, LRU_BLOCK), lambda i: (0, 0, 0))
    return pl.pallas_call(
        body, name="lru_fwd", grid=(rows // tm,),
        in_specs=[_seg_spec(tm, 4), _seg_spec(tm, 5), pl.BlockSpec((4, D_MODEL), lambda i: (0, 0)),
                  vec, vec, vec, vec, mat, mat],
        out_specs=[pl.BlockSpec((tm, D_MODEL), lambda i: (i, 0)), pl.BlockSpec((tm, D_MODEL), lambda i: (i, 0))],
        out_shape=[jax.ShapeDtypeStruct((rows, D_MODEL), F32), jax.ShapeDtypeStruct((rows, D_MODEL), BF16)],
        scratch_shapes=[pltpu.VMEM((tm + 8, D_MODEL), F32), pltpu.VMEM((tm, D_MODEL), F32),
                        pltpu.VMEM((tm, D_MODEL), F32), pltpu.VMEM((1, D_MODEL), F32)],
        compiler_params=_cparams(("arbitrary",)),
    )(proj, proj, conv_w, conv_b, ba, bx, lam, wa_g, wx_g)


def _mix_fwd(zr, zl, proj, h0, wbr, wbl, wout):
    rows = h0.shape[0]
    tm = _tile(rows, 320)

    def body(zr_ref, zl_ref, ga_ref, gb_ref, h0_ref, wbr_ref, wbl_ref, wo_ref, h1_ref, yr_ref, yl_ref, mx_ref):
        yr = _dot(zr_ref[...], wbr_ref[...])
        yl = _dot(zl_ref[...], wbl_ref[...])
        mixed = (_sigmoid(ga_ref[...]) * yr + _sigmoid(gb_ref[...]) * yl).astype(BF16)
        yr_ref[...] = yr
        yl_ref[...] = yl
        mx_ref[...] = mixed
        h1_ref[...] = h0_ref[...] + _dot(mixed, wo_ref[...])

    row = pl.BlockSpec((tm, D_MODEL), lambda i: (i, 0))
    wsp = pl.BlockSpec((D_MODEL, D_MODEL), lambda i: (0, 0))
    return pl.pallas_call(
        body, name="mix_fwd", grid=(rows // tm,),
        in_specs=[row, row, _seg_spec(tm, 6), _seg_spec(tm, 7), row, wsp, wsp, wsp],
        out_specs=[row, row, row, row],
        out_shape=[jax.ShapeDtypeStruct((rows, D_MODEL), F32)] * 3 + [jax.ShapeDtypeStruct((rows, D_MODEL), BF16)],
        compiler_params=_cparams(("parallel",)),
    )(zr, zl, proj, proj, h0, wbr, wbl, wout)


def _ffn_fwd_loss(h1, norm_w, wfi_g, wfo_g, final_w, target):
    rows = h1.shape[0]
    tm = _tile(rows, 640)
    last = FFN_GROUPS - 1

    def body(h1_ref, nw_ref, wg_ref, wu_ref, wo_ref, fw_ref, t_ref,
             u2_ref, g_ref, up_ref, act_ref, dh2_ref, red_ref, u2_s, acc):
        i, d = pl.program_id(0), pl.program_id(1)

        @pl.when(jnp.logical_and(i == 0, d == 0))
        def _():
            red_ref[...] = jnp.zeros_like(red_ref)

        @pl.when(d == 0)
        def _():
            x = h1_ref[...]
            rs = lax.rsqrt(jnp.mean(x * x, axis=-1, keepdims=True) + NORM_EPS)
            u2 = (x * rs * nw_ref[...]).astype(BF16)
            u2_s[...] = u2
            u2_ref[...] = u2
            acc[...] = jnp.zeros_like(acc)

        g = _dot(u2_s[...], wg_ref[...])
        up = _dot(u2_s[...], wu_ref[...])
        act = (g * _sigmoid(g) * up).astype(BF16)
        g_ref[...] = g.astype(BF16)
        up_ref[...] = up.astype(BF16)
        act_ref[...] = act
        acc[...] += _dot(act, wo_ref[...])

        @pl.when(d == last)
        def _():
            h2 = h1_ref[...] + acc[...]
            rs = lax.rsqrt(jnp.mean(h2 * h2, axis=-1, keepdims=True) + NORM_EPS)
            nh = h2 * rs
            fw = fw_ref[...]
            row = i * tm + lax.broadcasted_iota(jnp.int32, (tm, 1), 0)
            diff = jnp.where(row >= CHUNK, nh * fw - t_ref[...], 0.0)
            dy = diff * (1.0 / D_MODEL)
            red_ref[0:1, :] += jnp.sum(diff * diff, axis=0, keepdims=True)
            red_ref[1:2, :] += jnp.sum(dy * nh, axis=0, keepdims=True)
            dn = dy * fw
            dh2_ref[...] = rs * (dn - nh * jnp.mean(dn * nh, axis=-1, keepdims=True))

    row = pl.BlockSpec((tm, D_MODEL), lambda i, d: (i, 0))
    vec = pl.BlockSpec((1, D_MODEL), lambda i, d: (0, 0))
    hid = pl.BlockSpec((tm, FFN_GROUP), lambda i, d: (i, d))
    hid_shape = jax.ShapeDtypeStruct((rows, FFN_GROUPS * FFN_GROUP), BF16)
    return pl.pallas_call(
        body, name="ffn_fwd_loss", grid=(rows // tm, FFN_GROUPS),
        in_specs=[row, vec,
                  pl.BlockSpec((None, D_MODEL, FFN_GROUP), lambda i, d: (d, 0, 0)),
                  pl.BlockSpec((None, D_MODEL, FFN_GROUP), lambda i, d: (d + FFN_GROUPS, 0, 0)),
                  pl.BlockSpec((None, FFN_GROUP, D_MODEL), lambda i, d: (d, 0, 0)),
                  vec, row],
        out_specs=[row, hid, hid, hid, row, pl.BlockSpec((8, D_MODEL), lambda i, d: (0, 0))],
        out_shape=[jax.ShapeDtypeStruct((rows, D_MODEL), BF16), hid_shape, hid_shape, hid_shape,
                   jax.ShapeDtypeStruct((rows, D_MODEL), F32), jax.ShapeDtypeStruct((8, D_MODEL), F32)],
        scratch_shapes=[pltpu.VMEM((tm, D_MODEL), BF16), pltpu.VMEM((tm, D_MODEL), F32)],
        compiler_params=_cparams(("arbitrary", "arbitrary")),
    )(h1, norm_w, wfi_g, wfi_g, wfo_g, final_w, target)


def _wgrad(a, b, ka, tn, out_dtype):
    rows = a.shape[0]
    na, nb = a.shape[1] // ka, b.shape[1] // tn
    tm = _tile(rows, 640)
    nm = rows // tm

    def body(a_ref, b_ref, o_ref, acc):
        m = pl.program_id(2)

        @pl.when(m == 0)
        def _():
            acc[...] = jnp.zeros_like(acc)

        acc[...] += _dot_tn(a_ref[...].astype(BF16), b_ref[...].astype(BF16))

        @pl.when(m == nm - 1)
        def _():
            o_ref[...] = acc[...].astype(out_dtype)

    return pl.pallas_call(
        body, name="wgrad", grid=(na, nb, nm),
        in_specs=[pl.BlockSpec((tm, ka), lambda p, q, m: (m, p)), pl.BlockSpec((tm, tn), lambda p, q, m: (m, q))],
        out_specs=pl.BlockSpec((None, None, ka, tn), lambda p, q, m: (p, q, 0, 0)),
        out_shape=jax.ShapeDtypeStruct((na, nb, ka, tn), out_dtype),
        scratch_shapes=[pltpu.VMEM((ka, tn), F32)],
        compiler_params=_cparams(("parallel", "parallel", "arbitrary")),
    )(a, b)


def _ffn_bwd(dh2, g, up, h1, norm_w, wfi_g, wfo_g):
    rows = h1.shape[0]
    tm = _tile(rows, 640)
    last = FFN_GROUPS - 1

    def body(dh2_ref, g_ref, up_ref, h1_ref, nw_ref, wg_ref, wu_ref, wo_ref, dg_ref, du_ref, dh1_ref, dw_ref,
             dh2_s, acc):
        i, d = pl.program_id(0), pl.program_id(1)

        @pl.when(jnp.logical_and(i == 0, d == 0))
        def _():
            dw_ref[...] = jnp.zeros_like(dw_ref)

        @pl.when(d == 0)
        def _():
            dh2_s[...] = dh2_ref[...].astype(BF16)
            acc[...] = jnp.zeros_like(acc)

        dact = _dot_nt(dh2_s[...], wo_ref[...])
        gv, uv = g_ref[...].astype(F32), up_ref[...].astype(F32)
        sg = _sigmoid(gv)
        dg = (dact * uv * (sg * (1.0 + gv * (1.0 - sg)))).astype(BF16)
        dup = (dact * (gv * sg)).astype(BF16)
        dg_ref[...] = dg
        du_ref[...] = dup
        acc[...] += _dot_nt(dg, wg_ref[...]) + _dot_nt(dup, wu_ref[...])

        @pl.when(d == last)
        def _():
            dx, dw = _rms_bwd(h1_ref[...], nw_ref[...], acc[...])
            dw_ref[0:1, :] += dw
            dh1_ref[...] = dh2_ref[...] + dx

    row = pl.BlockSpec((tm, D_MODEL), lambda i, d: (i, 0))
    vec = pl.BlockSpec((1, D_MODEL), lambda i, d: (0, 0))
    hid = pl.BlockSpec((tm, FFN_GROUP), lambda i, d: (i, d))
    return pl.pallas_call(
        body, name="ffn_bwd", grid=(rows // tm, FFN_GROUPS),
        in_specs=[row, hid, hid, row, vec,
                  pl.BlockSpec((None, D_MODEL, FFN_GROUP), lambda i, d: (d, 0, 0)),
                  pl.BlockSpec((None, D_MODEL, FFN_GROUP), lambda i, d: (d + FFN_GROUPS, 0, 0)),
                  pl.BlockSpec((None, FFN_GROUP, D_MODEL), lambda i, d: (d, 0, 0))],
        out_specs=[hid, hid, row, pl.BlockSpec((8, D_MODEL), lambda i, d: (0, 0))],
        out_shape=[jax.ShapeDtypeStruct((rows, FFN_GROUPS * FFN_GROUP), BF16)] * 2
        + [jax.ShapeDtypeStruct((rows, D_MODEL), F32), jax.ShapeDtypeStruct((8, D_MODEL), F32)],
        scratch_shapes=[pltpu.VMEM((tm, D_MODEL), BF16), pltpu.VMEM((tm, D_MODEL), F32)],
        compiler_params=_cparams(("arbitrary", "arbitrary")),
    )(dh2, g, up, h1, norm_w, wfi_g, wfi_g, wfo_g)


def _mix_bwd(dh1, yr, yl, proj, wbr, wbl, wout):
    rows = dh1.shape[0]
    tm = _tile(rows, 320)

    def body(dh1_ref, yr_ref, yl_ref, ga_ref, gb_ref, wbr_ref, wbl_ref, wo_ref,
             dyr_ref, dyl_ref, dga_ref, dgb_ref, dzr_ref, dzl_ref):
        dmix = _dot_nt(dh1_ref[...].astype(BF16), wo_ref[...])
        sa, sb = _sigmoid(ga_ref[...]), _sigmoid(gb_ref[...])
        dyr = (dmix * sa).astype(BF16)
        dyl = (dmix * sb).astype(BF16)
        dyr_ref[...] = dyr
        dyl_ref[...] = dyl
        dga_ref[...] = (dmix * yr_ref[...] * (sa * (1.0 - sa))).astype(BF16)
        dgb_ref[...] = (dmix * yl_ref[...] * (sb * (1.0 - sb))).astype(BF16)
        dzr_ref[...] = _dot_nt(dyr, wbr_ref[...])
        dzl_ref[...] = _dot_nt(dyl, wbl_ref[...])

    row = pl.BlockSpec((tm, D_MODEL), lambda i: (i, 0))
    wsp = pl.BlockSpec((D_MODEL, D_MODEL), lambda i: (0, 0))
    bshape = jax.ShapeDtypeStruct((rows, D_MODEL), BF16)
    fshape = jax.ShapeDtypeStruct((rows, D_MODEL), F32)
    return pl.pallas_call(
        body, name="mix_bwd", grid=(rows // tm,),
        in_specs=[row, row, row, _seg_spec(tm, 6), _seg_spec(tm, 7), wsp, wsp, wsp],
        out_specs=[row] * 6,
        out_shape=[bshape, bshape, bshape, bshape, fshape, fshape],
        compiler_params=_cparams(("parallel",)),
    )(dh1, yr, yl, proj, proj, wbr, wbl, wout)


def _lru_bwd(dzl, hs, proj, conv_w, conv_b, ba, bx, lam, wa_g, wx_g):
    rows = dzl.shape[0]
    tm = _tile(rows, 320)
    nt = rows // tm
    t8 = tm // 8

    def body(dzl_ref, hs_ref, hsp_ref, x_ref, xp_ref, gt_ref, cw_ref, cb_ref, ba_ref, bx_ref, lam_ref,
             wa_ref, wx_ref, dlin_ref, dlg_ref, dwa_ref, dwx_ref, sm_ref,
             xbuf, abuf, bbuf, dbuf, dcbuf, anext, dhcar):
        step = pl.program_id(0)
        i = nt - 1 - step

        @pl.when(step == 0)
        def _():
            dwa_ref[...] = jnp.zeros_like(dwa_ref)
            dwx_ref[...] = jnp.zeros_like(dwx_ref)
            sm_ref[...] = jnp.zeros_like(sm_ref)
            anext[...] = jnp.zeros_like(anext)
            dhcar[...] = jnp.zeros_like(dhcar)
            dcbuf[tm:tm + 8, :] = jnp.zeros((8, D_MODEL), F32)

        first = i == 0
        xbuf[0:8, :] = jnp.where(first, 0.0, xp_ref[...])
        xbuf[8:8 + tm, :] = x_ref[...]
        c = _conv_taps(xbuf, tm, cw_ref, cb_ref)
        lam_v = lam_ref[...]
        r, ig, a, mult, sp = _lru_gates(c, ba_ref[...], bx_ref[...], lam_v, wa_ref, wx_ref)
        hs_v = hs_ref[...]
        gl, dgl = _gelu_parts(gt_ref[...])
        dzl_v = dzl_ref[...]
        dlg_ref[...] = (dzl_v * hs_v * dgl).astype(BF16)
        dbuf[...] = dzl_v * gl
        abuf[0:tm, :] = a
        abuf[tm:tm + 8, :] = jnp.broadcast_to(anext[...], (8, D_MODEL))
        bbuf[...] = abuf[1:tm + 1, :]

        sub = lax.broadcasted_iota(jnp.int32, (8, D_MODEL), 0)

        def block(k, carry):
            off = pl.multiple_of((t8 - 1 - k) * 8, 8)
            av = bbuf[pl.ds(off, 8), :]
            uv = dbuf[pl.ds(off, 8), :]
            for s in (1, 2, 4):
                us = jnp.where(sub < 8 - s, pltpu.roll(uv, 8 - s, 0), 0.0)
                as_ = jnp.where(sub < 8 - s, pltpu.roll(av, 8 - s, 0), 1.0)
                uv = uv + av * us
                av = av * as_
            hv = uv + av * carry
            dbuf[pl.ds(off, 8), :] = hv
            return hv[0:1, :]

        dhcar[...] = lax.fori_loop(0, t8, block, dhcar[...])
        anext[...] = abuf[0:1, :]
        dh = dbuf[...]

        xbuf[0:8, :] = jnp.where(first, 0.0, hsp_ref[...])
        xbuf[8:8 + tm, :] = hs_v
        hprev = xbuf[7:7 + tm, :]
        row = i * tm + lax.broadcasted_iota(jnp.int32, (tm, 1), 0)
        duu = jnp.where(row >= PAD_ROWS, dh, 0.0)
        da = dh * hprev
        dmult = duu * ig * c
        di = duu * mult * c
        dc = duu * mult * ig
        dlog_a = da * a - dmult * (a * a) / mult
        dr = dlog_a * (-LRU_C * sp)
        dsp = jnp.sum(dlog_a * (-LRU_C * r), axis=0, keepdims=True)
        dpr = dr * r * (1.0 - r)
        dpi = di * ig * (1.0 - ig)
        dpr_b, dpi_b = dpr.astype(BF16), dpi.astype(BF16)
        dcs = []
        for g in range(LRU_BLOCKS):
            sl = slice(LRU_BLOCK * g, LRU_BLOCK * (g + 1))
            cg = c[:, sl].astype(BF16)
            dwa_ref[g] += _dot_tn(cg, dpr_b[:, sl])
            dwx_ref[g] += _dot_tn(cg, dpi_b[:, sl])
            dcs.append(_dot_nt(dpr_b[:, sl], wa_ref[g]) + _dot_nt(dpi_b[:, sl], wx_ref[g]))
        dc = dc + jnp.concatenate(dcs, axis=1)

        dcbuf[0:tm, :] = dc
        xbuf[8:8 + tm, :] = x_ref[...]
        xbuf[0:8, :] = jnp.where(first, 0.0, xp_ref[...])
        dlin = cw_ref[3:4, :] * dc
        sm_ref[3:4, :] += jnp.sum(dc * xbuf[8:8 + tm, :], axis=0, keepdims=True)
        for back in (1, 2, 3):
            dlin = dlin + cw_ref[3 - back:4 - back, :] * dcbuf[back:back + tm, :]
            sm_ref[3 - back:4 - back, :] += jnp.sum(dc * xbuf[8 - back:8 - back + tm, :], axis=0, keepdims=True)
        dlin_ref[...] = dlin.astype(BF16)
        dcbuf[tm:tm + 8, :] = dcbuf[0:8, :]
        sm_ref[4:5, :] += jnp.sum(dc, axis=0, keepdims=True)
        sm_ref[5:6, :] += jnp.sum(dpr, axis=0, keepdims=True)
        sm_ref[6:7, :] += jnp.sum(dpi, axis=0, keepdims=True)
        sm_ref[7:8, :] += dsp * (-_sigmoid(-lam_v))

    rowb = pl.BlockSpec((tm, D_MODEL), lambda s: (nt - 1 - s, 0))
    prev8 = pl.BlockSpec((8, D_MODEL), lambda s: (jnp.maximum((nt - 1 - s) * t8 - 1, 0), 0))
    seg = lambda k: pl.BlockSpec((tm, D_MODEL), lambda s, k=k: (nt - 1 - s, k))
    prev8_seg4 = pl.BlockSpec((8, D_MODEL), lambda s: (jnp.maximum((nt - 1 - s) * t8 - 1, 0), 4))
    vec = pl.BlockSpec((1, D_MODEL), lambda s: (0, 0))
    mat = pl.BlockSpec((LRU_BLOCKS, LRU_BLOCK, LRU_BLOCK), lambda s: (0, 0, 0))
    bshape = jax.ShapeDtypeStruct((rows, D_MODEL), BF16)
    mshape = jax.ShapeDtypeStruct((LRU_BLOCKS, LRU_BLOCK, LRU_BLOCK), F32)
    return pl.pallas_call(
        body, name="lru_bwd", grid=(nt,),
        in_specs=[rowb, rowb, prev8, seg(4), prev8_seg4, seg(5), pl.BlockSpec((4, D_MODEL), lambda s: (0, 0)),
                  vec, vec, vec, vec, mat, mat],
        out_specs=[rowb, rowb, mat, mat, pl.BlockSpec((8, D_MODEL), lambda s: (0, 0))],
        out_shape=[bshape, bshape, mshape, mshape, jax.ShapeDtypeStruct((8, D_MODEL), F32)],
        scratch_shapes=[pltpu.VMEM((tm + 8, D_MODEL), F32), pltpu.VMEM((tm + 8, D_MODEL), F32),
                        pltpu.VMEM((tm, D_MODEL), F32), pltpu.VMEM((tm, D_MODEL), F32),
                        pltpu.VMEM((tm + 8, D_MODEL), F32),
                        pltpu.VMEM((1, D_MODEL), F32), pltpu.VMEM((1, D_MODEL), F32)],
        compiler_params=_cparams(("arbitrary",)),
    )(dzl, hs, hs, proj, proj, proj, conv_w, conv_b, ba, bx, lam, wa_g, wx_g)


def _retention_bwd(dzr, o, proj, states, cos2, sin2, dec):
    rows = dzr.shape[0]
    n_chunks = rows // CHUNK

    def body(dzr_ref, o_ref, q_ref, k_ref, v_ref, g_ref, st_ref, c_ref, s_ref, dec_ref,
             dq_ref, dk_ref, dv_ref, dgr_ref, dstate):
        @pl.when(pl.program_id(0) == 0)
        def _():
            dstate[...] = jnp.zeros_like(dstate)
        cos_t, sin_t = c_ref[...], s_ref[...]
        for h in range(HEADS):
            sl = slice(HEAD_DIM * h, HEAD_DIM * (h + 1))
            o = o_ref[:, sl]
            g = g_ref[:, sl]
            dzr_v = dzr_ref[:, sl]
            sg = _sigmoid(g)
            r = lax.rsqrt(jnp.mean(o * o, axis=-1, keepdims=True) + NORM_EPS)
            on = o * r
            dgr_ref[:, sl] = (dzr_v * on * (sg * (1.0 + g * (1.0 - sg)))).astype(BF16)
            don = dzr_v * (g * sg)
            do = r * (don - on * jnp.mean(don * on, axis=-1, keepdims=True))
            dob = do.astype(BF16)

            qh = _rot(q_ref[:, sl], cos_t, sin_t)
            kh = _rot(k_ref[:, sl], cos_t, sin_t) * QK_SCALE
            qb, kb, vb = qh.astype(BF16), kh.astype(BF16), v_ref[:, sl].astype(BF16)
            intra, qd, kd, cd = dec_ref[0, h], dec_ref[1, h], dec_ref[2, h], dec_ref[3, h]
            s = (_dot_nt(qb, kb) * intra).astype(BF16)
            ds = (_dot_nt(dob, vb) * intra).astype(BF16)
            st_b = st_ref[h].astype(BF16)
            dst = dstate[h]
            dst_b = dst.astype(BF16)
            dv = _dot_tn(s, dob) + _dot((kh * kd).astype(BF16), dst_b)
            dq = _dot(ds, kb) + _dot_nt(dob, st_b) * qd
            dk = _dot_tn(ds, qb) + _dot_nt(vb, dst_b) * kd
            dstate[h] = dst * cd + _dot_tn((qh * qd).astype(BF16), dob)
            dv_ref[:, sl] = dv.astype(BF16)
            dq_ref[:, sl] = _rot_t(dq, cos_t, sin_t).astype(BF16)
            dk_ref[:, sl] = (_rot_t(dk, cos_t, sin_t) * QK_SCALE).astype(BF16)

    rev = lambda s: n_chunks - 1 - s
    rowb = pl.BlockSpec((CHUNK, D_MODEL), lambda s: (rev(s), 0))
    seg = lambda k: pl.BlockSpec((CHUNK, D_MODEL), lambda s, k=k: (rev(s), k))
    tab = pl.BlockSpec((CHUNK, HEAD_DIM), lambda s: (rev(s), 0))
    bshape = jax.ShapeDtypeStruct((rows, D_MODEL), BF16)
    return pl.pallas_call(
        body, name="retention_bwd", grid=(n_chunks,),
        in_specs=[rowb, rowb, seg(0), seg(1), seg(2), seg(3),
                  pl.BlockSpec((None, HEADS, HEAD_DIM, HEAD_DIM), lambda s: (rev(s), 0, 0, 0)), tab, tab,
                  pl.BlockSpec((4, HEADS, CHUNK, CHUNK), lambda s: (0, 0, 0, 0))],
        out_specs=[rowb] * 4,
        out_shape=[bshape] * 4,
        scratch_shapes=[pltpu.VMEM((HEADS, HEAD_DIM, HEAD_DIM), F32)],
        compiler_params=_cparams(("arbitrary",)),
    )(dzr, o, proj, proj, proj, proj, states, cos2, sin2, dec)


def _in_proj_bwd(dsegs, win_g, h0, norm_w, dh1):
    rows = h0.shape[0]
    tm = _tile(rows, 320)

    def body(*refs):
        seg_refs = refs[:N_DEV]
        w_ref, h0_ref, nw_ref, dh1_ref, dh0_ref, dw_ref, acc = refs[N_DEV:]
        i, j = pl.program_id(0), pl.program_id(1)

        @pl.when(jnp.logical_and(i == 0, j == 0))
        def _():
            dw_ref[...] = jnp.zeros_like(dw_ref)

        @pl.when(j == 0)
        def _():
            acc[...] = jnp.zeros_like(acc)

        for k in range(N_DEV):
            @pl.when(j == k)
            def _(k=k):
                acc[...] += _dot_nt(seg_refs[k][...], w_ref[...])

        @pl.when(j == N_DEV - 1)
        def _():
            dx, dw = _rms_bwd(h0_ref[...], nw_ref[...], acc[...])
            dw_ref[0:1, :] += dw
            dh0_ref[...] = dh1_ref[...] + dx

    row = pl.BlockSpec((tm, D_MODEL), lambda i, j: (i, 0))
    vec = pl.BlockSpec((1, D_MODEL), lambda i, j: (0, 0))
    return pl.pallas_call(
        body, name="in_proj_bwd", grid=(rows // tm, N_DEV),
        in_specs=[row] * N_DEV + [pl.BlockSpec((None, D_MODEL, D_MODEL), lambda i, j: (j, 0, 0)), row, vec, row],
        out_specs=[row, pl.BlockSpec((8, D_MODEL), lambda i, j: (0, 0))],
        out_shape=[jax.ShapeDtypeStruct((rows, D_MODEL), F32), jax.ShapeDtypeStruct((8, D_MODEL), F32)],
        scratch_shapes=[pltpu.VMEM((tm, D_MODEL), F32)],
        compiler_params=_cparams(("arbitrary", "arbitrary")),
    )(*dsegs, win_g, h0, norm_w, dh1)


def _adamw(g_slots, w, m, v):
    slots, rows, cols = g_slots.shape
    tr = rows
    for cand in (256, 128, 64, 32, 16, 8):
        if rows % cand == 0 and rows > cand:
            tr = cand
            break

    def body(g_ref, w_ref, m_ref, v_ref, go_ref, d_ref, mo_ref, vo_ref):
        g = g_ref[0].astype(F32)
        for s in range(1, slots):
            g = g + g_ref[s].astype(F32)
        m2 = ADAM_B1 * m_ref[...] + (1.0 - ADAM_B1) * g
        v2 = ADAM_B2 * v_ref[...] + (1.0 - ADAM_B2) * (g * g)
        m_hat = m2 / (1.0 - ADAM_B1 ** ADAM_STEP)
        v_hat = v2 / (1.0 - ADAM_B2 ** ADAM_STEP)
        go_ref[...] = g
        d_ref[...] = -ADAM_LR * (m_hat / (jnp.sqrt(v_hat) + ADAM_EPS) + ADAM_WD * w_ref[...])
        mo_ref[...] = m2
        vo_ref[...] = v2

    blk = pl.BlockSpec((tr, cols), lambda i: (i, 0))
    shape = jax.ShapeDtypeStruct((rows, cols), F32)
    return pl.pallas_call(
        body, name="adamw", grid=(rows // tr,),
        in_specs=[pl.BlockSpec((slots, tr, cols), lambda i: (0, i, 0)), blk, blk, blk],
        out_specs=[blk] * 4, out_shape=[shape] * 4,
        compiler_params=_cparams(("parallel",)),
    )(g_slots, w, m, v)


def _sum_slots(packs):
    slots, rows, cols = packs.shape

    def body(p_ref, o_ref):
        acc = p_ref[0]
        for s in range(1, slots):
            acc = acc + p_ref[s]
        o_ref[...] = acc

    return pl.pallas_call(
        body, name="sum_slots", out_shape=jax.ShapeDtypeStruct((rows, cols), F32),
        compiler_params=pltpu.CompilerParams(vmem_limit_bytes=VMEM_LIMIT),
    )(packs)


def _my_index():
    return 4 * lax.axis_index("x") + 2 * lax.axis_index("y") + lax.axis_index("c")


def _peer(k):
    x, y, c = lax.axis_index("x"), lax.axis_index("y"), lax.axis_index("c")
    px = 1 - x if k & 4 else x
    py = 1 - y if k & 2 else y
    pc = 1 - c if k & 1 else c
    return (px, py, pc), 4 * px + 2 * py + pc


def _gather_weights(shards):
    n_arr = 9
    out_shapes = [
        jax.ShapeDtypeStruct((N_DEV, D_MODEL, D_MODEL), BF16),
        jax.ShapeDtypeStruct((N_DEV, D_MODEL // N_DEV, D_MODEL), BF16),
        jax.ShapeDtypeStruct((N_DEV, D_MODEL // N_DEV, D_MODEL), BF16),
        jax.ShapeDtypeStruct((N_DEV, D_MODEL // N_DEV, D_MODEL), BF16),
        jax.ShapeDtypeStruct((N_DEV, D_MODEL, FFN_GROUP), BF16),
        jax.ShapeDtypeStruct((FFN_GROUPS, FFN_GROUP, D_MODEL), BF16),
        jax.ShapeDtypeStruct((LRU_BLOCKS, LRU_BLOCK, LRU_BLOCK), BF16),
        jax.ShapeDtypeStruct((LRU_BLOCKS, LRU_BLOCK, LRU_BLOCK), BF16),
        jax.ShapeDtypeStruct((N_DEV, 24, HEAD_DIM), F32),
    ]
    lru_rows = LRU_BLOCK // N_DEV
    pad_rows = FFN_GROUP - 2 * FFN_OUT_SHARD

    def body(*refs):
        ins = refs[:n_arr]
        outs = refs[n_arr:2 * n_arr]
        send_sems, recv_sems, loc_sems, zbuf = refs[2 * n_arr:]
        me = _my_index()

        def slots(d):
            half = pl.multiple_of((d % 2) * FFN_OUT_SHARD, 16)
            lrow = pl.multiple_of(d * lru_rows, 16)
            return [outs[0].at[d], outs[1].at[d], outs[2].at[d], outs[3].at[d], outs[4].at[d],
                    outs[5].at[d // 2, pl.ds(half, FFN_OUT_SHARD), :],
                    outs[6].at[:, pl.ds(lrow, lru_rows), :], outs[7].at[:, pl.ds(lrow, lru_rows), :],
                    outs[8].at[d]]

        mine = slots(me)
        remote = []
        for k in range(1, N_DEV):
            pos, _ = _peer(k)
            for a in range(n_arr):
                cp = pltpu.make_async_remote_copy(
                    src_ref=ins[a], dst_ref=mine[a], send_sem=send_sems.at[(k - 1) * n_arr + a],
                    recv_sem=recv_sems.at[(k - 1) * n_arr + a], device_id=pos, device_id_type=MESH_ID)
                cp.start()
                remote.append(cp)
        local = [pltpu.make_async_copy(ins[a], mine[a], loc_sems.at[a]) for a in range(n_arr)]
        zbuf[...] = jnp.zeros_like(zbuf)
        for gidx in range(FFN_GROUPS):
            local.append(pltpu.make_async_copy(
                zbuf, outs[5].at[gidx, pl.ds(2 * FFN_OUT_SHARD, pad_rows), :], loc_sems.at[n_arr + gidx]))
        for cp in local:
            cp.start()
        for k in range(1, N_DEV):
            _, p = _peer(k)
            theirs = slots(p)
            for a in range(n_arr):
                pltpu.make_async_remote_copy(
                    src_ref=ins[a], dst_ref=theirs[a], send_sem=send_sems.at[(k - 1) * n_arr + a],
                    recv_sem=recv_sems.at[(k - 1) * n_arr + a], device_id=_peer(k)[0],
                    device_id_type=MESH_ID).wait_recv()
        for cp in remote:
            cp.wait_send()
        for cp in local:
            cp.wait()

    n_rem = (N_DEV - 1) * n_arr
    return pl.pallas_call(
        body, name="gather_weights", in_specs=[ANY] * n_arr, out_specs=[ANY] * n_arr, out_shape=out_shapes,
        scratch_shapes=[pltpu.SemaphoreType.DMA((n_rem,)), pltpu.SemaphoreType.DMA((n_rem,)),
                        pltpu.SemaphoreType.DMA((n_arr + FFN_GROUPS,)), pltpu.VMEM((pad_rows, D_MODEL), BF16)],
        compiler_params=pltpu.CompilerParams(has_side_effects=True),
    )(*shards)


def _scatter_grads(grads):
    n_arr = 9
    lru_rows = LRU_BLOCK // N_DEV
    out_shapes = [
        jax.ShapeDtypeStruct((N_DEV, D_MODEL, D_MODEL), BF16),
        jax.ShapeDtypeStruct((N_DEV, D_MODEL // N_DEV, D_MODEL), BF16),
        jax.ShapeDtypeStruct((N_DEV, D_MODEL // N_DEV, D_MODEL), BF16),
        jax.ShapeDtypeStruct((N_DEV, D_MODEL // N_DEV, D_MODEL), BF16),
        jax.ShapeDtypeStruct((N_DEV, D_MODEL, FFN_GROUP), BF16),
        jax.ShapeDtypeStruct((N_DEV, FFN_OUT_SHARD, D_MODEL), BF16),
        jax.ShapeDtypeStruct((N_DEV, LRU_BLOCKS, lru_rows, LRU_BLOCK), F32),
        jax.ShapeDtypeStruct((N_DEV, LRU_BLOCKS, lru_rows, LRU_BLOCK), F32),
        jax.ShapeDtypeStruct((N_DEV, 32, D_MODEL), F32),
    ]

    def body(*refs):
        ins = refs[:n_arr]
        outs = refs[n_arr:2 * n_arr]
        send_sems, recv_sems, loc_sems = refs[2 * n_arr:]
        me = _my_index()

        def parts(d):
            half = pl.multiple_of((d % 2) * FFN_OUT_SHARD, 16)
            lrow = pl.multiple_of(d * lru_rows, 8)
            return [ins[0].at[d], ins[1].at[d], ins[2].at[d], ins[3].at[d], ins[4].at[d],
                    ins[5].at[d // 2, pl.ds(half, FFN_OUT_SHARD), :],
                    ins[6].at[:, pl.ds(lrow, lru_rows), :], ins[7].at[:, pl.ds(lrow, lru_rows), :], ins[8]]

        def landing(s):
            return [outs[a].at[s] for a in range(n_arr)]

        land_me = landing(me)
        remote = []
        for k in range(1, N_DEV):
            pos, p = _peer(k)
            src = parts(p)
            for a in range(n_arr):
                cp = pltpu.make_async_remote_copy(
                    src_ref=src[a], dst_ref=land_me[a], send_sem=send_sems.at[(k - 1) * n_arr + a],
                    recv_sem=recv_sems.at[(k - 1) * n_arr + a], device_id=pos, device_id_type=MESH_ID)
                cp.start()
                remote.append(cp)
        own = parts(me)
        local = [pltpu.make_async_copy(own[a], land_me[a], loc_sems.at[a]) for a in range(n_arr)]
        for cp in local:
            cp.start()
        for k in range(1, N_DEV):
            pos, p = _peer(k)
            land_p = landing(p)
            for a in range(n_arr):
                pltpu.make_async_remote_copy(
                    src_ref=own[a], dst_ref=land_p[a], send_sem=send_sems.at[(k - 1) * n_arr + a],
                    recv_sem=recv_sems.at[(k - 1) * n_arr + a], device_id=pos,
                    device_id_type=MESH_ID).wait_recv()
        for cp in remote:
            cp.wait_send()
        for cp in local:
            cp.wait()

    n_rem = (N_DEV - 1) * n_arr
    return pl.pallas_call(
        body, name="scatter_grads", in_specs=[ANY] * n_arr, out_specs=[ANY] * n_arr, out_shape=out_shapes,
        scratch_shapes=[pltpu.SemaphoreType.DMA((n_rem,)), pltpu.SemaphoreType.DMA((n_rem,)),
                        pltpu.SemaphoreType.DMA((n_arr,))],
        compiler_params=pltpu.CompilerParams(has_side_effects=True),
    )(*grads)


def _local_step(x, target, meta_full, conv_w_full, vecs, gathered):
    mix_norm_w, conv_b, lru_ba, lru_bx, lru_lam, ffn_norm_w, final_norm_w = vecs
    win_g, wbr, wbl, wout, wfi_g, wfo_g, wa_g, wx_g = gathered
    seq = x.shape[0]
    rows = seq + CHUNK
    h0 = jnp.concatenate([jnp.zeros((PAD_ROWS, D_MODEL), F32), meta_full, x], axis=0)
    tgt = jnp.concatenate([jnp.zeros((CHUNK, D_MODEL), F32), target], axis=0)
    cos2, sin2 = _rope_tables(rows)
    dec = _retention_consts()

    proj, u = _in_proj(h0, mix_norm_w, win_g)
    o, zr, states = _retention_fwd(proj, cos2, sin2, dec)
    hs, zl = _lru_fwd(proj, conv_w_full, conv_b, lru_ba, lru_bx, lru_lam, wa_g, wx_g)
    h1, yr, yl, mixed = _mix_fwd(zr, zl, proj, h0, wbr, wbl, wout)
    u2, g, up, act, dh2, red = _ffn_fwd_loss(h1, ffn_norm_w, wfi_g, wfo_g, final_norm_w, tgt)

    d_wfo = _wgrad(act, dh2, FFN_GROUP, D_MODEL, BF16)[:, 0]
    dg, dup, dh1, dw_ffn_norm = _ffn_bwd(dh2, g, up, h1, ffn_norm_w, wfi_g, wfo_g)
    d_wfi = jnp.concatenate([_wgrad(u2, dg, D_MODEL, FFN_GROUP, BF16)[0],
                             _wgrad(u2, dup, D_MODEL, FFN_GROUP, BF16)[0]], axis=0)
    d_wout = _wgrad(mixed, dh1, D_MODEL, D_MODEL, BF16)[0, 0]
    dyr, dyl, dga, dgb, dzr, dzl = _mix_bwd(dh1, yr, yl, proj, wbr, wbl, wout)
    d_wbr = _wgrad(zr, dyr, D_MODEL, D_MODEL, BF16)[0, 0]
    d_wbl = _wgrad(zl, dyl, D_MODEL, D_MODEL, BF16)[0, 0]
    dlin, dlg, d_wa, d_wx, lru_small = _lru_bwd(dzl, hs, proj, conv_w_full, conv_b, lru_ba, lru_bx, lru_lam,
                                                wa_g, wx_g)
    dq, dk, dv, dgr = _retention_bwd(dzr, o, proj, states, cos2, sin2, dec)
    dsegs = [dq, dk, dv, dgr, dlin, dlg, dga, dgb]
    d_win = jnp.stack([_wgrad(u, ds, D_MODEL, D_MODEL, BF16)[0, 0] for ds in dsegs])
    dh0, dw_mix_norm = _in_proj_bwd(dsegs, win_g, h0, mix_norm_w, dh1)

    pack = jnp.concatenate([
        dw_mix_norm[0:1], lru_small[4:8], dw_ffn_norm[0:1], red[1:2],
        dh0[PAD_ROWS:CHUNK],
        lru_small[0:4],
        red[0:1],
        jnp.zeros((4, D_MODEL), F32)], axis=0)
    big = [d_win, d_wbr.reshape(N_DEV, -1, D_MODEL), d_wbl.reshape(N_DEV, -1, D_MODEL),
           d_wout.reshape(N_DEV, -1, D_MODEL), d_wfi, d_wfo, d_wa, d_wx]
    return dh0[CHUNK:], big, pack


def kernel(x, meta_tokens, mix_norm_w, w_in, conv_w, conv_b, lru_wa, lru_ba, lru_wx, lru_bx, lru_lambda, w_branch_ret, w_branch_lru, w_out, ffn_norm_w, w_ffn_in, w_ffn_out, final_norm_w, loss_target, m_meta_tokens, m_mix_norm_w, m_w_in, m_conv_w, m_conv_b, m_lru_wa, m_lru_ba, m_lru_wx, m_lru_bx, m_lru_lambda, m_w_branch_ret, m_w_branch_lru, m_w_out, m_ffn_norm_w, m_w_ffn_in, m_w_ffn_out, m_final_norm_w, v_meta_tokens, v_mix_norm_w, v_w_in, v_conv_w, v_conv_b, v_lru_wa, v_lru_ba, v_lru_wx, v_lru_bx, v_lru_lambda, v_w_branch_ret, v_w_branch_lru, v_w_out, v_ffn_norm_w, v_w_ffn_in, v_w_ffn_out, v_final_norm_w):
    me = _my_index()
    lru_rows = LRU_BLOCK // N_DEV

    small = jnp.concatenate([meta_tokens, conv_w[0], jnp.zeros((4, HEAD_DIM), F32)], axis=0)
    shards = [w_in[0].astype(BF16), w_branch_ret[0].astype(BF16), w_branch_lru[0].astype(BF16),
              w_out[0].astype(BF16), jnp.pad(w_ffn_in[0].astype(BF16), ((0, 0), (0, FFN_GROUP - FFN_SHARD))),
              w_ffn_out[0].astype(BF16), lru_wa[0].astype(BF16), lru_wx[0].astype(BF16), small]
    win_g, wbr_g, wbl_g, wout_g, wfi_g, wfo_g, wa_g, wx_g, small_g = _gather_weights(shards)
    meta_full = small_g[:, :N_META].transpose(1, 0, 2).reshape(N_META, D_MODEL)
    conv_w_full = small_g[:, N_META:N_META + 4].transpose(1, 0, 2).reshape(4, D_MODEL)
    gathered = (win_g, wbr_g.reshape(D_MODEL, D_MODEL), wbl_g.reshape(D_MODEL, D_MODEL),
                wout_g.reshape(D_MODEL, D_MODEL), wfi_g, wfo_g, wa_g, wx_g)
    fw = final_norm_w.reshape(1, D_MODEL)
    vecs = (mix_norm_w, conv_b, lru_ba, lru_bx, lru_lambda, ffn_norm_w, fw)

    grad_x, big, pack = _local_step(x[0], loss_target[0], meta_full, conv_w_full, vecs, gathered)

    r_in, r_br, r_bl, r_out, r_fi, r_fo, r_wa, r_wx, packs = _scatter_grads(big + [pack])
    small_sum = _sum_slots(packs)
    loss = (0.5 / D_MODEL) * jnp.sum(small_sum[27])

    def big_update(slots, w, m, v):
        shape = w.shape
        w2, m2, v2 = (t.reshape(slots.shape[1:]) for t in (w, m, v))
        return [t.reshape(shape) for t in _adamw(slots, w2, m2, v2)]

    res = {}
    res["w_in"] = big_update(r_in, w_in, m_w_in, v_w_in)
    res["w_branch_ret"] = big_update(r_br, w_branch_ret, m_w_branch_ret, v_w_branch_ret)
    res["w_branch_lru"] = big_update(r_bl, w_branch_lru, m_w_branch_lru, v_w_branch_lru)
    res["w_out"] = big_update(r_out, w_out, m_w_out, v_w_out)
    res["w_ffn_in"] = big_update(r_fi[:, :, :FFN_SHARD], w_ffn_in, m_w_ffn_in, v_w_ffn_in)
    res["w_ffn_out"] = big_update(r_fo, w_ffn_out, m_w_ffn_out, v_w_ffn_out)
    res["lru_wa"] = big_update(r_wa.reshape(N_DEV, LRU_BLOCKS * lru_rows, LRU_BLOCK), lru_wa, m_lru_wa, v_lru_wa)
    res["lru_wx"] = big_update(r_wx.reshape(N_DEV, LRU_BLOCKS * lru_rows, LRU_BLOCK), lru_wx, m_lru_wx, v_lru_wx)

    col = me * HEAD_DIM
    g_meta = lax.dynamic_slice(small_sum, (7, col), (N_META, HEAD_DIM))
    g_conv = lax.dynamic_slice(small_sum, (23, col), (4, HEAD_DIM))
    small_names = ["mix_norm_w", "conv_b", "lru_ba", "lru_bx", "lru_lambda", "ffn_norm_w", "final_norm_w"]
    small_w = [mix_norm_w, conv_b, lru_ba, lru_bx, lru_lambda, ffn_norm_w, fw]
    small_m = [m_mix_norm_w, m_conv_b, m_lru_ba, m_lru_bx, m_lru_lambda, m_ffn_norm_w, m_final_norm_w.reshape(1, -1)]
    small_v = [v_mix_norm_w, v_conv_b, v_lru_ba, v_lru_bx, v_lru_lambda, v_ffn_norm_w, v_final_norm_w.reshape(1, -1)]
    tail = jnp.zeros((4, HEAD_DIM), F32)

    def pack_small(vec_list, meta_t, conv_t):
        return jnp.concatenate([t.reshape(8, HEAD_DIM) for t in vec_list] + [meta_t, conv_t[0], tail], axis=0)

    g_small = jnp.concatenate([small_sum[0:7].reshape(56, HEAD_DIM), g_meta, g_conv, tail], axis=0)
    outs_small = _adamw(g_small[None], pack_small(small_w, meta_tokens, conv_w),
                        pack_small(small_m, m_meta_tokens, m_conv_w), pack_small(small_v, v_meta_tokens, v_conv_w))
    for idx, name in enumerate(small_names):
        shape = final_norm_w.shape if name == "final_norm_w" else (1, D_MODEL)
        res[name] = [t[8 * idx:8 * idx + 8].reshape(shape) for t in outs_small]
    res["meta_tokens"] = [t[56:72] for t in outs_small]
    res["conv_w"] = [t[72:76].reshape(1, 4, HEAD_DIM) for t in outs_small]

    order = ["meta_tokens", "mix_norm_w", "w_in", "conv_w", "conv_b", "lru_wa", "lru_ba", "lru_wx", "lru_bx",
             "lru_lambda", "w_branch_ret", "w_branch_lru", "w_out", "ffn_norm_w", "w_ffn_in", "w_ffn_out",
             "final_norm_w"]
    out = [loss, grad_x[None]]
    for kind in range(4):
        out += [res[name][kind] for name in order]
    return tuple(out)
```

```python
import functools

import numpy as np
import jax
import jax.numpy as jnp
from jax import lax
from jax.experimental import pallas as pl
from jax.experimental.pallas import tpu as pltpu

F32 = jnp.float32
BF16 = jnp.bfloat16

D_MODEL = 1024
N_META = 16
CHUNK = 128
PAD_ROWS = CHUNK - N_META
HEADS = 8
HEAD_DIM = 128
ROPE_BASE = 10000.0
QK_SCALE = HEAD_DIM ** -0.5
LRU_BLOCKS = 4
LRU_BLOCK = 256
LRU_C = 8.0
FFN_HIDDEN = 2816
N_DEV = 8
FFN_SHARD = 2 * FFN_HIDDEN // N_DEV
FFN_GROUP = 768
FFN_GROUPS = 4
FFN_OUT_SHARD = FFN_HIDDEN // N_DEV
NORM_EPS = 1e-6

ADAM_LR = 0.001
ADAM_B1 = 0.9
ADAM_B2 = 0.999
ADAM_EPS = 1e-08
ADAM_WD = 0.01
ADAM_STEP = 10

VMEM_LIMIT = 56 * 1024 * 1024
MESH_ID = pl.DeviceIdType.MESH
ANY = pl.BlockSpec(memory_space=pl.ANY)


def _cparams(sem):
    return pltpu.CompilerParams(dimension_semantics=sem, vmem_limit_bytes=VMEM_LIMIT)


def _tile(rows, cap):
    t = cap - cap % 64
    while rows % t:
        t -= 64
    return t


def _dot(a, b):
    return jnp.dot(a, b, preferred_element_type=F32)


def _dot_nt(a, b):
    return lax.dot_general(a, b, (((1,), (1,)), ((), ())), preferred_element_type=F32)


def _dot_tn(a, b):
    return lax.dot_general(a, b, (((0,), (0,)), ((), ())), preferred_element_type=F32)


def _sigmoid(x):
    return 1.0 / (1.0 + jnp.exp(-x))


def _gelu_parts(x):
    k = 0.7978845608028654
    inner = k * (x + 0.044715 * x * x * x)
    t = jnp.tanh(inner)
    g = 0.5 * x * (1.0 + t)
    dg = 0.5 * (1.0 + t) + 0.5 * x * (1.0 - t * t) * k * (1.0 + 3.0 * 0.044715 * x * x)
    return g, dg


def _rot(x, cos2, sin2):
    return x * cos2 + pltpu.roll(x, HEAD_DIM // 2, 1) * sin2


def _rot_t(dx, cos2, sin2):
    return dx * cos2 - pltpu.roll(dx, HEAD_DIM // 2, 1) * sin2


def _rms_bwd(x, w, dy):
    rs = lax.rsqrt(jnp.mean(x * x, axis=-1, keepdims=True) + NORM_EPS)
    nh = x * rs
    dw = jnp.sum(dy * nh, axis=0, keepdims=True)
    dn = dy * w
    dx = rs * (dn - nh * jnp.mean(dn * nh, axis=-1, keepdims=True))
    return dx, dw


def _retention_consts():
    h = jnp.arange(HEADS, dtype=F32)
    log_g = jnp.log(1.0 - 2.0 ** (-5.0 - h))
    idx = jnp.arange(CHUNK, dtype=F32)
    diff = idx[:, None] - idx[None, :]
    intra = jnp.where(diff[None] >= 0, jnp.exp(jnp.maximum(diff, 0.0)[None] * log_g[:, None, None]), 0.0)
    q_decay = jnp.exp((idx + 1.0)[:, None] * log_g[None, :])
    k_decay = jnp.exp((CHUNK - 1.0 - idx)[:, None] * log_g[None, :])
    chunk_decay = jnp.exp(CHUNK * log_g)
    shape = (HEADS, CHUNK, CHUNK)
    qd = jnp.broadcast_to(q_decay.T[:, :, None], shape)
    kd = jnp.broadcast_to(k_decay.T[:, :, None], shape)
    cd = jnp.broadcast_to(chunk_decay[:, None, None], shape)
    return jnp.stack([intra, qd, kd, cd])


def _rope_tables(rows):
    pos = jnp.maximum(jnp.arange(rows) - PAD_ROWS, 0).astype(F32)
    inv_freq = ROPE_BASE ** (-jnp.arange(0, HEAD_DIM, 2, dtype=F32) / HEAD_DIM)
    ang = pos[:, None] * inv_freq[None, :]
    cos, sin = jnp.cos(ang), jnp.sin(ang)
    return jnp.concatenate([cos, cos], axis=1), jnp.concatenate([-sin, sin], axis=1)


def _my_index():
    return 4 * lax.axis_index("x") + 2 * lax.axis_index("y") + lax.axis_index("c")


def _peer(k):
    x, y, c = lax.axis_index("x"), lax.axis_index("y"), lax.axis_index("c")
    px = 1 - x if k & 4 else x
    py = 1 - y if k & 2 else y
    pc = 1 - c if k & 1 else c
    return (px, py, pc), 4 * px + 2 * py + pc


def _push_sems(n_arr):
    n_rem = (N_DEV - 1) * n_arr
    return [pltpu.SemaphoreType.DMA((n_rem,)), pltpu.SemaphoreType.DMA((n_rem,)), pltpu.SemaphoreType.DMA((n_arr,))]


class _Push:
    def __init__(self, send_part, land_slot, sems, n_arr):
        self.send_part, self.land_slot, self.n_arr = send_part, land_slot, n_arr
        self.send_sems, self.recv_sems, self.loc_sems = sems

    def _remote(self, k, a, src, dst, pos):
        idx = (k - 1) * self.n_arr + a
        return pltpu.make_async_remote_copy(src_ref=src, dst_ref=dst, send_sem=self.send_sems.at[idx],
                                            recv_sem=self.recv_sems.at[idx], device_id=pos, device_id_type=MESH_ID)

    def _outgoing(self):
        me = _my_index()
        land = self.land_slot(me)
        remote = []
        for k in range(1, N_DEV):
            pos, p = _peer(k)
            src = self.send_part(p)
            remote += [self._remote(k, a, src[a], land[a], pos) for a in range(self.n_arr)]
        own = self.send_part(me)
        local = [pltpu.make_async_copy(own[a], land[a], self.loc_sems.at[a]) for a in range(self.n_arr)]
        return remote, local

    def start(self):
        remote, local = self._outgoing()
        for cp in remote + local:
            cp.start()

    def wait(self):
        own = self.send_part(_my_index())
        for k in range(1, N_DEV):
            pos, p = _peer(k)
            land = self.land_slot(p)
            for a in range(self.n_arr):
                self._remote(k, a, own[a], land[a], pos).wait_recv()
        remote, local = self._outgoing()
        for cp in remote:
            cp.wait_send()
        for cp in local:
            cp.wait()


def _push_call(name, arrays, out_shapes, send_part, land_slot):
    n_arr = len(arrays)

    def body(*refs):
        ins, outs, sems = refs[:n_arr], refs[n_arr:2 * n_arr], refs[2 * n_arr:]
        push = _Push(lambda p: send_part(ins, p), lambda s: land_slot(outs, s), sems, n_arr)
        push.start()
        push.wait()

    return pl.pallas_call(
        body, name=name, in_specs=[ANY] * n_arr, out_specs=[ANY] * n_arr, out_shape=out_shapes,
        scratch_shapes=_push_sems(n_arr), compiler_params=pltpu.CompilerParams(has_side_effects=True),
    )(*arrays)


LRU_ROWS = LRU_BLOCK // N_DEV
FFN_PAD_ROWS = FFN_GROUP - 2 * FFN_OUT_SHARD


def _half_rows(d):
    return pl.ds(pl.multiple_of((d % 2) * FFN_OUT_SHARD, 16), FFN_OUT_SHARD)


def _lru_rows(d):
    return pl.ds(pl.multiple_of(d * LRU_ROWS, 16), LRU_ROWS)


def _rest_slots(outs, d):
    return [outs[0].at[d], outs[1].at[d], outs[2].at[d], outs[3].at[d],
            outs[4].at[d // 2, _half_rows(d), :], outs[5].at[:, _lru_rows(d), :], outs[6].at[:, _lru_rows(d), :]]


REST_SHAPES = [
    jax.ShapeDtypeStruct((N_DEV, D_MODEL // N_DEV, D_MODEL), BF16),
    jax.ShapeDtypeStruct((N_DEV, D_MODEL // N_DEV, D_MODEL), BF16),
    jax.ShapeDtypeStruct((N_DEV, D_MODEL // N_DEV, D_MODEL), BF16),
    jax.ShapeDtypeStruct((N_DEV, D_MODEL, FFN_GROUP), BF16),
    jax.ShapeDtypeStruct((FFN_GROUPS, FFN_GROUP, D_MODEL), BF16),
    jax.ShapeDtypeStruct((LRU_BLOCKS, LRU_BLOCK, LRU_BLOCK), BF16),
    jax.ShapeDtypeStruct((LRU_BLOCKS, LRU_BLOCK, LRU_BLOCK), BF16),
]


def _in_proj(h0, norm_w, win_g, rest_shards):
    rows = h0.shape[0]
    tm = _tile(rows, 640)
    n_i = rows // tm
    n_rest = len(rest_shards)

    def body(h_ref, nw_ref, w_ref, *refs):
        shard_refs = refs[:n_rest]
        proj_ref, u_ref = refs[n_rest:n_rest + 2]
        gath_refs = refs[n_rest + 2:2 * n_rest + 2]
        u_s, send_sems, recv_sems, loc_sems, zero_sems, zbuf = refs[2 * n_rest + 2:]
        i, j = pl.program_id(0), pl.program_id(1)
        push = _Push(lambda p: list(shard_refs), lambda s: _rest_slots(gath_refs, s),
                     (send_sems, recv_sems, loc_sems), n_rest)
        zero_fill = [pltpu.make_async_copy(zbuf, gath_refs[4].at[g, pl.ds(2 * FFN_OUT_SHARD, FFN_PAD_ROWS), :],
                                           zero_sems.at[g]) for g in range(FFN_GROUPS)]

        @pl.when(jnp.logical_and(i == 0, j == 0))
        def _():
            push.start()
            zbuf[...] = jnp.zeros_like(zbuf)
            for cp in zero_fill:
                cp.start()

        @pl.when(j == 0)
        def _():
            x = h_ref[...]
            rs = lax.rsqrt(jnp.mean(x * x, axis=-1, keepdims=True) + NORM_EPS)
            u = (x * rs * nw_ref[...]).astype(BF16)
            u_s[...] = u
            u_ref[...] = u
        proj_ref[...] = _dot(u_s[...], w_ref[...])

        @pl.when(jnp.logical_and(i == n_i - 1, j == N_DEV - 1))
        def _():
            push.wait()
            for cp in zero_fill:
                cp.wait()

    return pl.pallas_call(
        body, name="in_proj", grid=(n_i, N_DEV),
        in_specs=[pl.BlockSpec((tm, D_MODEL), lambda i, j: (i, 0)),
                  pl.BlockSpec((1, D_MODEL), lambda i, j: (0, 0)),
                  pl.BlockSpec((None, D_MODEL, D_MODEL), lambda i, j: (j, 0, 0))] + [ANY] * n_rest,
        out_specs=[pl.BlockSpec((tm, D_MODEL), lambda i, j: (i, j)),
                   pl.BlockSpec((tm, D_MODEL), lambda i, j: (i, 0))] + [ANY] * n_rest,
        out_shape=[jax.ShapeDtypeStruct((rows, N_DEV * D_MODEL), F32),
                   jax.ShapeDtypeStruct((rows, D_MODEL), BF16)] + REST_SHAPES,
        scratch_shapes=[pltpu.VMEM((tm, D_MODEL), BF16)] + _push_sems(n_rest)
        + [pltpu.SemaphoreType.DMA((FFN_GROUPS,)), pltpu.VMEM((FFN_PAD_ROWS, D_MODEL), BF16)],
        compiler_params=pltpu.CompilerParams(dimension_semantics=("arbitrary", "arbitrary"),
                                             vmem_limit_bytes=VMEM_LIMIT, has_side_effects=True),
    )(h0, norm_w, win_g, *rest_shards)


def _seg_spec(rows_per_block, seg):
    return pl.BlockSpec((rows_per_block, D_MODEL), lambda n, seg=seg: (n, seg))


def _retention_fwd(proj, cos2, sin2, dec):
    rows = proj.shape[0]
    n_chunks = rows // CHUNK

    def body(q_ref, k_ref, v_ref, g_ref, c_ref, s_ref, dec_ref, o_ref, zr_ref, st_ref, state):
        @pl.when(pl.program_id(0) == 0)
        def _():
            state[...] = jnp.zeros_like(state)
        cos_t, sin_t = c_ref[...], s_ref[...]
        for h in range(HEADS):
            sl = slice(HEAD_DIM * h, HEAD_DIM * (h + 1))
            qh = _rot(q_ref[:, sl], cos_t, sin_t)
            kh = _rot(k_ref[:, sl], cos_t, sin_t) * QK_SCALE
            qb, kb, vb = qh.astype(BF16), kh.astype(BF16), v_ref[:, sl].astype(BF16)
            s = _dot_nt(qb, kb) * dec_ref[0, h]
            st = state[h]
            st_ref[h] = st
            o = _dot(s.astype(BF16), vb) + _dot(qb, st.astype(BF16)) * dec_ref[1, h]
            state[h] = st * dec_ref[3, h] + _dot_tn((kh * dec_ref[2, h]).astype(BF16), vb)
            o_ref[:, sl] = o
            r = lax.rsqrt(jnp.mean(o * o, axis=-1, keepdims=True) + NORM_EPS)
            g = g_ref[:, sl]
            zr_ref[:, sl] = (g * _sigmoid(g) * (o * r)).astype(BF16)

    tab = pl.BlockSpec((CHUNK, HEAD_DIM), lambda n: (n, 0))
    return pl.pallas_call(
        body, name="retention_fwd", grid=(n_chunks,),
        in_specs=[_seg_spec(CHUNK, 0), _seg_spec(CHUNK, 1), _seg_spec(CHUNK, 2), _seg_spec(CHUNK, 3), tab, tab,
                  pl.BlockSpec((4, HEADS, CHUNK, CHUNK), lambda n: (0, 0, 0, 0))],
        out_specs=[pl.BlockSpec((CHUNK, D_MODEL), lambda n: (n, 0)),
                   pl.BlockSpec((CHUNK, D_MODEL), lambda n: (n, 0)),
                   pl.BlockSpec((None, HEADS, HEAD_DIM, HEAD_DIM), lambda n: (n, 0, 0, 0))],
        out_shape=[jax.ShapeDtypeStruct((rows, D_MODEL), F32),
                   jax.ShapeDtypeStruct((rows, D_MODEL), BF16),
                   jax.ShapeDtypeStruct((n_chunks, HEADS, HEAD_DIM, HEAD_DIM), F32)],
        scratch_shapes=[pltpu.VMEM((HEADS, HEAD_DIM, HEAD_DIM), F32)],
        compiler_params=_cparams(("arbitrary",)),
    )(proj, proj, proj, proj, cos2, sin2, dec)


def _lru_gates(c, ba, bx, lam, wa_ref, wx_ref):
    pre_r, pre_i = [], []
    for g in range(LRU_BLOCKS):
        cg = c[:, LRU_BLOCK * g:LRU_BLOCK * (g + 1)].astype(BF16)
        pre_r.append(_dot(cg, wa_ref[g]))
        pre_i.append(_dot(cg, wx_ref[g]))
    r = _sigmoid(jnp.concatenate(pre_r, axis=1) + ba)
    ig = _sigmoid(jnp.concatenate(pre_i, axis=1) + bx)
    sp = jnp.maximum(-lam, 0.0) + jnp.log(1.0 + jnp.exp(-jnp.abs(lam)))
    log_a = -LRU_C * r * sp
    a = jnp.exp(log_a)
    mult = jnp.sqrt(-jnp.tanh(log_a) * (a * a + 1.0))
    return r, ig, a, mult, sp


def _conv_taps(xbuf, tm, cw_ref, cb_ref):
    c = cb_ref[...] + cw_ref[3:4, :] * xbuf[8:8 + tm, :]
    for back in (1, 2, 3):
        c = c + cw_ref[3 - back:4 - back, :] * xbuf[8 - back:8 - back + tm, :]
    return c


def _lru_fwd(proj, conv_w, conv_b, ba, bx, lam, wa_g, wx_g):
    rows = proj.shape[0]
    tm = _tile(rows, 320)

    def body(x_ref, gt_ref, cw_ref, cb_ref, ba_ref, bx_ref, lam_ref, wa_ref, wx_ref, hs_ref, zl_ref,
             xbuf, abuf, ubuf, hcar):
        i = pl.program_id(0)

        @pl.when(i == 0)
        def _():
            xbuf[0:8, :] = jnp.zeros((8, D_MODEL), F32)
            hcar[...] = jnp.zeros_like(hcar)

        xbuf[8:8 + tm, :] = x_ref[...]
        c = _conv_taps(xbuf, tm, cw_ref, cb_ref)
        xbuf[0:8, :] = xbuf[tm:tm + 8, :]
        r, ig, a, mult, _ = _lru_gates(c, ba_ref[...], bx_ref[...], lam_ref[...], wa_ref, wx_ref)
        row = i * tm + lax.broadcasted_iota(jnp.int32, (tm, 1), 0)
        abuf[...] = a
        ubuf[...] = jnp.where(row >= PAD_ROWS, mult * (ig * c), 0.0)

        sub = lax.broadcasted_iota(jnp.int32, (8, D_MODEL), 0)

        def block(b, carry):
            off = pl.multiple_of(b * 8, 8)
            av, uv = abuf[pl.ds(off, 8), :], ubuf[pl.ds(off, 8), :]
            for s in (1, 2, 4):
                us = jnp.where(sub >= s, pltpu.roll(uv, s, 0), 0.0)
                as_ = jnp.where(sub >= s, pltpu.roll(av, s, 0), 1.0)
                uv = uv + av * us
                av = av * as_
            hv = uv + av * carry
            hs_ref[pl.ds(off, 8), :] = hv
            return hv[7:8, :]

        hcar[...] = lax.fori_loop(0, tm // 8, block, hcar[...])
        gl, _ = _gelu_parts(gt_ref[...])
        zl_ref[...] = (gl * hs_ref[...]).astype(BF16)

    vec = pl.BlockSpec((1, D_MODEL), lambda i: (0, 0))
    mat = pl.BlockSpec((LRU_BLOCKS, LRU_BLOCK, LRU_BLOCK), lambda i: (0, 0, 0))
    return pl.pallas_call(
        body, name="lru_fwd", grid=(rows // tm,),
        in_specs=[_seg_spec(tm, 4), _seg_spec(tm, 5), pl.BlockSpec((4, D_MODEL), lambda i: (0, 0)),
                  vec, vec, vec, vec, mat, mat],
        out_specs=[pl.BlockSpec((tm, D_MODEL), lambda i: (i, 0)), pl.BlockSpec((tm, D_MODEL), lambda i: (i, 0))],
        out_shape=[jax.ShapeDtypeStruct((rows, D_MODEL), F32), jax.ShapeDtypeStruct((rows, D_MODEL), BF16)],
        scratch_shapes=[pltpu.VMEM((tm + 8, D_MODEL), F32), pltpu.VMEM((tm, D_MODEL), F32),
                        pltpu.VMEM((tm, D_MODEL), F32), pltpu.VMEM((1, D_MODEL), F32)],
        compiler_params=_cparams(("arbitrary",)),
    )(proj, proj, conv_w, conv_b, ba, bx, lam, wa_g, wx_g)


def _mix_fwd(zr, zl, proj, h0, wbr, wbl, wout):
    rows = h0.shape[0]
    tm = _tile(rows, 320)

    def body(zr_ref, zl_ref, ga_ref, gb_ref, h0_ref, wbr_ref, wbl_ref, wo_ref, h1_ref, yr_ref, yl_ref, mx_ref):
        yr = _dot(zr_ref[...], wbr_ref[...])
        yl = _dot(zl_ref[...], wbl_ref[...])
        mixed = (_sigmoid(ga_ref[...]) * yr + _sigmoid(gb_ref[...]) * yl).astype(BF16)
        yr_ref[...] = yr
        yl_ref[...] = yl
        mx_ref[...] = mixed
        h1_ref[...] = h0_ref[...] + _dot(mixed, wo_ref[...])

    row = pl.BlockSpec((tm, D_MODEL), lambda i: (i, 0))
    wsp = pl.BlockSpec((D_MODEL, D_MODEL), lambda i: (0, 0))
    return pl.pallas_call(
        body, name="mix_fwd", grid=(rows // tm,),
        in_specs=[row, row, _seg_spec(tm, 6), _seg_spec(tm, 7), row, wsp, wsp, wsp],
        out_specs=[row, row, row, row],
        out_shape=[jax.ShapeDtypeStruct((rows, D_MODEL), F32)] * 3 + [jax.ShapeDtypeStruct((rows, D_MODEL), BF16)],
        compiler_params=_cparams(("parallel",)),
    )(zr, zl, proj, proj, h0, wbr, wbl, wout)


def _ffn_fwd_loss(h1, norm_w, wfi_g, wfo_g, final_w, target):
    rows = h1.shape[0]
    tm = _tile(rows, 640)
    last = FFN_GROUPS - 1

    def body(h1_ref, nw_ref, wg_ref, wu_ref, wo_ref, fw_ref, t_ref,
             u2_ref, g_ref, up_ref, act_ref, dh2_ref, red_ref, u2_s, acc):
        i, d = pl.program_id(0), pl.program_id(1)

        @pl.when(jnp.logical_and(i == 0, d == 0))
        def _():
            red_ref[...] = jnp.zeros_like(red_ref)

        @pl.when(d == 0)
        def _():
            x = h1_ref[...]
            rs = lax.rsqrt(jnp.mean(x * x, axis=-1, keepdims=True) + NORM_EPS)
            u2 = (x * rs * nw_ref[...]).astype(BF16)
            u2_s[...] = u2
            u2_ref[...] = u2
            acc[...] = jnp.zeros_like(acc)

        g = _dot(u2_s[...], wg_ref[...])
        up = _dot(u2_s[...], wu_ref[...])
        act = (g * _sigmoid(g) * up).astype(BF16)
        g_ref[...] = g.astype(BF16)
        up_ref[...] = up.astype(BF16)
        act_ref[...] = act
        acc[...] += _dot(act, wo_ref[...])

        @pl.when(d == last)
        def _():
            h2 = h1_ref[...] + acc[...]
            rs = lax.rsqrt(jnp.mean(h2 * h2, axis=-1, keepdims=True) + NORM_EPS)
            nh = h2 * rs
            fw = fw_ref[...]
            row = i * tm + lax.broadcasted_iota(jnp.int32, (tm, 1), 0)
            diff = jnp.where(row >= CHUNK, nh * fw - t_ref[...], 0.0)
            dy = diff * (1.0 / D_MODEL)
            red_ref[0:1, :] += jnp.sum(diff * diff, axis=0, keepdims=True)
            red_ref[1:2, :] += jnp.sum(dy * nh, axis=0, keepdims=True)
            dn = dy * fw
            dh2_ref[...] = rs * (dn - nh * jnp.mean(dn * nh, axis=-1, keepdims=True))

    row = pl.BlockSpec((tm, D_MODEL), lambda i, d: (i, 0))
    vec = pl.BlockSpec((1, D_MODEL), lambda i, d: (0, 0))
    hid = pl.BlockSpec((tm, FFN_GROUP), lambda i, d: (i, d))
    hid_shape = jax.ShapeDtypeStruct((rows, FFN_GROUPS * FFN_GROUP), BF16)
    return pl.pallas_call(
        body, name="ffn_fwd_loss", grid=(rows // tm, FFN_GROUPS),
        in_specs=[row, vec,
                  pl.BlockSpec((None, D_MODEL, FFN_GROUP), lambda i, d: (d, 0, 0)),
                  pl.BlockSpec((None, D_MODEL, FFN_GROUP), lambda i, d: (d + FFN_GROUPS, 0, 0)),
                  pl.BlockSpec((None, FFN_GROUP, D_MODEL), lambda i, d: (d, 0, 0)),
                  vec, row],
        out_specs=[row, hid, hid, hid, row, pl.BlockSpec((8, D_MODEL), lambda i, d: (0, 0))],
        out_shape=[jax.ShapeDtypeStruct((rows, D_MODEL), BF16), hid_shape, hid_shape, hid_shape,
                   jax.ShapeDtypeStruct((rows, D_MODEL), F32), jax.ShapeDtypeStruct((8, D_MODEL), F32)],
        scratch_shapes=[pltpu.VMEM((tm, D_MODEL), BF16), pltpu.VMEM((tm, D_MODEL), F32)],
        compiler_params=_cparams(("arbitrary", "arbitrary")),
    )(h1, norm_w, wfi_g, wfi_g, wfo_g, final_w, target)


def _wgrad(a, b, ka, tn, out_dtype, b_halves=False):
    rows = a.shape[0]
    na = a.shape[1] // ka
    tm = _tile(rows, 640)
    nm = rows // tm
    if b_halves:
        per_half = b.shape[2] // tn
        nb = 2 * per_half
        b_spec = pl.BlockSpec((None, tm, tn), lambda p, q, m: (q // per_half, m, q % per_half))
    else:
        nb = b.shape[1] // tn
        b_spec = pl.BlockSpec((tm, tn), lambda p, q, m: (m, q))

    def body(a_ref, b_ref, o_ref, acc):
        m = pl.program_id(2)

        @pl.when(m == 0)
        def _():
            acc[...] = jnp.zeros_like(acc)

        acc[...] += _dot_tn(a_ref[...].astype(BF16), b_ref[...].astype(BF16))

        @pl.when(m == nm - 1)
        def _():
            o_ref[...] = acc[...].astype(out_dtype)

    return pl.pallas_call(
        body, name="wgrad", grid=(na, nb, nm),
        in_specs=[pl.BlockSpec((tm, ka), lambda p, q, m: (m, p)), b_spec],
        out_specs=pl.BlockSpec((None, None, ka, tn), lambda p, q, m: (p, q, 0, 0)),
        out_shape=jax.ShapeDtypeStruct((na, nb, ka, tn), out_dtype),
        scratch_shapes=[pltpu.VMEM((ka, tn), F32)],
        compiler_params=_cparams(("parallel", "parallel", "arbitrary")),
    )(a, b)


def _ffn_bwd(dh2, g, up, h1, norm_w, wfi_g, wfo_g):
    rows = h1.shape[0]
    tm = _tile(rows, 640)
    last = FFN_GROUPS - 1

    def body(dh2_ref, g_ref, up_ref, h1_ref, nw_ref, wg_ref, wu_ref, wo_ref, dgu_ref, dh1_ref, dw_ref,
             dh2_s, acc):
        i, d = pl.program_id(0), pl.program_id(1)

        @pl.when(jnp.logical_and(i == 0, d == 0))
        def _():
            dw_ref[...] = jnp.zeros_like(dw_ref)

        @pl.when(d == 0)
        def _():
            dh2_s[...] = dh2_ref[...].astype(BF16)
            acc[...] = jnp.zeros_like(acc)

        dact = _dot_nt(dh2_s[...], wo_ref[...])
        gv, uv = g_ref[...].astype(F32), up_ref[...].astype(F32)
        sg = _sigmoid(gv)
        dg = (dact * uv * (sg * (1.0 + gv * (1.0 - sg)))).astype(BF16)
        dup = (dact * (gv * sg)).astype(BF16)
        dgu_ref[0] = dg
        dgu_ref[1] = dup
        acc[...] += _dot_nt(dg, wg_ref[...]) + _dot_nt(dup, wu_ref[...])

        @pl.when(d == last)
        def _():
            dx, dw = _rms_bwd(h1_ref[...], nw_ref[...], acc[...])
            dw_ref[0:1, :] += dw
            dh1_ref[...] = dh2_ref[...] + dx

    row = pl.BlockSpec((tm, D_MODEL), lambda i, d: (i, 0))
    vec = pl.BlockSpec((1, D_MODEL), lambda i, d: (0, 0))
    hid = pl.BlockSpec((tm, FFN_GROUP), lambda i, d: (i, d))
    return pl.pallas_call(
        body, name="ffn_bwd", grid=(rows // tm, FFN_GROUPS),
        in_specs=[row, hid, hid, row, vec,
                  pl.BlockSpec((None, D_MODEL, FFN_GROUP), lambda i, d: (d, 0, 0)),
                  pl.BlockSpec((None, D_MODEL, FFN_GROUP), lambda i, d: (d + FFN_GROUPS, 0, 0)),
                  pl.BlockSpec((None, FFN_GROUP, D_MODEL), lambda i, d: (d, 0, 0))],
        out_specs=[pl.BlockSpec((2, tm, FFN_GROUP), lambda i, d: (0, i, d)), row,
                   pl.BlockSpec((8, D_MODEL), lambda i, d: (0, 0))],
        out_shape=[jax.ShapeDtypeStruct((2, rows, FFN_GROUPS * FFN_GROUP), BF16),
                   jax.ShapeDtypeStruct((rows, D_MODEL), F32), jax.ShapeDtypeStruct((8, D_MODEL), F32)],
        scratch_shapes=[pltpu.VMEM((tm, D_MODEL), BF16), pltpu.VMEM((tm, D_MODEL), F32)],
        compiler_params=_cparams(("arbitrary", "arbitrary")),
    )(dh2, g, up, h1, norm_w, wfi_g, wfi_g, wfo_g)


def _mix_bwd(dh1, yr, yl, proj, wbr, wbl, wout):
    rows = dh1.shape[0]
    tm = _tile(rows, 320)

    def body(dh1_ref, yr_ref, yl_ref, ga_ref, gb_ref, wbr_ref, wbl_ref, wo_ref,
             dyr_ref, dyl_ref, dseg_ref, dzr_ref, dzl_ref):
        dmix = _dot_nt(dh1_ref[...].astype(BF16), wo_ref[...])
        sa, sb = _sigmoid(ga_ref[...]), _sigmoid(gb_ref[...])
        dyr = (dmix * sa).astype(BF16)
        dyl = (dmix * sb).astype(BF16)
        dyr_ref[...] = dyr
        dyl_ref[...] = dyl
        dseg_ref[:, 0:D_MODEL] = (dmix * yr_ref[...] * (sa * (1.0 - sa))).astype(BF16)
        dseg_ref[:, D_MODEL:2 * D_MODEL] = (dmix * yl_ref[...] * (sb * (1.0 - sb))).astype(BF16)
        dzr_ref[...] = _dot_nt(dyr, wbr_ref[...])
        dzl_ref[...] = _dot_nt(dyl, wbl_ref[...])

    row = pl.BlockSpec((tm, D_MODEL), lambda i: (i, 0))
    wsp = pl.BlockSpec((D_MODEL, D_MODEL), lambda i: (0, 0))
    bshape = jax.ShapeDtypeStruct((rows, D_MODEL), BF16)
    fshape = jax.ShapeDtypeStruct((rows, D_MODEL), F32)
    return pl.pallas_call(
        body, name="mix_bwd", grid=(rows // tm,),
        in_specs=[row, row, row, _seg_spec(tm, 6), _seg_spec(tm, 7), wsp, wsp, wsp],
        out_specs=[row, row, pl.BlockSpec((tm, 2 * D_MODEL), lambda i: (i, 3)), row, row],
        out_shape=[bshape, bshape, jax.ShapeDtypeStruct((rows, N_DEV * D_MODEL), BF16), fshape, fshape],
        compiler_params=_cparams(("parallel",)),
    )(dh1, yr, yl, proj, proj, wbr, wbl, wout)


S1_SHAPES = [
    jax.ShapeDtypeStruct((N_DEV, D_MODEL // N_DEV, D_MODEL), BF16),
    jax.ShapeDtypeStruct((N_DEV, D_MODEL // N_DEV, D_MODEL), BF16),
    jax.ShapeDtypeStruct((N_DEV, D_MODEL // N_DEV, D_MODEL), BF16),
    jax.ShapeDtypeStruct((N_DEV, D_MODEL, FFN_GROUP), BF16),
    jax.ShapeDtypeStruct((N_DEV, FFN_OUT_SHARD, D_MODEL), BF16),
]


def _s1_parts(ins, p):
    return [ins[0].at[p], ins[1].at[p], ins[2].at[p], ins[3].at[p], ins[4].at[p // 2, _half_rows(p), :]]


def _lru_bwd(dzl, hs, proj, dproj, conv_w, conv_b, ba, bx, lam, wa_g, wx_g, s1_grads):
    rows = dzl.shape[0]
    tm = _tile(rows, 320)
    nt = rows // tm
    t8 = tm // 8
    n_s1 = len(s1_grads)

    def body(dzl_ref, hs_ref, hsp_ref, x_ref, xp_ref, gt_ref, cw_ref, cb_ref, ba_ref, bx_ref, lam_ref,
             wa_ref, wx_ref, dproj_in, *refs):
        del dproj_in
        s1_refs = refs[:n_s1]
        dseg_ref, dwa_ref, dwx_ref, sm_ref = refs[n_s1:n_s1 + 4]
        land_refs = refs[n_s1 + 4:2 * n_s1 + 4]
        xbuf, abuf, bbuf, dbuf, dcbuf, anext, dhcar, send_sems, recv_sems, loc_sems = refs[2 * n_s1 + 4:]
        step = pl.program_id(0)
        i = nt - 1 - step
        push = _Push(lambda p: _s1_parts(s1_refs, p), lambda s: [r.at[s] for r in land_refs],
                     (send_sems, recv_sems, loc_sems), n_s1)

        @pl.when(step == 0)
        def _():
            push.start()
            dwa_ref[...] = jnp.zeros_like(dwa_ref)
            dwx_ref[...] = jnp.zeros_like(dwx_ref)
            sm_ref[...] = jnp.zeros_like(sm_ref)
            anext[...] = jnp.zeros_like(anext)
            dhcar[...] = jnp.zeros_like(dhcar)
            dcbuf[tm:tm + 8, :] = jnp.zeros((8, D_MODEL), F32)

        first = i == 0
        xbuf[0:8, :] = jnp.where(first, 0.0, xp_ref[...])
        xbuf[8:8 + tm, :] = x_ref[...]
        c = _conv_taps(xbuf, tm, cw_ref, cb_ref)
        lam_v = lam_ref[...]
        r, ig, a, mult, sp = _lru_gates(c, ba_ref[...], bx_ref[...], lam_v, wa_ref, wx_ref)
        hs_v = hs_ref[...]
        gl, dgl = _gelu_parts(gt_ref[...])
        dzl_v = dzl_ref[...]
        dseg_ref[:, D_MODEL:2 * D_MODEL] = (dzl_v * hs_v * dgl).astype(BF16)
        dbuf[...] = dzl_v * gl
        abuf[0:tm, :] = a
        abuf[tm:tm + 8, :] = jnp.broadcast_to(anext[...], (8, D_MODEL))
        bbuf[...] = abuf[1:tm + 1, :]

        sub = lax.broadcasted_iota(jnp.int32, (8, D_MODEL), 0)

        def block(k, carry):
            off = pl.multiple_of((t8 - 1 - k) * 8, 8)
            av = bbuf[pl.ds(off, 8), :]
            uv = dbuf[pl.ds(off, 8), :]
            for s in (1, 2, 4):
                us = jnp.where(sub < 8 - s, pltpu.roll(uv, 8 - s, 0), 0.0)
                as_ = jnp.where(sub < 8 - s, pltpu.roll(av, 8 - s, 0), 1.0)
                uv = uv + av * us
                av = av * as_
            hv = uv + av * carry
            dbuf[pl.ds(off, 8), :] = hv
            return hv[0:1, :]

        dhcar[...] = lax.fori_loop(0, t8, block, dhcar[...])
        anext[...] = abuf[0:1, :]
        dh = dbuf[...]

        xbuf[0:8, :] = jnp.where(first, 0.0, hsp_ref[...])
        xbuf[8:8 + tm, :] = hs_v
        hprev = xbuf[7:7 + tm, :]
        row = i * tm + lax.broadcasted_iota(jnp.int32, (tm, 1), 0)
        duu = jnp.where(row >= PAD_ROWS, dh, 0.0)
        da = dh * hprev
        dmult = duu * ig * c
        di = duu * mult * c
        dc = duu * mult * ig
        dlog_a = da * a - dmult * (a * a) / mult
        dr = dlog_a * (-LRU_C * sp)
        dsp = jnp.sum(dlog_a * (-LRU_C * r), axis=0, keepdims=True)
        dpr = dr * r * (1.0 - r)
        dpi = di * ig * (1.0 - ig)
        dpr_b, dpi_b = dpr.astype(BF16), dpi.astype(BF16)
        dcs = []
        for g in range(LRU_BLOCKS):
            sl = slice(LRU_BLOCK * g, LRU_BLOCK * (g + 1))
            cg = c[:, sl].astype(BF16)
            dwa_ref[g] += _dot_tn(cg, dpr_b[:, sl])
            dwx_ref[g] += _dot_tn(cg, dpi_b[:, sl])
            dcs.append(_dot_nt(dpr_b[:, sl], wa_ref[g]) + _dot_nt(dpi_b[:, sl], wx_ref[g]))
        dc = dc + jnp.concatenate(dcs, axis=1)

        dcbuf[0:tm, :] = dc
        xbuf[8:8 + tm, :] = x_ref[...]
        xbuf[0:8, :] = jnp.where(first, 0.0, xp_ref[...])
        dlin = cw_ref[3:4, :] * dc
        sm_ref[3:4, :] += jnp.sum(dc * xbuf[8:8 + tm, :], axis=0, keepdims=True)
        for back in (1, 2, 3):
            dlin = dlin + cw_ref[3 - back:4 - back, :] * dcbuf[back:back + tm, :]
            sm_ref[3 - back:4 - back, :] += jnp.sum(dc * xbuf[8 - back:8 - back + tm, :], axis=0, keepdims=True)
        dseg_ref[:, 0:D_MODEL] = dlin.astype(BF16)
        dcbuf[tm:tm + 8, :] = dcbuf[0:8, :]
        sm_ref[4:5, :] += jnp.sum(dc, axis=0, keepdims=True)
        sm_ref[5:6, :] += jnp.sum(dpr, axis=0, keepdims=True)
        sm_ref[6:7, :] += jnp.sum(dpi, axis=0, keepdims=True)
        sm_ref[7:8, :] += dsp * (-_sigmoid(-lam_v))

        @pl.when(step == nt - 1)
        def _():
            push.wait()

    rowb = pl.BlockSpec((tm, D_MODEL), lambda s: (nt - 1 - s, 0))
    prev8 = pl.BlockSpec((8, D_MODEL), lambda s: (jnp.maximum((nt - 1 - s) * t8 - 1, 0), 0))
    seg = lambda k: pl.BlockSpec((tm, D_MODEL), lambda s, k=k: (nt - 1 - s, k))
    prev8_seg4 = pl.BlockSpec((8, D_MODEL), lambda s: (jnp.maximum((nt - 1 - s) * t8 - 1, 0), 4))
    vec = pl.BlockSpec((1, D_MODEL), lambda s: (0, 0))
    mat = pl.BlockSpec((LRU_BLOCKS, LRU_BLOCK, LRU_BLOCK), lambda s: (0, 0, 0))
    mshape = jax.ShapeDtypeStruct((LRU_BLOCKS, LRU_BLOCK, LRU_BLOCK), F32)
    n_in = 13
    return pl.pallas_call(
        body, name="lru_bwd", grid=(nt,),
        in_specs=[rowb, rowb, prev8, seg(4), prev8_seg4, seg(5), pl.BlockSpec((4, D_MODEL), lambda s: (0, 0)),
                  vec, vec, vec, vec, mat, mat, ANY] + [ANY] * n_s1,
        out_specs=[pl.BlockSpec((tm, 2 * D_MODEL), lambda s: (nt - 1 - s, 2)), mat, mat,
                   pl.BlockSpec((8, D_MODEL), lambda s: (0, 0))] + [ANY] * n_s1,
        out_shape=[jax.ShapeDtypeStruct(dproj.shape, dproj.dtype), mshape, mshape,
                   jax.ShapeDtypeStruct((8, D_MODEL), F32)] + S1_SHAPES,
        input_output_aliases={n_in: 0},
        scratch_shapes=[pltpu.VMEM((tm + 8, D_MODEL), F32), pltpu.VMEM((tm + 8, D_MODEL), F32),
                        pltpu.VMEM((tm, D_MODEL), F32), pltpu.VMEM((tm, D_MODEL), F32),
                        pltpu.VMEM((tm + 8, D_MODEL), F32),
                        pltpu.VMEM((1, D_MODEL), F32), pltpu.VMEM((1, D_MODEL), F32)] + _push_sems(n_s1),
        compiler_params=pltpu.CompilerParams(dimension_semantics=("arbitrary",), vmem_limit_bytes=VMEM_LIMIT,
                                             has_side_effects=True),
    )(dzl, hs, hs, proj, proj, proj, conv_w, conv_b, ba, bx, lam, wa_g, wx_g, dproj, *s1_grads)


def _retention_bwd(dzr, o, proj, states, cos2, sin2, dec, dproj):
    rows = dzr.shape[0]
    n_chunks = rows // CHUNK

    def body(dzr_ref, o_ref, q_ref, k_ref, v_ref, g_ref, st_ref, c_ref, s_ref, dec_ref, dproj_in,
             dseg_ref, dstate):
        del dproj_in

        @pl.when(pl.program_id(0) == 0)
        def _():
            dstate[...] = jnp.zeros_like(dstate)
        cos_t, sin_t = c_ref[...], s_ref[...]
        for h in range(HEADS):
            sl = slice(HEAD_DIM * h, HEAD_DIM * (h + 1))
            o = o_ref[:, sl]
            g = g_ref[:, sl]
            dzr_v = dzr_ref[:, sl]
            sg = _sigmoid(g)
            r = lax.rsqrt(jnp.mean(o * o, axis=-1, keepdims=True) + NORM_EPS)
            on = o * r
            dseg_ref[:, 3 * D_MODEL + HEAD_DIM * h:3 * D_MODEL + HEAD_DIM * (h + 1)] = (
                dzr_v * on * (sg * (1.0 + g * (1.0 - sg)))).astype(BF16)
            don = dzr_v * (g * sg)
            do = r * (don - on * jnp.mean(don * on, axis=-1, keepdims=True))
            dob = do.astype(BF16)

            qh = _rot(q_ref[:, sl], cos_t, sin_t)
            kh = _rot(k_ref[:, sl], cos_t, sin_t) * QK_SCALE
            qb, kb, vb = qh.astype(BF16), kh.astype(BF16), v_ref[:, sl].astype(BF16)
            intra, qd, kd, cd = dec_ref[0, h], dec_ref[1, h], dec_ref[2, h], dec_ref[3, h]
            s = (_dot_nt(qb, kb) * intra).astype(BF16)
            ds = (_dot_nt(dob, vb) * intra).astype(BF16)
            st_b = st_ref[h].astype(BF16)
            dst = dstate[h]
            dst_b = dst.astype(BF16)
            dv = _dot_tn(s, dob) + _dot((kh * kd).astype(BF16), dst_b)
            dq = _dot(ds, kb) + _dot_nt(dob, st_b) * qd
            dk = _dot_tn(ds, qb) + _dot_nt(vb, dst_b) * kd
            dstate[h] = dst * cd + _dot_tn((qh * qd).astype(BF16), dob)
            dseg_ref[:, 2 * D_MODEL + HEAD_DIM * h:2 * D_MODEL + HEAD_DIM * (h + 1)] = dv.astype(BF16)
            dseg_ref[:, sl] = _rot_t(dq, cos_t, sin_t).astype(BF16)
            dseg_ref[:, D_MODEL + HEAD_DIM * h:D_MODEL + HEAD_DIM * (h + 1)] = (
                _rot_t(dk, cos_t, sin_t) * QK_SCALE).astype(BF16)

    rev = lambda s: n_chunks - 1 - s
    rowb = pl.BlockSpec((CHUNK, D_MODEL), lambda s: (rev(s), 0))
    seg = lambda k: pl.BlockSpec((CHUNK, D_MODEL), lambda s, k=k: (rev(s), k))
    tab = pl.BlockSpec((CHUNK, HEAD_DIM), lambda s: (rev(s), 0))
    return pl.pallas_call(
        body, name="retention_bwd", grid=(n_chunks,),
        in_specs=[rowb, rowb, seg(0), seg(1), seg(2), seg(3),
                  pl.BlockSpec((None, HEADS, HEAD_DIM, HEAD_DIM), lambda s: (rev(s), 0, 0, 0)), tab, tab,
                  pl.BlockSpec((4, HEADS, CHUNK, CHUNK), lambda s: (0, 0, 0, 0)), ANY],
        out_specs=pl.BlockSpec((CHUNK, 4 * D_MODEL), lambda s: (rev(s), 0)),
        out_shape=jax.ShapeDtypeStruct(dproj.shape, dproj.dtype),
        input_output_aliases={10: 0},
        scratch_shapes=[pltpu.VMEM((HEADS, HEAD_DIM, HEAD_DIM), F32)],
        compiler_params=_cparams(("arbitrary",)),
    )(dzr, o, proj, proj, proj, proj, states, cos2, sin2, dec, dproj)


S2_SHAPES = [
    jax.ShapeDtypeStruct((N_DEV, D_MODEL, D_MODEL), BF16),
    jax.ShapeDtypeStruct((N_DEV, LRU_BLOCKS, LRU_ROWS, LRU_BLOCK), F32),
    jax.ShapeDtypeStruct((N_DEV, LRU_BLOCKS, LRU_ROWS, LRU_BLOCK), F32),
]


def _s2_parts(ins, p):
    rows_p = pl.ds(pl.multiple_of(p * LRU_ROWS, 8), LRU_ROWS)
    return [ins[0].at[p], ins[1].at[:, rows_p, :], ins[2].at[:, rows_p, :]]


def _in_proj_bwd(dproj, win_g, h0, norm_w, dh1, s2_grads):
    rows = h0.shape[0]
    tm = _tile(rows, 640)
    n_i = rows // tm
    n_s2 = len(s2_grads)

    def body(dseg_ref, w_ref, h0_ref, nw_ref, dh1_ref, *refs):
        s2_refs = refs[:n_s2]
        dh0_ref, dw_ref = refs[n_s2:n_s2 + 2]
        land_refs = refs[n_s2 + 2:2 * n_s2 + 2]
        acc, send_sems, recv_sems, loc_sems = refs[2 * n_s2 + 2:]
        i, j = pl.program_id(0), pl.program_id(1)
        push = _Push(lambda p: _s2_parts(s2_refs, p), lambda s: [r.at[s] for r in land_refs],
                     (send_sems, recv_sems, loc_sems), n_s2)

        @pl.when(jnp.logical_and(i == 0, j == 0))
        def _():
            push.start()
            dw_ref[...] = jnp.zeros_like(dw_ref)

        @pl.when(j == 0)
        def _():
            acc[...] = jnp.zeros_like(acc)

        acc[...] += _dot_nt(dseg_ref[...], w_ref[...])

        @pl.when(j == N_DEV - 1)
        def _():
            dx, dw = _rms_bwd(h0_ref[...], nw_ref[...], acc[...])
            dw_ref[0:1, :] += dw
            dh0_ref[...] = dh1_ref[...] + dx

        @pl.when(jnp.logical_and(i == n_i - 1, j == N_DEV - 1))
        def _():
            push.wait()

    row = pl.BlockSpec((tm, D_MODEL), lambda i, j: (i, 0))
    vec = pl.BlockSpec((1, D_MODEL), lambda i, j: (0, 0))
    return pl.pallas_call(
        body, name="in_proj_bwd", grid=(n_i, N_DEV),
        in_specs=[pl.BlockSpec((tm, D_MODEL), lambda i, j: (i, j)),
                  pl.BlockSpec((None, D_MODEL, D_MODEL), lambda i, j: (j, 0, 0)), row, vec, row] + [ANY] * n_s2,
        out_specs=[row, pl.BlockSpec((8, D_MODEL), lambda i, j: (0, 0))] + [ANY] * n_s2,
        out_shape=[jax.ShapeDtypeStruct((rows, D_MODEL), F32), jax.ShapeDtypeStruct((8, D_MODEL), F32)] + S2_SHAPES,
        scratch_shapes=[pltpu.VMEM((tm, D_MODEL), F32)] + _push_sems(n_s2),
        compiler_params=pltpu.CompilerParams(dimension_semantics=("arbitrary", "arbitrary"),
                                             vmem_limit_bytes=VMEM_LIMIT, has_side_effects=True),
    )(dproj, win_g, h0, norm_w, dh1, *s2_grads)


def _adamw(g_slots, w, m, v):
    slots, rows, cols = g_slots.shape
    tr = rows
    for cand in (256, 128, 64, 32, 16, 8):
        if rows % cand == 0 and rows > cand:
            tr = cand
            break

    def body(g_ref, w_ref, m_ref, v_ref, go_ref, d_ref, mo_ref, vo_ref):
        g = g_ref[0].astype(F32)
        for s in range(1, slots):
            g = g + g_ref[s].astype(F32)
        m2 = ADAM_B1 * m_ref[...] + (1.0 - ADAM_B1) * g
        v2 = ADAM_B2 * v_ref[...] + (1.0 - ADAM_B2) * (g * g)
        m_hat = m2 / (1.0 - ADAM_B1 ** ADAM_STEP)
        v_hat = v2 / (1.0 - ADAM_B2 ** ADAM_STEP)
        go_ref[...] = g
        d_ref[...] = -ADAM_LR * (m_hat / (jnp.sqrt(v_hat) + ADAM_EPS) + ADAM_WD * w_ref[...])
        mo_ref[...] = m2
        vo_ref[...] = v2

    blk = pl.BlockSpec((tr, cols), lambda i: (i, 0))
    shape = jax.ShapeDtypeStruct((rows, cols), F32)
    return pl.pallas_call(
        body, name="adamw", grid=(rows // tr,),
        in_specs=[pl.BlockSpec((slots, tr, cols), lambda i: (0, i, 0)), blk, blk, blk],
        out_specs=[blk] * 4, out_shape=[shape] * 4,
        compiler_params=_cparams(("parallel",)),
    )(g_slots, w, m, v)


def _sum_slots(packs):
    slots, rows, cols = packs.shape

    def body(p_ref, o_ref):
        acc = p_ref[0]
        for s in range(1, slots):
            acc = acc + p_ref[s]
        o_ref[...] = acc

    return pl.pallas_call(
        body, name="sum_slots", out_shape=jax.ShapeDtypeStruct((rows, cols), F32),
        compiler_params=pltpu.CompilerParams(vmem_limit_bytes=VMEM_LIMIT),
    )(packs)


def _gather_first(win_shard, small):
    shapes = [jax.ShapeDtypeStruct((N_DEV,) + win_shard.shape, BF16), jax.ShapeDtypeStruct((N_DEV,) + small.shape, F32)]
    return _push_call("gather_first", [win_shard, small], shapes,
                      lambda ins, p: list(ins), lambda outs, s: [r.at[s] for r in outs])


def _share_pack(pack):
    shapes = [jax.ShapeDtypeStruct((N_DEV,) + pack.shape, F32)]
    return _push_call("share_pack", [pack], shapes,
                      lambda ins, p: list(ins), lambda outs, s: [r.at[s] for r in outs])[0]


PACK_MIX_NORM, PACK_CONV_W, PACK_CONV_B, PACK_BA, PACK_BX, PACK_LAM = 0, 8, 12, 13, 14, 15
PACK_FFN_NORM, PACK_SQ_ERR, PACK_FINAL_NORM, PACK_META = 16, 24, 25, 32


def kernel(x, meta_tokens, mix_norm_w, w_in, conv_w, conv_b, lru_wa, lru_ba, lru_wx, lru_bx, lru_lambda, w_branch_ret, w_branch_lru, w_out, ffn_norm_w, w_ffn_in, w_ffn_out, final_norm_w, loss_target, m_meta_tokens, m_mix_norm_w, m_w_in, m_conv_w, m_conv_b, m_lru_wa, m_lru_ba, m_lru_wx, m_lru_bx, m_lru_lambda, m_w_branch_ret, m_w_branch_lru, m_w_out, m_ffn_norm_w, m_w_ffn_in, m_w_ffn_out, m_final_norm_w, v_meta_tokens, v_mix_norm_w, v_w_in, v_conv_w, v_conv_b, v_lru_wa, v_lru_ba, v_lru_wx, v_lru_bx, v_lru_lambda, v_w_branch_ret, v_w_branch_lru, v_w_out, v_ffn_norm_w, v_w_ffn_in, v_w_ffn_out, v_final_norm_w):
    me = _my_index()
    pad4 = ((0, 4), (0, 0))
    fw = final_norm_w.reshape(1, D_MODEL)

    small = jnp.concatenate([meta_tokens, jnp.pad(conv_w[0], pad4)], axis=0)
    win_g, small_g = _gather_first(w_in[0].astype(BF16), small)
    meta_full = small_g[:, :N_META].transpose(1, 0, 2).reshape(N_META, D_MODEL)
    conv_w_full = small_g[:, N_META:N_META + 4].transpose(1, 0, 2).reshape(4, D_MODEL)
    rest_shards = [w_branch_ret[0].astype(BF16), w_branch_lru[0].astype(BF16), w_out[0].astype(BF16),
                   jnp.pad(w_ffn_in[0].astype(BF16), ((0, 0), (0, FFN_GROUP - FFN_SHARD))),
                   w_ffn_out[0].astype(BF16), lru_wa[0].astype(BF16), lru_wx[0].astype(BF16)]

    rows = x.shape[1] + CHUNK
    h0 = jnp.concatenate([jnp.zeros((PAD_ROWS, D_MODEL), F32), meta_full, x[0]], axis=0)
    tgt = jnp.concatenate([jnp.zeros((CHUNK, D_MODEL), F32), loss_target[0]], axis=0)
    cos2, sin2 = _rope_tables(rows)
    dec = _retention_consts()

    proj, u, wbr_g, wbl_g, wout_g, wfi_g, wfo_g, wa_g, wx_g = _in_proj(h0, mix_norm_w, win_g, rest_shards)
    wbr, wbl, wout = (t.reshape(D_MODEL, D_MODEL) for t in (wbr_g, wbl_g, wout_g))
    o, zr, states = _retention_fwd(proj, cos2, sin2, dec)
    hs, zl = _lru_fwd(proj, conv_w_full, conv_b, lru_ba, lru_bx, lru_lambda, wa_g, wx_g)
    h1, yr, yl, mixed = _mix_fwd(zr, zl, proj, h0, wbr, wbl, wout)
    u2, g, up, act, dh2, red = _ffn_fwd_loss(h1, ffn_norm_w, wfi_g, wfo_g, fw, tgt)

    d_wfo = _wgrad(act, dh2, FFN_GROUP, D_MODEL, BF16)[:, 0]
    dgu, dh1, dw_ffn_norm = _ffn_bwd(dh2, g, up, h1, ffn_norm_w, wfi_g, wfo_g)
    d_wfi = _wgrad(u2, dgu, D_MODEL, FFN_GROUP, BF16, b_halves=True)[0]
    d_wout = _wgrad(mixed, dh1, D_MODEL, D_MODEL, BF16)[0, 0]
    dyr, dyl, dproj, dzr, dzl = _mix_bwd(dh1, yr, yl, proj, wbr, wbl, wout)
    d_wbr = _wgrad(zr, dyr, D_MODEL, D_MODEL, BF16)[0, 0]
    d_wbl = _wgrad(zl, dyl, D_MODEL, D_MODEL, BF16)[0, 0]
    s1_grads = [t.reshape(N_DEV, D_MODEL // N_DEV, D_MODEL) for t in (d_wbr, d_wbl, d_wout)] + [d_wfi, d_wfo]
    dproj, d_wa, d_wx, lru_small, r_br, r_bl, r_out, r_fi, r_fo = _lru_bwd(
        dzl, hs, proj, dproj, conv_w_full, conv_b, lru_ba, lru_bx, lru_lambda, wa_g, wx_g, s1_grads)
    dproj = _retention_bwd(dzr, o, proj, states, cos2, sin2, dec, dproj)
    d_win = _wgrad(u, dproj, D_MODEL, D_MODEL, BF16)[0]
    dh0, dw_mix_norm, r_in, r_wa, r_wx = _in_proj_bwd(dproj, win_g, h0, mix_norm_w, dh1, [d_win, d_wa, d_wx])
    grad_x = dh0[CHUNK:]

    pack = jnp.concatenate([dw_mix_norm, lru_small, dw_ffn_norm, red, dh0[PAD_ROWS:CHUNK]], axis=0)
    small_sum = _sum_slots(_share_pack(pack))
    loss = (0.5 / D_MODEL) * jnp.sum(small_sum[PACK_SQ_ERR])

    def big_update(slots, w, m, v):
        shape = w.shape
        w2, m2, v2 = (t.reshape(slots.shape[1:]) for t in (w, m, v))
        return [t.reshape(shape) for t in _adamw(slots, w2, m2, v2)]

    res = {}
    res["w_in"] = big_update(r_in, w_in, m_w_in, v_w_in)
    res["w_branch_ret"] = big_update(r_br, w_branch_ret, m_w_branch_ret, v_w_branch_ret)
    res["w_branch_lru"] = big_update(r_bl, w_branch_lru, m_w_branch_lru, v_w_branch_lru)
    res["w_out"] = big_update(r_out, w_out, m_w_out, v_w_out)
    res["w_ffn_in"] = big_update(r_fi[:, :, :FFN_SHARD], w_ffn_in, m_w_ffn_in, v_w_ffn_in)
    res["w_ffn_out"] = big_update(r_fo, w_ffn_out, m_w_ffn_out, v_w_ffn_out)
    res["lru_wa"] = big_update(r_wa.reshape(N_DEV, LRU_BLOCKS * LRU_ROWS, LRU_BLOCK), lru_wa, m_lru_wa, v_lru_wa)
    res["lru_wx"] = big_update(r_wx.reshape(N_DEV, LRU_BLOCKS * LRU_ROWS, LRU_BLOCK), lru_wx, m_lru_wx, v_lru_wx)

    col = me * HEAD_DIM
    g_meta = lax.dynamic_slice(small_sum, (PACK_META, col), (N_META, HEAD_DIM))
    g_conv = lax.dynamic_slice(small_sum, (PACK_CONV_W, col), (8, HEAD_DIM))
    small_names = ["mix_norm_w", "conv_b", "lru_ba", "lru_bx", "lru_lambda", "ffn_norm_w", "final_norm_w"]
    small_rows = [PACK_MIX_NORM, PACK_CONV_B, PACK_BA, PACK_BX, PACK_LAM, PACK_FFN_NORM, PACK_FINAL_NORM]
    small_w = [mix_norm_w, conv_b, lru_ba, lru_bx, lru_lambda, ffn_norm_w, fw]
    small_m = [m_mix_norm_w, m_conv_b, m_lru_ba, m_lru_bx, m_lru_lambda, m_ffn_norm_w, m_final_norm_w.reshape(1, -1)]
    small_v = [v_mix_norm_w, v_conv_b, v_lru_ba, v_lru_bx, v_lru_lambda, v_ffn_norm_w, v_final_norm_w.reshape(1, -1)]

    def pack_small(vec_list, meta_t, conv_t):
        return jnp.concatenate([t.reshape(8, HEAD_DIM) for t in vec_list] + [meta_t, jnp.pad(conv_t[0], pad4)], axis=0)

    g_small = jnp.concatenate([small_sum[r].reshape(8, HEAD_DIM) for r in small_rows] + [g_meta, g_conv], axis=0)
    outs_small = _adamw(g_small[None], pack_small(small_w, meta_tokens, conv_w),
                        pack_small(small_m, m_meta_tokens, m_conv_w), pack_small(small_v, v_meta_tokens, v_conv_w))
    for idx, name in enumerate(small_names):
        shape = final_norm_w.shape if name == "final_norm_w" else (1, D_MODEL)
        res[name] = [t[8 * idx:8 * idx + 8].reshape(shape) for t in outs_small]
    res["meta_tokens"] = [t[56:72] for t in outs_small]
    res["conv_w"] = [t[72:76].reshape(1, 4, HEAD_DIM) for t in outs_small]

    order = ["meta_tokens", "mix_norm_w", "w_in", "conv_w", "conv_b", "lru_wa", "lru_ba", "lru_wx", "lru_bx",
             "lru_lambda", "w_branch_ret", "w_branch_lru", "w_out", "ffn_norm_w", "w_ffn_in", "w_ffn_out",
             "final_norm_w"]
    out = [loss, grad_x[None]]
    for kind in range(4):
        out += [res[name][kind] for name in order]
    return tuple(out)
```

```python
import functools

import numpy as np
import jax
import jax.numpy as jnp
from jax import lax
from jax.experimental import pallas as pl
from jax.experimental.pallas import tpu as pltpu

F32 = jnp.float32
BF16 = jnp.bfloat16

D_MODEL = 1024
N_META = 16
CHUNK = 128
PAD_ROWS = CHUNK - N_META
HEADS = 8
HEAD_DIM = 128
ROPE_BASE = 10000.0
QK_SCALE = HEAD_DIM ** -0.5
LRU_BLOCKS = 4
LRU_BLOCK = 256
LRU_C = 8.0
FFN_HIDDEN = 2816
N_DEV = 8
FFN_SHARD = 2 * FFN_HIDDEN // N_DEV
FFN_GROUP = 768
FFN_GROUPS = 4
FFN_OUT_SHARD = FFN_HIDDEN // N_DEV
NORM_EPS = 1e-6

ADAM_LR = 0.001
ADAM_B1 = 0.9
ADAM_B2 = 0.999
ADAM_EPS = 1e-08
ADAM_WD = 0.01
ADAM_STEP = 10

VMEM_LIMIT = 56 * 1024 * 1024
MESH_ID = pl.DeviceIdType.MESH
ANY = pl.BlockSpec(memory_space=pl.ANY)


def _cparams(sem):
    return pltpu.CompilerParams(dimension_semantics=sem, vmem_limit_bytes=VMEM_LIMIT)


def _tile(rows, cap):
    t = cap - cap % 64
    while rows % t:
        t -= 64
    return t


def _dot(a, b):
    return jnp.dot(a, b, preferred_element_type=F32)


def _dot_nt(a, b):
    return lax.dot_general(a, b, (((1,), (1,)), ((), ())), preferred_element_type=F32)


def _dot_tn(a, b):
    return lax.dot_general(a, b, (((0,), (0,)), ((), ())), preferred_element_type=F32)


def _sigmoid(x):
    return 1.0 / (1.0 + jnp.exp(-x))


def _gelu_parts(x):
    k = 0.7978845608028654
    inner = k * (x + 0.044715 * x * x * x)
    t = jnp.tanh(inner)
    g = 0.5 * x * (1.0 + t)
    dg = 0.5 * (1.0 + t) + 0.5 * x * (1.0 - t * t) * k * (1.0 + 3.0 * 0.044715 * x * x)
    return g, dg


def _rot(x, cos2, sin2):
    return x * cos2 + pltpu.roll(x, HEAD_DIM // 2, 1) * sin2


def _rot_t(dx, cos2, sin2):
    return dx * cos2 - pltpu.roll(dx, HEAD_DIM // 2, 1) * sin2


def _rms_bwd(x, w, dy):
    rs = lax.rsqrt(jnp.mean(x * x, axis=-1, keepdims=True) + NORM_EPS)
    nh = x * rs
    dw = jnp.sum(dy * nh, axis=0, keepdims=True)
    dn = dy * w
    dx = rs * (dn - nh * jnp.mean(dn * nh, axis=-1, keepdims=True))
    return dx, dw


def _retention_consts():
    h = jnp.arange(HEADS, dtype=F32)
    log_g = jnp.log(1.0 - 2.0 ** (-5.0 - h))
    idx = jnp.arange(CHUNK, dtype=F32)
    diff = idx[:, None] - idx[None, :]
    intra = jnp.where(diff[None] >= 0, jnp.exp(jnp.maximum(diff, 0.0)[None] * log_g[:, None, None]), 0.0)
    q_decay = jnp.exp((idx + 1.0)[:, None] * log_g[None, :])
    k_decay = jnp.exp((CHUNK - 1.0 - idx)[:, None] * log_g[None, :])
    chunk_decay = jnp.exp(CHUNK * log_g)
    shape = (HEADS, CHUNK, CHUNK)
    qd = jnp.broadcast_to(q_decay.T[:, :, None], shape)
    kd = jnp.broadcast_to(k_decay.T[:, :, None], shape)
    cd = jnp.broadcast_to(chunk_decay[:, None, None], shape)
    return jnp.stack([intra, qd, kd, cd])


def _rope_tables(rows):
    pos = jnp.maximum(jnp.arange(rows) - PAD_ROWS, 0).astype(F32)
    inv_freq = ROPE_BASE ** (-jnp.arange(0, HEAD_DIM, 2, dtype=F32) / HEAD_DIM)
    ang = pos[:, None] * inv_freq[None, :]
    cos, sin = jnp.cos(ang), jnp.sin(ang)
    return jnp.concatenate([cos, cos], axis=1), jnp.concatenate([-sin, sin], axis=1)


def _my_index():
    return 4 * lax.axis_index("x") + 2 * lax.axis_index("y") + lax.axis_index("c")


def _peer(k):
    x, y, c = lax.axis_index("x"), lax.axis_index("y"), lax.axis_index("c")
    px = 1 - x if k & 4 else x
    py = 1 - y if k & 2 else y
    pc = 1 - c if k & 1 else c
    return (px, py, pc), 4 * px + 2 * py + pc


def _push_sems(n_arr):
    n_rem = (N_DEV - 1) * n_arr
    return [pltpu.SemaphoreType.DMA((n_rem,)), pltpu.SemaphoreType.DMA((n_rem,)), pltpu.SemaphoreType.DMA((n_arr,))]


class _Push:
    def __init__(self, send_part, land_slot, sems, n_arr):
        self.send_part, self.land_slot, self.n_arr = send_part, land_slot, n_arr
        self.send_sems, self.recv_sems, self.loc_sems = sems

    def _remote(self, k, a, src, dst, pos):
        idx = (k - 1) * self.n_arr + a
        return pltpu.make_async_remote_copy(src_ref=src, dst_ref=dst, send_sem=self.send_sems.at[idx],
                                            recv_sem=self.recv_sems.at[idx], device_id=pos, device_id_type=MESH_ID)

    def _outgoing(self):
        me = _my_index()
        land = self.land_slot(me)
        remote = []
        for k in range(1, N_DEV):
            pos, p = _peer(k)
            src = self.send_part(p)
            remote += [self._remote(k, a, src[a], land[a], pos) for a in range(self.n_arr)]
        own = self.send_part(me)
        local = [pltpu.make_async_copy(own[a], land[a], self.loc_sems.at[a]) for a in range(self.n_arr)]
        return remote, local

    def start(self):
        remote, local = self._outgoing()
        for cp in remote + local:
            cp.start()

    def wait(self):
        own = self.send_part(_my_index())
        for k in range(1, N_DEV):
            pos, p = _peer(k)
            land = self.land_slot(p)
            for a in range(self.n_arr):
                self._remote(k, a, own[a], land[a], pos).wait_recv()
        remote, local = self._outgoing()
        for cp in remote:
            cp.wait_send()
        for cp in local:
            cp.wait()


def _push_call(name, arrays, out_shapes, send_part, land_slot):
    n_arr = len(arrays)

    def body(*refs):
        ins, outs, sems = refs[:n_arr], refs[n_arr:2 * n_arr], refs[2 * n_arr:]
        push = _Push(lambda p: send_part(ins, p), lambda s: land_slot(outs, s), sems, n_arr)
        push.start()
        push.wait()

    return pl.pallas_call(
        body, name=name, in_specs=[ANY] * n_arr, out_specs=[ANY] * n_arr, out_shape=out_shapes,
        scratch_shapes=_push_sems(n_arr), compiler_params=pltpu.CompilerParams(has_side_effects=True),
    )(*arrays)


LRU_ROWS = LRU_BLOCK // N_DEV
FFN_PAD_ROWS = FFN_GROUP - 2 * FFN_OUT_SHARD


def _half_rows(d):
    return pl.ds(pl.multiple_of((d % 2) * FFN_OUT_SHARD, 16), FFN_OUT_SHARD)


def _lru_rows(d):
    return pl.ds(pl.multiple_of(d * LRU_ROWS, 16), LRU_ROWS)


def _rest_slots(outs, d):
    return [outs[0].at[d], outs[1].at[d], outs[2].at[d], outs[3].at[d],
            outs[4].at[d // 2, _half_rows(d), :], outs[5].at[:, _lru_rows(d), :], outs[6].at[:, _lru_rows(d), :]]


REST_SHAPES = [
    jax.ShapeDtypeStruct((N_DEV, D_MODEL // N_DEV, D_MODEL), BF16),
    jax.ShapeDtypeStruct((N_DEV, D_MODEL // N_DEV, D_MODEL), BF16),
    jax.ShapeDtypeStruct((N_DEV, D_MODEL // N_DEV, D_MODEL), BF16),
    jax.ShapeDtypeStruct((N_DEV, D_MODEL, FFN_GROUP), BF16),
    jax.ShapeDtypeStruct((FFN_GROUPS, FFN_GROUP, D_MODEL), BF16),
    jax.ShapeDtypeStruct((LRU_BLOCKS, LRU_BLOCK, LRU_BLOCK), BF16),
    jax.ShapeDtypeStruct((LRU_BLOCKS, LRU_BLOCK, LRU_BLOCK), BF16),
]


def _in_proj(h0, norm_w, win_g, rest_shards):
    rows = h0.shape[0]
    tm = _tile(rows, 1664)
    n_i = rows // tm
    n_rest = len(rest_shards)

    def body(h_ref, nw_ref, w_ref, *refs):
        shard_refs = refs[:n_rest]
        proj_ref, u_ref = refs[n_rest:n_rest + 2]
        gath_refs = refs[n_rest + 2:2 * n_rest + 2]
        u_s, send_sems, recv_sems, loc_sems, zero_sems, zbuf = refs[2 * n_rest + 2:]
        i, j = pl.program_id(0), pl.program_id(1)
        push = _Push(lambda p: list(shard_refs), lambda s: _rest_slots(gath_refs, s),
                     (send_sems, recv_sems, loc_sems), n_rest)
        zero_fill = [pltpu.make_async_copy(zbuf, gath_refs[4].at[g, pl.ds(2 * FFN_OUT_SHARD, FFN_PAD_ROWS), :],
                                           zero_sems.at[g]) for g in range(FFN_GROUPS)]

        @pl.when(jnp.logical_and(i == 0, j == 0))
        def _():
            push.start()
            zbuf[...] = jnp.zeros_like(zbuf)
            for cp in zero_fill:
                cp.start()

        @pl.when(j == 0)
        def _():
            x = h_ref[...]
            rs = lax.rsqrt(jnp.mean(x * x, axis=-1, keepdims=True) + NORM_EPS)
            u = (x * rs * nw_ref[...]).astype(BF16)
            u_s[...] = u
            u_ref[...] = u
        proj_ref[...] = _dot(u_s[...], w_ref[...]).astype(BF16)

        @pl.when(jnp.logical_and(i == n_i - 1, j == N_DEV - 1))
        def _():
            push.wait()
            for cp in zero_fill:
                cp.wait()

    return pl.pallas_call(
        body, name="in_proj", grid=(n_i, N_DEV),
        in_specs=[pl.BlockSpec((tm, D_MODEL), lambda i, j: (i, 0)),
                  pl.BlockSpec((1, D_MODEL), lambda i, j: (0, 0)),
                  pl.BlockSpec((None, D_MODEL, D_MODEL), lambda i, j: (j, 0, 0))] + [ANY] * n_rest,
        out_specs=[pl.BlockSpec((tm, D_MODEL), lambda i, j: (i, j)),
                   pl.BlockSpec((tm, D_MODEL), lambda i, j: (i, 0))] + [ANY] * n_rest,
        out_shape=[jax.ShapeDtypeStruct((rows, N_DEV * D_MODEL), BF16),
                   jax.ShapeDtypeStruct((rows, D_MODEL), BF16)] + REST_SHAPES,
        scratch_shapes=[pltpu.VMEM((tm, D_MODEL), BF16)] + _push_sems(n_rest)
        + [pltpu.SemaphoreType.DMA((FFN_GROUPS,)), pltpu.VMEM((FFN_PAD_ROWS, D_MODEL), BF16)],
        compiler_params=pltpu.CompilerParams(dimension_semantics=("arbitrary", "arbitrary"),
                                             vmem_limit_bytes=VMEM_LIMIT, has_side_effects=True),
    )(h0, norm_w, win_g, *rest_shards)


def _seg_spec(rows_per_block, seg):
    return pl.BlockSpec((rows_per_block, D_MODEL), lambda n, seg=seg: (n, seg))


def _retention_fwd(proj, cos2, sin2, dec):
    rows = proj.shape[0]
    n_chunks = rows // CHUNK

    def body(q_ref, k_ref, v_ref, g_ref, c_ref, s_ref, dec_ref, o_ref, zr_ref, st_ref, state):
        @pl.when(pl.program_id(0) == 0)
        def _():
            state[...] = jnp.zeros_like(state)
        cos_t, sin_t = c_ref[...], s_ref[...]
        for h in range(HEADS):
            sl = slice(HEAD_DIM * h, HEAD_DIM * (h + 1))
            qh = _rot(q_ref[:, sl].astype(F32), cos_t, sin_t)
            kh = _rot(k_ref[:, sl].astype(F32), cos_t, sin_t) * QK_SCALE
            qb, kb, vb = qh.astype(BF16), kh.astype(BF16), v_ref[:, sl]
            s = _dot_nt(qb, kb) * dec_ref[0, h]
            st = state[h]
            st_ref[h] = st
            o = _dot(s.astype(BF16), vb) + _dot(qb, st.astype(BF16)) * dec_ref[1, h]
            state[h] = st * dec_ref[3, h] + _dot_tn((kh * dec_ref[2, h]).astype(BF16), vb)
            o_ref[:, sl] = o.astype(BF16)
            r = lax.rsqrt(jnp.mean(o * o, axis=-1, keepdims=True) + NORM_EPS)
            g = g_ref[:, sl].astype(F32)
            zr_ref[:, sl] = (g * _sigmoid(g) * (o * r)).astype(BF16)

    tab = pl.BlockSpec((CHUNK, HEAD_DIM), lambda n: (n, 0))
    return pl.pallas_call(
        body, name="retention_fwd", grid=(n_chunks,),
        in_specs=[_seg_spec(CHUNK, 0), _seg_spec(CHUNK, 1), _seg_spec(CHUNK, 2), _seg_spec(CHUNK, 3), tab, tab,
                  pl.BlockSpec((4, HEADS, CHUNK, CHUNK), lambda n: (0, 0, 0, 0))],
        out_specs=[pl.BlockSpec((CHUNK, D_MODEL), lambda n: (n, 0)),
                   pl.BlockSpec((CHUNK, D_MODEL), lambda n: (n, 0)),
                   pl.BlockSpec((None, HEADS, HEAD_DIM, HEAD_DIM), lambda n: (n, 0, 0, 0))],
        out_shape=[jax.ShapeDtypeStruct((rows, D_MODEL), BF16),
                   jax.ShapeDtypeStruct((rows, D_MODEL), BF16),
                   jax.ShapeDtypeStruct((n_chunks, HEADS, HEAD_DIM, HEAD_DIM), F32)],
        scratch_shapes=[pltpu.VMEM((HEADS, HEAD_DIM, HEAD_DIM), F32)],
        compiler_params=_cparams(("arbitrary",)),
    )(proj, proj, proj, proj, cos2, sin2, dec)


def _lru_gates(c, ba, bx, lam, wa_ref, wx_ref):
    pre_r, pre_i = [], []
    for g in range(LRU_BLOCKS):
        cg = c[:, LRU_BLOCK * g:LRU_BLOCK * (g + 1)].astype(BF16)
        pre_r.append(_dot(cg, wa_ref[g]))
        pre_i.append(_dot(cg, wx_ref[g]))
    r = _sigmoid(jnp.concatenate(pre_r, axis=1) + ba)
    ig = _sigmoid(jnp.concatenate(pre_i, axis=1) + bx)
    sp = jnp.maximum(-lam, 0.0) + jnp.log(1.0 + jnp.exp(-jnp.abs(lam)))
    log_a = -LRU_C * r * sp
    a = jnp.exp(log_a)
    mult = jnp.sqrt(-jnp.tanh(log_a) * (a * a + 1.0))
    return r, ig, a, mult, sp


def _conv_taps(xbuf, tm, cw_ref, cb_ref):
    c = cb_ref[...] + cw_ref[3:4, :] * xbuf[8:8 + tm, :]
    for back in (1, 2, 3):
        c = c + cw_ref[3 - back:4 - back, :] * xbuf[8 - back:8 - back + tm, :]
    return c


def _lru_fwd(proj, conv_w, conv_b, ba, bx, lam, wa_g, wx_g):
    rows = proj.shape[0]
    tm = _tile(rows, 320)

    def body(x_ref, gt_ref, cw_ref, cb_ref, ba_ref, bx_ref, lam_ref, wa_ref, wx_ref, hs_ref, zl_ref,
             xbuf, abuf, ubuf, hcar):
        i = pl.program_id(0)

        @pl.when(i == 0)
        def _():
            xbuf[0:8, :] = jnp.zeros((8, D_MODEL), F32)
            hcar[...] = jnp.zeros_like(hcar)

        xbuf[8:8 + tm, :] = x_ref[...].astype(F32)
        c = _conv_taps(xbuf, tm, cw_ref, cb_ref)
        xbuf[0:8, :] = xbuf[tm:tm + 8, :]
        r, ig, a, mult, _ = _lru_gates(c, ba_ref[...], bx_ref[...], lam_ref[...], wa_ref, wx_ref)
        row = i * tm + lax.broadcasted_iota(jnp.int32, (tm, 1), 0)
        abuf[...] = a
        ubuf[...] = jnp.where(row >= PAD_ROWS, mult * (ig * c), 0.0)

        sub = lax.broadcasted_iota(jnp.int32, (8, D_MODEL), 0)

        def block(b, carry):
            off = pl.multiple_of(b * 8, 8)
            av, uv = abuf[pl.ds(off, 8), :], ubuf[pl.ds(off, 8), :]
            for s in (1, 2, 4):
                us = jnp.where(sub >= s, pltpu.roll(uv, s, 0), 0.0)
                as_ = jnp.where(sub >= s, pltpu.roll(av, s, 0), 1.0)
                uv = uv + av * us
                av = av * as_
            hv = uv + av * carry
            ubuf[pl.ds(off, 8), :] = hv
            return hv[7:8, :]

        hcar[...] = lax.fori_loop(0, tm // 8, block, hcar[...])
        gl, _ = _gelu_parts(gt_ref[...].astype(F32))
        hs = ubuf[...]
        hs_ref[...] = hs.astype(BF16)
        zl_ref[...] = (gl * hs).astype(BF16)

    vec = pl.BlockSpec((1, D_MODEL), lambda i: (0, 0))
    mat = pl.BlockSpec((LRU_BLOCKS, LRU_BLOCK, LRU_BLOCK), lambda i: (0, 0, 0))
    return pl.pallas_call(
        body, name="lru_fwd", grid=(rows // tm,),
        in_specs=[_seg_spec(tm, 4), _seg_spec(tm, 5), pl.BlockSpec((4, D_MODEL), lambda i: (0, 0)),
                  vec, vec, vec, vec, mat, mat],
        out_specs=[pl.BlockSpec((tm, D_MODEL), lambda i: (i, 0)), pl.BlockSpec((tm, D_MODEL), lambda i: (i, 0))],
        out_shape=[jax.ShapeDtypeStruct((rows, D_MODEL), BF16), jax.ShapeDtypeStruct((rows, D_MODEL), BF16)],
        scratch_shapes=[pltpu.VMEM((tm + 8, D_MODEL), F32), pltpu.VMEM((tm, D_MODEL), F32),
                        pltpu.VMEM((tm, D_MODEL), F32), pltpu.VMEM((1, D_MODEL), F32)],
        compiler_params=_cparams(("arbitrary",)),
    )(proj, proj, conv_w, conv_b, ba, bx, lam, wa_g, wx_g)


def _mix_fwd(zr, zl, proj, h0, wbr, wbl, wout):
    rows = h0.shape[0]
    tm = _tile(rows, 640)

    def body(zr_ref, zl_ref, ga_ref, gb_ref, h0_ref, wbr_ref, wbl_ref, wo_ref, h1_ref, yr_ref, yl_ref, mx_ref):
        yr = _dot(zr_ref[...], wbr_ref[...])
        yl = _dot(zl_ref[...], wbl_ref[...])
        mixed = (_sigmoid(ga_ref[...].astype(F32)) * yr + _sigmoid(gb_ref[...].astype(F32)) * yl).astype(BF16)
        yr_ref[...] = yr.astype(BF16)
        yl_ref[...] = yl.astype(BF16)
        mx_ref[...] = mixed
        h1_ref[...] = h0_ref[...] + _dot(mixed, wo_ref[...])

    row = pl.BlockSpec((tm, D_MODEL), lambda i: (i, 0))
    wsp = pl.BlockSpec((D_MODEL, D_MODEL), lambda i: (0, 0))
    return pl.pallas_call(
        body, name="mix_fwd", grid=(rows // tm,),
        in_specs=[row, row, _seg_spec(tm, 6), _seg_spec(tm, 7), row, wsp, wsp, wsp],
        out_specs=[row, row, row, row],
        out_shape=[jax.ShapeDtypeStruct((rows, D_MODEL), F32)] + [jax.ShapeDtypeStruct((rows, D_MODEL), BF16)] * 3,
        compiler_params=_cparams(("parallel",)),
    )(zr, zl, proj, proj, h0, wbr, wbl, wout)


def _ffn_fwd_loss(h1, norm_w, wfi_g, wfo_g, final_w, target):
    rows = h1.shape[0]
    tm = _tile(rows, 640)
    last = FFN_GROUPS - 1

    def body(h1_ref, nw_ref, wg_ref, wu_ref, wo_ref, fw_ref, t_ref,
             u2_ref, g_ref, up_ref, act_ref, dh2_ref, red_ref, u2_s, acc):
        i, d = pl.program_id(0), pl.program_id(1)

        @pl.when(jnp.logical_and(i == 0, d == 0))
        def _():
            red_ref[...] = jnp.zeros_like(red_ref)

        @pl.when(d == 0)
        def _():
            x = h1_ref[...]
            rs = lax.rsqrt(jnp.mean(x * x, axis=-1, keepdims=True) + NORM_EPS)
            u2 = (x * rs * nw_ref[...]).astype(BF16)
            u2_s[...] = u2
            u2_ref[...] = u2
            acc[...] = jnp.zeros_like(acc)

        g = _dot(u2_s[...], wg_ref[...])
        up = _dot(u2_s[...], wu_ref[...])
        act = (g * _sigmoid(g) * up).astype(BF16)
        g_ref[...] = g.astype(BF16)
        up_ref[...] = up.astype(BF16)
        act_ref[...] = act
        acc[...] += _dot(act, wo_ref[...])

        @pl.when(d == last)
        def _():
            h2 = h1_ref[...] + acc[...]
            rs = lax.rsqrt(jnp.mean(h2 * h2, axis=-1, keepdims=True) + NORM_EPS)
            nh = h2 * rs
            fw = fw_ref[...]
            row = i * tm + lax.broadcasted_iota(jnp.int32, (tm, 1), 0)
            diff = jnp.where(row >= CHUNK, nh * fw - t_ref[...], 0.0)
            dy = diff * (1.0 / D_MODEL)
            red_ref[0:1, :] += jnp.sum(diff * diff, axis=0, keepdims=True)
            red_ref[1:2, :] += jnp.sum(dy * nh, axis=0, keepdims=True)
            dn = dy * fw
            dh2_ref[...] = rs * (dn - nh * jnp.mean(dn * nh, axis=-1, keepdims=True))

    row = pl.BlockSpec((tm, D_MODEL), lambda i, d: (i, 0))
    vec = pl.BlockSpec((1, D_MODEL), lambda i, d: (0, 0))
    hid = pl.BlockSpec((tm, FFN_GROUP), lambda i, d: (i, d))
    hid_shape = jax.ShapeDtypeStruct((rows, FFN_GROUPS * FFN_GROUP), BF16)
    return pl.pallas_call(
        body, name="ffn_fwd_loss", grid=(rows // tm, FFN_GROUPS),
        in_specs=[row, vec,
                  pl.BlockSpec((None, D_MODEL, FFN_GROUP), lambda i, d: (d, 0, 0)),
                  pl.BlockSpec((None, D_MODEL, FFN_GROUP), lambda i, d: (d + FFN_GROUPS, 0, 0)),
                  pl.BlockSpec((None, FFN_GROUP, D_MODEL), lambda i, d: (d, 0, 0)),
                  vec, row],
        out_specs=[row, hid, hid, hid, row, pl.BlockSpec((8, D_MODEL), lambda i, d: (0, 0))],
        out_shape=[jax.ShapeDtypeStruct((rows, D_MODEL), BF16), hid_shape, hid_shape, hid_shape,
                   jax.ShapeDtypeStruct((rows, D_MODEL), F32), jax.ShapeDtypeStruct((8, D_MODEL), F32)],
        scratch_shapes=[pltpu.VMEM((tm, D_MODEL), BF16), pltpu.VMEM((tm, D_MODEL), F32)],
        compiler_params=_cparams(("arbitrary", "arbitrary")),
    )(h1, norm_w, wfi_g, wfi_g, wfo_g, final_w, target)


def _wgrad(a, b, ka, tn, out_dtype, b_halves=False):
    rows = a.shape[0]
    na = a.shape[1] // ka
    tm = _tile(rows, 640)
    nm = rows // tm
    if b_halves:
        per_half = b.shape[2] // tn
        nb = 2 * per_half
        b_spec = pl.BlockSpec((None, tm, tn), lambda p, q, m: (q // per_half, m, q % per_half))
    else:
        nb = b.shape[1] // tn
        b_spec = pl.BlockSpec((tm, tn), lambda p, q, m: (m, q))

    def body(a_ref, b_ref, o_ref, acc):
        m = pl.program_id(2)

        @pl.when(m == 0)
        def _():
            acc[...] = jnp.zeros_like(acc)

        acc[...] += _dot_tn(a_ref[...].astype(BF16), b_ref[...].astype(BF16))

        @pl.when(m == nm - 1)
        def _():
            o_ref[...] = acc[...].astype(out_dtype)

    return pl.pallas_call(
        body, name="wgrad", grid=(na, nb, nm),
        in_specs=[pl.BlockSpec((tm, ka), lambda p, q, m: (m, p)), b_spec],
        out_specs=pl.BlockSpec((None, None, ka, tn), lambda p, q, m: (p, q, 0, 0)),
        out_shape=jax.ShapeDtypeStruct((na, nb, ka, tn), out_dtype),
        scratch_shapes=[pltpu.VMEM((ka, tn), F32)],
        compiler_params=_cparams(("parallel", "parallel", "arbitrary")),
    )(a, b)


def _ffn_bwd(dh2, g, up, h1, norm_w, wfi_g, wfo_g):
    rows = h1.shape[0]
    tm = _tile(rows, 640)
    last = FFN_GROUPS - 1

    def body(dh2_ref, g_ref, up_ref, h1_ref, nw_ref, wg_ref, wu_ref, wo_ref, dgu_ref, dh1_ref, dw_ref,
             dh2_s, acc):
        i, d = pl.program_id(0), pl.program_id(1)

        @pl.when(jnp.logical_and(i == 0, d == 0))
        def _():
            dw_ref[...] = jnp.zeros_like(dw_ref)

        @pl.when(d == 0)
        def _():
            dh2_s[...] = dh2_ref[...].astype(BF16)
            acc[...] = jnp.zeros_like(acc)

        dact = _dot_nt(dh2_s[...], wo_ref[...])
        gv, uv = g_ref[...].astype(F32), up_ref[...].astype(F32)
        sg = _sigmoid(gv)
        dg = (dact * uv * (sg * (1.0 + gv * (1.0 - sg)))).astype(BF16)
        dup = (dact * (gv * sg)).astype(BF16)
        dgu_ref[0] = dg
        dgu_ref[1] = dup
        acc[...] += _dot_nt(dg, wg_ref[...]) + _dot_nt(dup, wu_ref[...])

        @pl.when(d == last)
        def _():
            dx, dw = _rms_bwd(h1_ref[...], nw_ref[...], acc[...])
            dw_ref[0:1, :] += dw
            dh1_ref[...] = dh2_ref[...] + dx

    row = pl.BlockSpec((tm, D_MODEL), lambda i, d: (i, 0))
    vec = pl.BlockSpec((1, D_MODEL), lambda i, d: (0, 0))
    hid = pl.BlockSpec((tm, FFN_GROUP), lambda i, d: (i, d))
    return pl.pallas_call(
        body, name="ffn_bwd", grid=(rows // tm, FFN_GROUPS),
        in_specs=[row, hid, hid, row, vec,
                  pl.BlockSpec((None, D_MODEL, FFN_GROUP), lambda i, d: (d, 0, 0)),
                  pl.BlockSpec((None, D_MODEL, FFN_GROUP), lambda i, d: (d + FFN_GROUPS, 0, 0)),
                  pl.BlockSpec((None, FFN_GROUP, D_MODEL), lambda i, d: (d, 0, 0))],
        out_specs=[pl.BlockSpec((2, tm, FFN_GROUP), lambda i, d: (0, i, d)), row,
                   pl.BlockSpec((8, D_MODEL), lambda i, d: (0, 0))],
        out_shape=[jax.ShapeDtypeStruct((2, rows, FFN_GROUPS * FFN_GROUP), BF16),
                   jax.ShapeDtypeStruct((rows, D_MODEL), F32), jax.ShapeDtypeStruct((8, D_MODEL), F32)],
        scratch_shapes=[pltpu.VMEM((tm, D_MODEL), BF16), pltpu.VMEM((tm, D_MODEL), F32)],
        compiler_params=_cparams(("arbitrary", "arbitrary")),
    )(dh2, g, up, h1, norm_w, wfi_g, wfi_g, wfo_g)


def _mix_bwd(dh1, yr, yl, proj, wbr, wbl, wout):
    rows = dh1.shape[0]
    tm = _tile(rows, 640)

    def body(dh1_ref, yr_ref, yl_ref, ga_ref, gb_ref, wbr_ref, wbl_ref, wo_ref,
             dyr_ref, dyl_ref, dseg_ref, dzr_ref, dzl_ref):
        dmix = _dot_nt(dh1_ref[...].astype(BF16), wo_ref[...])
        sa, sb = _sigmoid(ga_ref[...].astype(F32)), _sigmoid(gb_ref[...].astype(F32))
        dyr = (dmix * sa).astype(BF16)
        dyl = (dmix * sb).astype(BF16)
        dyr_ref[...] = dyr
        dyl_ref[...] = dyl
        dseg_ref[:, 0:D_MODEL] = (dmix * yr_ref[...].astype(F32) * (sa * (1.0 - sa))).astype(BF16)
        dseg_ref[:, D_MODEL:2 * D_MODEL] = (dmix * yl_ref[...].astype(F32) * (sb * (1.0 - sb))).astype(BF16)
        dzr_ref[...] = _dot_nt(dyr, wbr_ref[...]).astype(BF16)
        dzl_ref[...] = _dot_nt(dyl, wbl_ref[...]).astype(BF16)

    row = pl.BlockSpec((tm, D_MODEL), lambda i: (i, 0))
    wsp = pl.BlockSpec((D_MODEL, D_MODEL), lambda i: (0, 0))
    bshape = jax.ShapeDtypeStruct((rows, D_MODEL), BF16)
    return pl.pallas_call(
        body, name="mix_bwd", grid=(rows // tm,),
        in_specs=[row, row, row, _seg_spec(tm, 6), _seg_spec(tm, 7), wsp, wsp, wsp],
        out_specs=[row, row, pl.BlockSpec((tm, 2 * D_MODEL), lambda i: (i, 3)), row, row],
        out_shape=[bshape, bshape, jax.ShapeDtypeStruct((rows, N_DEV * D_MODEL), BF16), bshape, bshape],
        compiler_params=_cparams(("parallel",)),
    )(dh1, yr, yl, proj, proj, wbr, wbl, wout)


S1_SHAPES = [
    jax.ShapeDtypeStruct((N_DEV, D_MODEL // N_DEV, D_MODEL), BF16),
    jax.ShapeDtypeStruct((N_DEV, D_MODEL // N_DEV, D_MODEL), BF16),
    jax.ShapeDtypeStruct((N_DEV, D_MODEL // N_DEV, D_MODEL), BF16),
    jax.ShapeDtypeStruct((N_DEV, D_MODEL, FFN_GROUP), BF16),
    jax.ShapeDtypeStruct((N_DEV, FFN_OUT_SHARD, D_MODEL), BF16),
]


def _s1_parts(ins, p):
    return [ins[0].at[p], ins[1].at[p], ins[2].at[p], ins[3].at[p], ins[4].at[p // 2, _half_rows(p), :]]


def _lru_bwd(dzl, hs, proj, dproj, conv_w, conv_b, ba, bx, lam, wa_g, wx_g, s1_grads):
    rows = dzl.shape[0]
    tm = _tile(rows, 320)
    nt = rows // tm
    t8 = tm // 8
    n_s1 = len(s1_grads)

    def body(dzl_ref, hs_ref, hsp_ref, x_ref, xp_ref, gt_ref, cw_ref, cb_ref, ba_ref, bx_ref, lam_ref,
             wa_ref, wx_ref, dproj_in, *refs):
        del dproj_in
        s1_refs = refs[:n_s1]
        dseg_ref, dwa_ref, dwx_ref, sm_ref = refs[n_s1:n_s1 + 4]
        land_refs = refs[n_s1 + 4:2 * n_s1 + 4]
        xbuf, abuf, bbuf, dbuf, dcbuf, anext, dhcar, send_sems, recv_sems, loc_sems = refs[2 * n_s1 + 4:]
        step = pl.program_id(0)
        i = nt - 1 - step
        push = _Push(lambda p: _s1_parts(s1_refs, p), lambda s: [r.at[s] for r in land_refs],
                     (send_sems, recv_sems, loc_sems), n_s1)

        @pl.when(step == 0)
        def _():
            push.start()
            dwa_ref[...] = jnp.zeros_like(dwa_ref)
            dwx_ref[...] = jnp.zeros_like(dwx_ref)
            sm_ref[...] = jnp.zeros_like(sm_ref)
            anext[...] = jnp.zeros_like(anext)
            dhcar[...] = jnp.zeros_like(dhcar)
            dcbuf[tm:tm + 8, :] = jnp.zeros((8, D_MODEL), F32)

        first = i == 0
        x_prev = jnp.where(first, 0.0, xp_ref[8:16, :].astype(F32))
        x_v = x_ref[...].astype(F32)
        xbuf[0:8, :] = x_prev
        xbuf[8:8 + tm, :] = x_v
        c = _conv_taps(xbuf, tm, cw_ref, cb_ref)
        lam_v = lam_ref[...]
        r, ig, a, mult, sp = _lru_gates(c, ba_ref[...], bx_ref[...], lam_v, wa_ref, wx_ref)
        hs_v = hs_ref[...].astype(F32)
        gl, dgl = _gelu_parts(gt_ref[...].astype(F32))
        dzl_v = dzl_ref[...].astype(F32)
        dseg_ref[:, D_MODEL:2 * D_MODEL] = (dzl_v * hs_v * dgl).astype(BF16)
        dbuf[...] = dzl_v * gl
        abuf[0:tm, :] = a
        abuf[tm:tm + 8, :] = jnp.broadcast_to(anext[...], (8, D_MODEL))
        bbuf[...] = abuf[1:tm + 1, :]

        sub = lax.broadcasted_iota(jnp.int32, (8, D_MODEL), 0)

        def block(k, carry):
            off = pl.multiple_of((t8 - 1 - k) * 8, 8)
            av = bbuf[pl.ds(off, 8), :]
            uv = dbuf[pl.ds(off, 8), :]
            for s in (1, 2, 4):
                us = jnp.where(sub < 8 - s, pltpu.roll(uv, 8 - s, 0), 0.0)
                as_ = jnp.where(sub < 8 - s, pltpu.roll(av, 8 - s, 0), 1.0)
                uv = uv + av * us
                av = av * as_
            hv = uv + av * carry
            dbuf[pl.ds(off, 8), :] = hv
            return hv[0:1, :]

        dhcar[...] = lax.fori_loop(0, t8, block, dhcar[...])
        anext[...] = abuf[0:1, :]
        dh = dbuf[...]

        xbuf[0:8, :] = jnp.where(first, 0.0, hsp_ref[8:16, :].astype(F32))
        xbuf[8:8 + tm, :] = hs_v
        hprev = xbuf[7:7 + tm, :]
        row = i * tm + lax.broadcasted_iota(jnp.int32, (tm, 1), 0)
        duu = jnp.where(row >= PAD_ROWS, dh, 0.0)
        da = dh * hprev
        dmult = duu * ig * c
        di = duu * mult * c
        dc = duu * mult * ig
        dlog_a = da * a - dmult * (a * a) / mult
        dr = dlog_a * (-LRU_C * sp)
        dsp = jnp.sum(dlog_a * (-LRU_C * r), axis=0, keepdims=True)
        dpr = dr * r * (1.0 - r)
        dpi = di * ig * (1.0 - ig)
        dpr_b, dpi_b = dpr.astype(BF16), dpi.astype(BF16)
        dcs = []
        for g in range(LRU_BLOCKS):
            sl = slice(LRU_BLOCK * g, LRU_BLOCK * (g + 1))
            cg = c[:, sl].astype(BF16)
            dwa_ref[g] += _dot_tn(cg, dpr_b[:, sl])
            dwx_ref[g] += _dot_tn(cg, dpi_b[:, sl])
            dcs.append(_dot_nt(dpr_b[:, sl], wa_ref[g]) + _dot_nt(dpi_b[:, sl], wx_ref[g]))
        dc = dc + jnp.concatenate(dcs, axis=1)

        dcbuf[0:tm, :] = dc
        xbuf[8:8 + tm, :] = x_v
        xbuf[0:8, :] = x_prev
        dlin = cw_ref[3:4, :] * dc
        sm_ref[3:4, :] += jnp.sum(dc * xbuf[8:8 + tm, :], axis=0, keepdims=True)
        for back in (1, 2, 3):
            dlin = dlin + cw_ref[3 - back:4 - back, :] * dcbuf[back:back + tm, :]
            sm_ref[3 - back:4 - back, :] += jnp.sum(dc * xbuf[8 - back:8 - back + tm, :], axis=0, keepdims=True)
        dseg_ref[:, 0:D_MODEL] = dlin.astype(BF16)
        dcbuf[tm:tm + 8, :] = dcbuf[0:8, :]
        sm_ref[4:5, :] += jnp.sum(dc, axis=0, keepdims=True)
        sm_ref[5:6, :] += jnp.sum(dpr, axis=0, keepdims=True)
        sm_ref[6:7, :] += jnp.sum(dpi, axis=0, keepdims=True)
        sm_ref[7:8, :] += dsp * (-_sigmoid(-lam_v))

        @pl.when(step == nt - 1)
        def _():
            push.wait()

    rowb = pl.BlockSpec((tm, D_MODEL), lambda s: (nt - 1 - s, 0))
    t16 = tm // 16
    prev8 = pl.BlockSpec((16, D_MODEL), lambda s: (jnp.maximum((nt - 1 - s) * t16 - 1, 0), 0))
    seg = lambda k: pl.BlockSpec((tm, D_MODEL), lambda s, k=k: (nt - 1 - s, k))
    prev8_seg4 = pl.BlockSpec((16, D_MODEL), lambda s: (jnp.maximum((nt - 1 - s) * t16 - 1, 0), 4))
    vec = pl.BlockSpec((1, D_MODEL), lambda s: (0, 0))
    mat = pl.BlockSpec((LRU_BLOCKS, LRU_BLOCK, LRU_BLOCK), lambda s: (0, 0, 0))
    mshape = jax.ShapeDtypeStruct((LRU_BLOCKS, LRU_BLOCK, LRU_BLOCK), F32)
    n_in = 13
    return pl.pallas_call(
        body, name="lru_bwd", grid=(nt,),
        in_specs=[rowb, rowb, prev8, seg(4), prev8_seg4, seg(5), pl.BlockSpec((4, D_MODEL), lambda s: (0, 0)),
                  vec, vec, vec, vec, mat, mat, ANY] + [ANY] * n_s1,
        out_specs=[pl.BlockSpec((tm, 2 * D_MODEL), lambda s: (nt - 1 - s, 2)), mat, mat,
                   pl.BlockSpec((8, D_MODEL), lambda s: (0, 0))] + [ANY] * n_s1,
        out_shape=[jax.ShapeDtypeStruct(dproj.shape, dproj.dtype), mshape, mshape,
                   jax.ShapeDtypeStruct((8, D_MODEL), F32)] + S1_SHAPES,
        input_output_aliases={n_in: 0},
        scratch_shapes=[pltpu.VMEM((tm + 8, D_MODEL), F32), pltpu.VMEM((tm + 8, D_MODEL), F32),
                        pltpu.VMEM((tm, D_MODEL), F32), pltpu.VMEM((tm, D_MODEL), F32),
                        pltpu.VMEM((tm + 8, D_MODEL), F32),
                        pltpu.VMEM((1, D_MODEL), F32), pltpu.VMEM((1, D_MODEL), F32)] + _push_sems(n_s1),
        compiler_params=pltpu.CompilerParams(dimension_semantics=("arbitrary",), vmem_limit_bytes=VMEM_LIMIT,
                                             has_side_effects=True),
    )(dzl, hs, hs, proj, proj, proj, conv_w, conv_b, ba, bx, lam, wa_g, wx_g, dproj, *s1_grads)


def _retention_bwd(dzr, o, proj, states, cos2, sin2, dec, dproj):
    rows = dzr.shape[0]
    n_chunks = rows // CHUNK

    def body(dzr_ref, o_ref, q_ref, k_ref, v_ref, g_ref, st_ref, c_ref, s_ref, dec_ref, dproj_in,
             dseg_ref, dstate):
        del dproj_in

        @pl.when(pl.program_id(0) == 0)
        def _():
            dstate[...] = jnp.zeros_like(dstate)
        cos_t, sin_t = c_ref[...], s_ref[...]
        for h in range(HEADS):
            sl = slice(HEAD_DIM * h, HEAD_DIM * (h + 1))
            o = o_ref[:, sl].astype(F32)
            g = g_ref[:, sl].astype(F32)
            dzr_v = dzr_ref[:, sl].astype(F32)
            sg = _sigmoid(g)
            r = lax.rsqrt(jnp.mean(o * o, axis=-1, keepdims=True) + NORM_EPS)
            on = o * r
            dseg_ref[:, 3 * D_MODEL + HEAD_DIM * h:3 * D_MODEL + HEAD_DIM * (h + 1)] = (
                dzr_v * on * (sg * (1.0 + g * (1.0 - sg)))).astype(BF16)
            don = dzr_v * (g * sg)
            do = r * (don - on * jnp.mean(don * on, axis=-1, keepdims=True))
            dob = do.astype(BF16)

            qh = _rot(q_ref[:, sl].astype(F32), cos_t, sin_t)
            kh = _rot(k_ref[:, sl].astype(F32), cos_t, sin_t) * QK_SCALE
            qb, kb, vb = qh.astype(BF16), kh.astype(BF16), v_ref[:, sl]
            intra, qd, kd, cd = dec_ref[0, h], dec_ref[1, h], dec_ref[2, h], dec_ref[3, h]
            s = (_dot_nt(qb, kb) * intra).astype(BF16)
            ds = (_dot_nt(dob, vb) * intra).astype(BF16)
            st_b = st_ref[h].astype(BF16)
            dst = dstate[h]
            dst_b = dst.astype(BF16)
            dv = _dot_tn(s, dob) + _dot((kh * kd).astype(BF16), dst_b)
            dq = _dot(ds, kb) + _dot_nt(dob, st_b) * qd
            dk = _dot_tn(ds, qb) + _dot_nt(vb, dst_b) * kd
            dstate[h] = dst * cd + _dot_tn((qh * qd).astype(BF16), dob)
            dseg_ref[:, 2 * D_MODEL + HEAD_DIM * h:2 * D_MODEL + HEAD_DIM * (h + 1)] = dv.astype(BF16)
            dseg_ref[:, sl] = _rot_t(dq, cos_t, sin_t).astype(BF16)
            dseg_ref[:, D_MODEL + HEAD_DIM * h:D_MODEL + HEAD_DIM * (h + 1)] = (
                _rot_t(dk, cos_t, sin_t) * QK_SCALE).astype(BF16)

    rev = lambda s: n_chunks - 1 - s
    rowb = pl.BlockSpec((CHUNK, D_MODEL), lambda s: (rev(s), 0))
    seg = lambda k: pl.BlockSpec((CHUNK, D_MODEL), lambda s, k=k: (rev(s), k))
    tab = pl.BlockSpec((CHUNK, HEAD_DIM), lambda s: (rev(s), 0))
    return pl.pallas_call(
        body, name="retention_bwd", grid=(n_chunks,),
        in_specs=[rowb, rowb, seg(0), seg(1), seg(2), seg(3),
                  pl.BlockSpec((None, HEADS, HEAD_DIM, HEAD_DIM), lambda s: (rev(s), 0, 0, 0)), tab, tab,
                  pl.BlockSpec((4, HEADS, CHUNK, CHUNK), lambda s: (0, 0, 0, 0)), ANY],
        out_specs=pl.BlockSpec((CHUNK, 4 * D_MODEL), lambda s: (rev(s), 0)),
        out_shape=jax.ShapeDtypeStruct(dproj.shape, dproj.dtype),
        input_output_aliases={10: 0},
        scratch_shapes=[pltpu.VMEM((HEADS, HEAD_DIM, HEAD_DIM), F32)],
        compiler_params=_cparams(("arbitrary",)),
    )(dzr, o, proj, proj, proj, proj, states, cos2, sin2, dec, dproj)


S2_SHAPES = [
    jax.ShapeDtypeStruct((N_DEV, D_MODEL, D_MODEL), BF16),
    jax.ShapeDtypeStruct((N_DEV, LRU_BLOCKS, LRU_ROWS, LRU_BLOCK), F32),
    jax.ShapeDtypeStruct((N_DEV, LRU_BLOCKS, LRU_ROWS, LRU_BLOCK), F32),
]


def _s2_parts(ins, p):
    rows_p = pl.ds(pl.multiple_of(p * LRU_ROWS, 8), LRU_ROWS)
    return [ins[0].at[p], ins[1].at[:, rows_p, :], ins[2].at[:, rows_p, :]]


def _in_proj_bwd(dproj, win_g, h0, norm_w, dh1, s2_grads):
    rows = h0.shape[0]
    tm = _tile(rows, 640)
    n_i = rows // tm
    n_s2 = len(s2_grads)

    def body(dseg_ref, w_ref, h0_ref, nw_ref, dh1_ref, *refs):
        s2_refs = refs[:n_s2]
        dh0_ref, dw_ref = refs[n_s2:n_s2 + 2]
        land_refs = refs[n_s2 + 2:2 * n_s2 + 2]
        acc, send_sems, recv_sems, loc_sems = refs[2 * n_s2 + 2:]
        i, j = pl.program_id(0), pl.program_id(1)
        push = _Push(lambda p: _s2_parts(s2_refs, p), lambda s: [r.at[s] for r in land_refs],
                     (send_sems, recv_sems, loc_sems), n_s2)

        @pl.when(jnp.logical_and(i == 0, j == 0))
        def _():
            push.start()
            dw_ref[...] = jnp.zeros_like(dw_ref)

        @pl.when(j == 0)
        def _():
            acc[...] = jnp.zeros_like(acc)

        acc[...] += _dot_nt(dseg_ref[...], w_ref[...])

        @pl.when(j == N_DEV - 1)
        def _():
            dx, dw = _rms_bwd(h0_ref[...], nw_ref[...], acc[...])
            dw_ref[0:1, :] += dw
            dh0_ref[...] = dh1_ref[...] + dx

        @pl.when(jnp.logical_and(i == n_i - 1, j == N_DEV - 1))
        def _():
            push.wait()

    row = pl.BlockSpec((tm, D_MODEL), lambda i, j: (i, 0))
    vec = pl.BlockSpec((1, D_MODEL), lambda i, j: (0, 0))
    return pl.pallas_call(
        body, name="in_proj_bwd", grid=(n_i, N_DEV),
        in_specs=[pl.BlockSpec((tm, D_MODEL), lambda i, j: (i, j)),
                  pl.BlockSpec((None, D_MODEL, D_MODEL), lambda i, j: (j, 0, 0)), row, vec, row] + [ANY] * n_s2,
        out_specs=[row, pl.BlockSpec((8, D_MODEL), lambda i, j: (0, 0))] + [ANY] * n_s2,
        out_shape=[jax.ShapeDtypeStruct((rows, D_MODEL), F32), jax.ShapeDtypeStruct((8, D_MODEL), F32)] + S2_SHAPES,
        scratch_shapes=[pltpu.VMEM((tm, D_MODEL), F32)] + _push_sems(n_s2),
        compiler_params=pltpu.CompilerParams(dimension_semantics=("arbitrary", "arbitrary"),
                                             vmem_limit_bytes=VMEM_LIMIT, has_side_effects=True),
    )(dproj, win_g, h0, norm_w, dh1, *s2_grads)


def _adamw(g_slots, w, m, v):
    slots, rows, cols = g_slots.shape
    tr = rows
    for cand in (256, 128, 64, 32, 16, 8):
        if rows % cand == 0 and rows > cand:
            tr = cand
            break

    def body(g_ref, w_ref, m_ref, v_ref, go_ref, d_ref, mo_ref, vo_ref):
        g = g_ref[0].astype(F32)
        for s in range(1, slots):
            g = g + g_ref[s].astype(F32)
        m2 = ADAM_B1 * m_ref[...] + (1.0 - ADAM_B1) * g
        v2 = ADAM_B2 * v_ref[...] + (1.0 - ADAM_B2) * (g * g)
        m_hat = m2 / (1.0 - ADAM_B1 ** ADAM_STEP)
        v_hat = v2 / (1.0 - ADAM_B2 ** ADAM_STEP)
        go_ref[...] = g
        d_ref[...] = -ADAM_LR * (m_hat / (jnp.sqrt(v_hat) + ADAM_EPS) + ADAM_WD * w_ref[...])
        mo_ref[...] = m2
        vo_ref[...] = v2

    blk = pl.BlockSpec((tr, cols), lambda i: (i, 0))
    shape = jax.ShapeDtypeStruct((rows, cols), F32)
    return pl.pallas_call(
        body, name="adamw", grid=(rows // tr,),
        in_specs=[pl.BlockSpec((slots, tr, cols), lambda i: (0, i, 0)), blk, blk, blk],
        out_specs=[blk] * 4, out_shape=[shape] * 4,
        compiler_params=_cparams(("parallel",)),
    )(g_slots, w, m, v)


def _sum_slots(packs):
    slots, rows, cols = packs.shape

    def body(p_ref, o_ref):
        acc = p_ref[0]
        for s in range(1, slots):
            acc = acc + p_ref[s]
        o_ref[...] = acc

    return pl.pallas_call(
        body, name="sum_slots", out_shape=jax.ShapeDtypeStruct((rows, cols), F32),
        compiler_params=pltpu.CompilerParams(vmem_limit_bytes=VMEM_LIMIT),
    )(packs)


def _gather_first(win_shard, small):
    shapes = [jax.ShapeDtypeStruct((N_DEV,) + win_shard.shape, BF16), jax.ShapeDtypeStruct((N_DEV,) + small.shape, F32)]
    return _push_call("gather_first", [win_shard, small], shapes,
                      lambda ins, p: list(ins), lambda outs, s: [r.at[s] for r in outs])


def _share_pack(pack):
    shapes = [jax.ShapeDtypeStruct((N_DEV,) + pack.shape, F32)]
    return _push_call("share_pack", [pack], shapes,
                      lambda ins, p: list(ins), lambda outs, s: [r.at[s] for r in outs])[0]


PACK_MIX_NORM, PACK_CONV_W, PACK_CONV_B, PACK_BA, PACK_BX, PACK_LAM = 0, 8, 12, 13, 14, 15
PACK_FFN_NORM, PACK_SQ_ERR, PACK_FINAL_NORM, PACK_META = 16, 24, 25, 32


def kernel(x, meta_tokens, mix_norm_w, w_in, conv_w, conv_b, lru_wa, lru_ba, lru_wx, lru_bx, lru_lambda, w_branch_ret, w_branch_lru, w_out, ffn_norm_w, w_ffn_in, w_ffn_out, final_norm_w, loss_target, m_meta_tokens, m_mix_norm_w, m_w_in, m_conv_w, m_conv_b, m_lru_wa, m_lru_ba, m_lru_wx, m_lru_bx, m_lru_lambda, m_w_branch_ret, m_w_branch_lru, m_w_out, m_ffn_norm_w, m_w_ffn_in, m_w_ffn_out, m_final_norm_w, v_meta_tokens, v_mix_norm_w, v_w_in, v_conv_w, v_conv_b, v_lru_wa, v_lru_ba, v_lru_wx, v_lru_bx, v_lru_lambda, v_w_branch_ret, v_w_branch_lru, v_w_out, v_ffn_norm_w, v_w_ffn_in, v_w_ffn_out, v_final_norm_w):
    me = _my_index()
    pad4 = ((0, 4), (0, 0))
    fw = final_norm_w.reshape(1, D_MODEL)

    small = jnp.concatenate([meta_tokens, jnp.pad(conv_w[0], pad4)], axis=0)
    win_g, small_g = _gather_first(w_in[0].astype(BF16), small)
    meta_full = small_g[:, :N_META].transpose(1, 0, 2).reshape(N_META, D_MODEL)
    conv_w_full = small_g[:, N_META:N_META + 4].transpose(1, 0, 2).reshape(4, D_MODEL)
    rest_shards = [w_branch_ret[0].astype(BF16), w_branch_lru[0].astype(BF16), w_out[0].astype(BF16),
                   jnp.pad(w_ffn_in[0].astype(BF16), ((0, 0), (0, FFN_GROUP - FFN_SHARD))),
                   w_ffn_out[0].astype(BF16), lru_wa[0].astype(BF16), lru_wx[0].astype(BF16)]

    rows = x.shape[1] + CHUNK
    h0 = jnp.concatenate([jnp.zeros((PAD_ROWS, D_MODEL), F32), meta_full, x[0]], axis=0)
    tgt = jnp.concatenate([jnp.zeros((CHUNK, D_MODEL), F32), loss_target[0]], axis=0)
    cos2, sin2 = _rope_tables(rows)
    dec = _retention_consts()

    proj, u, wbr_g, wbl_g, wout_g, wfi_g, wfo_g, wa_g, wx_g = _in_proj(h0, mix_norm_w, win_g, rest_shards)
    wbr, wbl, wout = (t.reshape(D_MODEL, D_MODEL) for t in (wbr_g, wbl_g, wout_g))
    o, zr, states = _retention_fwd(proj, cos2, sin2, dec)
    hs, zl = _lru_fwd(proj, conv_w_full, conv_b, lru_ba, lru_bx, lru_lambda, wa_g, wx_g)
    h1, yr, yl, mixed = _mix_fwd(zr, zl, proj, h0, wbr, wbl, wout)
    u2, g, up, act, dh2, red = _ffn_fwd_loss(h1, ffn_norm_w, wfi_g, wfo_g, fw, tgt)

    d_wfo = _wgrad(act, dh2, FFN_GROUP, D_MODEL, BF16)[:, 0]
    dgu, dh1, dw_ffn_norm = _ffn_bwd(dh2, g, up, h1, ffn_norm_w, wfi_g, wfo_g)
    d_wfi = _wgrad(u2, dgu, D_MODEL, FFN_GROUP, BF16, b_halves=True)[0]
    d_wout = _wgrad(mixed, dh1, D_MODEL, D_MODEL, BF16)[0, 0]
    dyr, dyl, dproj, dzr, dzl = _mix_bwd(dh1, yr, yl, proj, wbr, wbl, wout)
    d_wbr = _wgrad(zr, dyr, D_MODEL, D_MODEL, BF16)[0, 0]
    d_wbl = _wgrad(zl, dyl, D_MODEL, D_MODEL, BF16)[0, 0]
    s1_grads = [t.reshape(N_DEV, D_MODEL // N_DEV, D_MODEL) for t in (d_wbr, d_wbl, d_wout)] + [d_wfi, d_wfo]
    dproj, d_wa, d_wx, lru_small, r_br, r_bl, r_out, r_fi, r_fo = _lru_bwd(
        dzl, hs, proj, dproj, conv_w_full, conv_b, lru_ba, lru_bx, lru_lambda, wa_g, wx_g, s1_grads)
    dproj = _retention_bwd(dzr, o, proj, states, cos2, sin2, dec, dproj)
    d_win = _wgrad(u, dproj, D_MODEL, D_MODEL, BF16)[0]
    dh0, dw_mix_norm, r_in, r_wa, r_wx = _in_proj_bwd(dproj, win_g, h0, mix_norm_w, dh1, [d_win, d_wa, d_wx])
    grad_x = dh0[CHUNK:]

    pack = jnp.concatenate([dw_mix_norm, lru_small, dw_ffn_norm, red, dh0[PAD_ROWS:CHUNK]], axis=0)
    small_sum = _sum_slots(_share_pack(pack))
    loss = (0.5 / D_MODEL) * jnp.sum(small_sum[PACK_SQ_ERR])

    def big_update(slots, w, m, v):
        shape = w.shape
        w2, m2, v2 = (t.reshape(slots.shape[1:]) for t in (w, m, v))
        return [t.reshape(shape) for t in _adamw(slots, w2, m2, v2)]

    res = {}
    res["w_in"] = big_update(r_in, w_in, m_w_in, v_w_in)
    res["w_branch_ret"] = big_update(r_br, w_branch_ret, m_w_branch_ret, v_w_branch_ret)
    res["w_branch_lru"] = big_update(r_bl, w_branch_lru, m_w_branch_lru, v_w_branch_lru)
    res["w_out"] = big_update(r_out, w_out, m_w_out, v_w_out)
    res["w_ffn_in"] = big_update(r_fi[:, :, :FFN_SHARD], w_ffn_in, m_w_ffn_in, v_w_ffn_in)
    res["w_ffn_out"] = big_update(r_fo, w_ffn_out, m_w_ffn_out, v_w_ffn_out)
    res["lru_wa"] = big_update(r_wa.reshape(N_DEV, LRU_BLOCKS * LRU_ROWS, LRU_BLOCK), lru_wa, m_lru_wa, v_lru_wa)
    res["lru_wx"] = big_update(r_wx.reshape(N_DEV, LRU_BLOCKS * LRU_ROWS, LRU_BLOCK), lru_wx, m_lru_wx, v_lru_wx)

    col = me * HEAD_DIM
    g_meta = lax.dynamic_slice(small_sum, (PACK_META, col), (N_META, HEAD_DIM))
    g_conv = lax.dynamic_slice(small_sum, (PACK_CONV_W, col), (8, HEAD_DIM))
    small_names = ["mix_norm_w", "conv_b", "lru_ba", "lru_bx", "lru_lambda", "ffn_norm_w", "final_norm_w"]
    small_rows = [PACK_MIX_NORM, PACK_CONV_B, PACK_BA, PACK_BX, PACK_LAM, PACK_FFN_NORM, PACK_FINAL_NORM]
    small_w = [mix_norm_w, conv_b, lru_ba, lru_bx, lru_lambda, ffn_norm_w, fw]
    small_m = [m_mix_norm_w, m_conv_b, m_lru_ba, m_lru_bx, m_lru_lambda, m_ffn_norm_w, m_final_norm_w.reshape(1, -1)]
    small_v = [v_mix_norm_w, v_conv_b, v_lru_ba, v_lru_bx, v_lru_lambda, v_ffn_norm_w, v_final_norm_w.reshape(1, -1)]

    def pack_small(vec_list, meta_t, conv_t):
        return jnp.concatenate([t.reshape(8, HEAD_DIM) for t in vec_list] + [meta_t, jnp.pad(conv_t[0], pad4)], axis=0)

    g_small = jnp.concatenate([small_sum[r].reshape(8, HEAD_DIM) for r in small_rows] + [g_meta, g_conv], axis=0)
    outs_small = _adamw(g_small[None], pack_small(small_w, meta_tokens, conv_w),
                        pack_small(small_m, m_meta_tokens, m_conv_w), pack_small(small_v, v_meta_tokens, v_conv_w))
    for idx, name in enumerate(small_names):
        shape = final_norm_w.shape if name == "final_norm_w" else (1, D_MODEL)
        res[name] = [t[8 * idx:8 * idx + 8].reshape(shape) for t in outs_small]
    res["meta_tokens"] = [t[56:72] for t in outs_small]
    res["conv_w"] = [t[72:76].reshape(1, 4, HEAD_DIM) for t in outs_small]

    order = ["meta_tokens", "mix_norm_w", "w_in", "conv_w", "conv_b", "lru_wa", "lru_ba", "lru_wx", "lru_bx",
             "lru_lambda", "w_branch_ret", "w_branch_lru", "w_out", "ffn_norm_w", "w_ffn_in", "w_ffn_out",
             "final_norm_w"]
    out = [loss, grad_x[None]]
    for kind in range(4):
        out += [res[name][kind] for name in order]
    return tuple(out)
```

```python
import functools

import numpy as np
import jax
import jax.numpy as jnp
from jax import lax
from jax.experimental import pallas as pl
from jax.experimental.pallas import tpu as pltpu

F32 = jnp.float32
BF16 = jnp.bfloat16

D_MODEL = 1024
N_META = 16
CHUNK = 128
PAD_ROWS = CHUNK - N_META
HEADS = 8
HEAD_DIM = 128
ROPE_BASE = 10000.0
QK_SCALE = HEAD_DIM ** -0.5
LRU_BLOCKS = 4
LRU_BLOCK = 256
LRU_C = 8.0
FFN_HIDDEN = 2816
N_DEV = 8
FFN_SHARD = 2 * FFN_HIDDEN // N_DEV
FFN_GROUP = 768
FFN_GROUPS = 4
FFN_OUT_SHARD = FFN_HIDDEN // N_DEV
NORM_EPS = 1e-6

ADAM_LR = 0.001
ADAM_B1 = 0.9
ADAM_B2 = 0.999
ADAM_EPS = 1e-08
ADAM_WD = 0.01
ADAM_STEP = 10

VMEM_LIMIT = 56 * 1024 * 1024
MESH_ID = pl.DeviceIdType.MESH
ANY = pl.BlockSpec(memory_space=pl.ANY)


def _cparams(sem):
    return pltpu.CompilerParams(dimension_semantics=sem, vmem_limit_bytes=VMEM_LIMIT)


def _tile(rows, cap):
    t = cap - cap % 64
    while rows % t:
        t -= 64
    return t


def _dot(a, b):
    return jnp.dot(a, b, preferred_element_type=F32)


def _dot_nt(a, b):
    return lax.dot_general(a, b, (((1,), (1,)), ((), ())), preferred_element_type=F32)


def _dot_tn(a, b):
    return lax.dot_general(a, b, (((0,), (0,)), ((), ())), preferred_element_type=F32)


def _sigmoid(x):
    return 1.0 / (1.0 + jnp.exp(-x))


def _gelu_parts(x):
    k = 0.7978845608028654
    inner = k * (x + 0.044715 * x * x * x)
    t = jnp.tanh(inner)
    g = 0.5 * x * (1.0 + t)
    dg = 0.5 * (1.0 + t) + 0.5 * x * (1.0 - t * t) * k * (1.0 + 3.0 * 0.044715 * x * x)
    return g, dg


def _rot(x, cos2, sin2):
    return x * cos2 + pltpu.roll(x, HEAD_DIM // 2, 1) * sin2


def _rot_t(dx, cos2, sin2):
    return dx * cos2 - pltpu.roll(dx, HEAD_DIM // 2, 1) * sin2


def _rms_bwd(x, w, dy):
    rs = lax.rsqrt(jnp.mean(x * x, axis=-1, keepdims=True) + NORM_EPS)
    nh = x * rs
    dw = jnp.sum(dy * nh, axis=0, keepdims=True)
    dn = dy * w
    dx = rs * (dn - nh * jnp.mean(dn * nh, axis=-1, keepdims=True))
    return dx, dw


def _retention_consts():
    h = jnp.arange(HEADS, dtype=F32)
    log_g = jnp.log(1.0 - 2.0 ** (-5.0 - h))
    idx = jnp.arange(CHUNK, dtype=F32)
    diff = idx[:, None] - idx[None, :]
    intra = jnp.where(diff[None] >= 0, jnp.exp(jnp.maximum(diff, 0.0)[None] * log_g[:, None, None]), 0.0)
    q_decay = jnp.exp((idx + 1.0)[:, None] * log_g[None, :])
    k_decay = jnp.exp((CHUNK - 1.0 - idx)[:, None] * log_g[None, :])
    chunk_decay = jnp.exp(CHUNK * log_g)
    shape = (HEADS, CHUNK, CHUNK)
    qd = jnp.broadcast_to(q_decay.T[:, :, None], shape)
    kd = jnp.broadcast_to(k_decay.T[:, :, None], shape)
    cd = jnp.broadcast_to(chunk_decay[:, None, None], shape)
    return jnp.stack([intra, qd, kd, cd])


def _rope_tables(rows):
    pos = jnp.maximum(jnp.arange(rows) - PAD_ROWS, 0).astype(F32)
    inv_freq = ROPE_BASE ** (-jnp.arange(0, HEAD_DIM, 2, dtype=F32) / HEAD_DIM)
    ang = pos[:, None] * inv_freq[None, :]
    cos, sin = jnp.cos(ang), jnp.sin(ang)
    return jnp.concatenate([cos, cos], axis=1), jnp.concatenate([-sin, sin], axis=1)


def _my_index():
    return 4 * lax.axis_index("x") + 2 * lax.axis_index("y") + lax.axis_index("c")


def _peer(k):
    x, y, c = lax.axis_index("x"), lax.axis_index("y"), lax.axis_index("c")
    px = 1 - x if k & 4 else x
    py = 1 - y if k & 2 else y
    pc = 1 - c if k & 1 else c
    return (px, py, pc), 4 * px + 2 * py + pc


def _push_sems(n_arr):
    n_rem = (N_DEV - 1) * n_arr
    return [pltpu.SemaphoreType.DMA((n_rem,)), pltpu.SemaphoreType.DMA((n_rem,)), pltpu.SemaphoreType.DMA((n_arr,))]


class _Push:
    def __init__(self, send_part, land_slot, sems, n_arr):
        self.send_part, self.land_slot, self.n_arr = send_part, land_slot, n_arr
        self.send_sems, self.recv_sems, self.loc_sems = sems

    def _remote(self, k, a, src, dst, pos):
        idx = (k - 1) * self.n_arr + a
        return pltpu.make_async_remote_copy(src_ref=src, dst_ref=dst, send_sem=self.send_sems.at[idx],
                                            recv_sem=self.recv_sems.at[idx], device_id=pos, device_id_type=MESH_ID)

    def _outgoing(self):
        me = _my_index()
        land = self.land_slot(me)
        remote = []
        for k in range(1, N_DEV):
            pos, p = _peer(k)
            src = self.send_part(p)
            remote += [self._remote(k, a, src[a], land[a], pos) for a in range(self.n_arr)]
        own = self.send_part(me)
        local = [pltpu.make_async_copy(own[a], land[a], self.loc_sems.at[a]) for a in range(self.n_arr)]
        return remote, local

    def start(self):
        remote, local = self._outgoing()
        for cp in remote + local:
            cp.start()

    def wait(self):
        own = self.send_part(_my_index())
        for k in range(1, N_DEV):
            pos, p = _peer(k)
            land = self.land_slot(p)
            for a in range(self.n_arr):
                self._remote(k, a, own[a], land[a], pos).wait_recv()
        remote, local = self._outgoing()
        for cp in remote:
            cp.wait_send()
        for cp in local:
            cp.wait()


class _Ride:
    def __init__(self, arrays, out_shapes, send_part, land_slot):
        self.arrays, self.out_shapes = list(arrays), list(out_shapes)
        self.send_part, self.land_slot, self.n = send_part, land_slot, len(arrays)

    def specs(self):
        return [ANY] * self.n

    def scratch(self):
        return _push_sems(self.n)

    def push(self, in_refs, out_refs, sems):
        return _Push(lambda p: self.send_part(in_refs, p), lambda s: self.land_slot(out_refs, s), tuple(sems), self.n)


def _slot_of_sender(out_refs, s):
    return [r.at[s] for r in out_refs]


def _push_call(name, arrays, out_shapes, send_part, land_slot):
    n_arr = len(arrays)

    def body(*refs):
        ins, outs, sems = refs[:n_arr], refs[n_arr:2 * n_arr], refs[2 * n_arr:]
        push = _Push(lambda p: send_part(ins, p), lambda s: land_slot(outs, s), sems, n_arr)
        push.start()
        push.wait()

    return pl.pallas_call(
        body, name=name, in_specs=[ANY] * n_arr, out_specs=[ANY] * n_arr, out_shape=out_shapes,
        scratch_shapes=_push_sems(n_arr), compiler_params=pltpu.CompilerParams(has_side_effects=True),
    )(*arrays)


LRU_ROWS = LRU_BLOCK // N_DEV
FFN_PAD_ROWS = FFN_GROUP - 2 * FFN_OUT_SHARD


def _half_rows(d):
    return pl.ds(pl.multiple_of((d % 2) * FFN_OUT_SHARD, 16), FFN_OUT_SHARD)


def _lru_rows(d):
    return pl.ds(pl.multiple_of(d * LRU_ROWS, 16), LRU_ROWS)


def _rest_slots(outs, d):
    return [outs[0].at[d], outs[1].at[d], outs[2].at[d],
            outs[3].at[d // 2, _half_rows(d), :], outs[4].at[:, _lru_rows(d), :], outs[5].at[:, _lru_rows(d), :]]


REST_WFO = 3
REST_SHAPES = [
    jax.ShapeDtypeStruct((N_DEV, D_MODEL // N_DEV, D_MODEL), BF16),
    jax.ShapeDtypeStruct((N_DEV, D_MODEL // N_DEV, D_MODEL), BF16),
    jax.ShapeDtypeStruct((N_DEV, D_MODEL // N_DEV, D_MODEL), BF16),
    jax.ShapeDtypeStruct((FFN_GROUPS, FFN_GROUP, D_MODEL), BF16),
    jax.ShapeDtypeStruct((LRU_BLOCKS, LRU_BLOCK, LRU_BLOCK), BF16),
    jax.ShapeDtypeStruct((LRU_BLOCKS, LRU_BLOCK, LRU_BLOCK), BF16),
]


def _in_proj(h0, norm_w, win_g, rest_shards):
    rows = h0.shape[0]
    tm = _tile(rows, 1664)
    n_i = rows // tm
    n_rest = len(rest_shards)

    def body(h_ref, nw_ref, w_ref, *refs):
        shard_refs = refs[:n_rest]
        proj_ref, u_ref = refs[n_rest:n_rest + 2]
        gath_refs = refs[n_rest + 2:2 * n_rest + 2]
        u_s, send_sems, recv_sems, loc_sems, zero_sems, zbuf = refs[2 * n_rest + 2:]
        i, j = pl.program_id(0), pl.program_id(1)
        push = _Push(lambda p: list(shard_refs), lambda s: _rest_slots(gath_refs, s),
                     (send_sems, recv_sems, loc_sems), n_rest)
        zero_fill = [pltpu.make_async_copy(zbuf, gath_refs[REST_WFO].at[g, pl.ds(2 * FFN_OUT_SHARD, FFN_PAD_ROWS), :],
                                           zero_sems.at[g]) for g in range(FFN_GROUPS)]

        @pl.when(jnp.logical_and(i == 0, j == 0))
        def _():
            push.start()
            zbuf[...] = jnp.zeros_like(zbuf)
            for cp in zero_fill:
                cp.start()

        @pl.when(j == 0)
        def _():
            x = h_ref[...]
            rs = lax.rsqrt(jnp.mean(x * x, axis=-1, keepdims=True) + NORM_EPS)
            u = (x * rs * nw_ref[...]).astype(BF16)
            u_s[...] = u
            u_ref[...] = u
        proj_ref[...] = _dot(u_s[...], w_ref[...]).astype(BF16)

        @pl.when(jnp.logical_and(i == n_i - 1, j == N_DEV - 1))
        def _():
            push.wait()
            for cp in zero_fill:
                cp.wait()

    return pl.pallas_call(
        body, name="in_proj", grid=(n_i, N_DEV),
        in_specs=[pl.BlockSpec((tm, D_MODEL), lambda i, j: (i, 0)),
                  pl.BlockSpec((1, D_MODEL), lambda i, j: (0, 0)),
                  pl.BlockSpec((None, D_MODEL, D_MODEL), lambda i, j: (j, 0, 0))] + [ANY] * n_rest,
        out_specs=[pl.BlockSpec((tm, D_MODEL), lambda i, j: (i, j)),
                   pl.BlockSpec((tm, D_MODEL), lambda i, j: (i, 0))] + [ANY] * n_rest,
        out_shape=[jax.ShapeDtypeStruct((rows, N_DEV * D_MODEL), BF16),
                   jax.ShapeDtypeStruct((rows, D_MODEL), BF16)] + REST_SHAPES,
        scratch_shapes=[pltpu.VMEM((tm, D_MODEL), BF16)] + _push_sems(n_rest)
        + [pltpu.SemaphoreType.DMA((FFN_GROUPS,)), pltpu.VMEM((FFN_PAD_ROWS, D_MODEL), BF16)],
        compiler_params=pltpu.CompilerParams(dimension_semantics=("arbitrary", "arbitrary"),
                                             vmem_limit_bytes=VMEM_LIMIT, has_side_effects=True),
    )(h0, norm_w, win_g, *rest_shards)


def _seg_spec(rows_per_block, seg):
    return pl.BlockSpec((rows_per_block, D_MODEL), lambda n, seg=seg: (n, seg))


def _retention_fwd(proj, cos2, sin2, dec, ride):
    rows = proj.shape[0]
    n_chunks = rows // CHUNK
    n_r = ride.n

    def body(q_ref, k_ref, v_ref, g_ref, c_ref, s_ref, dec_ref, *refs):
        o_ref, zr_ref, st_ref = refs[n_r:n_r + 3]
        state = refs[2 * n_r + 3]
        push = ride.push(refs[:n_r], refs[n_r + 3:2 * n_r + 3], refs[2 * n_r + 4:])

        @pl.when(pl.program_id(0) == 0)
        def _():
            push.start()
            state[...] = jnp.zeros_like(state)
        cos_t, sin_t = c_ref[...], s_ref[...]
        for h in range(HEADS):
            sl = slice(HEAD_DIM * h, HEAD_DIM * (h + 1))
            qh = _rot(q_ref[:, sl].astype(F32), cos_t, sin_t)
            kh = _rot(k_ref[:, sl].astype(F32), cos_t, sin_t) * QK_SCALE
            qb, kb, vb = qh.astype(BF16), kh.astype(BF16), v_ref[:, sl]
            s = _dot_nt(qb, kb) * dec_ref[0, h]
            st = state[h]
            st_ref[h] = st
            o = _dot(s.astype(BF16), vb) + _dot(qb, st.astype(BF16)) * dec_ref[1, h]
            state[h] = st * dec_ref[3, h] + _dot_tn((kh * dec_ref[2, h]).astype(BF16), vb)
            o_ref[:, sl] = o.astype(BF16)
            r = lax.rsqrt(jnp.mean(o * o, axis=-1, keepdims=True) + NORM_EPS)
            g = g_ref[:, sl].astype(F32)
            zr_ref[:, sl] = (g * _sigmoid(g) * (o * r)).astype(BF16)

        @pl.when(pl.program_id(0) == n_chunks - 1)
        def _():
            push.wait()

    tab = pl.BlockSpec((CHUNK, HEAD_DIM), lambda n: (n, 0))
    return pl.pallas_call(
        body, name="retention_fwd", grid=(n_chunks,),
        in_specs=[_seg_spec(CHUNK, 0), _seg_spec(CHUNK, 1), _seg_spec(CHUNK, 2), _seg_spec(CHUNK, 3), tab, tab,
                  pl.BlockSpec((4, HEADS, CHUNK, CHUNK), lambda n: (0, 0, 0, 0))] + ride.specs(),
        out_specs=[pl.BlockSpec((CHUNK, D_MODEL), lambda n: (n, 0)),
                   pl.BlockSpec((CHUNK, D_MODEL), lambda n: (n, 0)),
                   pl.BlockSpec((None, HEADS, HEAD_DIM, HEAD_DIM), lambda n: (n, 0, 0, 0))] + ride.specs(),
        out_shape=[jax.ShapeDtypeStruct((rows, D_MODEL), BF16),
                   jax.ShapeDtypeStruct((rows, D_MODEL), BF16),
                   jax.ShapeDtypeStruct((n_chunks, HEADS, HEAD_DIM, HEAD_DIM), F32)] + ride.out_shapes,
        scratch_shapes=[pltpu.VMEM((HEADS, HEAD_DIM, HEAD_DIM), F32)] + ride.scratch(),
        compiler_params=pltpu.CompilerParams(dimension_semantics=("arbitrary",), vmem_limit_bytes=VMEM_LIMIT,
                                             has_side_effects=True),
    )(proj, proj, proj, proj, cos2, sin2, dec, *ride.arrays)


def _lru_gates(c, ba, bx, lam, wa_ref, wx_ref):
    pre_r, pre_i = [], []
    for g in range(LRU_BLOCKS):
        cg = c[:, LRU_BLOCK * g:LRU_BLOCK * (g + 1)].astype(BF16)
        pre_r.append(_dot(cg, wa_ref[g]))
        pre_i.append(_dot(cg, wx_ref[g]))
    r = _sigmoid(jnp.concatenate(pre_r, axis=1) + ba)
    ig = _sigmoid(jnp.concatenate(pre_i, axis=1) + bx)
    sp = jnp.maximum(-lam, 0.0) + jnp.log(1.0 + jnp.exp(-jnp.abs(lam)))
    log_a = -LRU_C * r * sp
    a = jnp.exp(log_a)
    mult = jnp.sqrt(-jnp.tanh(log_a) * (a * a + 1.0))
    return r, ig, a, mult, sp


def _conv_taps(xbuf, tm, cw_ref, cb_ref):
    c = cb_ref[...] + cw_ref[3:4, :] * xbuf[8:8 + tm, :]
    for back in (1, 2, 3):
        c = c + cw_ref[3 - back:4 - back, :] * xbuf[8 - back:8 - back + tm, :]
    return c


def _lru_fwd(proj, conv_w, conv_b, ba, bx, lam, wa_g, wx_g):
    rows = proj.shape[0]
    tm = _tile(rows, 320)

    def body(x_ref, gt_ref, cw_ref, cb_ref, ba_ref, bx_ref, lam_ref, wa_ref, wx_ref, hs_ref, zl_ref,
             xbuf, abuf, ubuf, hcar):
        i = pl.program_id(0)

        @pl.when(i == 0)
        def _():
            xbuf[0:8, :] = jnp.zeros((8, D_MODEL), F32)
            hcar[...] = jnp.zeros_like(hcar)

        xbuf[8:8 + tm, :] = x_ref[...].astype(F32)
        c = _conv_taps(xbuf, tm, cw_ref, cb_ref)
        xbuf[0:8, :] = xbuf[tm:tm + 8, :]
        r, ig, a, mult, _ = _lru_gates(c, ba_ref[...], bx_ref[...], lam_ref[...], wa_ref, wx_ref)
        row = i * tm + lax.broadcasted_iota(jnp.int32, (tm, 1), 0)
        abuf[...] = a
        ubuf[...] = jnp.where(row >= PAD_ROWS, mult * (ig * c), 0.0)

        sub = lax.broadcasted_iota(jnp.int32, (8, D_MODEL), 0)

        def block(b, carry):
            off = pl.multiple_of(b * 8, 8)
            av, uv = abuf[pl.ds(off, 8), :], ubuf[pl.ds(off, 8), :]
            for s in (1, 2, 4):
                us = jnp.where(sub >= s, pltpu.roll(uv, s, 0), 0.0)
                as_ = jnp.where(sub >= s, pltpu.roll(av, s, 0), 1.0)
                uv = uv + av * us
                av = av * as_
            hv = uv + av * carry
            ubuf[pl.ds(off, 8), :] = hv
            return hv[7:8, :]

        hcar[...] = lax.fori_loop(0, tm // 8, block, hcar[...])
        gl, _ = _gelu_parts(gt_ref[...].astype(F32))
        hs = ubuf[...]
        hs_ref[...] = hs.astype(BF16)
        zl_ref[...] = (gl * hs).astype(BF16)

    vec = pl.BlockSpec((1, D_MODEL), lambda i: (0, 0))
    mat = pl.BlockSpec((LRU_BLOCKS, LRU_BLOCK, LRU_BLOCK), lambda i: (0, 0, 0))
    return pl.pallas_call(
        body, name="lru_fwd", grid=(rows // tm,),
        in_specs=[_seg_spec(tm, 4), _seg_spec(tm, 5), pl.BlockSpec((4, D_MODEL), lambda i: (0, 0)),
                  vec, vec, vec, vec, mat, mat],
        out_specs=[pl.BlockSpec((tm, D_MODEL), lambda i: (i, 0)), pl.BlockSpec((tm, D_MODEL), lambda i: (i, 0))],
        out_shape=[jax.ShapeDtypeStruct((rows, D_MODEL), BF16), jax.ShapeDtypeStruct((rows, D_MODEL), BF16)],
        scratch_shapes=[pltpu.VMEM((tm + 8, D_MODEL), F32), pltpu.VMEM((tm, D_MODEL), F32),
                        pltpu.VMEM((tm, D_MODEL), F32), pltpu.VMEM((1, D_MODEL), F32)],
        compiler_params=_cparams(("arbitrary",)),
    )(proj, proj, conv_w, conv_b, ba, bx, lam, wa_g, wx_g)


def _mix_fwd(zr, zl, proj, h0, wbr, wbl, wout):
    rows = h0.shape[0]
    tm = _tile(rows, 640)

    def body(zr_ref, zl_ref, ga_ref, gb_ref, h0_ref, wbr_ref, wbl_ref, wo_ref, h1_ref, yr_ref, yl_ref, mx_ref):
        yr = _dot(zr_ref[...], wbr_ref[...])
        yl = _dot(zl_ref[...], wbl_ref[...])
        mixed = (_sigmoid(ga_ref[...].astype(F32)) * yr + _sigmoid(gb_ref[...].astype(F32)) * yl).astype(BF16)
        yr_ref[...] = yr.astype(BF16)
        yl_ref[...] = yl.astype(BF16)
        mx_ref[...] = mixed
        h1_ref[...] = h0_ref[...] + _dot(mixed, wo_ref[...])

    row = pl.BlockSpec((tm, D_MODEL), lambda i: (i, 0))
    wsp = pl.BlockSpec((D_MODEL, D_MODEL), lambda i: (0, 0))
    return pl.pallas_call(
        body, name="mix_fwd", grid=(rows // tm,),
        in_specs=[row, row, _seg_spec(tm, 6), _seg_spec(tm, 7), row, wsp, wsp, wsp],
        out_specs=[row, row, row, row],
        out_shape=[jax.ShapeDtypeStruct((rows, D_MODEL), F32)] + [jax.ShapeDtypeStruct((rows, D_MODEL), BF16)] * 3,
        compiler_params=_cparams(("parallel",)),
    )(zr, zl, proj, proj, h0, wbr, wbl, wout)


def _ffn_fwd_loss(h1, norm_w, wfi_g, wfo_g, final_w, target):
    rows = h1.shape[0]
    tm = _tile(rows, 640)
    last = FFN_GROUPS - 1

    def body(h1_ref, nw_ref, wg_ref, wu_ref, wo_ref, fw_ref, t_ref,
             u2_ref, g_ref, up_ref, act_ref, dh2_ref, red_ref, u2_s, acc):
        i, d = pl.program_id(0), pl.program_id(1)

        @pl.when(jnp.logical_and(i == 0, d == 0))
        def _():
            red_ref[...] = jnp.zeros_like(red_ref)

        @pl.when(d == 0)
        def _():
            x = h1_ref[...]
            rs = lax.rsqrt(jnp.mean(x * x, axis=-1, keepdims=True) + NORM_EPS)
            u2 = (x * rs * nw_ref[...]).astype(BF16)
            u2_s[...] = u2
            u2_ref[...] = u2
            acc[...] = jnp.zeros_like(acc)

        g = _dot(u2_s[...], wg_ref[...])
        up = _dot(u2_s[...], wu_ref[...])
        act = (g * _sigmoid(g) * up).astype(BF16)
        g_ref[...] = g.astype(BF16)
        up_ref[...] = up.astype(BF16)
        act_ref[...] = act
        acc[...] += _dot(act, wo_ref[...])

        @pl.when(d == last)
        def _():
            h2 = h1_ref[...] + acc[...]
            rs = lax.rsqrt(jnp.mean(h2 * h2, axis=-1, keepdims=True) + NORM_EPS)
            nh = h2 * rs
            fw = fw_ref[...]
            row = i * tm + lax.broadcasted_iota(jnp.int32, (tm, 1), 0)
            diff = jnp.where(row >= CHUNK, nh * fw - t_ref[...], 0.0)
            dy = diff * (1.0 / D_MODEL)
            red_ref[0:1, :] += jnp.sum(diff * diff, axis=0, keepdims=True)
            red_ref[1:2, :] += jnp.sum(dy * nh, axis=0, keepdims=True)
            dn = dy * fw
            dh2_ref[...] = rs * (dn - nh * jnp.mean(dn * nh, axis=-1, keepdims=True))

    row = pl.BlockSpec((tm, D_MODEL), lambda i, d: (i, 0))
    vec = pl.BlockSpec((1, D_MODEL), lambda i, d: (0, 0))
    hid = pl.BlockSpec((tm, FFN_GROUP), lambda i, d: (i, d))
    hid_shape = jax.ShapeDtypeStruct((rows, FFN_GROUPS * FFN_GROUP), BF16)
    return pl.pallas_call(
        body, name="ffn_fwd_loss", grid=(rows // tm, FFN_GROUPS),
        in_specs=[row, vec,
                  pl.BlockSpec((None, D_MODEL, FFN_GROUP), lambda i, d: (d, 0, 0)),
                  pl.BlockSpec((None, D_MODEL, FFN_GROUP), lambda i, d: (d + FFN_GROUPS, 0, 0)),
                  pl.BlockSpec((None, FFN_GROUP, D_MODEL), lambda i, d: (d, 0, 0)),
                  vec, row],
        out_specs=[row, hid, hid, hid, row, pl.BlockSpec((8, D_MODEL), lambda i, d: (0, 0))],
        out_shape=[jax.ShapeDtypeStruct((rows, D_MODEL), BF16), hid_shape, hid_shape, hid_shape,
                   jax.ShapeDtypeStruct((rows, D_MODEL), F32), jax.ShapeDtypeStruct((8, D_MODEL), F32)],
        scratch_shapes=[pltpu.VMEM((tm, D_MODEL), BF16), pltpu.VMEM((tm, D_MODEL), F32)],
        compiler_params=_cparams(("arbitrary", "arbitrary")),
    )(h1, norm_w, wfi_g, wfi_g, wfo_g, final_w, target)


def _wgrad(a, b, ka, tn, out_dtype, b_halves=False):
    rows = a.shape[0]
    na = a.shape[1] // ka
    tm = _tile(rows, 640)
    nm = rows // tm
    if b_halves:
        per_half = b.shape[2] // tn
        nb = 2 * per_half
        b_spec = pl.BlockSpec((None, tm, tn), lambda p, q, m: (q // per_half, m, q % per_half))
    else:
        nb = b.shape[1] // tn
        b_spec = pl.BlockSpec((tm, tn), lambda p, q, m: (m, q))

    def body(a_ref, b_ref, o_ref, acc):
        m = pl.program_id(2)

        @pl.when(m == 0)
        def _():
            acc[...] = jnp.zeros_like(acc)

        acc[...] += _dot_tn(a_ref[...].astype(BF16), b_ref[...].astype(BF16))

        @pl.when(m == nm - 1)
        def _():
            o_ref[...] = acc[...].astype(out_dtype)

    return pl.pallas_call(
        body, name="wgrad", grid=(na, nb, nm),
        in_specs=[pl.BlockSpec((tm, ka), lambda p, q, m: (m, p)), b_spec],
        out_specs=pl.BlockSpec((None, None, ka, tn), lambda p, q, m: (p, q, 0, 0)),
        out_shape=jax.ShapeDtypeStruct((na, nb, ka, tn), out_dtype),
        scratch_shapes=[pltpu.VMEM((ka, tn), F32)],
        compiler_params=_cparams(("parallel", "parallel", "arbitrary")),
    )(a, b)


def _ffn_bwd(dh2, g, up, h1, norm_w, wfi_g, wfo_g):
    rows = h1.shape[0]
    tm = _tile(rows, 640)
    last = FFN_GROUPS - 1

    def body(dh2_ref, g_ref, up_ref, h1_ref, nw_ref, wg_ref, wu_ref, wo_ref, dgu_ref, dh1_ref, dw_ref,
             dh2_s, acc):
        i, d = pl.program_id(0), pl.program_id(1)

        @pl.when(jnp.logical_and(i == 0, d == 0))
        def _():
            dw_ref[...] = jnp.zeros_like(dw_ref)

        @pl.when(d == 0)
        def _():
            dh2_s[...] = dh2_ref[...].astype(BF16)
            acc[...] = jnp.zeros_like(acc)

        dact = _dot_nt(dh2_s[...], wo_ref[...])
        gv, uv = g_ref[...].astype(F32), up_ref[...].astype(F32)
        sg = _sigmoid(gv)
        dg = (dact * uv * (sg * (1.0 + gv * (1.0 - sg)))).astype(BF16)
        dup = (dact * (gv * sg)).astype(BF16)
        dgu_ref[0] = dg
        dgu_ref[1] = dup
        acc[...] += _dot_nt(dg, wg_ref[...]) + _dot_nt(dup, wu_ref[...])

        @pl.when(d == last)
        def _():
            dx, dw = _rms_bwd(h1_ref[...], nw_ref[...], acc[...])
            dw_ref[0:1, :] += dw
            dh1_ref[...] = dh2_ref[...] + dx

    row = pl.BlockSpec((tm, D_MODEL), lambda i, d: (i, 0))
    vec = pl.BlockSpec((1, D_MODEL), lambda i, d: (0, 0))
    hid = pl.BlockSpec((tm, FFN_GROUP), lambda i, d: (i, d))
    return pl.pallas_call(
        body, name="ffn_bwd", grid=(rows // tm, FFN_GROUPS),
        in_specs=[row, hid, hid, row, vec,
                  pl.BlockSpec((None, D_MODEL, FFN_GROUP), lambda i, d: (d, 0, 0)),
                  pl.BlockSpec((None, D_MODEL, FFN_GROUP), lambda i, d: (d + FFN_GROUPS, 0, 0)),
                  pl.BlockSpec((None, FFN_GROUP, D_MODEL), lambda i, d: (d, 0, 0))],
        out_specs=[pl.BlockSpec((2, tm, FFN_GROUP), lambda i, d: (0, i, d)), row,
                   pl.BlockSpec((8, D_MODEL), lambda i, d: (0, 0))],
        out_shape=[jax.ShapeDtypeStruct((2, rows, FFN_GROUPS * FFN_GROUP), BF16),
                   jax.ShapeDtypeStruct((rows, D_MODEL), F32), jax.ShapeDtypeStruct((8, D_MODEL), F32)],
        scratch_shapes=[pltpu.VMEM((tm, D_MODEL), BF16), pltpu.VMEM((tm, D_MODEL), F32)],
        compiler_params=_cparams(("arbitrary", "arbitrary")),
    )(dh2, g, up, h1, norm_w, wfi_g, wfi_g, wfo_g)


def _mix_bwd(dh1, yr, yl, proj, wbr, wbl, wout):
    rows = dh1.shape[0]
    tm = _tile(rows, 640)

    def body(dh1_ref, yr_ref, yl_ref, ga_ref, gb_ref, wbr_ref, wbl_ref, wo_ref,
             dyr_ref, dyl_ref, dseg_ref, dzr_ref, dzl_ref):
        dmix = _dot_nt(dh1_ref[...].astype(BF16), wo_ref[...])
        sa, sb = _sigmoid(ga_ref[...].astype(F32)), _sigmoid(gb_ref[...].astype(F32))
        dyr = (dmix * sa).astype(BF16)
        dyl = (dmix * sb).astype(BF16)
        dyr_ref[...] = dyr
        dyl_ref[...] = dyl
        dseg_ref[:, 0:D_MODEL] = (dmix * yr_ref[...].astype(F32) * (sa * (1.0 - sa))).astype(BF16)
        dseg_ref[:, D_MODEL:2 * D_MODEL] = (dmix * yl_ref[...].astype(F32) * (sb * (1.0 - sb))).astype(BF16)
        dzr_ref[...] = _dot_nt(dyr, wbr_ref[...]).astype(BF16)
        dzl_ref[...] = _dot_nt(dyl, wbl_ref[...]).astype(BF16)

    row = pl.BlockSpec((tm, D_MODEL), lambda i: (i, 0))
    wsp = pl.BlockSpec((D_MODEL, D_MODEL), lambda i: (0, 0))
    bshape = jax.ShapeDtypeStruct((rows, D_MODEL), BF16)
    return pl.pallas_call(
        body, name="mix_bwd", grid=(rows // tm,),
        in_specs=[row, row, row, _seg_spec(tm, 6), _seg_spec(tm, 7), wsp, wsp, wsp],
        out_specs=[row, row, pl.BlockSpec((tm, 2 * D_MODEL), lambda i: (i, 3)), row, row],
        out_shape=[bshape, bshape, jax.ShapeDtypeStruct((rows, N_DEV * D_MODEL), BF16), bshape, bshape],
        compiler_params=_cparams(("parallel",)),
    )(dh1, yr, yl, proj, proj, wbr, wbl, wout)


S1_SHAPES = [
    jax.ShapeDtypeStruct((N_DEV, D_MODEL, FFN_GROUP), BF16),
    jax.ShapeDtypeStruct((N_DEV, FFN_OUT_SHARD, D_MODEL), BF16),
]


def _s1_parts(ins, p):
    return [ins[0].at[p], ins[1].at[p // 2, _half_rows(p), :]]


def _lru_bwd(dzl, hs, proj, dproj, conv_w, conv_b, ba, bx, lam, wa_g, wx_g, s1_grads):
    rows = dzl.shape[0]
    tm = _tile(rows, 320)
    nt = rows // tm
    t8 = tm // 8
    n_s1 = len(s1_grads)

    def body(dzl_ref, hs_ref, hsp_ref, x_ref, xp_ref, gt_ref, cw_ref, cb_ref, ba_ref, bx_ref, lam_ref,
             wa_ref, wx_ref, dproj_in, *refs):
        del dproj_in
        s1_refs = refs[:n_s1]
        dseg_ref, dwa_ref, dwx_ref, sm_ref = refs[n_s1:n_s1 + 4]
        land_refs = refs[n_s1 + 4:2 * n_s1 + 4]
        xbuf, abuf, bbuf, dbuf, dcbuf, anext, dhcar, send_sems, recv_sems, loc_sems = refs[2 * n_s1 + 4:]
        step = pl.program_id(0)
        i = nt - 1 - step
        push = _Push(lambda p: _s1_parts(s1_refs, p), lambda s: [r.at[s] for r in land_refs],
                     (send_sems, recv_sems, loc_sems), n_s1)

        @pl.when(step == 0)
        def _():
            push.start()
            dwa_ref[...] = jnp.zeros_like(dwa_ref)
            dwx_ref[...] = jnp.zeros_like(dwx_ref)
            sm_ref[...] = jnp.zeros_like(sm_ref)
            anext[...] = jnp.zeros_like(anext)
            dhcar[...] = jnp.zeros_like(dhcar)
            dcbuf[tm:tm + 8, :] = jnp.zeros((8, D_MODEL), F32)

        first = i == 0
        x_prev = jnp.where(first, 0.0, xp_ref[8:16, :].astype(F32))
        x_v = x_ref[...].astype(F32)
        xbuf[0:8, :] = x_prev
        xbuf[8:8 + tm, :] = x_v
        c = _conv_taps(xbuf, tm, cw_ref, cb_ref)
        lam_v = lam_ref[...]
        r, ig, a, mult, sp = _lru_gates(c, ba_ref[...], bx_ref[...], lam_v, wa_ref, wx_ref)
        hs_v = hs_ref[...].astype(F32)
        gl, dgl = _gelu_parts(gt_ref[...].astype(F32))
        dzl_v = dzl_ref[...].astype(F32)
        dseg_ref[:, D_MODEL:2 * D_MODEL] = (dzl_v * hs_v * dgl).astype(BF16)
        dbuf[...] = dzl_v * gl
        abuf[0:tm, :] = a
        abuf[tm:tm + 8, :] = jnp.broadcast_to(anext[...], (8, D_MODEL))
        bbuf[...] = abuf[1:tm + 1, :]

        sub = lax.broadcasted_iota(jnp.int32, (8, D_MODEL), 0)

        def block(k, carry):
            off = pl.multiple_of((t8 - 1 - k) * 8, 8)
            av = bbuf[pl.ds(off, 8), :]
            uv = dbuf[pl.ds(off, 8), :]
            for s in (1, 2, 4):
                us = jnp.where(sub < 8 - s, pltpu.roll(uv, 8 - s, 0), 0.0)
                as_ = jnp.where(sub < 8 - s, pltpu.roll(av, 8 - s, 0), 1.0)
                uv = uv + av * us
                av = av * as_
            hv = uv + av * carry
            dbuf[pl.ds(off, 8), :] = hv
            return hv[0:1, :]

        dhcar[...] = lax.fori_loop(0, t8, block, dhcar[...])
        anext[...] = abuf[0:1, :]
        dh = dbuf[...]

        xbuf[0:8, :] = jnp.where(first, 0.0, hsp_ref[8:16, :].astype(F32))
        xbuf[8:8 + tm, :] = hs_v
        hprev = xbuf[7:7 + tm, :]
        row = i * tm + lax.broadcasted_iota(jnp.int32, (tm, 1), 0)
        duu = jnp.where(row >= PAD_ROWS, dh, 0.0)
        da = dh * hprev
        dmult = duu * ig * c
        di = duu * mult * c
        dc = duu * mult * ig
        dlog_a = da * a - dmult * (a * a) / mult
        dr = dlog_a * (-LRU_C * sp)
        dsp = jnp.sum(dlog_a * (-LRU_C * r), axis=0, keepdims=True)
        dpr = dr * r * (1.0 - r)
        dpi = di * ig * (1.0 - ig)
        dpr_b, dpi_b = dpr.astype(BF16), dpi.astype(BF16)
        dcs = []
        for g in range(LRU_BLOCKS):
            sl = slice(LRU_BLOCK * g, LRU_BLOCK * (g + 1))
            cg = c[:, sl].astype(BF16)
            dwa_ref[g] += _dot_tn(cg, dpr_b[:, sl])
            dwx_ref[g] += _dot_tn(cg, dpi_b[:, sl])
            dcs.append(_dot_nt(dpr_b[:, sl], wa_ref[g]) + _dot_nt(dpi_b[:, sl], wx_ref[g]))
        dc = dc + jnp.concatenate(dcs, axis=1)

        dcbuf[0:tm, :] = dc
        xbuf[8:8 + tm, :] = x_v
        xbuf[0:8, :] = x_prev
        dlin = cw_ref[3:4, :] * dc
        sm_ref[3:4, :] += jnp.sum(dc * xbuf[8:8 + tm, :], axis=0, keepdims=True)
        for back in (1, 2, 3):
            dlin = dlin + cw_ref[3 - back:4 - back, :] * dcbuf[back:back + tm, :]
            sm_ref[3 - back:4 - back, :] += jnp.sum(dc * xbuf[8 - back:8 - back + tm, :], axis=0, keepdims=True)
        dseg_ref[:, 0:D_MODEL] = dlin.astype(BF16)
        dcbuf[tm:tm + 8, :] = dcbuf[0:8, :]
        sm_ref[4:5, :] += jnp.sum(dc, axis=0, keepdims=True)
        sm_ref[5:6, :] += jnp.sum(dpr, axis=0, keepdims=True)
        sm_ref[6:7, :] += jnp.sum(dpi, axis=0, keepdims=True)
        sm_ref[7:8, :] += dsp * (-_sigmoid(-lam_v))

        @pl.when(step == nt - 1)
        def _():
            push.wait()

    rowb = pl.BlockSpec((tm, D_MODEL), lambda s: (nt - 1 - s, 0))
    t16 = tm // 16
    prev8 = pl.BlockSpec((16, D_MODEL), lambda s: (jnp.maximum((nt - 1 - s) * t16 - 1, 0), 0))
    seg = lambda k: pl.BlockSpec((tm, D_MODEL), lambda s, k=k: (nt - 1 - s, k))
    prev8_seg4 = pl.BlockSpec((16, D_MODEL), lambda s: (jnp.maximum((nt - 1 - s) * t16 - 1, 0), 4))
    vec = pl.BlockSpec((1, D_MODEL), lambda s: (0, 0))
    mat = pl.BlockSpec((LRU_BLOCKS, LRU_BLOCK, LRU_BLOCK), lambda s: (0, 0, 0))
    mshape = jax.ShapeDtypeStruct((LRU_BLOCKS, LRU_BLOCK, LRU_BLOCK), F32)
    n_in = 13
    return pl.pallas_call(
        body, name="lru_bwd", grid=(nt,),
        in_specs=[rowb, rowb, prev8, seg(4), prev8_seg4, seg(5), pl.BlockSpec((4, D_MODEL), lambda s: (0, 0)),
                  vec, vec, vec, vec, mat, mat, ANY] + [ANY] * n_s1,
        out_specs=[pl.BlockSpec((tm, 2 * D_MODEL), lambda s: (nt - 1 - s, 2)), mat, mat,
                   pl.BlockSpec((8, D_MODEL), lambda s: (0, 0))] + [ANY] * n_s1,
        out_shape=[jax.ShapeDtypeStruct(dproj.shape, dproj.dtype), mshape, mshape,
                   jax.ShapeDtypeStruct((8, D_MODEL), F32)] + S1_SHAPES,
        input_output_aliases={n_in: 0},
        scratch_shapes=[pltpu.VMEM((tm + 8, D_MODEL), F32), pltpu.VMEM((tm + 8, D_MODEL), F32),
                        pltpu.VMEM((tm, D_MODEL), F32), pltpu.VMEM((tm, D_MODEL), F32),
                        pltpu.VMEM((tm + 8, D_MODEL), F32),
                        pltpu.VMEM((1, D_MODEL), F32), pltpu.VMEM((1, D_MODEL), F32)] + _push_sems(n_s1),
        compiler_params=pltpu.CompilerParams(dimension_semantics=("arbitrary",), vmem_limit_bytes=VMEM_LIMIT,
                                             has_side_effects=True),
    )(dzl, hs, hs, proj, proj, proj, conv_w, conv_b, ba, bx, lam, wa_g, wx_g, dproj, *s1_grads)


def _retention_bwd(dzr, o, proj, states, cos2, sin2, dec, dproj, ride):
    rows = dzr.shape[0]
    n_chunks = rows // CHUNK
    n_r = ride.n

    def body(dzr_ref, o_ref, q_ref, k_ref, v_ref, g_ref, st_ref, c_ref, s_ref, dec_ref, dproj_in, *refs):
        del dproj_in
        dseg_ref = refs[n_r]
        dstate = refs[2 * n_r + 1]
        push = ride.push(refs[:n_r], refs[n_r + 1:2 * n_r + 1], refs[2 * n_r + 2:])

        @pl.when(pl.program_id(0) == 0)
        def _():
            push.start()
            dstate[...] = jnp.zeros_like(dstate)
        cos_t, sin_t = c_ref[...], s_ref[...]
        for h in range(HEADS):
            sl = slice(HEAD_DIM * h, HEAD_DIM * (h + 1))
            o = o_ref[:, sl].astype(F32)
            g = g_ref[:, sl].astype(F32)
            dzr_v = dzr_ref[:, sl].astype(F32)
            sg = _sigmoid(g)
            r = lax.rsqrt(jnp.mean(o * o, axis=-1, keepdims=True) + NORM_EPS)
            on = o * r
            dseg_ref[:, 3 * D_MODEL + HEAD_DIM * h:3 * D_MODEL + HEAD_DIM * (h + 1)] = (
                dzr_v * on * (sg * (1.0 + g * (1.0 - sg)))).astype(BF16)
            don = dzr_v * (g * sg)
            do = r * (don - on * jnp.mean(don * on, axis=-1, keepdims=True))
            dob = do.astype(BF16)

            qh = _rot(q_ref[:, sl].astype(F32), cos_t, sin_t)
            kh = _rot(k_ref[:, sl].astype(F32), cos_t, sin_t) * QK_SCALE
            qb, kb, vb = qh.astype(BF16), kh.astype(BF16), v_ref[:, sl]
            intra, qd, kd, cd = dec_ref[0, h], dec_ref[1, h], dec_ref[2, h], dec_ref[3, h]
            s = (_dot_nt(qb, kb) * intra).astype(BF16)
            ds = (_dot_nt(dob, vb) * intra).astype(BF16)
            st_b = st_ref[h].astype(BF16)
            dst = dstate[h]
            dst_b = dst.astype(BF16)
            dv = _dot_tn(s, dob) + _dot((kh * kd).astype(BF16), dst_b)
            dq = _dot(ds, kb) + _dot_nt(dob, st_b) * qd
            dk = _dot_tn(ds, qb) + _dot_nt(vb, dst_b) * kd
            dstate[h] = dst * cd + _dot_tn((qh * qd).astype(BF16), dob)
            dseg_ref[:, 2 * D_MODEL + HEAD_DIM * h:2 * D_MODEL + HEAD_DIM * (h + 1)] = dv.astype(BF16)
            dseg_ref[:, sl] = _rot_t(dq, cos_t, sin_t).astype(BF16)
            dseg_ref[:, D_MODEL + HEAD_DIM * h:D_MODEL + HEAD_DIM * (h + 1)] = (
                _rot_t(dk, cos_t, sin_t) * QK_SCALE).astype(BF16)

        @pl.when(pl.program_id(0) == n_chunks - 1)
        def _():
            push.wait()

    rev = lambda s: n_chunks - 1 - s
    rowb = pl.BlockSpec((CHUNK, D_MODEL), lambda s: (rev(s), 0))
    seg = lambda k: pl.BlockSpec((CHUNK, D_MODEL), lambda s, k=k: (rev(s), k))
    tab = pl.BlockSpec((CHUNK, HEAD_DIM), lambda s: (rev(s), 0))
    return pl.pallas_call(
        body, name="retention_bwd", grid=(n_chunks,),
        in_specs=[rowb, rowb, seg(0), seg(1), seg(2), seg(3),
                  pl.BlockSpec((None, HEADS, HEAD_DIM, HEAD_DIM), lambda s: (rev(s), 0, 0, 0)), tab, tab,
                  pl.BlockSpec((4, HEADS, CHUNK, CHUNK), lambda s: (0, 0, 0, 0)), ANY] + ride.specs(),
        out_specs=[pl.BlockSpec((CHUNK, 4 * D_MODEL), lambda s: (rev(s), 0))] + ride.specs(),
        out_shape=[jax.ShapeDtypeStruct(dproj.shape, dproj.dtype)] + ride.out_shapes,
        input_output_aliases={10: 0},
        scratch_shapes=[pltpu.VMEM((HEADS, HEAD_DIM, HEAD_DIM), F32)] + ride.scratch(),
        compiler_params=pltpu.CompilerParams(dimension_semantics=("arbitrary",), vmem_limit_bytes=VMEM_LIMIT,
                                             has_side_effects=True),
    )(dzr, o, proj, proj, proj, proj, states, cos2, sin2, dec, dproj, *ride.arrays)


S2_SHAPES = [
    jax.ShapeDtypeStruct((N_DEV, D_MODEL, D_MODEL), BF16),
    jax.ShapeDtypeStruct((N_DEV, LRU_BLOCKS, LRU_ROWS, LRU_BLOCK), F32),
    jax.ShapeDtypeStruct((N_DEV, LRU_BLOCKS, LRU_ROWS, LRU_BLOCK), F32),
]


def _s2_parts(ins, p):
    rows_p = pl.ds(pl.multiple_of(p * LRU_ROWS, 8), LRU_ROWS)
    return [ins[0].at[p], ins[1].at[:, rows_p, :], ins[2].at[:, rows_p, :]]


def _in_proj_bwd(dproj, win_g, h0, norm_w, dh1, s2_grads):
    rows = h0.shape[0]
    tm = _tile(rows, 640)
    n_i = rows // tm
    n_s2 = len(s2_grads)

    def body(dseg_ref, w_ref, h0_ref, nw_ref, dh1_ref, *refs):
        s2_refs = refs[:n_s2]
        dh0_ref, dw_ref = refs[n_s2:n_s2 + 2]
        land_refs = refs[n_s2 + 2:2 * n_s2 + 2]
        acc, send_sems, recv_sems, loc_sems = refs[2 * n_s2 + 2:]
        i, j = pl.program_id(0), pl.program_id(1)
        push = _Push(lambda p: _s2_parts(s2_refs, p), lambda s: [r.at[s] for r in land_refs],
                     (send_sems, recv_sems, loc_sems), n_s2)

        @pl.when(jnp.logical_and(i == 0, j == 0))
        def _():
            push.start()
            dw_ref[...] = jnp.zeros_like(dw_ref)

        @pl.when(j == 0)
        def _():
            acc[...] = jnp.zeros_like(acc)

        acc[...] += _dot_nt(dseg_ref[...], w_ref[...])

        @pl.when(j == N_DEV - 1)
        def _():
            dx, dw = _rms_bwd(h0_ref[...], nw_ref[...], acc[...])
            dw_ref[0:1, :] += dw
            dh0_ref[...] = dh1_ref[...] + dx

        @pl.when(jnp.logical_and(i == n_i - 1, j == N_DEV - 1))
        def _():
            push.wait()

    row = pl.BlockSpec((tm, D_MODEL), lambda i, j: (i, 0))
    vec = pl.BlockSpec((1, D_MODEL), lambda i, j: (0, 0))
    return pl.pallas_call(
        body, name="in_proj_bwd", grid=(n_i, N_DEV),
        in_specs=[pl.BlockSpec((tm, D_MODEL), lambda i, j: (i, j)),
                  pl.BlockSpec((None, D_MODEL, D_MODEL), lambda i, j: (j, 0, 0)), row, vec, row] + [ANY] * n_s2,
        out_specs=[row, pl.BlockSpec((8, D_MODEL), lambda i, j: (0, 0))] + [ANY] * n_s2,
        out_shape=[jax.ShapeDtypeStruct((rows, D_MODEL), F32), jax.ShapeDtypeStruct((8, D_MODEL), F32)] + S2_SHAPES,
        scratch_shapes=[pltpu.VMEM((tm, D_MODEL), F32)] + _push_sems(n_s2),
        compiler_params=pltpu.CompilerParams(dimension_semantics=("arbitrary", "arbitrary"),
                                             vmem_limit_bytes=VMEM_LIMIT, has_side_effects=True),
    )(dproj, win_g, h0, norm_w, dh1, *s2_grads)


def _adamw(g_slots, w, m, v):
    slots, rows, cols = g_slots.shape
    tr = rows
    for cand in (256, 128, 64, 32, 16, 8):
        if rows % cand == 0 and rows > cand:
            tr = cand
            break

    def body(g_ref, w_ref, m_ref, v_ref, go_ref, d_ref, mo_ref, vo_ref):
        g = g_ref[0].astype(F32)
        for s in range(1, slots):
            g = g + g_ref[s].astype(F32)
        m2 = ADAM_B1 * m_ref[...] + (1.0 - ADAM_B1) * g
        v2 = ADAM_B2 * v_ref[...] + (1.0 - ADAM_B2) * (g * g)
        m_hat = m2 / (1.0 - ADAM_B1 ** ADAM_STEP)
        v_hat = v2 / (1.0 - ADAM_B2 ** ADAM_STEP)
        go_ref[...] = g
        d_ref[...] = -ADAM_LR * (m_hat / (jnp.sqrt(v_hat) + ADAM_EPS) + ADAM_WD * w_ref[...])
        mo_ref[...] = m2
        vo_ref[...] = v2

    blk = pl.BlockSpec((tr, cols), lambda i: (i, 0))
    shape = jax.ShapeDtypeStruct((rows, cols), F32)
    return pl.pallas_call(
        body, name="adamw", grid=(rows // tr,),
        in_specs=[pl.BlockSpec((slots, tr, cols), lambda i: (0, i, 0)), blk, blk, blk],
        out_specs=[blk] * 4, out_shape=[shape] * 4,
        compiler_params=_cparams(("parallel",)),
    )(g_slots, w, m, v)


def _sum_slots(packs):
    slots, rows, cols = packs.shape

    def body(p_ref, o_ref):
        acc = p_ref[0]
        for s in range(1, slots):
            acc = acc + p_ref[s]
        o_ref[...] = acc

    return pl.pallas_call(
        body, name="sum_slots", out_shape=jax.ShapeDtypeStruct((rows, cols), F32),
        compiler_params=pltpu.CompilerParams(vmem_limit_bytes=VMEM_LIMIT),
    )(packs)


def _gather_first(win_shard, small):
    shapes = [jax.ShapeDtypeStruct((N_DEV,) + win_shard.shape, BF16), jax.ShapeDtypeStruct((N_DEV,) + small.shape, F32)]
    return _push_call("gather_first", [win_shard, small], shapes,
                      lambda ins, p: list(ins), lambda outs, s: [r.at[s] for r in outs])


def _share_pack(pack):
    shapes = [jax.ShapeDtypeStruct((N_DEV,) + pack.shape, F32)]
    return _push_call("share_pack", [pack], shapes,
                      lambda ins, p: list(ins), lambda outs, s: [r.at[s] for r in outs])[0]


PACK_MIX_NORM, PACK_CONV_W, PACK_CONV_B, PACK_BA, PACK_BX, PACK_LAM = 0, 8, 12, 13, 14, 15
PACK_FFN_NORM, PACK_SQ_ERR, PACK_FINAL_NORM, PACK_META = 16, 24, 25, 32


def kernel(x, meta_tokens, mix_norm_w, w_in, conv_w, conv_b, lru_wa, lru_ba, lru_wx, lru_bx, lru_lambda, w_branch_ret, w_branch_lru, w_out, ffn_norm_w, w_ffn_in, w_ffn_out, final_norm_w, loss_target, m_meta_tokens, m_mix_norm_w, m_w_in, m_conv_w, m_conv_b, m_lru_wa, m_lru_ba, m_lru_wx, m_lru_bx, m_lru_lambda, m_w_branch_ret, m_w_branch_lru, m_w_out, m_ffn_norm_w, m_w_ffn_in, m_w_ffn_out, m_final_norm_w, v_meta_tokens, v_mix_norm_w, v_w_in, v_conv_w, v_conv_b, v_lru_wa, v_lru_ba, v_lru_wx, v_lru_bx, v_lru_lambda, v_w_branch_ret, v_w_branch_lru, v_w_out, v_ffn_norm_w, v_w_ffn_in, v_w_ffn_out, v_final_norm_w):
    me = _my_index()
    pad4 = ((0, 4), (0, 0))
    fw = final_norm_w.reshape(1, D_MODEL)

    small = jnp.concatenate([meta_tokens, jnp.pad(conv_w[0], pad4)], axis=0)
    win_g, small_g = _gather_first(w_in[0].astype(BF16), small)
    meta_full = small_g[:, :N_META].transpose(1, 0, 2).reshape(N_META, D_MODEL)
    conv_w_full = small_g[:, N_META:N_META + 4].transpose(1, 0, 2).reshape(4, D_MODEL)
    rest_shards = [w_branch_ret[0].astype(BF16), w_branch_lru[0].astype(BF16), w_out[0].astype(BF16),
                   w_ffn_out[0].astype(BF16), lru_wa[0].astype(BF16), lru_wx[0].astype(BF16)]
    wfi_shard = jnp.pad(w_ffn_in[0].astype(BF16), ((0, 0), (0, FFN_GROUP - FFN_SHARD)))
    own_slot = lambda ins, p: list(ins)
    part_of_owner = lambda ins, p: [r.at[p] for r in ins]

    rows = x.shape[1] + CHUNK
    h0 = jnp.concatenate([jnp.zeros((PAD_ROWS, D_MODEL), F32), meta_full, x[0]], axis=0)
    tgt = jnp.concatenate([jnp.zeros((CHUNK, D_MODEL), F32), loss_target[0]], axis=0)
    cos2, sin2 = _rope_tables(rows)
    dec = _retention_consts()

    proj, u, wbr_g, wbl_g, wout_g, wfo_g, wa_g, wx_g = _in_proj(h0, mix_norm_w, win_g, rest_shards)
    wbr, wbl, wout = (t.reshape(D_MODEL, D_MODEL) for t in (wbr_g, wbl_g, wout_g))
    gather_wfi = _Ride([wfi_shard], [jax.ShapeDtypeStruct((N_DEV, D_MODEL, FFN_GROUP), BF16)],
                       own_slot, _slot_of_sender)
    o, zr, states, wfi_g = _retention_fwd(proj, cos2, sin2, dec, gather_wfi)
    hs, zl = _lru_fwd(proj, conv_w_full, conv_b, lru_ba, lru_bx, lru_lambda, wa_g, wx_g)
    h1, yr, yl, mixed = _mix_fwd(zr, zl, proj, h0, wbr, wbl, wout)
    u2, g, up, act, dh2, red = _ffn_fwd_loss(h1, ffn_norm_w, wfi_g, wfo_g, fw, tgt)

    d_wfo = _wgrad(act, dh2, FFN_GROUP, D_MODEL, BF16)[:, 0]
    dgu, dh1, dw_ffn_norm = _ffn_bwd(dh2, g, up, h1, ffn_norm_w, wfi_g, wfo_g)
    d_wfi = _wgrad(u2, dgu, D_MODEL, FFN_GROUP, BF16, b_halves=True)[0]
    d_wout = _wgrad(mixed, dh1, D_MODEL, D_MODEL, BF16)[0, 0]
    dyr, dyl, dproj, dzr, dzl = _mix_bwd(dh1, yr, yl, proj, wbr, wbl, wout)
    d_wbr = _wgrad(zr, dyr, D_MODEL, D_MODEL, BF16)[0, 0]
    d_wbl = _wgrad(zl, dyl, D_MODEL, D_MODEL, BF16)[0, 0]
    dproj, d_wa, d_wx, lru_small, r_fi, r_fo = _lru_bwd(
        dzl, hs, proj, dproj, conv_w_full, conv_b, lru_ba, lru_bx, lru_lambda, wa_g, wx_g, [d_wfi, d_wfo])
    mix_shape = jax.ShapeDtypeStruct((N_DEV, D_MODEL // N_DEV, D_MODEL), BF16)
    scatter_mix = _Ride([t.reshape(mix_shape.shape) for t in (d_wbr, d_wbl, d_wout)], [mix_shape] * 3,
                        part_of_owner, _slot_of_sender)
    dproj, r_br, r_bl, r_out = _retention_bwd(dzr, o, proj, states, cos2, sin2, dec, dproj, scatter_mix)
    d_win = _wgrad(u, dproj, D_MODEL, D_MODEL, BF16)[0]
    dh0, dw_mix_norm, r_in, r_wa, r_wx = _in_proj_bwd(dproj, win_g, h0, mix_norm_w, dh1, [d_win, d_wa, d_wx])
    grad_x = dh0[CHUNK:]

    pack = jnp.concatenate([dw_mix_norm, lru_small, dw_ffn_norm, red, dh0[PAD_ROWS:CHUNK]], axis=0)
    small_sum = _sum_slots(_share_pack(pack))
    loss = (0.5 / D_MODEL) * jnp.sum(small_sum[PACK_SQ_ERR])

    def big_update(slots, w, m, v):
        shape = w.shape
        w2, m2, v2 = (t.reshape(slots.shape[1:]) for t in (w, m, v))
        return [t.reshape(shape) for t in _adamw(slots, w2, m2, v2)]

    res = {}
    res["w_in"] = big_update(r_in, w_in, m_w_in, v_w_in)
    res["w_branch_ret"] = big_update(r_br, w_branch_ret, m_w_branch_ret, v_w_branch_ret)
    res["w_branch_lru"] = big_update(r_bl, w_branch_lru, m_w_branch_lru, v_w_branch_lru)
    res["w_out"] = big_update(r_out, w_out, m_w_out, v_w_out)
    res["w_ffn_in"] = big_update(r_fi[:, :, :FFN_SHARD], w_ffn_in, m_w_ffn_in, v_w_ffn_in)
    res["w_ffn_out"] = big_update(r_fo, w_ffn_out, m_w_ffn_out, v_w_ffn_out)
    res["lru_wa"] = big_update(r_wa.reshape(N_DEV, LRU_BLOCKS * LRU_ROWS, LRU_BLOCK), lru_wa, m_lru_wa, v_lru_wa)
    res["lru_wx"] = big_update(r_wx.reshape(N_DEV, LRU_BLOCKS * LRU_ROWS, LRU_BLOCK), lru_wx, m_lru_wx, v_lru_wx)

    col = me * HEAD_DIM
    g_meta = lax.dynamic_slice(small_sum, (PACK_META, col), (N_META, HEAD_DIM))
    g_conv = lax.dynamic_slice(small_sum, (PACK_CONV_W, col), (8, HEAD_DIM))
    small_names = ["mix_norm_w", "conv_b", "lru_ba", "lru_bx", "lru_lambda", "ffn_norm_w", "final_norm_w"]
    small_rows = [PACK_MIX_NORM, PACK_CONV_B, PACK_BA, PACK_BX, PACK_LAM, PACK_FFN_NORM, PACK_FINAL_NORM]
    small_w = [mix_norm_w, conv_b, lru_ba, lru_bx, lru_lambda, ffn_norm_w, fw]
    small_m = [m_mix_norm_w, m_conv_b, m_lru_ba, m_lru_bx, m_lru_lambda, m_ffn_norm_w, m_final_norm_w.reshape(1, -1)]
    small_v = [v_mix_norm_w, v_conv_b, v_lru_ba, v_lru_bx, v_lru_lambda, v_ffn_norm_w, v_final_norm_w.reshape(1, -1)]

    def pack_small(vec_list, meta_t, conv_t):
        return jnp.concatenate([t.reshape(8, HEAD_DIM) for t in vec_list] + [meta_t, jnp.pad(conv_t[0], pad4)], axis=0)

    g_small = jnp.concatenate([small_sum[r].reshape(8, HEAD_DIM) for r in small_rows] + [g_meta, g_conv], axis=0)
    outs_small = _adamw(g_small[None], pack_small(small_w, meta_tokens, conv_w),
                        pack_small(small_m, m_meta_tokens, m_conv_w), pack_small(small_v, v_meta_tokens, v_conv_w))
    for idx, name in enumerate(small_names):
        shape = final_norm_w.shape if name == "final_norm_w" else (1, D_MODEL)
        res[name] = [t[8 * idx:8 * idx + 8].reshape(shape) for t in outs_small]
    res["meta_tokens"] = [t[56:72] for t in outs_small]
    res["conv_w"] = [t[72:76].reshape(1, 4, HEAD_DIM) for t in outs_small]

    order = ["meta_tokens", "mix_norm_w", "w_in", "conv_w", "conv_b", "lru_wa", "lru_ba", "lru_wx", "lru_bx",
             "lru_lambda", "w_branch_ret", "w_branch_lru", "w_out", "ffn_norm_w", "w_ffn_in", "w_ffn_out",
             "final_norm_w"]
    out = [loss, grad_x[None]]
    for kind in range(4):
        out += [res[name][kind] for name in order]
    return tuple(out)
```

```python
import functools

import numpy as np
import jax
import jax.numpy as jnp
from jax import lax
from jax.experimental import pallas as pl
from jax.experimental.pallas import tpu as pltpu

F32 = jnp.float32
BF16 = jnp.bfloat16

D_MODEL = 1024
N_META = 16
CHUNK = 128
PAD_ROWS = CHUNK - N_META
HEADS = 8
HEAD_DIM = 128
ROPE_BASE = 10000.0
QK_SCALE = HEAD_DIM ** -0.5
LRU_BLOCKS = 4
LRU_BLOCK = 256
LRU_C = 8.0
FFN_HIDDEN = 2816
N_DEV = 8
FFN_SHARD = 2 * FFN_HIDDEN // N_DEV
FFN_GROUP = 768
FFN_GROUPS = 4
FFN_OUT_SHARD = FFN_HIDDEN // N_DEV
NORM_EPS = 1e-6

ADAM_LR = 0.001
ADAM_B1 = 0.9
ADAM_B2 = 0.999
ADAM_EPS = 1e-08
ADAM_WD = 0.01
ADAM_STEP = 10

VMEM_LIMIT = 56 * 1024 * 1024
MESH_ID = pl.DeviceIdType.MESH
ANY = pl.BlockSpec(memory_space=pl.ANY)


def _cparams(sem):
    return pltpu.CompilerParams(dimension_semantics=sem, vmem_limit_bytes=VMEM_LIMIT)


def _tile(rows, cap):
    t = cap - cap % 64
    while rows % t:
        t -= 64
    return t


def _dot(a, b):
    return jnp.dot(a, b, preferred_element_type=F32)


def _dot_nt(a, b):
    return lax.dot_general(a, b, (((1,), (1,)), ((), ())), preferred_element_type=F32)


def _dot_tn(a, b):
    return lax.dot_general(a, b, (((0,), (0,)), ((), ())), preferred_element_type=F32)


def _sigmoid(x):
    return 1.0 / (1.0 + jnp.exp(-x))


def _gelu_parts(x):
    k = 0.7978845608028654
    inner = k * (x + 0.044715 * x * x * x)
    t = jnp.tanh(inner)
    g = 0.5 * x * (1.0 + t)
    dg = 0.5 * (1.0 + t) + 0.5 * x * (1.0 - t * t) * k * (1.0 + 3.0 * 0.044715 * x * x)
    return g, dg


def _rot(x, cos2, sin2):
    return x * cos2 + pltpu.roll(x, HEAD_DIM // 2, 1) * sin2


def _rot_t(dx, cos2, sin2):
    return dx * cos2 - pltpu.roll(dx, HEAD_DIM // 2, 1) * sin2


def _rms_bwd(x, w, dy):
    rs = lax.rsqrt(jnp.mean(x * x, axis=-1, keepdims=True) + NORM_EPS)
    nh = x * rs
    dw = jnp.sum(dy * nh, axis=0, keepdims=True)
    dn = dy * w
    dx = rs * (dn - nh * jnp.mean(dn * nh, axis=-1, keepdims=True))
    return dx, dw


def _retention_consts():
    h = jnp.arange(HEADS, dtype=F32)
    log_g = jnp.log(1.0 - 2.0 ** (-5.0 - h))
    idx = jnp.arange(CHUNK, dtype=F32)
    diff = idx[:, None] - idx[None, :]
    intra = jnp.where(diff[None] >= 0, jnp.exp(jnp.maximum(diff, 0.0)[None] * log_g[:, None, None]), 0.0)
    q_decay = jnp.exp((idx + 1.0)[:, None] * log_g[None, :])
    k_decay = jnp.exp((CHUNK - 1.0 - idx)[:, None] * log_g[None, :])
    chunk_decay = jnp.exp(CHUNK * log_g)
    shape = (HEADS, CHUNK, CHUNK)
    qd = jnp.broadcast_to(q_decay.T[:, :, None], shape)
    kd = jnp.broadcast_to(k_decay.T[:, :, None], shape)
    cd = jnp.broadcast_to(chunk_decay[:, None, None], shape)
    return jnp.stack([intra, qd, kd, cd])


def _rope_tables(rows):
    pos = jnp.maximum(jnp.arange(rows) - PAD_ROWS, 0).astype(F32)
    inv_freq = ROPE_BASE ** (-jnp.arange(0, HEAD_DIM, 2, dtype=F32) / HEAD_DIM)
    ang = pos[:, None] * inv_freq[None, :]
    cos, sin = jnp.cos(ang), jnp.sin(ang)
    return jnp.concatenate([cos, cos], axis=1), jnp.concatenate([-sin, sin], axis=1)


def _my_index():
    return 4 * lax.axis_index("x") + 2 * lax.axis_index("y") + lax.axis_index("c")


def _peer(k):
    x, y, c = lax.axis_index("x"), lax.axis_index("y"), lax.axis_index("c")
    px = 1 - x if k & 4 else x
    py = 1 - y if k & 2 else y
    pc = 1 - c if k & 1 else c
    return (px, py, pc), 4 * px + 2 * py + pc


def _push_sems(n_arr):
    n_rem = (N_DEV - 1) * n_arr
    return [pltpu.SemaphoreType.DMA((n_rem,)), pltpu.SemaphoreType.DMA((n_rem,)), pltpu.SemaphoreType.DMA((n_arr,))]


class _Push:
    def __init__(self, send_part, land_slot, sems, n_arr):
        self.send_part, self.land_slot, self.n_arr = send_part, land_slot, n_arr
        self.send_sems, self.recv_sems, self.loc_sems = sems

    def _remote(self, k, a, src, dst, pos):
        idx = (k - 1) * self.n_arr + a
        return pltpu.make_async_remote_copy(src_ref=src, dst_ref=dst, send_sem=self.send_sems.at[idx],
                                            recv_sem=self.recv_sems.at[idx], device_id=pos, device_id_type=MESH_ID)

    def _outgoing(self):
        me = _my_index()
        land = self.land_slot(me)
        remote = []
        for k in range(1, N_DEV):
            pos, p = _peer(k)
            src = self.send_part(p)
            remote += [self._remote(k, a, src[a], land[a], pos) for a in range(self.n_arr)]
        own = self.send_part(me)
        local = [pltpu.make_async_copy(own[a], land[a], self.loc_sems.at[a]) for a in range(self.n_arr)]
        return remote, local

    def start(self):
        remote, local = self._outgoing()
        for cp in remote + local:
            cp.start()

    def wait_recv_from(self, k):
        own = self.send_part(_my_index())
        pos, p = _peer(k)
        land = self.land_slot(p)
        for a in range(self.n_arr):
            self._remote(k, a, own[a], land[a], pos).wait_recv()

    def wait_sends(self):
        remote, local = self._outgoing()
        for cp in remote:
            cp.wait_send()
        for cp in local:
            cp.wait()

    def wait(self):
        for k in range(1, N_DEV):
            self.wait_recv_from(k)
        self.wait_sends()


class _Ride:
    def __init__(self, arrays, out_shapes, send_part, land_slot, zero_dsts=None, zero_shape=None, n_zero=0):
        self.arrays, self.out_shapes = list(arrays), list(out_shapes)
        self.send_part, self.land_slot, self.n = send_part, land_slot, len(arrays)
        self.zero_dsts, self.zero_shape, self.n_zero = zero_dsts, zero_shape, n_zero

    def specs(self):
        return [ANY] * self.n

    def scratch(self):
        extra = [pltpu.SemaphoreType.DMA((self.n_zero,)), pltpu.VMEM(self.zero_shape, BF16)] if self.n_zero else []
        return _push_sems(self.n) + extra

    def push(self, in_refs, out_refs, scratch):
        ride = self
        push = _Push(lambda p: ride.send_part(in_refs, p), lambda s: ride.land_slot(out_refs, s),
                     tuple(scratch[:3]), self.n)

        class Both:
            def _fills(self):
                if not ride.n_zero:
                    return []
                zsems, zbuf = scratch[3], scratch[4]
                return [pltpu.make_async_copy(zbuf, dst, zsems.at[z]) for z, dst in enumerate(ride.zero_dsts(out_refs))]

            def start(self):
                push.start()
                if ride.n_zero:
                    scratch[4][...] = jnp.zeros(ride.zero_shape, BF16)
                for cp in self._fills():
                    cp.start()

            def wait(self):
                push.wait()
                for cp in self._fills():
                    cp.wait()

        return Both()


def _slot_of_sender(out_refs, s):
    return [r.at[s] for r in out_refs]


def _push_call(name, arrays, out_shapes, send_part, land_slot):
    n_arr = len(arrays)

    def body(*refs):
        ins, outs, sems = refs[:n_arr], refs[n_arr:2 * n_arr], refs[2 * n_arr:]
        push = _Push(lambda p: send_part(ins, p), lambda s: land_slot(outs, s), sems, n_arr)
        push.start()
        push.wait()

    return pl.pallas_call(
        body, name=name, in_specs=[ANY] * n_arr, out_specs=[ANY] * n_arr, out_shape=out_shapes,
        scratch_shapes=_push_sems(n_arr), compiler_params=pltpu.CompilerParams(has_side_effects=True),
    )(*arrays)


LRU_ROWS = LRU_BLOCK // N_DEV
FFN_PAD_ROWS = FFN_GROUP - 2 * FFN_OUT_SHARD


def _half_rows(d):
    return pl.ds(pl.multiple_of((d % 2) * FFN_OUT_SHARD, 16), FFN_OUT_SHARD)


def _lru_rows(d):
    return pl.ds(pl.multiple_of(d * LRU_ROWS, 16), LRU_ROWS)


def _rest_slots(outs, d):
    return [outs[0].at[d], outs[1].at[d], outs[2].at[d],
            outs[3].at[d // 2, _half_rows(d), :], outs[4].at[:, _lru_rows(d), :], outs[5].at[:, _lru_rows(d), :]]


REST_WFO = 3
REST_SHAPES = [
    jax.ShapeDtypeStruct((N_DEV, D_MODEL // N_DEV, D_MODEL), BF16),
    jax.ShapeDtypeStruct((N_DEV, D_MODEL // N_DEV, D_MODEL), BF16),
    jax.ShapeDtypeStruct((N_DEV, D_MODEL // N_DEV, D_MODEL), BF16),
    jax.ShapeDtypeStruct((FFN_GROUPS, FFN_GROUP, D_MODEL), BF16),
    jax.ShapeDtypeStruct((LRU_BLOCKS, LRU_BLOCK, LRU_BLOCK), BF16),
    jax.ShapeDtypeStruct((LRU_BLOCKS, LRU_BLOCK, LRU_BLOCK), BF16),
]


def _rest_ride(rest_shards):
    zero_dsts = lambda outs: [outs[REST_WFO].at[g, pl.ds(2 * FFN_OUT_SHARD, FFN_PAD_ROWS), :]
                              for g in range(FFN_GROUPS)]
    return _Ride(rest_shards, REST_SHAPES, lambda ins, p: list(ins), _rest_slots,
                 zero_dsts, (FFN_PAD_ROWS, D_MODEL), FFN_GROUPS)


def _arrival_rank_to_relation(jj):
    return jnp.where(jj == 3, 4, jnp.where(jj == 4, 3, jj))


def _in_proj(h0, norm_w, win_shard, me_arr):
    rows = h0.shape[0]
    tm = _tile(rows, 640)
    n_i = rows // tm

    def body(me_ref, h_ref, nw_ref, wsh_ref, proj_ref, u_ref, wing_ref, u_all, wbuf, copy_sem,
             send_sems, recv_sems, loc_sems):
        del me_ref
        jj, i = pl.program_id(0), pl.program_id(1)
        push = _Push(lambda p: [wsh_ref], lambda s: [wing_ref.at[s]], (send_sems, recv_sems, loc_sems), 1)

        @pl.when(jnp.logical_and(jj == 0, i == 0))
        def _():
            push.start()
            own = pltpu.make_async_copy(wsh_ref, wbuf, copy_sem)
            own.start()
            own.wait()

        for k in range(1, N_DEV):
            rank = {3: 4, 4: 3}.get(k, k)

            @pl.when(jnp.logical_and(jj == rank, i == 0))
            def _(k=k):
                push.wait_recv_from(k)
                landed = pltpu.make_async_copy(wing_ref.at[_peer(k)[1]], wbuf, copy_sem)
                landed.start()
                landed.wait()

        rows_i = pl.ds(pl.multiple_of(i * tm, tm), tm)

        @pl.when(jj == 0)
        def _():
            x = h_ref[...]
            rs = lax.rsqrt(jnp.mean(x * x, axis=-1, keepdims=True) + NORM_EPS)
            u = (x * rs * nw_ref[...]).astype(BF16)
            u_all[rows_i, :] = u
            u_ref[...] = u
        proj_ref[...] = _dot(u_all[rows_i, :], wbuf[...]).astype(BF16)

        @pl.when(jnp.logical_and(jj == N_DEV - 1, i == n_i - 1))
        def _():
            push.wait_sends()

    first_pass = lambda jj, i: jnp.where(jj == 0, i, n_i - 1)
    grid_spec = pltpu.PrefetchScalarGridSpec(
        num_scalar_prefetch=1, grid=(N_DEV, n_i),
        in_specs=[pl.BlockSpec((tm, D_MODEL), lambda jj, i, me: (first_pass(jj, i), 0)),
                  pl.BlockSpec((1, D_MODEL), lambda jj, i, me: (0, 0)), ANY],
        out_specs=[pl.BlockSpec((tm, D_MODEL), lambda jj, i, me: (i, me[0] ^ _arrival_rank_to_relation(jj))),
                   pl.BlockSpec((tm, D_MODEL), lambda jj, i, me: (first_pass(jj, i), 0)), ANY],
        scratch_shapes=[pltpu.VMEM((rows, D_MODEL), BF16), pltpu.VMEM((D_MODEL, D_MODEL), BF16),
                        pltpu.SemaphoreType.DMA(())] + _push_sems(1))
    return pl.pallas_call(
        body, name="in_proj", grid_spec=grid_spec,
        out_shape=[jax.ShapeDtypeStruct((rows, N_DEV * D_MODEL), BF16),
                   jax.ShapeDtypeStruct((rows, D_MODEL), BF16),
                   jax.ShapeDtypeStruct((N_DEV, D_MODEL, D_MODEL), BF16)],
        compiler_params=pltpu.CompilerParams(dimension_semantics=("arbitrary", "arbitrary"),
                                             vmem_limit_bytes=VMEM_LIMIT, has_side_effects=True),
    )(me_arr, h0, norm_w, win_shard)


def _seg_spec(rows_per_block, seg):
    return pl.BlockSpec((rows_per_block, D_MODEL), lambda n, seg=seg: (n, seg))


def _retention_fwd(proj, cos2, sin2, dec, ride):
    rows = proj.shape[0]
    n_chunks = rows // CHUNK
    n_r = ride.n

    def body(q_ref, k_ref, v_ref, g_ref, c_ref, s_ref, dec_ref, *refs):
        o_ref, zr_ref, st_ref = refs[n_r:n_r + 3]
        state = refs[2 * n_r + 3]
        push = ride.push(refs[:n_r], refs[n_r + 3:2 * n_r + 3], refs[2 * n_r + 4:])

        @pl.when(pl.program_id(0) == 0)
        def _():
            push.start()
            state[...] = jnp.zeros_like(state)
        cos_t, sin_t = c_ref[...], s_ref[...]
        for h in range(HEADS):
            sl = slice(HEAD_DIM * h, HEAD_DIM * (h + 1))
            qh = _rot(q_ref[:, sl].astype(F32), cos_t, sin_t)
            kh = _rot(k_ref[:, sl].astype(F32), cos_t, sin_t) * QK_SCALE
            qb, kb, vb = qh.astype(BF16), kh.astype(BF16), v_ref[:, sl]
            s = _dot_nt(qb, kb) * dec_ref[0, h]
            st = state[h]
            st_ref[h] = st
            o = _dot(s.astype(BF16), vb) + _dot(qb, st.astype(BF16)) * dec_ref[1, h]
            state[h] = st * dec_ref[3, h] + _dot_tn((kh * dec_ref[2, h]).astype(BF16), vb)
            o_ref[:, sl] = o.astype(BF16)
            r = lax.rsqrt(jnp.mean(o * o, axis=-1, keepdims=True) + NORM_EPS)
            g = g_ref[:, sl].astype(F32)
            zr_ref[:, sl] = (g * _sigmoid(g) * (o * r)).astype(BF16)

        @pl.when(pl.program_id(0) == n_chunks - 1)
        def _():
            push.wait()

    tab = pl.BlockSpec((CHUNK, HEAD_DIM), lambda n: (n, 0))
    return pl.pallas_call(
        body, name="retention_fwd", grid=(n_chunks,),
        in_specs=[_seg_spec(CHUNK, 0), _seg_spec(CHUNK, 1), _seg_spec(CHUNK, 2), _seg_spec(CHUNK, 3), tab, tab,
                  pl.BlockSpec((4, HEADS, CHUNK, CHUNK), lambda n: (0, 0, 0, 0))] + ride.specs(),
        out_specs=[pl.BlockSpec((CHUNK, D_MODEL), lambda n: (n, 0)),
                   pl.BlockSpec((CHUNK, D_MODEL), lambda n: (n, 0)),
                   pl.BlockSpec((None, HEADS, HEAD_DIM, HEAD_DIM), lambda n: (n, 0, 0, 0))] + ride.specs(),
        out_shape=[jax.ShapeDtypeStruct((rows, D_MODEL), BF16),
                   jax.ShapeDtypeStruct((rows, D_MODEL), BF16),
                   jax.ShapeDtypeStruct((n_chunks, HEADS, HEAD_DIM, HEAD_DIM), F32)] + ride.out_shapes,
        scratch_shapes=[pltpu.VMEM((HEADS, HEAD_DIM, HEAD_DIM), F32)] + ride.scratch(),
        compiler_params=pltpu.CompilerParams(dimension_semantics=("arbitrary",), vmem_limit_bytes=VMEM_LIMIT,
                                             has_side_effects=True),
    )(proj, proj, proj, proj, cos2, sin2, dec, *ride.arrays)


def _lru_gates(c, ba, bx, lam, wa_ref, wx_ref):
    pre_r, pre_i = [], []
    for g in range(LRU_BLOCKS):
        cg = c[:, LRU_BLOCK * g:LRU_BLOCK * (g + 1)].astype(BF16)
        pre_r.append(_dot(cg, wa_ref[g]))
        pre_i.append(_dot(cg, wx_ref[g]))
    r = _sigmoid(jnp.concatenate(pre_r, axis=1) + ba)
    ig = _sigmoid(jnp.concatenate(pre_i, axis=1) + bx)
    sp = jnp.maximum(-lam, 0.0) + jnp.log(1.0 + jnp.exp(-jnp.abs(lam)))
    log_a = -LRU_C * r * sp
    a = jnp.exp(log_a)
    mult = jnp.sqrt(-jnp.tanh(log_a) * (a * a + 1.0))
    return r, ig, a, mult, sp


def _conv_taps(xbuf, tm, cw_ref, cb_ref):
    c = cb_ref[...] + cw_ref[3:4, :] * xbuf[8:8 + tm, :]
    for back in (1, 2, 3):
        c = c + cw_ref[3 - back:4 - back, :] * xbuf[8 - back:8 - back + tm, :]
    return c


def _lru_fwd(proj, conv_w, conv_b, ba, bx, lam, wa_g, wx_g, ride):
    rows = proj.shape[0]
    tm = _tile(rows, 320)
    n_t = rows // tm
    n_r = ride.n

    def body(x_ref, gt_ref, cw_ref, cb_ref, ba_ref, bx_ref, lam_ref, wa_ref, wx_ref, *refs):
        hs_ref, zl_ref = refs[n_r:n_r + 2]
        xbuf, abuf, ubuf, hcar = refs[2 * n_r + 2:2 * n_r + 6]
        push = ride.push(refs[:n_r], refs[n_r + 2:2 * n_r + 2], refs[2 * n_r + 6:])
        i = pl.program_id(0)

        @pl.when(i == 0)
        def _():
            push.start()
            xbuf[0:8, :] = jnp.zeros((8, D_MODEL), F32)
            hcar[...] = jnp.zeros_like(hcar)

        xbuf[8:8 + tm, :] = x_ref[...].astype(F32)
        c = _conv_taps(xbuf, tm, cw_ref, cb_ref)
        xbuf[0:8, :] = xbuf[tm:tm + 8, :]
        r, ig, a, mult, _ = _lru_gates(c, ba_ref[...], bx_ref[...], lam_ref[...], wa_ref, wx_ref)
        row = i * tm + lax.broadcasted_iota(jnp.int32, (tm, 1), 0)
        abuf[...] = a
        ubuf[...] = jnp.where(row >= PAD_ROWS, mult * (ig * c), 0.0)

        sub = lax.broadcasted_iota(jnp.int32, (8, D_MODEL), 0)

        def block(b, carry):
            off = pl.multiple_of(b * 8, 8)
            av, uv = abuf[pl.ds(off, 8), :], ubuf[pl.ds(off, 8), :]
            for s in (1, 2, 4):
                us = jnp.where(sub >= s, pltpu.roll(uv, s, 0), 0.0)
                as_ = jnp.where(sub >= s, pltpu.roll(av, s, 0), 1.0)
                uv = uv + av * us
                av = av * as_
            hv = uv + av * carry
            ubuf[pl.ds(off, 8), :] = hv
            return hv[7:8, :]

        hcar[...] = lax.fori_loop(0, tm // 8, block, hcar[...])
        gl, _ = _gelu_parts(gt_ref[...].astype(F32))
        hs = ubuf[...]
        hs_ref[...] = hs.astype(BF16)
        zl_ref[...] = (gl * hs).astype(BF16)

        @pl.when(i == n_t - 1)
        def _():
            push.wait()

    vec = pl.BlockSpec((1, D_MODEL), lambda i: (0, 0))
    mat = pl.BlockSpec((LRU_BLOCKS, LRU_BLOCK, LRU_BLOCK), lambda i: (0, 0, 0))
    row = pl.BlockSpec((tm, D_MODEL), lambda i: (i, 0))
    return pl.pallas_call(
        body, name="lru_fwd", grid=(n_t,),
        in_specs=[_seg_spec(tm, 4), _seg_spec(tm, 5), pl.BlockSpec((4, D_MODEL), lambda i: (0, 0)),
                  vec, vec, vec, vec, mat, mat] + ride.specs(),
        out_specs=[row, row] + ride.specs(),
        out_shape=[jax.ShapeDtypeStruct((rows, D_MODEL), BF16)] * 2 + ride.out_shapes,
        scratch_shapes=[pltpu.VMEM((tm + 8, D_MODEL), F32), pltpu.VMEM((tm, D_MODEL), F32),
                        pltpu.VMEM((tm, D_MODEL), F32), pltpu.VMEM((1, D_MODEL), F32)] + ride.scratch(),
        compiler_params=pltpu.CompilerParams(dimension_semantics=("arbitrary",), vmem_limit_bytes=VMEM_LIMIT,
                                             has_side_effects=True),
    )(proj, proj, conv_w, conv_b, ba, bx, lam, wa_g, wx_g, *ride.arrays)


def _mix_fwd(zr, zl, proj, h0, wbr, wbl, wout):
    rows = h0.shape[0]
    tm = _tile(rows, 640)

    def body(zr_ref, zl_ref, ga_ref, gb_ref, h0_ref, wbr_ref, wbl_ref, wo_ref, h1_ref, yr_ref, yl_ref, mx_ref):
        yr = _dot(zr_ref[...], wbr_ref[...])
        yl = _dot(zl_ref[...], wbl_ref[...])
        mixed = (_sigmoid(ga_ref[...].astype(F32)) * yr + _sigmoid(gb_ref[...].astype(F32)) * yl).astype(BF16)
        yr_ref[...] = yr.astype(BF16)
        yl_ref[...] = yl.astype(BF16)
        mx_ref[...] = mixed
        h1_ref[...] = h0_ref[...] + _dot(mixed, wo_ref[...])

    row = pl.BlockSpec((tm, D_MODEL), lambda i: (i, 0))
    wsp = pl.BlockSpec((D_MODEL, D_MODEL), lambda i: (0, 0))
    return pl.pallas_call(
        body, name="mix_fwd", grid=(rows // tm,),
        in_specs=[row, row, _seg_spec(tm, 6), _seg_spec(tm, 7), row, wsp, wsp, wsp],
        out_specs=[row, row, row, row],
        out_shape=[jax.ShapeDtypeStruct((rows, D_MODEL), F32)] + [jax.ShapeDtypeStruct((rows, D_MODEL), BF16)] * 3,
        compiler_params=_cparams(("parallel",)),
    )(zr, zl, proj, proj, h0, wbr, wbl, wout)


def _ffn_fwd_loss(h1, norm_w, wfi_g, wfo_g, final_w, target):
    rows = h1.shape[0]
    tm = _tile(rows, 640)
    last = FFN_GROUPS - 1

    def body(h1_ref, nw_ref, wg_ref, wu_ref, wo_ref, fw_ref, t_ref,
             u2_ref, g_ref, up_ref, act_ref, dh2_ref, red_ref, u2_s, acc):
        i, d = pl.program_id(0), pl.program_id(1)

        @pl.when(jnp.logical_and(i == 0, d == 0))
        def _():
            red_ref[...] = jnp.zeros_like(red_ref)

        @pl.when(d == 0)
        def _():
            x = h1_ref[...]
            rs = lax.rsqrt(jnp.mean(x * x, axis=-1, keepdims=True) + NORM_EPS)
            u2 = (x * rs * nw_ref[...]).astype(BF16)
            u2_s[...] = u2
            u2_ref[...] = u2
            acc[...] = jnp.zeros_like(acc)

        g = _dot(u2_s[...], wg_ref[...])
        up = _dot(u2_s[...], wu_ref[...])
        act = (g * _sigmoid(g) * up).astype(BF16)
        g_ref[...] = g.astype(BF16)
        up_ref[...] = up.astype(BF16)
        act_ref[...] = act
        acc[...] += _dot(act, wo_ref[...])

        @pl.when(d == last)
        def _():
            h2 = h1_ref[...] + acc[...]
            rs = lax.rsqrt(jnp.mean(h2 * h2, axis=-1, keepdims=True) + NORM_EPS)
            nh = h2 * rs
            fw = fw_ref[...]
            row = i * tm + lax.broadcasted_iota(jnp.int32, (tm, 1), 0)
            diff = jnp.where(row >= CHUNK, nh * fw - t_ref[...], 0.0)
            dy = diff * (1.0 / D_MODEL)
            red_ref[0:1, :] += jnp.sum(diff * diff, axis=0, keepdims=True)
            red_ref[1:2, :] += jnp.sum(dy * nh, axis=0, keepdims=True)
            dn = dy * fw
            dh2_ref[...] = rs * (dn - nh * jnp.mean(dn * nh, axis=-1, keepdims=True))

    row = pl.BlockSpec((tm, D_MODEL), lambda i, d: (i, 0))
    vec = pl.BlockSpec((1, D_MODEL), lambda i, d: (0, 0))
    hid = pl.BlockSpec((tm, FFN_GROUP), lambda i, d: (i, d))
    hid_shape = jax.ShapeDtypeStruct((rows, FFN_GROUPS * FFN_GROUP), BF16)
    return pl.pallas_call(
        body, name="ffn_fwd_loss", grid=(rows // tm, FFN_GROUPS),
        in_specs=[row, vec,
                  pl.BlockSpec((None, D_MODEL, FFN_GROUP), lambda i, d: (d, 0, 0)),
                  pl.BlockSpec((None, D_MODEL, FFN_GROUP), lambda i, d: (d + FFN_GROUPS, 0, 0)),
                  pl.BlockSpec((None, FFN_GROUP, D_MODEL), lambda i, d: (d, 0, 0)),
                  vec, row],
        out_specs=[row, hid, hid, hid, row, pl.BlockSpec((8, D_MODEL), lambda i, d: (0, 0))],
        out_shape=[jax.ShapeDtypeStruct((rows, D_MODEL), BF16), hid_shape, hid_shape, hid_shape,
                   jax.ShapeDtypeStruct((rows, D_MODEL), F32), jax.ShapeDtypeStruct((8, D_MODEL), F32)],
        scratch_shapes=[pltpu.VMEM((tm, D_MODEL), BF16), pltpu.VMEM((tm, D_MODEL), F32)],
        compiler_params=_cparams(("arbitrary", "arbitrary")),
    )(h1, norm_w, wfi_g, wfi_g, wfo_g, final_w, target)


def _wgrad(a, b, ka, tn, out_dtype, b_halves=False):
    rows = a.shape[0]
    na = a.shape[1] // ka
    tm = _tile(rows, 640)
    nm = rows // tm
    if b_halves:
        per_half = b.shape[2] // tn
        nb = 2 * per_half
        b_spec = pl.BlockSpec((None, tm, tn), lambda p, q, m: (q // per_half, m, q % per_half))
    else:
        nb = b.shape[1] // tn
        b_spec = pl.BlockSpec((tm, tn), lambda p, q, m: (m, q))

    def body(a_ref, b_ref, o_ref, acc):
        m = pl.program_id(2)

        @pl.when(m == 0)
        def _():
            acc[...] = jnp.zeros_like(acc)

        acc[...] += _dot_tn(a_ref[...].astype(BF16), b_ref[...].astype(BF16))

        @pl.when(m == nm - 1)
        def _():
            o_ref[...] = acc[...].astype(out_dtype)

    return pl.pallas_call(
        body, name="wgrad", grid=(na, nb, nm),
        in_specs=[pl.BlockSpec((tm, ka), lambda p, q, m: (m, p)), b_spec],
        out_specs=pl.BlockSpec((None, None, ka, tn), lambda p, q, m: (p, q, 0, 0)),
        out_shape=jax.ShapeDtypeStruct((na, nb, ka, tn), out_dtype),
        scratch_shapes=[pltpu.VMEM((ka, tn), F32)],
        compiler_params=_cparams(("parallel", "parallel", "arbitrary")),
    )(a, b)


def _ffn_bwd(dh2, g, up, h1, norm_w, wfi_g, wfo_g):
    rows = h1.shape[0]
    tm = _tile(rows, 640)
    last = FFN_GROUPS - 1

    def body(dh2_ref, g_ref, up_ref, h1_ref, nw_ref, wg_ref, wu_ref, wo_ref, dgu_ref, dh1_ref, dw_ref,
             dh2_s, acc):
        i, d = pl.program_id(0), pl.program_id(1)

        @pl.when(jnp.logical_and(i == 0, d == 0))
        def _():
            dw_ref[...] = jnp.zeros_like(dw_ref)

        @pl.when(d == 0)
        def _():
            dh2_s[...] = dh2_ref[...].astype(BF16)
            acc[...] = jnp.zeros_like(acc)

        dact = _dot_nt(dh2_s[...], wo_ref[...])
        gv, uv = g_ref[...].astype(F32), up_ref[...].astype(F32)
        sg = _sigmoid(gv)
        dg = (dact * uv * (sg * (1.0 + gv * (1.0 - sg)))).astype(BF16)
        dup = (dact * (gv * sg)).astype(BF16)
        dgu_ref[0] = dg
        dgu_ref[1] = dup
        acc[...] += _dot_nt(dg, wg_ref[...]) + _dot_nt(dup, wu_ref[...])

        @pl.when(d == last)
        def _():
            dx, dw = _rms_bwd(h1_ref[...], nw_ref[...], acc[...])
            dw_ref[0:1, :] += dw
            dh1_ref[...] = dh2_ref[...] + dx

    row = pl.BlockSpec((tm, D_MODEL), lambda i, d: (i, 0))
    vec = pl.BlockSpec((1, D_MODEL), lambda i, d: (0, 0))
    hid = pl.BlockSpec((tm, FFN_GROUP), lambda i, d: (i, d))
    return pl.pallas_call(
        body, name="ffn_bwd", grid=(rows // tm, FFN_GROUPS),
        in_specs=[row, hid, hid, row, vec,
                  pl.BlockSpec((None, D_MODEL, FFN_GROUP), lambda i, d: (d, 0, 0)),
                  pl.BlockSpec((None, D_MODEL, FFN_GROUP), lambda i, d: (d + FFN_GROUPS, 0, 0)),
                  pl.BlockSpec((None, FFN_GROUP, D_MODEL), lambda i, d: (d, 0, 0))],
        out_specs=[pl.BlockSpec((2, tm, FFN_GROUP), lambda i, d: (0, i, d)), row,
                   pl.BlockSpec((8, D_MODEL), lambda i, d: (0, 0))],
        out_shape=[jax.ShapeDtypeStruct((2, rows, FFN_GROUPS * FFN_GROUP), BF16),
                   jax.ShapeDtypeStruct((rows, D_MODEL), F32), jax.ShapeDtypeStruct((8, D_MODEL), F32)],
        scratch_shapes=[pltpu.VMEM((tm, D_MODEL), BF16), pltpu.VMEM((tm, D_MODEL), F32)],
        compiler_params=_cparams(("arbitrary", "arbitrary")),
    )(dh2, g, up, h1, norm_w, wfi_g, wfi_g, wfo_g)


def _mix_bwd(dh1, yr, yl, proj, wbr, wbl, wout):
    rows = dh1.shape[0]
    tm = _tile(rows, 640)

    def body(dh1_ref, yr_ref, yl_ref, ga_ref, gb_ref, wbr_ref, wbl_ref, wo_ref,
             dyr_ref, dyl_ref, dseg_ref, dzr_ref, dzl_ref):
        dmix = _dot_nt(dh1_ref[...].astype(BF16), wo_ref[...])
        sa, sb = _sigmoid(ga_ref[...].astype(F32)), _sigmoid(gb_ref[...].astype(F32))
        dyr = (dmix * sa).astype(BF16)
        dyl = (dmix * sb).astype(BF16)
        dyr_ref[...] = dyr
        dyl_ref[...] = dyl
        dseg_ref[:, 0:D_MODEL] = (dmix * yr_ref[...].astype(F32) * (sa * (1.0 - sa))).astype(BF16)
        dseg_ref[:, D_MODEL:2 * D_MODEL] = (dmix * yl_ref[...].astype(F32) * (sb * (1.0 - sb))).astype(BF16)
        dzr_ref[...] = _dot_nt(dyr, wbr_ref[...]).astype(BF16)
        dzl_ref[...] = _dot_nt(dyl, wbl_ref[...]).astype(BF16)

    row = pl.BlockSpec((tm, D_MODEL), lambda i: (i, 0))
    wsp = pl.BlockSpec((D_MODEL, D_MODEL), lambda i: (0, 0))
    bshape = jax.ShapeDtypeStruct((rows, D_MODEL), BF16)
    return pl.pallas_call(
        body, name="mix_bwd", grid=(rows // tm,),
        in_specs=[row, row, row, _seg_spec(tm, 6), _seg_spec(tm, 7), wsp, wsp, wsp],
        out_specs=[row, row, pl.BlockSpec((tm, 2 * D_MODEL), lambda i: (i, 3)), row, row],
        out_shape=[bshape, bshape, jax.ShapeDtypeStruct((rows, N_DEV * D_MODEL), BF16), bshape, bshape],
        compiler_params=_cparams(("parallel",)),
    )(dh1, yr, yl, proj, proj, wbr, wbl, wout)


S1_SHAPES = [
    jax.ShapeDtypeStruct((N_DEV, D_MODEL, FFN_GROUP), BF16),
    jax.ShapeDtypeStruct((N_DEV, FFN_OUT_SHARD, D_MODEL), BF16),
]


def _s1_parts(ins, p):
    return [ins[0].at[p], ins[1].at[p // 2, _half_rows(p), :]]


def _lru_bwd(dzl, hs, proj, dproj, conv_w, conv_b, ba, bx, lam, wa_g, wx_g, s1_grads):
    rows = dzl.shape[0]
    tm = _tile(rows, 320)
    nt = rows // tm
    t8 = tm // 8
    n_s1 = len(s1_grads)

    def body(dzl_ref, hs_ref, hsp_ref, x_ref, xp_ref, gt_ref, cw_ref, cb_ref, ba_ref, bx_ref, lam_ref,
             wa_ref, wx_ref, dproj_in, *refs):
        del dproj_in
        s1_refs = refs[:n_s1]
        dseg_ref, dwa_ref, dwx_ref, sm_ref = refs[n_s1:n_s1 + 4]
        land_refs = refs[n_s1 + 4:2 * n_s1 + 4]
        xbuf, abuf, bbuf, dbuf, dcbuf, anext, dhcar, send_sems, recv_sems, loc_sems = refs[2 * n_s1 + 4:]
        step = pl.program_id(0)
        i = nt - 1 - step
        push = _Push(lambda p: _s1_parts(s1_refs, p), lambda s: [r.at[s] for r in land_refs],
                     (send_sems, recv_sems, loc_sems), n_s1)

        @pl.when(step == 0)
        def _():
            push.start()
            dwa_ref[...] = jnp.zeros_like(dwa_ref)
            dwx_ref[...] = jnp.zeros_like(dwx_ref)
            sm_ref[...] = jnp.zeros_like(sm_ref)
            anext[...] = jnp.zeros_like(anext)
            dhcar[...] = jnp.zeros_like(dhcar)
            dcbuf[tm:tm + 8, :] = jnp.zeros((8, D_MODEL), F32)

        first = i == 0
        x_prev = jnp.where(first, 0.0, xp_ref[8:16, :].astype(F32))
        x_v = x_ref[...].astype(F32)
        xbuf[0:8, :] = x_prev
        xbuf[8:8 + tm, :] = x_v
        c = _conv_taps(xbuf, tm, cw_ref, cb_ref)
        lam_v = lam_ref[...]
        r, ig, a, mult, sp = _lru_gates(c, ba_ref[...], bx_ref[...], lam_v, wa_ref, wx_ref)
        hs_v = hs_ref[...].astype(F32)
        gl, dgl = _gelu_parts(gt_ref[...].astype(F32))
        dzl_v = dzl_ref[...].astype(F32)
        dseg_ref[:, D_MODEL:2 * D_MODEL] = (dzl_v * hs_v * dgl).astype(BF16)
        dbuf[...] = dzl_v * gl
        abuf[0:tm, :] = a
        abuf[tm:tm + 8, :] = jnp.broadcast_to(anext[...], (8, D_MODEL))
        bbuf[...] = abuf[1:tm + 1, :]

        sub = lax.broadcasted_iota(jnp.int32, (8, D_MODEL), 0)

        def block(k, carry):
            off = pl.multiple_of((t8 - 1 - k) * 8, 8)
            av = bbuf[pl.ds(off, 8), :]
            uv = dbuf[pl.ds(off, 8), :]
            for s in (1, 2, 4):
                us = jnp.where(sub < 8 - s, pltpu.roll(uv, 8 - s, 0), 0.0)
                as_ = jnp.where(sub < 8 - s, pltpu.roll(av, 8 - s, 0), 1.0)
                uv = uv + av * us
                av = av * as_
            hv = uv + av * carry
            dbuf[pl.ds(off, 8), :] = hv
            return hv[0:1, :]

        dhcar[...] = lax.fori_loop(0, t8, block, dhcar[...])
        anext[...] = abuf[0:1, :]
        dh = dbuf[...]

        xbuf[0:8, :] = jnp.where(first, 0.0, hsp_ref[8:16, :].astype(F32))
        xbuf[8:8 + tm, :] = hs_v
        hprev = xbuf[7:7 + tm, :]
        row = i * tm + lax.broadcasted_iota(jnp.int32, (tm, 1), 0)
        duu = jnp.where(row >= PAD_ROWS, dh, 0.0)
        da = dh * hprev
        dmult = duu * ig * c
        di = duu * mult * c
        dc = duu * mult * ig
        dlog_a = da * a - dmult * (a * a) / mult
        dr = dlog_a * (-LRU_C * sp)
        dsp = jnp.sum(dlog_a * (-LRU_C * r), axis=0, keepdims=True)
        dpr = dr * r * (1.0 - r)
        dpi = di * ig * (1.0 - ig)
        dpr_b, dpi_b = dpr.astype(BF16), dpi.astype(BF16)
        dcs = []
        for g in range(LRU_BLOCKS):
            sl = slice(LRU_BLOCK * g, LRU_BLOCK * (g + 1))
            cg = c[:, sl].astype(BF16)
            dwa_ref[g] += _dot_tn(cg, dpr_b[:, sl])
            dwx_ref[g] += _dot_tn(cg, dpi_b[:, sl])
            dcs.append(_dot_nt(dpr_b[:, sl], wa_ref[g]) + _dot_nt(dpi_b[:, sl], wx_ref[g]))
        dc = dc + jnp.concatenate(dcs, axis=1)

        dcbuf[0:tm, :] = dc
        xbuf[8:8 + tm, :] = x_v
        xbuf[0:8, :] = x_prev
        dlin = cw_ref[3:4, :] * dc
        sm_ref[3:4, :] += jnp.sum(dc * xbuf[8:8 + tm, :], axis=0, keepdims=True)
        for back in (1, 2, 3):
            dlin = dlin + cw_ref[3 - back:4 - back, :] * dcbuf[back:back + tm, :]
            sm_ref[3 - back:4 - back, :] += jnp.sum(dc * xbuf[8 - back:8 - back + tm, :], axis=0, keepdims=True)
        dseg_ref[:, 0:D_MODEL] = dlin.astype(BF16)
        dcbuf[tm:tm + 8, :] = dcbuf[0:8, :]
        sm_ref[4:5, :] += jnp.sum(dc, axis=0, keepdims=True)
        sm_ref[5:6, :] += jnp.sum(dpr, axis=0, keepdims=True)
        sm_ref[6:7, :] += jnp.sum(dpi, axis=0, keepdims=True)
        sm_ref[7:8, :] += dsp * (-_sigmoid(-lam_v))

        @pl.when(step == nt - 1)
        def _():
            push.wait()

    rowb = pl.BlockSpec((tm, D_MODEL), lambda s: (nt - 1 - s, 0))
    t16 = tm // 16
    prev8 = pl.BlockSpec((16, D_MODEL), lambda s: (jnp.maximum((nt - 1 - s) * t16 - 1, 0), 0))
    seg = lambda k: pl.BlockSpec((tm, D_MODEL), lambda s, k=k: (nt - 1 - s, k))
    prev8_seg4 = pl.BlockSpec((16, D_MODEL), lambda s: (jnp.maximum((nt - 1 - s) * t16 - 1, 0), 4))
    vec = pl.BlockSpec((1, D_MODEL), lambda s: (0, 0))
    mat = pl.BlockSpec((LRU_BLOCKS, LRU_BLOCK, LRU_BLOCK), lambda s: (0, 0, 0))
    mshape = jax.ShapeDtypeStruct((LRU_BLOCKS, LRU_BLOCK, LRU_BLOCK), F32)
    n_in = 13
    return pl.pallas_call(
        body, name="lru_bwd", grid=(nt,),
        in_specs=[rowb, rowb, prev8, seg(4), prev8_seg4, seg(5), pl.BlockSpec((4, D_MODEL), lambda s: (0, 0)),
                  vec, vec, vec, vec, mat, mat, ANY] + [ANY] * n_s1,
        out_specs=[pl.BlockSpec((tm, 2 * D_MODEL), lambda s: (nt - 1 - s, 2)), mat, mat,
                   pl.BlockSpec((8, D_MODEL), lambda s: (0, 0))] + [ANY] * n_s1,
        out_shape=[jax.ShapeDtypeStruct(dproj.shape, dproj.dtype), mshape, mshape,
                   jax.ShapeDtypeStruct((8, D_MODEL), F32)] + S1_SHAPES,
        input_output_aliases={n_in: 0},
        scratch_shapes=[pltpu.VMEM((tm + 8, D_MODEL), F32), pltpu.VMEM((tm + 8, D_MODEL), F32),
                        pltpu.VMEM((tm, D_MODEL), F32), pltpu.VMEM((tm, D_MODEL), F32),
                        pltpu.VMEM((tm + 8, D_MODEL), F32),
                        pltpu.VMEM((1, D_MODEL), F32), pltpu.VMEM((1, D_MODEL), F32)] + _push_sems(n_s1),
        compiler_params=pltpu.CompilerParams(dimension_semantics=("arbitrary",), vmem_limit_bytes=VMEM_LIMIT,
                                             has_side_effects=True),
    )(dzl, hs, hs, proj, proj, proj, conv_w, conv_b, ba, bx, lam, wa_g, wx_g, dproj, *s1_grads)


def _retention_bwd(dzr, o, proj, states, cos2, sin2, dec, dproj, ride):
    rows = dzr.shape[0]
    n_chunks = rows // CHUNK
    n_r = ride.n

    def body(dzr_ref, o_ref, q_ref, k_ref, v_ref, g_ref, st_ref, c_ref, s_ref, dec_ref, dproj_in, *refs):
        del dproj_in
        dseg_ref = refs[n_r]
        dstate = refs[2 * n_r + 1]
        push = ride.push(refs[:n_r], refs[n_r + 1:2 * n_r + 1], refs[2 * n_r + 2:])

        @pl.when(pl.program_id(0) == 0)
        def _():
            push.start()
            dstate[...] = jnp.zeros_like(dstate)
        cos_t, sin_t = c_ref[...], s_ref[...]
        for h in range(HEADS):
            sl = slice(HEAD_DIM * h, HEAD_DIM * (h + 1))
            o = o_ref[:, sl].astype(F32)
            g = g_ref[:, sl].astype(F32)
            dzr_v = dzr_ref[:, sl].astype(F32)
            sg = _sigmoid(g)
            r = lax.rsqrt(jnp.mean(o * o, axis=-1, keepdims=True) + NORM_EPS)
            on = o * r
            dseg_ref[:, 3 * D_MODEL + HEAD_DIM * h:3 * D_MODEL + HEAD_DIM * (h + 1)] = (
                dzr_v * on * (sg * (1.0 + g * (1.0 - sg)))).astype(BF16)
            don = dzr_v * (g * sg)
            do = r * (don - on * jnp.mean(don * on, axis=-1, keepdims=True))
            dob = do.astype(BF16)

            qh = _rot(q_ref[:, sl].astype(F32), cos_t, sin_t)
            kh = _rot(k_ref[:, sl].astype(F32), cos_t, sin_t) * QK_SCALE
            qb, kb, vb = qh.astype(BF16), kh.astype(BF16), v_ref[:, sl]
            intra, qd, kd, cd = dec_ref[0, h], dec_ref[1, h], dec_ref[2, h], dec_ref[3, h]
            s = (_dot_nt(qb, kb) * intra).astype(BF16)
            ds = (_dot_nt(dob, vb) * intra).astype(BF16)
            st_b = st_ref[h].astype(BF16)
            dst = dstate[h]
            dst_b = dst.astype(BF16)
            dv = _dot_tn(s, dob) + _dot((kh * kd).astype(BF16), dst_b)
            dq = _dot(ds, kb) + _dot_nt(dob, st_b) * qd
            dk = _dot_tn(ds, qb) + _dot_nt(vb, dst_b) * kd
            dstate[h] = dst * cd + _dot_tn((qh * qd).astype(BF16), dob)
            dseg_ref[:, 2 * D_MODEL + HEAD_DIM * h:2 * D_MODEL + HEAD_DIM * (h + 1)] = dv.astype(BF16)
            dseg_ref[:, sl] = _rot_t(dq, cos_t, sin_t).astype(BF16)
            dseg_ref[:, D_MODEL + HEAD_DIM * h:D_MODEL + HEAD_DIM * (h + 1)] = (
                _rot_t(dk, cos_t, sin_t) * QK_SCALE).astype(BF16)

        @pl.when(pl.program_id(0) == n_chunks - 1)
        def _():
            push.wait()

    rev = lambda s: n_chunks - 1 - s
    rowb = pl.BlockSpec((CHUNK, D_MODEL), lambda s: (rev(s), 0))
    seg = lambda k: pl.BlockSpec((CHUNK, D_MODEL), lambda s, k=k: (rev(s), k))
    tab = pl.BlockSpec((CHUNK, HEAD_DIM), lambda s: (rev(s), 0))
    return pl.pallas_call(
        body, name="retention_bwd", grid=(n_chunks,),
        in_specs=[rowb, rowb, seg(0), seg(1), seg(2), seg(3),
                  pl.BlockSpec((None, HEADS, HEAD_DIM, HEAD_DIM), lambda s: (rev(s), 0, 0, 0)), tab, tab,
                  pl.BlockSpec((4, HEADS, CHUNK, CHUNK), lambda s: (0, 0, 0, 0)), ANY] + ride.specs(),
        out_specs=[pl.BlockSpec((CHUNK, 4 * D_MODEL), lambda s: (rev(s), 0))] + ride.specs(),
        out_shape=[jax.ShapeDtypeStruct(dproj.shape, dproj.dtype)] + ride.out_shapes,
        input_output_aliases={10: 0},
        scratch_shapes=[pltpu.VMEM((HEADS, HEAD_DIM, HEAD_DIM), F32)] + ride.scratch(),
        compiler_params=pltpu.CompilerParams(dimension_semantics=("arbitrary",), vmem_limit_bytes=VMEM_LIMIT,
                                             has_side_effects=True),
    )(dzr, o, proj, proj, proj, proj, states, cos2, sin2, dec, dproj, *ride.arrays)


S2_SHAPES = [
    jax.ShapeDtypeStruct((N_DEV, D_MODEL, D_MODEL), BF16),
    jax.ShapeDtypeStruct((N_DEV, LRU_BLOCKS, LRU_ROWS, LRU_BLOCK), F32),
    jax.ShapeDtypeStruct((N_DEV, LRU_BLOCKS, LRU_ROWS, LRU_BLOCK), F32),
]


def _s2_parts(ins, p):
    rows_p = pl.ds(pl.multiple_of(p * LRU_ROWS, 8), LRU_ROWS)
    return [ins[0].at[p], ins[1].at[:, rows_p, :], ins[2].at[:, rows_p, :]]


def _in_proj_bwd(dproj, win_g, h0, norm_w, dh1, s2_grads):
    rows = h0.shape[0]
    tm = _tile(rows, 640)
    n_i = rows // tm
    n_s2 = len(s2_grads)

    def body(dseg_ref, w_ref, h0_ref, nw_ref, dh1_ref, *refs):
        s2_refs = refs[:n_s2]
        dh0_ref, dw_ref = refs[n_s2:n_s2 + 2]
        land_refs = refs[n_s2 + 2:2 * n_s2 + 2]
        acc, send_sems, recv_sems, loc_sems = refs[2 * n_s2 + 2:]
        i, j = pl.program_id(0), pl.program_id(1)
        push = _Push(lambda p: _s2_parts(s2_refs, p), lambda s: [r.at[s] for r in land_refs],
                     (send_sems, recv_sems, loc_sems), n_s2)

        @pl.when(jnp.logical_and(i == 0, j == 0))
        def _():
            push.start()
            dw_ref[...] = jnp.zeros_like(dw_ref)

        @pl.when(j == 0)
        def _():
            acc[...] = jnp.zeros_like(acc)

        acc[...] += _dot_nt(dseg_ref[...], w_ref[...])

        @pl.when(j == N_DEV - 1)
        def _():
            dx, dw = _rms_bwd(h0_ref[...], nw_ref[...], acc[...])
            dw_ref[0:1, :] += dw
            dh0_ref[...] = dh1_ref[...] + dx

        @pl.when(jnp.logical_and(i == n_i - 1, j == N_DEV - 1))
        def _():
            push.wait()

    row = pl.BlockSpec((tm, D_MODEL), lambda i, j: (i, 0))
    vec = pl.BlockSpec((1, D_MODEL), lambda i, j: (0, 0))
    return pl.pallas_call(
        body, name="in_proj_bwd", grid=(n_i, N_DEV),
        in_specs=[pl.BlockSpec((tm, D_MODEL), lambda i, j: (i, j)),
                  pl.BlockSpec((None, D_MODEL, D_MODEL), lambda i, j: (j, 0, 0)), row, vec, row] + [ANY] * n_s2,
        out_specs=[row, pl.BlockSpec((8, D_MODEL), lambda i, j: (0, 0))] + [ANY] * n_s2,
        out_shape=[jax.ShapeDtypeStruct((rows, D_MODEL), F32), jax.ShapeDtypeStruct((8, D_MODEL), F32)] + S2_SHAPES,
        scratch_shapes=[pltpu.VMEM((tm, D_MODEL), F32)] + _push_sems(n_s2),
        compiler_params=pltpu.CompilerParams(dimension_semantics=("arbitrary", "arbitrary"),
                                             vmem_limit_bytes=VMEM_LIMIT, has_side_effects=True),
    )(dproj, win_g, h0, norm_w, dh1, *s2_grads)


def _adamw(g_slots, w, m, v):
    slots, rows, cols = g_slots.shape
    tr = rows
    for cand in (256, 128, 64, 32, 16, 8):
        if rows % cand == 0 and rows > cand:
            tr = cand
            break

    def body(g_ref, w_ref, m_ref, v_ref, go_ref, d_ref, mo_ref, vo_ref):
        g = g_ref[0].astype(F32)
        for s in range(1, slots):
            g = g + g_ref[s].astype(F32)
        m2 = ADAM_B1 * m_ref[...] + (1.0 - ADAM_B1) * g
        v2 = ADAM_B2 * v_ref[...] + (1.0 - ADAM_B2) * (g * g)
        m_hat = m2 / (1.0 - ADAM_B1 ** ADAM_STEP)
        v_hat = v2 / (1.0 - ADAM_B2 ** ADAM_STEP)
        go_ref[...] = g
        d_ref[...] = -ADAM_LR * (m_hat / (jnp.sqrt(v_hat) + ADAM_EPS) + ADAM_WD * w_ref[...])
        mo_ref[...] = m2
        vo_ref[...] = v2

    blk = pl.BlockSpec((tr, cols), lambda i: (i, 0))
    shape = jax.ShapeDtypeStruct((rows, cols), F32)
    return pl.pallas_call(
        body, name="adamw", grid=(rows // tr,),
        in_specs=[pl.BlockSpec((slots, tr, cols), lambda i: (0, i, 0)), blk, blk, blk],
        out_specs=[blk] * 4, out_shape=[shape] * 4,
        compiler_params=_cparams(("parallel",)),
    )(g_slots, w, m, v)


def _sum_slots(packs):
    slots, rows, cols = packs.shape

    def body(p_ref, o_ref):
        acc = p_ref[0]
        for s in range(1, slots):
            acc = acc + p_ref[s]
        o_ref[...] = acc

    return pl.pallas_call(
        body, name="sum_slots", out_shape=jax.ShapeDtypeStruct((rows, cols), F32),
        compiler_params=pltpu.CompilerParams(vmem_limit_bytes=VMEM_LIMIT),
    )(packs)


def _gather_small(small):
    shapes = [jax.ShapeDtypeStruct((N_DEV,) + small.shape, F32)]
    return _push_call("gather_small", [small], shapes,
                      lambda ins, p: list(ins), lambda outs, s: [r.at[s] for r in outs])[0]


def _share_pack(pack):
    shapes = [jax.ShapeDtypeStruct((N_DEV,) + pack.shape, F32)]
    return _push_call("share_pack", [pack], shapes,
                      lambda ins, p: list(ins), lambda outs, s: [r.at[s] for r in outs])[0]


PACK_MIX_NORM, PACK_CONV_W, PACK_CONV_B, PACK_BA, PACK_BX, PACK_LAM = 0, 8, 12, 13, 14, 15
PACK_FFN_NORM, PACK_SQ_ERR, PACK_FINAL_NORM, PACK_META = 16, 24, 25, 32


def kernel(x, meta_tokens, mix_norm_w, w_in, conv_w, conv_b, lru_wa, lru_ba, lru_wx, lru_bx, lru_lambda, w_branch_ret, w_branch_lru, w_out, ffn_norm_w, w_ffn_in, w_ffn_out, final_norm_w, loss_target, m_meta_tokens, m_mix_norm_w, m_w_in, m_conv_w, m_conv_b, m_lru_wa, m_lru_ba, m_lru_wx, m_lru_bx, m_lru_lambda, m_w_branch_ret, m_w_branch_lru, m_w_out, m_ffn_norm_w, m_w_ffn_in, m_w_ffn_out, m_final_norm_w, v_meta_tokens, v_mix_norm_w, v_w_in, v_conv_w, v_conv_b, v_lru_wa, v_lru_ba, v_lru_wx, v_lru_bx, v_lru_lambda, v_w_branch_ret, v_w_branch_lru, v_w_out, v_ffn_norm_w, v_w_ffn_in, v_w_ffn_out, v_final_norm_w):
    me = _my_index()
    pad4 = ((0, 4), (0, 0))
    fw = final_norm_w.reshape(1, D_MODEL)

    small = jnp.concatenate([meta_tokens, jnp.pad(conv_w[0], pad4)], axis=0)
    small_g = _gather_small(small)
    meta_full = small_g[:, :N_META].transpose(1, 0, 2).reshape(N_META, D_MODEL)
    conv_w_full = small_g[:, N_META:N_META + 4].transpose(1, 0, 2).reshape(4, D_MODEL)
    rest_shards = [w_branch_ret[0].astype(BF16), w_branch_lru[0].astype(BF16), w_out[0].astype(BF16),
                   w_ffn_out[0].astype(BF16), lru_wa[0].astype(BF16), lru_wx[0].astype(BF16)]
    wfi_shard = jnp.pad(w_ffn_in[0].astype(BF16), ((0, 0), (0, FFN_GROUP - FFN_SHARD)))
    own_slot = lambda ins, p: list(ins)
    part_of_owner = lambda ins, p: [r.at[p] for r in ins]

    rows = x.shape[1] + CHUNK
    h0 = jnp.concatenate([jnp.zeros((PAD_ROWS, D_MODEL), F32), meta_full, x[0]], axis=0)
    tgt = jnp.concatenate([jnp.zeros((CHUNK, D_MODEL), F32), loss_target[0]], axis=0)
    cos2, sin2 = _rope_tables(rows)
    dec = _retention_consts()

    proj, u, win_g = _in_proj(h0, mix_norm_w, w_in[0].astype(BF16), me.astype(jnp.int32).reshape(1))
    o, zr, states, wbr_g, wbl_g, wout_g, wfo_g, wa_g, wx_g = _retention_fwd(
        proj, cos2, sin2, dec, _rest_ride(rest_shards))
    wbr, wbl, wout = (t.reshape(D_MODEL, D_MODEL) for t in (wbr_g, wbl_g, wout_g))
    gather_wfi = _Ride([wfi_shard], [jax.ShapeDtypeStruct((N_DEV, D_MODEL, FFN_GROUP), BF16)],
                       own_slot, _slot_of_sender)
    hs, zl, wfi_g = _lru_fwd(proj, conv_w_full, conv_b, lru_ba, lru_bx, lru_lambda, wa_g, wx_g, gather_wfi)
    h1, yr, yl, mixed = _mix_fwd(zr, zl, proj, h0, wbr, wbl, wout)
    u2, g, up, act, dh2, red = _ffn_fwd_loss(h1, ffn_norm_w, wfi_g, wfo_g, fw, tgt)

    d_wfo = _wgrad(act, dh2, FFN_GROUP, D_MODEL, BF16)[:, 0]
    dgu, dh1, dw_ffn_norm = _ffn_bwd(dh2, g, up, h1, ffn_norm_w, wfi_g, wfo_g)
    d_wfi = _wgrad(u2, dgu, D_MODEL, FFN_GROUP, BF16, b_halves=True)[0]
    d_wout = _wgrad(mixed, dh1, D_MODEL, D_MODEL, BF16)[0, 0]
    dyr, dyl, dproj, dzr, dzl = _mix_bwd(dh1, yr, yl, proj, wbr, wbl, wout)
    d_wbr = _wgrad(zr, dyr, D_MODEL, D_MODEL, BF16)[0, 0]
    d_wbl = _wgrad(zl, dyl, D_MODEL, D_MODEL, BF16)[0, 0]
    dproj, d_wa, d_wx, lru_small, r_fi, r_fo = _lru_bwd(
        dzl, hs, proj, dproj, conv_w_full, conv_b, lru_ba, lru_bx, lru_lambda, wa_g, wx_g, [d_wfi, d_wfo])
    mix_shape = jax.ShapeDtypeStruct((N_DEV, D_MODEL // N_DEV, D_MODEL), BF16)
    scatter_mix = _Ride([t.reshape(mix_shape.shape) for t in (d_wbr, d_wbl, d_wout)], [mix_shape] * 3,
                        part_of_owner, _slot_of_sender)
    dproj, r_br, r_bl, r_out = _retention_bwd(dzr, o, proj, states, cos2, sin2, dec, dproj, scatter_mix)
    d_win = _wgrad(u, dproj, D_MODEL, D_MODEL, BF16)[0]
    dh0, dw_mix_norm, r_in, r_wa, r_wx = _in_proj_bwd(dproj, win_g, h0, mix_norm_w, dh1, [d_win, d_wa, d_wx])
    grad_x = dh0[CHUNK:]

    pack = jnp.concatenate([dw_mix_norm, lru_small, dw_ffn_norm, red, dh0[PAD_ROWS:CHUNK]], axis=0)
    small_sum = _sum_slots(_share_pack(pack))
    loss = (0.5 / D_MODEL) * jnp.sum(small_sum[PACK_SQ_ERR])

    def big_update(slots, w, m, v):
        shape = w.shape
        w2, m2, v2 = (t.reshape(slots.shape[1:]) for t in (w, m, v))
        return [t.reshape(shape) for t in _adamw(slots, w2, m2, v2)]

    res = {}
    res["w_in"] = big_update(r_in, w_in, m_w_in, v_w_in)
    res["w_branch_ret"] = big_update(r_br, w_branch_ret, m_w_branch_ret, v_w_branch_ret)
    res["w_branch_lru"] = big_update(r_bl, w_branch_lru, m_w_branch_lru, v_w_branch_lru)
    res["w_out"] = big_update(r_out, w_out, m_w_out, v_w_out)
    res["w_ffn_in"] = big_update(r_fi[:, :, :FFN_SHARD], w_ffn_in, m_w_ffn_in, v_w_ffn_in)
    res["w_ffn_out"] = big_update(r_fo, w_ffn_out, m_w_ffn_out, v_w_ffn_out)
    res["lru_wa"] = big_update(r_wa.reshape(N_DEV, LRU_BLOCKS * LRU_ROWS, LRU_BLOCK), lru_wa, m_lru_wa, v_lru_wa)
    res["lru_wx"] = big_update(r_wx.reshape(N_DEV, LRU_BLOCKS * LRU_ROWS, LRU_BLOCK), lru_wx, m_lru_wx, v_lru_wx)

    col = me * HEAD_DIM
    g_meta = lax.dynamic_slice(small_sum, (PACK_META, col), (N_META, HEAD_DIM))
    g_conv = lax.dynamic_slice(small_sum, (PACK_CONV_W, col), (8, HEAD_DIM))
    small_names = ["mix_norm_w", "conv_b", "lru_ba", "lru_bx", "lru_lambda", "ffn_norm_w", "final_norm_w"]
    small_rows = [PACK_MIX_NORM, PACK_CONV_B, PACK_BA, PACK_BX, PACK_LAM, PACK_FFN_NORM, PACK_FINAL_NORM]
    small_w = [mix_norm_w, conv_b, lru_ba, lru_bx, lru_lambda, ffn_norm_w, fw]
    small_m = [m_mix_norm_w, m_conv_b, m_lru_ba, m_lru_bx, m_lru_lambda, m_ffn_norm_w, m_final_norm_w.reshape(1, -1)]
    small_v = [v_mix_norm_w, v_conv_b, v_lru_ba, v_lru_bx, v_lru_lambda, v_ffn_norm_w, v_final_norm_w.reshape(1, -1)]

    def pack_small(vec_list, meta_t, conv_t):
        return jnp.concatenate([t.reshape(8, HEAD_DIM) for t in vec_list] + [meta_t, jnp.pad(conv_t[0], pad4)], axis=0)

    g_small = jnp.concatenate([small_sum[r].reshape(8, HEAD_DIM) for r in small_rows] + [g_meta, g_conv], axis=0)
    outs_small = _adamw(g_small[None], pack_small(small_w, meta_tokens, conv_w),
                        pack_small(small_m, m_meta_tokens, m_conv_w), pack_small(small_v, v_meta_tokens, v_conv_w))
    for idx, name in enumerate(small_names):
        shape = final_norm_w.shape if name == "final_norm_w" else (1, D_MODEL)
        res[name] = [t[8 * idx:8 * idx + 8].reshape(shape) for t in outs_small]
    res["meta_tokens"] = [t[56:72] for t in outs_small]
    res["conv_w"] = [t[72:76].reshape(1, 4, HEAD_DIM) for t in outs_small]

    order = ["meta_tokens", "mix_norm_w", "w_in", "conv_w", "conv_b", "lru_wa", "lru_ba", "lru_wx", "lru_bx",
             "lru_lambda", "w_branch_ret", "w_branch_lru", "w_out", "ffn_norm_w", "w_ffn_in", "w_ffn_out",
             "final_norm_w"]
    out = [loss, grad_x[None]]
    for kind in range(4):
        out += [res[name][kind] for name in order]
    return tuple(out)
```

```python
import functools

import numpy as np
import jax
import jax.numpy as jnp
from jax import lax
from jax.experimental import pallas as pl
from jax.experimental.pallas import tpu as pltpu

F32 = jnp.float32
BF16 = jnp.bfloat16

D_MODEL = 1024
N_META = 16
CHUNK = 128
PAD_ROWS = CHUNK - N_META
HEADS = 8
HEAD_DIM = 128
ROPE_BASE = 10000.0
QK_SCALE = HEAD_DIM ** -0.5
LRU_BLOCKS = 4
LRU_BLOCK = 256
LRU_C = 8.0
FFN_HIDDEN = 2816
N_DEV = 8
FFN_SHARD = 2 * FFN_HIDDEN // N_DEV
FFN_GROUP = 768
FFN_GROUPS = 4
FFN_OUT_SHARD = FFN_HIDDEN // N_DEV
NORM_EPS = 1e-6

ADAM_LR = 0.001
ADAM_B1 = 0.9
ADAM_B2 = 0.999
ADAM_EPS = 1e-08
ADAM_WD = 0.01
ADAM_STEP = 10

VMEM_LIMIT = 56 * 1024 * 1024
MESH_ID = pl.DeviceIdType.MESH
ANY = pl.BlockSpec(memory_space=pl.ANY)


def _cparams(sem):
    return pltpu.CompilerParams(dimension_semantics=sem, vmem_limit_bytes=VMEM_LIMIT)


def _tile(rows, cap):
    t = cap - cap % 64
    while rows % t:
        t -= 64
    return t


def _dot(a, b):
    return jnp.dot(a, b, preferred_element_type=F32)


def _dot_nt(a, b):
    return lax.dot_general(a, b, (((1,), (1,)), ((), ())), preferred_element_type=F32)


def _dot_tn(a, b):
    return lax.dot_general(a, b, (((0,), (0,)), ((), ())), preferred_element_type=F32)


def _sigmoid(x):
    return 1.0 / (1.0 + jnp.exp(-x))


def _gelu_parts(x):
    k = 0.7978845608028654
    inner = k * (x + 0.044715 * x * x * x)
    t = jnp.tanh(inner)
    g = 0.5 * x * (1.0 + t)
    dg = 0.5 * (1.0 + t) + 0.5 * x * (1.0 - t * t) * k * (1.0 + 3.0 * 0.044715 * x * x)
    return g, dg


def _rot(x, cos2, sin2):
    return x * cos2 + pltpu.roll(x, HEAD_DIM // 2, 1) * sin2


def _rot_t(dx, cos2, sin2):
    return dx * cos2 - pltpu.roll(dx, HEAD_DIM // 2, 1) * sin2


def _rms_bwd(x, w, dy):
    rs = lax.rsqrt(jnp.mean(x * x, axis=-1, keepdims=True) + NORM_EPS)
    nh = x * rs
    dw = jnp.sum(dy * nh, axis=0, keepdims=True)
    dn = dy * w
    dx = rs * (dn - nh * jnp.mean(dn * nh, axis=-1, keepdims=True))
    return dx, dw


def _retention_consts():
    h = jnp.arange(HEADS, dtype=F32)
    log_g = jnp.log(1.0 - 2.0 ** (-5.0 - h))
    idx = jnp.arange(CHUNK, dtype=F32)
    diff = idx[:, None] - idx[None, :]
    intra = jnp.where(diff[None] >= 0, jnp.exp(jnp.maximum(diff, 0.0)[None] * log_g[:, None, None]), 0.0)
    q_decay = jnp.exp((idx + 1.0)[:, None] * log_g[None, :])
    k_decay = jnp.exp((CHUNK - 1.0 - idx)[:, None] * log_g[None, :])
    chunk_decay = jnp.exp(CHUNK * log_g)
    shape = (HEADS, CHUNK, CHUNK)
    qd = jnp.broadcast_to(q_decay.T[:, :, None], shape)
    kd = jnp.broadcast_to(k_decay.T[:, :, None], shape)
    cd = jnp.broadcast_to(chunk_decay[:, None, None], shape)
    return jnp.stack([intra, qd, kd, cd])


def _rope_tables(rows):
    pos = jnp.maximum(jnp.arange(rows) - PAD_ROWS, 0).astype(F32)
    inv_freq = ROPE_BASE ** (-jnp.arange(0, HEAD_DIM, 2, dtype=F32) / HEAD_DIM)
    ang = pos[:, None] * inv_freq[None, :]
    cos, sin = jnp.cos(ang), jnp.sin(ang)
    return jnp.concatenate([cos, cos], axis=1), jnp.concatenate([-sin, sin], axis=1)


def _my_index():
    return 4 * lax.axis_index("x") + 2 * lax.axis_index("y") + lax.axis_index("c")


def _peer(k):
    x, y, c = lax.axis_index("x"), lax.axis_index("y"), lax.axis_index("c")
    px = 1 - x if k & 4 else x
    py = 1 - y if k & 2 else y
    pc = 1 - c if k & 1 else c
    return (px, py, pc), 4 * px + 2 * py + pc


def _push_sems(n_arr):
    n_rem = (N_DEV - 1) * n_arr
    return [pltpu.SemaphoreType.DMA((n_rem,)), pltpu.SemaphoreType.DMA((n_rem,)), pltpu.SemaphoreType.DMA((n_arr,))]


class _Push:
    def __init__(self, send_part, land_slot, sems, n_arr):
        self.send_part, self.land_slot, self.n_arr = send_part, land_slot, n_arr
        self.send_sems, self.recv_sems, self.loc_sems = sems

    def _remote(self, k, a, src, dst, pos):
        idx = (k - 1) * self.n_arr + a
        return pltpu.make_async_remote_copy(src_ref=src, dst_ref=dst, send_sem=self.send_sems.at[idx],
                                            recv_sem=self.recv_sems.at[idx], device_id=pos, device_id_type=MESH_ID)

    def _outgoing(self):
        me = _my_index()
        land = self.land_slot(me)
        remote = []
        for k in range(1, N_DEV):
            pos, p = _peer(k)
            src = self.send_part(p)
            remote += [self._remote(k, a, src[a], land[a], pos) for a in range(self.n_arr)]
        own = self.send_part(me)
        local = [pltpu.make_async_copy(own[a], land[a], self.loc_sems.at[a]) for a in range(self.n_arr)]
        return remote, local

    def start(self):
        remote, local = self._outgoing()
        for cp in remote + local:
            cp.start()

    def wait_recv_from(self, k):
        own = self.send_part(_my_index())
        pos, p = _peer(k)
        land = self.land_slot(p)
        for a in range(self.n_arr):
            self._remote(k, a, own[a], land[a], pos).wait_recv()

    def wait_sends(self):
        remote, local = self._outgoing()
        for cp in remote:
            cp.wait_send()
        for cp in local:
            cp.wait()

    def wait(self):
        for k in range(1, N_DEV):
            self.wait_recv_from(k)
        self.wait_sends()


class _Ride:
    def __init__(self, arrays, out_shapes, send_part, land_slot, zero_dsts=None, zero_shape=None, n_zero=0):
        self.arrays, self.out_shapes = list(arrays), list(out_shapes)
        self.send_part, self.land_slot, self.n = send_part, land_slot, len(arrays)
        self.zero_dsts, self.zero_shape, self.n_zero = zero_dsts, zero_shape, n_zero

    def specs(self):
        return [ANY] * self.n

    def scratch(self):
        extra = [pltpu.SemaphoreType.DMA((self.n_zero,)), pltpu.VMEM(self.zero_shape, BF16)] if self.n_zero else []
        return _push_sems(self.n) + extra

    def push(self, in_refs, out_refs, scratch):
        ride = self
        push = _Push(lambda p: ride.send_part(in_refs, p), lambda s: ride.land_slot(out_refs, s),
                     tuple(scratch[:3]), self.n)

        class Both:
            def _fills(self):
                if not ride.n_zero:
                    return []
                zsems, zbuf = scratch[3], scratch[4]
                return [pltpu.make_async_copy(zbuf, dst, zsems.at[z]) for z, dst in enumerate(ride.zero_dsts(out_refs))]

            def start(self):
                push.start()
                if ride.n_zero:
                    scratch[4][...] = jnp.zeros(ride.zero_shape, BF16)
                for cp in self._fills():
                    cp.start()

            def wait(self):
                push.wait()
                for cp in self._fills():
                    cp.wait()

        return Both()


def _slot_of_sender(out_refs, s):
    return [r.at[s] for r in out_refs]


def _push_call(name, arrays, out_shapes, send_part, land_slot):
    n_arr = len(arrays)

    def body(*refs):
        ins, outs, sems = refs[:n_arr], refs[n_arr:2 * n_arr], refs[2 * n_arr:]
        push = _Push(lambda p: send_part(ins, p), lambda s: land_slot(outs, s), sems, n_arr)
        push.start()
        push.wait()

    return pl.pallas_call(
        body, name=name, in_specs=[ANY] * n_arr, out_specs=[ANY] * n_arr, out_shape=out_shapes,
        scratch_shapes=_push_sems(n_arr), compiler_params=pltpu.CompilerParams(has_side_effects=True),
    )(*arrays)


LRU_ROWS = LRU_BLOCK // N_DEV
FFN_PAD_ROWS = FFN_GROUP - 2 * FFN_OUT_SHARD


def _half_rows(d):
    return pl.ds(pl.multiple_of((d % 2) * FFN_OUT_SHARD, 16), FFN_OUT_SHARD)


def _lru_rows(d):
    return pl.ds(pl.multiple_of(d * LRU_ROWS, 16), LRU_ROWS)


def _mixer_slots(outs, d):
    return [outs[0].at[d], outs[1].at[d], outs[2].at[d],
            outs[3].at[:, _lru_rows(d), :], outs[4].at[:, _lru_rows(d), :]]


MIXER_SHAPES = [
    jax.ShapeDtypeStruct((N_DEV, D_MODEL // N_DEV, D_MODEL), BF16),
    jax.ShapeDtypeStruct((N_DEV, D_MODEL // N_DEV, D_MODEL), BF16),
    jax.ShapeDtypeStruct((N_DEV, D_MODEL // N_DEV, D_MODEL), BF16),
    jax.ShapeDtypeStruct((LRU_BLOCKS, LRU_BLOCK, LRU_BLOCK), BF16),
    jax.ShapeDtypeStruct((LRU_BLOCKS, LRU_BLOCK, LRU_BLOCK), BF16),
]


def _mixer_weights_ride(shards):
    return _Ride(shards, MIXER_SHAPES, lambda ins, p: list(ins), _mixer_slots)


def _wfo_ride(shard):
    zero_dsts = lambda outs: [outs[0].at[g, pl.ds(2 * FFN_OUT_SHARD, FFN_PAD_ROWS), :] for g in range(FFN_GROUPS)]
    return _Ride([shard], [jax.ShapeDtypeStruct((FFN_GROUPS, FFN_GROUP, D_MODEL), BF16)], lambda ins, p: list(ins),
                 lambda outs, d: [outs[0].at[d // 2, _half_rows(d), :]], zero_dsts, (FFN_PAD_ROWS, D_MODEL), FFN_GROUPS)


def _arrival_rank_to_relation(jj):
    return jnp.where(jj == 3, 4, jnp.where(jj == 4, 3, jj))


def _in_proj(h0, norm_w, win_shard, me_arr):
    rows = h0.shape[0]
    tm = _tile(rows, 640)
    n_i = rows // tm

    def body(me_ref, h_ref, nw_ref, wsh_ref, proj_ref, u_ref, wing_ref, u_all, wbuf, copy_sem,
             send_sems, recv_sems, loc_sems):
        del me_ref
        jj, i = pl.program_id(0), pl.program_id(1)
        push = _Push(lambda p: [wsh_ref], lambda s: [wing_ref.at[s]], (send_sems, recv_sems, loc_sems), 1)

        @pl.when(jnp.logical_and(jj == 0, i == 0))
        def _():
            push.start()
            own = pltpu.make_async_copy(wsh_ref, wbuf, copy_sem)
            own.start()
            own.wait()

        for k in range(1, N_DEV):
            rank = {3: 4, 4: 3}.get(k, k)

            @pl.when(jnp.logical_and(jj == rank, i == 0))
            def _(k=k):
                push.wait_recv_from(k)
                landed = pltpu.make_async_copy(wing_ref.at[_peer(k)[1]], wbuf, copy_sem)
                landed.start()
                landed.wait()

        rows_i = pl.ds(pl.multiple_of(i * tm, tm), tm)

        @pl.when(jj == 0)
        def _():
            x = h_ref[...]
            rs = lax.rsqrt(jnp.mean(x * x, axis=-1, keepdims=True) + NORM_EPS)
            u = (x * rs * nw_ref[...]).astype(BF16)
            u_all[rows_i, :] = u
            u_ref[...] = u
        proj_ref[...] = _dot(u_all[rows_i, :], wbuf[...]).astype(BF16)

        @pl.when(jnp.logical_and(jj == N_DEV - 1, i == n_i - 1))
        def _():
            push.wait_sends()

    first_pass = lambda jj, i: jnp.where(jj == 0, i, n_i - 1)
    grid_spec = pltpu.PrefetchScalarGridSpec(
        num_scalar_prefetch=1, grid=(N_DEV, n_i),
        in_specs=[pl.BlockSpec((tm, D_MODEL), lambda jj, i, me: (first_pass(jj, i), 0)),
                  pl.BlockSpec((1, D_MODEL), lambda jj, i, me: (0, 0)), ANY],
        out_specs=[pl.BlockSpec((tm, D_MODEL), lambda jj, i, me: (i, me[0] ^ _arrival_rank_to_relation(jj))),
                   pl.BlockSpec((tm, D_MODEL), lambda jj, i, me: (first_pass(jj, i), 0)), ANY],
        scratch_shapes=[pltpu.VMEM((rows, D_MODEL), BF16), pltpu.VMEM((D_MODEL, D_MODEL), BF16),
                        pltpu.SemaphoreType.DMA(())] + _push_sems(1))
    return pl.pallas_call(
        body, name="in_proj", grid_spec=grid_spec,
        out_shape=[jax.ShapeDtypeStruct((rows, N_DEV * D_MODEL), BF16),
                   jax.ShapeDtypeStruct((rows, D_MODEL), BF16),
                   jax.ShapeDtypeStruct((N_DEV, D_MODEL, D_MODEL), BF16)],
        compiler_params=pltpu.CompilerParams(dimension_semantics=("arbitrary", "arbitrary"),
                                             vmem_limit_bytes=VMEM_LIMIT, has_side_effects=True),
    )(me_arr, h0, norm_w, win_shard)


def _seg_spec(rows_per_block, seg):
    return pl.BlockSpec((rows_per_block, D_MODEL), lambda n, seg=seg: (n, seg))


def _retention_fwd(proj, cos2, sin2, dec, ride):
    rows = proj.shape[0]
    n_chunks = rows // CHUNK
    n_r = ride.n

    def body(q_ref, k_ref, v_ref, g_ref, c_ref, s_ref, dec_ref, *refs):
        o_ref, zr_ref, st_ref = refs[n_r:n_r + 3]
        state = refs[2 * n_r + 3]
        push = ride.push(refs[:n_r], refs[n_r + 3:2 * n_r + 3], refs[2 * n_r + 4:])

        @pl.when(pl.program_id(0) == 0)
        def _():
            push.start()
            state[...] = jnp.zeros_like(state)
        cos_t, sin_t = c_ref[...], s_ref[...]
        for h in range(HEADS):
            sl = slice(HEAD_DIM * h, HEAD_DIM * (h + 1))
            qh = _rot(q_ref[:, sl].astype(F32), cos_t, sin_t)
            kh = _rot(k_ref[:, sl].astype(F32), cos_t, sin_t) * QK_SCALE
            qb, kb, vb = qh.astype(BF16), kh.astype(BF16), v_ref[:, sl]
            s = _dot_nt(qb, kb) * dec_ref[0, h]
            st = state[h]
            st_ref[h] = st
            o = _dot(s.astype(BF16), vb) + _dot(qb, st.astype(BF16)) * dec_ref[1, h]
            state[h] = st * dec_ref[3, h] + _dot_tn((kh * dec_ref[2, h]).astype(BF16), vb)
            o_ref[:, sl] = o.astype(BF16)
            r = lax.rsqrt(jnp.mean(o * o, axis=-1, keepdims=True) + NORM_EPS)
            g = g_ref[:, sl].astype(F32)
            zr_ref[:, sl] = (g * _sigmoid(g) * (o * r)).astype(BF16)

        @pl.when(pl.program_id(0) == n_chunks - 1)
        def _():
            push.wait()

    tab = pl.BlockSpec((CHUNK, HEAD_DIM), lambda n: (n, 0))
    return pl.pallas_call(
        body, name="retention_fwd", grid=(n_chunks,),
        in_specs=[_seg_spec(CHUNK, 0), _seg_spec(CHUNK, 1), _seg_spec(CHUNK, 2), _seg_spec(CHUNK, 3), tab, tab,
                  pl.BlockSpec((4, HEADS, CHUNK, CHUNK), lambda n: (0, 0, 0, 0))] + ride.specs(),
        out_specs=[pl.BlockSpec((CHUNK, D_MODEL), lambda n: (n, 0)),
                   pl.BlockSpec((CHUNK, D_MODEL), lambda n: (n, 0)),
                   pl.BlockSpec((None, HEADS, HEAD_DIM, HEAD_DIM), lambda n: (n, 0, 0, 0))] + ride.specs(),
        out_shape=[jax.ShapeDtypeStruct((rows, D_MODEL), BF16),
                   jax.ShapeDtypeStruct((rows, D_MODEL), BF16),
                   jax.ShapeDtypeStruct((n_chunks, HEADS, HEAD_DIM, HEAD_DIM), F32)] + ride.out_shapes,
        scratch_shapes=[pltpu.VMEM((HEADS, HEAD_DIM, HEAD_DIM), F32)] + ride.scratch(),
        compiler_params=pltpu.CompilerParams(dimension_semantics=("arbitrary",), vmem_limit_bytes=VMEM_LIMIT,
                                             has_side_effects=True),
    )(proj, proj, proj, proj, cos2, sin2, dec, *ride.arrays)


def _lru_gates(c, ba, bx, lam, wa_ref, wx_ref):
    pre_r, pre_i = [], []
    for g in range(LRU_BLOCKS):
        cg = c[:, LRU_BLOCK * g:LRU_BLOCK * (g + 1)].astype(BF16)
        pre_r.append(_dot(cg, wa_ref[g]))
        pre_i.append(_dot(cg, wx_ref[g]))
    r = _sigmoid(jnp.concatenate(pre_r, axis=1) + ba)
    ig = _sigmoid(jnp.concatenate(pre_i, axis=1) + bx)
    sp = jnp.maximum(-lam, 0.0) + jnp.log(1.0 + jnp.exp(-jnp.abs(lam)))
    log_a = -LRU_C * r * sp
    a = jnp.exp(log_a)
    mult = jnp.sqrt(-jnp.tanh(log_a) * (a * a + 1.0))
    return r, ig, a, mult, sp


def _conv_taps(xbuf, tm, cw_ref, cb_ref):
    c = cb_ref[...] + cw_ref[3:4, :] * xbuf[8:8 + tm, :]
    for back in (1, 2, 3):
        c = c + cw_ref[3 - back:4 - back, :] * xbuf[8 - back:8 - back + tm, :]
    return c


def _lru_fwd(proj, conv_w, conv_b, ba, bx, lam, wa_g, wx_g, ride):
    rows = proj.shape[0]
    tm = _tile(rows, 320)
    n_t = rows // tm
    n_r = ride.n

    def body(x_ref, gt_ref, cw_ref, cb_ref, ba_ref, bx_ref, lam_ref, wa_ref, wx_ref, *refs):
        hs_ref, zl_ref = refs[n_r:n_r + 2]
        xbuf, abuf, ubuf, hcar = refs[2 * n_r + 2:2 * n_r + 6]
        push = ride.push(refs[:n_r], refs[n_r + 2:2 * n_r + 2], refs[2 * n_r + 6:])
        i = pl.program_id(0)

        @pl.when(i == 0)
        def _():
            push.start()
            xbuf[0:8, :] = jnp.zeros((8, D_MODEL), F32)
            hcar[...] = jnp.zeros_like(hcar)

        xbuf[8:8 + tm, :] = x_ref[...].astype(F32)
        c = _conv_taps(xbuf, tm, cw_ref, cb_ref)
        xbuf[0:8, :] = xbuf[tm:tm + 8, :]
        r, ig, a, mult, _ = _lru_gates(c, ba_ref[...], bx_ref[...], lam_ref[...], wa_ref, wx_ref)
        row = i * tm + lax.broadcasted_iota(jnp.int32, (tm, 1), 0)
        abuf[...] = a
        ubuf[...] = jnp.where(row >= PAD_ROWS, mult * (ig * c), 0.0)

        sub = lax.broadcasted_iota(jnp.int32, (8, D_MODEL), 0)

        def block(b, carry):
            off = pl.multiple_of(b * 8, 8)
            av, uv = abuf[pl.ds(off, 8), :], ubuf[pl.ds(off, 8), :]
            for s in (1, 2, 4):
                us = jnp.where(sub >= s, pltpu.roll(uv, s, 0), 0.0)
                as_ = jnp.where(sub >= s, pltpu.roll(av, s, 0), 1.0)
                uv = uv + av * us
                av = av * as_
            hv = uv + av * carry
            ubuf[pl.ds(off, 8), :] = hv
            return hv[7:8, :]

        hcar[...] = lax.fori_loop(0, tm // 8, block, hcar[...])
        gl, _ = _gelu_parts(gt_ref[...].astype(F32))
        hs = ubuf[...]
        hs_ref[...] = hs.astype(BF16)
        zl_ref[...] = (gl * hs).astype(BF16)

        @pl.when(i == n_t - 1)
        def _():
            push.wait()

    vec = pl.BlockSpec((1, D_MODEL), lambda i: (0, 0))
    mat = pl.BlockSpec((LRU_BLOCKS, LRU_BLOCK, LRU_BLOCK), lambda i: (0, 0, 0))
    row = pl.BlockSpec((tm, D_MODEL), lambda i: (i, 0))
    return pl.pallas_call(
        body, name="lru_fwd", grid=(n_t,),
        in_specs=[_seg_spec(tm, 4), _seg_spec(tm, 5), pl.BlockSpec((4, D_MODEL), lambda i: (0, 0)),
                  vec, vec, vec, vec, mat, mat] + ride.specs(),
        out_specs=[row, row] + ride.specs(),
        out_shape=[jax.ShapeDtypeStruct((rows, D_MODEL), BF16)] * 2 + ride.out_shapes,
        scratch_shapes=[pltpu.VMEM((tm + 8, D_MODEL), F32), pltpu.VMEM((tm, D_MODEL), F32),
                        pltpu.VMEM((tm, D_MODEL), F32), pltpu.VMEM((1, D_MODEL), F32)] + ride.scratch(),
        compiler_params=pltpu.CompilerParams(dimension_semantics=("arbitrary",), vmem_limit_bytes=VMEM_LIMIT,
                                             has_side_effects=True),
    )(proj, proj, conv_w, conv_b, ba, bx, lam, wa_g, wx_g, *ride.arrays)


def _mix_fwd(zr, zl, proj, h0, wbr, wbl, wout, ride):
    rows = h0.shape[0]
    tm = _tile(rows, 640)
    n_t = rows // tm
    n_r = ride.n

    def body(zr_ref, zl_ref, ga_ref, gb_ref, h0_ref, wbr_ref, wbl_ref, wo_ref, *refs):
        h1_ref, yr_ref, yl_ref, mx_ref = refs[n_r:n_r + 4]
        push = ride.push(refs[:n_r], refs[n_r + 4:2 * n_r + 4], refs[2 * n_r + 4:])

        @pl.when(pl.program_id(0) == 0)
        def _():
            push.start()

        yr = _dot(zr_ref[...], wbr_ref[...])
        yl = _dot(zl_ref[...], wbl_ref[...])
        mixed = (_sigmoid(ga_ref[...].astype(F32)) * yr + _sigmoid(gb_ref[...].astype(F32)) * yl).astype(BF16)
        yr_ref[...] = yr.astype(BF16)
        yl_ref[...] = yl.astype(BF16)
        mx_ref[...] = mixed
        h1_ref[...] = h0_ref[...] + _dot(mixed, wo_ref[...])

        @pl.when(pl.program_id(0) == n_t - 1)
        def _():
            push.wait()

    row = pl.BlockSpec((tm, D_MODEL), lambda i: (i, 0))
    wsp = pl.BlockSpec((D_MODEL, D_MODEL), lambda i: (0, 0))
    return pl.pallas_call(
        body, name="mix_fwd", grid=(n_t,),
        in_specs=[row, row, _seg_spec(tm, 6), _seg_spec(tm, 7), row, wsp, wsp, wsp] + ride.specs(),
        out_specs=[row, row, row, row] + ride.specs(),
        out_shape=[jax.ShapeDtypeStruct((rows, D_MODEL), F32)] + [jax.ShapeDtypeStruct((rows, D_MODEL), BF16)] * 3
        + ride.out_shapes,
        scratch_shapes=ride.scratch(),
        compiler_params=pltpu.CompilerParams(dimension_semantics=("arbitrary",), vmem_limit_bytes=VMEM_LIMIT,
                                             has_side_effects=True),
    )(zr, zl, proj, proj, h0, wbr, wbl, wout, *ride.arrays)


def _ffn_fwd_loss(h1, norm_w, wfi_g, wfo_g, final_w, target):
    rows = h1.shape[0]
    tm = _tile(rows, 640)
    last = FFN_GROUPS - 1

    def body(h1_ref, nw_ref, wg_ref, wu_ref, wo_ref, fw_ref, t_ref,
             u2_ref, g_ref, up_ref, act_ref, dh2_ref, red_ref, u2_s, acc):
        i, d = pl.program_id(0), pl.program_id(1)

        @pl.when(jnp.logical_and(i == 0, d == 0))
        def _():
            red_ref[...] = jnp.zeros_like(red_ref)

        @pl.when(d == 0)
        def _():
            x = h1_ref[...]
            rs = lax.rsqrt(jnp.mean(x * x, axis=-1, keepdims=True) + NORM_EPS)
            u2 = (x * rs * nw_ref[...]).astype(BF16)
            u2_s[...] = u2
            u2_ref[...] = u2
            acc[...] = jnp.zeros_like(acc)

        g = _dot(u2_s[...], wg_ref[...])
        up = _dot(u2_s[...], wu_ref[...])
        act = (g * _sigmoid(g) * up).astype(BF16)
        g_ref[...] = g.astype(BF16)
        up_ref[...] = up.astype(BF16)
        act_ref[...] = act
        acc[...] += _dot(act, wo_ref[...])

        @pl.when(d == last)
        def _():
            h2 = h1_ref[...] + acc[...]
            rs = lax.rsqrt(jnp.mean(h2 * h2, axis=-1, keepdims=True) + NORM_EPS)
            nh = h2 * rs
            fw = fw_ref[...]
            row = i * tm + lax.broadcasted_iota(jnp.int32, (tm, 1), 0)
            diff = jnp.where(row >= CHUNK, nh * fw - t_ref[...], 0.0)
            dy = diff * (1.0 / D_MODEL)
            red_ref[0:1, :] += jnp.sum(diff * diff, axis=0, keepdims=True)
            red_ref[1:2, :] += jnp.sum(dy * nh, axis=0, keepdims=True)
            dn = dy * fw
            dh2_ref[...] = rs * (dn - nh * jnp.mean(dn * nh, axis=-1, keepdims=True))

    row = pl.BlockSpec((tm, D_MODEL), lambda i, d: (i, 0))
    vec = pl.BlockSpec((1, D_MODEL), lambda i, d: (0, 0))
    hid = pl.BlockSpec((tm, FFN_GROUP), lambda i, d: (i, d))
    hid_shape = jax.ShapeDtypeStruct((rows, FFN_GROUPS * FFN_GROUP), BF16)
    return pl.pallas_call(
        body, name="ffn_fwd_loss", grid=(rows // tm, FFN_GROUPS),
        in_specs=[row, vec,
                  pl.BlockSpec((None, D_MODEL, FFN_GROUP), lambda i, d: (d, 0, 0)),
                  pl.BlockSpec((None, D_MODEL, FFN_GROUP), lambda i, d: (d + FFN_GROUPS, 0, 0)),
                  pl.BlockSpec((None, FFN_GROUP, D_MODEL), lambda i, d: (d, 0, 0)),
                  vec, row],
        out_specs=[row, hid, hid, hid, row, pl.BlockSpec((8, D_MODEL), lambda i, d: (0, 0))],
        out_shape=[jax.ShapeDtypeStruct((rows, D_MODEL), BF16), hid_shape, hid_shape, hid_shape,
                   jax.ShapeDtypeStruct((rows, D_MODEL), F32), jax.ShapeDtypeStruct((8, D_MODEL), F32)],
        scratch_shapes=[pltpu.VMEM((tm, D_MODEL), BF16), pltpu.VMEM((tm, D_MODEL), F32)],
        compiler_params=_cparams(("arbitrary", "arbitrary")),
    )(h1, norm_w, wfi_g, wfi_g, wfo_g, final_w, target)


def _wgrad(a, b, ka, tn, out_dtype, b_halves=False):
    rows = a.shape[0]
    na = a.shape[1] // ka
    tm = _tile(rows, 640)
    nm = rows // tm
    if b_halves:
        per_half = b.shape[2] // tn
        nb = 2 * per_half
        b_spec = pl.BlockSpec((None, tm, tn), lambda p, q, m: (q // per_half, m, q % per_half))
    else:
        nb = b.shape[1] // tn
        b_spec = pl.BlockSpec((tm, tn), lambda p, q, m: (m, q))

    def body(a_ref, b_ref, o_ref, acc):
        m = pl.program_id(2)

        @pl.when(m == 0)
        def _():
            acc[...] = jnp.zeros_like(acc)

        acc[...] += _dot_tn(a_ref[...].astype(BF16), b_ref[...].astype(BF16))

        @pl.when(m == nm - 1)
        def _():
            o_ref[...] = acc[...].astype(out_dtype)

    return pl.pallas_call(
        body, name="wgrad", grid=(na, nb, nm),
        in_specs=[pl.BlockSpec((tm, ka), lambda p, q, m: (m, p)), b_spec],
        out_specs=pl.BlockSpec((None, None, ka, tn), lambda p, q, m: (p, q, 0, 0)),
        out_shape=jax.ShapeDtypeStruct((na, nb, ka, tn), out_dtype),
        scratch_shapes=[pltpu.VMEM((ka, tn), F32)],
        compiler_params=_cparams(("parallel", "parallel", "arbitrary")),
    )(a, b)


def _ffn_bwd(dh2, g, up, h1, norm_w, wfi_g, wfo_g):
    rows = h1.shape[0]
    tm = _tile(rows, 640)
    last = FFN_GROUPS - 1

    def body(dh2_ref, g_ref, up_ref, h1_ref, nw_ref, wg_ref, wu_ref, wo_ref, dgu_ref, dh1_ref, dw_ref,
             dh2_s, acc):
        i, d = pl.program_id(0), pl.program_id(1)

        @pl.when(jnp.logical_and(i == 0, d == 0))
        def _():
            dw_ref[...] = jnp.zeros_like(dw_ref)

        @pl.when(d == 0)
        def _():
            dh2_s[...] = dh2_ref[...].astype(BF16)
            acc[...] = jnp.zeros_like(acc)

        dact = _dot_nt(dh2_s[...], wo_ref[...])
        gv, uv = g_ref[...].astype(F32), up_ref[...].astype(F32)
        sg = _sigmoid(gv)
        dg = (dact * uv * (sg * (1.0 + gv * (1.0 - sg)))).astype(BF16)
        dup = (dact * (gv * sg)).astype(BF16)
        dgu_ref[0] = dg
        dgu_ref[1] = dup
        acc[...] += _dot_nt(dg, wg_ref[...]) + _dot_nt(dup, wu_ref[...])

        @pl.when(d == last)
        def _():
            dx, dw = _rms_bwd(h1_ref[...], nw_ref[...], acc[...])
            dw_ref[0:1, :] += dw
            dh1_ref[...] = dh2_ref[...] + dx

    row = pl.BlockSpec((tm, D_MODEL), lambda i, d: (i, 0))
    vec = pl.BlockSpec((1, D_MODEL), lambda i, d: (0, 0))
    hid = pl.BlockSpec((tm, FFN_GROUP), lambda i, d: (i, d))
    return pl.pallas_call(
        body, name="ffn_bwd", grid=(rows // tm, FFN_GROUPS),
        in_specs=[row, hid, hid, row, vec,
                  pl.BlockSpec((None, D_MODEL, FFN_GROUP), lambda i, d: (d, 0, 0)),
                  pl.BlockSpec((None, D_MODEL, FFN_GROUP), lambda i, d: (d + FFN_GROUPS, 0, 0)),
                  pl.BlockSpec((None, FFN_GROUP, D_MODEL), lambda i, d: (d, 0, 0))],
        out_specs=[pl.BlockSpec((2, tm, FFN_GROUP), lambda i, d: (0, i, d)), row,
                   pl.BlockSpec((8, D_MODEL), lambda i, d: (0, 0))],
        out_shape=[jax.ShapeDtypeStruct((2, rows, FFN_GROUPS * FFN_GROUP), BF16),
                   jax.ShapeDtypeStruct((rows, D_MODEL), F32), jax.ShapeDtypeStruct((8, D_MODEL), F32)],
        scratch_shapes=[pltpu.VMEM((tm, D_MODEL), BF16), pltpu.VMEM((tm, D_MODEL), F32)],
        compiler_params=_cparams(("arbitrary", "arbitrary")),
    )(dh2, g, up, h1, norm_w, wfi_g, wfi_g, wfo_g)


def _mix_bwd(dh1, yr, yl, proj, wbr, wbl, wout):
    rows = dh1.shape[0]
    tm = _tile(rows, 640)

    def body(dh1_ref, yr_ref, yl_ref, ga_ref, gb_ref, wbr_ref, wbl_ref, wo_ref,
             dyr_ref, dyl_ref, dseg_ref, dzr_ref, dzl_ref):
        dmix = _dot_nt(dh1_ref[...].astype(BF16), wo_ref[...])
        sa, sb = _sigmoid(ga_ref[...].astype(F32)), _sigmoid(gb_ref[...].astype(F32))
        dyr = (dmix * sa).astype(BF16)
        dyl = (dmix * sb).astype(BF16)
        dyr_ref[...] = dyr
        dyl_ref[...] = dyl
        dseg_ref[:, 0:D_MODEL] = (dmix * yr_ref[...].astype(F32) * (sa * (1.0 - sa))).astype(BF16)
        dseg_ref[:, D_MODEL:2 * D_MODEL] = (dmix * yl_ref[...].astype(F32) * (sb * (1.0 - sb))).astype(BF16)
        dzr_ref[...] = _dot_nt(dyr, wbr_ref[...]).astype(BF16)
        dzl_ref[...] = _dot_nt(dyl, wbl_ref[...]).astype(BF16)

    row = pl.BlockSpec((tm, D_MODEL), lambda i: (i, 0))
    wsp = pl.BlockSpec((D_MODEL, D_MODEL), lambda i: (0, 0))
    bshape = jax.ShapeDtypeStruct((rows, D_MODEL), BF16)
    return pl.pallas_call(
        body, name="mix_bwd", grid=(rows // tm,),
        in_specs=[row, row, row, _seg_spec(tm, 6), _seg_spec(tm, 7), wsp, wsp, wsp],
        out_specs=[row, row, pl.BlockSpec((tm, 2 * D_MODEL), lambda i: (i, 3)), row, row],
        out_shape=[bshape, bshape, jax.ShapeDtypeStruct((rows, N_DEV * D_MODEL), BF16), bshape, bshape],
        compiler_params=_cparams(("parallel",)),
    )(dh1, yr, yl, proj, proj, wbr, wbl, wout)


S1_SHAPES = [
    jax.ShapeDtypeStruct((N_DEV, D_MODEL, FFN_GROUP), BF16),
    jax.ShapeDtypeStruct((N_DEV, FFN_OUT_SHARD, D_MODEL), BF16),
]


def _s1_parts(ins, p):
    return [ins[0].at[p], ins[1].at[p // 2, _half_rows(p), :]]


def _lru_bwd(dzl, hs, proj, dproj, conv_w, conv_b, ba, bx, lam, wa_g, wx_g, s1_grads):
    rows = dzl.shape[0]
    tm = _tile(rows, 320)
    nt = rows // tm
    t8 = tm // 8
    n_s1 = len(s1_grads)

    def body(dzl_ref, hs_ref, hsp_ref, x_ref, xp_ref, gt_ref, cw_ref, cb_ref, ba_ref, bx_ref, lam_ref,
             wa_ref, wx_ref, dproj_in, *refs):
        del dproj_in
        s1_refs = refs[:n_s1]
        dseg_ref, dwa_ref, dwx_ref, sm_ref = refs[n_s1:n_s1 + 4]
        land_refs = refs[n_s1 + 4:2 * n_s1 + 4]
        xbuf, abuf, bbuf, dbuf, dcbuf, anext, dhcar, send_sems, recv_sems, loc_sems = refs[2 * n_s1 + 4:]
        step = pl.program_id(0)
        i = nt - 1 - step
        push = _Push(lambda p: _s1_parts(s1_refs, p), lambda s: [r.at[s] for r in land_refs],
                     (send_sems, recv_sems, loc_sems), n_s1)

        @pl.when(step == 0)
        def _():
            push.start()
            dwa_ref[...] = jnp.zeros_like(dwa_ref)
            dwx_ref[...] = jnp.zeros_like(dwx_ref)
            sm_ref[...] = jnp.zeros_like(sm_ref)
            anext[...] = jnp.zeros_like(anext)
            dhcar[...] = jnp.zeros_like(dhcar)
            dcbuf[tm:tm + 8, :] = jnp.zeros((8, D_MODEL), F32)

        first = i == 0
        x_prev = jnp.where(first, 0.0, xp_ref[8:16, :].astype(F32))
        x_v = x_ref[...].astype(F32)
        xbuf[0:8, :] = x_prev
        xbuf[8:8 + tm, :] = x_v
        c = _conv_taps(xbuf, tm, cw_ref, cb_ref)
        lam_v = lam_ref[...]
        r, ig, a, mult, sp = _lru_gates(c, ba_ref[...], bx_ref[...], lam_v, wa_ref, wx_ref)
        hs_v = hs_ref[...].astype(F32)
        gl, dgl = _gelu_parts(gt_ref[...].astype(F32))
        dzl_v = dzl_ref[...].astype(F32)
        dseg_ref[:, D_MODEL:2 * D_MODEL] = (dzl_v * hs_v * dgl).astype(BF16)
        dbuf[...] = dzl_v * gl
        abuf[0:tm, :] = a
        abuf[tm:tm + 8, :] = jnp.broadcast_to(anext[...], (8, D_MODEL))
        bbuf[...] = abuf[1:tm + 1, :]

        sub = lax.broadcasted_iota(jnp.int32, (8, D_MODEL), 0)

        def block(k, carry):
            off = pl.multiple_of((t8 - 1 - k) * 8, 8)
            av = bbuf[pl.ds(off, 8), :]
            uv = dbuf[pl.ds(off, 8), :]
            for s in (1, 2, 4):
                us = jnp.where(sub < 8 - s, pltpu.roll(uv, 8 - s, 0), 0.0)
                as_ = jnp.where(sub < 8 - s, pltpu.roll(av, 8 - s, 0), 1.0)
                uv = uv + av * us
                av = av * as_
            hv = uv + av * carry
            dbuf[pl.ds(off, 8), :] = hv
            return hv[0:1, :]

        dhcar[...] = lax.fori_loop(0, t8, block, dhcar[...])
        anext[...] = abuf[0:1, :]
        dh = dbuf[...]

        xbuf[0:8, :] = jnp.where(first, 0.0, hsp_ref[8:16, :].astype(F32))
        xbuf[8:8 + tm, :] = hs_v
        hprev = xbuf[7:7 + tm, :]
        row = i * tm + lax.broadcasted_iota(jnp.int32, (tm, 1), 0)
        duu = jnp.where(row >= PAD_ROWS, dh, 0.0)
        da = dh * hprev
        dmult = duu * ig * c
        di = duu * mult * c
        dc = duu * mult * ig
        dlog_a = da * a - dmult * (a * a) / mult
        dr = dlog_a * (-LRU_C * sp)
        dsp = jnp.sum(dlog_a * (-LRU_C * r), axis=0, keepdims=True)
        dpr = dr * r * (1.0 - r)
        dpi = di * ig * (1.0 - ig)
        dpr_b, dpi_b = dpr.astype(BF16), dpi.astype(BF16)
        dcs = []
        for g in range(LRU_BLOCKS):
            sl = slice(LRU_BLOCK * g, LRU_BLOCK * (g + 1))
            cg = c[:, sl].astype(BF16)
            dwa_ref[g] += _dot_tn(cg, dpr_b[:, sl])
            dwx_ref[g] += _dot_tn(cg, dpi_b[:, sl])
            dcs.append(_dot_nt(dpr_b[:, sl], wa_ref[g]) + _dot_nt(dpi_b[:, sl], wx_ref[g]))
        dc = dc + jnp.concatenate(dcs, axis=1)

        dcbuf[0:tm, :] = dc
        xbuf[8:8 + tm, :] = x_v
        xbuf[0:8, :] = x_prev
        dlin = cw_ref[3:4, :] * dc
        sm_ref[3:4, :] += jnp.sum(dc * xbuf[8:8 + tm, :], axis=0, keepdims=True)
        for back in (1, 2, 3):
            dlin = dlin + cw_ref[3 - back:4 - back, :] * dcbuf[back:back + tm, :]
            sm_ref[3 - back:4 - back, :] += jnp.sum(dc * xbuf[8 - back:8 - back + tm, :], axis=0, keepdims=True)
        dseg_ref[:, 0:D_MODEL] = dlin.astype(BF16)
        dcbuf[tm:tm + 8, :] = dcbuf[0:8, :]
        sm_ref[4:5, :] += jnp.sum(dc, axis=0, keepdims=True)
        sm_ref[5:6, :] += jnp.sum(dpr, axis=0, keepdims=True)
        sm_ref[6:7, :] += jnp.sum(dpi, axis=0, keepdims=True)
        sm_ref[7:8, :] += dsp * (-_sigmoid(-lam_v))

        @pl.when(step == nt - 1)
        def _():
            push.wait()

    rowb = pl.BlockSpec((tm, D_MODEL), lambda s: (nt - 1 - s, 0))
    t16 = tm // 16
    prev8 = pl.BlockSpec((16, D_MODEL), lambda s: (jnp.maximum((nt - 1 - s) * t16 - 1, 0), 0))
    seg = lambda k: pl.BlockSpec((tm, D_MODEL), lambda s, k=k: (nt - 1 - s, k))
    prev8_seg4 = pl.BlockSpec((16, D_MODEL), lambda s: (jnp.maximum((nt - 1 - s) * t16 - 1, 0), 4))
    vec = pl.BlockSpec((1, D_MODEL), lambda s: (0, 0))
    mat = pl.BlockSpec((LRU_BLOCKS, LRU_BLOCK, LRU_BLOCK), lambda s: (0, 0, 0))
    mshape = jax.ShapeDtypeStruct((LRU_BLOCKS, LRU_BLOCK, LRU_BLOCK), F32)
    n_in = 13
    return pl.pallas_call(
        body, name="lru_bwd", grid=(nt,),
        in_specs=[rowb, rowb, prev8, seg(4), prev8_seg4, seg(5), pl.BlockSpec((4, D_MODEL), lambda s: (0, 0)),
                  vec, vec, vec, vec, mat, mat, ANY] + [ANY] * n_s1,
        out_specs=[pl.BlockSpec((tm, 2 * D_MODEL), lambda s: (nt - 1 - s, 2)), mat, mat,
                   pl.BlockSpec((8, D_MODEL), lambda s: (0, 0))] + [ANY] * n_s1,
        out_shape=[jax.ShapeDtypeStruct(dproj.shape, dproj.dtype), mshape, mshape,
                   jax.ShapeDtypeStruct((8, D_MODEL), F32)] + S1_SHAPES,
        input_output_aliases={n_in: 0},
        scratch_shapes=[pltpu.VMEM((tm + 8, D_MODEL), F32), pltpu.VMEM((tm + 8, D_MODEL), F32),
                        pltpu.VMEM((tm, D_MODEL), F32), pltpu.VMEM((tm, D_MODEL), F32),
                        pltpu.VMEM((tm + 8, D_MODEL), F32),
                        pltpu.VMEM((1, D_MODEL), F32), pltpu.VMEM((1, D_MODEL), F32)] + _push_sems(n_s1),
        compiler_params=pltpu.CompilerParams(dimension_semantics=("arbitrary",), vmem_limit_bytes=VMEM_LIMIT,
                                             has_side_effects=True),
    )(dzl, hs, hs, proj, proj, proj, conv_w, conv_b, ba, bx, lam, wa_g, wx_g, dproj, *s1_grads)


def _retention_bwd(dzr, o, proj, states, cos2, sin2, dec, dproj, ride):
    rows = dzr.shape[0]
    n_chunks = rows // CHUNK
    n_r = ride.n

    def body(dzr_ref, o_ref, q_ref, k_ref, v_ref, g_ref, st_ref, c_ref, s_ref, dec_ref, dproj_in, *refs):
        del dproj_in
        dseg_ref = refs[n_r]
        dstate = refs[2 * n_r + 1]
        push = ride.push(refs[:n_r], refs[n_r + 1:2 * n_r + 1], refs[2 * n_r + 2:])

        @pl.when(pl.program_id(0) == 0)
        def _():
            push.start()
            dstate[...] = jnp.zeros_like(dstate)
        cos_t, sin_t = c_ref[...], s_ref[...]
        for h in range(HEADS):
            sl = slice(HEAD_DIM * h, HEAD_DIM * (h + 1))
            o = o_ref[:, sl].astype(F32)
            g = g_ref[:, sl].astype(F32)
            dzr_v = dzr_ref[:, sl].astype(F32)
            sg = _sigmoid(g)
            r = lax.rsqrt(jnp.mean(o * o, axis=-1, keepdims=True) + NORM_EPS)
            on = o * r
            dseg_ref[:, 3 * D_MODEL + HEAD_DIM * h:3 * D_MODEL + HEAD_DIM * (h + 1)] = (
                dzr_v * on * (sg * (1.0 + g * (1.0 - sg)))).astype(BF16)
            don = dzr_v * (g * sg)
            do = r * (don - on * jnp.mean(don * on, axis=-1, keepdims=True))
            dob = do.astype(BF16)

            qh = _rot(q_ref[:, sl].astype(F32), cos_t, sin_t)
            kh = _rot(k_ref[:, sl].astype(F32), cos_t, sin_t) * QK_SCALE
            qb, kb, vb = qh.astype(BF16), kh.astype(BF16), v_ref[:, sl]
            intra, qd, kd, cd = dec_ref[0, h], dec_ref[1, h], dec_ref[2, h], dec_ref[3, h]
            s = (_dot_nt(qb, kb) * intra).astype(BF16)
            ds = (_dot_nt(dob, vb) * intra).astype(BF16)
            st_b = st_ref[h].astype(BF16)
            dst = dstate[h]
            dst_b = dst.astype(BF16)
            dv = _dot_tn(s, dob) + _dot((kh * kd).astype(BF16), dst_b)
            dq = _dot(ds, kb) + _dot_nt(dob, st_b) * qd
            dk = _dot_tn(ds, qb) + _dot_nt(vb, dst_b) * kd
            dstate[h] = dst * cd + _dot_tn((qh * qd).astype(BF16), dob)
            dseg_ref[:, 2 * D_MODEL + HEAD_DIM * h:2 * D_MODEL + HEAD_DIM * (h + 1)] = dv.astype(BF16)
            dseg_ref[:, sl] = _rot_t(dq, cos_t, sin_t).astype(BF16)
            dseg_ref[:, D_MODEL + HEAD_DIM * h:D_MODEL + HEAD_DIM * (h + 1)] = (
                _rot_t(dk, cos_t, sin_t) * QK_SCALE).astype(BF16)

        @pl.when(pl.program_id(0) == n_chunks - 1)
        def _():
            push.wait()

    rev = lambda s: n_chunks - 1 - s
    rowb = pl.BlockSpec((CHUNK, D_MODEL), lambda s: (rev(s), 0))
    seg = lambda k: pl.BlockSpec((CHUNK, D_MODEL), lambda s, k=k: (rev(s), k))
    tab = pl.BlockSpec((CHUNK, HEAD_DIM), lambda s: (rev(s), 0))
    return pl.pallas_call(
        body, name="retention_bwd", grid=(n_chunks,),
        in_specs=[rowb, rowb, seg(0), seg(1), seg(2), seg(3),
                  pl.BlockSpec((None, HEADS, HEAD_DIM, HEAD_DIM), lambda s: (rev(s), 0, 0, 0)), tab, tab,
                  pl.BlockSpec((4, HEADS, CHUNK, CHUNK), lambda s: (0, 0, 0, 0)), ANY] + ride.specs(),
        out_specs=[pl.BlockSpec((CHUNK, 4 * D_MODEL), lambda s: (rev(s), 0))] + ride.specs(),
        out_shape=[jax.ShapeDtypeStruct(dproj.shape, dproj.dtype)] + ride.out_shapes,
        input_output_aliases={10: 0},
        scratch_shapes=[pltpu.VMEM((HEADS, HEAD_DIM, HEAD_DIM), F32)] + ride.scratch(),
        compiler_params=pltpu.CompilerParams(dimension_semantics=("arbitrary",), vmem_limit_bytes=VMEM_LIMIT,
                                             has_side_effects=True),
    )(dzr, o, proj, proj, proj, proj, states, cos2, sin2, dec, dproj, *ride.arrays)


S2_SHAPES = [
    jax.ShapeDtypeStruct((N_DEV, D_MODEL, D_MODEL), BF16),
    jax.ShapeDtypeStruct((N_DEV, LRU_BLOCKS, LRU_ROWS, LRU_BLOCK), F32),
    jax.ShapeDtypeStruct((N_DEV, LRU_BLOCKS, LRU_ROWS, LRU_BLOCK), F32),
]


def _s2_parts(ins, p):
    rows_p = pl.ds(pl.multiple_of(p * LRU_ROWS, 8), LRU_ROWS)
    return [ins[0].at[p], ins[1].at[:, rows_p, :], ins[2].at[:, rows_p, :]]


def _in_proj_bwd(dproj, win_g, h0, norm_w, dh1, s2_grads):
    rows = h0.shape[0]
    tm = _tile(rows, 640)
    n_i = rows // tm
    n_s2 = len(s2_grads)

    def body(dseg_ref, w_ref, h0_ref, nw_ref, dh1_ref, *refs):
        s2_refs = refs[:n_s2]
        dh0_ref, dw_ref = refs[n_s2:n_s2 + 2]
        land_refs = refs[n_s2 + 2:2 * n_s2 + 2]
        acc, send_sems, recv_sems, loc_sems = refs[2 * n_s2 + 2:]
        i, j = pl.program_id(0), pl.program_id(1)
        push = _Push(lambda p: _s2_parts(s2_refs, p), lambda s: [r.at[s] for r in land_refs],
                     (send_sems, recv_sems, loc_sems), n_s2)

        @pl.when(jnp.logical_and(i == 0, j == 0))
        def _():
            push.start()
            dw_ref[...] = jnp.zeros_like(dw_ref)

        @pl.when(j == 0)
        def _():
            acc[...] = jnp.zeros_like(acc)

        acc[...] += _dot_nt(dseg_ref[...], w_ref[...])

        @pl.when(j == N_DEV - 1)
        def _():
            dx, dw = _rms_bwd(h0_ref[...], nw_ref[...], acc[...])
            dw_ref[0:1, :] += dw
            dh0_ref[...] = dh1_ref[...] + dx

        @pl.when(jnp.logical_and(i == n_i - 1, j == N_DEV - 1))
        def _():
            push.wait()

    row = pl.BlockSpec((tm, D_MODEL), lambda i, j: (i, 0))
    vec = pl.BlockSpec((1, D_MODEL), lambda i, j: (0, 0))
    return pl.pallas_call(
        body, name="in_proj_bwd", grid=(n_i, N_DEV),
        in_specs=[pl.BlockSpec((tm, D_MODEL), lambda i, j: (i, j)),
                  pl.BlockSpec((None, D_MODEL, D_MODEL), lambda i, j: (j, 0, 0)), row, vec, row] + [ANY] * n_s2,
        out_specs=[row, pl.BlockSpec((8, D_MODEL), lambda i, j: (0, 0))] + [ANY] * n_s2,
        out_shape=[jax.ShapeDtypeStruct((rows, D_MODEL), F32), jax.ShapeDtypeStruct((8, D_MODEL), F32)] + S2_SHAPES,
        scratch_shapes=[pltpu.VMEM((tm, D_MODEL), F32)] + _push_sems(n_s2),
        compiler_params=pltpu.CompilerParams(dimension_semantics=("arbitrary", "arbitrary"),
                                             vmem_limit_bytes=VMEM_LIMIT, has_side_effects=True),
    )(dproj, win_g, h0, norm_w, dh1, *s2_grads)


def _adamw(g_slots, w, m, v):
    slots, rows, cols = g_slots.shape
    tr = rows
    for cand in (256, 128, 64, 32, 16, 8):
        if rows % cand == 0 and rows > cand:
            tr = cand
            break

    def body(g_ref, w_ref, m_ref, v_ref, go_ref, d_ref, mo_ref, vo_ref):
        g = g_ref[0].astype(F32)
        for s in range(1, slots):
            g = g + g_ref[s].astype(F32)
        m2 = ADAM_B1 * m_ref[...] + (1.0 - ADAM_B1) * g
        v2 = ADAM_B2 * v_ref[...] + (1.0 - ADAM_B2) * (g * g)
        m_hat = m2 / (1.0 - ADAM_B1 ** ADAM_STEP)
        v_hat = v2 / (1.0 - ADAM_B2 ** ADAM_STEP)
        go_ref[...] = g
        d_ref[...] = -ADAM_LR * (m_hat / (jnp.sqrt(v_hat) + ADAM_EPS) + ADAM_WD * w_ref[...])
        mo_ref[...] = m2
        vo_ref[...] = v2

    blk = pl.BlockSpec((tr, cols), lambda i: (i, 0))
    shape = jax.ShapeDtypeStruct((rows, cols), F32)
    return pl.pallas_call(
        body, name="adamw", grid=(rows // tr,),
        in_specs=[pl.BlockSpec((slots, tr, cols), lambda i: (0, i, 0)), blk, blk, blk],
        out_specs=[blk] * 4, out_shape=[shape] * 4,
        compiler_params=_cparams(("parallel",)),
    )(g_slots, w, m, v)


def _sum_slots(packs):
    slots, rows, cols = packs.shape

    def body(p_ref, o_ref):
        acc = p_ref[0]
        for s in range(1, slots):
            acc = acc + p_ref[s]
        o_ref[...] = acc

    return pl.pallas_call(
        body, name="sum_slots", out_shape=jax.ShapeDtypeStruct((rows, cols), F32),
        compiler_params=pltpu.CompilerParams(vmem_limit_bytes=VMEM_LIMIT),
    )(packs)


def _gather_small(small):
    shapes = [jax.ShapeDtypeStruct((N_DEV,) + small.shape, F32)]
    return _push_call("gather_small", [small], shapes,
                      lambda ins, p: list(ins), lambda outs, s: [r.at[s] for r in outs])[0]


def _share_pack(pack):
    shapes = [jax.ShapeDtypeStruct((N_DEV,) + pack.shape, F32)]
    return _push_call("share_pack", [pack], shapes,
                      lambda ins, p: list(ins), lambda outs, s: [r.at[s] for r in outs])[0]


PACK_MIX_NORM, PACK_CONV_W, PACK_CONV_B, PACK_BA, PACK_BX, PACK_LAM = 0, 8, 12, 13, 14, 15
PACK_FFN_NORM, PACK_SQ_ERR, PACK_FINAL_NORM, PACK_META = 16, 24, 25, 32


def kernel(x, meta_tokens, mix_norm_w, w_in, conv_w, conv_b, lru_wa, lru_ba, lru_wx, lru_bx, lru_lambda, w_branch_ret, w_branch_lru, w_out, ffn_norm_w, w_ffn_in, w_ffn_out, final_norm_w, loss_target, m_meta_tokens, m_mix_norm_w, m_w_in, m_conv_w, m_conv_b, m_lru_wa, m_lru_ba, m_lru_wx, m_lru_bx, m_lru_lambda, m_w_branch_ret, m_w_branch_lru, m_w_out, m_ffn_norm_w, m_w_ffn_in, m_w_ffn_out, m_final_norm_w, v_meta_tokens, v_mix_norm_w, v_w_in, v_conv_w, v_conv_b, v_lru_wa, v_lru_ba, v_lru_wx, v_lru_bx, v_lru_lambda, v_w_branch_ret, v_w_branch_lru, v_w_out, v_ffn_norm_w, v_w_ffn_in, v_w_ffn_out, v_final_norm_w):
    me = _my_index()
    pad4 = ((0, 4), (0, 0))
    fw = final_norm_w.reshape(1, D_MODEL)

    small = jnp.concatenate([meta_tokens, jnp.pad(conv_w[0], pad4)], axis=0)
    small_g = _gather_small(small)
    meta_full = small_g[:, :N_META].transpose(1, 0, 2).reshape(N_META, D_MODEL)
    conv_w_full = small_g[:, N_META:N_META + 4].transpose(1, 0, 2).reshape(4, D_MODEL)
    mixer_shards = [w_branch_ret[0].astype(BF16), w_branch_lru[0].astype(BF16), w_out[0].astype(BF16),
                    lru_wa[0].astype(BF16), lru_wx[0].astype(BF16)]
    wfi_shard = jnp.pad(w_ffn_in[0].astype(BF16), ((0, 0), (0, FFN_GROUP - FFN_SHARD)))
    own_slot = lambda ins, p: list(ins)
    part_of_owner = lambda ins, p: [r.at[p] for r in ins]

    rows = x.shape[1] + CHUNK
    h0 = jnp.concatenate([jnp.zeros((PAD_ROWS, D_MODEL), F32), meta_full, x[0]], axis=0)
    tgt = jnp.concatenate([jnp.zeros((CHUNK, D_MODEL), F32), loss_target[0]], axis=0)
    cos2, sin2 = _rope_tables(rows)
    dec = _retention_consts()

    proj, u, win_g = _in_proj(h0, mix_norm_w, w_in[0].astype(BF16), me.astype(jnp.int32).reshape(1))
    o, zr, states, wbr_g, wbl_g, wout_g, wa_g, wx_g = _retention_fwd(
        proj, cos2, sin2, dec, _mixer_weights_ride(mixer_shards))
    wbr, wbl, wout = (t.reshape(D_MODEL, D_MODEL) for t in (wbr_g, wbl_g, wout_g))
    gather_wfi = _Ride([wfi_shard], [jax.ShapeDtypeStruct((N_DEV, D_MODEL, FFN_GROUP), BF16)],
                       own_slot, _slot_of_sender)
    hs, zl, wfi_g = _lru_fwd(proj, conv_w_full, conv_b, lru_ba, lru_bx, lru_lambda, wa_g, wx_g, gather_wfi)
    h1, yr, yl, mixed, wfo_g = _mix_fwd(zr, zl, proj, h0, wbr, wbl, wout, _wfo_ride(w_ffn_out[0].astype(BF16)))
    u2, g, up, act, dh2, red = _ffn_fwd_loss(h1, ffn_norm_w, wfi_g, wfo_g, fw, tgt)

    d_wfo = _wgrad(act, dh2, FFN_GROUP, D_MODEL, BF16)[:, 0]
    dgu, dh1, dw_ffn_norm = _ffn_bwd(dh2, g, up, h1, ffn_norm_w, wfi_g, wfo_g)
    d_wfi = _wgrad(u2, dgu, D_MODEL, FFN_GROUP, BF16, b_halves=True)[0]
    d_wout = _wgrad(mixed, dh1, D_MODEL, D_MODEL, BF16)[0, 0]
    dyr, dyl, dproj, dzr, dzl = _mix_bwd(dh1, yr, yl, proj, wbr, wbl, wout)
    d_wbr = _wgrad(zr, dyr, D_MODEL, D_MODEL, BF16)[0, 0]
    d_wbl = _wgrad(zl, dyl, D_MODEL, D_MODEL, BF16)[0, 0]
    dproj, d_wa, d_wx, lru_small, r_fi, r_fo = _lru_bwd(
        dzl, hs, proj, dproj, conv_w_full, conv_b, lru_ba, lru_bx, lru_lambda, wa_g, wx_g, [d_wfi, d_wfo])
    mix_shape = jax.ShapeDtypeStruct((N_DEV, D_MODEL // N_DEV, D_MODEL), BF16)
    scatter_mix = _Ride([t.reshape(mix_shape.shape) for t in (d_wbr, d_wbl, d_wout)], [mix_shape] * 3,
                        part_of_owner, _slot_of_sender)
    dproj, r_br, r_bl, r_out = _retention_bwd(dzr, o, proj, states, cos2, sin2, dec, dproj, scatter_mix)
    d_win = _wgrad(u, dproj, D_MODEL, D_MODEL, BF16)[0]
    dh0, dw_mix_norm, r_in, r_wa, r_wx = _in_proj_bwd(dproj, win_g, h0, mix_norm_w, dh1, [d_win, d_wa, d_wx])
    grad_x = dh0[CHUNK:]

    pack = jnp.concatenate([dw_mix_norm, lru_small, dw_ffn_norm, red, dh0[PAD_ROWS:CHUNK]], axis=0)
    small_sum = _sum_slots(_share_pack(pack))
    loss = (0.5 / D_MODEL) * jnp.sum(small_sum[PACK_SQ_ERR])

    def big_update(slots, w, m, v):
        shape = w.shape
        w2, m2, v2 = (t.reshape(slots.shape[1:]) for t in (w, m, v))
        return [t.reshape(shape) for t in _adamw(slots, w2, m2, v2)]

    res = {}
    res["w_in"] = big_update(r_in, w_in, m_w_in, v_w_in)
    res["w_branch_ret"] = big_update(r_br, w_branch_ret, m_w_branch_ret, v_w_branch_ret)
    res["w_branch_lru"] = big_update(r_bl, w_branch_lru, m_w_branch_lru, v_w_branch_lru)
    res["w_out"] = big_update(r_out, w_out, m_w_out, v_w_out)
    res["w_ffn_in"] = big_update(r_fi[:, :, :FFN_SHARD], w_ffn_in, m_w_ffn_in, v_w_ffn_in)
    res["w_ffn_out"] = big_update(r_fo, w_ffn_out, m_w_ffn_out, v_w_ffn_out)
    res["lru_wa"] = big_update(r_wa.reshape(N_DEV, LRU_BLOCKS * LRU_ROWS, LRU_BLOCK), lru_wa, m_lru_wa, v_lru_wa)
    res["lru_wx"] = big_update(r_wx.reshape(N_DEV, LRU_BLOCKS * LRU_ROWS, LRU_BLOCK), lru_wx, m_lru_wx, v_lru_wx)

    col = me * HEAD_DIM
    g_meta = lax.dynamic_slice(small_sum, (PACK_META, col), (N_META, HEAD_DIM))
    g_conv = lax.dynamic_slice(small_sum, (PACK_CONV_W, col), (8, HEAD_DIM))
    small_names = ["mix_norm_w", "conv_b", "lru_ba", "lru_bx", "lru_lambda", "ffn_norm_w", "final_norm_w"]
    small_rows = [PACK_MIX_NORM, PACK_CONV_B, PACK_BA, PACK_BX, PACK_LAM, PACK_FFN_NORM, PACK_FINAL_NORM]
    small_w = [mix_norm_w, conv_b, lru_ba, lru_bx, lru_lambda, ffn_norm_w, fw]
    small_m = [m_mix_norm_w, m_conv_b, m_lru_ba, m_lru_bx, m_lru_lambda, m_ffn_norm_w, m_final_norm_w.reshape(1, -1)]
    small_v = [v_mix_norm_w, v_conv_b, v_lru_ba, v_lru_bx, v_lru_lambda, v_ffn_norm_w, v_final_norm_w.reshape(1, -1)]

    def pack_small(vec_list, meta_t, conv_t):
        return jnp.concatenate([t.reshape(8, HEAD_DIM) for t in vec_list] + [meta_t, jnp.pad(conv_t[0], pad4)], axis=0)

    g_small = jnp.concatenate([small_sum[r].reshape(8, HEAD_DIM) for r in small_rows] + [g_meta, g_conv], axis=0)
    outs_small = _adamw(g_small[None], pack_small(small_w, meta_tokens, conv_w),
                        pack_small(small_m, m_meta_tokens, m_conv_w), pack_small(small_v, v_meta_tokens, v_conv_w))
    for idx, name in enumerate(small_names):
        shape = final_norm_w.shape if name == "final_norm_w" else (1, D_MODEL)
        res[name] = [t[8 * idx:8 * idx + 8].reshape(shape) for t in outs_small]
    res["meta_tokens"] = [t[56:72] for t in outs_small]
    res["conv_w"] = [t[72:76].reshape(1, 4, HEAD_DIM) for t in outs_small]

    order = ["meta_tokens", "mix_norm_w", "w_in", "conv_w", "conv_b", "lru_wa", "lru_ba", "lru_wx", "lru_bx",
             "lru_lambda", "w_branch_ret", "w_branch_lru", "w_out", "ffn_norm_w", "w_ffn_in", "w_ffn_out",
             "final_norm_w"]
    out = [loss, grad_x[None]]
    for kind in range(4):
        out += [res[name][kind] for name in order]
    return tuple(out)
```

```python
import functools

import numpy as np
import jax
import jax.numpy as jnp
from jax import lax
from jax.experimental import pallas as pl
from jax.experimental.pallas import tpu as pltpu

F32 = jnp.float32
BF16 = jnp.bfloat16

D_MODEL = 1024
N_META = 16
CHUNK = 128
PAD_ROWS = CHUNK - N_META
HEADS = 8
HEAD_DIM = 128
ROPE_BASE = 10000.0
QK_SCALE = HEAD_DIM ** -0.5
LRU_BLOCKS = 4
LRU_BLOCK = 256
LRU_C = 8.0
FFN_HIDDEN = 2816
N_DEV = 8
FFN_SHARD = 2 * FFN_HIDDEN // N_DEV
FFN_GROUP = 768
FFN_GROUPS = 4
FFN_OUT_SHARD = FFN_HIDDEN // N_DEV
NORM_EPS = 1e-6

ADAM_LR = 0.001
ADAM_B1 = 0.9
ADAM_B2 = 0.999
ADAM_EPS = 1e-08
ADAM_WD = 0.01
ADAM_STEP = 10

VMEM_LIMIT = 56 * 1024 * 1024
MESH_ID = pl.DeviceIdType.MESH
ANY = pl.BlockSpec(memory_space=pl.ANY)


def _cparams(sem):
    return pltpu.CompilerParams(dimension_semantics=sem, vmem_limit_bytes=VMEM_LIMIT)


def _tile(rows, cap):
    t = cap - cap % 64
    while rows % t:
        t -= 64
    return t


def _dot(a, b):
    return jnp.dot(a, b, preferred_element_type=F32)


def _dot_nt(a, b):
    return lax.dot_general(a, b, (((1,), (1,)), ((), ())), preferred_element_type=F32)


def _dot_tn(a, b):
    return lax.dot_general(a, b, (((0,), (0,)), ((), ())), preferred_element_type=F32)


def _sigmoid(x):
    return 0.5 * jnp.tanh(0.5 * x) + 0.5


def _gelu_parts(x):
    k = 0.7978845608028654
    inner = k * (x + 0.044715 * x * x * x)
    t = jnp.tanh(inner)
    g = 0.5 * x * (1.0 + t)
    dg = 0.5 * (1.0 + t) + 0.5 * x * (1.0 - t * t) * k * (1.0 + 3.0 * 0.044715 * x * x)
    return g, dg


def _rot(x, cos2, sin2):
    return x * cos2 + pltpu.roll(x, HEAD_DIM // 2, 1) * sin2


def _rot_t(dx, cos2, sin2):
    return dx * cos2 - pltpu.roll(dx, HEAD_DIM // 2, 1) * sin2


def _rms_bwd(x, w, dy):
    rs = lax.rsqrt(jnp.mean(x * x, axis=-1, keepdims=True) + NORM_EPS)
    nh = x * rs
    dw = jnp.sum(dy * nh, axis=0, keepdims=True)
    dn = dy * w
    dx = rs * (dn - nh * jnp.mean(dn * nh, axis=-1, keepdims=True))
    return dx, dw


def _retention_consts():
    h = jnp.arange(HEADS, dtype=F32)
    log_g = jnp.log(1.0 - 2.0 ** (-5.0 - h))
    idx = jnp.arange(CHUNK, dtype=F32)
    diff = idx[:, None] - idx[None, :]
    intra = jnp.where(diff[None] >= 0, jnp.exp(jnp.maximum(diff, 0.0)[None] * log_g[:, None, None]), 0.0)
    q_decay = jnp.exp((idx + 1.0)[:, None] * log_g[None, :])
    k_decay = jnp.exp((CHUNK - 1.0 - idx)[:, None] * log_g[None, :])
    chunk_decay = jnp.exp(CHUNK * log_g)
    shape = (HEADS, CHUNK, CHUNK)
    qd = jnp.broadcast_to(q_decay.T[:, :, None], shape)
    kd = jnp.broadcast_to(k_decay.T[:, :, None], shape)
    cd = jnp.broadcast_to(chunk_decay[:, None, None], shape)
    return jnp.stack([intra, qd, kd, cd])


def _rope_tables(rows):
    pos = jnp.maximum(jnp.arange(rows) - PAD_ROWS, 0).astype(F32)
    inv_freq = ROPE_BASE ** (-jnp.arange(0, HEAD_DIM, 2, dtype=F32) / HEAD_DIM)
    ang = pos[:, None] * inv_freq[None, :]
    cos, sin = jnp.cos(ang), jnp.sin(ang)
    return jnp.concatenate([cos, cos], axis=1), jnp.concatenate([-sin, sin], axis=1)


def _my_index():
    return 4 * lax.axis_index("x") + 2 * lax.axis_index("y") + lax.axis_index("c")


def _peer(k):
    x, y, c = lax.axis_index("x"), lax.axis_index("y"), lax.axis_index("c")
    px = 1 - x if k & 4 else x
    py = 1 - y if k & 2 else y
    pc = 1 - c if k & 1 else c
    return (px, py, pc), 4 * px + 2 * py + pc


def _push_sems(n_arr):
    n_rem = (N_DEV - 1) * n_arr
    return [pltpu.SemaphoreType.DMA((n_rem,)), pltpu.SemaphoreType.DMA((n_rem,)), pltpu.SemaphoreType.DMA((n_arr,))]


class _Push:
    def __init__(self, send_part, land_slot, sems, n_arr):
        self.send_part, self.land_slot, self.n_arr = send_part, land_slot, n_arr
        self.send_sems, self.recv_sems, self.loc_sems = sems

    def _remote(self, k, a, src, dst, pos):
        idx = (k - 1) * self.n_arr + a
        return pltpu.make_async_remote_copy(src_ref=src, dst_ref=dst, send_sem=self.send_sems.at[idx],
                                            recv_sem=self.recv_sems.at[idx], device_id=pos, device_id_type=MESH_ID)

    def _outgoing(self):
        me = _my_index()
        land = self.land_slot(me)
        remote = []
        for k in range(1, N_DEV):
            pos, p = _peer(k)
            src = self.send_part(p)
            remote += [self._remote(k, a, src[a], land[a], pos) for a in range(self.n_arr)]
        own = self.send_part(me)
        local = [pltpu.make_async_copy(own[a], land[a], self.loc_sems.at[a]) for a in range(self.n_arr)]
        return remote, local

    def start(self):
        remote, local = self._outgoing()
        for cp in remote + local:
            cp.start()

    def wait_recv_from(self, k):
        own = self.send_part(_my_index())
        pos, p = _peer(k)
        land = self.land_slot(p)
        for a in range(self.n_arr):
            self._remote(k, a, own[a], land[a], pos).wait_recv()

    def wait_sends(self):
        remote, local = self._outgoing()
        for cp in remote:
            cp.wait_send()
        for cp in local:
            cp.wait()

    def wait(self):
        for k in range(1, N_DEV):
            self.wait_recv_from(k)
        self.wait_sends()


class _Ride:
    def __init__(self, arrays, out_shapes, send_part, land_slot, zero_dsts=None, zero_shape=None, n_zero=0):
        self.arrays, self.out_shapes = list(arrays), list(out_shapes)
        self.send_part, self.land_slot, self.n = send_part, land_slot, len(arrays)
        self.zero_dsts, self.zero_shape, self.n_zero = zero_dsts, zero_shape, n_zero

    def specs(self):
        return [ANY] * self.n

    def scratch(self):
        extra = [pltpu.SemaphoreType.DMA((self.n_zero,)), pltpu.VMEM(self.zero_shape, BF16)] if self.n_zero else []
        return _push_sems(self.n) + extra

    def push(self, in_refs, out_refs, scratch):
        ride = self
        push = _Push(lambda p: ride.send_part(in_refs, p), lambda s: ride.land_slot(out_refs, s),
                     tuple(scratch[:3]), self.n)

        class Both:
            def _fills(self):
                if not ride.n_zero:
                    return []
                zsems, zbuf = scratch[3], scratch[4]
                return [pltpu.make_async_copy(zbuf, dst, zsems.at[z]) for z, dst in enumerate(ride.zero_dsts(out_refs))]

            def start(self):
                push.start()
                if ride.n_zero:
                    scratch[4][...] = jnp.zeros(ride.zero_shape, BF16)
                for cp in self._fills():
                    cp.start()

            def wait(self):
                push.wait()
                for cp in self._fills():
                    cp.wait()

        return Both()


def _slot_of_sender(out_refs, s):
    return [r.at[s] for r in out_refs]


def _push_call(name, arrays, out_shapes, send_part, land_slot):
    n_arr = len(arrays)

    def body(*refs):
        ins, outs, sems = refs[:n_arr], refs[n_arr:2 * n_arr], refs[2 * n_arr:]
        push = _Push(lambda p: send_part(ins, p), lambda s: land_slot(outs, s), sems, n_arr)
        push.start()
        push.wait()

    return pl.pallas_call(
        body, name=name, in_specs=[ANY] * n_arr, out_specs=[ANY] * n_arr, out_shape=out_shapes,
        scratch_shapes=_push_sems(n_arr), compiler_params=pltpu.CompilerParams(has_side_effects=True),
    )(*arrays)


LRU_ROWS = LRU_BLOCK // N_DEV
FFN_PAD_ROWS = FFN_GROUP - 2 * FFN_OUT_SHARD


def _half_rows(d):
    return pl.ds(pl.multiple_of((d % 2) * FFN_OUT_SHARD, 16), FFN_OUT_SHARD)


MIXER_SHAPES = [
    jax.ShapeDtypeStruct((N_DEV, D_MODEL // N_DEV, D_MODEL), BF16),
    jax.ShapeDtypeStruct((N_DEV, D_MODEL // N_DEV, D_MODEL), BF16),
    jax.ShapeDtypeStruct((N_DEV, D_MODEL // N_DEV, D_MODEL), BF16),
    jax.ShapeDtypeStruct((N_DEV, LRU_BLOCKS, LRU_ROWS, LRU_BLOCK), BF16),
    jax.ShapeDtypeStruct((N_DEV, LRU_BLOCKS, LRU_ROWS, LRU_BLOCK), BF16),
]


def _by_owner(t):
    return t.reshape(LRU_BLOCKS, N_DEV, LRU_ROWS, LRU_BLOCK).transpose(1, 0, 2, 3)


def _from_owners(t):
    return t.transpose(1, 0, 2, 3).reshape(LRU_BLOCKS, LRU_BLOCK, LRU_BLOCK)


def _mixer_weights_ride(shards):
    return _Ride(shards, MIXER_SHAPES, lambda ins, p: list(ins), _slot_of_sender)


def _wfo_ride(shard):
    zero_dsts = lambda outs: [outs[0].at[g, pl.ds(2 * FFN_OUT_SHARD, FFN_PAD_ROWS), :] for g in range(FFN_GROUPS)]
    return _Ride([shard], [jax.ShapeDtypeStruct((FFN_GROUPS, FFN_GROUP, D_MODEL), BF16)], lambda ins, p: list(ins),
                 lambda outs, d: [outs[0].at[d // 2, _half_rows(d), :]], zero_dsts, (FFN_PAD_ROWS, D_MODEL), FFN_GROUPS)


def _arrival_rank_to_relation(jj):
    return jnp.where(jj == 3, 4, jnp.where(jj == 4, 3, jj))


def _in_proj(h0, norm_w, win_shard, me_arr):
    rows = h0.shape[0]
    tm = _tile(rows, 640)
    n_i = rows // tm

    def body(me_ref, h_ref, nw_ref, wsh_ref, proj_ref, u_ref, wing_ref, u_all, wbuf, copy_sem,
             send_sems, recv_sems, loc_sems):
        del me_ref
        jj, i = pl.program_id(0), pl.program_id(1)
        push = _Push(lambda p: [wsh_ref], lambda s: [wing_ref.at[s]], (send_sems, recv_sems, loc_sems), 1)

        @pl.when(jnp.logical_and(jj == 0, i == 0))
        def _():
            push.start()
            own = pltpu.make_async_copy(wsh_ref, wbuf, copy_sem)
            own.start()
            own.wait()

        for k in range(1, N_DEV):
            rank = {3: 4, 4: 3}.get(k, k)

            @pl.when(jnp.logical_and(jj == rank, i == 0))
            def _(k=k):
                push.wait_recv_from(k)
                landed = pltpu.make_async_copy(wing_ref.at[_peer(k)[1]], wbuf, copy_sem)
                landed.start()
                landed.wait()

        rows_i = pl.ds(pl.multiple_of(i * tm, tm), tm)

        @pl.when(jj == 0)
        def _():
            x = h_ref[...]
            rs = lax.rsqrt(jnp.mean(x * x, axis=-1, keepdims=True) + NORM_EPS)
            u = (x * rs * nw_ref[...]).astype(BF16)
            u_all[rows_i, :] = u
            u_ref[...] = u
        proj_ref[...] = _dot(u_all[rows_i, :], wbuf[...]).astype(BF16)

        @pl.when(jnp.logical_and(jj == N_DEV - 1, i == n_i - 1))
        def _():
            push.wait_sends()

    first_pass = lambda jj, i: jnp.where(jj == 0, i, n_i - 1)
    grid_spec = pltpu.PrefetchScalarGridSpec(
        num_scalar_prefetch=1, grid=(N_DEV, n_i),
        in_specs=[pl.BlockSpec((tm, D_MODEL), lambda jj, i, me: (first_pass(jj, i), 0)),
                  pl.BlockSpec((1, D_MODEL), lambda jj, i, me: (0, 0)), ANY],
        out_specs=[pl.BlockSpec((tm, D_MODEL), lambda jj, i, me: (i, me[0] ^ _arrival_rank_to_relation(jj))),
                   pl.BlockSpec((tm, D_MODEL), lambda jj, i, me: (first_pass(jj, i), 0)), ANY],
        scratch_shapes=[pltpu.VMEM((rows, D_MODEL), BF16), pltpu.VMEM((D_MODEL, D_MODEL), BF16),
                        pltpu.SemaphoreType.DMA(())] + _push_sems(1))
    return pl.pallas_call(
        body, name="in_proj", grid_spec=grid_spec,
        out_shape=[jax.ShapeDtypeStruct((rows, N_DEV * D_MODEL), BF16),
                   jax.ShapeDtypeStruct((rows, D_MODEL), BF16),
                   jax.ShapeDtypeStruct((N_DEV, D_MODEL, D_MODEL), BF16)],
        compiler_params=pltpu.CompilerParams(dimension_semantics=("arbitrary", "arbitrary"),
                                             vmem_limit_bytes=VMEM_LIMIT, has_side_effects=True),
    )(me_arr, h0, norm_w, win_shard)


def _seg_spec(rows_per_block, seg):
    return pl.BlockSpec((rows_per_block, D_MODEL), lambda n, seg=seg: (n, seg))


def _retention_fwd(proj, cos2, sin2, dec, ride):
    rows = proj.shape[0]
    n_chunks = rows // CHUNK
    n_r = ride.n

    def body(q_ref, k_ref, v_ref, g_ref, c_ref, s_ref, dec_ref, *refs):
        o_ref, zr_ref, st_ref = refs[n_r:n_r + 3]
        state = refs[2 * n_r + 3]
        push = ride.push(refs[:n_r], refs[n_r + 3:2 * n_r + 3], refs[2 * n_r + 4:])

        @pl.when(pl.program_id(0) == 0)
        def _():
            push.start()
            state[...] = jnp.zeros_like(state)
        cos_t, sin_t = c_ref[...], s_ref[...]
        st_ref[...] = state[...]
        outs, gated, new_states = [], [], []
        for h in range(HEADS):
            sl = slice(HEAD_DIM * h, HEAD_DIM * (h + 1))
            qh = _rot(q_ref[:, sl].astype(F32), cos_t, sin_t)
            kh = _rot(k_ref[:, sl].astype(F32), cos_t, sin_t) * QK_SCALE
            qb, kb, vb = qh.astype(BF16), kh.astype(BF16), v_ref[:, sl]
            s = _dot_nt(qb, kb) * dec_ref[0, h]
            st = state[h]
            o = _dot(s.astype(BF16), vb) + _dot(qb, st.astype(BF16)) * dec_ref[1, h]
            new_states.append(st * dec_ref[3, h] + _dot_tn((kh * dec_ref[2, h]).astype(BF16), vb))
            outs.append(o.astype(BF16))
            r = lax.rsqrt(jnp.mean(o * o, axis=-1, keepdims=True) + NORM_EPS)
            g = g_ref[:, sl].astype(F32)
            gated.append((g * _sigmoid(g) * (o * r)).astype(BF16))
        o_ref[...] = jnp.concatenate(outs, axis=1)
        zr_ref[...] = jnp.concatenate(gated, axis=1)
        for h in range(HEADS):
            state[h] = new_states[h]

        @pl.when(pl.program_id(0) == n_chunks - 1)
        def _():
            push.wait()

    tab = pl.BlockSpec((CHUNK, HEAD_DIM), lambda n: (n, 0))
    return pl.pallas_call(
        body, name="retention_fwd", grid=(n_chunks,),
        in_specs=[_seg_spec(CHUNK, 0), _seg_spec(CHUNK, 1), _seg_spec(CHUNK, 2), _seg_spec(CHUNK, 3), tab, tab,
                  pl.BlockSpec((4, HEADS, CHUNK, CHUNK), lambda n: (0, 0, 0, 0))] + ride.specs(),
        out_specs=[pl.BlockSpec((CHUNK, D_MODEL), lambda n: (n, 0)),
                   pl.BlockSpec((CHUNK, D_MODEL), lambda n: (n, 0)),
                   pl.BlockSpec((None, HEADS, HEAD_DIM, HEAD_DIM), lambda n: (n, 0, 0, 0))] + ride.specs(),
        out_shape=[jax.ShapeDtypeStruct((rows, D_MODEL), BF16),
                   jax.ShapeDtypeStruct((rows, D_MODEL), BF16),
                   jax.ShapeDtypeStruct((n_chunks, HEADS, HEAD_DIM, HEAD_DIM), F32)] + ride.out_shapes,
        scratch_shapes=[pltpu.VMEM((HEADS, HEAD_DIM, HEAD_DIM), F32)] + ride.scratch(),
        compiler_params=pltpu.CompilerParams(dimension_semantics=("arbitrary",), vmem_limit_bytes=VMEM_LIMIT,
                                             has_side_effects=True),
    )(proj, proj, proj, proj, cos2, sin2, dec, *ride.arrays)


def _lru_gates(c, ba, bx, lam, wa_ref, wx_ref):
    pre_r, pre_i = [], []
    for g in range(LRU_BLOCKS):
        cg = c[:, LRU_BLOCK * g:LRU_BLOCK * (g + 1)].astype(BF16)
        pre_r.append(_dot(cg, wa_ref[g]))
        pre_i.append(_dot(cg, wx_ref[g]))
    r = _sigmoid(jnp.concatenate(pre_r, axis=1) + ba)
    ig = _sigmoid(jnp.concatenate(pre_i, axis=1) + bx)
    sp = jnp.maximum(-lam, 0.0) + jnp.log(1.0 + jnp.exp(-jnp.abs(lam)))
    log_a = -LRU_C * r * sp
    a = jnp.exp(log_a)
    one_minus_a2 = -jnp.tanh(log_a) * (a * a + 1.0)
    inv_mult = lax.rsqrt(jnp.maximum(one_minus_a2, 1e-30))
    return r, ig, a, one_minus_a2 * inv_mult, inv_mult, sp


def _conv_taps(xbuf, tm, cw_ref, cb_ref):
    c = cb_ref[...] + cw_ref[3:4, :] * xbuf[8:8 + tm, :]
    for back in (1, 2, 3):
        c = c + cw_ref[3 - back:4 - back, :] * xbuf[8 - back:8 - back + tm, :]
    return c


def _lru_fwd(proj, conv_w, conv_b, ba, bx, lam, wa_g, wx_g, ride):
    rows = proj.shape[0]
    tm = _tile(rows, 320)
    n_t = rows // tm
    n_r = ride.n

    def body(x_ref, gt_ref, cw_ref, cb_ref, ba_ref, bx_ref, lam_ref, wa_ref, wx_ref, *refs):
        hs_ref, zl_ref = refs[n_r:n_r + 2]
        xbuf, abuf, ubuf, hcar = refs[2 * n_r + 2:2 * n_r + 6]
        push = ride.push(refs[:n_r], refs[n_r + 2:2 * n_r + 2], refs[2 * n_r + 6:])
        i = pl.program_id(0)

        @pl.when(i == 0)
        def _():
            push.start()
            xbuf[0:8, :] = jnp.zeros((8, D_MODEL), F32)
            hcar[...] = jnp.zeros_like(hcar)

        xbuf[8:8 + tm, :] = x_ref[...].astype(F32)
        c = _conv_taps(xbuf, tm, cw_ref, cb_ref)
        xbuf[0:8, :] = xbuf[tm:tm + 8, :]
        r, ig, a, mult, _, _ = _lru_gates(c, ba_ref[...], bx_ref[...], lam_ref[...], wa_ref, wx_ref)
        row = i * tm + lax.broadcasted_iota(jnp.int32, (tm, 1), 0)
        abuf[...] = a
        ubuf[...] = jnp.where(row >= PAD_ROWS, mult * (ig * c), 0.0)

        sub = lax.broadcasted_iota(jnp.int32, (8, D_MODEL), 0)

        def block(b, carry):
            off = pl.multiple_of(b * 8, 8)
            av, uv = abuf[pl.ds(off, 8), :], ubuf[pl.ds(off, 8), :]
            for s in (1, 2, 4):
                us = jnp.where(sub >= s, pltpu.roll(uv, s, 0), 0.0)
                as_ = jnp.where(sub >= s, pltpu.roll(av, s, 0), 1.0)
                uv = uv + av * us
                av = av * as_
            hv = uv + av * carry
            ubuf[pl.ds(off, 8), :] = hv
            return hv[7:8, :]

        hcar[...] = lax.fori_loop(0, tm // 8, block, hcar[...])
        gl, _ = _gelu_parts(gt_ref[...].astype(F32))
        hs = ubuf[...]
        hs_ref[...] = hs.astype(BF16)
        zl_ref[...] = (gl * hs).astype(BF16)

        @pl.when(i == n_t - 1)
        def _():
            push.wait()

    vec = pl.BlockSpec((1, D_MODEL), lambda i: (0, 0))
    mat = pl.BlockSpec((LRU_BLOCKS, LRU_BLOCK, LRU_BLOCK), lambda i: (0, 0, 0))
    row = pl.BlockSpec((tm, D_MODEL), lambda i: (i, 0))
    return pl.pallas_call(
        body, name="lru_fwd", grid=(n_t,),
        in_specs=[_seg_spec(tm, 4), _seg_spec(tm, 5), pl.BlockSpec((4, D_MODEL), lambda i: (0, 0)),
                  vec, vec, vec, vec, mat, mat] + ride.specs(),
        out_specs=[row, row] + ride.specs(),
        out_shape=[jax.ShapeDtypeStruct((rows, D_MODEL), BF16)] * 2 + ride.out_shapes,
        scratch_shapes=[pltpu.VMEM((tm + 8, D_MODEL), F32), pltpu.VMEM((tm, D_MODEL), F32),
                        pltpu.VMEM((tm, D_MODEL), F32), pltpu.VMEM((1, D_MODEL), F32)] + ride.scratch(),
        compiler_params=pltpu.CompilerParams(dimension_semantics=("arbitrary",), vmem_limit_bytes=VMEM_LIMIT,
                                             has_side_effects=True),
    )(proj, proj, conv_w, conv_b, ba, bx, lam, wa_g, wx_g, *ride.arrays)


def _mix_fwd(zr, zl, proj, h0, wbr, wbl, wout, ride):
    rows = h0.shape[0]
    tm = _tile(rows, 640)
    n_t = rows // tm
    n_r = ride.n

    def body(zr_ref, zl_ref, ga_ref, gb_ref, h0_ref, wbr_ref, wbl_ref, wo_ref, *refs):
        h1_ref, yr_ref, yl_ref, mx_ref = refs[n_r:n_r + 4]
        push = ride.push(refs[:n_r], refs[n_r + 4:2 * n_r + 4], refs[2 * n_r + 4:])

        @pl.when(pl.program_id(0) == 0)
        def _():
            push.start()

        yr = _dot(zr_ref[...], wbr_ref[...])
        yl = _dot(zl_ref[...], wbl_ref[...])
        mixed = (_sigmoid(ga_ref[...].astype(F32)) * yr + _sigmoid(gb_ref[...].astype(F32)) * yl).astype(BF16)
        yr_ref[...] = yr.astype(BF16)
        yl_ref[...] = yl.astype(BF16)
        mx_ref[...] = mixed
        h1_ref[...] = h0_ref[...] + _dot(mixed, wo_ref[...])

        @pl.when(pl.program_id(0) == n_t - 1)
        def _():
            push.wait()

    row = pl.BlockSpec((tm, D_MODEL), lambda i: (i, 0))
    wsp = pl.BlockSpec((D_MODEL, D_MODEL), lambda i: (0, 0))
    return pl.pallas_call(
        body, name="mix_fwd", grid=(n_t,),
        in_specs=[row, row, _seg_spec(tm, 6), _seg_spec(tm, 7), row, wsp, wsp, wsp] + ride.specs(),
        out_specs=[row, row, row, row] + ride.specs(),
        out_shape=[jax.ShapeDtypeStruct((rows, D_MODEL), F32)] + [jax.ShapeDtypeStruct((rows, D_MODEL), BF16)] * 3
        + ride.out_shapes,
        scratch_shapes=ride.scratch(),
        compiler_params=pltpu.CompilerParams(dimension_semantics=("arbitrary",), vmem_limit_bytes=VMEM_LIMIT,
                                             has_side_effects=True),
    )(zr, zl, proj, proj, h0, wbr, wbl, wout, *ride.arrays)


def _ffn_fwd_loss(h1, norm_w, wfi_g, wfo_g, final_w, target):
    rows = h1.shape[0]
    tm = _tile(rows, 640)
    last = FFN_GROUPS - 1

    def body(h1_ref, nw_ref, wg_ref, wu_ref, wo_ref, fw_ref, t_ref,
             u2_ref, g_ref, up_ref, act_ref, dh2_ref, red_ref, u2_s, acc):
        i, d = pl.program_id(0), pl.program_id(1)

        @pl.when(jnp.logical_and(i == 0, d == 0))
        def _():
            red_ref[...] = jnp.zeros_like(red_ref)

        @pl.when(d == 0)
        def _():
            x = h1_ref[...]
            rs = lax.rsqrt(jnp.mean(x * x, axis=-1, keepdims=True) + NORM_EPS)
            u2 = (x * rs * nw_ref[...]).astype(BF16)
            u2_s[...] = u2
            u2_ref[...] = u2
            acc[...] = jnp.zeros_like(acc)

        g = _dot(u2_s[...], wg_ref[...])
        up = _dot(u2_s[...], wu_ref[...])
        act = (g * _sigmoid(g) * up).astype(BF16)
        g_ref[...] = g.astype(BF16)
        up_ref[...] = up.astype(BF16)
        act_ref[...] = act
        acc[...] += _dot(act, wo_ref[...])

        @pl.when(d == last)
        def _():
            h2 = h1_ref[...] + acc[...]
            rs = lax.rsqrt(jnp.mean(h2 * h2, axis=-1, keepdims=True) + NORM_EPS)
            nh = h2 * rs
            fw = fw_ref[...]
            row = i * tm + lax.broadcasted_iota(jnp.int32, (tm, 1), 0)
            diff = jnp.where(row >= CHUNK, nh * fw - t_ref[...], 0.0)
            dy = diff * (1.0 / D_MODEL)
            red_ref[0:1, :] += jnp.sum(diff * diff, axis=0, keepdims=True)
            red_ref[1:2, :] += jnp.sum(dy * nh, axis=0, keepdims=True)
            dn = dy * fw
            dh2_ref[...] = rs * (dn - nh * jnp.mean(dn * nh, axis=-1, keepdims=True))

    row = pl.BlockSpec((tm, D_MODEL), lambda i, d: (i, 0))
    vec = pl.BlockSpec((1, D_MODEL), lambda i, d: (0, 0))
    hid = pl.BlockSpec((tm, FFN_GROUP), lambda i, d: (i, d))
    hid_shape = jax.ShapeDtypeStruct((rows, FFN_GROUPS * FFN_GROUP), BF16)
    return pl.pallas_call(
        body, name="ffn_fwd_loss", grid=(rows // tm, FFN_GROUPS),
        in_specs=[row, vec,
                  pl.BlockSpec((None, D_MODEL, FFN_GROUP), lambda i, d: (d, 0, 0)),
                  pl.BlockSpec((None, D_MODEL, FFN_GROUP), lambda i, d: (d + FFN_GROUPS, 0, 0)),
                  pl.BlockSpec((None, FFN_GROUP, D_MODEL), lambda i, d: (d, 0, 0)),
                  vec, row],
        out_specs=[row, hid, hid, hid, row, pl.BlockSpec((8, D_MODEL), lambda i, d: (0, 0))],
        out_shape=[jax.ShapeDtypeStruct((rows, D_MODEL), BF16), hid_shape, hid_shape, hid_shape,
                   jax.ShapeDtypeStruct((rows, D_MODEL), F32), jax.ShapeDtypeStruct((8, D_MODEL), F32)],
        scratch_shapes=[pltpu.VMEM((tm, D_MODEL), BF16), pltpu.VMEM((tm, D_MODEL), F32)],
        compiler_params=_cparams(("arbitrary", "arbitrary")),
    )(h1, norm_w, wfi_g, wfi_g, wfo_g, final_w, target)


def _wgrad(a, b, ka, tn, out_dtype, b_halves=False):
    rows = a.shape[0]
    na = a.shape[1] // ka
    tm = _tile(rows, 1664)
    nm = rows // tm
    if b_halves:
        per_half = b.shape[2] // tn
        nb = 2 * per_half
        b_spec = pl.BlockSpec((None, tm, tn), lambda p, q, m: (q // per_half, m, q % per_half))
    else:
        nb = b.shape[1] // tn
        b_spec = pl.BlockSpec((tm, tn), lambda p, q, m: (m, q))

    def body(a_ref, b_ref, o_ref, acc):
        m = pl.program_id(2)

        @pl.when(m == 0)
        def _():
            acc[...] = jnp.zeros_like(acc)

        acc[...] += _dot_tn(a_ref[...].astype(BF16), b_ref[...].astype(BF16))

        @pl.when(m == nm - 1)
        def _():
            o_ref[...] = acc[...].astype(out_dtype)

    return pl.pallas_call(
        body, name="wgrad", grid=(na, nb, nm),
        in_specs=[pl.BlockSpec((tm, ka), lambda p, q, m: (m, p)), b_spec],
        out_specs=pl.BlockSpec((None, None, ka, tn), lambda p, q, m: (p, q, 0, 0)),
        out_shape=jax.ShapeDtypeStruct((na, nb, ka, tn), out_dtype),
        scratch_shapes=[pltpu.VMEM((ka, tn), F32)],
        compiler_params=_cparams(("parallel", "parallel", "arbitrary")),
    )(a, b)


def _ffn_bwd(dh2, g, up, h1, norm_w, wfi_g, wfo_g):
    rows = h1.shape[0]
    tm = _tile(rows, 640)
    last = FFN_GROUPS - 1

    def body(dh2_ref, g_ref, up_ref, h1_ref, nw_ref, wg_ref, wu_ref, wo_ref, dgu_ref, dh1_ref, dw_ref,
             dh2_s, acc):
        i, d = pl.program_id(0), pl.program_id(1)

        @pl.when(jnp.logical_and(i == 0, d == 0))
        def _():
            dw_ref[...] = jnp.zeros_like(dw_ref)

        @pl.when(d == 0)
        def _():
            dh2_s[...] = dh2_ref[...].astype(BF16)
            acc[...] = jnp.zeros_like(acc)

        dact = _dot_nt(dh2_s[...], wo_ref[...])
        gv, uv = g_ref[...].astype(F32), up_ref[...].astype(F32)
        sg = _sigmoid(gv)
        dg = (dact * uv * (sg * (1.0 + gv * (1.0 - sg)))).astype(BF16)
        dup = (dact * (gv * sg)).astype(BF16)
        dgu_ref[0] = dg
        dgu_ref[1] = dup
        acc[...] += _dot_nt(dg, wg_ref[...]) + _dot_nt(dup, wu_ref[...])

        @pl.when(d == last)
        def _():
            dx, dw = _rms_bwd(h1_ref[...], nw_ref[...], acc[...])
            dw_ref[0:1, :] += dw
            dh1_ref[...] = dh2_ref[...] + dx

    row = pl.BlockSpec((tm, D_MODEL), lambda i, d: (i, 0))
    vec = pl.BlockSpec((1, D_MODEL), lambda i, d: (0, 0))
    hid = pl.BlockSpec((tm, FFN_GROUP), lambda i, d: (i, d))
    return pl.pallas_call(
        body, name="ffn_bwd", grid=(rows // tm, FFN_GROUPS),
        in_specs=[row, hid, hid, row, vec,
                  pl.BlockSpec((None, D_MODEL, FFN_GROUP), lambda i, d: (d, 0, 0)),
                  pl.BlockSpec((None, D_MODEL, FFN_GROUP), lambda i, d: (d + FFN_GROUPS, 0, 0)),
                  pl.BlockSpec((None, FFN_GROUP, D_MODEL), lambda i, d: (d, 0, 0))],
        out_specs=[pl.BlockSpec((2, tm, FFN_GROUP), lambda i, d: (0, i, d)), row,
                   pl.BlockSpec((8, D_MODEL), lambda i, d: (0, 0))],
        out_shape=[jax.ShapeDtypeStruct((2, rows, FFN_GROUPS * FFN_GROUP), BF16),
                   jax.ShapeDtypeStruct((rows, D_MODEL), F32), jax.ShapeDtypeStruct((8, D_MODEL), F32)],
        scratch_shapes=[pltpu.VMEM((tm, D_MODEL), BF16), pltpu.VMEM((tm, D_MODEL), F32)],
        compiler_params=_cparams(("arbitrary", "arbitrary")),
    )(dh2, g, up, h1, norm_w, wfi_g, wfi_g, wfo_g)


def _mix_bwd(dh1, yr, yl, proj, wbr, wbl, wout):
    rows = dh1.shape[0]
    tm = _tile(rows, 640)

    def body(dh1_ref, yr_ref, yl_ref, ga_ref, gb_ref, wbr_ref, wbl_ref, wo_ref,
             dyr_ref, dyl_ref, dseg_ref, dzr_ref, dzl_ref):
        dmix = _dot_nt(dh1_ref[...].astype(BF16), wo_ref[...])
        sa, sb = _sigmoid(ga_ref[...].astype(F32)), _sigmoid(gb_ref[...].astype(F32))
        dyr = (dmix * sa).astype(BF16)
        dyl = (dmix * sb).astype(BF16)
        dyr_ref[...] = dyr
        dyl_ref[...] = dyl
        dseg_ref[:, 0:D_MODEL] = (dmix * yr_ref[...].astype(F32) * (sa * (1.0 - sa))).astype(BF16)
        dseg_ref[:, D_MODEL:2 * D_MODEL] = (dmix * yl_ref[...].astype(F32) * (sb * (1.0 - sb))).astype(BF16)
        dzr_ref[...] = _dot_nt(dyr, wbr_ref[...]).astype(BF16)
        dzl_ref[...] = _dot_nt(dyl, wbl_ref[...]).astype(BF16)

    row = pl.BlockSpec((tm, D_MODEL), lambda i: (i, 0))
    wsp = pl.BlockSpec((D_MODEL, D_MODEL), lambda i: (0, 0))
    bshape = jax.ShapeDtypeStruct((rows, D_MODEL), BF16)
    return pl.pallas_call(
        body, name="mix_bwd", grid=(rows // tm,),
        in_specs=[row, row, row, _seg_spec(tm, 6), _seg_spec(tm, 7), wsp, wsp, wsp],
        out_specs=[row, row, pl.BlockSpec((tm, 2 * D_MODEL), lambda i: (i, 3)), row, row],
        out_shape=[bshape, bshape, jax.ShapeDtypeStruct((rows, N_DEV * D_MODEL), BF16), bshape, bshape],
        compiler_params=_cparams(("parallel",)),
    )(dh1, yr, yl, proj, proj, wbr, wbl, wout)


S1_SHAPES = [
    jax.ShapeDtypeStruct((N_DEV, D_MODEL, FFN_GROUP), BF16),
    jax.ShapeDtypeStruct((N_DEV, FFN_OUT_SHARD, D_MODEL), BF16),
]


def _s1_parts(ins, p):
    return [ins[0].at[p], ins[1].at[p // 2, _half_rows(p), :]]


def _lru_bwd(dzl, hs, proj, dproj, conv_w, conv_b, ba, bx, lam, wa_g, wx_g, s1_grads):
    rows = dzl.shape[0]
    tm = _tile(rows, 320)
    nt = rows // tm
    t8 = tm // 8
    n_s1 = len(s1_grads)

    def body(dzl_ref, hs_ref, hsp_ref, x_ref, xp_ref, gt_ref, cw_ref, cb_ref, ba_ref, bx_ref, lam_ref,
             wa_ref, wx_ref, dproj_in, *refs):
        del dproj_in
        s1_refs = refs[:n_s1]
        dseg_ref, dwa_ref, dwx_ref, sm_ref = refs[n_s1:n_s1 + 4]
        land_refs = refs[n_s1 + 4:2 * n_s1 + 4]
        xbuf, abuf, bbuf, dbuf, dcbuf, anext, dhcar, send_sems, recv_sems, loc_sems = refs[2 * n_s1 + 4:]
        step = pl.program_id(0)
        i = nt - 1 - step
        push = _Push(lambda p: _s1_parts(s1_refs, p), lambda s: [r.at[s] for r in land_refs],
                     (send_sems, recv_sems, loc_sems), n_s1)

        @pl.when(step == 0)
        def _():
            push.start()
            dwa_ref[...] = jnp.zeros_like(dwa_ref)
            dwx_ref[...] = jnp.zeros_like(dwx_ref)
            sm_ref[...] = jnp.zeros_like(sm_ref)
            anext[...] = jnp.zeros_like(anext)
            dhcar[...] = jnp.zeros_like(dhcar)
            dcbuf[tm:tm + 8, :] = jnp.zeros((8, D_MODEL), F32)

        first = i == 0
        x_prev = jnp.where(first, 0.0, xp_ref[8:16, :].astype(F32))
        x_v = x_ref[...].astype(F32)
        xbuf[0:8, :] = x_prev
        xbuf[8:8 + tm, :] = x_v
        c = _conv_taps(xbuf, tm, cw_ref, cb_ref)
        lam_v = lam_ref[...]
        r, ig, a, mult, inv_mult, sp = _lru_gates(c, ba_ref[...], bx_ref[...], lam_v, wa_ref, wx_ref)
        hs_v = hs_ref[...].astype(F32)
        gl, dgl = _gelu_parts(gt_ref[...].astype(F32))
        dzl_v = dzl_ref[...].astype(F32)
        dseg_ref[:, D_MODEL:2 * D_MODEL] = (dzl_v * hs_v * dgl).astype(BF16)
        dbuf[...] = dzl_v * gl
        abuf[0:tm, :] = a
        abuf[tm:tm + 8, :] = jnp.broadcast_to(anext[...], (8, D_MODEL))
        bbuf[...] = abuf[1:tm + 1, :]

        sub = lax.broadcasted_iota(jnp.int32, (8, D_MODEL), 0)

        def block(k, carry):
            off = pl.multiple_of((t8 - 1 - k) * 8, 8)
            av = bbuf[pl.ds(off, 8), :]
            uv = dbuf[pl.ds(off, 8), :]
            for s in (1, 2, 4):
                us = jnp.where(sub < 8 - s, pltpu.roll(uv, 8 - s, 0), 0.0)
                as_ = jnp.where(sub < 8 - s, pltpu.roll(av, 8 - s, 0), 1.0)
                uv = uv + av * us
                av = av * as_
            hv = uv + av * carry
            dbuf[pl.ds(off, 8), :] = hv
            return hv[0:1, :]

        dhcar[...] = lax.fori_loop(0, t8, block, dhcar[...])
        anext[...] = abuf[0:1, :]
        dh = dbuf[...]

        xbuf[0:8, :] = jnp.where(first, 0.0, hsp_ref[8:16, :].astype(F32))
        xbuf[8:8 + tm, :] = hs_v
        hprev = xbuf[7:7 + tm, :]
        row = i * tm + lax.broadcasted_iota(jnp.int32, (tm, 1), 0)
        duu = jnp.where(row >= PAD_ROWS, dh, 0.0)
        da = dh * hprev
        dmult = duu * ig * c
        di = duu * mult * c
        dc = duu * mult * ig
        dlog_a = da * a - dmult * (a * a) * inv_mult
        dr = dlog_a * (-LRU_C * sp)
        dsp = jnp.sum(dlog_a * (-LRU_C * r), axis=0, keepdims=True)
        dpr = dr * r * (1.0 - r)
        dpi = di * ig * (1.0 - ig)
        dpr_b, dpi_b = dpr.astype(BF16), dpi.astype(BF16)
        dcs = []
        for g in range(LRU_BLOCKS):
            sl = slice(LRU_BLOCK * g, LRU_BLOCK * (g + 1))
            cg = c[:, sl].astype(BF16)
            dwa_ref[g] += _dot_tn(cg, dpr_b[:, sl])
            dwx_ref[g] += _dot_tn(cg, dpi_b[:, sl])
            dcs.append(_dot_nt(dpr_b[:, sl], wa_ref[g]) + _dot_nt(dpi_b[:, sl], wx_ref[g]))
        dc = dc + jnp.concatenate(dcs, axis=1)

        dcbuf[0:tm, :] = dc
        xbuf[8:8 + tm, :] = x_v
        xbuf[0:8, :] = x_prev
        dlin = cw_ref[3:4, :] * dc
        sm_ref[3:4, :] += jnp.sum(dc * xbuf[8:8 + tm, :], axis=0, keepdims=True)
        for back in (1, 2, 3):
            dlin = dlin + cw_ref[3 - back:4 - back, :] * dcbuf[back:back + tm, :]
            sm_ref[3 - back:4 - back, :] += jnp.sum(dc * xbuf[8 - back:8 - back + tm, :], axis=0, keepdims=True)
        dseg_ref[:, 0:D_MODEL] = dlin.astype(BF16)
        dcbuf[tm:tm + 8, :] = dcbuf[0:8, :]
        sm_ref[4:5, :] += jnp.sum(dc, axis=0, keepdims=True)
        sm_ref[5:6, :] += jnp.sum(dpr, axis=0, keepdims=True)
        sm_ref[6:7, :] += jnp.sum(dpi, axis=0, keepdims=True)
        sm_ref[7:8, :] += dsp * (-_sigmoid(-lam_v))

        @pl.when(step == nt - 1)
        def _():
            push.wait()

    rowb = pl.BlockSpec((tm, D_MODEL), lambda s: (nt - 1 - s, 0))
    t16 = tm // 16
    prev8 = pl.BlockSpec((16, D_MODEL), lambda s: (jnp.maximum((nt - 1 - s) * t16 - 1, 0), 0))
    seg = lambda k: pl.BlockSpec((tm, D_MODEL), lambda s, k=k: (nt - 1 - s, k))
    prev8_seg4 = pl.BlockSpec((16, D_MODEL), lambda s: (jnp.maximum((nt - 1 - s) * t16 - 1, 0), 4))
    vec = pl.BlockSpec((1, D_MODEL), lambda s: (0, 0))
    mat = pl.BlockSpec((LRU_BLOCKS, LRU_BLOCK, LRU_BLOCK), lambda s: (0, 0, 0))
    mshape = jax.ShapeDtypeStruct((LRU_BLOCKS, LRU_BLOCK, LRU_BLOCK), F32)
    n_in = 13
    return pl.pallas_call(
        body, name="lru_bwd", grid=(nt,),
        in_specs=[rowb, rowb, prev8, seg(4), prev8_seg4, seg(5), pl.BlockSpec((4, D_MODEL), lambda s: (0, 0)),
                  vec, vec, vec, vec, mat, mat, ANY] + [ANY] * n_s1,
        out_specs=[pl.BlockSpec((tm, 2 * D_MODEL), lambda s: (nt - 1 - s, 2)), mat, mat,
                   pl.BlockSpec((8, D_MODEL), lambda s: (0, 0))] + [ANY] * n_s1,
        out_shape=[jax.ShapeDtypeStruct(dproj.shape, dproj.dtype), mshape, mshape,
                   jax.ShapeDtypeStruct((8, D_MODEL), F32)] + S1_SHAPES,
        input_output_aliases={n_in: 0},
        scratch_shapes=[pltpu.VMEM((tm + 8, D_MODEL), F32), pltpu.VMEM((tm + 8, D_MODEL), F32),
                        pltpu.VMEM((tm, D_MODEL), F32), pltpu.VMEM((tm, D_MODEL), F32),
                        pltpu.VMEM((tm + 8, D_MODEL), F32),
                        pltpu.VMEM((1, D_MODEL), F32), pltpu.VMEM((1, D_MODEL), F32)] + _push_sems(n_s1),
        compiler_params=pltpu.CompilerParams(dimension_semantics=("arbitrary",), vmem_limit_bytes=VMEM_LIMIT,
                                             has_side_effects=True),
    )(dzl, hs, hs, proj, proj, proj, conv_w, conv_b, ba, bx, lam, wa_g, wx_g, dproj, *s1_grads)


def _retention_bwd(dzr, o, proj, states, cos2, sin2, dec, dproj, ride):
    rows = dzr.shape[0]
    n_chunks = rows // CHUNK
    n_r = ride.n

    def body(dzr_ref, o_ref, q_ref, k_ref, v_ref, g_ref, st_ref, c_ref, s_ref, dec_ref, dproj_in, *refs):
        del dproj_in
        dseg_ref = refs[n_r]
        dstate = refs[2 * n_r + 1]
        push = ride.push(refs[:n_r], refs[n_r + 1:2 * n_r + 1], refs[2 * n_r + 2:])

        @pl.when(pl.program_id(0) == 0)
        def _():
            push.start()
            dstate[...] = jnp.zeros_like(dstate)
        cos_t, sin_t = c_ref[...], s_ref[...]
        for h in range(HEADS):
            sl = slice(HEAD_DIM * h, HEAD_DIM * (h + 1))
            o = o_ref[:, sl].astype(F32)
            g = g_ref[:, sl].astype(F32)
            dzr_v = dzr_ref[:, sl].astype(F32)
            sg = _sigmoid(g)
            r = lax.rsqrt(jnp.mean(o * o, axis=-1, keepdims=True) + NORM_EPS)
            on = o * r
            dseg_ref[:, 3 * D_MODEL + HEAD_DIM * h:3 * D_MODEL + HEAD_DIM * (h + 1)] = (
                dzr_v * on * (sg * (1.0 + g * (1.0 - sg)))).astype(BF16)
            don = dzr_v * (g * sg)
            do = r * (don - on * jnp.mean(don * on, axis=-1, keepdims=True))
            dob = do.astype(BF16)

            qh = _rot(q_ref[:, sl].astype(F32), cos_t, sin_t)
            kh = _rot(k_ref[:, sl].astype(F32), cos_t, sin_t) * QK_SCALE
            qb, kb, vb = qh.astype(BF16), kh.astype(BF16), v_ref[:, sl]
            intra, qd, kd, cd = dec_ref[0, h], dec_ref[1, h], dec_ref[2, h], dec_ref[3, h]
            s = (_dot_nt(qb, kb) * intra).astype(BF16)
            ds = (_dot_nt(dob, vb) * intra).astype(BF16)
            st_b = st_ref[h].astype(BF16)
            dst = dstate[h]
            dst_b = dst.astype(BF16)
            dv = _dot_tn(s, dob) + _dot((kh * kd).astype(BF16), dst_b)
            dq = _dot(ds, kb) + _dot_nt(dob, st_b) * qd
            dk = _dot_tn(ds, qb) + _dot_nt(vb, dst_b) * kd
            dstate[h] = dst * cd + _dot_tn((qh * qd).astype(BF16), dob)
            dseg_ref[:, 2 * D_MODEL + HEAD_DIM * h:2 * D_MODEL + HEAD_DIM * (h + 1)] = dv.astype(BF16)
            dseg_ref[:, sl] = _rot_t(dq, cos_t, sin_t).astype(BF16)
            dseg_ref[:, D_MODEL + HEAD_DIM * h:D_MODEL + HEAD_DIM * (h + 1)] = (
                _rot_t(dk, cos_t, sin_t) * QK_SCALE).astype(BF16)

        @pl.when(pl.program_id(0) == n_chunks - 1)
        def _():
            push.wait()

    rev = lambda s: n_chunks - 1 - s
    rowb = pl.BlockSpec((CHUNK, D_MODEL), lambda s: (rev(s), 0))
    seg = lambda k: pl.BlockSpec((CHUNK, D_MODEL), lambda s, k=k: (rev(s), k))
    tab = pl.BlockSpec((CHUNK, HEAD_DIM), lambda s: (rev(s), 0))
    return pl.pallas_call(
        body, name="retention_bwd", grid=(n_chunks,),
        in_specs=[rowb, rowb, seg(0), seg(1), seg(2), seg(3),
                  pl.BlockSpec((None, HEADS, HEAD_DIM, HEAD_DIM), lambda s: (rev(s), 0, 0, 0)), tab, tab,
                  pl.BlockSpec((4, HEADS, CHUNK, CHUNK), lambda s: (0, 0, 0, 0)), ANY] + ride.specs(),
        out_specs=[pl.BlockSpec((CHUNK, 4 * D_MODEL), lambda s: (rev(s), 0))] + ride.specs(),
        out_shape=[jax.ShapeDtypeStruct(dproj.shape, dproj.dtype)] + ride.out_shapes,
        input_output_aliases={10: 0},
        scratch_shapes=[pltpu.VMEM((HEADS, HEAD_DIM, HEAD_DIM), F32)] + ride.scratch(),
        compiler_params=pltpu.CompilerParams(dimension_semantics=("arbitrary",), vmem_limit_bytes=VMEM_LIMIT,
                                             has_side_effects=True),
    )(dzr, o, proj, proj, proj, proj, states, cos2, sin2, dec, dproj, *ride.arrays)


S2_SHAPES = [
    jax.ShapeDtypeStruct((N_DEV, D_MODEL, D_MODEL), BF16),
    jax.ShapeDtypeStruct((N_DEV, LRU_BLOCKS, LRU_ROWS, LRU_BLOCK), F32),
    jax.ShapeDtypeStruct((N_DEV, LRU_BLOCKS, LRU_ROWS, LRU_BLOCK), F32),
]


def _s2_parts(ins, p):
    return [r.at[p] for r in ins]


def _in_proj_bwd(dproj, win_g, h0, norm_w, dh1, s2_grads):
    rows = h0.shape[0]
    tm = _tile(rows, 640)
    n_i = rows // tm
    n_s2 = len(s2_grads)

    def body(dseg_ref, w_ref, h0_ref, nw_ref, dh1_ref, *refs):
        s2_refs = refs[:n_s2]
        dh0_ref, dw_ref = refs[n_s2:n_s2 + 2]
        land_refs = refs[n_s2 + 2:2 * n_s2 + 2]
        acc, send_sems, recv_sems, loc_sems = refs[2 * n_s2 + 2:]
        i, j = pl.program_id(0), pl.program_id(1)
        push = _Push(lambda p: _s2_parts(s2_refs, p), lambda s: [r.at[s] for r in land_refs],
                     (send_sems, recv_sems, loc_sems), n_s2)

        @pl.when(jnp.logical_and(i == 0, j == 0))
        def _():
            push.start()
            dw_ref[...] = jnp.zeros_like(dw_ref)

        @pl.when(j == 0)
        def _():
            acc[...] = jnp.zeros_like(acc)

        acc[...] += _dot_nt(dseg_ref[...], w_ref[...])

        @pl.when(j == N_DEV - 1)
        def _():
            dx, dw = _rms_bwd(h0_ref[...], nw_ref[...], acc[...])
            dw_ref[0:1, :] += dw
            dh0_ref[...] = dh1_ref[...] + dx

        @pl.when(jnp.logical_and(i == n_i - 1, j == N_DEV - 1))
        def _():
            push.wait()

    row = pl.BlockSpec((tm, D_MODEL), lambda i, j: (i, 0))
    vec = pl.BlockSpec((1, D_MODEL), lambda i, j: (0, 0))
    return pl.pallas_call(
        body, name="in_proj_bwd", grid=(n_i, N_DEV),
        in_specs=[pl.BlockSpec((tm, D_MODEL), lambda i, j: (i, j)),
                  pl.BlockSpec((None, D_MODEL, D_MODEL), lambda i, j: (j, 0, 0)), row, vec, row] + [ANY] * n_s2,
        out_specs=[row, pl.BlockSpec((8, D_MODEL), lambda i, j: (0, 0))] + [ANY] * n_s2,
        out_shape=[jax.ShapeDtypeStruct((rows, D_MODEL), F32), jax.ShapeDtypeStruct((8, D_MODEL), F32)] + S2_SHAPES,
        scratch_shapes=[pltpu.VMEM((tm, D_MODEL), F32)] + _push_sems(n_s2),
        compiler_params=pltpu.CompilerParams(dimension_semantics=("arbitrary", "arbitrary"),
                                             vmem_limit_bytes=VMEM_LIMIT, has_side_effects=True),
    )(dproj, win_g, h0, norm_w, dh1, *s2_grads)


def _adamw(g_slots, w, m, v):
    slots, rows, cols = g_slots.shape
    tr = rows
    for cand in (256, 128, 64, 32, 16, 8):
        if rows % cand == 0 and rows > cand:
            tr = cand
            break

    def body(g_ref, w_ref, m_ref, v_ref, go_ref, d_ref, mo_ref, vo_ref):
        g = g_ref[0].astype(F32)
        for s in range(1, slots):
            g = g + g_ref[s].astype(F32)
        m2 = ADAM_B1 * m_ref[...] + (1.0 - ADAM_B1) * g
        v2 = ADAM_B2 * v_ref[...] + (1.0 - ADAM_B2) * (g * g)
        m_hat = m2 / (1.0 - ADAM_B1 ** ADAM_STEP)
        v_hat = v2 / (1.0 - ADAM_B2 ** ADAM_STEP)
        go_ref[...] = g
        d_ref[...] = -ADAM_LR * (m_hat / (jnp.sqrt(v_hat) + ADAM_EPS) + ADAM_WD * w_ref[...])
        mo_ref[...] = m2
        vo_ref[...] = v2

    blk = pl.BlockSpec((tr, cols), lambda i: (i, 0))
    shape = jax.ShapeDtypeStruct((rows, cols), F32)
    return pl.pallas_call(
        body, name="adamw", grid=(rows // tr,),
        in_specs=[pl.BlockSpec((slots, tr, cols), lambda i: (0, i, 0)), blk, blk, blk],
        out_specs=[blk] * 4, out_shape=[shape] * 4,
        compiler_params=_cparams(("parallel",)),
    )(g_slots, w, m, v)


def _sum_slots(packs):
    slots, rows, cols = packs.shape

    def body(p_ref, o_ref):
        acc = p_ref[0]
        for s in range(1, slots):
            acc = acc + p_ref[s]
        o_ref[...] = acc

    return pl.pallas_call(
        body, name="sum_slots", out_shape=jax.ShapeDtypeStruct((rows, cols), F32),
        compiler_params=pltpu.CompilerParams(vmem_limit_bytes=VMEM_LIMIT),
    )(packs)


def _gather_small(small):
    shapes = [jax.ShapeDtypeStruct((N_DEV,) + small.shape, F32)]
    return _push_call("gather_small", [small], shapes,
                      lambda ins, p: list(ins), lambda outs, s: [r.at[s] for r in outs])[0]


def _share_pack(pack):
    shapes = [jax.ShapeDtypeStruct((N_DEV,) + pack.shape, F32)]
    return _push_call("share_pack", [pack], shapes,
                      lambda ins, p: list(ins), lambda outs, s: [r.at[s] for r in outs])[0]


PACK_MIX_NORM, PACK_CONV_W, PACK_CONV_B, PACK_BA, PACK_BX, PACK_LAM = 0, 8, 12, 13, 14, 15
PACK_FFN_NORM, PACK_SQ_ERR, PACK_FINAL_NORM, PACK_META = 16, 24, 25, 32


def kernel(x, meta_tokens, mix_norm_w, w_in, conv_w, conv_b, lru_wa, lru_ba, lru_wx, lru_bx, lru_lambda, w_branch_ret, w_branch_lru, w_out, ffn_norm_w, w_ffn_in, w_ffn_out, final_norm_w, loss_target, m_meta_tokens, m_mix_norm_w, m_w_in, m_conv_w, m_conv_b, m_lru_wa, m_lru_ba, m_lru_wx, m_lru_bx, m_lru_lambda, m_w_branch_ret, m_w_branch_lru, m_w_out, m_ffn_norm_w, m_w_ffn_in, m_w_ffn_out, m_final_norm_w, v_meta_tokens, v_mix_norm_w, v_w_in, v_conv_w, v_conv_b, v_lru_wa, v_lru_ba, v_lru_wx, v_lru_bx, v_lru_lambda, v_w_branch_ret, v_w_branch_lru, v_w_out, v_ffn_norm_w, v_w_ffn_in, v_w_ffn_out, v_final_norm_w):
    me = _my_index()
    pad4 = ((0, 4), (0, 0))
    fw = final_norm_w.reshape(1, D_MODEL)

    small = jnp.concatenate([meta_tokens, jnp.pad(conv_w[0], pad4)], axis=0)
    small_g = _gather_small(small)
    meta_full = small_g[:, :N_META].transpose(1, 0, 2).reshape(N_META, D_MODEL)
    conv_w_full = small_g[:, N_META:N_META + 4].transpose(1, 0, 2).reshape(4, D_MODEL)
    mixer_shards = [w_branch_ret[0].astype(BF16), w_branch_lru[0].astype(BF16), w_out[0].astype(BF16),
                    lru_wa[0].astype(BF16), lru_wx[0].astype(BF16)]
    wfi_shard = jnp.pad(w_ffn_in[0].astype(BF16), ((0, 0), (0, FFN_GROUP - FFN_SHARD)))
    own_slot = lambda ins, p: list(ins)
    part_of_owner = lambda ins, p: [r.at[p] for r in ins]

    rows = x.shape[1] + CHUNK
    h0 = jnp.concatenate([jnp.zeros((PAD_ROWS, D_MODEL), F32), meta_full, x[0]], axis=0)
    tgt = jnp.concatenate([jnp.zeros((CHUNK, D_MODEL), F32), loss_target[0]], axis=0)
    cos2, sin2 = _rope_tables(rows)
    dec = _retention_consts()

    proj, u, win_g = _in_proj(h0, mix_norm_w, w_in[0].astype(BF16), me.astype(jnp.int32).reshape(1))
    o, zr, states, wbr_g, wbl_g, wout_g, wa_g, wx_g = _retention_fwd(
        proj, cos2, sin2, dec, _mixer_weights_ride(mixer_shards))
    wbr, wbl, wout = (t.reshape(D_MODEL, D_MODEL) for t in (wbr_g, wbl_g, wout_g))
    wa_g, wx_g = _from_owners(wa_g), _from_owners(wx_g)
    gather_wfi = _Ride([wfi_shard], [jax.ShapeDtypeStruct((N_DEV, D_MODEL, FFN_GROUP), BF16)],
                       own_slot, _slot_of_sender)
    hs, zl, wfi_g = _lru_fwd(proj, conv_w_full, conv_b, lru_ba, lru_bx, lru_lambda, wa_g, wx_g, gather_wfi)
    h1, yr, yl, mixed, wfo_g = _mix_fwd(zr, zl, proj, h0, wbr, wbl, wout, _wfo_ride(w_ffn_out[0].astype(BF16)))
    u2, g, up, act, dh2, red = _ffn_fwd_loss(h1, ffn_norm_w, wfi_g, wfo_g, fw, tgt)

    d_wfo = _wgrad(act, dh2, FFN_GROUP, D_MODEL, BF16)[:, 0]
    dgu, dh1, dw_ffn_norm = _ffn_bwd(dh2, g, up, h1, ffn_norm_w, wfi_g, wfo_g)
    d_wfi = _wgrad(u2, dgu, D_MODEL, FFN_GROUP, BF16, b_halves=True)[0]
    d_wout = _wgrad(mixed, dh1, D_MODEL, D_MODEL, BF16)[0, 0]
    dyr, dyl, dproj, dzr, dzl = _mix_bwd(dh1, yr, yl, proj, wbr, wbl, wout)
    d_wbr = _wgrad(zr, dyr, D_MODEL, D_MODEL, BF16)[0, 0]
    d_wbl = _wgrad(zl, dyl, D_MODEL, D_MODEL, BF16)[0, 0]
    dproj, d_wa, d_wx, lru_small, r_fi, r_fo = _lru_bwd(
        dzl, hs, proj, dproj, conv_w_full, conv_b, lru_ba, lru_bx, lru_lambda, wa_g, wx_g, [d_wfi, d_wfo])
    mix_shape = jax.ShapeDtypeStruct((N_DEV, D_MODEL // N_DEV, D_MODEL), BF16)
    scatter_mix = _Ride([t.reshape(mix_shape.shape) for t in (d_wbr, d_wbl, d_wout)], [mix_shape] * 3,
                        part_of_owner, _slot_of_sender)
    dproj, r_br, r_bl, r_out = _retention_bwd(dzr, o, proj, states, cos2, sin2, dec, dproj, scatter_mix)
    d_win = _wgrad(u, dproj, D_MODEL, D_MODEL, BF16)[0]
    dh0, dw_mix_norm, r_in, r_wa, r_wx = _in_proj_bwd(dproj, win_g, h0, mix_norm_w, dh1,
                                                      [d_win, _by_owner(d_wa), _by_owner(d_wx)])
    grad_x = dh0[CHUNK:]

    pack = jnp.concatenate([dw_mix_norm, lru_small, dw_ffn_norm, red, dh0[PAD_ROWS:CHUNK]], axis=0)
    small_sum = _sum_slots(_share_pack(pack))
    loss = (0.5 / D_MODEL) * jnp.sum(small_sum[PACK_SQ_ERR])

    def big_update(slots, w, m, v):
        shape = w.shape
        w2, m2, v2 = (t.reshape(slots.shape[1:]) for t in (w, m, v))
        return [t.reshape(shape) for t in _adamw(slots, w2, m2, v2)]

    res = {}
    res["w_in"] = big_update(r_in, w_in, m_w_in, v_w_in)
    res["w_branch_ret"] = big_update(r_br, w_branch_ret, m_w_branch_ret, v_w_branch_ret)
    res["w_branch_lru"] = big_update(r_bl, w_branch_lru, m_w_branch_lru, v_w_branch_lru)
    res["w_out"] = big_update(r_out, w_out, m_w_out, v_w_out)
    res["w_ffn_in"] = big_update(r_fi[:, :, :FFN_SHARD], w_ffn_in, m_w_ffn_in, v_w_ffn_in)
    res["w_ffn_out"] = big_update(r_fo, w_ffn_out, m_w_ffn_out, v_w_ffn_out)
    res["lru_wa"] = big_update(r_wa.reshape(N_DEV, LRU_BLOCKS * LRU_ROWS, LRU_BLOCK), lru_wa, m_lru_wa, v_lru_wa)
    res["lru_wx"] = big_update(r_wx.reshape(N_DEV, LRU_BLOCKS * LRU_ROWS, LRU_BLOCK), lru_wx, m_lru_wx, v_lru_wx)

    col = me * HEAD_DIM
    g_meta = lax.dynamic_slice(small_sum, (PACK_META, col), (N_META, HEAD_DIM))
    g_conv = lax.dynamic_slice(small_sum, (PACK_CONV_W, col), (8, HEAD_DIM))
    small_names = ["mix_norm_w", "conv_b", "lru_ba", "lru_bx", "lru_lambda", "ffn_norm_w", "final_norm_w"]
    small_rows = [PACK_MIX_NORM, PACK_CONV_B, PACK_BA, PACK_BX, PACK_LAM, PACK_FFN_NORM, PACK_FINAL_NORM]
    small_w = [mix_norm_w, conv_b, lru_ba, lru_bx, lru_lambda, ffn_norm_w, fw]
    small_m = [m_mix_norm_w, m_conv_b, m_lru_ba, m_lru_bx, m_lru_lambda, m_ffn_norm_w, m_final_norm_w.reshape(1, -1)]
    small_v = [v_mix_norm_w, v_conv_b, v_lru_ba, v_lru_bx, v_lru_lambda, v_ffn_norm_w, v_final_norm_w.reshape(1, -1)]

    def pack_small(vec_list, meta_t, conv_t):
        return jnp.concatenate([t.reshape(8, HEAD_DIM) for t in vec_list] + [meta_t, jnp.pad(conv_t[0], pad4)], axis=0)

    g_small = jnp.concatenate([small_sum[r].reshape(8, HEAD_DIM) for r in small_rows] + [g_meta, g_conv], axis=0)
    outs_small = _adamw(g_small[None], pack_small(small_w, meta_tokens, conv_w),
                        pack_small(small_m, m_meta_tokens, m_conv_w), pack_small(small_v, v_meta_tokens, v_conv_w))
    for idx, name in enumerate(small_names):
        shape = final_norm_w.shape if name == "final_norm_w" else (1, D_MODEL)
        res[name] = [t[8 * idx:8 * idx + 8].reshape(shape) for t in outs_small]
    res["meta_tokens"] = [t[56:72] for t in outs_small]
    res["conv_w"] = [t[72:76].reshape(1, 4, HEAD_DIM) for t in outs_small]

    order = ["meta_tokens", "mix_norm_w", "w_in", "conv_w", "conv_b", "lru_wa", "lru_ba", "lru_wx", "lru_bx",
             "lru_lambda", "w_branch_ret", "w_branch_lru", "w_out", "ffn_norm_w", "w_ffn_in", "w_ffn_out",
             "final_norm_w"]
    out = [loss, grad_x[None]]
    for kind in range(4):
        out += [res[name][kind] for name in order]
    return tuple(out)
```

```python
import functools

import numpy as np
import jax
import jax.numpy as jnp
from jax import lax
from jax.experimental import pallas as pl
from jax.experimental.pallas import tpu as pltpu

F32 = jnp.float32
BF16 = jnp.bfloat16

D_MODEL = 1024
N_META = 16
CHUNK = 128
PAD_ROWS = CHUNK - N_META
HEADS = 8
HEAD_DIM = 128
ROPE_BASE = 10000.0
QK_SCALE = HEAD_DIM ** -0.5
LRU_BLOCKS = 4
LRU_BLOCK = 256
LRU_C = 8.0
FFN_HIDDEN = 2816
N_DEV = 8
FFN_SHARD = 2 * FFN_HIDDEN // N_DEV
FFN_GROUP = 768
FFN_GROUPS = 4
FFN_OUT_SHARD = FFN_HIDDEN // N_DEV
NORM_EPS = 1e-6

ADAM_LR = 0.001
ADAM_B1 = 0.9
ADAM_B2 = 0.999
ADAM_EPS = 1e-08
ADAM_WD = 0.01
ADAM_STEP = 10

VMEM_LIMIT = 56 * 1024 * 1024
MESH_ID = pl.DeviceIdType.MESH
ANY = pl.BlockSpec(memory_space=pl.ANY)


def _cparams(sem):
    return pltpu.CompilerParams(dimension_semantics=sem, vmem_limit_bytes=VMEM_LIMIT)


def _tile(rows, cap):
    t = cap - cap % 64
    while rows % t:
        t -= 64
    return t


def _dot(a, b):
    return jnp.dot(a, b, preferred_element_type=F32)


def _dot_nt(a, b):
    return lax.dot_general(a, b, (((1,), (1,)), ((), ())), preferred_element_type=F32)


def _dot_tn(a, b):
    return lax.dot_general(a, b, (((0,), (0,)), ((), ())), preferred_element_type=F32)


def _sigmoid(x):
    return 0.5 * jnp.tanh(0.5 * x) + 0.5


def _gelu_parts(x):
    k = 0.7978845608028654
    inner = k * (x + 0.044715 * x * x * x)
    t = jnp.tanh(inner)
    g = 0.5 * x * (1.0 + t)
    dg = 0.5 * (1.0 + t) + 0.5 * x * (1.0 - t * t) * k * (1.0 + 3.0 * 0.044715 * x * x)
    return g, dg


def _rot(x, cos2, sin2):
    return x * cos2 + pltpu.roll(x, HEAD_DIM // 2, 1) * sin2


def _rot_t(dx, cos2, sin2):
    return dx * cos2 - pltpu.roll(dx, HEAD_DIM // 2, 1) * sin2


def _rms_bwd(x, w, dy):
    rs = lax.rsqrt(jnp.mean(x * x, axis=-1, keepdims=True) + NORM_EPS)
    nh = x * rs
    dw = jnp.sum(dy * nh, axis=0, keepdims=True)
    dn = dy * w
    dx = rs * (dn - nh * jnp.mean(dn * nh, axis=-1, keepdims=True))
    return dx, dw


def _retention_consts():
    h = jnp.arange(HEADS, dtype=F32)
    log_g = jnp.log(1.0 - 2.0 ** (-5.0 - h))
    idx = jnp.arange(CHUNK, dtype=F32)
    diff = idx[:, None] - idx[None, :]
    intra = jnp.where(diff[None] >= 0, jnp.exp(jnp.maximum(diff, 0.0)[None] * log_g[:, None, None]), 0.0)
    q_decay = jnp.exp((idx + 1.0)[:, None] * log_g[None, :])
    k_decay = jnp.exp((CHUNK - 1.0 - idx)[:, None] * log_g[None, :])
    chunk_decay = jnp.exp(CHUNK * log_g)
    shape = (HEADS, CHUNK, CHUNK)
    qd = jnp.broadcast_to(q_decay.T[:, :, None], shape)
    kd = jnp.broadcast_to(k_decay.T[:, :, None], shape)
    cd = jnp.broadcast_to(chunk_decay[:, None, None], shape)
    return jnp.stack([intra, qd, kd, cd])


def _rope_tables(rows):
    pos = jnp.maximum(jnp.arange(rows) - PAD_ROWS, 0).astype(F32)
    inv_freq = ROPE_BASE ** (-jnp.arange(0, HEAD_DIM, 2, dtype=F32) / HEAD_DIM)
    ang = pos[:, None] * inv_freq[None, :]
    cos, sin = jnp.cos(ang), jnp.sin(ang)
    return jnp.concatenate([cos, cos], axis=1), jnp.concatenate([-sin, sin], axis=1)


def _my_index():
    return 4 * lax.axis_index("x") + 2 * lax.axis_index("y") + lax.axis_index("c")


def _peer(k):
    x, y, c = lax.axis_index("x"), lax.axis_index("y"), lax.axis_index("c")
    px = 1 - x if k & 4 else x
    py = 1 - y if k & 2 else y
    pc = 1 - c if k & 1 else c
    return (px, py, pc), 4 * px + 2 * py + pc


def _push_sems(n_arr):
    n_rem = (N_DEV - 1) * n_arr
    return [pltpu.SemaphoreType.DMA((n_rem,)), pltpu.SemaphoreType.DMA((n_rem,)), pltpu.SemaphoreType.DMA((n_arr,))]


class _Push:
    def __init__(self, send_part, land_slot, sems, n_arr):
        self.send_part, self.land_slot, self.n_arr = send_part, land_slot, n_arr
        self.send_sems, self.recv_sems, self.loc_sems = sems

    def _remote(self, k, a, src, dst, pos):
        idx = (k - 1) * self.n_arr + a
        return pltpu.make_async_remote_copy(src_ref=src, dst_ref=dst, send_sem=self.send_sems.at[idx],
                                            recv_sem=self.recv_sems.at[idx], device_id=pos, device_id_type=MESH_ID)

    def _outgoing(self):
        me = _my_index()
        land = self.land_slot(me)
        remote = []
        for k in range(1, N_DEV):
            pos, p = _peer(k)
            src = self.send_part(p)
            remote += [self._remote(k, a, src[a], land[a], pos) for a in range(self.n_arr)]
        own = self.send_part(me)
        local = [pltpu.make_async_copy(own[a], land[a], self.loc_sems.at[a]) for a in range(self.n_arr)]
        return remote, local

    def start(self):
        remote, local = self._outgoing()
        for cp in remote + local:
            cp.start()

    def wait_recv_from(self, k):
        own = self.send_part(_my_index())
        pos, p = _peer(k)
        land = self.land_slot(p)
        for a in range(self.n_arr):
            self._remote(k, a, own[a], land[a], pos).wait_recv()

    def wait_sends(self):
        remote, local = self._outgoing()
        for cp in remote:
            cp.wait_send()
        for cp in local:
            cp.wait()

    def wait(self):
        for k in range(1, N_DEV):
            self.wait_recv_from(k)
        self.wait_sends()


class _Ride:
    def __init__(self, arrays, out_shapes, send_part, land_slot, zero_dsts=None, zero_shape=None, n_zero=0):
        self.arrays, self.out_shapes = list(arrays), list(out_shapes)
        self.send_part, self.land_slot, self.n = send_part, land_slot, len(arrays)
        self.zero_dsts, self.zero_shape, self.n_zero = zero_dsts, zero_shape, n_zero

    def specs(self):
        return [ANY] * self.n

    def scratch(self):
        extra = [pltpu.SemaphoreType.DMA((self.n_zero,)), pltpu.VMEM(self.zero_shape, BF16)] if self.n_zero else []
        return _push_sems(self.n) + extra

    def push(self, in_refs, out_refs, scratch):
        ride = self
        push = _Push(lambda p: ride.send_part(in_refs, p), lambda s: ride.land_slot(out_refs, s),
                     tuple(scratch[:3]), self.n)

        class Both:
            def _fills(self):
                if not ride.n_zero:
                    return []
                zsems, zbuf = scratch[3], scratch[4]
                return [pltpu.make_async_copy(zbuf, dst, zsems.at[z]) for z, dst in enumerate(ride.zero_dsts(out_refs))]

            def start(self):
                push.start()
                if ride.n_zero:
                    scratch[4][...] = jnp.zeros(ride.zero_shape, BF16)
                for cp in self._fills():
                    cp.start()

            def wait(self):
                push.wait()
                for cp in self._fills():
                    cp.wait()

        return Both()


def _slot_of_sender(out_refs, s):
    return [r.at[s] for r in out_refs]


def _push_call(name, arrays, out_shapes, send_part, land_slot):
    n_arr = len(arrays)

    def body(*refs):
        ins, outs, sems = refs[:n_arr], refs[n_arr:2 * n_arr], refs[2 * n_arr:]
        push = _Push(lambda p: send_part(ins, p), lambda s: land_slot(outs, s), sems, n_arr)
        push.start()
        push.wait()

    return pl.pallas_call(
        body, name=name, in_specs=[ANY] * n_arr, out_specs=[ANY] * n_arr, out_shape=out_shapes,
        scratch_shapes=_push_sems(n_arr), compiler_params=pltpu.CompilerParams(has_side_effects=True),
    )(*arrays)


LRU_ROWS = LRU_BLOCK // N_DEV
FFN_PAD_ROWS = FFN_GROUP - 2 * FFN_OUT_SHARD


def _half_rows(d):
    return pl.ds(pl.multiple_of((d % 2) * FFN_OUT_SHARD, 16), FFN_OUT_SHARD)


MIXER_SHAPES = [
    jax.ShapeDtypeStruct((N_DEV, D_MODEL // N_DEV, D_MODEL), BF16),
    jax.ShapeDtypeStruct((N_DEV, D_MODEL // N_DEV, D_MODEL), BF16),
    jax.ShapeDtypeStruct((N_DEV, D_MODEL // N_DEV, D_MODEL), BF16),
    jax.ShapeDtypeStruct((N_DEV, LRU_BLOCKS, LRU_ROWS, LRU_BLOCK), BF16),
    jax.ShapeDtypeStruct((N_DEV, LRU_BLOCKS, LRU_ROWS, LRU_BLOCK), BF16),
]


def _by_owner(t):
    return t.reshape(LRU_BLOCKS, N_DEV, LRU_ROWS, LRU_BLOCK).transpose(1, 0, 2, 3)


def _from_owners(t):
    return t.transpose(1, 0, 2, 3).reshape(LRU_BLOCKS, LRU_BLOCK, LRU_BLOCK)


def _mixer_weights_ride(shards):
    return _Ride(shards, MIXER_SHAPES, lambda ins, p: list(ins), _slot_of_sender)


def _wfo_ride(shard):
    zero_dsts = lambda outs: [outs[0].at[g, pl.ds(2 * FFN_OUT_SHARD, FFN_PAD_ROWS), :] for g in range(FFN_GROUPS)]
    return _Ride([shard], [jax.ShapeDtypeStruct((FFN_GROUPS, FFN_GROUP, D_MODEL), BF16)], lambda ins, p: list(ins),
                 lambda outs, d: [outs[0].at[d // 2, _half_rows(d), :]], zero_dsts, (FFN_PAD_ROWS, D_MODEL), FFN_GROUPS)


def _arrival_rank_to_relation(jj):
    return jnp.where(jj == 3, 4, jnp.where(jj == 4, 3, jj))


def _in_proj(h0, norm_w, win_shard, me_arr):
    rows = h0.shape[0]
    tm = _tile(rows, 640)
    n_i = rows // tm

    direct = (1, 2, 4, 6)
    relayed = (2, 4, 6)

    def body(me_ref, h_ref, nw_ref, wsh_ref, proj_ref, u_ref, wing_ref, u_all, wbuf, copy_sem,
             send_sems, recv_sems, relay_send_sems, relay_recv_sems, own_sem):
        del me_ref
        jj, i = pl.program_id(0), pl.program_id(1)
        me = _my_index()
        sibling = _peer(1)[0]

        def direct_copy(k, slot):
            n = direct.index(k)
            return pltpu.make_async_remote_copy(src_ref=wsh_ref, dst_ref=wing_ref.at[slot], send_sem=send_sems.at[n],
                                                recv_sem=recv_sems.at[n], device_id=_peer(k)[0], device_id_type=MESH_ID)

        def relay_copy(q, slot):
            n = relayed.index(q)
            return pltpu.make_async_remote_copy(src_ref=wing_ref.at[slot], dst_ref=wing_ref.at[slot],
                                                send_sem=relay_send_sems.at[n], recv_sem=relay_recv_sems.at[n],
                                                device_id=sibling, device_id_type=MESH_ID)

        own_slot = pltpu.make_async_copy(wsh_ref, wing_ref.at[me], own_sem)

        @pl.when(jnp.logical_and(jj == 0, i == 0))
        def _():
            for k in direct:
                direct_copy(k, me).start()
            own_slot.start()
            own = pltpu.make_async_copy(wsh_ref, wbuf, copy_sem)
            own.start()
            own.wait()

        for k in range(1, N_DEV):
            rank = {3: 4, 4: 3}.get(k, k)

            @pl.when(jnp.logical_and(jj == rank, i == 0))
            def _(k=k):
                p = _peer(k)[1]
                if k in direct:
                    direct_copy(k, p).wait_recv()
                    if k in relayed:
                        relay_copy(k, p).start()
                else:
                    relay_copy(k - 1, p).wait_recv()
                landed = pltpu.make_async_copy(wing_ref.at[p], wbuf, copy_sem)
                landed.start()
                landed.wait()

        rows_i = pl.ds(pl.multiple_of(i * tm, tm), tm)

        @pl.when(jj == 0)
        def _():
            x = h_ref[...]
            rs = lax.rsqrt(jnp.mean(x * x, axis=-1, keepdims=True) + NORM_EPS)
            u = (x * rs * nw_ref[...]).astype(BF16)
            u_all[rows_i, :] = u
            u_ref[...] = u
        proj_ref[...] = _dot(u_all[rows_i, :], wbuf[...]).astype(BF16)

        @pl.when(jnp.logical_and(jj == N_DEV - 1, i == n_i - 1))
        def _():
            for k in direct:
                direct_copy(k, me).wait_send()
            for q in relayed:
                relay_copy(q, _peer(q)[1]).wait_send()
            own_slot.wait()

    first_pass = lambda jj, i: jnp.where(jj == 0, i, n_i - 1)
    grid_spec = pltpu.PrefetchScalarGridSpec(
        num_scalar_prefetch=1, grid=(N_DEV, n_i),
        in_specs=[pl.BlockSpec((tm, D_MODEL), lambda jj, i, me: (first_pass(jj, i), 0)),
                  pl.BlockSpec((1, D_MODEL), lambda jj, i, me: (0, 0)), ANY],
        out_specs=[pl.BlockSpec((tm, D_MODEL), lambda jj, i, me: (i, me[0] ^ _arrival_rank_to_relation(jj))),
                   pl.BlockSpec((tm, D_MODEL), lambda jj, i, me: (first_pass(jj, i), 0)), ANY],
        scratch_shapes=[pltpu.VMEM((rows, D_MODEL), BF16), pltpu.VMEM((D_MODEL, D_MODEL), BF16),
                        pltpu.SemaphoreType.DMA(()),
                        pltpu.SemaphoreType.DMA((len(direct),)), pltpu.SemaphoreType.DMA((len(direct),)),
                        pltpu.SemaphoreType.DMA((len(relayed),)), pltpu.SemaphoreType.DMA((len(relayed),)),
                        pltpu.SemaphoreType.DMA(())])
    return pl.pallas_call(
        body, name="in_proj", grid_spec=grid_spec,
        out_shape=[jax.ShapeDtypeStruct((rows, N_DEV * D_MODEL), BF16),
                   jax.ShapeDtypeStruct((rows, D_MODEL), BF16),
                   jax.ShapeDtypeStruct((N_DEV, D_MODEL, D_MODEL), BF16)],
        compiler_params=pltpu.CompilerParams(dimension_semantics=("arbitrary", "arbitrary"),
                                             vmem_limit_bytes=VMEM_LIMIT, has_side_effects=True),
    )(me_arr, h0, norm_w, win_shard)


def _seg_spec(rows_per_block, seg):
    return pl.BlockSpec((rows_per_block, D_MODEL), lambda n, seg=seg: (n, seg))


def _retention_fwd(proj, cos2, sin2, dec, ride):
    rows = proj.shape[0]
    n_chunks = rows // CHUNK
    n_r = ride.n

    def body(q_ref, k_ref, v_ref, g_ref, c_ref, s_ref, dec_ref, *refs):
        o_ref, zr_ref, st_ref = refs[n_r:n_r + 3]
        state = refs[2 * n_r + 3]
        push = ride.push(refs[:n_r], refs[n_r + 3:2 * n_r + 3], refs[2 * n_r + 4:])

        @pl.when(pl.program_id(0) == 0)
        def _():
            push.start()
            state[...] = jnp.zeros_like(state)
        cos_t, sin_t = c_ref[...], s_ref[...]
        st_ref[...] = state[...]
        outs, gated, new_states = [], [], []
        for h in range(HEADS):
            sl = slice(HEAD_DIM * h, HEAD_DIM * (h + 1))
            qh = _rot(q_ref[:, sl].astype(F32), cos_t, sin_t)
            kh = _rot(k_ref[:, sl].astype(F32), cos_t, sin_t) * QK_SCALE
            qb, kb, vb = qh.astype(BF16), kh.astype(BF16), v_ref[:, sl]
            s = _dot_nt(qb, kb) * dec_ref[0, h]
            st = state[h]
            o = _dot(s.astype(BF16), vb) + _dot(qb, st.astype(BF16)) * dec_ref[1, h]
            new_states.append(st * dec_ref[3, h] + _dot_tn((kh * dec_ref[2, h]).astype(BF16), vb))
            outs.append(o.astype(BF16))
            r = lax.rsqrt(jnp.mean(o * o, axis=-1, keepdims=True) + NORM_EPS)
            g = g_ref[:, sl].astype(F32)
            gated.append((g * _sigmoid(g) * (o * r)).astype(BF16))
        o_ref[...] = jnp.concatenate(outs, axis=1)
        zr_ref[...] = jnp.concatenate(gated, axis=1)
        for h in range(HEADS):
            state[h] = new_states[h]

        @pl.when(pl.program_id(0) == n_chunks - 1)
        def _():
            push.wait()

    tab = pl.BlockSpec((CHUNK, HEAD_DIM), lambda n: (n, 0))
    return pl.pallas_call(
        body, name="retention_fwd", grid=(n_chunks,),
        in_specs=[_seg_spec(CHUNK, 0), _seg_spec(CHUNK, 1), _seg_spec(CHUNK, 2), _seg_spec(CHUNK, 3), tab, tab,
                  pl.BlockSpec((4, HEADS, CHUNK, CHUNK), lambda n: (0, 0, 0, 0))] + ride.specs(),
        out_specs=[pl.BlockSpec((CHUNK, D_MODEL), lambda n: (n, 0)),
                   pl.BlockSpec((CHUNK, D_MODEL), lambda n: (n, 0)),
                   pl.BlockSpec((None, HEADS, HEAD_DIM, HEAD_DIM), lambda n: (n, 0, 0, 0))] + ride.specs(),
        out_shape=[jax.ShapeDtypeStruct((rows, D_MODEL), BF16),
                   jax.ShapeDtypeStruct((rows, D_MODEL), BF16),
                   jax.ShapeDtypeStruct((n_chunks, HEADS, HEAD_DIM, HEAD_DIM), F32)] + ride.out_shapes,
        scratch_shapes=[pltpu.VMEM((HEADS, HEAD_DIM, HEAD_DIM), F32)] + ride.scratch(),
        compiler_params=pltpu.CompilerParams(dimension_semantics=("arbitrary",), vmem_limit_bytes=VMEM_LIMIT,
                                             has_side_effects=True),
    )(proj, proj, proj, proj, cos2, sin2, dec, *ride.arrays)


def _lru_gates(c, ba, bx, lam, wa_ref, wx_ref):
    pre_r, pre_i = [], []
    for g in range(LRU_BLOCKS):
        cg = c[:, LRU_BLOCK * g:LRU_BLOCK * (g + 1)].astype(BF16)
        pre_r.append(_dot(cg, wa_ref[g]))
        pre_i.append(_dot(cg, wx_ref[g]))
    r = _sigmoid(jnp.concatenate(pre_r, axis=1) + ba)
    ig = _sigmoid(jnp.concatenate(pre_i, axis=1) + bx)
    sp = jnp.maximum(-lam, 0.0) + jnp.log(1.0 + jnp.exp(-jnp.abs(lam)))
    log_a = -LRU_C * r * sp
    a = jnp.exp(log_a)
    one_minus_a2 = -jnp.tanh(log_a) * (a * a + 1.0)
    inv_mult = lax.rsqrt(jnp.maximum(one_minus_a2, 1e-30))
    return r, ig, a, one_minus_a2 * inv_mult, inv_mult, sp


def _conv_taps(xbuf, tm, cw_ref, cb_ref):
    c = cb_ref[...] + cw_ref[3:4, :] * xbuf[8:8 + tm, :]
    for back in (1, 2, 3):
        c = c + cw_ref[3 - back:4 - back, :] * xbuf[8 - back:8 - back + tm, :]
    return c


def _lru_fwd(proj, conv_w, conv_b, ba, bx, lam, wa_g, wx_g, ride):
    rows = proj.shape[0]
    tm = _tile(rows, 320)
    n_t = rows // tm
    n_r = ride.n

    def body(x_ref, gt_ref, cw_ref, cb_ref, ba_ref, bx_ref, lam_ref, wa_ref, wx_ref, *refs):
        hs_ref, zl_ref = refs[n_r:n_r + 2]
        xbuf, abuf, ubuf, hcar = refs[2 * n_r + 2:2 * n_r + 6]
        push = ride.push(refs[:n_r], refs[n_r + 2:2 * n_r + 2], refs[2 * n_r + 6:])
        i = pl.program_id(0)

        @pl.when(i == 0)
        def _():
            push.start()
            xbuf[0:8, :] = jnp.zeros((8, D_MODEL), F32)
            hcar[...] = jnp.zeros_like(hcar)

        xbuf[8:8 + tm, :] = x_ref[...].astype(F32)
        c = _conv_taps(xbuf, tm, cw_ref, cb_ref)
        xbuf[0:8, :] = xbuf[tm:tm + 8, :]
        r, ig, a, mult, _, _ = _lru_gates(c, ba_ref[...], bx_ref[...], lam_ref[...], wa_ref, wx_ref)
        row = i * tm + lax.broadcasted_iota(jnp.int32, (tm, 1), 0)
        abuf[...] = a
        ubuf[...] = jnp.where(row >= PAD_ROWS, mult * (ig * c), 0.0)

        sub = lax.broadcasted_iota(jnp.int32, (8, D_MODEL), 0)

        def block(b, carry):
            off = pl.multiple_of(b * 8, 8)
            av, uv = abuf[pl.ds(off, 8), :], ubuf[pl.ds(off, 8), :]
            for s in (1, 2, 4):
                us = jnp.where(sub >= s, pltpu.roll(uv, s, 0), 0.0)
                as_ = jnp.where(sub >= s, pltpu.roll(av, s, 0), 1.0)
                uv = uv + av * us
                av = av * as_
            hv = uv + av * carry
            ubuf[pl.ds(off, 8), :] = hv
            return hv[7:8, :]

        hcar[...] = lax.fori_loop(0, tm // 8, block, hcar[...])
        gl, _ = _gelu_parts(gt_ref[...].astype(F32))
        hs = ubuf[...]
        hs_ref[...] = hs.astype(BF16)
        zl_ref[...] = (gl * hs).astype(BF16)

        @pl.when(i == n_t - 1)
        def _():
            push.wait()

    vec = pl.BlockSpec((1, D_MODEL), lambda i: (0, 0))
    mat = pl.BlockSpec((LRU_BLOCKS, LRU_BLOCK, LRU_BLOCK), lambda i: (0, 0, 0))
    row = pl.BlockSpec((tm, D_MODEL), lambda i: (i, 0))
    return pl.pallas_call(
        body, name="lru_fwd", grid=(n_t,),
        in_specs=[_seg_spec(tm, 4), _seg_spec(tm, 5), pl.BlockSpec((4, D_MODEL), lambda i: (0, 0)),
                  vec, vec, vec, vec, mat, mat] + ride.specs(),
        out_specs=[row, row] + ride.specs(),
        out_shape=[jax.ShapeDtypeStruct((rows, D_MODEL), BF16)] * 2 + ride.out_shapes,
        scratch_shapes=[pltpu.VMEM((tm + 8, D_MODEL), F32), pltpu.VMEM((tm, D_MODEL), F32),
                        pltpu.VMEM((tm, D_MODEL), F32), pltpu.VMEM((1, D_MODEL), F32)] + ride.scratch(),
        compiler_params=pltpu.CompilerParams(dimension_semantics=("arbitrary",), vmem_limit_bytes=VMEM_LIMIT,
                                             has_side_effects=True),
    )(proj, proj, conv_w, conv_b, ba, bx, lam, wa_g, wx_g, *ride.arrays)


def _mix_fwd(zr, zl, proj, h0, wbr, wbl, wout, ride):
    rows = h0.shape[0]
    tm = _tile(rows, 640)
    n_t = rows // tm
    n_r = ride.n

    def body(zr_ref, zl_ref, ga_ref, gb_ref, h0_ref, wbr_ref, wbl_ref, wo_ref, *refs):
        h1_ref, yr_ref, yl_ref, mx_ref = refs[n_r:n_r + 4]
        push = ride.push(refs[:n_r], refs[n_r + 4:2 * n_r + 4], refs[2 * n_r + 4:])

        @pl.when(pl.program_id(0) == 0)
        def _():
            push.start()

        yr = _dot(zr_ref[...], wbr_ref[...])
        yl = _dot(zl_ref[...], wbl_ref[...])
        mixed = (_sigmoid(ga_ref[...].astype(F32)) * yr + _sigmoid(gb_ref[...].astype(F32)) * yl).astype(BF16)
        yr_ref[...] = yr.astype(BF16)
        yl_ref[...] = yl.astype(BF16)
        mx_ref[...] = mixed
        h1_ref[...] = h0_ref[...] + _dot(mixed, wo_ref[...])

        @pl.when(pl.program_id(0) == n_t - 1)
        def _():
            push.wait()

    row = pl.BlockSpec((tm, D_MODEL), lambda i: (i, 0))
    wsp = pl.BlockSpec((D_MODEL, D_MODEL), lambda i: (0, 0))
    return pl.pallas_call(
        body, name="mix_fwd", grid=(n_t,),
        in_specs=[row, row, _seg_spec(tm, 6), _seg_spec(tm, 7), row, wsp, wsp, wsp] + ride.specs(),
        out_specs=[row, row, row, row] + ride.specs(),
        out_shape=[jax.ShapeDtypeStruct((rows, D_MODEL), F32)] + [jax.ShapeDtypeStruct((rows, D_MODEL), BF16)] * 3
        + ride.out_shapes,
        scratch_shapes=ride.scratch(),
        compiler_params=pltpu.CompilerParams(dimension_semantics=("arbitrary",), vmem_limit_bytes=VMEM_LIMIT,
                                             has_side_effects=True),
    )(zr, zl, proj, proj, h0, wbr, wbl, wout, *ride.arrays)


def _ffn_fwd_loss(h1, norm_w, wfi_g, wfo_g, final_w, target):
    rows = h1.shape[0]
    tm = _tile(rows, 320)

    def body(h1_ref, nw_ref, wfi_ref, wfo_ref, fw_ref, t_ref, u2_ref, g_ref, up_ref, act_ref, dh2_ref, red_ref):
        i = pl.program_id(0)

        @pl.when(i == 0)
        def _():
            red_ref[...] = jnp.zeros_like(red_ref)

        x = h1_ref[...]
        rs = lax.rsqrt(jnp.mean(x * x, axis=-1, keepdims=True) + NORM_EPS)
        u2 = (x * rs * nw_ref[...]).astype(BF16)
        u2_ref[...] = u2
        ffn = None
        for d in range(FFN_GROUPS):
            cols = slice(FFN_GROUP * d, FFN_GROUP * (d + 1))
            g = _dot(u2, wfi_ref[d])
            up = _dot(u2, wfi_ref[d + FFN_GROUPS])
            act = (g * _sigmoid(g) * up).astype(BF16)
            g_ref[:, cols] = g.astype(BF16)
            up_ref[:, cols] = up.astype(BF16)
            act_ref[:, cols] = act
            part = _dot(act, wfo_ref[d])
            ffn = part if ffn is None else ffn + part

        h2 = x + ffn
        rs = lax.rsqrt(jnp.mean(h2 * h2, axis=-1, keepdims=True) + NORM_EPS)
        nh = h2 * rs
        fw = fw_ref[...]
        row = i * tm + lax.broadcasted_iota(jnp.int32, (tm, 1), 0)
        diff = jnp.where(row >= CHUNK, nh * fw - t_ref[...], 0.0)
        dy = diff * (1.0 / D_MODEL)
        red_ref[0:1, :] += jnp.sum(diff * diff, axis=0, keepdims=True)
        red_ref[1:2, :] += jnp.sum(dy * nh, axis=0, keepdims=True)
        dn = dy * fw
        dh2_ref[...] = rs * (dn - nh * jnp.mean(dn * nh, axis=-1, keepdims=True))

    row = pl.BlockSpec((tm, D_MODEL), lambda i: (i, 0))
    vec = pl.BlockSpec((1, D_MODEL), lambda i: (0, 0))
    hid = pl.BlockSpec((tm, FFN_GROUPS * FFN_GROUP), lambda i: (i, 0))
    hid_shape = jax.ShapeDtypeStruct((rows, FFN_GROUPS * FFN_GROUP), BF16)
    resident = dict(pipeline_mode=pl.Buffered(1))
    return pl.pallas_call(
        body, name="ffn_fwd_loss", grid=(rows // tm,),
        in_specs=[row, vec,
                  pl.BlockSpec((2 * FFN_GROUPS, D_MODEL, FFN_GROUP), lambda i: (0, 0, 0), **resident),
                  pl.BlockSpec((FFN_GROUPS, FFN_GROUP, D_MODEL), lambda i: (0, 0, 0), **resident),
                  vec, row],
        out_specs=[row, hid, hid, hid, row, pl.BlockSpec((8, D_MODEL), lambda i: (0, 0))],
        out_shape=[jax.ShapeDtypeStruct((rows, D_MODEL), BF16), hid_shape, hid_shape, hid_shape,
                   jax.ShapeDtypeStruct((rows, D_MODEL), F32), jax.ShapeDtypeStruct((8, D_MODEL), F32)],
        compiler_params=_cparams(("arbitrary",)),
    )(h1, norm_w, wfi_g, wfo_g, final_w, target)


def _wgrad(a, b, ka, tn, out_dtype, b_halves=False):
    rows = a.shape[0]
    na = a.shape[1] // ka
    tm = _tile(rows, 1664)
    nm = rows // tm
    if b_halves:
        per_half = b.shape[2] // tn
        nb = 2 * per_half
        b_spec = pl.BlockSpec((None, tm, tn), lambda p, q, m: (q // per_half, m, q % per_half))
    else:
        nb = b.shape[1] // tn
        b_spec = pl.BlockSpec((tm, tn), lambda p, q, m: (m, q))

    def body(a_ref, b_ref, o_ref, acc):
        m = pl.program_id(2)

        @pl.when(m == 0)
        def _():
            acc[...] = jnp.zeros_like(acc)

        acc[...] += _dot_tn(a_ref[...].astype(BF16), b_ref[...].astype(BF16))

        @pl.when(m == nm - 1)
        def _():
            o_ref[...] = acc[...].astype(out_dtype)

    return pl.pallas_call(
        body, name="wgrad", grid=(na, nb, nm),
        in_specs=[pl.BlockSpec((tm, ka), lambda p, q, m: (m, p)), b_spec],
        out_specs=pl.BlockSpec((None, None, ka, tn), lambda p, q, m: (p, q, 0, 0)),
        out_shape=jax.ShapeDtypeStruct((na, nb, ka, tn), out_dtype),
        scratch_shapes=[pltpu.VMEM((ka, tn), F32)],
        compiler_params=_cparams(("parallel", "parallel", "arbitrary")),
    )(a, b)


def _ffn_bwd(dh2, g, up, h1, norm_w, wfi_g, wfo_g):
    rows = h1.shape[0]
    tm = _tile(rows, 320)

    def body(dh2_ref, g_ref, up_ref, h1_ref, nw_ref, wfi_ref, wfo_ref, dgu_ref, dh1_ref, dw_ref):
        @pl.when(pl.program_id(0) == 0)
        def _():
            dw_ref[...] = jnp.zeros_like(dw_ref)

        dh2 = dh2_ref[...]
        dh2_b = dh2.astype(BF16)
        du2 = None
        for d in range(FFN_GROUPS):
            cols = slice(FFN_GROUP * d, FFN_GROUP * (d + 1))
            dact = _dot_nt(dh2_b, wfo_ref[d])
            gv, uv = g_ref[:, cols].astype(F32), up_ref[:, cols].astype(F32)
            sg = _sigmoid(gv)
            dg = (dact * uv * (sg * (1.0 + gv * (1.0 - sg)))).astype(BF16)
            dup = (dact * (gv * sg)).astype(BF16)
            dgu_ref[0, :, cols] = dg
            dgu_ref[1, :, cols] = dup
            part = _dot_nt(dg, wfi_ref[d]) + _dot_nt(dup, wfi_ref[d + FFN_GROUPS])
            du2 = part if du2 is None else du2 + part
        dx, dw = _rms_bwd(h1_ref[...], nw_ref[...], du2)
        dw_ref[0:1, :] += dw
        dh1_ref[...] = dh2 + dx

    row = pl.BlockSpec((tm, D_MODEL), lambda i: (i, 0))
    vec = pl.BlockSpec((1, D_MODEL), lambda i: (0, 0))
    hid = pl.BlockSpec((tm, FFN_GROUPS * FFN_GROUP), lambda i: (i, 0))
    resident = dict(pipeline_mode=pl.Buffered(1))
    return pl.pallas_call(
        body, name="ffn_bwd", grid=(rows // tm,),
        in_specs=[row, hid, hid, row, vec,
                  pl.BlockSpec((2 * FFN_GROUPS, D_MODEL, FFN_GROUP), lambda i: (0, 0, 0), **resident),
                  pl.BlockSpec((FFN_GROUPS, FFN_GROUP, D_MODEL), lambda i: (0, 0, 0), **resident)],
        out_specs=[pl.BlockSpec((2, tm, FFN_GROUPS * FFN_GROUP), lambda i: (0, i, 0)), row,
                   pl.BlockSpec((8, D_MODEL), lambda i: (0, 0))],
        out_shape=[jax.ShapeDtypeStruct((2, rows, FFN_GROUPS * FFN_GROUP), BF16),
                   jax.ShapeDtypeStruct((rows, D_MODEL), F32), jax.ShapeDtypeStruct((8, D_MODEL), F32)],
        compiler_params=_cparams(("arbitrary",)),
    )(dh2, g, up, h1, norm_w, wfi_g, wfo_g)


def _mix_bwd(dh1, yr, yl, proj, wbr, wbl, wout):
    rows = dh1.shape[0]
    tm = _tile(rows, 640)

    def body(dh1_ref, yr_ref, yl_ref, ga_ref, gb_ref, wbr_ref, wbl_ref, wo_ref,
             dyr_ref, dyl_ref, dseg_ref, dzr_ref, dzl_ref):
        dmix = _dot_nt(dh1_ref[...].astype(BF16), wo_ref[...])
        sa, sb = _sigmoid(ga_ref[...].astype(F32)), _sigmoid(gb_ref[...].astype(F32))
        dyr = (dmix * sa).astype(BF16)
        dyl = (dmix * sb).astype(BF16)
        dyr_ref[...] = dyr
        dyl_ref[...] = dyl
        dseg_ref[:, 0:D_MODEL] = (dmix * yr_ref[...].astype(F32) * (sa * (1.0 - sa))).astype(BF16)
        dseg_ref[:, D_MODEL:2 * D_MODEL] = (dmix * yl_ref[...].astype(F32) * (sb * (1.0 - sb))).astype(BF16)
        dzr_ref[...] = _dot_nt(dyr, wbr_ref[...]).astype(BF16)
        dzl_ref[...] = _dot_nt(dyl, wbl_ref[...]).astype(BF16)

    row = pl.BlockSpec((tm, D_MODEL), lambda i: (i, 0))
    wsp = pl.BlockSpec((D_MODEL, D_MODEL), lambda i: (0, 0))
    bshape = jax.ShapeDtypeStruct((rows, D_MODEL), BF16)
    return pl.pallas_call(
        body, name="mix_bwd", grid=(rows // tm,),
        in_specs=[row, row, row, _seg_spec(tm, 6), _seg_spec(tm, 7), wsp, wsp, wsp],
        out_specs=[row, row, pl.BlockSpec((tm, 2 * D_MODEL), lambda i: (i, 3)), row, row],
        out_shape=[bshape, bshape, jax.ShapeDtypeStruct((rows, N_DEV * D_MODEL), BF16), bshape, bshape],
        compiler_params=_cparams(("parallel",)),
    )(dh1, yr, yl, proj, proj, wbr, wbl, wout)


S1_SHAPES = [
    jax.ShapeDtypeStruct((N_DEV, D_MODEL, FFN_GROUP), BF16),
    jax.ShapeDtypeStruct((N_DEV, FFN_OUT_SHARD, D_MODEL), BF16),
]


def _s1_parts(ins, p):
    return [ins[0].at[p], ins[1].at[p // 2, _half_rows(p), :]]


def _lru_bwd(dzl, hs, proj, dproj, conv_w, conv_b, ba, bx, lam, wa_g, wx_g, s1_grads):
    rows = dzl.shape[0]
    tm = _tile(rows, 320)
    nt = rows // tm
    t8 = tm // 8
    n_s1 = len(s1_grads)

    def body(dzl_ref, hs_ref, hsp_ref, x_ref, xp_ref, gt_ref, cw_ref, cb_ref, ba_ref, bx_ref, lam_ref,
             wa_ref, wx_ref, dproj_in, *refs):
        del dproj_in
        s1_refs = refs[:n_s1]
        dseg_ref, dwa_ref, dwx_ref, sm_ref = refs[n_s1:n_s1 + 4]
        land_refs = refs[n_s1 + 4:2 * n_s1 + 4]
        xbuf, abuf, bbuf, dbuf, dcbuf, anext, dhcar, send_sems, recv_sems, loc_sems = refs[2 * n_s1 + 4:]
        step = pl.program_id(0)
        i = nt - 1 - step
        push = _Push(lambda p: _s1_parts(s1_refs, p), lambda s: [r.at[s] for r in land_refs],
                     (send_sems, recv_sems, loc_sems), n_s1)

        @pl.when(step == 0)
        def _():
            push.start()
            dwa_ref[...] = jnp.zeros_like(dwa_ref)
            dwx_ref[...] = jnp.zeros_like(dwx_ref)
            sm_ref[...] = jnp.zeros_like(sm_ref)
            anext[...] = jnp.zeros_like(anext)
            dhcar[...] = jnp.zeros_like(dhcar)
            dcbuf[tm:tm + 8, :] = jnp.zeros((8, D_MODEL), F32)

        first = i == 0
        x_prev = jnp.where(first, 0.0, xp_ref[8:16, :].astype(F32))
        x_v = x_ref[...].astype(F32)
        xbuf[0:8, :] = x_prev
        xbuf[8:8 + tm, :] = x_v
        c = _conv_taps(xbuf, tm, cw_ref, cb_ref)
        lam_v = lam_ref[...]
        r, ig, a, mult, inv_mult, sp = _lru_gates(c, ba_ref[...], bx_ref[...], lam_v, wa_ref, wx_ref)
        hs_v = hs_ref[...].astype(F32)
        gl, dgl = _gelu_parts(gt_ref[...].astype(F32))
        dzl_v = dzl_ref[...].astype(F32)
        dseg_ref[:, D_MODEL:2 * D_MODEL] = (dzl_v * hs_v * dgl).astype(BF16)
        dbuf[...] = dzl_v * gl
        abuf[0:tm, :] = a
        abuf[tm:tm + 8, :] = jnp.broadcast_to(anext[...], (8, D_MODEL))
        bbuf[...] = abuf[1:tm + 1, :]

        sub = lax.broadcasted_iota(jnp.int32, (8, D_MODEL), 0)

        def block(k, carry):
            off = pl.multiple_of((t8 - 1 - k) * 8, 8)
            av = bbuf[pl.ds(off, 8), :]
            uv = dbuf[pl.ds(off, 8), :]
            for s in (1, 2, 4):
                us = jnp.where(sub < 8 - s, pltpu.roll(uv, 8 - s, 0), 0.0)
                as_ = jnp.where(sub < 8 - s, pltpu.roll(av, 8 - s, 0), 1.0)
                uv = uv + av * us
                av = av * as_
            hv = uv + av * carry
            dbuf[pl.ds(off, 8), :] = hv
            return hv[0:1, :]

        dhcar[...] = lax.fori_loop(0, t8, block, dhcar[...])
        anext[...] = abuf[0:1, :]
        dh = dbuf[...]

        xbuf[0:8, :] = jnp.where(first, 0.0, hsp_ref[8:16, :].astype(F32))
        xbuf[8:8 + tm, :] = hs_v
        hprev = xbuf[7:7 + tm, :]
        row = i * tm + lax.broadcasted_iota(jnp.int32, (tm, 1), 0)
        duu = jnp.where(row >= PAD_ROWS, dh, 0.0)
        da = dh * hprev
        dmult = duu * ig * c
        di = duu * mult * c
        dc = duu * mult * ig
        dlog_a = da * a - dmult * (a * a) * inv_mult
        dr = dlog_a * (-LRU_C * sp)
        dsp = jnp.sum(dlog_a * (-LRU_C * r), axis=0, keepdims=True)
        dpr = dr * r * (1.0 - r)
        dpi = di * ig * (1.0 - ig)
        dpr_b, dpi_b = dpr.astype(BF16), dpi.astype(BF16)
        dcs = []
        for g in range(LRU_BLOCKS):
            sl = slice(LRU_BLOCK * g, LRU_BLOCK * (g + 1))
            cg = c[:, sl].astype(BF16)
            dwa_ref[g] += _dot_tn(cg, dpr_b[:, sl])
            dwx_ref[g] += _dot_tn(cg, dpi_b[:, sl])
            dcs.append(_dot_nt(dpr_b[:, sl], wa_ref[g]) + _dot_nt(dpi_b[:, sl], wx_ref[g]))
        dc = dc + jnp.concatenate(dcs, axis=1)

        dcbuf[0:tm, :] = dc
        xbuf[8:8 + tm, :] = x_v
        xbuf[0:8, :] = x_prev
        dlin = cw_ref[3:4, :] * dc
        sm_ref[3:4, :] += jnp.sum(dc * xbuf[8:8 + tm, :], axis=0, keepdims=True)
        for back in (1, 2, 3):
            dlin = dlin + cw_ref[3 - back:4 - back, :] * dcbuf[back:back + tm, :]
            sm_ref[3 - back:4 - back, :] += jnp.sum(dc * xbuf[8 - back:8 - back + tm, :], axis=0, keepdims=True)
        dseg_ref[:, 0:D_MODEL] = dlin.astype(BF16)
        dcbuf[tm:tm + 8, :] = dcbuf[0:8, :]
        sm_ref[4:5, :] += jnp.sum(dc, axis=0, keepdims=True)
        sm_ref[5:6, :] += jnp.sum(dpr, axis=0, keepdims=True)
        sm_ref[6:7, :] += jnp.sum(dpi, axis=0, keepdims=True)
        sm_ref[7:8, :] += dsp * (-_sigmoid(-lam_v))

        @pl.when(step == nt - 1)
        def _():
            push.wait()

    rowb = pl.BlockSpec((tm, D_MODEL), lambda s: (nt - 1 - s, 0))
    t16 = tm // 16
    prev8 = pl.BlockSpec((16, D_MODEL), lambda s: (jnp.maximum((nt - 1 - s) * t16 - 1, 0), 0))
    seg = lambda k: pl.BlockSpec((tm, D_MODEL), lambda s, k=k: (nt - 1 - s, k))
    prev8_seg4 = pl.BlockSpec((16, D_MODEL), lambda s: (jnp.maximum((nt - 1 - s) * t16 - 1, 0), 4))
    vec = pl.BlockSpec((1, D_MODEL), lambda s: (0, 0))
    mat = pl.BlockSpec((LRU_BLOCKS, LRU_BLOCK, LRU_BLOCK), lambda s: (0, 0, 0))
    mshape = jax.ShapeDtypeStruct((LRU_BLOCKS, LRU_BLOCK, LRU_BLOCK), F32)
    n_in = 13
    return pl.pallas_call(
        body, name="lru_bwd", grid=(nt,),
        in_specs=[rowb, rowb, prev8, seg(4), prev8_seg4, seg(5), pl.BlockSpec((4, D_MODEL), lambda s: (0, 0)),
                  vec, vec, vec, vec, mat, mat, ANY] + [ANY] * n_s1,
        out_specs=[pl.BlockSpec((tm, 2 * D_MODEL), lambda s: (nt - 1 - s, 2)), mat, mat,
                   pl.BlockSpec((8, D_MODEL), lambda s: (0, 0))] + [ANY] * n_s1,
        out_shape=[jax.ShapeDtypeStruct(dproj.shape, dproj.dtype), mshape, mshape,
                   jax.ShapeDtypeStruct((8, D_MODEL), F32)] + S1_SHAPES,
        input_output_aliases={n_in: 0},
        scratch_shapes=[pltpu.VMEM((tm + 8, D_MODEL), F32), pltpu.VMEM((tm + 8, D_MODEL), F32),
                        pltpu.VMEM((tm, D_MODEL), F32), pltpu.VMEM((tm, D_MODEL), F32),
                        pltpu.VMEM((tm + 8, D_MODEL), F32),
                        pltpu.VMEM((1, D_MODEL), F32), pltpu.VMEM((1, D_MODEL), F32)] + _push_sems(n_s1),
        compiler_params=pltpu.CompilerParams(dimension_semantics=("arbitrary",), vmem_limit_bytes=VMEM_LIMIT,
                                             has_side_effects=True),
    )(dzl, hs, hs, proj, proj, proj, conv_w, conv_b, ba, bx, lam, wa_g, wx_g, dproj, *s1_grads)


def _retention_bwd(dzr, o, proj, states, cos2, sin2, dec, dproj, ride):
    rows = dzr.shape[0]
    n_chunks = rows // CHUNK
    n_r = ride.n

    def body(dzr_ref, o_ref, q_ref, k_ref, v_ref, g_ref, st_ref, c_ref, s_ref, dec_ref, dproj_in, *refs):
        del dproj_in
        dseg_ref = refs[n_r]
        dstate = refs[2 * n_r + 1]
        push = ride.push(refs[:n_r], refs[n_r + 1:2 * n_r + 1], refs[2 * n_r + 2:])

        @pl.when(pl.program_id(0) == 0)
        def _():
            push.start()
            dstate[...] = jnp.zeros_like(dstate)
        cos_t, sin_t = c_ref[...], s_ref[...]
        for h in range(HEADS):
            sl = slice(HEAD_DIM * h, HEAD_DIM * (h + 1))
            o = o_ref[:, sl].astype(F32)
            g = g_ref[:, sl].astype(F32)
            dzr_v = dzr_ref[:, sl].astype(F32)
            sg = _sigmoid(g)
            r = lax.rsqrt(jnp.mean(o * o, axis=-1, keepdims=True) + NORM_EPS)
            on = o * r
            dseg_ref[:, 3 * D_MODEL + HEAD_DIM * h:3 * D_MODEL + HEAD_DIM * (h + 1)] = (
                dzr_v * on * (sg * (1.0 + g * (1.0 - sg)))).astype(BF16)
            don = dzr_v * (g * sg)
            do = r * (don - on * jnp.mean(don * on, axis=-1, keepdims=True))
            dob = do.astype(BF16)

            qh = _rot(q_ref[:, sl].astype(F32), cos_t, sin_t)
            kh = _rot(k_ref[:, sl].astype(F32), cos_t, sin_t) * QK_SCALE
            qb, kb, vb = qh.astype(BF16), kh.astype(BF16), v_ref[:, sl]
            intra, qd, kd, cd = dec_ref[0, h], dec_ref[1, h], dec_ref[2, h], dec_ref[3, h]
            s = (_dot_nt(qb, kb) * intra).astype(BF16)
            ds = (_dot_nt(dob, vb) * intra).astype(BF16)
            st_b = st_ref[h].astype(BF16)
            dst = dstate[h]
            dst_b = dst.astype(BF16)
            dv = _dot_tn(s, dob) + _dot((kh * kd).astype(BF16), dst_b)
            dq = _dot(ds, kb) + _dot_nt(dob, st_b) * qd
            dk = _dot_tn(ds, qb) + _dot_nt(vb, dst_b) * kd
            dstate[h] = dst * cd + _dot_tn((qh * qd).astype(BF16), dob)
            dseg_ref[:, 2 * D_MODEL + HEAD_DIM * h:2 * D_MODEL + HEAD_DIM * (h + 1)] = dv.astype(BF16)
            dseg_ref[:, sl] = _rot_t(dq, cos_t, sin_t).astype(BF16)
            dseg_ref[:, D_MODEL + HEAD_DIM * h:D_MODEL + HEAD_DIM * (h + 1)] = (
                _rot_t(dk, cos_t, sin_t) * QK_SCALE).astype(BF16)

        @pl.when(pl.program_id(0) == n_chunks - 1)
        def _():
            push.wait()

    rev = lambda s: n_chunks - 1 - s
    rowb = pl.BlockSpec((CHUNK, D_MODEL), lambda s: (rev(s), 0))
    seg = lambda k: pl.BlockSpec((CHUNK, D_MODEL), lambda s, k=k: (rev(s), k))
    tab = pl.BlockSpec((CHUNK, HEAD_DIM), lambda s: (rev(s), 0))
    return pl.pallas_call(
        body, name="retention_bwd", grid=(n_chunks,),
        in_specs=[rowb, rowb, seg(0), seg(1), seg(2), seg(3),
                  pl.BlockSpec((None, HEADS, HEAD_DIM, HEAD_DIM), lambda s: (rev(s), 0, 0, 0)), tab, tab,
                  pl.BlockSpec((4, HEADS, CHUNK, CHUNK), lambda s: (0, 0, 0, 0)), ANY] + ride.specs(),
        out_specs=[pl.BlockSpec((CHUNK, 4 * D_MODEL), lambda s: (rev(s), 0))] + ride.specs(),
        out_shape=[jax.ShapeDtypeStruct(dproj.shape, dproj.dtype)] + ride.out_shapes,
        input_output_aliases={10: 0},
        scratch_shapes=[pltpu.VMEM((HEADS, HEAD_DIM, HEAD_DIM), F32)] + ride.scratch(),
        compiler_params=pltpu.CompilerParams(dimension_semantics=("arbitrary",), vmem_limit_bytes=VMEM_LIMIT,
                                             has_side_effects=True),
    )(dzr, o, proj, proj, proj, proj, states, cos2, sin2, dec, dproj, *ride.arrays)


S2_SHAPES = [
    jax.ShapeDtypeStruct((N_DEV, D_MODEL, D_MODEL), BF16),
    jax.ShapeDtypeStruct((N_DEV, LRU_BLOCKS, LRU_ROWS, LRU_BLOCK), F32),
    jax.ShapeDtypeStruct((N_DEV, LRU_BLOCKS, LRU_ROWS, LRU_BLOCK), F32),
]


def _s2_parts(ins, p):
    return [r.at[p] for r in ins]


def _in_proj_bwd(dproj, win_g, h0, norm_w, dh1, s2_grads):
    rows = h0.shape[0]
    tm = _tile(rows, 320)
    n_i = rows // tm
    n_s2 = len(s2_grads)

    def body(dseg_ref, w_ref, h0_ref, nw_ref, dh1_ref, *refs):
        s2_refs = refs[:n_s2]
        dh0_ref, dw_ref = refs[n_s2:n_s2 + 2]
        land_refs = refs[n_s2 + 2:2 * n_s2 + 2]
        send_sems, recv_sems, loc_sems = refs[2 * n_s2 + 2:]
        i = pl.program_id(0)
        push = _Push(lambda p: _s2_parts(s2_refs, p), lambda s: [r.at[s] for r in land_refs],
                     (send_sems, recv_sems, loc_sems), n_s2)

        @pl.when(i == 0)
        def _():
            push.start()
            dw_ref[...] = jnp.zeros_like(dw_ref)

        du = _dot_nt(dseg_ref[:, 0:D_MODEL], w_ref[0])
        for j in range(1, N_DEV):
            du = du + _dot_nt(dseg_ref[:, D_MODEL * j:D_MODEL * (j + 1)], w_ref[j])
        dx, dw = _rms_bwd(h0_ref[...], nw_ref[...], du)
        dw_ref[0:1, :] += dw
        dh0_ref[...] = dh1_ref[...] + dx

        @pl.when(i == n_i - 1)
        def _():
            push.wait()

    row = pl.BlockSpec((tm, D_MODEL), lambda i: (i, 0))
    vec = pl.BlockSpec((1, D_MODEL), lambda i: (0, 0))
    return pl.pallas_call(
        body, name="in_proj_bwd", grid=(n_i,),
        in_specs=[pl.BlockSpec((tm, N_DEV * D_MODEL), lambda i: (i, 0)),
                  pl.BlockSpec((N_DEV, D_MODEL, D_MODEL), lambda i: (0, 0, 0), pipeline_mode=pl.Buffered(1)),
                  row, vec, row] + [ANY] * n_s2,
        out_specs=[row, pl.BlockSpec((8, D_MODEL), lambda i: (0, 0))] + [ANY] * n_s2,
        out_shape=[jax.ShapeDtypeStruct((rows, D_MODEL), F32), jax.ShapeDtypeStruct((8, D_MODEL), F32)] + S2_SHAPES,
        scratch_shapes=_push_sems(n_s2),
        compiler_params=pltpu.CompilerParams(dimension_semantics=("arbitrary",),
                                             vmem_limit_bytes=VMEM_LIMIT, has_side_effects=True),
    )(dproj, win_g, h0, norm_w, dh1, *s2_grads)


def _adamw(g_slots, w, m, v):
    slots, rows, cols = g_slots.shape
    tr = rows
    for cand in (256, 128, 64, 32, 16, 8):
        if rows % cand == 0 and rows > cand:
            tr = cand
            break

    def body(g_ref, w_ref, m_ref, v_ref, go_ref, d_ref, mo_ref, vo_ref):
        g = g_ref[0].astype(F32)
        for s in range(1, slots):
            g = g + g_ref[s].astype(F32)
        m2 = ADAM_B1 * m_ref[...] + (1.0 - ADAM_B1) * g
        v2 = ADAM_B2 * v_ref[...] + (1.0 - ADAM_B2) * (g * g)
        m_hat = m2 / (1.0 - ADAM_B1 ** ADAM_STEP)
        v_hat = v2 / (1.0 - ADAM_B2 ** ADAM_STEP)
        go_ref[...] = g
        d_ref[...] = -ADAM_LR * (m_hat / (jnp.sqrt(v_hat) + ADAM_EPS) + ADAM_WD * w_ref[...])
        mo_ref[...] = m2
        vo_ref[...] = v2

    blk = pl.BlockSpec((tr, cols), lambda i: (i, 0))
    shape = jax.ShapeDtypeStruct((rows, cols), F32)
    return pl.pallas_call(
        body, name="adamw", grid=(rows // tr,),
        in_specs=[pl.BlockSpec((slots, tr, cols), lambda i: (0, i, 0)), blk, blk, blk],
        out_specs=[blk] * 4, out_shape=[shape] * 4,
        compiler_params=_cparams(("parallel",)),
    )(g_slots, w, m, v)


def _sum_slots(packs):
    slots, rows, cols = packs.shape

    def body(p_ref, o_ref):
        acc = p_ref[0]
        for s in range(1, slots):
            acc = acc + p_ref[s]
        o_ref[...] = acc

    return pl.pallas_call(
        body, name="sum_slots", out_shape=jax.ShapeDtypeStruct((rows, cols), F32),
        compiler_params=pltpu.CompilerParams(vmem_limit_bytes=VMEM_LIMIT),
    )(packs)


def _gather_small(small):
    shapes = [jax.ShapeDtypeStruct((N_DEV,) + small.shape, F32)]
    return _push_call("gather_small", [small], shapes,
                      lambda ins, p: list(ins), lambda outs, s: [r.at[s] for r in outs])[0]


def _share_pack(pack):
    shapes = [jax.ShapeDtypeStruct((N_DEV,) + pack.shape, F32)]
    return _push_call("share_pack", [pack], shapes,
                      lambda ins, p: list(ins), lambda outs, s: [r.at[s] for r in outs])[0]


PACK_MIX_NORM, PACK_CONV_W, PACK_CONV_B, PACK_BA, PACK_BX, PACK_LAM = 0, 8, 12, 13, 14, 15
PACK_FFN_NORM, PACK_SQ_ERR, PACK_FINAL_NORM, PACK_META = 16, 24, 25, 32


def kernel(x, meta_tokens, mix_norm_w, w_in, conv_w, conv_b, lru_wa, lru_ba, lru_wx, lru_bx, lru_lambda, w_branch_ret, w_branch_lru, w_out, ffn_norm_w, w_ffn_in, w_ffn_out, final_norm_w, loss_target, m_meta_tokens, m_mix_norm_w, m_w_in, m_conv_w, m_conv_b, m_lru_wa, m_lru_ba, m_lru_wx, m_lru_bx, m_lru_lambda, m_w_branch_ret, m_w_branch_lru, m_w_out, m_ffn_norm_w, m_w_ffn_in, m_w_ffn_out, m_final_norm_w, v_meta_tokens, v_mix_norm_w, v_w_in, v_conv_w, v_conv_b, v_lru_wa, v_lru_ba, v_lru_wx, v_lru_bx, v_lru_lambda, v_w_branch_ret, v_w_branch_lru, v_w_out, v_ffn_norm_w, v_w_ffn_in, v_w_ffn_out, v_final_norm_w):
    me = _my_index()
    pad4 = ((0, 4), (0, 0))
    fw = final_norm_w.reshape(1, D_MODEL)

    small = jnp.concatenate([meta_tokens, jnp.pad(conv_w[0], pad4)], axis=0)
    small_g = _gather_small(small)
    meta_full = small_g[:, :N_META].transpose(1, 0, 2).reshape(N_META, D_MODEL)
    conv_w_full = small_g[:, N_META:N_META + 4].transpose(1, 0, 2).reshape(4, D_MODEL)
    mixer_shards = [w_branch_ret[0].astype(BF16), w_branch_lru[0].astype(BF16), w_out[0].astype(BF16),
                    lru_wa[0].astype(BF16), lru_wx[0].astype(BF16)]
    wfi_shard = jnp.pad(w_ffn_in[0].astype(BF16), ((0, 0), (0, FFN_GROUP - FFN_SHARD)))
    own_slot = lambda ins, p: list(ins)
    part_of_owner = lambda ins, p: [r.at[p] for r in ins]

    rows = x.shape[1] + CHUNK
    h0 = jnp.concatenate([jnp.zeros((PAD_ROWS, D_MODEL), F32), meta_full, x[0]], axis=0)
    tgt = jnp.concatenate([jnp.zeros((CHUNK, D_MODEL), F32), loss_target[0]], axis=0)
    cos2, sin2 = _rope_tables(rows)
    dec = _retention_consts()

    proj, u, win_g = _in_proj(h0, mix_norm_w, w_in[0].astype(BF16), me.astype(jnp.int32).reshape(1))
    o, zr, states, wbr_g, wbl_g, wout_g, wa_g, wx_g = _retention_fwd(
        proj, cos2, sin2, dec, _mixer_weights_ride(mixer_shards))
    wbr, wbl, wout = (t.reshape(D_MODEL, D_MODEL) for t in (wbr_g, wbl_g, wout_g))
    wa_g, wx_g = _from_owners(wa_g), _from_owners(wx_g)
    gather_wfi = _Ride([wfi_shard], [jax.ShapeDtypeStruct((N_DEV, D_MODEL, FFN_GROUP), BF16)],
                       own_slot, _slot_of_sender)
    hs, zl, wfi_g = _lru_fwd(proj, conv_w_full, conv_b, lru_ba, lru_bx, lru_lambda, wa_g, wx_g, gather_wfi)
    h1, yr, yl, mixed, wfo_g = _mix_fwd(zr, zl, proj, h0, wbr, wbl, wout, _wfo_ride(w_ffn_out[0].astype(BF16)))
    u2, g, up, act, dh2, red = _ffn_fwd_loss(h1, ffn_norm_w, wfi_g, wfo_g, fw, tgt)

    d_wfo = _wgrad(act, dh2, FFN_GROUP, D_MODEL, BF16)[:, 0]
    dgu, dh1, dw_ffn_norm = _ffn_bwd(dh2, g, up, h1, ffn_norm_w, wfi_g, wfo_g)
    d_wfi = _wgrad(u2, dgu, D_MODEL, FFN_GROUP, BF16, b_halves=True)[0]
    d_wout = _wgrad(mixed, dh1, D_MODEL, D_MODEL, BF16)[0, 0]
    dyr, dyl, dproj, dzr, dzl = _mix_bwd(dh1, yr, yl, proj, wbr, wbl, wout)
    d_wbr = _wgrad(zr, dyr, D_MODEL, D_MODEL, BF16)[0, 0]
    d_wbl = _wgrad(zl, dyl, D_MODEL, D_MODEL, BF16)[0, 0]
    dproj, d_wa, d_wx, lru_small, r_fi, r_fo = _lru_bwd(
        dzl, hs, proj, dproj, conv_w_full, conv_b, lru_ba, lru_bx, lru_lambda, wa_g, wx_g, [d_wfi, d_wfo])
    mix_shape = jax.ShapeDtypeStruct((N_DEV, D_MODEL // N_DEV, D_MODEL), BF16)
    scatter_mix = _Ride([t.reshape(mix_shape.shape) for t in (d_wbr, d_wbl, d_wout)], [mix_shape] * 3,
                        part_of_owner, _slot_of_sender)
    dproj, r_br, r_bl, r_out = _retention_bwd(dzr, o, proj, states, cos2, sin2, dec, dproj, scatter_mix)
    d_win = _wgrad(u, dproj, D_MODEL, D_MODEL, BF16)[0]
    dh0, dw_mix_norm, r_in, r_wa, r_wx = _in_proj_bwd(dproj, win_g, h0, mix_norm_w, dh1,
                                                      [d_win, _by_owner(d_wa), _by_owner(d_wx)])
    grad_x = dh0[CHUNK:]

    pack = jnp.concatenate([dw_mix_norm, lru_small, dw_ffn_norm, red, dh0[PAD_ROWS:CHUNK]], axis=0)
    small_sum = _sum_slots(_share_pack(pack))
    loss = (0.5 / D_MODEL) * jnp.sum(small_sum[PACK_SQ_ERR])

    def big_update(slots, w, m, v):
        shape = w.shape
        w2, m2, v2 = (t.reshape(slots.shape[1:]) for t in (w, m, v))
        return [t.reshape(shape) for t in _adamw(slots, w2, m2, v2)]

    res = {}
    res["w_in"] = big_update(r_in, w_in, m_w_in, v_w_in)
    res["w_branch_ret"] = big_update(r_br, w_branch_ret, m_w_branch_ret, v_w_branch_ret)
    res["w_branch_lru"] = big_update(r_bl, w_branch_lru, m_w_branch_lru, v_w_branch_lru)
    res["w_out"] = big_update(r_out, w_out, m_w_out, v_w_out)
    res["w_ffn_in"] = big_update(r_fi[:, :, :FFN_SHARD], w_ffn_in, m_w_ffn_in, v_w_ffn_in)
    res["w_ffn_out"] = big_update(r_fo, w_ffn_out, m_w_ffn_out, v_w_ffn_out)
    res["lru_wa"] = big_update(r_wa.reshape(N_DEV, LRU_BLOCKS * LRU_ROWS, LRU_BLOCK), lru_wa, m_lru_wa, v_lru_wa)
    res["lru_wx"] = big_update(r_wx.reshape(N_DEV, LRU_BLOCKS * LRU_ROWS, LRU_BLOCK), lru_wx, m_lru_wx, v_lru_wx)

    col = me * HEAD_DIM
    g_meta = lax.dynamic_slice(small_sum, (PACK_META, col), (N_META, HEAD_DIM))
    g_conv = lax.dynamic_slice(small_sum, (PACK_CONV_W, col), (8, HEAD_DIM))
    small_names = ["mix_norm_w", "conv_b", "lru_ba", "lru_bx", "lru_lambda", "ffn_norm_w", "final_norm_w"]
    small_rows = [PACK_MIX_NORM, PACK_CONV_B, PACK_BA, PACK_BX, PACK_LAM, PACK_FFN_NORM, PACK_FINAL_NORM]
    small_w = [mix_norm_w, conv_b, lru_ba, lru_bx, lru_lambda, ffn_norm_w, fw]
    small_m = [m_mix_norm_w, m_conv_b, m_lru_ba, m_lru_bx, m_lru_lambda, m_ffn_norm_w, m_final_norm_w.reshape(1, -1)]
    small_v = [v_mix_norm_w, v_conv_b, v_lru_ba, v_lru_bx, v_lru_lambda, v_ffn_norm_w, v_final_norm_w.reshape(1, -1)]

    def pack_small(vec_list, meta_t, conv_t):
        return jnp.concatenate([t.reshape(8, HEAD_DIM) for t in vec_list] + [meta_t, jnp.pad(conv_t[0], pad4)], axis=0)

    g_small = jnp.concatenate([small_sum[r].reshape(8, HEAD_DIM) for r in small_rows] + [g_meta, g_conv], axis=0)
    outs_small = _adamw(g_small[None], pack_small(small_w, meta_tokens, conv_w),
                        pack_small(small_m, m_meta_tokens, m_conv_w), pack_small(small_v, v_meta_tokens, v_conv_w))
    for idx, name in enumerate(small_names):
        shape = final_norm_w.shape if name == "final_norm_w" else (1, D_MODEL)
        res[name] = [t[8 * idx:8 * idx + 8].reshape(shape) for t in outs_small]
    res["meta_tokens"] = [t[56:72] for t in outs_small]
    res["conv_w"] = [t[72:76].reshape(1, 4, HEAD_DIM) for t in outs_small]

    order = ["meta_tokens", "mix_norm_w", "w_in", "conv_w", "conv_b", "lru_wa", "lru_ba", "lru_wx", "lru_bx",
             "lru_lambda", "w_branch_ret", "w_branch_lru", "w_out", "ffn_norm_w", "w_ffn_in", "w_ffn_out",
             "final_norm_w"]
    out = [loss, grad_x[None]]
    for kind in range(4):
        out += [res[name][kind] for name in order]
    return tuple(out)
```

```python
import functools

import numpy as np
import jax
import jax.numpy as jnp
from jax import lax
from jax.experimental import pallas as pl
from jax.experimental.pallas import tpu as pltpu

F32 = jnp.float32
BF16 = jnp.bfloat16

D_MODEL = 1024
N_META = 16
CHUNK = 128
PAD_ROWS = CHUNK - N_META
HEADS = 8
HEAD_DIM = 128
ROPE_BASE = 10000.0
QK_SCALE = HEAD_DIM ** -0.5
LRU_BLOCKS = 4
LRU_BLOCK = 256
LRU_C = 8.0
FFN_HIDDEN = 2816
N_DEV = 8
FFN_SHARD = 2 * FFN_HIDDEN // N_DEV
FFN_GROUP = 768
FFN_GROUPS = 4
FFN_OUT_SHARD = FFN_HIDDEN // N_DEV
NORM_EPS = 1e-6

ADAM_LR = 0.001
ADAM_B1 = 0.9
ADAM_B2 = 0.999
ADAM_EPS = 1e-08
ADAM_WD = 0.01
ADAM_STEP = 10

VMEM_LIMIT = 56 * 1024 * 1024
MESH_ID = pl.DeviceIdType.MESH
ANY = pl.BlockSpec(memory_space=pl.ANY)


def _cparams(sem):
    return pltpu.CompilerParams(dimension_semantics=sem, vmem_limit_bytes=VMEM_LIMIT)


def _tile(rows, cap):
    t = cap - cap % 64
    while rows % t:
        t -= 64
    return t


def _dot(a, b):
    return jnp.dot(a, b, preferred_element_type=F32)


def _dot_nt(a, b):
    return lax.dot_general(a, b, (((1,), (1,)), ((), ())), preferred_element_type=F32)


def _dot_tn(a, b):
    return lax.dot_general(a, b, (((0,), (0,)), ((), ())), preferred_element_type=F32)


def _sigmoid(x):
    return 0.5 * jnp.tanh(0.5 * x) + 0.5


def _gelu_parts(x):
    k = 0.7978845608028654
    inner = k * (x + 0.044715 * x * x * x)
    t = jnp.tanh(inner)
    g = 0.5 * x * (1.0 + t)
    dg = 0.5 * (1.0 + t) + 0.5 * x * (1.0 - t * t) * k * (1.0 + 3.0 * 0.044715 * x * x)
    return g, dg


def _rot(x, cos2, sin2):
    return x * cos2 + pltpu.roll(x, HEAD_DIM // 2, 1) * sin2


def _rot_t(dx, cos2, sin2):
    return dx * cos2 - pltpu.roll(dx, HEAD_DIM // 2, 1) * sin2


def _rms_bwd(x, w, dy):
    rs = lax.rsqrt(jnp.mean(x * x, axis=-1, keepdims=True) + NORM_EPS)
    nh = x * rs
    dw = jnp.sum(dy * nh, axis=0, keepdims=True)
    dn = dy * w
    dx = rs * (dn - nh * jnp.mean(dn * nh, axis=-1, keepdims=True))
    return dx, dw


def _retention_consts():
    h = jnp.arange(HEADS, dtype=F32)
    log_g = jnp.log(1.0 - 2.0 ** (-5.0 - h))
    idx = jnp.arange(CHUNK, dtype=F32)
    diff = idx[:, None] - idx[None, :]
    intra = jnp.where(diff[None] >= 0, jnp.exp(jnp.maximum(diff, 0.0)[None] * log_g[:, None, None]), 0.0)
    q_decay = jnp.exp((idx + 1.0)[:, None] * log_g[None, :])
    k_decay = jnp.exp((CHUNK - 1.0 - idx)[:, None] * log_g[None, :])
    chunk_decay = jnp.exp(CHUNK * log_g)
    shape = (HEADS, CHUNK, CHUNK)
    qd = jnp.broadcast_to(q_decay.T[:, :, None], shape)
    kd = jnp.broadcast_to(k_decay.T[:, :, None], shape)
    cd = jnp.broadcast_to(chunk_decay[:, None, None], shape)
    return jnp.stack([intra, qd, kd, cd])


def _rope_tables(rows):
    pos = jnp.maximum(jnp.arange(rows) - PAD_ROWS, 0).astype(F32)
    inv_freq = ROPE_BASE ** (-jnp.arange(0, HEAD_DIM, 2, dtype=F32) / HEAD_DIM)
    ang = pos[:, None] * inv_freq[None, :]
    cos, sin = jnp.cos(ang), jnp.sin(ang)
    return jnp.concatenate([cos, cos], axis=1), jnp.concatenate([-sin, sin], axis=1)


def _my_index():
    return 4 * lax.axis_index("x") + 2 * lax.axis_index("y") + lax.axis_index("c")


def _peer(k):
    x, y, c = lax.axis_index("x"), lax.axis_index("y"), lax.axis_index("c")
    px = 1 - x if k & 4 else x
    py = 1 - y if k & 2 else y
    pc = 1 - c if k & 1 else c
    return (px, py, pc), 4 * px + 2 * py + pc


def _push_sems(n_arr):
    n_rem = (N_DEV - 1) * n_arr
    return [pltpu.SemaphoreType.DMA((n_rem,)), pltpu.SemaphoreType.DMA((n_rem,)), pltpu.SemaphoreType.DMA((n_arr,))]


class _Push:
    def __init__(self, send_part, land_slot, sems, n_arr):
        self.send_part, self.land_slot, self.n_arr = send_part, land_slot, n_arr
        self.send_sems, self.recv_sems, self.loc_sems = sems

    def _remote(self, k, a, src, dst, pos):
        idx = (k - 1) * self.n_arr + a
        return pltpu.make_async_remote_copy(src_ref=src, dst_ref=dst, send_sem=self.send_sems.at[idx],
                                            recv_sem=self.recv_sems.at[idx], device_id=pos, device_id_type=MESH_ID)

    def _outgoing(self):
        me = _my_index()
        land = self.land_slot(me)
        remote = []
        for k in range(1, N_DEV):
            pos, p = _peer(k)
            src = self.send_part(p)
            remote += [self._remote(k, a, src[a], land[a], pos) for a in range(self.n_arr)]
        own = self.send_part(me)
        local = [pltpu.make_async_copy(own[a], land[a], self.loc_sems.at[a]) for a in range(self.n_arr)]
        return remote, local

    def start(self):
        remote, local = self._outgoing()
        for cp in remote + local:
            cp.start()

    def wait_recv_from(self, k):
        own = self.send_part(_my_index())
        pos, p = _peer(k)
        land = self.land_slot(p)
        for a in range(self.n_arr):
            self._remote(k, a, own[a], land[a], pos).wait_recv()

    def wait_sends(self):
        remote, local = self._outgoing()
        for cp in remote:
            cp.wait_send()
        for cp in local:
            cp.wait()

    def wait(self):
        for k in range(1, N_DEV):
            self.wait_recv_from(k)
        self.wait_sends()


DIRECT = (1, 2, 4, 6)
RELAYED = (2, 4, 6)


def _gather_by_chip_sems(n_arr):
    direct, relayed = len(DIRECT) * n_arr, len(RELAYED) * n_arr
    return [pltpu.SemaphoreType.DMA((direct,)), pltpu.SemaphoreType.DMA((direct,)),
            pltpu.SemaphoreType.DMA((relayed,)), pltpu.SemaphoreType.DMA((relayed,)), pltpu.SemaphoreType.DMA((n_arr,))]


class _GatherByChip:
    def __init__(self, srcs, land_slot, sems, n_arr):
        self.srcs, self.land_slot, self.n_arr = srcs, land_slot, n_arr
        self.send_sems, self.recv_sems, self.relay_send_sems, self.relay_recv_sems, self.loc_sems = sems

    def _direct(self, k, a, slot):
        idx = DIRECT.index(k) * self.n_arr + a
        return pltpu.make_async_remote_copy(src_ref=self.srcs[a], dst_ref=self.land_slot(slot)[a],
                                            send_sem=self.send_sems.at[idx], recv_sem=self.recv_sems.at[idx],
                                            device_id=_peer(k)[0], device_id_type=MESH_ID)

    def _relay(self, q, a, slot):
        idx = RELAYED.index(q) * self.n_arr + a
        block = self.land_slot(slot)[a]
        return pltpu.make_async_remote_copy(src_ref=block, dst_ref=block, send_sem=self.relay_send_sems.at[idx],
                                            recv_sem=self.relay_recv_sems.at[idx], device_id=_peer(1)[0],
                                            device_id_type=MESH_ID)

    def _own(self, a):
        return pltpu.make_async_copy(self.srcs[a], self.land_slot(_my_index())[a], self.loc_sems.at[a])

    def start(self):
        me = _my_index()
        for k in DIRECT:
            for a in range(self.n_arr):
                self._direct(k, a, me).start()
        for a in range(self.n_arr):
            self._own(a).start()

    def relay(self):
        for q in RELAYED:
            p = _peer(q)[1]
            for a in range(self.n_arr):
                self._direct(q, a, p).wait_recv()
                self._relay(q, a, p).start()

    def wait(self):
        me = _my_index()
        for a in range(self.n_arr):
            self._direct(1, a, _peer(1)[1]).wait_recv()
        for q in RELAYED:
            for a in range(self.n_arr):
                self._relay(q, a, _peer(q + 1)[1]).wait_recv()
        for k in DIRECT:
            for a in range(self.n_arr):
                self._direct(k, a, me).wait_send()
        for q in RELAYED:
            for a in range(self.n_arr):
                self._relay(q, a, _peer(q)[1]).wait_send()
        for a in range(self.n_arr):
            self._own(a).wait()


class _Ride:
    def __init__(self, arrays, out_shapes, send_part, land_slot, zero_dsts=None, zero_shape=None, n_zero=0,
                 gather_by_chip=False):
        self.arrays, self.out_shapes = list(arrays), list(out_shapes)
        self.send_part, self.land_slot, self.n = send_part, land_slot, len(arrays)
        self.zero_dsts, self.zero_shape, self.n_zero = zero_dsts, zero_shape, n_zero
        self.gather_by_chip = gather_by_chip

    def specs(self):
        return [ANY] * self.n

    def scratch(self):
        extra = [pltpu.SemaphoreType.DMA((self.n_zero,)), pltpu.VMEM(self.zero_shape, BF16)] if self.n_zero else []
        sems = _gather_by_chip_sems(self.n) if self.gather_by_chip else _push_sems(self.n)
        return sems + extra

    def push(self, in_refs, out_refs, scratch):
        ride = self
        n_sems = 5 if self.gather_by_chip else 3
        land = lambda s: ride.land_slot(out_refs, s)
        if self.gather_by_chip:
            push = _GatherByChip(list(in_refs), land, tuple(scratch[:n_sems]), self.n)
        else:
            push = _Push(lambda p: ride.send_part(in_refs, p), land, tuple(scratch[:n_sems]), self.n)

        class Both:
            def _fills(self):
                if not ride.n_zero:
                    return []
                zsems, zbuf = scratch[n_sems], scratch[n_sems + 1]
                return [pltpu.make_async_copy(zbuf, dst, zsems.at[z]) for z, dst in enumerate(ride.zero_dsts(out_refs))]

            def start(self):
                push.start()
                if ride.n_zero:
                    scratch[n_sems + 1][...] = jnp.zeros(ride.zero_shape, BF16)
                for cp in self._fills():
                    cp.start()

            def relay(self):
                if ride.gather_by_chip:
                    push.relay()

            def wait(self):
                push.wait()
                for cp in self._fills():
                    cp.wait()

        return Both()


def _slot_of_sender(out_refs, s):
    return [r.at[s] for r in out_refs]


def _push_call(name, arrays, out_shapes, send_part, land_slot):
    n_arr = len(arrays)

    def body(*refs):
        ins, outs, sems = refs[:n_arr], refs[n_arr:2 * n_arr], refs[2 * n_arr:]
        push = _Push(lambda p: send_part(ins, p), lambda s: land_slot(outs, s), sems, n_arr)
        push.start()
        push.wait()

    return pl.pallas_call(
        body, name=name, in_specs=[ANY] * n_arr, out_specs=[ANY] * n_arr, out_shape=out_shapes,
        scratch_shapes=_push_sems(n_arr), compiler_params=pltpu.CompilerParams(has_side_effects=True),
    )(*arrays)


LRU_ROWS = LRU_BLOCK // N_DEV
FFN_PAD_ROWS = FFN_GROUP - 2 * FFN_OUT_SHARD


def _half_rows(d):
    return pl.ds(pl.multiple_of((d % 2) * FFN_OUT_SHARD, 16), FFN_OUT_SHARD)


MIXER_SHAPES = [
    jax.ShapeDtypeStruct((N_DEV, D_MODEL // N_DEV, D_MODEL), BF16),
    jax.ShapeDtypeStruct((N_DEV, D_MODEL // N_DEV, D_MODEL), BF16),
    jax.ShapeDtypeStruct((N_DEV, D_MODEL // N_DEV, D_MODEL), BF16),
    jax.ShapeDtypeStruct((N_DEV, LRU_BLOCKS, LRU_ROWS, LRU_BLOCK), BF16),
    jax.ShapeDtypeStruct((N_DEV, LRU_BLOCKS, LRU_ROWS, LRU_BLOCK), BF16),
]


def _by_owner(t):
    return t.reshape(LRU_BLOCKS, N_DEV, LRU_ROWS, LRU_BLOCK).transpose(1, 0, 2, 3)


def _from_owners(t):
    return t.transpose(1, 0, 2, 3).reshape(LRU_BLOCKS, LRU_BLOCK, LRU_BLOCK)


def _mixer_weights_ride(shards):
    return _Ride(shards, MIXER_SHAPES, lambda ins, p: list(ins), _slot_of_sender, gather_by_chip=True)


def _wfo_ride(shard):
    zero_dsts = lambda outs: [outs[0].at[g, pl.ds(2 * FFN_OUT_SHARD, FFN_PAD_ROWS), :] for g in range(FFN_GROUPS)]
    return _Ride([shard], [jax.ShapeDtypeStruct((FFN_GROUPS, FFN_GROUP, D_MODEL), BF16)], lambda ins, p: list(ins),
                 lambda outs, d: [outs[0].at[d // 2, _half_rows(d), :]], zero_dsts, (FFN_PAD_ROWS, D_MODEL), FFN_GROUPS,
                 gather_by_chip=True)


def _arrival_rank_to_relation(jj):
    return jnp.where(jj == 3, 4, jnp.where(jj == 4, 3, jj))


def _in_proj(h0, norm_w, win_shard, me_arr):
    rows = h0.shape[0]
    tm = _tile(rows, 1664)
    n_i = rows // tm

    direct, relayed = DIRECT, RELAYED

    def body(me_ref, h_ref, nw_ref, wsh_ref, proj_ref, u_ref, wing_ref, u_all, wbuf, copy_sem,
             send_sems, recv_sems, relay_send_sems, relay_recv_sems, own_sem):
        del me_ref
        jj, i = pl.program_id(0), pl.program_id(1)
        me = _my_index()
        sibling = _peer(1)[0]

        def direct_copy(k, slot):
            n = direct.index(k)
            return pltpu.make_async_remote_copy(src_ref=wsh_ref, dst_ref=wing_ref.at[slot], send_sem=send_sems.at[n],
                                                recv_sem=recv_sems.at[n], device_id=_peer(k)[0], device_id_type=MESH_ID)

        def relay_copy(q, slot):
            n = relayed.index(q)
            return pltpu.make_async_remote_copy(src_ref=wing_ref.at[slot], dst_ref=wing_ref.at[slot],
                                                send_sem=relay_send_sems.at[n], recv_sem=relay_recv_sems.at[n],
                                                device_id=sibling, device_id_type=MESH_ID)

        own_slot = pltpu.make_async_copy(wsh_ref, wing_ref.at[me], own_sem)

        @pl.when(jnp.logical_and(jj == 0, i == 0))
        def _():
            for k in direct:
                direct_copy(k, me).start()
            own_slot.start()
            own = pltpu.make_async_copy(wsh_ref, wbuf, copy_sem)
            own.start()
            own.wait()

        for k in range(1, N_DEV):
            rank = {3: 4, 4: 3}.get(k, k)

            @pl.when(jnp.logical_and(jj == rank, i == 0))
            def _(k=k):
                p = _peer(k)[1]
                if k in direct:
                    direct_copy(k, p).wait_recv()
                    if k in relayed:
                        relay_copy(k, p).start()
                else:
                    relay_copy(k - 1, p).wait_recv()
                landed = pltpu.make_async_copy(wing_ref.at[p], wbuf, copy_sem)
                landed.start()
                landed.wait()

        rows_i = pl.ds(pl.multiple_of(i * tm, tm), tm)

        @pl.when(jj == 0)
        def _():
            x = h_ref[...]
            rs = lax.rsqrt(jnp.mean(x * x, axis=-1, keepdims=True) + NORM_EPS)
            u = (x * rs * nw_ref[...]).astype(BF16)
            u_all[rows_i, :] = u
            u_ref[...] = u
        proj_ref[...] = _dot(u_all[rows_i, :], wbuf[...]).astype(BF16)

        @pl.when(jnp.logical_and(jj == N_DEV - 1, i == n_i - 1))
        def _():
            for k in direct:
                direct_copy(k, me).wait_send()
            for q in relayed:
                relay_copy(q, _peer(q)[1]).wait_send()
            own_slot.wait()

    first_pass = lambda jj, i: jnp.where(jj == 0, i, n_i - 1)
    grid_spec = pltpu.PrefetchScalarGridSpec(
        num_scalar_prefetch=1, grid=(N_DEV, n_i),
        in_specs=[pl.BlockSpec((tm, D_MODEL), lambda jj, i, me: (first_pass(jj, i), 0)),
                  pl.BlockSpec((1, D_MODEL), lambda jj, i, me: (0, 0)), ANY],
        out_specs=[pl.BlockSpec((tm, D_MODEL), lambda jj, i, me: (i, me[0] ^ _arrival_rank_to_relation(jj))),
                   pl.BlockSpec((tm, D_MODEL), lambda jj, i, me: (first_pass(jj, i), 0)), ANY],
        scratch_shapes=[pltpu.VMEM((rows, D_MODEL), BF16), pltpu.VMEM((D_MODEL, D_MODEL), BF16),
                        pltpu.SemaphoreType.DMA(()),
                        pltpu.SemaphoreType.DMA((len(direct),)), pltpu.SemaphoreType.DMA((len(direct),)),
                        pltpu.SemaphoreType.DMA((len(relayed),)), pltpu.SemaphoreType.DMA((len(relayed),)),
                        pltpu.SemaphoreType.DMA(())])
    return pl.pallas_call(
        body, name="in_proj", grid_spec=grid_spec,
        out_shape=[jax.ShapeDtypeStruct((rows, N_DEV * D_MODEL), BF16),
                   jax.ShapeDtypeStruct((rows, D_MODEL), BF16),
                   jax.ShapeDtypeStruct((N_DEV, D_MODEL, D_MODEL), BF16)],
        compiler_params=pltpu.CompilerParams(dimension_semantics=("arbitrary", "arbitrary"),
                                             vmem_limit_bytes=VMEM_LIMIT, has_side_effects=True),
    )(me_arr, h0, norm_w, win_shard)


def _seg_spec(rows_per_block, seg):
    return pl.BlockSpec((rows_per_block, D_MODEL), lambda n, seg=seg: (n, seg))


def _retention_fwd(proj, cos2, sin2, dec, ride):
    rows = proj.shape[0]
    n_chunks = rows // CHUNK
    n_r = ride.n

    def body(q_ref, k_ref, v_ref, g_ref, c_ref, s_ref, dec_ref, *refs):
        o_ref, zr_ref, st_ref = refs[n_r:n_r + 3]
        state = refs[2 * n_r + 3]
        push = ride.push(refs[:n_r], refs[n_r + 3:2 * n_r + 3], refs[2 * n_r + 4:])

        @pl.when(pl.program_id(0) == 0)
        def _():
            push.start()
            state[...] = jnp.zeros_like(state)
        cos_t, sin_t = c_ref[...], s_ref[...]
        st_ref[...] = state[...]
        outs, gated, new_states = [], [], []
        for h in range(HEADS):
            sl = slice(HEAD_DIM * h, HEAD_DIM * (h + 1))
            qh = _rot(q_ref[:, sl].astype(F32), cos_t, sin_t)
            kh = _rot(k_ref[:, sl].astype(F32), cos_t, sin_t) * QK_SCALE
            qb, kb, vb = qh.astype(BF16), kh.astype(BF16), v_ref[:, sl]
            s = _dot_nt(qb, kb) * dec_ref[0, h]
            st = state[h]
            o = _dot(s.astype(BF16), vb) + _dot(qb, st.astype(BF16)) * dec_ref[1, h]
            new_states.append(st * dec_ref[3, h] + _dot_tn((kh * dec_ref[2, h]).astype(BF16), vb))
            outs.append(o.astype(BF16))
            r = lax.rsqrt(jnp.mean(o * o, axis=-1, keepdims=True) + NORM_EPS)
            g = g_ref[:, sl].astype(F32)
            gated.append((g * _sigmoid(g) * (o * r)).astype(BF16))
        o_ref[...] = jnp.concatenate(outs, axis=1)
        zr_ref[...] = jnp.concatenate(gated, axis=1)
        for h in range(HEADS):
            state[h] = new_states[h]

        @pl.when(pl.program_id(0) == n_chunks // 2)
        def _():
            push.relay()

        @pl.when(pl.program_id(0) == n_chunks - 1)
        def _():
            push.wait()

    tab = pl.BlockSpec((CHUNK, HEAD_DIM), lambda n: (n, 0))
    return pl.pallas_call(
        body, name="retention_fwd", grid=(n_chunks,),
        in_specs=[_seg_spec(CHUNK, 0), _seg_spec(CHUNK, 1), _seg_spec(CHUNK, 2), _seg_spec(CHUNK, 3), tab, tab,
                  pl.BlockSpec((4, HEADS, CHUNK, CHUNK), lambda n: (0, 0, 0, 0))] + ride.specs(),
        out_specs=[pl.BlockSpec((CHUNK, D_MODEL), lambda n: (n, 0)),
                   pl.BlockSpec((CHUNK, D_MODEL), lambda n: (n, 0)),
                   pl.BlockSpec((None, HEADS, HEAD_DIM, HEAD_DIM), lambda n: (n, 0, 0, 0))] + ride.specs(),
        out_shape=[jax.ShapeDtypeStruct((rows, D_MODEL), BF16),
                   jax.ShapeDtypeStruct((rows, D_MODEL), BF16),
                   jax.ShapeDtypeStruct((n_chunks, HEADS, HEAD_DIM, HEAD_DIM), F32)] + ride.out_shapes,
        scratch_shapes=[pltpu.VMEM((HEADS, HEAD_DIM, HEAD_DIM), F32)] + ride.scratch(),
        compiler_params=pltpu.CompilerParams(dimension_semantics=("arbitrary",), vmem_limit_bytes=VMEM_LIMIT,
                                             has_side_effects=True),
    )(proj, proj, proj, proj, cos2, sin2, dec, *ride.arrays)


def _lru_gates(c, ba, bx, lam, wa_ref, wx_ref):
    pre_r, pre_i = [], []
    for g in range(LRU_BLOCKS):
        cg = c[:, LRU_BLOCK * g:LRU_BLOCK * (g + 1)].astype(BF16)
        pre_r.append(_dot(cg, wa_ref[g]))
        pre_i.append(_dot(cg, wx_ref[g]))
    r = _sigmoid(jnp.concatenate(pre_r, axis=1) + ba)
    ig = _sigmoid(jnp.concatenate(pre_i, axis=1) + bx)
    sp = jnp.maximum(-lam, 0.0) + jnp.log(1.0 + jnp.exp(-jnp.abs(lam)))
    log_a = -LRU_C * r * sp
    a = jnp.exp(log_a)
    one_minus_a2 = -jnp.tanh(log_a) * (a * a + 1.0)
    inv_mult = lax.rsqrt(jnp.maximum(one_minus_a2, 1e-30))
    return r, ig, a, one_minus_a2 * inv_mult, inv_mult, sp


def _conv_taps(xbuf, tm, cw_ref, cb_ref):
    c = cb_ref[...] + cw_ref[3:4, :] * xbuf[8:8 + tm, :]
    for back in (1, 2, 3):
        c = c + cw_ref[3 - back:4 - back, :] * xbuf[8 - back:8 - back + tm, :]
    return c


def _lru_fwd(proj, conv_w, conv_b, ba, bx, lam, wa_g, wx_g, ride):
    rows = proj.shape[0]
    tm = _tile(rows, 320)
    n_t = rows // tm
    n_r = ride.n

    def body(x_ref, gt_ref, cw_ref, cb_ref, ba_ref, bx_ref, lam_ref, wa_ref, wx_ref, *refs):
        hs_ref, zl_ref = refs[n_r:n_r + 2]
        xbuf, abuf, ubuf, hcar = refs[2 * n_r + 2:2 * n_r + 6]
        push = ride.push(refs[:n_r], refs[n_r + 2:2 * n_r + 2], refs[2 * n_r + 6:])
        i = pl.program_id(0)

        @pl.when(i == 0)
        def _():
            push.start()
            xbuf[0:8, :] = jnp.zeros((8, D_MODEL), F32)
            hcar[...] = jnp.zeros_like(hcar)

        xbuf[8:8 + tm, :] = x_ref[...].astype(F32)
        c = _conv_taps(xbuf, tm, cw_ref, cb_ref)
        xbuf[0:8, :] = xbuf[tm:tm + 8, :]
        r, ig, a, mult, _, _ = _lru_gates(c, ba_ref[...], bx_ref[...], lam_ref[...], wa_ref, wx_ref)
        row = i * tm + lax.broadcasted_iota(jnp.int32, (tm, 1), 0)
        abuf[...] = a
        ubuf[...] = jnp.where(row >= PAD_ROWS, mult * (ig * c), 0.0)

        sub = lax.broadcasted_iota(jnp.int32, (8, D_MODEL), 0)

        def block(b, carry):
            off = pl.multiple_of(b * 8, 8)
            av, uv = abuf[pl.ds(off, 8), :], ubuf[pl.ds(off, 8), :]
            for s in (1, 2, 4):
                us = jnp.where(sub >= s, pltpu.roll(uv, s, 0), 0.0)
                as_ = jnp.where(sub >= s, pltpu.roll(av, s, 0), 1.0)
                uv = uv + av * us
                av = av * as_
            hv = uv + av * carry
            ubuf[pl.ds(off, 8), :] = hv
            return hv[7:8, :]

        hcar[...] = lax.fori_loop(0, tm // 8, block, hcar[...])
        gl, _ = _gelu_parts(gt_ref[...].astype(F32))
        hs = ubuf[...]
        hs_ref[...] = hs.astype(BF16)
        zl_ref[...] = (gl * hs).astype(BF16)

        @pl.when(i == n_t // 2)
        def _():
            push.relay()

        @pl.when(i == n_t - 1)
        def _():
            push.wait()

    vec = pl.BlockSpec((1, D_MODEL), lambda i: (0, 0))
    mat = pl.BlockSpec((LRU_BLOCKS, LRU_BLOCK, LRU_BLOCK), lambda i: (0, 0, 0))
    row = pl.BlockSpec((tm, D_MODEL), lambda i: (i, 0))
    return pl.pallas_call(
        body, name="lru_fwd", grid=(n_t,),
        in_specs=[_seg_spec(tm, 4), _seg_spec(tm, 5), pl.BlockSpec((4, D_MODEL), lambda i: (0, 0)),
                  vec, vec, vec, vec, mat, mat] + ride.specs(),
        out_specs=[row, row] + ride.specs(),
        out_shape=[jax.ShapeDtypeStruct((rows, D_MODEL), BF16)] * 2 + ride.out_shapes,
        scratch_shapes=[pltpu.VMEM((tm + 8, D_MODEL), F32), pltpu.VMEM((tm, D_MODEL), F32),
                        pltpu.VMEM((tm, D_MODEL), F32), pltpu.VMEM((1, D_MODEL), F32)] + ride.scratch(),
        compiler_params=pltpu.CompilerParams(dimension_semantics=("arbitrary",), vmem_limit_bytes=VMEM_LIMIT,
                                             has_side_effects=True),
    )(proj, proj, conv_w, conv_b, ba, bx, lam, wa_g, wx_g, *ride.arrays)


def _mix_fwd(zr, zl, proj, h0, wbr, wbl, wout, ride):
    rows = h0.shape[0]
    tm = _tile(rows, 640)
    n_t = rows // tm
    n_r = ride.n

    def body(zr_ref, zl_ref, ga_ref, gb_ref, h0_ref, wbr_ref, wbl_ref, wo_ref, *refs):
        h1_ref, yr_ref, yl_ref, mx_ref = refs[n_r:n_r + 4]
        push = ride.push(refs[:n_r], refs[n_r + 4:2 * n_r + 4], refs[2 * n_r + 4:])

        @pl.when(pl.program_id(0) == 0)
        def _():
            push.start()

        yr = _dot(zr_ref[...], wbr_ref[...])
        yl = _dot(zl_ref[...], wbl_ref[...])
        mixed = (_sigmoid(ga_ref[...].astype(F32)) * yr + _sigmoid(gb_ref[...].astype(F32)) * yl).astype(BF16)
        yr_ref[...] = yr.astype(BF16)
        yl_ref[...] = yl.astype(BF16)
        mx_ref[...] = mixed
        h1_ref[...] = h0_ref[...] + _dot(mixed, wo_ref[...])

        @pl.when(pl.program_id(0) == n_t // 2)
        def _():
            push.relay()

        @pl.when(pl.program_id(0) == n_t - 1)
        def _():
            push.wait()

    row = pl.BlockSpec((tm, D_MODEL), lambda i: (i, 0))
    wsp = pl.BlockSpec((D_MODEL, D_MODEL), lambda i: (0, 0))
    return pl.pallas_call(
        body, name="mix_fwd", grid=(n_t,),
        in_specs=[row, row, _seg_spec(tm, 6), _seg_spec(tm, 7), row, wsp, wsp, wsp] + ride.specs(),
        out_specs=[row, row, row, row] + ride.specs(),
        out_shape=[jax.ShapeDtypeStruct((rows, D_MODEL), F32)] + [jax.ShapeDtypeStruct((rows, D_MODEL), BF16)] * 3
        + ride.out_shapes,
        scratch_shapes=ride.scratch(),
        compiler_params=pltpu.CompilerParams(dimension_semantics=("arbitrary",), vmem_limit_bytes=VMEM_LIMIT,
                                             has_side_effects=True),
    )(zr, zl, proj, proj, h0, wbr, wbl, wout, *ride.arrays)


def _ffn_fwd_loss(h1, norm_w, wfi_g, wfo_g, final_w, target):
    rows = h1.shape[0]
    tm = _tile(rows, 320)
    piece = 64
    n_piece = tm // piece

    def body(h1_ref, nw_ref, wfi_ref, wfo_ref, fw_ref, *refs):
        t_refs = refs[:n_piece]
        u2_ref, g_ref, up_ref, act_ref, dh2_ref, red_ref = refs[n_piece:]
        i = pl.program_id(0)

        @pl.when(i == 0)
        def _():
            red_ref[...] = jnp.zeros_like(red_ref)

        x = h1_ref[...]
        rs = lax.rsqrt(jnp.mean(x * x, axis=-1, keepdims=True) + NORM_EPS)
        u2 = (x * rs * nw_ref[...]).astype(BF16)
        u2_ref[...] = u2
        ffn = None
        for d in range(FFN_GROUPS):
            cols = slice(FFN_GROUP * d, FFN_GROUP * (d + 1))
            g = _dot(u2, wfi_ref[d])
            up = _dot(u2, wfi_ref[d + FFN_GROUPS])
            act = (g * _sigmoid(g) * up).astype(BF16)
            g_ref[:, cols] = g.astype(BF16)
            up_ref[:, cols] = up.astype(BF16)
            act_ref[:, cols] = act
            part = _dot(act, wfo_ref[d])
            ffn = part if ffn is None else ffn + part

        h2 = x + ffn
        rs = lax.rsqrt(jnp.mean(h2 * h2, axis=-1, keepdims=True) + NORM_EPS)
        nh = h2 * rs
        fw = fw_ref[...]
        row = i * tm + lax.broadcasted_iota(jnp.int32, (tm, 1), 0)
        tgt = jnp.concatenate([t[...] for t in t_refs], axis=0)
        diff = jnp.where(row >= CHUNK, nh * fw - tgt, 0.0)
        dy = diff * (1.0 / D_MODEL)
        red_ref[0:1, :] += jnp.sum(diff * diff, axis=0, keepdims=True)
        red_ref[1:2, :] += jnp.sum(dy * nh, axis=0, keepdims=True)
        dn = dy * fw
        dh2_ref[...] = rs * (dn - nh * jnp.mean(dn * nh, axis=-1, keepdims=True))

    row = pl.BlockSpec((tm, D_MODEL), lambda i: (i, 0))
    vec = pl.BlockSpec((1, D_MODEL), lambda i: (0, 0))
    hid = pl.BlockSpec((tm, FFN_GROUPS * FFN_GROUP), lambda i: (i, 0))
    hid_shape = jax.ShapeDtypeStruct((rows, FFN_GROUPS * FFN_GROUP), BF16)
    resident = dict(pipeline_mode=pl.Buffered(1))
    head_pieces = CHUNK // piece
    t_specs = [pl.BlockSpec((piece, D_MODEL), lambda i, k=k: (jnp.maximum(i * n_piece + k - head_pieces, 0), 0))
               for k in range(n_piece)]
    return pl.pallas_call(
        body, name="ffn_fwd_loss", grid=(rows // tm,),
        in_specs=[row, vec,
                  pl.BlockSpec((2 * FFN_GROUPS, D_MODEL, FFN_GROUP), lambda i: (0, 0, 0), **resident),
                  pl.BlockSpec((FFN_GROUPS, FFN_GROUP, D_MODEL), lambda i: (0, 0, 0), **resident),
                  vec] + t_specs,
        out_specs=[row, hid, hid, hid, row, pl.BlockSpec((8, D_MODEL), lambda i: (0, 0))],
        out_shape=[jax.ShapeDtypeStruct((rows, D_MODEL), BF16), hid_shape, hid_shape, hid_shape,
                   jax.ShapeDtypeStruct((rows, D_MODEL), F32), jax.ShapeDtypeStruct((8, D_MODEL), F32)],
        compiler_params=_cparams(("arbitrary",)),
    )(h1, norm_w, wfi_g, wfo_g, final_w, *([target] * n_piece))


def _wgrad(a, b, ka, tn, out_dtype, b_halves=False):
    rows = a.shape[0]
    na = a.shape[1] // ka
    tm = _tile(rows, 1664)
    nm = rows // tm
    if b_halves:
        per_half = b.shape[2] // tn
        nb = 2 * per_half
        b_spec = pl.BlockSpec((None, tm, tn), lambda p, q, m: (q // per_half, m, q % per_half))
    else:
        nb = b.shape[1] // tn
        b_spec = pl.BlockSpec((tm, tn), lambda p, q, m: (m, q))

    def body(a_ref, b_ref, o_ref, acc):
        m = pl.program_id(2)

        @pl.when(m == 0)
        def _():
            acc[...] = jnp.zeros_like(acc)

        acc[...] += _dot_tn(a_ref[...].astype(BF16), b_ref[...].astype(BF16))

        @pl.when(m == nm - 1)
        def _():
            o_ref[...] = acc[...].astype(out_dtype)

    return pl.pallas_call(
        body, name="wgrad", grid=(na, nb, nm),
        in_specs=[pl.BlockSpec((tm, ka), lambda p, q, m: (m, p)), b_spec],
        out_specs=pl.BlockSpec((None, None, ka, tn), lambda p, q, m: (p, q, 0, 0)),
        out_shape=jax.ShapeDtypeStruct((na, nb, ka, tn), out_dtype),
        scratch_shapes=[pltpu.VMEM((ka, tn), F32)],
        compiler_params=_cparams(("parallel", "parallel", "arbitrary")),
    )(a, b)


def _ffn_bwd(dh2, g, up, h1, norm_w, wfi_g, wfo_g):
    rows = h1.shape[0]
    tm = _tile(rows, 320)

    def body(dh2_ref, g_ref, up_ref, h1_ref, nw_ref, wfi_ref, wfo_ref, dgu_ref, dh1_ref, dw_ref):
        @pl.when(pl.program_id(0) == 0)
        def _():
            dw_ref[...] = jnp.zeros_like(dw_ref)

        dh2 = dh2_ref[...]
        dh2_b = dh2.astype(BF16)
        du2 = None
        for d in range(FFN_GROUPS):
            cols = slice(FFN_GROUP * d, FFN_GROUP * (d + 1))
            dact = _dot_nt(dh2_b, wfo_ref[d])
            gv, uv = g_ref[:, cols].astype(F32), up_ref[:, cols].astype(F32)
            sg = _sigmoid(gv)
            dg = (dact * uv * (sg * (1.0 + gv * (1.0 - sg)))).astype(BF16)
            dup = (dact * (gv * sg)).astype(BF16)
            dgu_ref[0, :, cols] = dg
            dgu_ref[1, :, cols] = dup
            part = _dot_nt(dg, wfi_ref[d]) + _dot_nt(dup, wfi_ref[d + FFN_GROUPS])
            du2 = part if du2 is None else du2 + part
        dx, dw = _rms_bwd(h1_ref[...], nw_ref[...], du2)
        dw_ref[0:1, :] += dw
        dh1_ref[...] = dh2 + dx

    row = pl.BlockSpec((tm, D_MODEL), lambda i: (i, 0))
    vec = pl.BlockSpec((1, D_MODEL), lambda i: (0, 0))
    hid = pl.BlockSpec((tm, FFN_GROUPS * FFN_GROUP), lambda i: (i, 0))
    resident = dict(pipeline_mode=pl.Buffered(1))
    return pl.pallas_call(
        body, name="ffn_bwd", grid=(rows // tm,),
        in_specs=[row, hid, hid, row, vec,
                  pl.BlockSpec((2 * FFN_GROUPS, D_MODEL, FFN_GROUP), lambda i: (0, 0, 0), **resident),
                  pl.BlockSpec((FFN_GROUPS, FFN_GROUP, D_MODEL), lambda i: (0, 0, 0), **resident)],
        out_specs=[pl.BlockSpec((2, tm, FFN_GROUPS * FFN_GROUP), lambda i: (0, i, 0)), row,
                   pl.BlockSpec((8, D_MODEL), lambda i: (0, 0))],
        out_shape=[jax.ShapeDtypeStruct((2, rows, FFN_GROUPS * FFN_GROUP), BF16),
                   jax.ShapeDtypeStruct((rows, D_MODEL), F32), jax.ShapeDtypeStruct((8, D_MODEL), F32)],
        compiler_params=_cparams(("arbitrary",)),
    )(dh2, g, up, h1, norm_w, wfi_g, wfo_g)


def _mix_bwd(dh1, yr, yl, proj, wbr, wbl, wout):
    rows = dh1.shape[0]
    tm = _tile(rows, 640)

    def body(dh1_ref, yr_ref, yl_ref, ga_ref, gb_ref, wbr_ref, wbl_ref, wo_ref,
             dyr_ref, dyl_ref, dseg_ref, dzr_ref, dzl_ref):
        dmix = _dot_nt(dh1_ref[...].astype(BF16), wo_ref[...])
        sa, sb = _sigmoid(ga_ref[...].astype(F32)), _sigmoid(gb_ref[...].astype(F32))
        dyr = (dmix * sa).astype(BF16)
        dyl = (dmix * sb).astype(BF16)
        dyr_ref[...] = dyr
        dyl_ref[...] = dyl
        dseg_ref[:, 0:D_MODEL] = (dmix * yr_ref[...].astype(F32) * (sa * (1.0 - sa))).astype(BF16)
        dseg_ref[:, D_MODEL:2 * D_MODEL] = (dmix * yl_ref[...].astype(F32) * (sb * (1.0 - sb))).astype(BF16)
        dzr_ref[...] = _dot_nt(dyr, wbr_ref[...]).astype(BF16)
        dzl_ref[...] = _dot_nt(dyl, wbl_ref[...]).astype(BF16)

    row = pl.BlockSpec((tm, D_MODEL), lambda i: (i, 0))
    wsp = pl.BlockSpec((D_MODEL, D_MODEL), lambda i: (0, 0))
    bshape = jax.ShapeDtypeStruct((rows, D_MODEL), BF16)
    return pl.pallas_call(
        body, name="mix_bwd", grid=(rows // tm,),
        in_specs=[row, row, row, _seg_spec(tm, 6), _seg_spec(tm, 7), wsp, wsp, wsp],
        out_specs=[row, row, pl.BlockSpec((tm, 2 * D_MODEL), lambda i: (i, 3)), row, row],
        out_shape=[bshape, bshape, jax.ShapeDtypeStruct((rows, N_DEV * D_MODEL), BF16), bshape, bshape],
        compiler_params=_cparams(("parallel",)),
    )(dh1, yr, yl, proj, proj, wbr, wbl, wout)


S1_SHAPES = [
    jax.ShapeDtypeStruct((N_DEV, D_MODEL, FFN_GROUP), BF16),
    jax.ShapeDtypeStruct((N_DEV, FFN_OUT_SHARD, D_MODEL), BF16),
]


def _s1_parts(ins, p):
    return [ins[0].at[p], ins[1].at[p // 2, _half_rows(p), :]]


def _lru_bwd(dzl, hs, proj, dproj, conv_w, conv_b, ba, bx, lam, wa_g, wx_g, s1_grads):
    rows = dzl.shape[0]
    tm = _tile(rows, 320)
    nt = rows // tm
    t8 = tm // 8
    n_s1 = len(s1_grads)

    def body(dzl_ref, hs_ref, hsp_ref, x_ref, xp_ref, gt_ref, cw_ref, cb_ref, ba_ref, bx_ref, lam_ref,
             wa_ref, wx_ref, dproj_in, *refs):
        del dproj_in
        s1_refs = refs[:n_s1]
        dseg_ref, dwa_ref, dwx_ref, sm_ref = refs[n_s1:n_s1 + 4]
        land_refs = refs[n_s1 + 4:2 * n_s1 + 4]
        xbuf, abuf, bbuf, dbuf, dcbuf, anext, dhcar, send_sems, recv_sems, loc_sems = refs[2 * n_s1 + 4:]
        step = pl.program_id(0)
        i = nt - 1 - step
        push = _Push(lambda p: _s1_parts(s1_refs, p), lambda s: [r.at[s] for r in land_refs],
                     (send_sems, recv_sems, loc_sems), n_s1)

        @pl.when(step == 0)
        def _():
            push.start()
            dwa_ref[...] = jnp.zeros_like(dwa_ref)
            dwx_ref[...] = jnp.zeros_like(dwx_ref)
            sm_ref[...] = jnp.zeros_like(sm_ref)
            anext[...] = jnp.zeros_like(anext)
            dhcar[...] = jnp.zeros_like(dhcar)
            dcbuf[tm:tm + 8, :] = jnp.zeros((8, D_MODEL), F32)

        first = i == 0
        x_prev = jnp.where(first, 0.0, xp_ref[8:16, :].astype(F32))
        x_v = x_ref[...].astype(F32)
        xbuf[0:8, :] = x_prev
        xbuf[8:8 + tm, :] = x_v
        c = _conv_taps(xbuf, tm, cw_ref, cb_ref)
        lam_v = lam_ref[...]
        r, ig, a, mult, inv_mult, sp = _lru_gates(c, ba_ref[...], bx_ref[...], lam_v, wa_ref, wx_ref)
        hs_v = hs_ref[...].astype(F32)
        gl, dgl = _gelu_parts(gt_ref[...].astype(F32))
        dzl_v = dzl_ref[...].astype(F32)
        dseg_ref[:, D_MODEL:2 * D_MODEL] = (dzl_v * hs_v * dgl).astype(BF16)
        dbuf[...] = dzl_v * gl
        abuf[0:tm, :] = a
        abuf[tm:tm + 8, :] = jnp.broadcast_to(anext[...], (8, D_MODEL))
        bbuf[...] = abuf[1:tm + 1, :]

        sub = lax.broadcasted_iota(jnp.int32, (8, D_MODEL), 0)

        def block(k, carry):
            off = pl.multiple_of((t8 - 1 - k) * 8, 8)
            av = bbuf[pl.ds(off, 8), :]
            uv = dbuf[pl.ds(off, 8), :]
            for s in (1, 2, 4):
                us = jnp.where(sub < 8 - s, pltpu.roll(uv, 8 - s, 0), 0.0)
                as_ = jnp.where(sub < 8 - s, pltpu.roll(av, 8 - s, 0), 1.0)
                uv = uv + av * us
                av = av * as_
            hv = uv + av * carry
            dbuf[pl.ds(off, 8), :] = hv
            return hv[0:1, :]

        dhcar[...] = lax.fori_loop(0, t8, block, dhcar[...])
        anext[...] = abuf[0:1, :]
        dh = dbuf[...]

        xbuf[0:8, :] = jnp.where(first, 0.0, hsp_ref[8:16, :].astype(F32))
        xbuf[8:8 + tm, :] = hs_v
        hprev = xbuf[7:7 + tm, :]
        row = i * tm + lax.broadcasted_iota(jnp.int32, (tm, 1), 0)
        duu = jnp.where(row >= PAD_ROWS, dh, 0.0)
        da = dh * hprev
        dmult = duu * ig * c
        di = duu * mult * c
        dc = duu * mult * ig
        dlog_a = da * a - dmult * (a * a) * inv_mult
        dr = dlog_a * (-LRU_C * sp)
        dsp = jnp.sum(dlog_a * (-LRU_C * r), axis=0, keepdims=True)
        dpr = dr * r * (1.0 - r)
        dpi = di * ig * (1.0 - ig)
        dpr_b, dpi_b = dpr.astype(BF16), dpi.astype(BF16)
        dcs = []
        for g in range(LRU_BLOCKS):
            sl = slice(LRU_BLOCK * g, LRU_BLOCK * (g + 1))
            cg = c[:, sl].astype(BF16)
            dwa_ref[g] += _dot_tn(cg, dpr_b[:, sl])
            dwx_ref[g] += _dot_tn(cg, dpi_b[:, sl])
            dcs.append(_dot_nt(dpr_b[:, sl], wa_ref[g]) + _dot_nt(dpi_b[:, sl], wx_ref[g]))
        dc = dc + jnp.concatenate(dcs, axis=1)

        dcbuf[0:tm, :] = dc
        xbuf[8:8 + tm, :] = x_v
        xbuf[0:8, :] = x_prev
        dlin = cw_ref[3:4, :] * dc
        sm_ref[3:4, :] += jnp.sum(dc * xbuf[8:8 + tm, :], axis=0, keepdims=True)
        for back in (1, 2, 3):
            dlin = dlin + cw_ref[3 - back:4 - back, :] * dcbuf[back:back + tm, :]
            sm_ref[3 - back:4 - back, :] += jnp.sum(dc * xbuf[8 - back:8 - back + tm, :], axis=0, keepdims=True)
        dseg_ref[:, 0:D_MODEL] = dlin.astype(BF16)
        dcbuf[tm:tm + 8, :] = dcbuf[0:8, :]
        sm_ref[4:5, :] += jnp.sum(dc, axis=0, keepdims=True)
        sm_ref[5:6, :] += jnp.sum(dpr, axis=0, keepdims=True)
        sm_ref[6:7, :] += jnp.sum(dpi, axis=0, keepdims=True)
        sm_ref[7:8, :] += dsp * (-_sigmoid(-lam_v))

        @pl.when(step == nt - 1)
        def _():
            push.wait()

    rowb = pl.BlockSpec((tm, D_MODEL), lambda s: (nt - 1 - s, 0))
    t16 = tm // 16
    prev8 = pl.BlockSpec((16, D_MODEL), lambda s: (jnp.maximum((nt - 1 - s) * t16 - 1, 0), 0))
    seg = lambda k: pl.BlockSpec((tm, D_MODEL), lambda s, k=k: (nt - 1 - s, k))
    prev8_seg4 = pl.BlockSpec((16, D_MODEL), lambda s: (jnp.maximum((nt - 1 - s) * t16 - 1, 0), 4))
    vec = pl.BlockSpec((1, D_MODEL), lambda s: (0, 0))
    mat = pl.BlockSpec((LRU_BLOCKS, LRU_BLOCK, LRU_BLOCK), lambda s: (0, 0, 0))
    mshape = jax.ShapeDtypeStruct((LRU_BLOCKS, LRU_BLOCK, LRU_BLOCK), F32)
    n_in = 13
    return pl.pallas_call(
        body, name="lru_bwd", grid=(nt,),
        in_specs=[rowb, rowb, prev8, seg(4), prev8_seg4, seg(5), pl.BlockSpec((4, D_MODEL), lambda s: (0, 0)),
                  vec, vec, vec, vec, mat, mat, ANY] + [ANY] * n_s1,
        out_specs=[pl.BlockSpec((tm, 2 * D_MODEL), lambda s: (nt - 1 - s, 2)), mat, mat,
                   pl.BlockSpec((8, D_MODEL), lambda s: (0, 0))] + [ANY] * n_s1,
        out_shape=[jax.ShapeDtypeStruct(dproj.shape, dproj.dtype), mshape, mshape,
                   jax.ShapeDtypeStruct((8, D_MODEL), F32)] + S1_SHAPES,
        input_output_aliases={n_in: 0},
        scratch_shapes=[pltpu.VMEM((tm + 8, D_MODEL), F32), pltpu.VMEM((tm + 8, D_MODEL), F32),
                        pltpu.VMEM((tm, D_MODEL), F32), pltpu.VMEM((tm, D_MODEL), F32),
                        pltpu.VMEM((tm + 8, D_MODEL), F32),
                        pltpu.VMEM((1, D_MODEL), F32), pltpu.VMEM((1, D_MODEL), F32)] + _push_sems(n_s1),
        compiler_params=pltpu.CompilerParams(dimension_semantics=("arbitrary",), vmem_limit_bytes=VMEM_LIMIT,
                                             has_side_effects=True),
    )(dzl, hs, hs, proj, proj, proj, conv_w, conv_b, ba, bx, lam, wa_g, wx_g, dproj, *s1_grads)


def _retention_bwd(dzr, o, proj, states, cos2, sin2, dec, dproj, ride):
    rows = dzr.shape[0]
    n_chunks = rows // CHUNK
    n_r = ride.n

    def body(dzr_ref, o_ref, q_ref, k_ref, v_ref, g_ref, st_ref, c_ref, s_ref, dec_ref, dproj_in, *refs):
        del dproj_in
        dseg_ref = refs[n_r]
        dstate = refs[2 * n_r + 1]
        push = ride.push(refs[:n_r], refs[n_r + 1:2 * n_r + 1], refs[2 * n_r + 2:])

        @pl.when(pl.program_id(0) == 0)
        def _():
            push.start()
            dstate[...] = jnp.zeros_like(dstate)
        cos_t, sin_t = c_ref[...], s_ref[...]
        for h in range(HEADS):
            sl = slice(HEAD_DIM * h, HEAD_DIM * (h + 1))
            o = o_ref[:, sl].astype(F32)
            g = g_ref[:, sl].astype(F32)
            dzr_v = dzr_ref[:, sl].astype(F32)
            sg = _sigmoid(g)
            r = lax.rsqrt(jnp.mean(o * o, axis=-1, keepdims=True) + NORM_EPS)
            on = o * r
            dseg_ref[:, 3 * D_MODEL + HEAD_DIM * h:3 * D_MODEL + HEAD_DIM * (h + 1)] = (
                dzr_v * on * (sg * (1.0 + g * (1.0 - sg)))).astype(BF16)
            don = dzr_v * (g * sg)
            do = r * (don - on * jnp.mean(don * on, axis=-1, keepdims=True))
            dob = do.astype(BF16)

            qh = _rot(q_ref[:, sl].astype(F32), cos_t, sin_t)
            kh = _rot(k_ref[:, sl].astype(F32), cos_t, sin_t) * QK_SCALE
            qb, kb, vb = qh.astype(BF16), kh.astype(BF16), v_ref[:, sl]
            intra, qd, kd, cd = dec_ref[0, h], dec_ref[1, h], dec_ref[2, h], dec_ref[3, h]
            s = (_dot_nt(qb, kb) * intra).astype(BF16)
            ds = (_dot_nt(dob, vb) * intra).astype(BF16)
            st_b = st_ref[h].astype(BF16)
            dst = dstate[h]
            dst_b = dst.astype(BF16)
            dv = _dot_tn(s, dob) + _dot((kh * kd).astype(BF16), dst_b)
            dq = _dot(ds, kb) + _dot_nt(dob, st_b) * qd
            dk = _dot_tn(ds, qb) + _dot_nt(vb, dst_b) * kd
            dstate[h] = dst * cd + _dot_tn((qh * qd).astype(BF16), dob)
            dseg_ref[:, 2 * D_MODEL + HEAD_DIM * h:2 * D_MODEL + HEAD_DIM * (h + 1)] = dv.astype(BF16)
            dseg_ref[:, sl] = _rot_t(dq, cos_t, sin_t).astype(BF16)
            dseg_ref[:, D_MODEL + HEAD_DIM * h:D_MODEL + HEAD_DIM * (h + 1)] = (
                _rot_t(dk, cos_t, sin_t) * QK_SCALE).astype(BF16)

        @pl.when(pl.program_id(0) == n_chunks - 1)
        def _():
            push.wait()

    rev = lambda s: n_chunks - 1 - s
    rowb = pl.BlockSpec((CHUNK, D_MODEL), lambda s: (rev(s), 0))
    seg = lambda k: pl.BlockSpec((CHUNK, D_MODEL), lambda s, k=k: (rev(s), k))
    tab = pl.BlockSpec((CHUNK, HEAD_DIM), lambda s: (rev(s), 0))
    return pl.pallas_call(
        body, name="retention_bwd", grid=(n_chunks,),
        in_specs=[rowb, rowb, seg(0), seg(1), seg(2), seg(3),
                  pl.BlockSpec((None, HEADS, HEAD_DIM, HEAD_DIM), lambda s: (rev(s), 0, 0, 0)), tab, tab,
                  pl.BlockSpec((4, HEADS, CHUNK, CHUNK), lambda s: (0, 0, 0, 0)), ANY] + ride.specs(),
        out_specs=[pl.BlockSpec((CHUNK, 4 * D_MODEL), lambda s: (rev(s), 0))] + ride.specs(),
        out_shape=[jax.ShapeDtypeStruct(dproj.shape, dproj.dtype)] + ride.out_shapes,
        input_output_aliases={10: 0},
        scratch_shapes=[pltpu.VMEM((HEADS, HEAD_DIM, HEAD_DIM), F32)] + ride.scratch(),
        compiler_params=pltpu.CompilerParams(dimension_semantics=("arbitrary",), vmem_limit_bytes=VMEM_LIMIT,
                                             has_side_effects=True),
    )(dzr, o, proj, proj, proj, proj, states, cos2, sin2, dec, dproj, *ride.arrays)


S2_SHAPES = [
    jax.ShapeDtypeStruct((N_DEV, D_MODEL, D_MODEL), BF16),
    jax.ShapeDtypeStruct((N_DEV, LRU_BLOCKS, LRU_ROWS, LRU_BLOCK), F32),
    jax.ShapeDtypeStruct((N_DEV, LRU_BLOCKS, LRU_ROWS, LRU_BLOCK), F32),
]


def _s2_parts(ins, p):
    return [r.at[p] for r in ins]


def _in_proj_bwd(dproj, win_g, h0, norm_w, dh1, s2_grads):
    rows = h0.shape[0]
    tm = _tile(rows, 320)
    n_i = rows // tm
    n_s2 = len(s2_grads)

    def body(dseg_ref, w_ref, h0_ref, nw_ref, dh1_ref, *refs):
        s2_refs = refs[:n_s2]
        dh0_ref, dw_ref = refs[n_s2:n_s2 + 2]
        land_refs = refs[n_s2 + 2:2 * n_s2 + 2]
        send_sems, recv_sems, loc_sems = refs[2 * n_s2 + 2:]
        i = pl.program_id(0)
        push = _Push(lambda p: _s2_parts(s2_refs, p), lambda s: [r.at[s] for r in land_refs],
                     (send_sems, recv_sems, loc_sems), n_s2)

        @pl.when(i == 0)
        def _():
            push.start()
            dw_ref[...] = jnp.zeros_like(dw_ref)

        du = _dot_nt(dseg_ref[:, 0:D_MODEL], w_ref[0])
        for j in range(1, N_DEV):
            du = du + _dot_nt(dseg_ref[:, D_MODEL * j:D_MODEL * (j + 1)], w_ref[j])
        dx, dw = _rms_bwd(h0_ref[...], nw_ref[...], du)
        dw_ref[0:1, :] += dw
        dh0_ref[...] = dh1_ref[...] + dx

        @pl.when(i == n_i - 1)
        def _():
            push.wait()

    row = pl.BlockSpec((tm, D_MODEL), lambda i: (i, 0))
    vec = pl.BlockSpec((1, D_MODEL), lambda i: (0, 0))
    return pl.pallas_call(
        body, name="in_proj_bwd", grid=(n_i,),
        in_specs=[pl.BlockSpec((tm, N_DEV * D_MODEL), lambda i: (i, 0)),
                  pl.BlockSpec((N_DEV, D_MODEL, D_MODEL), lambda i: (0, 0, 0), pipeline_mode=pl.Buffered(1)),
                  row, vec, row] + [ANY] * n_s2,
        out_specs=[row, pl.BlockSpec((8, D_MODEL), lambda i: (0, 0))] + [ANY] * n_s2,
        out_shape=[jax.ShapeDtypeStruct((rows, D_MODEL), F32), jax.ShapeDtypeStruct((8, D_MODEL), F32)] + S2_SHAPES,
        scratch_shapes=_push_sems(n_s2),
        compiler_params=pltpu.CompilerParams(dimension_semantics=("arbitrary",),
                                             vmem_limit_bytes=VMEM_LIMIT, has_side_effects=True),
    )(dproj, win_g, h0, norm_w, dh1, *s2_grads)


def _adamw(g_slots, w, m, v):
    slots, rows, cols = g_slots.shape
    tr = rows
    for cand in (256, 128, 64, 32, 16, 8):
        if rows % cand == 0 and rows > cand:
            tr = cand
            break

    def body(g_ref, w_ref, m_ref, v_ref, go_ref, d_ref, mo_ref, vo_ref):
        g = g_ref[0].astype(F32)
        for s in range(1, slots):
            g = g + g_ref[s].astype(F32)
        m2 = ADAM_B1 * m_ref[...] + (1.0 - ADAM_B1) * g
        v2 = ADAM_B2 * v_ref[...] + (1.0 - ADAM_B2) * (g * g)
        m_hat = m2 / (1.0 - ADAM_B1 ** ADAM_STEP)
        v_hat = v2 / (1.0 - ADAM_B2 ** ADAM_STEP)
        go_ref[...] = g
        d_ref[...] = -ADAM_LR * (m_hat / (jnp.sqrt(v_hat) + ADAM_EPS) + ADAM_WD * w_ref[...])
        mo_ref[...] = m2
        vo_ref[...] = v2

    blk = pl.BlockSpec((tr, cols), lambda i: (i, 0))
    shape = jax.ShapeDtypeStruct((rows, cols), F32)
    return pl.pallas_call(
        body, name="adamw", grid=(rows // tr,),
        in_specs=[pl.BlockSpec((slots, tr, cols), lambda i: (0, i, 0)), blk, blk, blk],
        out_specs=[blk] * 4, out_shape=[shape] * 4,
        compiler_params=_cparams(("parallel",)),
    )(g_slots, w, m, v)


def _sum_slots(packs):
    slots, rows, cols = packs.shape

    def body(p_ref, o_ref):
        acc = p_ref[0]
        for s in range(1, slots):
            acc = acc + p_ref[s]
        o_ref[...] = acc

    return pl.pallas_call(
        body, name="sum_slots", out_shape=jax.ShapeDtypeStruct((rows, cols), F32),
        compiler_params=pltpu.CompilerParams(vmem_limit_bytes=VMEM_LIMIT),
    )(packs)


def _gather_small(small):
    shapes = [jax.ShapeDtypeStruct((N_DEV,) + small.shape, F32)]
    return _push_call("gather_small", [small], shapes,
                      lambda ins, p: list(ins), lambda outs, s: [r.at[s] for r in outs])[0]


def _share_pack(pack):
    shapes = [jax.ShapeDtypeStruct((N_DEV,) + pack.shape, F32)]
    return _push_call("share_pack", [pack], shapes,
                      lambda ins, p: list(ins), lambda outs, s: [r.at[s] for r in outs])[0]


PACK_MIX_NORM, PACK_CONV_W, PACK_CONV_B, PACK_BA, PACK_BX, PACK_LAM = 0, 8, 12, 13, 14, 15
PACK_FFN_NORM, PACK_SQ_ERR, PACK_FINAL_NORM, PACK_META = 16, 24, 25, 32


def kernel(x, meta_tokens, mix_norm_w, w_in, conv_w, conv_b, lru_wa, lru_ba, lru_wx, lru_bx, lru_lambda, w_branch_ret, w_branch_lru, w_out, ffn_norm_w, w_ffn_in, w_ffn_out, final_norm_w, loss_target, m_meta_tokens, m_mix_norm_w, m_w_in, m_conv_w, m_conv_b, m_lru_wa, m_lru_ba, m_lru_wx, m_lru_bx, m_lru_lambda, m_w_branch_ret, m_w_branch_lru, m_w_out, m_ffn_norm_w, m_w_ffn_in, m_w_ffn_out, m_final_norm_w, v_meta_tokens, v_mix_norm_w, v_w_in, v_conv_w, v_conv_b, v_lru_wa, v_lru_ba, v_lru_wx, v_lru_bx, v_lru_lambda, v_w_branch_ret, v_w_branch_lru, v_w_out, v_ffn_norm_w, v_w_ffn_in, v_w_ffn_out, v_final_norm_w):
    me = _my_index()
    pad4 = ((0, 4), (0, 0))
    fw = final_norm_w.reshape(1, D_MODEL)

    small = jnp.concatenate([meta_tokens, jnp.pad(conv_w[0], pad4)], axis=0)
    small_g = _gather_small(small)
    meta_full = small_g[:, :N_META].transpose(1, 0, 2).reshape(N_META, D_MODEL)
    conv_w_full = small_g[:, N_META:N_META + 4].transpose(1, 0, 2).reshape(4, D_MODEL)
    mixer_shards = [w_branch_ret[0].astype(BF16), w_branch_lru[0].astype(BF16), w_out[0].astype(BF16),
                    lru_wa[0].astype(BF16), lru_wx[0].astype(BF16)]
    wfi_shard = jnp.pad(w_ffn_in[0].astype(BF16), ((0, 0), (0, FFN_GROUP - FFN_SHARD)))
    own_slot = lambda ins, p: list(ins)
    part_of_owner = lambda ins, p: [r.at[p] for r in ins]

    rows = x.shape[1] + CHUNK
    h0 = jnp.concatenate([jnp.zeros((PAD_ROWS, D_MODEL), F32), meta_full, x[0]], axis=0)
    cos2, sin2 = _rope_tables(rows)
    dec = _retention_consts()

    proj, u, win_g = _in_proj(h0, mix_norm_w, w_in[0].astype(BF16), me.astype(jnp.int32).reshape(1))
    o, zr, states, wbr_g, wbl_g, wout_g, wa_g, wx_g = _retention_fwd(
        proj, cos2, sin2, dec, _mixer_weights_ride(mixer_shards))
    wbr, wbl, wout = (t.reshape(D_MODEL, D_MODEL) for t in (wbr_g, wbl_g, wout_g))
    wa_g, wx_g = _from_owners(wa_g), _from_owners(wx_g)
    gather_wfi = _Ride([wfi_shard], [jax.ShapeDtypeStruct((N_DEV, D_MODEL, FFN_GROUP), BF16)],
                       own_slot, _slot_of_sender, gather_by_chip=True)
    hs, zl, wfi_g = _lru_fwd(proj, conv_w_full, conv_b, lru_ba, lru_bx, lru_lambda, wa_g, wx_g, gather_wfi)
    h1, yr, yl, mixed, wfo_g = _mix_fwd(zr, zl, proj, h0, wbr, wbl, wout, _wfo_ride(w_ffn_out[0].astype(BF16)))
    u2, g, up, act, dh2, red = _ffn_fwd_loss(h1, ffn_norm_w, wfi_g, wfo_g, fw, loss_target[0])

    d_wfo = _wgrad(act, dh2, FFN_GROUP, D_MODEL, BF16)[:, 0]
    dgu, dh1, dw_ffn_norm = _ffn_bwd(dh2, g, up, h1, ffn_norm_w, wfi_g, wfo_g)
    d_wfi = _wgrad(u2, dgu, D_MODEL, FFN_GROUP, BF16, b_halves=True)[0]
    d_wout = _wgrad(mixed, dh1, D_MODEL, D_MODEL, BF16)[0, 0]
    dyr, dyl, dproj, dzr, dzl = _mix_bwd(dh1, yr, yl, proj, wbr, wbl, wout)
    d_wbr = _wgrad(zr, dyr, D_MODEL, D_MODEL, BF16)[0, 0]
    d_wbl = _wgrad(zl, dyl, D_MODEL, D_MODEL, BF16)[0, 0]
    dproj, d_wa, d_wx, lru_small, r_fi, r_fo = _lru_bwd(
        dzl, hs, proj, dproj, conv_w_full, conv_b, lru_ba, lru_bx, lru_lambda, wa_g, wx_g, [d_wfi, d_wfo])
    mix_shape = jax.ShapeDtypeStruct((N_DEV, D_MODEL // N_DEV, D_MODEL), BF16)
    scatter_mix = _Ride([t.reshape(mix_shape.shape) for t in (d_wbr, d_wbl, d_wout)], [mix_shape] * 3,
                        part_of_owner, _slot_of_sender)
    dproj, r_br, r_bl, r_out = _retention_bwd(dzr, o, proj, states, cos2, sin2, dec, dproj, scatter_mix)
    d_win = _wgrad(u, dproj, D_MODEL, D_MODEL, BF16)[0]
    dh0, dw_mix_norm, r_in, r_wa, r_wx = _in_proj_bwd(dproj, win_g, h0, mix_norm_w, dh1,
                                                      [d_win, _by_owner(d_wa), _by_owner(d_wx)])
    grad_x = dh0[CHUNK:]

    pack = jnp.concatenate([dw_mix_norm, lru_small, dw_ffn_norm, red, dh0[PAD_ROWS:CHUNK]], axis=0)
    small_sum = _sum_slots(_share_pack(pack))
    loss = (0.5 / D_MODEL) * jnp.sum(small_sum[PACK_SQ_ERR])

    def big_update(slots, w, m, v):
        shape = w.shape
        w2, m2, v2 = (t.reshape(slots.shape[1:]) for t in (w, m, v))
        return [t.reshape(shape) for t in _adamw(slots, w2, m2, v2)]

    res = {}
    res["w_in"] = big_update(r_in, w_in, m_w_in, v_w_in)
    res["w_branch_ret"] = big_update(r_br, w_branch_ret, m_w_branch_ret, v_w_branch_ret)
    res["w_branch_lru"] = big_update(r_bl, w_branch_lru, m_w_branch_lru, v_w_branch_lru)
    res["w_out"] = big_update(r_out, w_out, m_w_out, v_w_out)
    res["w_ffn_in"] = big_update(r_fi[:, :, :FFN_SHARD], w_ffn_in, m_w_ffn_in, v_w_ffn_in)
    res["w_ffn_out"] = big_update(r_fo, w_ffn_out, m_w_ffn_out, v_w_ffn_out)
    res["lru_wa"] = big_update(r_wa.reshape(N_DEV, LRU_BLOCKS * LRU_ROWS, LRU_BLOCK), lru_wa, m_lru_wa, v_lru_wa)
    res["lru_wx"] = big_update(r_wx.reshape(N_DEV, LRU_BLOCKS * LRU_ROWS, LRU_BLOCK), lru_wx, m_lru_wx, v_lru_wx)

    col = me * HEAD_DIM
    g_meta = lax.dynamic_slice(small_sum, (PACK_META, col), (N_META, HEAD_DIM))
    g_conv = lax.dynamic_slice(small_sum, (PACK_CONV_W, col), (8, HEAD_DIM))
    small_names = ["mix_norm_w", "conv_b", "lru_ba", "lru_bx", "lru_lambda", "ffn_norm_w", "final_norm_w"]
    small_rows = [PACK_MIX_NORM, PACK_CONV_B, PACK_BA, PACK_BX, PACK_LAM, PACK_FFN_NORM, PACK_FINAL_NORM]
    small_w = [mix_norm_w, conv_b, lru_ba, lru_bx, lru_lambda, ffn_norm_w, fw]
    small_m = [m_mix_norm_w, m_conv_b, m_lru_ba, m_lru_bx, m_lru_lambda, m_ffn_norm_w, m_final_norm_w.reshape(1, -1)]
    small_v = [v_mix_norm_w, v_conv_b, v_lru_ba, v_lru_bx, v_lru_lambda, v_ffn_norm_w, v_final_norm_w.reshape(1, -1)]

    def pack_small(vec_list, meta_t, conv_t):
        return jnp.concatenate([t.reshape(8, HEAD_DIM) for t in vec_list] + [meta_t, jnp.pad(conv_t[0], pad4)], axis=0)

    g_small = jnp.concatenate([small_sum[r].reshape(8, HEAD_DIM) for r in small_rows] + [g_meta, g_conv], axis=0)
    outs_small = _adamw(g_small[None], pack_small(small_w, meta_tokens, conv_w),
                        pack_small(small_m, m_meta_tokens, m_conv_w), pack_small(small_v, v_meta_tokens, v_conv_w))
    for idx, name in enumerate(small_names):
        shape = final_norm_w.shape if name == "final_norm_w" else (1, D_MODEL)
        res[name] = [t[8 * idx:8 * idx + 8].reshape(shape) for t in outs_small]
    res["meta_tokens"] = [t[56:72] for t in outs_small]
    res["conv_w"] = [t[72:76].reshape(1, 4, HEAD_DIM) for t in outs_small]

    order = ["meta_tokens", "mix_norm_w", "w_in", "conv_w", "conv_b", "lru_wa", "lru_ba", "lru_wx", "lru_bx",
             "lru_lambda", "w_branch_ret", "w_branch_lru", "w_out", "ffn_norm_w", "w_ffn_in", "w_ffn_out",
             "final_norm_w"]
    out = [loss, grad_x[None]]
    for kind in range(4):
        out += [res[name][kind] for name in order]
    return tuple(out)
```

```python
import functools

import numpy as np
import jax
import jax.numpy as jnp
from jax import lax
from jax.experimental import pallas as pl
from jax.experimental.pallas import tpu as pltpu

F32 = jnp.float32
BF16 = jnp.bfloat16

D_MODEL = 1024
N_META = 16
CHUNK = 128
PAD_ROWS = CHUNK - N_META
HEADS = 8
HEAD_DIM = 128
ROPE_BASE = 10000.0
QK_SCALE = HEAD_DIM ** -0.5
LRU_BLOCKS = 4
LRU_BLOCK = 256
LRU_C = 8.0
FFN_HIDDEN = 2816
N_DEV = 8
FFN_SHARD = 2 * FFN_HIDDEN // N_DEV
FFN_GROUP = 768
FFN_GROUPS = 4
FFN_OUT_SHARD = FFN_HIDDEN // N_DEV
NORM_EPS = 1e-6

ADAM_LR = 0.001
ADAM_B1 = 0.9
ADAM_B2 = 0.999
ADAM_EPS = 1e-08
ADAM_WD = 0.01
ADAM_STEP = 10

VMEM_LIMIT = 56 * 1024 * 1024
MESH_ID = pl.DeviceIdType.MESH
ANY = pl.BlockSpec(memory_space=pl.ANY)


def _cparams(sem):
    return pltpu.CompilerParams(dimension_semantics=sem, vmem_limit_bytes=VMEM_LIMIT)


def _tile(rows, cap):
    t = cap - cap % 64
    while rows % t:
        t -= 64
    return t


def _dot(a, b):
    return jnp.dot(a, b, preferred_element_type=F32)


def _dot_nt(a, b):
    return lax.dot_general(a, b, (((1,), (1,)), ((), ())), preferred_element_type=F32)


def _dot_tn(a, b):
    return lax.dot_general(a, b, (((0,), (0,)), ((), ())), preferred_element_type=F32)


def _sigmoid(x):
    return 0.5 * jnp.tanh(0.5 * x) + 0.5


def _gelu_parts(x):
    k = 0.7978845608028654
    inner = k * (x + 0.044715 * x * x * x)
    t = jnp.tanh(inner)
    g = 0.5 * x * (1.0 + t)
    dg = 0.5 * (1.0 + t) + 0.5 * x * (1.0 - t * t) * k * (1.0 + 3.0 * 0.044715 * x * x)
    return g, dg


def _rot(x, cos2, sin2):
    return x * cos2 + pltpu.roll(x, HEAD_DIM // 2, 1) * sin2


def _rot_t(dx, cos2, sin2):
    return dx * cos2 - pltpu.roll(dx, HEAD_DIM // 2, 1) * sin2


def _rms_bwd(x, w, dy):
    rs = lax.rsqrt(jnp.mean(x * x, axis=-1, keepdims=True) + NORM_EPS)
    nh = x * rs
    dw = jnp.sum(dy * nh, axis=0, keepdims=True)
    dn = dy * w
    dx = rs * (dn - nh * jnp.mean(dn * nh, axis=-1, keepdims=True))
    return dx, dw


def _retention_consts():
    h = jnp.arange(HEADS, dtype=F32)
    log_g = jnp.log(1.0 - 2.0 ** (-5.0 - h))
    idx = jnp.arange(CHUNK, dtype=F32)
    diff = idx[:, None] - idx[None, :]
    intra = jnp.where(diff[None] >= 0, jnp.exp(jnp.maximum(diff, 0.0)[None] * log_g[:, None, None]), 0.0)
    q_decay = jnp.exp((idx + 1.0)[:, None] * log_g[None, :])
    k_decay = jnp.exp((CHUNK - 1.0 - idx)[:, None] * log_g[None, :])
    chunk_decay = jnp.exp(CHUNK * log_g)
    shape = (HEADS, CHUNK, CHUNK)
    qd = jnp.broadcast_to(q_decay.T[:, :, None], shape)
    kd = jnp.broadcast_to(k_decay.T[:, :, None], shape)
    cd = jnp.broadcast_to(chunk_decay[:, None, None], shape)
    return jnp.stack([intra, qd, kd, cd])


def _rope_tables(rows):
    pos = jnp.maximum(jnp.arange(rows) - PAD_ROWS, 0).astype(F32)
    inv_freq = ROPE_BASE ** (-jnp.arange(0, HEAD_DIM, 2, dtype=F32) / HEAD_DIM)
    ang = pos[:, None] * inv_freq[None, :]
    cos, sin = jnp.cos(ang), jnp.sin(ang)
    return jnp.concatenate([cos, cos], axis=1), jnp.concatenate([-sin, sin], axis=1)


def _my_index():
    return 4 * lax.axis_index("x") + 2 * lax.axis_index("y") + lax.axis_index("c")


def _peer(k):
    x, y, c = lax.axis_index("x"), lax.axis_index("y"), lax.axis_index("c")
    px = 1 - x if k & 4 else x
    py = 1 - y if k & 2 else y
    pc = 1 - c if k & 1 else c
    return (px, py, pc), 4 * px + 2 * py + pc


def _push_sems(n_arr):
    n_rem = (N_DEV - 1) * n_arr
    return [pltpu.SemaphoreType.DMA((n_rem,)), pltpu.SemaphoreType.DMA((n_rem,)), pltpu.SemaphoreType.DMA((n_arr,))]


class _Push:
    def __init__(self, send_part, land_slot, sems, n_arr):
        self.send_part, self.land_slot, self.n_arr = send_part, land_slot, n_arr
        self.send_sems, self.recv_sems, self.loc_sems = sems

    def _remote(self, k, a, src, dst, pos):
        idx = (k - 1) * self.n_arr + a
        return pltpu.make_async_remote_copy(src_ref=src, dst_ref=dst, send_sem=self.send_sems.at[idx],
                                            recv_sem=self.recv_sems.at[idx], device_id=pos, device_id_type=MESH_ID)

    def _outgoing(self):
        me = _my_index()
        land = self.land_slot(me)
        remote = []
        for k in range(1, N_DEV):
            pos, p = _peer(k)
            src = self.send_part(p)
            remote += [self._remote(k, a, src[a], land[a], pos) for a in range(self.n_arr)]
        own = self.send_part(me)
        local = [pltpu.make_async_copy(own[a], land[a], self.loc_sems.at[a]) for a in range(self.n_arr)]
        return remote, local

    def start(self):
        remote, local = self._outgoing()
        for cp in remote + local:
            cp.start()

    def wait_recv_from(self, k):
        own = self.send_part(_my_index())
        pos, p = _peer(k)
        land = self.land_slot(p)
        for a in range(self.n_arr):
            self._remote(k, a, own[a], land[a], pos).wait_recv()

    def wait_sends(self):
        remote, local = self._outgoing()
        for cp in remote:
            cp.wait_send()
        for cp in local:
            cp.wait()

    def wait(self):
        for k in range(1, N_DEV):
            self.wait_recv_from(k)
        self.wait_sends()


DIRECT = (1, 2, 4, 6)
RELAYED = (2, 4, 6)


def _gather_by_chip_sems(n_arr):
    direct, relayed = len(DIRECT) * n_arr, len(RELAYED) * n_arr
    return [pltpu.SemaphoreType.DMA((direct,)), pltpu.SemaphoreType.DMA((direct,)),
            pltpu.SemaphoreType.DMA((relayed,)), pltpu.SemaphoreType.DMA((relayed,)), pltpu.SemaphoreType.DMA((n_arr,))]


class _GatherByChip:
    def __init__(self, srcs, land_slot, sems, n_arr):
        self.srcs, self.land_slot, self.n_arr = srcs, land_slot, n_arr
        self.send_sems, self.recv_sems, self.relay_send_sems, self.relay_recv_sems, self.loc_sems = sems

    def _direct(self, k, a, slot):
        idx = DIRECT.index(k) * self.n_arr + a
        return pltpu.make_async_remote_copy(src_ref=self.srcs[a], dst_ref=self.land_slot(slot)[a],
                                            send_sem=self.send_sems.at[idx], recv_sem=self.recv_sems.at[idx],
                                            device_id=_peer(k)[0], device_id_type=MESH_ID)

    def _relay(self, q, a, slot):
        idx = RELAYED.index(q) * self.n_arr + a
        block = self.land_slot(slot)[a]
        return pltpu.make_async_remote_copy(src_ref=block, dst_ref=block, send_sem=self.relay_send_sems.at[idx],
                                            recv_sem=self.relay_recv_sems.at[idx], device_id=_peer(1)[0],
                                            device_id_type=MESH_ID)

    def _own(self, a):
        return pltpu.make_async_copy(self.srcs[a], self.land_slot(_my_index())[a], self.loc_sems.at[a])

    def start(self):
        me = _my_index()
        for k in DIRECT:
            for a in range(self.n_arr):
                self._direct(k, a, me).start()
        for a in range(self.n_arr):
            self._own(a).start()

    def relay(self):
        for q in RELAYED:
            p = _peer(q)[1]
            for a in range(self.n_arr):
                self._direct(q, a, p).wait_recv()
                self._relay(q, a, p).start()

    def wait(self):
        me = _my_index()
        for a in range(self.n_arr):
            self._direct(1, a, _peer(1)[1]).wait_recv()
        for q in RELAYED:
            for a in range(self.n_arr):
                self._relay(q, a, _peer(q + 1)[1]).wait_recv()
        for k in DIRECT:
            for a in range(self.n_arr):
                self._direct(k, a, me).wait_send()
        for q in RELAYED:
            for a in range(self.n_arr):
                self._relay(q, a, _peer(q)[1]).wait_send()
        for a in range(self.n_arr):
            self._own(a).wait()


class _Ride:
    def __init__(self, arrays, out_shapes, send_part, land_slot, zero_dsts=None, zero_shape=None, n_zero=0,
                 gather_by_chip=False):
        self.arrays, self.out_shapes = list(arrays), list(out_shapes)
        self.send_part, self.land_slot, self.n = send_part, land_slot, len(arrays)
        self.zero_dsts, self.zero_shape, self.n_zero = zero_dsts, zero_shape, n_zero
        self.gather_by_chip = gather_by_chip

    def specs(self):
        return [ANY] * self.n

    def scratch(self):
        extra = [pltpu.SemaphoreType.DMA((self.n_zero,)), pltpu.VMEM(self.zero_shape, BF16)] if self.n_zero else []
        sems = _gather_by_chip_sems(self.n) if self.gather_by_chip else _push_sems(self.n)
        return sems + extra

    def push(self, in_refs, out_refs, scratch):
        ride = self
        n_sems = 5 if self.gather_by_chip else 3
        land = lambda s: ride.land_slot(out_refs, s)
        if self.gather_by_chip:
            push = _GatherByChip(list(in_refs), land, tuple(scratch[:n_sems]), self.n)
        else:
            push = _Push(lambda p: ride.send_part(in_refs, p), land, tuple(scratch[:n_sems]), self.n)

        class Both:
            def _fills(self):
                if not ride.n_zero:
                    return []
                zsems, zbuf = scratch[n_sems], scratch[n_sems + 1]
                return [pltpu.make_async_copy(zbuf, dst, zsems.at[z]) for z, dst in enumerate(ride.zero_dsts(out_refs))]

            def start(self):
                push.start()
                if ride.n_zero:
                    scratch[n_sems + 1][...] = jnp.zeros(ride.zero_shape, BF16)
                for cp in self._fills():
                    cp.start()

            def relay(self):
                if ride.gather_by_chip:
                    push.relay()

            def wait(self):
                push.wait()
                for cp in self._fills():
                    cp.wait()

        return Both()


def _slot_of_sender(out_refs, s):
    return [r.at[s] for r in out_refs]


def _push_call(name, arrays, out_shapes, send_part, land_slot):
    n_arr = len(arrays)

    def body(*refs):
        ins, outs, sems = refs[:n_arr], refs[n_arr:2 * n_arr], refs[2 * n_arr:]
        push = _Push(lambda p: send_part(ins, p), lambda s: land_slot(outs, s), sems, n_arr)
        push.start()
        push.wait()

    return pl.pallas_call(
        body, name=name, in_specs=[ANY] * n_arr, out_specs=[ANY] * n_arr, out_shape=out_shapes,
        scratch_shapes=_push_sems(n_arr), compiler_params=pltpu.CompilerParams(has_side_effects=True),
    )(*arrays)


LRU_ROWS = LRU_BLOCK // N_DEV
FFN_PAD_ROWS = FFN_GROUP - 2 * FFN_OUT_SHARD


def _half_rows(d):
    return pl.ds(pl.multiple_of((d % 2) * FFN_OUT_SHARD, 16), FFN_OUT_SHARD)


MIXER_SHAPES = [
    jax.ShapeDtypeStruct((N_DEV, D_MODEL // N_DEV, D_MODEL), BF16),
    jax.ShapeDtypeStruct((N_DEV, D_MODEL // N_DEV, D_MODEL), BF16),
    jax.ShapeDtypeStruct((N_DEV, D_MODEL // N_DEV, D_MODEL), BF16),
    jax.ShapeDtypeStruct((N_DEV, LRU_BLOCKS, LRU_ROWS, LRU_BLOCK), BF16),
    jax.ShapeDtypeStruct((N_DEV, LRU_BLOCKS, LRU_ROWS, LRU_BLOCK), BF16),
]


def _by_owner(t):
    return t.reshape(LRU_BLOCKS, N_DEV, LRU_ROWS, LRU_BLOCK).transpose(1, 0, 2, 3)


def _from_owners(t):
    return t.transpose(1, 0, 2, 3).reshape(LRU_BLOCKS, LRU_BLOCK, LRU_BLOCK)


def _mixer_weights_ride(shards):
    return _Ride(shards, MIXER_SHAPES, lambda ins, p: list(ins), _slot_of_sender, gather_by_chip=True)


def _wfo_ride(shard):
    zero_dsts = lambda outs: [outs[0].at[g, pl.ds(2 * FFN_OUT_SHARD, FFN_PAD_ROWS), :] for g in range(FFN_GROUPS)]
    return _Ride([shard], [jax.ShapeDtypeStruct((FFN_GROUPS, FFN_GROUP, D_MODEL), BF16)], lambda ins, p: list(ins),
                 lambda outs, d: [outs[0].at[d // 2, _half_rows(d), :]], zero_dsts, (FFN_PAD_ROWS, D_MODEL), FFN_GROUPS,
                 gather_by_chip=True)


def _arrival_rank_to_relation(jj):
    return jnp.where(jj == 3, 4, jnp.where(jj == 4, 3, jj))


def _in_proj(h0, norm_w, win_shard, me_arr):
    rows = h0.shape[0]
    tm = _tile(rows, 1664)
    n_i = rows // tm

    direct, relayed = DIRECT, RELAYED

    def body(me_ref, h_ref, nw_ref, wsh_ref, proj_ref, u_ref, wing_ref, u_all, wbuf, copy_sem,
             send_sems, recv_sems, relay_send_sems, relay_recv_sems, own_sem):
        del me_ref
        jj, i = pl.program_id(0), pl.program_id(1)
        me = _my_index()
        sibling = _peer(1)[0]

        def direct_copy(k, slot):
            n = direct.index(k)
            return pltpu.make_async_remote_copy(src_ref=wsh_ref, dst_ref=wing_ref.at[slot], send_sem=send_sems.at[n],
                                                recv_sem=recv_sems.at[n], device_id=_peer(k)[0], device_id_type=MESH_ID)

        def relay_copy(q, slot):
            n = relayed.index(q)
            return pltpu.make_async_remote_copy(src_ref=wing_ref.at[slot], dst_ref=wing_ref.at[slot],
                                                send_sem=relay_send_sems.at[n], recv_sem=relay_recv_sems.at[n],
                                                device_id=sibling, device_id_type=MESH_ID)

        own_slot = pltpu.make_async_copy(wsh_ref, wing_ref.at[me], own_sem)

        @pl.when(jnp.logical_and(jj == 0, i == 0))
        def _():
            for k in direct:
                direct_copy(k, me).start()
            own_slot.start()
            own = pltpu.make_async_copy(wsh_ref, wbuf, copy_sem)
            own.start()
            own.wait()

        for k in range(1, N_DEV):
            rank = {3: 4, 4: 3}.get(k, k)

            @pl.when(jnp.logical_and(jj == rank, i == 0))
            def _(k=k):
                p = _peer(k)[1]
                if k in direct:
                    direct_copy(k, p).wait_recv()
                    if k in relayed:
                        relay_copy(k, p).start()
                else:
                    relay_copy(k - 1, p).wait_recv()
                landed = pltpu.make_async_copy(wing_ref.at[p], wbuf, copy_sem)
                landed.start()
                landed.wait()

        rows_i = pl.ds(pl.multiple_of(i * tm, tm), tm)

        @pl.when(jj == 0)
        def _():
            x = h_ref[...]
            rs = lax.rsqrt(jnp.mean(x * x, axis=-1, keepdims=True) + NORM_EPS)
            u = (x * rs * nw_ref[...]).astype(BF16)
            u_all[rows_i, :] = u
            u_ref[...] = u
        proj_ref[...] = _dot(u_all[rows_i, :], wbuf[...]).astype(BF16)

        @pl.when(jnp.logical_and(jj == N_DEV - 1, i == n_i - 1))
        def _():
            for k in direct:
                direct_copy(k, me).wait_send()
            for q in relayed:
                relay_copy(q, _peer(q)[1]).wait_send()
            own_slot.wait()

    first_pass = lambda jj, i: jnp.where(jj == 0, i, n_i - 1)
    grid_spec = pltpu.PrefetchScalarGridSpec(
        num_scalar_prefetch=1, grid=(N_DEV, n_i),
        in_specs=[pl.BlockSpec((tm, D_MODEL), lambda jj, i, me: (first_pass(jj, i), 0)),
                  pl.BlockSpec((1, D_MODEL), lambda jj, i, me: (0, 0)), ANY],
        out_specs=[pl.BlockSpec((tm, D_MODEL), lambda jj, i, me: (i, me[0] ^ _arrival_rank_to_relation(jj))),
                   pl.BlockSpec((tm, D_MODEL), lambda jj, i, me: (first_pass(jj, i), 0)), ANY],
        scratch_shapes=[pltpu.VMEM((rows, D_MODEL), BF16), pltpu.VMEM((D_MODEL, D_MODEL), BF16),
                        pltpu.SemaphoreType.DMA(()),
                        pltpu.SemaphoreType.DMA((len(direct),)), pltpu.SemaphoreType.DMA((len(direct),)),
                        pltpu.SemaphoreType.DMA((len(relayed),)), pltpu.SemaphoreType.DMA((len(relayed),)),
                        pltpu.SemaphoreType.DMA(())])
    return pl.pallas_call(
        body, name="in_proj", grid_spec=grid_spec,
        out_shape=[jax.ShapeDtypeStruct((rows, N_DEV * D_MODEL), BF16),
                   jax.ShapeDtypeStruct((rows, D_MODEL), BF16),
                   jax.ShapeDtypeStruct((N_DEV, D_MODEL, D_MODEL), BF16)],
        compiler_params=pltpu.CompilerParams(dimension_semantics=("arbitrary", "arbitrary"),
                                             vmem_limit_bytes=VMEM_LIMIT, has_side_effects=True),
    )(me_arr, h0, norm_w, win_shard)


def _seg_spec(rows_per_block, seg):
    return pl.BlockSpec((rows_per_block, D_MODEL), lambda n, seg=seg: (n, seg))


def _chunks_per_step(n_chunks):
    return next(c for c in (5, 3, 2, 1) if n_chunks % c == 0)


def _retention_fwd(proj, cos2, sin2, dec, ride):
    rows = proj.shape[0]
    n_chunks = rows // CHUNK
    per_step = _chunks_per_step(n_chunks)
    n_steps = n_chunks // per_step
    tm = per_step * CHUNK
    n_r = ride.n

    def body(q_ref, k_ref, v_ref, g_ref, c_ref, s_ref, dec_ref, *refs):
        o_ref, zr_ref, st_ref = refs[n_r:n_r + 3]
        state = refs[2 * n_r + 3]
        push = ride.push(refs[:n_r], refs[n_r + 3:2 * n_r + 3], refs[2 * n_r + 4:])

        @pl.when(pl.program_id(0) == 0)
        def _():
            push.start()
            state[...] = jnp.zeros_like(state)

        for h in range(HEADS):
            sl = slice(HEAD_DIM * h, HEAD_DIM * (h + 1))
            st = state[h]
            for c in range(per_step):
                rw = slice(CHUNK * c, CHUNK * (c + 1))
                cos_t, sin_t = c_ref[rw, :], s_ref[rw, :]
                qh = _rot(q_ref[rw, sl].astype(F32), cos_t, sin_t)
                kh = _rot(k_ref[rw, sl].astype(F32), cos_t, sin_t) * QK_SCALE
                qb, kb, vb = qh.astype(BF16), kh.astype(BF16), v_ref[rw, sl]
                s = _dot_nt(qb, kb) * dec_ref[0, h]
                st_ref[c, h] = st
                o = _dot(s.astype(BF16), vb) + _dot(qb, st.astype(BF16)) * dec_ref[1, h]
                st = st * dec_ref[3, h] + _dot_tn((kh * dec_ref[2, h]).astype(BF16), vb)
                o_ref[rw, sl] = o.astype(BF16)
                r = lax.rsqrt(jnp.mean(o * o, axis=-1, keepdims=True) + NORM_EPS)
                g = g_ref[rw, sl].astype(F32)
                zr_ref[rw, sl] = (g * _sigmoid(g) * (o * r)).astype(BF16)
            state[h] = st

        @pl.when(pl.program_id(0) == n_steps // 2)
        def _():
            push.relay()

        @pl.when(pl.program_id(0) == n_steps - 1)
        def _():
            push.wait()

    tab = pl.BlockSpec((tm, HEAD_DIM), lambda n: (n, 0))
    return pl.pallas_call(
        body, name="retention_fwd", grid=(n_steps,),
        in_specs=[_seg_spec(tm, 0), _seg_spec(tm, 1), _seg_spec(tm, 2), _seg_spec(tm, 3), tab, tab,
                  pl.BlockSpec((4, HEADS, CHUNK, CHUNK), lambda n: (0, 0, 0, 0))] + ride.specs(),
        out_specs=[pl.BlockSpec((tm, D_MODEL), lambda n: (n, 0)),
                   pl.BlockSpec((tm, D_MODEL), lambda n: (n, 0)),
                   pl.BlockSpec((per_step, HEADS, HEAD_DIM, HEAD_DIM), lambda n: (n, 0, 0, 0))] + ride.specs(),
        out_shape=[jax.ShapeDtypeStruct((rows, D_MODEL), BF16),
                   jax.ShapeDtypeStruct((rows, D_MODEL), BF16),
                   jax.ShapeDtypeStruct((n_chunks, HEADS, HEAD_DIM, HEAD_DIM), F32)] + ride.out_shapes,
        scratch_shapes=[pltpu.VMEM((HEADS, HEAD_DIM, HEAD_DIM), F32)] + ride.scratch(),
        compiler_params=pltpu.CompilerParams(dimension_semantics=("arbitrary",), vmem_limit_bytes=VMEM_LIMIT,
                                             has_side_effects=True),
    )(proj, proj, proj, proj, cos2, sin2, dec, *ride.arrays)


def _lru_gates(c, ba, bx, wa_ref, wx_ref):
    pre_r, pre_i = [], []
    for g in range(LRU_BLOCKS):
        cg = c[:, LRU_BLOCK * g:LRU_BLOCK * (g + 1)].astype(BF16)
        pre_r.append(_dot(cg, wa_ref[g]))
        pre_i.append(_dot(cg, wx_ref[g]))
    return _sigmoid(jnp.concatenate(pre_r, axis=1) + ba), _sigmoid(jnp.concatenate(pre_i, axis=1) + bx)


def _lru_decay(r, lam):
    sp = jnp.maximum(-lam, 0.0) + jnp.log(1.0 + jnp.exp(-jnp.abs(lam)))
    log_a = -LRU_C * r * sp
    a = jnp.exp(log_a)
    one_minus_a2 = -jnp.tanh(log_a) * (a * a + 1.0)
    inv_mult = lax.rsqrt(jnp.maximum(one_minus_a2, 1e-30))
    return a, one_minus_a2 * inv_mult, inv_mult, sp


def _conv_taps(xbuf, tm, cw_ref, cb_ref):
    c = cb_ref[...] + cw_ref[3:4, :] * xbuf[8:8 + tm, :]
    for back in (1, 2, 3):
        c = c + cw_ref[3 - back:4 - back, :] * xbuf[8 - back:8 - back + tm, :]
    return c


def _lru_fwd(proj, conv_w, conv_b, ba, bx, lam, wa_g, wx_g, ride):
    rows = proj.shape[0]
    tm = _tile(rows, 320)
    n_t = rows // tm
    n_r = ride.n

    def body(x_ref, gt_ref, cw_ref, cb_ref, ba_ref, bx_ref, lam_ref, wa_ref, wx_ref, *refs):
        hs_ref, zl_ref, cri_ref = refs[n_r:n_r + 3]
        xbuf, abuf, ubuf, hcar = refs[2 * n_r + 3:2 * n_r + 7]
        push = ride.push(refs[:n_r], refs[n_r + 3:2 * n_r + 3], refs[2 * n_r + 7:])
        i = pl.program_id(0)

        @pl.when(i == 0)
        def _():
            push.start()
            xbuf[0:8, :] = jnp.zeros((8, D_MODEL), F32)
            hcar[...] = jnp.zeros_like(hcar)

        xbuf[8:8 + tm, :] = x_ref[...].astype(F32)
        c = _conv_taps(xbuf, tm, cw_ref, cb_ref)
        xbuf[0:8, :] = xbuf[tm:tm + 8, :]
        r, ig = _lru_gates(c, ba_ref[...], bx_ref[...], wa_ref, wx_ref)
        a, mult, _, _ = _lru_decay(r, lam_ref[...])
        cri_ref[0] = c.astype(BF16)
        cri_ref[1] = r.astype(BF16)
        cri_ref[2] = ig.astype(BF16)
        row = i * tm + lax.broadcasted_iota(jnp.int32, (tm, 1), 0)
        abuf[...] = a
        ubuf[...] = jnp.where(row >= PAD_ROWS, mult * (ig * c), 0.0)

        sub = lax.broadcasted_iota(jnp.int32, (8, D_MODEL), 0)

        def block(b, carry):
            off = pl.multiple_of(b * 8, 8)
            av, uv = abuf[pl.ds(off, 8), :], ubuf[pl.ds(off, 8), :]
            for s in (1, 2, 4):
                us = jnp.where(sub >= s, pltpu.roll(uv, s, 0), 0.0)
                as_ = jnp.where(sub >= s, pltpu.roll(av, s, 0), 1.0)
                uv = uv + av * us
                av = av * as_
            hv = uv + av * carry
            ubuf[pl.ds(off, 8), :] = hv
            return hv[7:8, :]

        hcar[...] = lax.fori_loop(0, tm // 8, block, hcar[...])
        gl, _ = _gelu_parts(gt_ref[...].astype(F32))
        hs = ubuf[...]
        hs_ref[...] = hs.astype(BF16)
        zl_ref[...] = (gl * hs).astype(BF16)

        @pl.when(i == n_t // 2)
        def _():
            push.relay()

        @pl.when(i == n_t - 1)
        def _():
            push.wait()

    vec = pl.BlockSpec((1, D_MODEL), lambda i: (0, 0))
    mat = pl.BlockSpec((LRU_BLOCKS, LRU_BLOCK, LRU_BLOCK), lambda i: (0, 0, 0))
    row = pl.BlockSpec((tm, D_MODEL), lambda i: (i, 0))
    return pl.pallas_call(
        body, name="lru_fwd", grid=(n_t,),
        in_specs=[_seg_spec(tm, 4), _seg_spec(tm, 5), pl.BlockSpec((4, D_MODEL), lambda i: (0, 0)),
                  vec, vec, vec, vec, mat, mat] + ride.specs(),
        out_specs=[row, row, pl.BlockSpec((3, tm, D_MODEL), lambda i: (0, i, 0))] + ride.specs(),
        out_shape=[jax.ShapeDtypeStruct((rows, D_MODEL), BF16)] * 2
        + [jax.ShapeDtypeStruct((3, rows, D_MODEL), BF16)] + ride.out_shapes,
        scratch_shapes=[pltpu.VMEM((tm + 8, D_MODEL), F32), pltpu.VMEM((tm, D_MODEL), F32),
                        pltpu.VMEM((tm, D_MODEL), F32), pltpu.VMEM((1, D_MODEL), F32)] + ride.scratch(),
        compiler_params=pltpu.CompilerParams(dimension_semantics=("arbitrary",), vmem_limit_bytes=VMEM_LIMIT,
                                             has_side_effects=True),
    )(proj, proj, conv_w, conv_b, ba, bx, lam, wa_g, wx_g, *ride.arrays)


def _mix_fwd(zr, zl, proj, h0, wbr, wbl, wout, ride):
    rows = h0.shape[0]
    tm = _tile(rows, 640)
    n_t = rows // tm
    n_r = ride.n

    def body(zr_ref, zl_ref, ga_ref, gb_ref, h0_ref, wbr_ref, wbl_ref, wo_ref, *refs):
        h1_ref, yr_ref, yl_ref, mx_ref = refs[n_r:n_r + 4]
        push = ride.push(refs[:n_r], refs[n_r + 4:2 * n_r + 4], refs[2 * n_r + 4:])

        @pl.when(pl.program_id(0) == 0)
        def _():
            push.start()

        yr = _dot(zr_ref[...], wbr_ref[...])
        yl = _dot(zl_ref[...], wbl_ref[...])
        mixed = (_sigmoid(ga_ref[...].astype(F32)) * yr + _sigmoid(gb_ref[...].astype(F32)) * yl).astype(BF16)
        yr_ref[...] = yr.astype(BF16)
        yl_ref[...] = yl.astype(BF16)
        mx_ref[...] = mixed
        h1_ref[...] = h0_ref[...] + _dot(mixed, wo_ref[...])

        @pl.when(pl.program_id(0) == n_t // 2)
        def _():
            push.relay()

        @pl.when(pl.program_id(0) == n_t - 1)
        def _():
            push.wait()

    row = pl.BlockSpec((tm, D_MODEL), lambda i: (i, 0))
    wsp = pl.BlockSpec((D_MODEL, D_MODEL), lambda i: (0, 0))
    return pl.pallas_call(
        body, name="mix_fwd", grid=(n_t,),
        in_specs=[row, row, _seg_spec(tm, 6), _seg_spec(tm, 7), row, wsp, wsp, wsp] + ride.specs(),
        out_specs=[row, row, row, row] + ride.specs(),
        out_shape=[jax.ShapeDtypeStruct((rows, D_MODEL), F32)] + [jax.ShapeDtypeStruct((rows, D_MODEL), BF16)] * 3
        + ride.out_shapes,
        scratch_shapes=ride.scratch(),
        compiler_params=pltpu.CompilerParams(dimension_semantics=("arbitrary",), vmem_limit_bytes=VMEM_LIMIT,
                                             has_side_effects=True),
    )(zr, zl, proj, proj, h0, wbr, wbl, wout, *ride.arrays)


def _ffn_fwd_loss(h1, norm_w, wfi_g, wfo_g, final_w, target):
    rows = h1.shape[0]
    tm = _tile(rows, 320)
    piece = 64
    n_piece = tm // piece

    def body(h1_ref, nw_ref, wfi_ref, wfo_ref, fw_ref, *refs):
        t_refs = refs[:n_piece]
        u2_ref, g_ref, up_ref, act_ref, dh2_ref, red_ref = refs[n_piece:]
        i = pl.program_id(0)

        @pl.when(i == 0)
        def _():
            red_ref[...] = jnp.zeros_like(red_ref)

        x = h1_ref[...]
        rs = lax.rsqrt(jnp.mean(x * x, axis=-1, keepdims=True) + NORM_EPS)
        u2 = (x * rs * nw_ref[...]).astype(BF16)
        u2_ref[...] = u2
        ffn = None
        for d in range(FFN_GROUPS):
            cols = slice(FFN_GROUP * d, FFN_GROUP * (d + 1))
            g = _dot(u2, wfi_ref[d])
            up = _dot(u2, wfi_ref[d + FFN_GROUPS])
            act = (g * _sigmoid(g) * up).astype(BF16)
            g_ref[:, cols] = g.astype(BF16)
            up_ref[:, cols] = up.astype(BF16)
            act_ref[:, cols] = act
            part = _dot(act, wfo_ref[d])
            ffn = part if ffn is None else ffn + part

        h2 = x + ffn
        rs = lax.rsqrt(jnp.mean(h2 * h2, axis=-1, keepdims=True) + NORM_EPS)
        nh = h2 * rs
        fw = fw_ref[...]
        row = i * tm + lax.broadcasted_iota(jnp.int32, (tm, 1), 0)
        tgt = jnp.concatenate([t[...] for t in t_refs], axis=0)
        diff = jnp.where(row >= CHUNK, nh * fw - tgt, 0.0)
        dy = diff * (1.0 / D_MODEL)
        red_ref[0:1, :] += jnp.sum(diff * diff, axis=0, keepdims=True)
        red_ref[1:2, :] += jnp.sum(dy * nh, axis=0, keepdims=True)
        dn = dy * fw
        dh2_ref[...] = rs * (dn - nh * jnp.mean(dn * nh, axis=-1, keepdims=True))

    row = pl.BlockSpec((tm, D_MODEL), lambda i: (i, 0))
    vec = pl.BlockSpec((1, D_MODEL), lambda i: (0, 0))
    hid = pl.BlockSpec((tm, FFN_GROUPS * FFN_GROUP), lambda i: (i, 0))
    hid_shape = jax.ShapeDtypeStruct((rows, FFN_GROUPS * FFN_GROUP), BF16)
    resident = dict(pipeline_mode=pl.Buffered(1))
    head_pieces = CHUNK // piece
    t_specs = [pl.BlockSpec((piece, D_MODEL), lambda i, k=k: (jnp.maximum(i * n_piece + k - head_pieces, 0), 0))
               for k in range(n_piece)]
    return pl.pallas_call(
        body, name="ffn_fwd_loss", grid=(rows // tm,),
        in_specs=[row, vec,
                  pl.BlockSpec((2 * FFN_GROUPS, D_MODEL, FFN_GROUP), lambda i: (0, 0, 0), **resident),
                  pl.BlockSpec((FFN_GROUPS, FFN_GROUP, D_MODEL), lambda i: (0, 0, 0), **resident),
                  vec] + t_specs,
        out_specs=[row, hid, hid, hid, row, pl.BlockSpec((8, D_MODEL), lambda i: (0, 0))],
        out_shape=[jax.ShapeDtypeStruct((rows, D_MODEL), BF16), hid_shape, hid_shape, hid_shape,
                   jax.ShapeDtypeStruct((rows, D_MODEL), F32), jax.ShapeDtypeStruct((8, D_MODEL), F32)],
        compiler_params=_cparams(("arbitrary",)),
    )(h1, norm_w, wfi_g, wfo_g, final_w, *([target] * n_piece))


def _wgrad(a, b, ka, tn, out_dtype, b_halves=False):
    rows = a.shape[0]
    na = a.shape[1] // ka
    tm = _tile(rows, 1664)
    nm = rows // tm
    if b_halves:
        per_half = b.shape[2] // tn
        nb = 2 * per_half
        b_spec = pl.BlockSpec((None, tm, tn), lambda p, q, m: (q // per_half, m, q % per_half))
    else:
        nb = b.shape[1] // tn
        b_spec = pl.BlockSpec((tm, tn), lambda p, q, m: (m, q))

    def body(a_ref, b_ref, o_ref, acc):
        m = pl.program_id(2)

        @pl.when(m == 0)
        def _():
            acc[...] = jnp.zeros_like(acc)

        acc[...] += _dot_tn(a_ref[...].astype(BF16), b_ref[...].astype(BF16))

        @pl.when(m == nm - 1)
        def _():
            o_ref[...] = acc[...].astype(out_dtype)

    return pl.pallas_call(
        body, name="wgrad", grid=(na, nb, nm),
        in_specs=[pl.BlockSpec((tm, ka), lambda p, q, m: (m, p)), b_spec],
        out_specs=pl.BlockSpec((None, None, ka, tn), lambda p, q, m: (p, q, 0, 0)),
        out_shape=jax.ShapeDtypeStruct((na, nb, ka, tn), out_dtype),
        scratch_shapes=[pltpu.VMEM((ka, tn), F32)],
        compiler_params=_cparams(("parallel", "parallel", "arbitrary")),
    )(a, b)


def _ffn_bwd(dh2, g, up, h1, norm_w, wfi_g, wfo_g):
    rows = h1.shape[0]
    tm = _tile(rows, 320)

    def body(dh2_ref, g_ref, up_ref, h1_ref, nw_ref, wfi_ref, wfo_ref, dgu_ref, dh1_ref, dw_ref):
        @pl.when(pl.program_id(0) == 0)
        def _():
            dw_ref[...] = jnp.zeros_like(dw_ref)

        dh2 = dh2_ref[...]
        dh2_b = dh2.astype(BF16)
        du2 = None
        for d in range(FFN_GROUPS):
            cols = slice(FFN_GROUP * d, FFN_GROUP * (d + 1))
            dact = _dot_nt(dh2_b, wfo_ref[d])
            gv, uv = g_ref[:, cols].astype(F32), up_ref[:, cols].astype(F32)
            sg = _sigmoid(gv)
            dg = (dact * uv * (sg * (1.0 + gv * (1.0 - sg)))).astype(BF16)
            dup = (dact * (gv * sg)).astype(BF16)
            dgu_ref[0, :, cols] = dg
            dgu_ref[1, :, cols] = dup
            part = _dot_nt(dg, wfi_ref[d]) + _dot_nt(dup, wfi_ref[d + FFN_GROUPS])
            du2 = part if du2 is None else du2 + part
        dx, dw = _rms_bwd(h1_ref[...], nw_ref[...], du2)
        dw_ref[0:1, :] += dw
        dh1_ref[...] = dh2 + dx

    row = pl.BlockSpec((tm, D_MODEL), lambda i: (i, 0))
    vec = pl.BlockSpec((1, D_MODEL), lambda i: (0, 0))
    hid = pl.BlockSpec((tm, FFN_GROUPS * FFN_GROUP), lambda i: (i, 0))
    resident = dict(pipeline_mode=pl.Buffered(1))
    return pl.pallas_call(
        body, name="ffn_bwd", grid=(rows // tm,),
        in_specs=[row, hid, hid, row, vec,
                  pl.BlockSpec((2 * FFN_GROUPS, D_MODEL, FFN_GROUP), lambda i: (0, 0, 0), **resident),
                  pl.BlockSpec((FFN_GROUPS, FFN_GROUP, D_MODEL), lambda i: (0, 0, 0), **resident)],
        out_specs=[pl.BlockSpec((2, tm, FFN_GROUPS * FFN_GROUP), lambda i: (0, i, 0)), row,
                   pl.BlockSpec((8, D_MODEL), lambda i: (0, 0))],
        out_shape=[jax.ShapeDtypeStruct((2, rows, FFN_GROUPS * FFN_GROUP), BF16),
                   jax.ShapeDtypeStruct((rows, D_MODEL), F32), jax.ShapeDtypeStruct((8, D_MODEL), F32)],
        compiler_params=_cparams(("arbitrary",)),
    )(dh2, g, up, h1, norm_w, wfi_g, wfo_g)


def _mix_bwd(dh1, yr, yl, proj, wbr, wbl, wout):
    rows = dh1.shape[0]
    tm = _tile(rows, 640)

    def body(dh1_ref, yr_ref, yl_ref, ga_ref, gb_ref, wbr_ref, wbl_ref, wo_ref,
             dyr_ref, dyl_ref, dseg_ref, dzr_ref, dzl_ref):
        dmix = _dot_nt(dh1_ref[...].astype(BF16), wo_ref[...])
        sa, sb = _sigmoid(ga_ref[...].astype(F32)), _sigmoid(gb_ref[...].astype(F32))
        dyr = (dmix * sa).astype(BF16)
        dyl = (dmix * sb).astype(BF16)
        dyr_ref[...] = dyr
        dyl_ref[...] = dyl
        dseg_ref[:, 0:D_MODEL] = (dmix * yr_ref[...].astype(F32) * (sa * (1.0 - sa))).astype(BF16)
        dseg_ref[:, D_MODEL:2 * D_MODEL] = (dmix * yl_ref[...].astype(F32) * (sb * (1.0 - sb))).astype(BF16)
        dzr_ref[...] = _dot_nt(dyr, wbr_ref[...]).astype(BF16)
        dzl_ref[...] = _dot_nt(dyl, wbl_ref[...]).astype(BF16)

    row = pl.BlockSpec((tm, D_MODEL), lambda i: (i, 0))
    wsp = pl.BlockSpec((D_MODEL, D_MODEL), lambda i: (0, 0))
    bshape = jax.ShapeDtypeStruct((rows, D_MODEL), BF16)
    return pl.pallas_call(
        body, name="mix_bwd", grid=(rows // tm,),
        in_specs=[row, row, row, _seg_spec(tm, 6), _seg_spec(tm, 7), wsp, wsp, wsp],
        out_specs=[row, row, pl.BlockSpec((tm, 2 * D_MODEL), lambda i: (i, 3)), row, row],
        out_shape=[bshape, bshape, jax.ShapeDtypeStruct((rows, N_DEV * D_MODEL), BF16), bshape, bshape],
        compiler_params=_cparams(("parallel",)),
    )(dh1, yr, yl, proj, proj, wbr, wbl, wout)


S1_SHAPES = [
    jax.ShapeDtypeStruct((N_DEV, D_MODEL, FFN_GROUP), BF16),
    jax.ShapeDtypeStruct((N_DEV, FFN_OUT_SHARD, D_MODEL), BF16),
]


def _s1_parts(ins, p):
    return [ins[0].at[p], ins[1].at[p // 2, _half_rows(p), :]]


def _lru_bwd(dzl, hs, cri, proj, dproj, conv_w, lam, wa_g, wx_g, s1_grads):
    rows = dzl.shape[0]
    tm = _tile(rows, 320)
    nt = rows // tm
    t8 = tm // 8
    n_s1 = len(s1_grads)

    def body(dzl_ref, hs_ref, hsp_ref, cri_ref, x_ref, gt_ref, cw_ref, lam_ref, wa_ref, wx_ref, dproj_in, *refs):
        del dproj_in
        s1_refs = refs[:n_s1]
        dseg_ref, dwa_ref, dwx_ref, sm_ref = refs[n_s1:n_s1 + 4]
        land_refs = refs[n_s1 + 4:2 * n_s1 + 4]
        xbuf, abuf, bbuf, dbuf, dcbuf, anext, dhcar, send_sems, recv_sems, loc_sems = refs[2 * n_s1 + 4:]
        step = pl.program_id(0)
        i = nt - 1 - step
        push = _Push(lambda p: _s1_parts(s1_refs, p), lambda s: [r.at[s] for r in land_refs],
                     (send_sems, recv_sems, loc_sems), n_s1)

        @pl.when(step == 0)
        def _():
            push.start()
            dwa_ref[...] = jnp.zeros_like(dwa_ref)
            dwx_ref[...] = jnp.zeros_like(dwx_ref)
            sm_ref[...] = jnp.zeros_like(sm_ref)
            anext[...] = jnp.zeros_like(anext)
            dhcar[...] = jnp.zeros_like(dhcar)
            dcbuf[tm:tm + 8, :] = jnp.zeros((8, D_MODEL), F32)

        first = i == 0
        c, r, ig = (cri_ref[n].astype(F32) for n in range(3))
        lam_v = lam_ref[...]
        a, mult, inv_mult, sp = _lru_decay(r, lam_v)
        hs_v = hs_ref[...].astype(F32)
        gl, dgl = _gelu_parts(gt_ref[...].astype(F32))
        dzl_v = dzl_ref[...].astype(F32)
        dseg_ref[:, D_MODEL:2 * D_MODEL] = (dzl_v * hs_v * dgl).astype(BF16)
        dbuf[...] = dzl_v * gl
        abuf[0:tm, :] = a
        abuf[tm:tm + 8, :] = jnp.broadcast_to(anext[...], (8, D_MODEL))
        bbuf[...] = abuf[1:tm + 1, :]

        sub = lax.broadcasted_iota(jnp.int32, (8, D_MODEL), 0)

        def block(k, carry):
            off = pl.multiple_of((t8 - 1 - k) * 8, 8)
            av = bbuf[pl.ds(off, 8), :]
            uv = dbuf[pl.ds(off, 8), :]
            for s in (1, 2, 4):
                us = jnp.where(sub < 8 - s, pltpu.roll(uv, 8 - s, 0), 0.0)
                as_ = jnp.where(sub < 8 - s, pltpu.roll(av, 8 - s, 0), 1.0)
                uv = uv + av * us
                av = av * as_
            hv = uv + av * carry
            dbuf[pl.ds(off, 8), :] = hv
            return hv[0:1, :]

        dhcar[...] = lax.fori_loop(0, t8, block, dhcar[...])
        anext[...] = abuf[0:1, :]
        dh = dbuf[...]

        xbuf[0:8, :] = jnp.where(first, 0.0, hsp_ref[8:16, :].astype(F32))
        xbuf[8:8 + tm, :] = hs_v
        hprev = xbuf[7:7 + tm, :]
        row = i * tm + lax.broadcasted_iota(jnp.int32, (tm, 1), 0)
        duu = jnp.where(row >= PAD_ROWS, dh, 0.0)
        da = dh * hprev
        dmult = duu * ig * c
        di = duu * mult * c
        dc = duu * mult * ig
        dlog_a = da * a - dmult * (a * a) * inv_mult
        dr = dlog_a * (-LRU_C * sp)
        dsp = jnp.sum(dlog_a * (-LRU_C * r), axis=0, keepdims=True)
        dpr = dr * r * (1.0 - r)
        dpi = di * ig * (1.0 - ig)
        dpr_b, dpi_b = dpr.astype(BF16), dpi.astype(BF16)
        dcs = []
        for g in range(LRU_BLOCKS):
            sl = slice(LRU_BLOCK * g, LRU_BLOCK * (g + 1))
            cg = c[:, sl].astype(BF16)
            dwa_ref[g] += _dot_tn(cg, dpr_b[:, sl])
            dwx_ref[g] += _dot_tn(cg, dpi_b[:, sl])
            dcs.append(_dot_nt(dpr_b[:, sl], wa_ref[g]) + _dot_nt(dpi_b[:, sl], wx_ref[g]))
        dc = dc + jnp.concatenate(dcs, axis=1)

        dcbuf[0:tm, :] = dc
        x_v = x_ref[...].astype(F32)
        dlin = cw_ref[3:4, :] * dc
        sm_ref[3:4, :] += jnp.sum(dc * x_v, axis=0, keepdims=True)
        for back in (1, 2, 3):
            dc_later = dcbuf[back:back + tm, :]
            dlin = dlin + cw_ref[3 - back:4 - back, :] * dc_later
            sm_ref[3 - back:4 - back, :] += jnp.sum(dc_later * x_v, axis=0, keepdims=True)
        dseg_ref[:, 0:D_MODEL] = dlin.astype(BF16)
        dcbuf[tm:tm + 8, :] = dcbuf[0:8, :]
        sm_ref[4:5, :] += jnp.sum(dc, axis=0, keepdims=True)
        sm_ref[5:6, :] += jnp.sum(dpr, axis=0, keepdims=True)
        sm_ref[6:7, :] += jnp.sum(dpi, axis=0, keepdims=True)
        sm_ref[7:8, :] += dsp * (-_sigmoid(-lam_v))

        @pl.when(step == nt - 1)
        def _():
            push.wait()

    rowb = pl.BlockSpec((tm, D_MODEL), lambda s: (nt - 1 - s, 0))
    t16 = tm // 16
    prev8 = pl.BlockSpec((16, D_MODEL), lambda s: (jnp.maximum((nt - 1 - s) * t16 - 1, 0), 0))
    seg = lambda k: pl.BlockSpec((tm, D_MODEL), lambda s, k=k: (nt - 1 - s, k))
    vec = pl.BlockSpec((1, D_MODEL), lambda s: (0, 0))
    mat = pl.BlockSpec((LRU_BLOCKS, LRU_BLOCK, LRU_BLOCK), lambda s: (0, 0, 0))
    mshape = jax.ShapeDtypeStruct((LRU_BLOCKS, LRU_BLOCK, LRU_BLOCK), F32)
    n_in = 10
    return pl.pallas_call(
        body, name="lru_bwd", grid=(nt,),
        in_specs=[rowb, rowb, prev8, pl.BlockSpec((3, tm, D_MODEL), lambda s: (0, nt - 1 - s, 0)), seg(4), seg(5),
                  pl.BlockSpec((4, D_MODEL), lambda s: (0, 0)), vec, mat, mat, ANY] + [ANY] * n_s1,
        out_specs=[pl.BlockSpec((tm, 2 * D_MODEL), lambda s: (nt - 1 - s, 2)), mat, mat,
                   pl.BlockSpec((8, D_MODEL), lambda s: (0, 0))] + [ANY] * n_s1,
        out_shape=[jax.ShapeDtypeStruct(dproj.shape, dproj.dtype), mshape, mshape,
                   jax.ShapeDtypeStruct((8, D_MODEL), F32)] + S1_SHAPES,
        input_output_aliases={n_in: 0},
        scratch_shapes=[pltpu.VMEM((tm + 8, D_MODEL), F32), pltpu.VMEM((tm + 8, D_MODEL), F32),
                        pltpu.VMEM((tm, D_MODEL), F32), pltpu.VMEM((tm, D_MODEL), F32),
                        pltpu.VMEM((tm + 8, D_MODEL), F32),
                        pltpu.VMEM((1, D_MODEL), F32), pltpu.VMEM((1, D_MODEL), F32)] + _push_sems(n_s1),
        compiler_params=pltpu.CompilerParams(dimension_semantics=("arbitrary",), vmem_limit_bytes=VMEM_LIMIT,
                                             has_side_effects=True),
    )(dzl, hs, hs, cri, proj, proj, conv_w, lam, wa_g, wx_g, dproj, *s1_grads)


def _retention_bwd(dzr, o, proj, states, cos2, sin2, dec, dproj, ride):
    rows = dzr.shape[0]
    n_chunks = rows // CHUNK
    per_step = _chunks_per_step(n_chunks)
    n_steps = n_chunks // per_step
    tm = per_step * CHUNK
    n_r = ride.n

    def body(dzr_ref, o_ref, q_ref, k_ref, v_ref, g_ref, st_ref, c_ref, s_ref, dec_ref, dproj_in, *refs):
        del dproj_in
        dseg_ref = refs[n_r]
        dstate = refs[2 * n_r + 1]
        push = ride.push(refs[:n_r], refs[n_r + 1:2 * n_r + 1], refs[2 * n_r + 2:])

        @pl.when(pl.program_id(0) == 0)
        def _():
            push.start()
            dstate[...] = jnp.zeros_like(dstate)

        for h in range(HEADS):
            sl = slice(HEAD_DIM * h, HEAD_DIM * (h + 1))
            intra, qd, kd, cd = dec_ref[0, h], dec_ref[1, h], dec_ref[2, h], dec_ref[3, h]
            dst = dstate[h]
            for c in reversed(range(per_step)):
                rw = slice(CHUNK * c, CHUNK * (c + 1))
                cos_t, sin_t = c_ref[rw, :], s_ref[rw, :]
                o = o_ref[rw, sl].astype(F32)
                g = g_ref[rw, sl].astype(F32)
                dzr_v = dzr_ref[rw, sl].astype(F32)
                sg = _sigmoid(g)
                r = lax.rsqrt(jnp.mean(o * o, axis=-1, keepdims=True) + NORM_EPS)
                on = o * r
                dseg_ref[rw, 3 * D_MODEL + HEAD_DIM * h:3 * D_MODEL + HEAD_DIM * (h + 1)] = (
                    dzr_v * on * (sg * (1.0 + g * (1.0 - sg)))).astype(BF16)
                don = dzr_v * (g * sg)
                do = r * (don - on * jnp.mean(don * on, axis=-1, keepdims=True))
                dob = do.astype(BF16)

                qh = _rot(q_ref[rw, sl].astype(F32), cos_t, sin_t)
                kh = _rot(k_ref[rw, sl].astype(F32), cos_t, sin_t) * QK_SCALE
                qb, kb, vb = qh.astype(BF16), kh.astype(BF16), v_ref[rw, sl]
                s = (_dot_nt(qb, kb) * intra).astype(BF16)
                ds = (_dot_nt(dob, vb) * intra).astype(BF16)
                st_b = st_ref[c, h].astype(BF16)
                dst_b = dst.astype(BF16)
                dv = _dot_tn(s, dob) + _dot((kh * kd).astype(BF16), dst_b)
                dq = _dot(ds, kb) + _dot_nt(dob, st_b) * qd
                dk = _dot_tn(ds, qb) + _dot_nt(vb, dst_b) * kd
                dst = dst * cd + _dot_tn((qh * qd).astype(BF16), dob)
                dseg_ref[rw, 2 * D_MODEL + HEAD_DIM * h:2 * D_MODEL + HEAD_DIM * (h + 1)] = dv.astype(BF16)
                dseg_ref[rw, sl] = _rot_t(dq, cos_t, sin_t).astype(BF16)
                dseg_ref[rw, D_MODEL + HEAD_DIM * h:D_MODEL + HEAD_DIM * (h + 1)] = (
                    _rot_t(dk, cos_t, sin_t) * QK_SCALE).astype(BF16)
            dstate[h] = dst

        @pl.when(pl.program_id(0) == n_steps - 1)
        def _():
            push.wait()

    rev = lambda s: n_steps - 1 - s
    rowb = pl.BlockSpec((tm, D_MODEL), lambda s: (rev(s), 0))
    seg = lambda k: pl.BlockSpec((tm, D_MODEL), lambda s, k=k: (rev(s), k))
    tab = pl.BlockSpec((tm, HEAD_DIM), lambda s: (rev(s), 0))
    return pl.pallas_call(
        body, name="retention_bwd", grid=(n_steps,),
        in_specs=[rowb, rowb, seg(0), seg(1), seg(2), seg(3),
                  pl.BlockSpec((per_step, HEADS, HEAD_DIM, HEAD_DIM), lambda s: (rev(s), 0, 0, 0)), tab, tab,
                  pl.BlockSpec((4, HEADS, CHUNK, CHUNK), lambda s: (0, 0, 0, 0)), ANY] + ride.specs(),
        out_specs=[pl.BlockSpec((tm, 4 * D_MODEL), lambda s: (rev(s), 0))] + ride.specs(),
        out_shape=[jax.ShapeDtypeStruct(dproj.shape, dproj.dtype)] + ride.out_shapes,
        input_output_aliases={10: 0},
        scratch_shapes=[pltpu.VMEM((HEADS, HEAD_DIM, HEAD_DIM), F32)] + ride.scratch(),
        compiler_params=pltpu.CompilerParams(dimension_semantics=("arbitrary",), vmem_limit_bytes=VMEM_LIMIT,
                                             has_side_effects=True),
    )(dzr, o, proj, proj, proj, proj, states, cos2, sin2, dec, dproj, *ride.arrays)


S2_SHAPES = [
    jax.ShapeDtypeStruct((N_DEV, D_MODEL, D_MODEL), BF16),
    jax.ShapeDtypeStruct((N_DEV, LRU_BLOCKS, LRU_ROWS, LRU_BLOCK), F32),
    jax.ShapeDtypeStruct((N_DEV, LRU_BLOCKS, LRU_ROWS, LRU_BLOCK), F32),
]


def _s2_parts(ins, p):
    return [r.at[p] for r in ins]


def _in_proj_bwd(dproj, win_g, h0, norm_w, dh1, s2_grads):
    rows = h0.shape[0]
    tm = _tile(rows, 320)
    n_i = rows // tm
    n_s2 = len(s2_grads)

    def body(dseg_ref, w_ref, h0_ref, nw_ref, dh1_ref, *refs):
        s2_refs = refs[:n_s2]
        dh0_ref, dw_ref = refs[n_s2:n_s2 + 2]
        land_refs = refs[n_s2 + 2:2 * n_s2 + 2]
        send_sems, recv_sems, loc_sems = refs[2 * n_s2 + 2:]
        i = pl.program_id(0)
        push = _Push(lambda p: _s2_parts(s2_refs, p), lambda s: [r.at[s] for r in land_refs],
                     (send_sems, recv_sems, loc_sems), n_s2)

        @pl.when(i == 0)
        def _():
            push.start()
            dw_ref[...] = jnp.zeros_like(dw_ref)

        du = _dot_nt(dseg_ref[:, 0:D_MODEL], w_ref[0])
        for j in range(1, N_DEV):
            du = du + _dot_nt(dseg_ref[:, D_MODEL * j:D_MODEL * (j + 1)], w_ref[j])
        dx, dw = _rms_bwd(h0_ref[...], nw_ref[...], du)
        dw_ref[0:1, :] += dw
        dh0_ref[...] = dh1_ref[...] + dx

        @pl.when(i == n_i - 1)
        def _():
            push.wait()

    row = pl.BlockSpec((tm, D_MODEL), lambda i: (i, 0))
    vec = pl.BlockSpec((1, D_MODEL), lambda i: (0, 0))
    return pl.pallas_call(
        body, name="in_proj_bwd", grid=(n_i,),
        in_specs=[pl.BlockSpec((tm, N_DEV * D_MODEL), lambda i: (i, 0)),
                  pl.BlockSpec((N_DEV, D_MODEL, D_MODEL), lambda i: (0, 0, 0), pipeline_mode=pl.Buffered(1)),
                  row, vec, row] + [ANY] * n_s2,
        out_specs=[row, pl.BlockSpec((8, D_MODEL), lambda i: (0, 0))] + [ANY] * n_s2,
        out_shape=[jax.ShapeDtypeStruct((rows, D_MODEL), F32), jax.ShapeDtypeStruct((8, D_MODEL), F32)] + S2_SHAPES,
        scratch_shapes=_push_sems(n_s2),
        compiler_params=pltpu.CompilerParams(dimension_semantics=("arbitrary",),
                                             vmem_limit_bytes=VMEM_LIMIT, has_side_effects=True),
    )(dproj, win_g, h0, norm_w, dh1, *s2_grads)


def _adamw(g_slots, w, m, v):
    slots, rows, cols = g_slots.shape
    tr = rows
    for cand in (256, 128, 64, 32, 16, 8):
        if rows % cand == 0 and rows > cand:
            tr = cand
            break

    def body(g_ref, w_ref, m_ref, v_ref, go_ref, d_ref, mo_ref, vo_ref):
        g = g_ref[0].astype(F32)
        for s in range(1, slots):
            g = g + g_ref[s].astype(F32)
        m2 = ADAM_B1 * m_ref[...] + (1.0 - ADAM_B1) * g
        v2 = ADAM_B2 * v_ref[...] + (1.0 - ADAM_B2) * (g * g)
        m_hat = m2 / (1.0 - ADAM_B1 ** ADAM_STEP)
        v_hat = v2 / (1.0 - ADAM_B2 ** ADAM_STEP)
        go_ref[...] = g
        d_ref[...] = -ADAM_LR * (m_hat / (jnp.sqrt(v_hat) + ADAM_EPS) + ADAM_WD * w_ref[...])
        mo_ref[...] = m2
        vo_ref[...] = v2

    blk = pl.BlockSpec((tr, cols), lambda i: (i, 0))
    shape = jax.ShapeDtypeStruct((rows, cols), F32)
    return pl.pallas_call(
        body, name="adamw", grid=(rows // tr,),
        in_specs=[pl.BlockSpec((slots, tr, cols), lambda i: (0, i, 0)), blk, blk, blk],
        out_specs=[blk] * 4, out_shape=[shape] * 4,
        compiler_params=_cparams(("parallel",)),
    )(g_slots, w, m, v)


def _sum_slots(packs):
    slots, rows, cols = packs.shape

    def body(p_ref, o_ref):
        acc = p_ref[0]
        for s in range(1, slots):
            acc = acc + p_ref[s]
        o_ref[...] = acc

    return pl.pallas_call(
        body, name="sum_slots", out_shape=jax.ShapeDtypeStruct((rows, cols), F32),
        compiler_params=pltpu.CompilerParams(vmem_limit_bytes=VMEM_LIMIT),
    )(packs)


def _gather_small(small):
    shapes = [jax.ShapeDtypeStruct((N_DEV,) + small.shape, F32)]
    return _push_call("gather_small", [small], shapes,
                      lambda ins, p: list(ins), lambda outs, s: [r.at[s] for r in outs])[0]


def _share_pack(pack):
    shapes = [jax.ShapeDtypeStruct((N_DEV,) + pack.shape, F32)]
    return _push_call("share_pack", [pack], shapes,
                      lambda ins, p: list(ins), lambda outs, s: [r.at[s] for r in outs])[0]


PACK_MIX_NORM, PACK_CONV_W, PACK_CONV_B, PACK_BA, PACK_BX, PACK_LAM = 0, 8, 12, 13, 14, 15
PACK_FFN_NORM, PACK_SQ_ERR, PACK_FINAL_NORM, PACK_META = 16, 24, 25, 32


def kernel(x, meta_tokens, mix_norm_w, w_in, conv_w, conv_b, lru_wa, lru_ba, lru_wx, lru_bx, lru_lambda, w_branch_ret, w_branch_lru, w_out, ffn_norm_w, w_ffn_in, w_ffn_out, final_norm_w, loss_target, m_meta_tokens, m_mix_norm_w, m_w_in, m_conv_w, m_conv_b, m_lru_wa, m_lru_ba, m_lru_wx, m_lru_bx, m_lru_lambda, m_w_branch_ret, m_w_branch_lru, m_w_out, m_ffn_norm_w, m_w_ffn_in, m_w_ffn_out, m_final_norm_w, v_meta_tokens, v_mix_norm_w, v_w_in, v_conv_w, v_conv_b, v_lru_wa, v_lru_ba, v_lru_wx, v_lru_bx, v_lru_lambda, v_w_branch_ret, v_w_branch_lru, v_w_out, v_ffn_norm_w, v_w_ffn_in, v_w_ffn_out, v_final_norm_w):
    me = _my_index()
    pad4 = ((0, 4), (0, 0))
    fw = final_norm_w.reshape(1, D_MODEL)

    small = jnp.concatenate([meta_tokens, jnp.pad(conv_w[0], pad4)], axis=0)
    small_g = _gather_small(small)
    meta_full = small_g[:, :N_META].transpose(1, 0, 2).reshape(N_META, D_MODEL)
    conv_w_full = small_g[:, N_META:N_META + 4].transpose(1, 0, 2).reshape(4, D_MODEL)
    mixer_shards = [w_branch_ret[0].astype(BF16), w_branch_lru[0].astype(BF16), w_out[0].astype(BF16),
                    lru_wa[0].astype(BF16), lru_wx[0].astype(BF16)]
    wfi_shard = jnp.pad(w_ffn_in[0].astype(BF16), ((0, 0), (0, FFN_GROUP - FFN_SHARD)))
    own_slot = lambda ins, p: list(ins)
    part_of_owner = lambda ins, p: [r.at[p] for r in ins]

    rows = x.shape[1] + CHUNK
    h0 = jnp.concatenate([jnp.zeros((PAD_ROWS, D_MODEL), F32), meta_full, x[0]], axis=0)
    cos2, sin2 = _rope_tables(rows)
    dec = _retention_consts()

    proj, u, win_g = _in_proj(h0, mix_norm_w, w_in[0].astype(BF16), me.astype(jnp.int32).reshape(1))
    o, zr, states, wbr_g, wbl_g, wout_g, wa_g, wx_g = _retention_fwd(
        proj, cos2, sin2, dec, _mixer_weights_ride(mixer_shards))
    wbr, wbl, wout = (t.reshape(D_MODEL, D_MODEL) for t in (wbr_g, wbl_g, wout_g))
    wa_g, wx_g = _from_owners(wa_g), _from_owners(wx_g)
    gather_wfi = _Ride([wfi_shard], [jax.ShapeDtypeStruct((N_DEV, D_MODEL, FFN_GROUP), BF16)],
                       own_slot, _slot_of_sender, gather_by_chip=True)
    hs, zl, cri, wfi_g = _lru_fwd(proj, conv_w_full, conv_b, lru_ba, lru_bx, lru_lambda, wa_g, wx_g, gather_wfi)
    h1, yr, yl, mixed, wfo_g = _mix_fwd(zr, zl, proj, h0, wbr, wbl, wout, _wfo_ride(w_ffn_out[0].astype(BF16)))
    u2, g, up, act, dh2, red = _ffn_fwd_loss(h1, ffn_norm_w, wfi_g, wfo_g, fw, loss_target[0])

    d_wfo = _wgrad(act, dh2, FFN_GROUP, D_MODEL, BF16)[:, 0]
    dgu, dh1, dw_ffn_norm = _ffn_bwd(dh2, g, up, h1, ffn_norm_w, wfi_g, wfo_g)
    d_wfi = _wgrad(u2, dgu, D_MODEL, FFN_GROUP, BF16, b_halves=True)[0]
    d_wout = _wgrad(mixed, dh1, D_MODEL, D_MODEL, BF16)[0, 0]
    dyr, dyl, dproj, dzr, dzl = _mix_bwd(dh1, yr, yl, proj, wbr, wbl, wout)
    d_wbr = _wgrad(zr, dyr, D_MODEL, D_MODEL, BF16)[0, 0]
    d_wbl = _wgrad(zl, dyl, D_MODEL, D_MODEL, BF16)[0, 0]
    dproj, d_wa, d_wx, lru_small, r_fi, r_fo = _lru_bwd(
        dzl, hs, cri, proj, dproj, conv_w_full, lru_lambda, wa_g, wx_g, [d_wfi, d_wfo])
    mix_shape = jax.ShapeDtypeStruct((N_DEV, D_MODEL // N_DEV, D_MODEL), BF16)
    scatter_mix = _Ride([t.reshape(mix_shape.shape) for t in (d_wbr, d_wbl, d_wout)], [mix_shape] * 3,
                        part_of_owner, _slot_of_sender)
    dproj, r_br, r_bl, r_out = _retention_bwd(dzr, o, proj, states, cos2, sin2, dec, dproj, scatter_mix)
    d_win = _wgrad(u, dproj, D_MODEL, D_MODEL, BF16)[0]
    dh0, dw_mix_norm, r_in, r_wa, r_wx = _in_proj_bwd(dproj, win_g, h0, mix_norm_w, dh1,
                                                      [d_win, _by_owner(d_wa), _by_owner(d_wx)])
    grad_x = dh0[CHUNK:]

    pack = jnp.concatenate([dw_mix_norm, lru_small, dw_ffn_norm, red, dh0[PAD_ROWS:CHUNK]], axis=0)
    small_sum = _sum_slots(_share_pack(pack))
    loss = (0.5 / D_MODEL) * jnp.sum(small_sum[PACK_SQ_ERR])

    def big_update(slots, w, m, v):
        shape = w.shape
        w2, m2, v2 = (t.reshape(slots.shape[1:]) for t in (w, m, v))
        return [t.reshape(shape) for t in _adamw(slots, w2, m2, v2)]

    res = {}
    res["w_in"] = big_update(r_in, w_in, m_w_in, v_w_in)
    res["w_branch_ret"] = big_update(r_br, w_branch_ret, m_w_branch_ret, v_w_branch_ret)
    res["w_branch_lru"] = big_update(r_bl, w_branch_lru, m_w_branch_lru, v_w_branch_lru)
    res["w_out"] = big_update(r_out, w_out, m_w_out, v_w_out)
    res["w_ffn_in"] = big_update(r_fi[:, :, :FFN_SHARD], w_ffn_in, m_w_ffn_in, v_w_ffn_in)
    res["w_ffn_out"] = big_update(r_fo, w_ffn_out, m_w_ffn_out, v_w_ffn_out)
    res["lru_wa"] = big_update(r_wa.reshape(N_DEV, LRU_BLOCKS * LRU_ROWS, LRU_BLOCK), lru_wa, m_lru_wa, v_lru_wa)
    res["lru_wx"] = big_update(r_wx.reshape(N_DEV, LRU_BLOCKS * LRU_ROWS, LRU_BLOCK), lru_wx, m_lru_wx, v_lru_wx)

    col = me * HEAD_DIM
    g_meta = lax.dynamic_slice(small_sum, (PACK_META, col), (N_META, HEAD_DIM))
    g_conv = lax.dynamic_slice(small_sum, (PACK_CONV_W, col), (8, HEAD_DIM))
    small_names = ["mix_norm_w", "conv_b", "lru_ba", "lru_bx", "lru_lambda", "ffn_norm_w", "final_norm_w"]
    small_rows = [PACK_MIX_NORM, PACK_CONV_B, PACK_BA, PACK_BX, PACK_LAM, PACK_FFN_NORM, PACK_FINAL_NORM]
    small_w = [mix_norm_w, conv_b, lru_ba, lru_bx, lru_lambda, ffn_norm_w, fw]
    small_m = [m_mix_norm_w, m_conv_b, m_lru_ba, m_lru_bx, m_lru_lambda, m_ffn_norm_w, m_final_norm_w.reshape(1, -1)]
    small_v = [v_mix_norm_w, v_conv_b, v_lru_ba, v_lru_bx, v_lru_lambda, v_ffn_norm_w, v_final_norm_w.reshape(1, -1)]

    def pack_small(vec_list, meta_t, conv_t):
        return jnp.concatenate([t.reshape(8, HEAD_DIM) for t in vec_list] + [meta_t, jnp.pad(conv_t[0], pad4)], axis=0)

    g_small = jnp.concatenate([small_sum[r].reshape(8, HEAD_DIM) for r in small_rows] + [g_meta, g_conv], axis=0)
    outs_small = _adamw(g_small[None], pack_small(small_w, meta_tokens, conv_w),
                        pack_small(small_m, m_meta_tokens, m_conv_w), pack_small(small_v, v_meta_tokens, v_conv_w))
    for idx, name in enumerate(small_names):
        shape = final_norm_w.shape if name == "final_norm_w" else (1, D_MODEL)
        res[name] = [t[8 * idx:8 * idx + 8].reshape(shape) for t in outs_small]
    res["meta_tokens"] = [t[56:72] for t in outs_small]
    res["conv_w"] = [t[72:76].reshape(1, 4, HEAD_DIM) for t in outs_small]

    order = ["meta_tokens", "mix_norm_w", "w_in", "conv_w", "conv_b", "lru_wa", "lru_ba", "lru_wx", "lru_bx",
             "lru_lambda", "w_branch_ret", "w_branch_lru", "w_out", "ffn_norm_w", "w_ffn_in", "w_ffn_out",
             "final_norm_w"]
    out = [loss, grad_x[None]]
    for kind in range(4):
        out += [res[name][kind] for name in order]
    return tuple(out)
```

```python
import functools

import numpy as np
import jax
import jax.numpy as jnp
from jax import lax
from jax.experimental import pallas as pl
from jax.experimental.pallas import tpu as pltpu

F32 = jnp.float32
BF16 = jnp.bfloat16

D_MODEL = 1024
N_META = 16
CHUNK = 128
PAD_ROWS = CHUNK - N_META
HEADS = 8
HEAD_DIM = 128
ROPE_BASE = 10000.0
QK_SCALE = HEAD_DIM ** -0.5
LRU_BLOCKS = 4
LRU_BLOCK = 256
LRU_C = 8.0
FFN_HIDDEN = 2816
N_DEV = 8
FFN_SHARD = 2 * FFN_HIDDEN // N_DEV
FFN_GROUP = 768
FFN_GROUPS = 4
FFN_OUT_SHARD = FFN_HIDDEN // N_DEV
NORM_EPS = 1e-6

ADAM_LR = 0.001
ADAM_B1 = 0.9
ADAM_B2 = 0.999
ADAM_EPS = 1e-08
ADAM_WD = 0.01
ADAM_STEP = 10

VMEM_LIMIT = 56 * 1024 * 1024
MESH_ID = pl.DeviceIdType.MESH
ANY = pl.BlockSpec(memory_space=pl.ANY)


def _cparams(sem):
    return pltpu.CompilerParams(dimension_semantics=sem, vmem_limit_bytes=VMEM_LIMIT)


def _tile(rows, cap):
    t = cap - cap % 64
    while rows % t:
        t -= 64
    return t


def _dot(a, b):
    return jnp.dot(a, b, preferred_element_type=F32)


def _dot_nt(a, b):
    return lax.dot_general(a, b, (((1,), (1,)), ((), ())), preferred_element_type=F32)


def _dot_tn(a, b):
    return lax.dot_general(a, b, (((0,), (0,)), ((), ())), preferred_element_type=F32)


def _sigmoid(x):
    return 0.5 * jnp.tanh(0.5 * x) + 0.5


def _gelu_parts(x):
    k = 0.7978845608028654
    inner = k * (x + 0.044715 * x * x * x)
    t = jnp.tanh(inner)
    g = 0.5 * x * (1.0 + t)
    dg = 0.5 * (1.0 + t) + 0.5 * x * (1.0 - t * t) * k * (1.0 + 3.0 * 0.044715 * x * x)
    return g, dg


def _rot(x, cos2, sin2):
    return x * cos2 + pltpu.roll(x, HEAD_DIM // 2, 1) * sin2


def _rot_t(dx, cos2, sin2):
    return dx * cos2 - pltpu.roll(dx, HEAD_DIM // 2, 1) * sin2


def _rms_bwd(x, w, dy):
    rs = lax.rsqrt(jnp.mean(x * x, axis=-1, keepdims=True) + NORM_EPS)
    nh = x * rs
    dw = jnp.sum(dy * nh, axis=0, keepdims=True)
    dn = dy * w
    dx = rs * (dn - nh * jnp.mean(dn * nh, axis=-1, keepdims=True))
    return dx, dw


def _retention_consts():
    h = jnp.arange(HEADS, dtype=F32)
    log_g = jnp.log(1.0 - 2.0 ** (-5.0 - h))
    idx = jnp.arange(CHUNK, dtype=F32)
    diff = idx[:, None] - idx[None, :]
    intra = jnp.where(diff[None] >= 0, jnp.exp(jnp.maximum(diff, 0.0)[None] * log_g[:, None, None]), 0.0)
    q_decay = jnp.exp((idx + 1.0)[:, None] * log_g[None, :])
    k_decay = jnp.exp((CHUNK - 1.0 - idx)[:, None] * log_g[None, :])
    chunk_decay = jnp.exp(CHUNK * log_g)
    shape = (HEADS, CHUNK, CHUNK)
    qd = jnp.broadcast_to(q_decay.T[:, :, None], shape)
    kd = jnp.broadcast_to(k_decay.T[:, :, None], shape)
    cd = jnp.broadcast_to(chunk_decay[:, None, None], shape)
    return jnp.stack([intra, qd, kd, cd])


def _rope_tables(rows):
    pos = jnp.maximum(jnp.arange(rows) - PAD_ROWS, 0).astype(F32)
    inv_freq = ROPE_BASE ** (-jnp.arange(0, HEAD_DIM, 2, dtype=F32) / HEAD_DIM)
    ang = pos[:, None] * inv_freq[None, :]
    cos, sin = jnp.cos(ang), jnp.sin(ang)
    return jnp.concatenate([cos, cos], axis=1), jnp.concatenate([-sin, sin], axis=1)


def _my_index():
    return 4 * lax.axis_index("x") + 2 * lax.axis_index("y") + lax.axis_index("c")


def _peer(k):
    x, y, c = lax.axis_index("x"), lax.axis_index("y"), lax.axis_index("c")
    px = 1 - x if k & 4 else x
    py = 1 - y if k & 2 else y
    pc = 1 - c if k & 1 else c
    return (px, py, pc), 4 * px + 2 * py + pc


def _push_sems(n_arr):
    n_rem = (N_DEV - 1) * n_arr
    return [pltpu.SemaphoreType.DMA((n_rem,)), pltpu.SemaphoreType.DMA((n_rem,)), pltpu.SemaphoreType.DMA((n_arr,))]


class _Push:
    def __init__(self, send_part, land_slot, sems, n_arr):
        self.send_part, self.land_slot, self.n_arr = send_part, land_slot, n_arr
        self.send_sems, self.recv_sems, self.loc_sems = sems

    def _remote(self, k, a, src, dst, pos):
        idx = (k - 1) * self.n_arr + a
        return pltpu.make_async_remote_copy(src_ref=src, dst_ref=dst, send_sem=self.send_sems.at[idx],
                                            recv_sem=self.recv_sems.at[idx], device_id=pos, device_id_type=MESH_ID)

    def _outgoing(self):
        me = _my_index()
        land = self.land_slot(me)
        remote = []
        for k in range(1, N_DEV):
            pos, p = _peer(k)
            src = self.send_part(p)
            remote += [self._remote(k, a, src[a], land[a], pos) for a in range(self.n_arr)]
        own = self.send_part(me)
        local = [pltpu.make_async_copy(own[a], land[a], self.loc_sems.at[a]) for a in range(self.n_arr)]
        return remote, local

    def start(self):
        remote, local = self._outgoing()
        for cp in remote + local:
            cp.start()

    def wait_recv_from(self, k):
        own = self.send_part(_my_index())
        pos, p = _peer(k)
        land = self.land_slot(p)
        for a in range(self.n_arr):
            self._remote(k, a, own[a], land[a], pos).wait_recv()

    def wait_sends(self):
        remote, local = self._outgoing()
        for cp in remote:
            cp.wait_send()
        for cp in local:
            cp.wait()

    def wait(self):
        for k in range(1, N_DEV):
            self.wait_recv_from(k)
        self.wait_sends()


DIRECT = (1, 2, 4, 6)
RELAYED = (2, 4, 6)


def _gather_by_chip_sems(n_arr):
    direct, relayed = len(DIRECT) * n_arr, len(RELAYED) * n_arr
    return [pltpu.SemaphoreType.DMA((direct,)), pltpu.SemaphoreType.DMA((direct,)),
            pltpu.SemaphoreType.DMA((relayed,)), pltpu.SemaphoreType.DMA((relayed,)), pltpu.SemaphoreType.DMA((n_arr,))]


class _GatherByChip:
    def __init__(self, srcs, land_slot, sems, n_arr):
        self.srcs, self.land_slot, self.n_arr = srcs, land_slot, n_arr
        self.send_sems, self.recv_sems, self.relay_send_sems, self.relay_recv_sems, self.loc_sems = sems

    def _direct(self, k, a, slot):
        idx = DIRECT.index(k) * self.n_arr + a
        return pltpu.make_async_remote_copy(src_ref=self.srcs[a], dst_ref=self.land_slot(slot)[a],
                                            send_sem=self.send_sems.at[idx], recv_sem=self.recv_sems.at[idx],
                                            device_id=_peer(k)[0], device_id_type=MESH_ID)

    def _relay(self, q, a, slot):
        idx = RELAYED.index(q) * self.n_arr + a
        block = self.land_slot(slot)[a]
        return pltpu.make_async_remote_copy(src_ref=block, dst_ref=block, send_sem=self.relay_send_sems.at[idx],
                                            recv_sem=self.relay_recv_sems.at[idx], device_id=_peer(1)[0],
                                            device_id_type=MESH_ID)

    def _own(self, a):
        return pltpu.make_async_copy(self.srcs[a], self.land_slot(_my_index())[a], self.loc_sems.at[a])

    def start(self):
        me = _my_index()
        for k in DIRECT:
            for a in range(self.n_arr):
                self._direct(k, a, me).start()
        for a in range(self.n_arr):
            self._own(a).start()

    def relay(self):
        for q in RELAYED:
            p = _peer(q)[1]
            for a in range(self.n_arr):
                self._direct(q, a, p).wait_recv()
                self._relay(q, a, p).start()

    def wait(self):
        me = _my_index()
        for a in range(self.n_arr):
            self._direct(1, a, _peer(1)[1]).wait_recv()
        for q in RELAYED:
            for a in range(self.n_arr):
                self._relay(q, a, _peer(q + 1)[1]).wait_recv()
        for k in DIRECT:
            for a in range(self.n_arr):
                self._direct(k, a, me).wait_send()
        for q in RELAYED:
            for a in range(self.n_arr):
                self._relay(q, a, _peer(q)[1]).wait_send()
        for a in range(self.n_arr):
            self._own(a).wait()


class _Ride:
    def __init__(self, arrays, out_shapes, send_part, land_slot, zero_dsts=None, zero_shape=None, n_zero=0,
                 gather_by_chip=False):
        self.arrays, self.out_shapes = list(arrays), list(out_shapes)
        self.send_part, self.land_slot, self.n = send_part, land_slot, len(arrays)
        self.zero_dsts, self.zero_shape, self.n_zero = zero_dsts, zero_shape, n_zero
        self.gather_by_chip = gather_by_chip

    def specs(self):
        return [ANY] * self.n

    def scratch(self):
        extra = [pltpu.SemaphoreType.DMA((self.n_zero,)), pltpu.VMEM(self.zero_shape, BF16)] if self.n_zero else []
        sems = _gather_by_chip_sems(self.n) if self.gather_by_chip else _push_sems(self.n)
        return sems + extra

    def push(self, in_refs, out_refs, scratch):
        ride = self
        n_sems = 5 if self.gather_by_chip else 3
        land = lambda s: ride.land_slot(out_refs, s)
        if self.gather_by_chip:
            push = _GatherByChip(list(in_refs), land, tuple(scratch[:n_sems]), self.n)
        else:
            push = _Push(lambda p: ride.send_part(in_refs, p), land, tuple(scratch[:n_sems]), self.n)

        class Both:
            def _fills(self):
                if not ride.n_zero:
                    return []
                zsems, zbuf = scratch[n_sems], scratch[n_sems + 1]
                return [pltpu.make_async_copy(zbuf, dst, zsems.at[z]) for z, dst in enumerate(ride.zero_dsts(out_refs))]

            def start(self):
                push.start()
                if ride.n_zero:
                    scratch[n_sems + 1][...] = jnp.zeros(ride.zero_shape, BF16)
                for cp in self._fills():
                    cp.start()

            def relay(self):
                if ride.gather_by_chip:
                    push.relay()

            def wait(self):
                push.wait()
                for cp in self._fills():
                    cp.wait()

        return Both()


def _slot_of_sender(out_refs, s):
    return [r.at[s] for r in out_refs]


def _push_call(name, arrays, out_shapes, send_part, land_slot):
    n_arr = len(arrays)

    def body(*refs):
        ins, outs, sems = refs[:n_arr], refs[n_arr:2 * n_arr], refs[2 * n_arr:]
        push = _Push(lambda p: send_part(ins, p), lambda s: land_slot(outs, s), sems, n_arr)
        push.start()
        push.wait()

    return pl.pallas_call(
        body, name=name, in_specs=[ANY] * n_arr, out_specs=[ANY] * n_arr, out_shape=out_shapes,
        scratch_shapes=_push_sems(n_arr), compiler_params=pltpu.CompilerParams(has_side_effects=True),
    )(*arrays)


LRU_ROWS = LRU_BLOCK // N_DEV
FFN_PAD_ROWS = FFN_GROUP - 2 * FFN_OUT_SHARD


def _half_rows(d):
    return pl.ds(pl.multiple_of((d % 2) * FFN_OUT_SHARD, 16), FFN_OUT_SHARD)


MIXER_SHAPES = [
    jax.ShapeDtypeStruct((N_DEV, D_MODEL // N_DEV, D_MODEL), BF16),
    jax.ShapeDtypeStruct((N_DEV, D_MODEL // N_DEV, D_MODEL), BF16),
    jax.ShapeDtypeStruct((N_DEV, D_MODEL // N_DEV, D_MODEL), BF16),
    jax.ShapeDtypeStruct((N_DEV, LRU_BLOCKS, LRU_ROWS, LRU_BLOCK), BF16),
    jax.ShapeDtypeStruct((N_DEV, LRU_BLOCKS, LRU_ROWS, LRU_BLOCK), BF16),
]


def _by_owner(t):
    return t.reshape(LRU_BLOCKS, N_DEV, LRU_ROWS, LRU_BLOCK).transpose(1, 0, 2, 3)


def _from_owners(t):
    return t.transpose(1, 0, 2, 3).reshape(LRU_BLOCKS, LRU_BLOCK, LRU_BLOCK)


def _mixer_weights_ride(shards):
    return _Ride(shards, MIXER_SHAPES, lambda ins, p: list(ins), _slot_of_sender, gather_by_chip=True)


def _wfo_ride(shard):
    zero_dsts = lambda outs: [outs[0].at[g, pl.ds(2 * FFN_OUT_SHARD, FFN_PAD_ROWS), :] for g in range(FFN_GROUPS)]
    return _Ride([shard], [jax.ShapeDtypeStruct((FFN_GROUPS, FFN_GROUP, D_MODEL), BF16)], lambda ins, p: list(ins),
                 lambda outs, d: [outs[0].at[d // 2, _half_rows(d), :]], zero_dsts, (FFN_PAD_ROWS, D_MODEL), FFN_GROUPS,
                 gather_by_chip=True)


def _arrival_rank_to_relation(jj):
    return jnp.where(jj == 3, 4, jnp.where(jj == 4, 3, jj))


def _in_proj(h0, norm_w, win_shard, me_arr):
    rows = h0.shape[0]
    tm = _tile(rows, 1664)
    n_i = rows // tm

    direct, relayed = DIRECT, RELAYED

    def body(me_ref, h_ref, nw_ref, wsh_ref, proj_ref, u_ref, wing_ref, u_all, wbuf, copy_sem,
             send_sems, recv_sems, relay_send_sems, relay_recv_sems, own_sem):
        del me_ref
        jj, i = pl.program_id(0), pl.program_id(1)
        me = _my_index()
        sibling = _peer(1)[0]

        def direct_copy(k, slot):
            n = direct.index(k)
            return pltpu.make_async_remote_copy(src_ref=wsh_ref, dst_ref=wing_ref.at[slot], send_sem=send_sems.at[n],
                                                recv_sem=recv_sems.at[n], device_id=_peer(k)[0], device_id_type=MESH_ID)

        def relay_copy(q, slot):
            n = relayed.index(q)
            return pltpu.make_async_remote_copy(src_ref=wing_ref.at[slot], dst_ref=wing_ref.at[slot],
                                                send_sem=relay_send_sems.at[n], recv_sem=relay_recv_sems.at[n],
                                                device_id=sibling, device_id_type=MESH_ID)

        own_slot = pltpu.make_async_copy(wsh_ref, wing_ref.at[me], own_sem)

        @pl.when(jnp.logical_and(jj == 0, i == 0))
        def _():
            for k in direct:
                direct_copy(k, me).start()
            own_slot.start()
            own = pltpu.make_async_copy(wsh_ref, wbuf, copy_sem)
            own.start()
            own.wait()

        for k in range(1, N_DEV):
            rank = {3: 4, 4: 3}.get(k, k)

            @pl.when(jnp.logical_and(jj == rank, i == 0))
            def _(k=k):
                p = _peer(k)[1]
                if k in direct:
                    direct_copy(k, p).wait_recv()
                    if k in relayed:
                        relay_copy(k, p).start()
                else:
                    relay_copy(k - 1, p).wait_recv()
                landed = pltpu.make_async_copy(wing_ref.at[p], wbuf, copy_sem)
                landed.start()
                landed.wait()

        rows_i = pl.ds(pl.multiple_of(i * tm, tm), tm)

        @pl.when(jj == 0)
        def _():
            x = h_ref[...]
            rs = lax.rsqrt(jnp.mean(x * x, axis=-1, keepdims=True) + NORM_EPS)
            u = (x * rs * nw_ref[...]).astype(BF16)
            u_all[rows_i, :] = u
            u_ref[...] = u
        proj_ref[...] = _dot(u_all[rows_i, :], wbuf[...]).astype(BF16)

        @pl.when(jnp.logical_and(jj == N_DEV - 1, i == n_i - 1))
        def _():
            for k in direct:
                direct_copy(k, me).wait_send()
            for q in relayed:
                relay_copy(q, _peer(q)[1]).wait_send()
            own_slot.wait()

    first_pass = lambda jj, i: jnp.where(jj == 0, i, n_i - 1)
    grid_spec = pltpu.PrefetchScalarGridSpec(
        num_scalar_prefetch=1, grid=(N_DEV, n_i),
        in_specs=[pl.BlockSpec((tm, D_MODEL), lambda jj, i, me: (first_pass(jj, i), 0)),
                  pl.BlockSpec((1, D_MODEL), lambda jj, i, me: (0, 0)), ANY],
        out_specs=[pl.BlockSpec((tm, D_MODEL), lambda jj, i, me: (i, me[0] ^ _arrival_rank_to_relation(jj))),
                   pl.BlockSpec((tm, D_MODEL), lambda jj, i, me: (first_pass(jj, i), 0)), ANY],
        scratch_shapes=[pltpu.VMEM((rows, D_MODEL), BF16), pltpu.VMEM((D_MODEL, D_MODEL), BF16),
                        pltpu.SemaphoreType.DMA(()),
                        pltpu.SemaphoreType.DMA((len(direct),)), pltpu.SemaphoreType.DMA((len(direct),)),
                        pltpu.SemaphoreType.DMA((len(relayed),)), pltpu.SemaphoreType.DMA((len(relayed),)),
                        pltpu.SemaphoreType.DMA(())])
    return pl.pallas_call(
        body, name="in_proj", grid_spec=grid_spec,
        out_shape=[jax.ShapeDtypeStruct((rows, N_DEV * D_MODEL), BF16),
                   jax.ShapeDtypeStruct((rows, D_MODEL), BF16),
                   jax.ShapeDtypeStruct((N_DEV, D_MODEL, D_MODEL), BF16)],
        compiler_params=pltpu.CompilerParams(dimension_semantics=("arbitrary", "arbitrary"),
                                             vmem_limit_bytes=VMEM_LIMIT, has_side_effects=True),
    )(me_arr, h0, norm_w, win_shard)


def _seg_spec(rows_per_block, seg):
    return pl.BlockSpec((rows_per_block, D_MODEL), lambda n, seg=seg: (n, seg))


def _chunks_per_step(n_chunks):
    return next(c for c in (5, 3, 2, 1) if n_chunks % c == 0)


def _retention_fwd(proj, cos2, sin2, dec, ride):
    rows = proj.shape[0]
    n_chunks = rows // CHUNK
    per_step = _chunks_per_step(n_chunks)
    n_steps = n_chunks // per_step
    tm = per_step * CHUNK
    n_r = ride.n

    def body(q_ref, k_ref, v_ref, g_ref, c_ref, s_ref, dec_ref, *refs):
        o_ref, zr_ref, st_ref = refs[n_r:n_r + 3]
        state = refs[2 * n_r + 3]
        push = ride.push(refs[:n_r], refs[n_r + 3:2 * n_r + 3], refs[2 * n_r + 4:])

        @pl.when(pl.program_id(0) == 0)
        def _():
            push.start()
            state[...] = jnp.zeros_like(state)

        for h in range(HEADS):
            sl = slice(HEAD_DIM * h, HEAD_DIM * (h + 1))
            st = state[h]
            for c in range(per_step):
                rw = slice(CHUNK * c, CHUNK * (c + 1))
                cos_t, sin_t = c_ref[rw, :], s_ref[rw, :]
                qh = _rot(q_ref[rw, sl].astype(F32), cos_t, sin_t)
                kh = _rot(k_ref[rw, sl].astype(F32), cos_t, sin_t) * QK_SCALE
                qb, kb, vb = qh.astype(BF16), kh.astype(BF16), v_ref[rw, sl]
                s = _dot_nt(qb, kb) * dec_ref[0, h]
                st_ref[c, h] = st
                o = _dot(s.astype(BF16), vb) + _dot(qb, st.astype(BF16)) * dec_ref[1, h]
                st = st * dec_ref[3, h] + _dot_tn((kh * dec_ref[2, h]).astype(BF16), vb)
                o_ref[rw, sl] = o.astype(BF16)
                r = lax.rsqrt(jnp.mean(o * o, axis=-1, keepdims=True) + NORM_EPS)
                g = g_ref[rw, sl].astype(F32)
                zr_ref[rw, sl] = (g * _sigmoid(g) * (o * r)).astype(BF16)
            state[h] = st

        @pl.when(pl.program_id(0) == n_steps // 2)
        def _():
            push.relay()

        @pl.when(pl.program_id(0) == n_steps - 1)
        def _():
            push.wait()

    tab = pl.BlockSpec((tm, HEAD_DIM), lambda n: (n, 0))
    return pl.pallas_call(
        body, name="retention_fwd", grid=(n_steps,),
        in_specs=[_seg_spec(tm, 0), _seg_spec(tm, 1), _seg_spec(tm, 2), _seg_spec(tm, 3), tab, tab,
                  pl.BlockSpec((4, HEADS, CHUNK, CHUNK), lambda n: (0, 0, 0, 0))] + ride.specs(),
        out_specs=[pl.BlockSpec((tm, D_MODEL), lambda n: (n, 0)),
                   pl.BlockSpec((tm, D_MODEL), lambda n: (n, 0)),
                   pl.BlockSpec((per_step, HEADS, HEAD_DIM, HEAD_DIM), lambda n: (n, 0, 0, 0))] + ride.specs(),
        out_shape=[jax.ShapeDtypeStruct((rows, D_MODEL), BF16),
                   jax.ShapeDtypeStruct((rows, D_MODEL), BF16),
                   jax.ShapeDtypeStruct((n_chunks, HEADS, HEAD_DIM, HEAD_DIM), F32)] + ride.out_shapes,
        scratch_shapes=[pltpu.VMEM((HEADS, HEAD_DIM, HEAD_DIM), F32)] + ride.scratch(),
        compiler_params=pltpu.CompilerParams(dimension_semantics=("arbitrary",), vmem_limit_bytes=VMEM_LIMIT,
                                             has_side_effects=True),
    )(proj, proj, proj, proj, cos2, sin2, dec, *ride.arrays)


def _lru_gates(c, ba, bx, wa_ref, wx_ref):
    pre_r, pre_i = [], []
    for g in range(LRU_BLOCKS):
        cg = c[:, LRU_BLOCK * g:LRU_BLOCK * (g + 1)].astype(BF16)
        pre_r.append(_dot(cg, wa_ref[g]))
        pre_i.append(_dot(cg, wx_ref[g]))
    return _sigmoid(jnp.concatenate(pre_r, axis=1) + ba), _sigmoid(jnp.concatenate(pre_i, axis=1) + bx)


def _lru_decay(r, lam):
    sp = jnp.maximum(-lam, 0.0) + jnp.log(1.0 + jnp.exp(-jnp.abs(lam)))
    log_a = -LRU_C * r * sp
    a = jnp.exp(log_a)
    one_minus_a2 = -jnp.tanh(log_a) * (a * a + 1.0)
    inv_mult = lax.rsqrt(jnp.maximum(one_minus_a2, 1e-30))
    return a, one_minus_a2 * inv_mult, inv_mult, sp


def _conv_taps(xbuf, tm, cw_ref, cb_ref):
    c = cb_ref[...] + cw_ref[3:4, :] * xbuf[8:8 + tm, :]
    for back in (1, 2, 3):
        c = c + cw_ref[3 - back:4 - back, :] * xbuf[8 - back:8 - back + tm, :]
    return c


def _lru_fwd(proj, conv_w, conv_b, ba, bx, lam, wa_g, wx_g, ride):
    rows = proj.shape[0]
    tm = _tile(rows, 320)
    n_t = rows // tm
    n_r = ride.n

    def body(x_ref, gt_ref, cw_ref, cb_ref, ba_ref, bx_ref, lam_ref, wa_ref, wx_ref, *refs):
        hs_ref, zl_ref, cri_ref = refs[n_r:n_r + 3]
        xbuf, abuf, ubuf, hcar = refs[2 * n_r + 3:2 * n_r + 7]
        push = ride.push(refs[:n_r], refs[n_r + 3:2 * n_r + 3], refs[2 * n_r + 7:])
        i = pl.program_id(0)

        @pl.when(i == 0)
        def _():
            push.start()
            xbuf[0:8, :] = jnp.zeros((8, D_MODEL), F32)
            hcar[...] = jnp.zeros_like(hcar)

        xbuf[8:8 + tm, :] = x_ref[...].astype(F32)
        c = _conv_taps(xbuf, tm, cw_ref, cb_ref)
        xbuf[0:8, :] = xbuf[tm:tm + 8, :]
        r, ig = _lru_gates(c, ba_ref[...], bx_ref[...], wa_ref, wx_ref)
        a, mult, _, _ = _lru_decay(r, lam_ref[...])
        cri_ref[0] = c.astype(BF16)
        cri_ref[1] = r.astype(BF16)
        cri_ref[2] = ig.astype(BF16)
        row = i * tm + lax.broadcasted_iota(jnp.int32, (tm, 1), 0)
        abuf[...] = a
        ubuf[...] = jnp.where(row >= PAD_ROWS, mult * (ig * c), 0.0)

        sub = lax.broadcasted_iota(jnp.int32, (8, D_MODEL), 0)

        def block(b, carry):
            off = pl.multiple_of(b * 8, 8)
            av, uv = abuf[pl.ds(off, 8), :], ubuf[pl.ds(off, 8), :]
            for s in (1, 2, 4):
                us = jnp.where(sub >= s, pltpu.roll(uv, s, 0), 0.0)
                as_ = jnp.where(sub >= s, pltpu.roll(av, s, 0), 1.0)
                uv = uv + av * us
                av = av * as_
            hv = uv + av * carry
            ubuf[pl.ds(off, 8), :] = hv
            return hv[7:8, :]

        hcar[...] = lax.fori_loop(0, tm // 8, block, hcar[...])
        gl, _ = _gelu_parts(gt_ref[...].astype(F32))
        hs = ubuf[...]
        hs_ref[...] = hs.astype(BF16)
        zl_ref[...] = (gl * hs).astype(BF16)

        @pl.when(i == n_t // 2)
        def _():
            push.relay()

        @pl.when(i == n_t - 1)
        def _():
            push.wait()

    vec = pl.BlockSpec((1, D_MODEL), lambda i: (0, 0))
    mat = pl.BlockSpec((LRU_BLOCKS, LRU_BLOCK, LRU_BLOCK), lambda i: (0, 0, 0))
    row = pl.BlockSpec((tm, D_MODEL), lambda i: (i, 0))
    return pl.pallas_call(
        body, name="lru_fwd", grid=(n_t,),
        in_specs=[_seg_spec(tm, 4), _seg_spec(tm, 5), pl.BlockSpec((4, D_MODEL), lambda i: (0, 0)),
                  vec, vec, vec, vec, mat, mat] + ride.specs(),
        out_specs=[row, row, pl.BlockSpec((3, tm, D_MODEL), lambda i: (0, i, 0))] + ride.specs(),
        out_shape=[jax.ShapeDtypeStruct((rows, D_MODEL), BF16)] * 2
        + [jax.ShapeDtypeStruct((3, rows, D_MODEL), BF16)] + ride.out_shapes,
        scratch_shapes=[pltpu.VMEM((tm + 8, D_MODEL), F32), pltpu.VMEM((tm, D_MODEL), F32),
                        pltpu.VMEM((tm, D_MODEL), F32), pltpu.VMEM((1, D_MODEL), F32)] + ride.scratch(),
        compiler_params=pltpu.CompilerParams(dimension_semantics=("arbitrary",), vmem_limit_bytes=VMEM_LIMIT,
                                             has_side_effects=True),
    )(proj, proj, conv_w, conv_b, ba, bx, lam, wa_g, wx_g, *ride.arrays)


def _mix_fwd(zr, zl, proj, h0, wbr, wbl, wout, ride):
    rows = h0.shape[0]
    tm = _tile(rows, 640)
    n_t = rows // tm
    n_r = ride.n

    def body(zr_ref, zl_ref, ga_ref, gb_ref, h0_ref, wbr_ref, wbl_ref, wo_ref, *refs):
        h1_ref, yr_ref, yl_ref, mx_ref = refs[n_r:n_r + 4]
        push = ride.push(refs[:n_r], refs[n_r + 4:2 * n_r + 4], refs[2 * n_r + 4:])

        @pl.when(pl.program_id(0) == 0)
        def _():
            push.start()

        yr = _dot(zr_ref[...], wbr_ref[...])
        yl = _dot(zl_ref[...], wbl_ref[...])
        mixed = (_sigmoid(ga_ref[...].astype(F32)) * yr + _sigmoid(gb_ref[...].astype(F32)) * yl).astype(BF16)
        yr_ref[...] = yr.astype(BF16)
        yl_ref[...] = yl.astype(BF16)
        mx_ref[...] = mixed
        h1_ref[...] = h0_ref[...] + _dot(mixed, wo_ref[...])

        @pl.when(pl.program_id(0) == n_t // 2)
        def _():
            push.relay()

        @pl.when(pl.program_id(0) == n_t - 1)
        def _():
            push.wait()

    row = pl.BlockSpec((tm, D_MODEL), lambda i: (i, 0))
    wsp = pl.BlockSpec((D_MODEL, D_MODEL), lambda i: (0, 0))
    return pl.pallas_call(
        body, name="mix_fwd", grid=(n_t,),
        in_specs=[row, row, _seg_spec(tm, 6), _seg_spec(tm, 7), row, wsp, wsp, wsp] + ride.specs(),
        out_specs=[row, row, row, row] + ride.specs(),
        out_shape=[jax.ShapeDtypeStruct((rows, D_MODEL), F32)] + [jax.ShapeDtypeStruct((rows, D_MODEL), BF16)] * 3
        + ride.out_shapes,
        scratch_shapes=ride.scratch(),
        compiler_params=pltpu.CompilerParams(dimension_semantics=("arbitrary",), vmem_limit_bytes=VMEM_LIMIT,
                                             has_side_effects=True),
    )(zr, zl, proj, proj, h0, wbr, wbl, wout, *ride.arrays)


def _ffn_fwd_loss(h1, norm_w, wfi_g, wfo_g, final_w, target):
    rows = h1.shape[0]
    tm = _tile(rows, 320)
    piece = 64
    n_piece = tm // piece

    def body(h1_ref, nw_ref, wfi_ref, wfo_ref, fw_ref, *refs):
        t_refs = refs[:n_piece]
        u2_ref, g_ref, up_ref, act_ref, dh2_ref, red_ref = refs[n_piece:]
        i = pl.program_id(0)

        @pl.when(i == 0)
        def _():
            red_ref[...] = jnp.zeros_like(red_ref)

        x = h1_ref[...]
        rs = lax.rsqrt(jnp.mean(x * x, axis=-1, keepdims=True) + NORM_EPS)
        u2 = (x * rs * nw_ref[...]).astype(BF16)
        u2_ref[...] = u2
        ffn = None
        for d in range(FFN_GROUPS):
            cols = slice(FFN_GROUP * d, FFN_GROUP * (d + 1))
            g = _dot(u2, wfi_ref[d])
            up = _dot(u2, wfi_ref[d + FFN_GROUPS])
            act = (g * _sigmoid(g) * up).astype(BF16)
            g_ref[:, cols] = g.astype(BF16)
            up_ref[:, cols] = up.astype(BF16)
            act_ref[:, cols] = act
            part = _dot(act, wfo_ref[d])
            ffn = part if ffn is None else ffn + part

        h2 = x + ffn
        rs = lax.rsqrt(jnp.mean(h2 * h2, axis=-1, keepdims=True) + NORM_EPS)
        nh = h2 * rs
        fw = fw_ref[...]
        row = i * tm + lax.broadcasted_iota(jnp.int32, (tm, 1), 0)
        tgt = jnp.concatenate([t[...] for t in t_refs], axis=0)
        diff = jnp.where(row >= CHUNK, nh * fw - tgt, 0.0)
        dy = diff * (1.0 / D_MODEL)
        red_ref[0:1, :] += jnp.sum(diff * diff, axis=0, keepdims=True)
        red_ref[1:2, :] += jnp.sum(dy * nh, axis=0, keepdims=True)
        dn = dy * fw
        dh2_ref[...] = rs * (dn - nh * jnp.mean(dn * nh, axis=-1, keepdims=True))

    row = pl.BlockSpec((tm, D_MODEL), lambda i: (i, 0))
    vec = pl.BlockSpec((1, D_MODEL), lambda i: (0, 0))
    hid = pl.BlockSpec((tm, FFN_GROUPS * FFN_GROUP), lambda i: (i, 0))
    hid_shape = jax.ShapeDtypeStruct((rows, FFN_GROUPS * FFN_GROUP), BF16)
    resident = dict(pipeline_mode=pl.Buffered(1))
    head_pieces = CHUNK // piece
    t_specs = [pl.BlockSpec((piece, D_MODEL), lambda i, k=k: (jnp.maximum(i * n_piece + k - head_pieces, 0), 0))
               for k in range(n_piece)]
    return pl.pallas_call(
        body, name="ffn_fwd_loss", grid=(rows // tm,),
        in_specs=[row, vec,
                  pl.BlockSpec((2 * FFN_GROUPS, D_MODEL, FFN_GROUP), lambda i: (0, 0, 0), **resident),
                  pl.BlockSpec((FFN_GROUPS, FFN_GROUP, D_MODEL), lambda i: (0, 0, 0), **resident),
                  vec] + t_specs,
        out_specs=[row, hid, hid, hid, row, pl.BlockSpec((8, D_MODEL), lambda i: (0, 0))],
        out_shape=[jax.ShapeDtypeStruct((rows, D_MODEL), BF16), hid_shape, hid_shape, hid_shape,
                   jax.ShapeDtypeStruct((rows, D_MODEL), F32), jax.ShapeDtypeStruct((8, D_MODEL), F32)],
        compiler_params=_cparams(("arbitrary",)),
    )(h1, norm_w, wfi_g, wfo_g, final_w, *([target] * n_piece))


def _wgrad(a, b, ka, tn, out_dtype, b_halves=False):
    rows = a.shape[0]
    na = a.shape[1] // ka
    tm = _tile(rows, 1664)
    nm = rows // tm
    if b_halves:
        per_half = b.shape[2] // tn
        nb = 2 * per_half
        b_spec = pl.BlockSpec((None, tm, tn), lambda p, q, m: (q // per_half, m, q % per_half))
    else:
        nb = b.shape[1] // tn
        b_spec = pl.BlockSpec((tm, tn), lambda p, q, m: (m, q))

    def body(a_ref, b_ref, o_ref, acc):
        m = pl.program_id(2)

        @pl.when(m == 0)
        def _():
            acc[...] = jnp.zeros_like(acc)

        acc[...] += _dot_tn(a_ref[...].astype(BF16), b_ref[...].astype(BF16))

        @pl.when(m == nm - 1)
        def _():
            o_ref[...] = acc[...].astype(out_dtype)

    return pl.pallas_call(
        body, name="wgrad", grid=(na, nb, nm),
        in_specs=[pl.BlockSpec((tm, ka), lambda p, q, m: (m, p)), b_spec],
        out_specs=pl.BlockSpec((None, None, ka, tn), lambda p, q, m: (p, q, 0, 0)),
        out_shape=jax.ShapeDtypeStruct((na, nb, ka, tn), out_dtype),
        scratch_shapes=[pltpu.VMEM((ka, tn), F32)],
        compiler_params=_cparams(("parallel", "parallel", "arbitrary")),
    )(a, b)


def _ffn_bwd(dh2, g, up, h1, norm_w, wfi_g, wfo_g):
    rows = h1.shape[0]
    tm = _tile(rows, 320)

    def body(dh2_ref, g_ref, up_ref, h1_ref, nw_ref, wfi_ref, wfo_ref, dgu_ref, dh1_ref, dw_ref):
        @pl.when(pl.program_id(0) == 0)
        def _():
            dw_ref[...] = jnp.zeros_like(dw_ref)

        dh2 = dh2_ref[...]
        dh2_b = dh2.astype(BF16)
        du2 = None
        for d in range(FFN_GROUPS):
            cols = slice(FFN_GROUP * d, FFN_GROUP * (d + 1))
            dact = _dot_nt(dh2_b, wfo_ref[d])
            gv, uv = g_ref[:, cols].astype(F32), up_ref[:, cols].astype(F32)
            sg = _sigmoid(gv)
            dg = (dact * uv * (sg * (1.0 + gv * (1.0 - sg)))).astype(BF16)
            dup = (dact * (gv * sg)).astype(BF16)
            dgu_ref[0, :, cols] = dg
            dgu_ref[1, :, cols] = dup
            part = _dot_nt(dg, wfi_ref[d]) + _dot_nt(dup, wfi_ref[d + FFN_GROUPS])
            du2 = part if du2 is None else du2 + part
        dx, dw = _rms_bwd(h1_ref[...], nw_ref[...], du2)
        dw_ref[0:1, :] += dw
        dh1_ref[...] = dh2 + dx

    row = pl.BlockSpec((tm, D_MODEL), lambda i: (i, 0))
    vec = pl.BlockSpec((1, D_MODEL), lambda i: (0, 0))
    hid = pl.BlockSpec((tm, FFN_GROUPS * FFN_GROUP), lambda i: (i, 0))
    resident = dict(pipeline_mode=pl.Buffered(1))
    return pl.pallas_call(
        body, name="ffn_bwd", grid=(rows // tm,),
        in_specs=[row, hid, hid, row, vec,
                  pl.BlockSpec((2 * FFN_GROUPS, D_MODEL, FFN_GROUP), lambda i: (0, 0, 0), **resident),
                  pl.BlockSpec((FFN_GROUPS, FFN_GROUP, D_MODEL), lambda i: (0, 0, 0), **resident)],
        out_specs=[pl.BlockSpec((2, tm, FFN_GROUPS * FFN_GROUP), lambda i: (0, i, 0)), row,
                   pl.BlockSpec((8, D_MODEL), lambda i: (0, 0))],
        out_shape=[jax.ShapeDtypeStruct((2, rows, FFN_GROUPS * FFN_GROUP), BF16),
                   jax.ShapeDtypeStruct((rows, D_MODEL), F32), jax.ShapeDtypeStruct((8, D_MODEL), F32)],
        compiler_params=_cparams(("arbitrary",)),
    )(dh2, g, up, h1, norm_w, wfi_g, wfo_g)


def _mix_bwd(dh1, yr, yl, proj, wbr, wbl, wout):
    rows = dh1.shape[0]
    tm = _tile(rows, 640)

    def body(dh1_ref, yr_ref, yl_ref, ga_ref, gb_ref, wbr_ref, wbl_ref, wo_ref,
             dyr_ref, dyl_ref, dseg_ref, dzr_ref, dzl_ref):
        dmix = _dot_nt(dh1_ref[...].astype(BF16), wo_ref[...])
        sa, sb = _sigmoid(ga_ref[...].astype(F32)), _sigmoid(gb_ref[...].astype(F32))
        dyr = (dmix * sa).astype(BF16)
        dyl = (dmix * sb).astype(BF16)
        dyr_ref[...] = dyr
        dyl_ref[...] = dyl
        dseg_ref[:, 0:D_MODEL] = (dmix * yr_ref[...].astype(F32) * (sa * (1.0 - sa))).astype(BF16)
        dseg_ref[:, D_MODEL:2 * D_MODEL] = (dmix * yl_ref[...].astype(F32) * (sb * (1.0 - sb))).astype(BF16)
        dzr_ref[...] = _dot_nt(dyr, wbr_ref[...]).astype(BF16)
        dzl_ref[...] = _dot_nt(dyl, wbl_ref[...]).astype(BF16)

    row = pl.BlockSpec((tm, D_MODEL), lambda i: (i, 0))
    wsp = pl.BlockSpec((D_MODEL, D_MODEL), lambda i: (0, 0))
    bshape = jax.ShapeDtypeStruct((rows, D_MODEL), BF16)
    return pl.pallas_call(
        body, name="mix_bwd", grid=(rows // tm,),
        in_specs=[row, row, row, _seg_spec(tm, 6), _seg_spec(tm, 7), wsp, wsp, wsp],
        out_specs=[row, row, pl.BlockSpec((tm, 2 * D_MODEL), lambda i: (i, 3)), row, row],
        out_shape=[bshape, bshape, jax.ShapeDtypeStruct((rows, N_DEV * D_MODEL), BF16), bshape, bshape],
        compiler_params=_cparams(("parallel",)),
    )(dh1, yr, yl, proj, proj, wbr, wbl, wout)


S1_SHAPES = [
    jax.ShapeDtypeStruct((N_DEV, D_MODEL, FFN_GROUP), BF16),
    jax.ShapeDtypeStruct((N_DEV, FFN_OUT_SHARD, D_MODEL), BF16),
]


def _s1_parts(ins, p):
    return [ins[0].at[p], ins[1].at[p // 2, _half_rows(p), :]]


def _lru_bwd(dzl, hs, cri, proj, dproj, conv_w, lam, wa_g, wx_g, s1_grads):
    rows = dzl.shape[0]
    tm = _tile(rows, 320)
    nt = rows // tm
    t8 = tm // 8
    n_s1 = len(s1_grads)

    def body(dzl_ref, hs_ref, hsp_ref, cri_ref, x_ref, gt_ref, cw_ref, lam_ref, wa_ref, wx_ref, dproj_in, *refs):
        del dproj_in
        s1_refs = refs[:n_s1]
        dseg_ref, dwa_ref, dwx_ref, sm_ref = refs[n_s1:n_s1 + 4]
        land_refs = refs[n_s1 + 4:2 * n_s1 + 4]
        (xbuf, abuf, mbuf, ibuf, dbuf, dcbuf, dpr_s, dpi_s, sums, conv_sums, anext, dhcar,
         send_sems, recv_sems, loc_sems) = refs[2 * n_s1 + 4:]
        step = pl.program_id(0)
        i = nt - 1 - step
        push = _Push(lambda p: _s1_parts(s1_refs, p), lambda s: [r.at[s] for r in land_refs],
                     (send_sems, recv_sems, loc_sems), n_s1)

        @pl.when(step == 0)
        def _():
            push.start()
            dwa_ref[...] = jnp.zeros_like(dwa_ref)
            dwx_ref[...] = jnp.zeros_like(dwx_ref)
            sm_ref[...] = jnp.zeros_like(sm_ref)
            anext[...] = jnp.zeros_like(anext)
            dhcar[...] = jnp.zeros_like(dhcar)
            dcbuf[tm:tm + 8, :] = jnp.zeros((8, D_MODEL), F32)

        slab, lanes = 16, 256
        lam_v = lam_ref[...]
        xbuf[0:8, :] = jnp.where(i == 0, 0.0, hsp_ref[8:16, :].astype(F32))
        sums[...] = jnp.zeros_like(sums)

        def before_scan(k, carry):
            rw = pl.ds(pl.multiple_of(k * slab, slab), slab)
            for q in range(D_MODEL // lanes):
                ln = slice(lanes * q, lanes * (q + 1))
                a, mult, inv_mult, _ = _lru_decay(cri_ref[1, rw, ln].astype(F32), lam_v[:, ln])
                abuf[rw, ln] = a
                mbuf[rw, ln] = mult
                ibuf[rw, ln] = inv_mult
                gl, dgl = _gelu_parts(gt_ref[rw, ln].astype(F32))
                dzl_v = dzl_ref[rw, ln].astype(F32)
                hs_v = hs_ref[rw, ln].astype(F32)
                dseg_ref[rw, D_MODEL + lanes * q:D_MODEL + lanes * (q + 1)] = (dzl_v * hs_v * dgl).astype(BF16)
                dbuf[rw, ln] = dzl_v * gl
                xbuf[pl.ds(pl.multiple_of(k * slab + 8, 8), slab), ln] = hs_v
            return carry

        lax.fori_loop(0, tm // slab, before_scan, 0)

        sub = lax.broadcasted_iota(jnp.int32, (8, D_MODEL), 0)

        def block(k, carry):
            dh_next, a_next = carry
            off = pl.multiple_of((t8 - 1 - k) * 8, 8)
            a_blk = abuf[pl.ds(off, 8), :]
            av = jnp.where(sub < 7, pltpu.roll(a_blk, 7, 0), a_next)
            uv = dbuf[pl.ds(off, 8), :]
            for s in (1, 2, 4):
                us = jnp.where(sub < 8 - s, pltpu.roll(uv, 8 - s, 0), 0.0)
                as_ = jnp.where(sub < 8 - s, pltpu.roll(av, 8 - s, 0), 1.0)
                uv = uv + av * us
                av = av * as_
            hv = uv + av * dh_next
            dbuf[pl.ds(off, 8), :] = hv
            return hv[0:1, :], a_blk[0:1, :]

        dh_first, a_first = lax.fori_loop(0, t8, block, (dhcar[...], anext[...]))
        dhcar[...] = dh_first
        anext[...] = a_first

        sp = jnp.maximum(-lam_v, 0.0) + jnp.log(1.0 + jnp.exp(-jnp.abs(lam_v)))
        sub_q = lax.broadcasted_iota(jnp.int32, (8, lanes), 0)
        row16 = lax.broadcasted_iota(jnp.int32, (slab, 1), 0)

        def after_scan(k, carry):
            off = pl.multiple_of(k * slab, slab)
            rw = pl.ds(off, slab)
            for q in range(D_MODEL // lanes):
                ln = slice(lanes * q, lanes * (q + 1))
                before = xbuf[pl.ds(off, 8), ln]
                h_lo = xbuf[pl.ds(pl.multiple_of(off + 8, 8), 8), ln]
                h_hi = xbuf[pl.ds(pl.multiple_of(off + 16, 8), 8), ln]
                hprev = jnp.concatenate([jnp.where(sub_q >= 1, pltpu.roll(h_lo, 1, 0), before[7:8, :]),
                                         jnp.where(sub_q >= 1, pltpu.roll(h_hi, 1, 0), h_lo[7:8, :])], axis=0)
                c, r, ig = (cri_ref[n, rw, ln].astype(F32) for n in range(3))
                a, mult, inv_mult = abuf[rw, ln], mbuf[rw, ln], ibuf[rw, ln]
                dh = dbuf[rw, ln]
                duu = jnp.where(i * tm + off + row16 >= PAD_ROWS, dh, 0.0)
                t_mult = duu * mult
                dlog_a = dh * hprev * a - duu * ig * c * (a * a) * inv_mult
                dpr = dlog_a * (-LRU_C * sp[:, ln]) * r * (1.0 - r)
                dpi = t_mult * c * ig * (1.0 - ig)
                dpr_s[rw, ln] = dpr.astype(BF16)
                dpi_s[rw, ln] = dpi.astype(BF16)
                dcbuf[rw, ln] = t_mult * ig
                sums[0, :, ln] += dlog_a * r
                sums[1, :, ln] += dpr
                sums[2, :, ln] += dpi
            return carry

        lax.fori_loop(0, tm // slab, after_scan, 0)

        dcs = []
        for g in range(LRU_BLOCKS):
            sl = slice(LRU_BLOCK * g, LRU_BLOCK * (g + 1))
            cg = cri_ref[0, :, sl]
            dpr_b, dpi_b = dpr_s[:, sl], dpi_s[:, sl]
            dwa_ref[g] += _dot_tn(cg, dpr_b)
            dwx_ref[g] += _dot_tn(cg, dpi_b)
            dcs.append(_dot_nt(dpr_b, wa_ref[g]) + _dot_nt(dpi_b, wx_ref[g]))
        dc = dcbuf[0:tm, :] + jnp.concatenate(dcs, axis=1)

        dcbuf[0:tm, :] = dc
        conv_sums[...] = jnp.zeros_like(conv_sums)

        def conv_back(k, carry):
            off = pl.multiple_of(k * slab, slab)
            rw = pl.ds(off, slab)
            for q in range(D_MODEL // lanes):
                ln = slice(lanes * q, lanes * (q + 1))
                blocks = [dcbuf[pl.ds(pl.multiple_of(off + 8 * b, 8), 8), ln] for b in range(3)]
                x_v = x_ref[rw, ln].astype(F32)
                now = jnp.concatenate(blocks[:2], axis=0)
                dlin = cw_ref[3:4, ln] * now
                conv_sums[3, :, ln] += now * x_v
                conv_sums[4, :, ln] += now
                for back in (1, 2, 3):
                    turned = [pltpu.roll(b, 8 - back, 0) for b in blocks]
                    later = jnp.concatenate([jnp.where(sub_q < 8 - back, turned[0], turned[1]),
                                             jnp.where(sub_q < 8 - back, turned[1], turned[2])], axis=0)
                    dlin = dlin + cw_ref[3 - back:4 - back, ln] * later
                    conv_sums[3 - back, :, ln] += later * x_v
                dseg_ref[rw, ln] = dlin.astype(BF16)
            return carry

        lax.fori_loop(0, tm // slab, conv_back, 0)
        dcbuf[tm:tm + 8, :] = dcbuf[0:8, :]
        for n in range(5):
            sm_ref[n:n + 1, :] += jnp.sum(conv_sums[n], axis=0, keepdims=True)
        sm_ref[5:6, :] += jnp.sum(sums[1], axis=0, keepdims=True)
        sm_ref[6:7, :] += jnp.sum(sums[2], axis=0, keepdims=True)
        sm_ref[7:8, :] += jnp.sum(sums[0], axis=0, keepdims=True) * (LRU_C * _sigmoid(-lam_v))

        @pl.when(step == nt - 1)
        def _():
            push.wait()

    rowb = pl.BlockSpec((tm, D_MODEL), lambda s: (nt - 1 - s, 0))
    t16 = tm // 16
    prev8 = pl.BlockSpec((16, D_MODEL), lambda s: (jnp.maximum((nt - 1 - s) * t16 - 1, 0), 0))
    seg = lambda k: pl.BlockSpec((tm, D_MODEL), lambda s, k=k: (nt - 1 - s, k))
    vec = pl.BlockSpec((1, D_MODEL), lambda s: (0, 0))
    mat = pl.BlockSpec((LRU_BLOCKS, LRU_BLOCK, LRU_BLOCK), lambda s: (0, 0, 0))
    mshape = jax.ShapeDtypeStruct((LRU_BLOCKS, LRU_BLOCK, LRU_BLOCK), F32)
    n_in = 10
    return pl.pallas_call(
        body, name="lru_bwd", grid=(nt,),
        in_specs=[rowb, rowb, prev8, pl.BlockSpec((3, tm, D_MODEL), lambda s: (0, nt - 1 - s, 0)), seg(4), seg(5),
                  pl.BlockSpec((4, D_MODEL), lambda s: (0, 0)), vec, mat, mat, ANY] + [ANY] * n_s1,
        out_specs=[pl.BlockSpec((tm, 2 * D_MODEL), lambda s: (nt - 1 - s, 2)), mat, mat,
                   pl.BlockSpec((8, D_MODEL), lambda s: (0, 0))] + [ANY] * n_s1,
        out_shape=[jax.ShapeDtypeStruct(dproj.shape, dproj.dtype), mshape, mshape,
                   jax.ShapeDtypeStruct((8, D_MODEL), F32)] + S1_SHAPES,
        input_output_aliases={n_in: 0},
        scratch_shapes=[pltpu.VMEM((tm + 8, D_MODEL), F32), pltpu.VMEM((tm, D_MODEL), F32),
                        pltpu.VMEM((tm, D_MODEL), F32), pltpu.VMEM((tm, D_MODEL), F32),
                        pltpu.VMEM((tm, D_MODEL), F32), pltpu.VMEM((tm + 8, D_MODEL), F32),
                        pltpu.VMEM((tm, D_MODEL), BF16), pltpu.VMEM((tm, D_MODEL), BF16),
                        pltpu.VMEM((3, 16, D_MODEL), F32), pltpu.VMEM((5, 16, D_MODEL), F32),
                        pltpu.VMEM((1, D_MODEL), F32), pltpu.VMEM((1, D_MODEL), F32)] + _push_sems(n_s1),
        compiler_params=pltpu.CompilerParams(dimension_semantics=("arbitrary",), vmem_limit_bytes=VMEM_LIMIT,
                                             has_side_effects=True),
    )(dzl, hs, hs, cri, proj, proj, conv_w, lam, wa_g, wx_g, dproj, *s1_grads)


def _retention_bwd(dzr, o, proj, states, cos2, sin2, dec, dproj, ride):
    rows = dzr.shape[0]
    n_chunks = rows // CHUNK
    per_step = _chunks_per_step(n_chunks)
    n_steps = n_chunks // per_step
    tm = per_step * CHUNK
    n_r = ride.n

    def body(dzr_ref, o_ref, q_ref, k_ref, v_ref, g_ref, st_ref, c_ref, s_ref, dec_ref, dproj_in, *refs):
        del dproj_in
        dseg_ref = refs[n_r]
        dstate = refs[2 * n_r + 1]
        push = ride.push(refs[:n_r], refs[n_r + 1:2 * n_r + 1], refs[2 * n_r + 2:])

        @pl.when(pl.program_id(0) == 0)
        def _():
            push.start()
            dstate[...] = jnp.zeros_like(dstate)

        for h in range(HEADS):
            sl = slice(HEAD_DIM * h, HEAD_DIM * (h + 1))
            intra, qd, kd, cd = dec_ref[0, h], dec_ref[1, h], dec_ref[2, h], dec_ref[3, h]
            dst = dstate[h]
            for c in reversed(range(per_step)):
                rw = slice(CHUNK * c, CHUNK * (c + 1))
                cos_t, sin_t = c_ref[rw, :], s_ref[rw, :]
                o = o_ref[rw, sl].astype(F32)
                g = g_ref[rw, sl].astype(F32)
                dzr_v = dzr_ref[rw, sl].astype(F32)
                sg = _sigmoid(g)
                r = lax.rsqrt(jnp.mean(o * o, axis=-1, keepdims=True) + NORM_EPS)
                on = o * r
                dseg_ref[rw, 3 * D_MODEL + HEAD_DIM * h:3 * D_MODEL + HEAD_DIM * (h + 1)] = (
                    dzr_v * on * (sg * (1.0 + g * (1.0 - sg)))).astype(BF16)
                don = dzr_v * (g * sg)
                do = r * (don - on * jnp.mean(don * on, axis=-1, keepdims=True))
                dob = do.astype(BF16)

                qh = _rot(q_ref[rw, sl].astype(F32), cos_t, sin_t)
                kh = _rot(k_ref[rw, sl].astype(F32), cos_t, sin_t) * QK_SCALE
                qb, kb, vb = qh.astype(BF16), kh.astype(BF16), v_ref[rw, sl]
                s = (_dot_nt(qb, kb) * intra).astype(BF16)
                ds = (_dot_nt(dob, vb) * intra).astype(BF16)
                st_b = st_ref[c, h].astype(BF16)
                dst_b = dst.astype(BF16)
                dv = _dot_tn(s, dob) + _dot((kh * kd).astype(BF16), dst_b)
                dq = _dot(ds, kb) + _dot_nt(dob, st_b) * qd
                dk = _dot_tn(ds, qb) + _dot_nt(vb, dst_b) * kd
                dst = dst * cd + _dot_tn((qh * qd).astype(BF16), dob)
                dseg_ref[rw, 2 * D_MODEL + HEAD_DIM * h:2 * D_MODEL + HEAD_DIM * (h + 1)] = dv.astype(BF16)
                dseg_ref[rw, sl] = _rot_t(dq, cos_t, sin_t).astype(BF16)
                dseg_ref[rw, D_MODEL + HEAD_DIM * h:D_MODEL + HEAD_DIM * (h + 1)] = (
                    _rot_t(dk, cos_t, sin_t) * QK_SCALE).astype(BF16)
            dstate[h] = dst

        @pl.when(pl.program_id(0) == n_steps - 1)
        def _():
            push.wait()

    rev = lambda s: n_steps - 1 - s
    rowb = pl.BlockSpec((tm, D_MODEL), lambda s: (rev(s), 0))
    seg = lambda k: pl.BlockSpec((tm, D_MODEL), lambda s, k=k: (rev(s), k))
    tab = pl.BlockSpec((tm, HEAD_DIM), lambda s: (rev(s), 0))
    return pl.pallas_call(
        body, name="retention_bwd", grid=(n_steps,),
        in_specs=[rowb, rowb, seg(0), seg(1), seg(2), seg(3),
                  pl.BlockSpec((per_step, HEADS, HEAD_DIM, HEAD_DIM), lambda s: (rev(s), 0, 0, 0)), tab, tab,
                  pl.BlockSpec((4, HEADS, CHUNK, CHUNK), lambda s: (0, 0, 0, 0)), ANY] + ride.specs(),
        out_specs=[pl.BlockSpec((tm, 4 * D_MODEL), lambda s: (rev(s), 0))] + ride.specs(),
        out_shape=[jax.ShapeDtypeStruct(dproj.shape, dproj.dtype)] + ride.out_shapes,
        input_output_aliases={10: 0},
        scratch_shapes=[pltpu.VMEM((HEADS, HEAD_DIM, HEAD_DIM), F32)] + ride.scratch(),
        compiler_params=pltpu.CompilerParams(dimension_semantics=("arbitrary",), vmem_limit_bytes=VMEM_LIMIT,
                                             has_side_effects=True),
    )(dzr, o, proj, proj, proj, proj, states, cos2, sin2, dec, dproj, *ride.arrays)


S2_SHAPES = [
    jax.ShapeDtypeStruct((N_DEV, D_MODEL, D_MODEL), BF16),
    jax.ShapeDtypeStruct((N_DEV, LRU_BLOCKS, LRU_ROWS, LRU_BLOCK), F32),
    jax.ShapeDtypeStruct((N_DEV, LRU_BLOCKS, LRU_ROWS, LRU_BLOCK), F32),
]


def _s2_parts(ins, p):
    return [r.at[p] for r in ins]


def _in_proj_bwd(dproj, win_g, h0, norm_w, dh1, s2_grads):
    rows = h0.shape[0]
    tm = _tile(rows, 320)
    n_i = rows // tm
    n_s2 = len(s2_grads)

    def body(dseg_ref, w_ref, h0_ref, nw_ref, dh1_ref, *refs):
        s2_refs = refs[:n_s2]
        dh0_ref, dw_ref = refs[n_s2:n_s2 + 2]
        land_refs = refs[n_s2 + 2:2 * n_s2 + 2]
        send_sems, recv_sems, loc_sems = refs[2 * n_s2 + 2:]
        i = pl.program_id(0)
        push = _Push(lambda p: _s2_parts(s2_refs, p), lambda s: [r.at[s] for r in land_refs],
                     (send_sems, recv_sems, loc_sems), n_s2)

        @pl.when(i == 0)
        def _():
            push.start()
            dw_ref[...] = jnp.zeros_like(dw_ref)

        du = _dot_nt(dseg_ref[:, 0:D_MODEL], w_ref[0])
        for j in range(1, N_DEV):
            du = du + _dot_nt(dseg_ref[:, D_MODEL * j:D_MODEL * (j + 1)], w_ref[j])
        dx, dw = _rms_bwd(h0_ref[...], nw_ref[...], du)
        dw_ref[0:1, :] += dw
        dh0_ref[...] = dh1_ref[...] + dx

        @pl.when(i == n_i - 1)
        def _():
            push.wait()

    row = pl.BlockSpec((tm, D_MODEL), lambda i: (i, 0))
    vec = pl.BlockSpec((1, D_MODEL), lambda i: (0, 0))
    return pl.pallas_call(
        body, name="in_proj_bwd", grid=(n_i,),
        in_specs=[pl.BlockSpec((tm, N_DEV * D_MODEL), lambda i: (i, 0)),
                  pl.BlockSpec((N_DEV, D_MODEL, D_MODEL), lambda i: (0, 0, 0), pipeline_mode=pl.Buffered(1)),
                  row, vec, row] + [ANY] * n_s2,
        out_specs=[row, pl.BlockSpec((8, D_MODEL), lambda i: (0, 0))] + [ANY] * n_s2,
        out_shape=[jax.ShapeDtypeStruct((rows, D_MODEL), F32), jax.ShapeDtypeStruct((8, D_MODEL), F32)] + S2_SHAPES,
        scratch_shapes=_push_sems(n_s2),
        compiler_params=pltpu.CompilerParams(dimension_semantics=("arbitrary",),
                                             vmem_limit_bytes=VMEM_LIMIT, has_side_effects=True),
    )(dproj, win_g, h0, norm_w, dh1, *s2_grads)


def _adamw(g_slots, w, m, v):
    slots, rows, cols = g_slots.shape
    tr = rows
    for cand in (256, 128, 64, 32, 16, 8):
        if rows % cand == 0 and rows > cand:
            tr = cand
            break

    def body(g_ref, w_ref, m_ref, v_ref, go_ref, d_ref, mo_ref, vo_ref):
        g = g_ref[0].astype(F32)
        for s in range(1, slots):
            g = g + g_ref[s].astype(F32)
        m2 = ADAM_B1 * m_ref[...] + (1.0 - ADAM_B1) * g
        v2 = ADAM_B2 * v_ref[...] + (1.0 - ADAM_B2) * (g * g)
        m_hat = m2 / (1.0 - ADAM_B1 ** ADAM_STEP)
        v_hat = v2 / (1.0 - ADAM_B2 ** ADAM_STEP)
        go_ref[...] = g
        d_ref[...] = -ADAM_LR * (m_hat / (jnp.sqrt(v_hat) + ADAM_EPS) + ADAM_WD * w_ref[...])
        mo_ref[...] = m2
        vo_ref[...] = v2

    blk = pl.BlockSpec((tr, cols), lambda i: (i, 0))
    shape = jax.ShapeDtypeStruct((rows, cols), F32)
    return pl.pallas_call(
        body, name="adamw", grid=(rows // tr,),
        in_specs=[pl.BlockSpec((slots, tr, cols), lambda i: (0, i, 0)), blk, blk, blk],
        out_specs=[blk] * 4, out_shape=[shape] * 4,
        compiler_params=_cparams(("parallel",)),
    )(g_slots, w, m, v)


def _sum_slots(packs):
    slots, rows, cols = packs.shape

    def body(p_ref, o_ref):
        acc = p_ref[0]
        for s in range(1, slots):
            acc = acc + p_ref[s]
        o_ref[...] = acc

    return pl.pallas_call(
        body, name="sum_slots", out_shape=jax.ShapeDtypeStruct((rows, cols), F32),
        compiler_params=pltpu.CompilerParams(vmem_limit_bytes=VMEM_LIMIT),
    )(packs)


def _gather_small(small):
    shapes = [jax.ShapeDtypeStruct((N_DEV,) + small.shape, F32)]
    return _push_call("gather_small", [small], shapes,
                      lambda ins, p: list(ins), lambda outs, s: [r.at[s] for r in outs])[0]


def _share_pack(pack):
    shapes = [jax.ShapeDtypeStruct((N_DEV,) + pack.shape, F32)]
    return _push_call("share_pack", [pack], shapes,
                      lambda ins, p: list(ins), lambda outs, s: [r.at[s] for r in outs])[0]


PACK_MIX_NORM, PACK_CONV_W, PACK_CONV_B, PACK_BA, PACK_BX, PACK_LAM = 0, 8, 12, 13, 14, 15
PACK_FFN_NORM, PACK_SQ_ERR, PACK_FINAL_NORM, PACK_META = 16, 24, 25, 32


def kernel(x, meta_tokens, mix_norm_w, w_in, conv_w, conv_b, lru_wa, lru_ba, lru_wx, lru_bx, lru_lambda, w_branch_ret, w_branch_lru, w_out, ffn_norm_w, w_ffn_in, w_ffn_out, final_norm_w, loss_target, m_meta_tokens, m_mix_norm_w, m_w_in, m_conv_w, m_conv_b, m_lru_wa, m_lru_ba, m_lru_wx, m_lru_bx, m_lru_lambda, m_w_branch_ret, m_w_branch_lru, m_w_out, m_ffn_norm_w, m_w_ffn_in, m_w_ffn_out, m_final_norm_w, v_meta_tokens, v_mix_norm_w, v_w_in, v_conv_w, v_conv_b, v_lru_wa, v_lru_ba, v_lru_wx, v_lru_bx, v_lru_lambda, v_w_branch_ret, v_w_branch_lru, v_w_out, v_ffn_norm_w, v_w_ffn_in, v_w_ffn_out, v_final_norm_w):
    me = _my_index()
    pad4 = ((0, 4), (0, 0))
    fw = final_norm_w.reshape(1, D_MODEL)

    small = jnp.concatenate([meta_tokens, jnp.pad(conv_w[0], pad4)], axis=0)
    small_g = _gather_small(small)
    meta_full = small_g[:, :N_META].transpose(1, 0, 2).reshape(N_META, D_MODEL)
    conv_w_full = small_g[:, N_META:N_META + 4].transpose(1, 0, 2).reshape(4, D_MODEL)
    mixer_shards = [w_branch_ret[0].astype(BF16), w_branch_lru[0].astype(BF16), w_out[0].astype(BF16),
                    lru_wa[0].astype(BF16), lru_wx[0].astype(BF16)]
    wfi_shard = jnp.pad(w_ffn_in[0].astype(BF16), ((0, 0), (0, FFN_GROUP - FFN_SHARD)))
    own_slot = lambda ins, p: list(ins)
    part_of_owner = lambda ins, p: [r.at[p] for r in ins]

    rows = x.shape[1] + CHUNK
    h0 = jnp.concatenate([jnp.zeros((PAD_ROWS, D_MODEL), F32), meta_full, x[0]], axis=0)
    cos2, sin2 = _rope_tables(rows)
    dec = _retention_consts()

    proj, u, win_g = _in_proj(h0, mix_norm_w, w_in[0].astype(BF16), me.astype(jnp.int32).reshape(1))
    o, zr, states, wbr_g, wbl_g, wout_g, wa_g, wx_g = _retention_fwd(
        proj, cos2, sin2, dec, _mixer_weights_ride(mixer_shards))
    wbr, wbl, wout = (t.reshape(D_MODEL, D_MODEL) for t in (wbr_g, wbl_g, wout_g))
    wa_g, wx_g = _from_owners(wa_g), _from_owners(wx_g)
    gather_wfi = _Ride([wfi_shard], [jax.ShapeDtypeStruct((N_DEV, D_MODEL, FFN_GROUP), BF16)],
                       own_slot, _slot_of_sender, gather_by_chip=True)
    hs, zl, cri, wfi_g = _lru_fwd(proj, conv_w_full, conv_b, lru_ba, lru_bx, lru_lambda, wa_g, wx_g, gather_wfi)
    h1, yr, yl, mixed, wfo_g = _mix_fwd(zr, zl, proj, h0, wbr, wbl, wout, _wfo_ride(w_ffn_out[0].astype(BF16)))
    u2, g, up, act, dh2, red = _ffn_fwd_loss(h1, ffn_norm_w, wfi_g, wfo_g, fw, loss_target[0])

    d_wfo = _wgrad(act, dh2, FFN_GROUP, D_MODEL, BF16)[:, 0]
    dgu, dh1, dw_ffn_norm = _ffn_bwd(dh2, g, up, h1, ffn_norm_w, wfi_g, wfo_g)
    d_wfi = _wgrad(u2, dgu, D_MODEL, FFN_GROUP, BF16, b_halves=True)[0]
    d_wout = _wgrad(mixed, dh1, D_MODEL, D_MODEL, BF16)[0, 0]
    dyr, dyl, dproj, dzr, dzl = _mix_bwd(dh1, yr, yl, proj, wbr, wbl, wout)
    d_wbr = _wgrad(zr, dyr, D_MODEL, D_MODEL, BF16)[0, 0]
    d_wbl = _wgrad(zl, dyl, D_MODEL, D_MODEL, BF16)[0, 0]
    dproj, d_wa, d_wx, lru_small, r_fi, r_fo = _lru_bwd(
        dzl, hs, cri, proj, dproj, conv_w_full, lru_lambda, wa_g, wx_g, [d_wfi, d_wfo])
    mix_shape = jax.ShapeDtypeStruct((N_DEV, D_MODEL // N_DEV, D_MODEL), BF16)
    scatter_mix = _Ride([t.reshape(mix_shape.shape) for t in (d_wbr, d_wbl, d_wout)], [mix_shape] * 3,
                        part_of_owner, _slot_of_sender)
    dproj, r_br, r_bl, r_out = _retention_bwd(dzr, o, proj, states, cos2, sin2, dec, dproj, scatter_mix)
    d_win = _wgrad(u, dproj, D_MODEL, D_MODEL, BF16)[0]
    dh0, dw_mix_norm, r_in, r_wa, r_wx = _in_proj_bwd(dproj, win_g, h0, mix_norm_w, dh1,
                                                      [d_win, _by_owner(d_wa), _by_owner(d_wx)])
    grad_x = dh0[CHUNK:]

    pack = jnp.concatenate([dw_mix_norm, lru_small, dw_ffn_norm, red, dh0[PAD_ROWS:CHUNK]], axis=0)
    small_sum = _sum_slots(_share_pack(pack))
    loss = (0.5 / D_MODEL) * jnp.sum(small_sum[PACK_SQ_ERR])

    def big_update(slots, w, m, v):
        shape = w.shape
        w2, m2, v2 = (t.reshape(slots.shape[1:]) for t in (w, m, v))
        return [t.reshape(shape) for t in _adamw(slots, w2, m2, v2)]

    res = {}
    res["w_in"] = big_update(r_in, w_in, m_w_in, v_w_in)
    res["w_branch_ret"] = big_update(r_br, w_branch_ret, m_w_branch_ret, v_w_branch_ret)
    res["w_branch_lru"] = big_update(r_bl, w_branch_lru, m_w_branch_lru, v_w_branch_lru)
    res["w_out"] = big_update(r_out, w_out, m_w_out, v_w_out)
    res["w_ffn_in"] = big_update(r_fi[:, :, :FFN_SHARD], w_ffn_in, m_w_ffn_in, v_w_ffn_in)
    res["w_ffn_out"] = big_update(r_fo, w_ffn_out, m_w_ffn_out, v_w_ffn_out)
    res["lru_wa"] = big_update(r_wa.reshape(N_DEV, LRU_BLOCKS * LRU_ROWS, LRU_BLOCK), lru_wa, m_lru_wa, v_lru_wa)
    res["lru_wx"] = big_update(r_wx.reshape(N_DEV, LRU_BLOCKS * LRU_ROWS, LRU_BLOCK), lru_wx, m_lru_wx, v_lru_wx)

    col = me * HEAD_DIM
    g_meta = lax.dynamic_slice(small_sum, (PACK_META, col), (N_META, HEAD_DIM))
    g_conv = lax.dynamic_slice(small_sum, (PACK_CONV_W, col), (8, HEAD_DIM))
    small_names = ["mix_norm_w", "conv_b", "lru_ba", "lru_bx", "lru_lambda", "ffn_norm_w", "final_norm_w"]
    small_rows = [PACK_MIX_NORM, PACK_CONV_B, PACK_BA, PACK_BX, PACK_LAM, PACK_FFN_NORM, PACK_FINAL_NORM]
    small_w = [mix_norm_w, conv_b, lru_ba, lru_bx, lru_lambda, ffn_norm_w, fw]
    small_m = [m_mix_norm_w, m_conv_b, m_lru_ba, m_lru_bx, m_lru_lambda, m_ffn_norm_w, m_final_norm_w.reshape(1, -1)]
    small_v = [v_mix_norm_w, v_conv_b, v_lru_ba, v_lru_bx, v_lru_lambda, v_ffn_norm_w, v_final_norm_w.reshape(1, -1)]

    def pack_small(vec_list, meta_t, conv_t):
        return jnp.concatenate([t.reshape(8, HEAD_DIM) for t in vec_list] + [meta_t, jnp.pad(conv_t[0], pad4)], axis=0)

    g_small = jnp.concatenate([small_sum[r].reshape(8, HEAD_DIM) for r in small_rows] + [g_meta, g_conv], axis=0)
    outs_small = _adamw(g_small[None], pack_small(small_w, meta_tokens, conv_w),
                        pack_small(small_m, m_meta_tokens, m_conv_w), pack_small(small_v, v_meta_tokens, v_conv_w))
    for idx, name in enumerate(small_names):
        shape = final_norm_w.shape if name == "final_norm_w" else (1, D_MODEL)
        res[name] = [t[8 * idx:8 * idx + 8].reshape(shape) for t in outs_small]
    res["meta_tokens"] = [t[56:72] for t in outs_small]
    res["conv_w"] = [t[72:76].reshape(1, 4, HEAD_DIM) for t in outs_small]

    order = ["meta_tokens", "mix_norm_w", "w_in", "conv_w", "conv_b", "lru_wa", "lru_ba", "lru_wx", "lru_bx",
             "lru_lambda", "w_branch_ret", "w_branch_lru", "w_out", "ffn_norm_w", "w_ffn_in", "w_ffn_out",
             "final_norm_w"]
    out = [loss, grad_x[None]]
    for kind in range(4):
        out += [res[name][kind] for name in order]
    return tuple(out)
```

```python
import functools

import numpy as np
import jax
import jax.numpy as jnp
from jax import lax
from jax.experimental import pallas as pl
from jax.experimental.pallas import tpu as pltpu

F32 = jnp.float32
BF16 = jnp.bfloat16

D_MODEL = 1024
N_META = 16
CHUNK = 128
PAD_ROWS = CHUNK - N_META
HEADS = 8
HEAD_DIM = 128
ROPE_BASE = 10000.0
QK_SCALE = HEAD_DIM ** -0.5
LRU_BLOCKS = 4
LRU_BLOCK = 256
LRU_C = 8.0
FFN_HIDDEN = 2816
N_DEV = 8
FFN_SHARD = 2 * FFN_HIDDEN // N_DEV
FFN_GROUP = 768
FFN_GROUPS = 4
FFN_OUT_SHARD = FFN_HIDDEN // N_DEV
NORM_EPS = 1e-6

ADAM_LR = 0.001
ADAM_B1 = 0.9
ADAM_B2 = 0.999
ADAM_EPS = 1e-08
ADAM_WD = 0.01
ADAM_STEP = 10

VMEM_LIMIT = 56 * 1024 * 1024
MESH_ID = pl.DeviceIdType.MESH
ANY = pl.BlockSpec(memory_space=pl.ANY)


def _cparams(sem):
    return pltpu.CompilerParams(dimension_semantics=sem, vmem_limit_bytes=VMEM_LIMIT)


def _tile(rows, cap):
    t = cap - cap % 64
    while rows % t:
        t -= 64
    return t


def _dot(a, b):
    return jnp.dot(a, b, preferred_element_type=F32)


def _dot_nt(a, b):
    return lax.dot_general(a, b, (((1,), (1,)), ((), ())), preferred_element_type=F32)


def _dot_tn(a, b):
    return lax.dot_general(a, b, (((0,), (0,)), ((), ())), preferred_element_type=F32)


def _sigmoid(x):
    return 0.5 * jnp.tanh(0.5 * x) + 0.5


def _gelu_parts(x):
    k = 0.7978845608028654
    inner = k * (x + 0.044715 * x * x * x)
    t = jnp.tanh(inner)
    g = 0.5 * x * (1.0 + t)
    dg = 0.5 * (1.0 + t) + 0.5 * x * (1.0 - t * t) * k * (1.0 + 3.0 * 0.044715 * x * x)
    return g, dg


def _rot(x, cos2, sin2):
    return x * cos2 + pltpu.roll(x, HEAD_DIM // 2, 1) * sin2


def _rot_t(dx, cos2, sin2):
    return dx * cos2 - pltpu.roll(dx, HEAD_DIM // 2, 1) * sin2


def _rms_bwd(x, w, dy):
    rs = lax.rsqrt(jnp.mean(x * x, axis=-1, keepdims=True) + NORM_EPS)
    nh = x * rs
    dw = jnp.sum(dy * nh, axis=0, keepdims=True)
    dn = dy * w
    dx = rs * (dn - nh * jnp.mean(dn * nh, axis=-1, keepdims=True))
    return dx, dw


def _retention_consts():
    h = jnp.arange(HEADS, dtype=F32)
    log_g = jnp.log(1.0 - 2.0 ** (-5.0 - h))
    idx = jnp.arange(CHUNK, dtype=F32)
    diff = idx[:, None] - idx[None, :]
    intra = jnp.where(diff[None] >= 0, jnp.exp(jnp.maximum(diff, 0.0)[None] * log_g[:, None, None]), 0.0)
    q_decay = jnp.exp((idx + 1.0)[:, None] * log_g[None, :])
    k_decay = jnp.exp((CHUNK - 1.0 - idx)[:, None] * log_g[None, :])
    chunk_decay = jnp.exp(CHUNK * log_g)
    shape = (HEADS, CHUNK, CHUNK)
    qd = jnp.broadcast_to(q_decay.T[:, :, None], shape)
    kd = jnp.broadcast_to(k_decay.T[:, :, None], shape)
    cd = jnp.broadcast_to(chunk_decay[:, None, None], shape)
    return jnp.stack([intra, qd, kd, cd])


def _rope_tables(rows):
    pos = jnp.maximum(jnp.arange(rows) - PAD_ROWS, 0).astype(F32)
    inv_freq = ROPE_BASE ** (-jnp.arange(0, HEAD_DIM, 2, dtype=F32) / HEAD_DIM)
    ang = pos[:, None] * inv_freq[None, :]
    cos, sin = jnp.cos(ang), jnp.sin(ang)
    return jnp.concatenate([cos, cos], axis=1), jnp.concatenate([-sin, sin], axis=1)


def _my_index():
    return 4 * lax.axis_index("x") + 2 * lax.axis_index("y") + lax.axis_index("c")


def _peer(k):
    x, y, c = lax.axis_index("x"), lax.axis_index("y"), lax.axis_index("c")
    px = 1 - x if k & 4 else x
    py = 1 - y if k & 2 else y
    pc = 1 - c if k & 1 else c
    return (px, py, pc), 4 * px + 2 * py + pc


def _push_sems(n_arr):
    n_rem = (N_DEV - 1) * n_arr
    return [pltpu.SemaphoreType.DMA((n_rem,)), pltpu.SemaphoreType.DMA((n_rem,)), pltpu.SemaphoreType.DMA((n_arr,))]


class _Push:
    def __init__(self, send_part, land_slot, sems, n_arr):
        self.send_part, self.land_slot, self.n_arr = send_part, land_slot, n_arr
        self.send_sems, self.recv_sems, self.loc_sems = sems

    def _remote(self, k, a, src, dst, pos):
        idx = (k - 1) * self.n_arr + a
        return pltpu.make_async_remote_copy(src_ref=src, dst_ref=dst, send_sem=self.send_sems.at[idx],
                                            recv_sem=self.recv_sems.at[idx], device_id=pos, device_id_type=MESH_ID)

    def _outgoing(self):
        me = _my_index()
        land = self.land_slot(me)
        remote = []
        for k in range(1, N_DEV):
            pos, p = _peer(k)
            src = self.send_part(p)
            remote += [self._remote(k, a, src[a], land[a], pos) for a in range(self.n_arr)]
        own = self.send_part(me)
        local = [pltpu.make_async_copy(own[a], land[a], self.loc_sems.at[a]) for a in range(self.n_arr)]
        return remote, local

    def start(self):
        remote, local = self._outgoing()
        for cp in remote + local:
            cp.start()

    def wait_recv_from(self, k):
        own = self.send_part(_my_index())
        pos, p = _peer(k)
        land = self.land_slot(p)
        for a in range(self.n_arr):
            self._remote(k, a, own[a], land[a], pos).wait_recv()

    def wait_sends(self):
        remote, local = self._outgoing()
        for cp in remote:
            cp.wait_send()
        for cp in local:
            cp.wait()

    def wait(self):
        for k in range(1, N_DEV):
            self.wait_recv_from(k)
        self.wait_sends()


DIRECT = (1, 2, 4, 6)
RELAYED = (2, 4, 6)


def _gather_by_chip_sems(n_arr):
    direct, relayed = len(DIRECT) * n_arr, len(RELAYED) * n_arr
    return [pltpu.SemaphoreType.DMA((direct,)), pltpu.SemaphoreType.DMA((direct,)),
            pltpu.SemaphoreType.DMA((relayed,)), pltpu.SemaphoreType.DMA((relayed,)), pltpu.SemaphoreType.DMA((n_arr,))]


class _GatherByChip:
    def __init__(self, srcs, land_slot, sems, n_arr):
        self.srcs, self.land_slot, self.n_arr = srcs, land_slot, n_arr
        self.send_sems, self.recv_sems, self.relay_send_sems, self.relay_recv_sems, self.loc_sems = sems

    def _direct(self, k, a, slot):
        idx = DIRECT.index(k) * self.n_arr + a
        return pltpu.make_async_remote_copy(src_ref=self.srcs[a], dst_ref=self.land_slot(slot)[a],
                                            send_sem=self.send_sems.at[idx], recv_sem=self.recv_sems.at[idx],
                                            device_id=_peer(k)[0], device_id_type=MESH_ID)

    def _relay(self, q, a, slot):
        idx = RELAYED.index(q) * self.n_arr + a
        block = self.land_slot(slot)[a]
        return pltpu.make_async_remote_copy(src_ref=block, dst_ref=block, send_sem=self.relay_send_sems.at[idx],
                                            recv_sem=self.relay_recv_sems.at[idx], device_id=_peer(1)[0],
                                            device_id_type=MESH_ID)

    def _own(self, a):
        return pltpu.make_async_copy(self.srcs[a], self.land_slot(_my_index())[a], self.loc_sems.at[a])

    def start(self):
        me = _my_index()
        for k in DIRECT:
            for a in range(self.n_arr):
                self._direct(k, a, me).start()
        for a in range(self.n_arr):
            self._own(a).start()

    def relay(self):
        for q in RELAYED:
            p = _peer(q)[1]
            for a in range(self.n_arr):
                self._direct(q, a, p).wait_recv()
                self._relay(q, a, p).start()

    def wait(self):
        me = _my_index()
        for a in range(self.n_arr):
            self._direct(1, a, _peer(1)[1]).wait_recv()
        for q in RELAYED:
            for a in range(self.n_arr):
                self._relay(q, a, _peer(q + 1)[1]).wait_recv()
        for k in DIRECT:
            for a in range(self.n_arr):
                self._direct(k, a, me).wait_send()
        for q in RELAYED:
            for a in range(self.n_arr):
                self._relay(q, a, _peer(q)[1]).wait_send()
        for a in range(self.n_arr):
            self._own(a).wait()


class _Ride:
    def __init__(self, arrays, out_shapes, send_part, land_slot, zero_dsts=None, zero_shape=None, n_zero=0,
                 gather_by_chip=False):
        self.arrays, self.out_shapes = list(arrays), list(out_shapes)
        self.send_part, self.land_slot, self.n = send_part, land_slot, len(arrays)
        self.zero_dsts, self.zero_shape, self.n_zero = zero_dsts, zero_shape, n_zero
        self.gather_by_chip = gather_by_chip

    def specs(self):
        return [ANY] * self.n

    def scratch(self):
        extra = [pltpu.SemaphoreType.DMA((self.n_zero,)), pltpu.VMEM(self.zero_shape, BF16)] if self.n_zero else []
        sems = _gather_by_chip_sems(self.n) if self.gather_by_chip else _push_sems(self.n)
        return sems + extra

    def push(self, in_refs, out_refs, scratch):
        ride = self
        n_sems = 5 if self.gather_by_chip else 3
        land = lambda s: ride.land_slot(out_refs, s)
        if self.gather_by_chip:
            push = _GatherByChip(list(in_refs), land, tuple(scratch[:n_sems]), self.n)
        else:
            push = _Push(lambda p: ride.send_part(in_refs, p), land, tuple(scratch[:n_sems]), self.n)

        class Both:
            def _fills(self):
                if not ride.n_zero:
                    return []
                zsems, zbuf = scratch[n_sems], scratch[n_sems + 1]
                return [pltpu.make_async_copy(zbuf, dst, zsems.at[z]) for z, dst in enumerate(ride.zero_dsts(out_refs))]

            def start(self):
                push.start()
                if ride.n_zero:
                    scratch[n_sems + 1][...] = jnp.zeros(ride.zero_shape, BF16)
                for cp in self._fills():
                    cp.start()

            def relay(self):
                if ride.gather_by_chip:
                    push.relay()

            def wait(self):
                push.wait()
                for cp in self._fills():
                    cp.wait()

        return Both()


def _slot_of_sender(out_refs, s):
    return [r.at[s] for r in out_refs]


def _push_call(name, arrays, out_shapes, send_part, land_slot):
    n_arr = len(arrays)

    def body(*refs):
        ins, outs, sems = refs[:n_arr], refs[n_arr:2 * n_arr], refs[2 * n_arr:]
        push = _Push(lambda p: send_part(ins, p), lambda s: land_slot(outs, s), sems, n_arr)
        push.start()
        push.wait()

    return pl.pallas_call(
        body, name=name, in_specs=[ANY] * n_arr, out_specs=[ANY] * n_arr, out_shape=out_shapes,
        scratch_shapes=_push_sems(n_arr), compiler_params=pltpu.CompilerParams(has_side_effects=True),
    )(*arrays)


LRU_ROWS = LRU_BLOCK // N_DEV
FFN_PAD_ROWS = FFN_GROUP - 2 * FFN_OUT_SHARD


def _half_rows(d):
    return pl.ds(pl.multiple_of((d % 2) * FFN_OUT_SHARD, 16), FFN_OUT_SHARD)


MIXER_SHAPES = [
    jax.ShapeDtypeStruct((N_DEV, D_MODEL // N_DEV, D_MODEL), BF16),
    jax.ShapeDtypeStruct((N_DEV, D_MODEL // N_DEV, D_MODEL), BF16),
    jax.ShapeDtypeStruct((N_DEV, D_MODEL // N_DEV, D_MODEL), BF16),
    jax.ShapeDtypeStruct((N_DEV, LRU_BLOCKS, LRU_ROWS, LRU_BLOCK), BF16),
    jax.ShapeDtypeStruct((N_DEV, LRU_BLOCKS, LRU_ROWS, LRU_BLOCK), BF16),
]


def _by_owner(t):
    return t.reshape(LRU_BLOCKS, N_DEV, LRU_ROWS, LRU_BLOCK).transpose(1, 0, 2, 3)


def _from_owners(t):
    return t.transpose(1, 0, 2, 3).reshape(LRU_BLOCKS, LRU_BLOCK, LRU_BLOCK)


def _mixer_weights_ride(shards):
    return _Ride(shards, MIXER_SHAPES, lambda ins, p: list(ins), _slot_of_sender, gather_by_chip=True)


def _wfo_ride(shard):
    zero_dsts = lambda outs: [outs[0].at[g, pl.ds(2 * FFN_OUT_SHARD, FFN_PAD_ROWS), :] for g in range(FFN_GROUPS)]
    return _Ride([shard], [jax.ShapeDtypeStruct((FFN_GROUPS, FFN_GROUP, D_MODEL), BF16)], lambda ins, p: list(ins),
                 lambda outs, d: [outs[0].at[d // 2, _half_rows(d), :]], zero_dsts, (FFN_PAD_ROWS, D_MODEL), FFN_GROUPS,
                 gather_by_chip=True)


def _arrival_rank_to_relation(jj):
    return jnp.where(jj == 3, 4, jnp.where(jj == 4, 3, jj))


def _in_proj(h0, norm_w, win_shard, me_arr):
    rows = h0.shape[0]
    tm = _tile(rows, 1664)
    n_i = rows // tm

    direct, relayed = DIRECT, RELAYED

    def body(me_ref, h_ref, nw_ref, wsh_ref, proj_ref, u_ref, wing_ref, u_all, wbuf, copy_sem,
             send_sems, recv_sems, relay_send_sems, relay_recv_sems, own_sem):
        del me_ref
        jj, i = pl.program_id(0), pl.program_id(1)
        me = _my_index()
        sibling = _peer(1)[0]

        def direct_copy(k, slot):
            n = direct.index(k)
            return pltpu.make_async_remote_copy(src_ref=wsh_ref, dst_ref=wing_ref.at[slot], send_sem=send_sems.at[n],
                                                recv_sem=recv_sems.at[n], device_id=_peer(k)[0], device_id_type=MESH_ID)

        def relay_copy(q, slot):
            n = relayed.index(q)
            return pltpu.make_async_remote_copy(src_ref=wing_ref.at[slot], dst_ref=wing_ref.at[slot],
                                                send_sem=relay_send_sems.at[n], recv_sem=relay_recv_sems.at[n],
                                                device_id=sibling, device_id_type=MESH_ID)

        own_slot = pltpu.make_async_copy(wsh_ref, wing_ref.at[me], own_sem)

        @pl.when(jnp.logical_and(jj == 0, i == 0))
        def _():
            for k in direct:
                direct_copy(k, me).start()
            own_slot.start()
            own = pltpu.make_async_copy(wsh_ref, wbuf, copy_sem)
            own.start()
            own.wait()

        for k in range(1, N_DEV):
            rank = {3: 4, 4: 3}.get(k, k)

            @pl.when(jnp.logical_and(jj == rank, i == 0))
            def _(k=k):
                p = _peer(k)[1]
                if k in direct:
                    direct_copy(k, p).wait_recv()
                    if k in relayed:
                        relay_copy(k, p).start()
                else:
                    relay_copy(k - 1, p).wait_recv()
                landed = pltpu.make_async_copy(wing_ref.at[p], wbuf, copy_sem)
                landed.start()
                landed.wait()

        rows_i = pl.ds(pl.multiple_of(i * tm, tm), tm)

        @pl.when(jj == 0)
        def _():
            x = h_ref[...]
            rs = lax.rsqrt(jnp.mean(x * x, axis=-1, keepdims=True) + NORM_EPS)
            u = (x * rs * nw_ref[...]).astype(BF16)
            u_all[rows_i, :] = u
            u_ref[...] = u
        proj_ref[...] = _dot(u_all[rows_i, :], wbuf[...]).astype(BF16)

        @pl.when(jnp.logical_and(jj == N_DEV - 1, i == n_i - 1))
        def _():
            for k in direct:
                direct_copy(k, me).wait_send()
            for q in relayed:
                relay_copy(q, _peer(q)[1]).wait_send()
            own_slot.wait()

    first_pass = lambda jj, i: jnp.where(jj == 0, i, n_i - 1)
    grid_spec = pltpu.PrefetchScalarGridSpec(
        num_scalar_prefetch=1, grid=(N_DEV, n_i),
        in_specs=[pl.BlockSpec((tm, D_MODEL), lambda jj, i, me: (first_pass(jj, i), 0)),
                  pl.BlockSpec((1, D_MODEL), lambda jj, i, me: (0, 0)), ANY],
        out_specs=[pl.BlockSpec((tm, D_MODEL), lambda jj, i, me: (i, me[0] ^ _arrival_rank_to_relation(jj))),
                   pl.BlockSpec((tm, D_MODEL), lambda jj, i, me: (first_pass(jj, i), 0)), ANY],
        scratch_shapes=[pltpu.VMEM((rows, D_MODEL), BF16), pltpu.VMEM((D_MODEL, D_MODEL), BF16),
                        pltpu.SemaphoreType.DMA(()),
                        pltpu.SemaphoreType.DMA((len(direct),)), pltpu.SemaphoreType.DMA((len(direct),)),
                        pltpu.SemaphoreType.DMA((len(relayed),)), pltpu.SemaphoreType.DMA((len(relayed),)),
                        pltpu.SemaphoreType.DMA(())])
    return pl.pallas_call(
        body, name="in_proj", grid_spec=grid_spec,
        out_shape=[jax.ShapeDtypeStruct((rows, N_DEV * D_MODEL), BF16),
                   jax.ShapeDtypeStruct((rows, D_MODEL), BF16),
                   jax.ShapeDtypeStruct((N_DEV, D_MODEL, D_MODEL), BF16)],
        compiler_params=pltpu.CompilerParams(dimension_semantics=("arbitrary", "arbitrary"),
                                             vmem_limit_bytes=VMEM_LIMIT, has_side_effects=True),
    )(me_arr, h0, norm_w, win_shard)


def _seg_spec(rows_per_block, seg):
    return pl.BlockSpec((rows_per_block, D_MODEL), lambda n, seg=seg: (n, seg))


def _chunks_per_step(n_chunks):
    return next(c for c in (5, 3, 2, 1) if n_chunks % c == 0)


def _retention_fwd(proj, cos2, sin2, dec, ride):
    rows = proj.shape[0]
    n_chunks = rows // CHUNK
    per_step = _chunks_per_step(n_chunks)
    n_steps = n_chunks // per_step
    tm = per_step * CHUNK
    n_r = ride.n

    def body(q_ref, k_ref, v_ref, g_ref, c_ref, s_ref, dec_ref, *refs):
        o_ref, zr_ref, st_ref = refs[n_r:n_r + 3]
        state = refs[2 * n_r + 3]
        push = ride.push(refs[:n_r], refs[n_r + 3:2 * n_r + 3], refs[2 * n_r + 4:])

        @pl.when(pl.program_id(0) == 0)
        def _():
            push.start()
            state[...] = jnp.zeros_like(state)

        for h in range(HEADS):
            sl = slice(HEAD_DIM * h, HEAD_DIM * (h + 1))
            st = state[h]
            for c in range(per_step):
                rw = slice(CHUNK * c, CHUNK * (c + 1))
                cos_t, sin_t = c_ref[rw, :], s_ref[rw, :]
                qh = _rot(q_ref[rw, sl].astype(F32), cos_t, sin_t)
                kh = _rot(k_ref[rw, sl].astype(F32), cos_t, sin_t) * QK_SCALE
                qb, kb, vb = qh.astype(BF16), kh.astype(BF16), v_ref[rw, sl]
                s = _dot_nt(qb, kb) * dec_ref[0, h]
                st_ref[c, h] = st
                o = _dot(s.astype(BF16), vb) + _dot(qb, st.astype(BF16)) * dec_ref[1, h]
                st = st * dec_ref[3, h] + _dot_tn((kh * dec_ref[2, h]).astype(BF16), vb)
                o_ref[rw, sl] = o.astype(BF16)
                r = lax.rsqrt(jnp.mean(o * o, axis=-1, keepdims=True) + NORM_EPS)
                g = g_ref[rw, sl].astype(F32)
                zr_ref[rw, sl] = (g * _sigmoid(g) * (o * r)).astype(BF16)
            state[h] = st

        @pl.when(pl.program_id(0) == n_steps // 2)
        def _():
            push.relay()

        @pl.when(pl.program_id(0) == n_steps - 1)
        def _():
            push.wait()

    tab = pl.BlockSpec((tm, HEAD_DIM), lambda n: (n, 0))
    return pl.pallas_call(
        body, name="retention_fwd", grid=(n_steps,),
        in_specs=[_seg_spec(tm, 0), _seg_spec(tm, 1), _seg_spec(tm, 2), _seg_spec(tm, 3), tab, tab,
                  pl.BlockSpec((4, HEADS, CHUNK, CHUNK), lambda n: (0, 0, 0, 0))] + ride.specs(),
        out_specs=[pl.BlockSpec((tm, D_MODEL), lambda n: (n, 0)),
                   pl.BlockSpec((tm, D_MODEL), lambda n: (n, 0)),
                   pl.BlockSpec((per_step, HEADS, HEAD_DIM, HEAD_DIM), lambda n: (n, 0, 0, 0))] + ride.specs(),
        out_shape=[jax.ShapeDtypeStruct((rows, D_MODEL), BF16),
                   jax.ShapeDtypeStruct((rows, D_MODEL), BF16),
                   jax.ShapeDtypeStruct((n_chunks, HEADS, HEAD_DIM, HEAD_DIM), F32)] + ride.out_shapes,
        scratch_shapes=[pltpu.VMEM((HEADS, HEAD_DIM, HEAD_DIM), F32)] + ride.scratch(),
        compiler_params=pltpu.CompilerParams(dimension_semantics=("arbitrary",), vmem_limit_bytes=VMEM_LIMIT,
                                             has_side_effects=True),
    )(proj, proj, proj, proj, cos2, sin2, dec, *ride.arrays)


def _lru_gates(c, ba, bx, wa_ref, wx_ref):
    pre_r, pre_i = [], []
    for g in range(LRU_BLOCKS):
        cg = c[:, LRU_BLOCK * g:LRU_BLOCK * (g + 1)].astype(BF16)
        pre_r.append(_dot(cg, wa_ref[g]))
        pre_i.append(_dot(cg, wx_ref[g]))
    return _sigmoid(jnp.concatenate(pre_r, axis=1) + ba), _sigmoid(jnp.concatenate(pre_i, axis=1) + bx)


def _lru_decay(r, lam):
    sp = jnp.maximum(-lam, 0.0) + jnp.log(1.0 + jnp.exp(-jnp.abs(lam)))
    log_a = -LRU_C * r * sp
    a = jnp.exp(log_a)
    one_minus_a2 = -jnp.tanh(log_a) * (a * a + 1.0)
    inv_mult = lax.rsqrt(jnp.maximum(one_minus_a2, 1e-30))
    return a, one_minus_a2 * inv_mult, inv_mult, sp


def _conv_taps(xbuf, tm, cw_ref, cb_ref):
    c = cb_ref[...] + cw_ref[3:4, :] * xbuf[8:8 + tm, :]
    for back in (1, 2, 3):
        c = c + cw_ref[3 - back:4 - back, :] * xbuf[8 - back:8 - back + tm, :]
    return c


def _lru_fwd(proj, conv_w, conv_b, ba, bx, lam, wa_g, wx_g, ride):
    rows = proj.shape[0]
    tm = _tile(rows, 320)
    n_t = rows // tm
    n_r = ride.n

    def body(x_ref, gt_ref, cw_ref, cb_ref, ba_ref, bx_ref, lam_ref, wa_ref, wx_ref, *refs):
        hs_ref, zl_ref, cri_ref = refs[n_r:n_r + 3]
        xbuf, abuf, ubuf, hcar = refs[2 * n_r + 3:2 * n_r + 7]
        push = ride.push(refs[:n_r], refs[n_r + 3:2 * n_r + 3], refs[2 * n_r + 7:])
        i = pl.program_id(0)

        @pl.when(i == 0)
        def _():
            push.start()
            xbuf[0:8, :] = jnp.zeros((8, D_MODEL), F32)
            hcar[...] = jnp.zeros_like(hcar)

        xbuf[8:8 + tm, :] = x_ref[...].astype(F32)
        c = _conv_taps(xbuf, tm, cw_ref, cb_ref)
        xbuf[0:8, :] = xbuf[tm:tm + 8, :]
        r, ig = _lru_gates(c, ba_ref[...], bx_ref[...], wa_ref, wx_ref)
        a, mult, _, _ = _lru_decay(r, lam_ref[...])
        cri_ref[0] = c.astype(BF16)
        cri_ref[1] = r.astype(BF16)
        cri_ref[2] = ig.astype(BF16)
        row = i * tm + lax.broadcasted_iota(jnp.int32, (tm, 1), 0)
        abuf[...] = a
        ubuf[...] = jnp.where(row >= PAD_ROWS, mult * (ig * c), 0.0)

        sub = lax.broadcasted_iota(jnp.int32, (8, D_MODEL), 0)

        def block(b, carry):
            off = pl.multiple_of(b * 8, 8)
            av, uv = abuf[pl.ds(off, 8), :], ubuf[pl.ds(off, 8), :]
            for s in (1, 2, 4):
                us = jnp.where(sub >= s, pltpu.roll(uv, s, 0), 0.0)
                as_ = jnp.where(sub >= s, pltpu.roll(av, s, 0), 1.0)
                uv = uv + av * us
                av = av * as_
            hv = uv + av * carry
            ubuf[pl.ds(off, 8), :] = hv
            return hv[7:8, :]

        hcar[...] = lax.fori_loop(0, tm // 8, block, hcar[...])
        gl, _ = _gelu_parts(gt_ref[...].astype(F32))
        hs = ubuf[...]
        hs_ref[...] = hs.astype(BF16)
        zl_ref[...] = (gl * hs).astype(BF16)

        @pl.when(i == n_t // 2)
        def _():
            push.relay()

        @pl.when(i == n_t - 1)
        def _():
            push.wait()

    vec = pl.BlockSpec((1, D_MODEL), lambda i: (0, 0))
    mat = pl.BlockSpec((LRU_BLOCKS, LRU_BLOCK, LRU_BLOCK), lambda i: (0, 0, 0))
    row = pl.BlockSpec((tm, D_MODEL), lambda i: (i, 0))
    return pl.pallas_call(
        body, name="lru_fwd", grid=(n_t,),
        in_specs=[_seg_spec(tm, 4), _seg_spec(tm, 5), pl.BlockSpec((4, D_MODEL), lambda i: (0, 0)),
                  vec, vec, vec, vec, mat, mat] + ride.specs(),
        out_specs=[row, row, pl.BlockSpec((3, tm, D_MODEL), lambda i: (0, i, 0))] + ride.specs(),
        out_shape=[jax.ShapeDtypeStruct((rows, D_MODEL), BF16)] * 2
        + [jax.ShapeDtypeStruct((3, rows, D_MODEL), BF16)] + ride.out_shapes,
        scratch_shapes=[pltpu.VMEM((tm + 8, D_MODEL), F32), pltpu.VMEM((tm, D_MODEL), F32),
                        pltpu.VMEM((tm, D_MODEL), F32), pltpu.VMEM((1, D_MODEL), F32)] + ride.scratch(),
        compiler_params=pltpu.CompilerParams(dimension_semantics=("arbitrary",), vmem_limit_bytes=VMEM_LIMIT,
                                             has_side_effects=True),
    )(proj, proj, conv_w, conv_b, ba, bx, lam, wa_g, wx_g, *ride.arrays)


def _mix_fwd(zr, zl, proj, h0, wbr, wbl, wout, ride):
    rows = h0.shape[0]
    tm = _tile(rows, 640)
    n_t = rows // tm
    n_r = ride.n

    def body(zr_ref, zl_ref, ga_ref, gb_ref, h0_ref, wbr_ref, wbl_ref, wo_ref, *refs):
        h1_ref, yr_ref, yl_ref, mx_ref = refs[n_r:n_r + 4]
        push = ride.push(refs[:n_r], refs[n_r + 4:2 * n_r + 4], refs[2 * n_r + 4:])

        @pl.when(pl.program_id(0) == 0)
        def _():
            push.start()

        yr = _dot(zr_ref[...], wbr_ref[...])
        yl = _dot(zl_ref[...], wbl_ref[...])
        mixed = (_sigmoid(ga_ref[...].astype(F32)) * yr + _sigmoid(gb_ref[...].astype(F32)) * yl).astype(BF16)
        yr_ref[...] = yr.astype(BF16)
        yl_ref[...] = yl.astype(BF16)
        mx_ref[...] = mixed
        h1_ref[...] = h0_ref[...] + _dot(mixed, wo_ref[...])

        @pl.when(pl.program_id(0) == n_t // 2)
        def _():
            push.relay()

        @pl.when(pl.program_id(0) == n_t - 1)
        def _():
            push.wait()

    row = pl.BlockSpec((tm, D_MODEL), lambda i: (i, 0))
    wsp = pl.BlockSpec((D_MODEL, D_MODEL), lambda i: (0, 0))
    return pl.pallas_call(
        body, name="mix_fwd", grid=(n_t,),
        in_specs=[row, row, _seg_spec(tm, 6), _seg_spec(tm, 7), row, wsp, wsp, wsp] + ride.specs(),
        out_specs=[row, row, row, row] + ride.specs(),
        out_shape=[jax.ShapeDtypeStruct((rows, D_MODEL), F32)] + [jax.ShapeDtypeStruct((rows, D_MODEL), BF16)] * 3
        + ride.out_shapes,
        scratch_shapes=ride.scratch(),
        compiler_params=pltpu.CompilerParams(dimension_semantics=("arbitrary",), vmem_limit_bytes=VMEM_LIMIT,
                                             has_side_effects=True),
    )(zr, zl, proj, proj, h0, wbr, wbl, wout, *ride.arrays)


def _ffn_fwd_loss(h1, norm_w, wfi_g, wfo_g, final_w, target):
    rows = h1.shape[0]
    tm = _tile(rows, 320)
    piece = 64
    n_piece = tm // piece

    def body(h1_ref, nw_ref, wfi_ref, wfo_ref, fw_ref, *refs):
        t_refs = refs[:n_piece]
        u2_ref, g_ref, up_ref, act_ref, dh2_ref, red_ref = refs[n_piece:]
        i = pl.program_id(0)

        @pl.when(i == 0)
        def _():
            red_ref[...] = jnp.zeros_like(red_ref)

        x = h1_ref[...]
        rs = lax.rsqrt(jnp.mean(x * x, axis=-1, keepdims=True) + NORM_EPS)
        u2 = (x * rs * nw_ref[...]).astype(BF16)
        u2_ref[...] = u2
        ffn = None
        for d in range(FFN_GROUPS):
            cols = slice(FFN_GROUP * d, FFN_GROUP * (d + 1))
            g = _dot(u2, wfi_ref[d])
            up = _dot(u2, wfi_ref[d + FFN_GROUPS])
            act = (g * _sigmoid(g) * up).astype(BF16)
            g_ref[:, cols] = g.astype(BF16)
            up_ref[:, cols] = up.astype(BF16)
            act_ref[:, cols] = act
            part = _dot(act, wfo_ref[d])
            ffn = part if ffn is None else ffn + part

        h2 = x + ffn
        rs = lax.rsqrt(jnp.mean(h2 * h2, axis=-1, keepdims=True) + NORM_EPS)
        nh = h2 * rs
        fw = fw_ref[...]
        row = i * tm + lax.broadcasted_iota(jnp.int32, (tm, 1), 0)
        tgt = jnp.concatenate([t[...] for t in t_refs], axis=0)
        diff = jnp.where(row >= CHUNK, nh * fw - tgt, 0.0)
        dy = diff * (1.0 / D_MODEL)
        red_ref[0:1, :] += jnp.sum(diff * diff, axis=0, keepdims=True)
        red_ref[1:2, :] += jnp.sum(dy * nh, axis=0, keepdims=True)
        dn = dy * fw
        dh2_ref[...] = rs * (dn - nh * jnp.mean(dn * nh, axis=-1, keepdims=True))

    row = pl.BlockSpec((tm, D_MODEL), lambda i: (i, 0))
    vec = pl.BlockSpec((1, D_MODEL), lambda i: (0, 0))
    hid = pl.BlockSpec((tm, FFN_GROUPS * FFN_GROUP), lambda i: (i, 0))
    hid_shape = jax.ShapeDtypeStruct((rows, FFN_GROUPS * FFN_GROUP), BF16)
    resident = dict(pipeline_mode=pl.Buffered(1))
    head_pieces = CHUNK // piece
    t_specs = [pl.BlockSpec((piece, D_MODEL), lambda i, k=k: (jnp.maximum(i * n_piece + k - head_pieces, 0), 0))
               for k in range(n_piece)]
    return pl.pallas_call(
        body, name="ffn_fwd_loss", grid=(rows // tm,),
        in_specs=[row, vec,
                  pl.BlockSpec((2 * FFN_GROUPS, D_MODEL, FFN_GROUP), lambda i: (0, 0, 0), **resident),
                  pl.BlockSpec((FFN_GROUPS, FFN_GROUP, D_MODEL), lambda i: (0, 0, 0), **resident),
                  vec] + t_specs,
        out_specs=[row, hid, hid, hid, row, pl.BlockSpec((8, D_MODEL), lambda i: (0, 0))],
        out_shape=[jax.ShapeDtypeStruct((rows, D_MODEL), BF16), hid_shape, hid_shape, hid_shape,
                   jax.ShapeDtypeStruct((rows, D_MODEL), F32), jax.ShapeDtypeStruct((8, D_MODEL), F32)],
        compiler_params=_cparams(("arbitrary",)),
    )(h1, norm_w, wfi_g, wfo_g, final_w, *([target] * n_piece))


def _wgrad(a, b, ka, tn, out_dtype, b_halves=False):
    rows = a.shape[0]
    na = a.shape[1] // ka
    tm = _tile(rows, 1664)
    nm = rows // tm
    if b_halves:
        per_half = b.shape[2] // tn
        nb = 2 * per_half
        b_spec = pl.BlockSpec((None, tm, tn), lambda p, q, m: (q // per_half, m, q % per_half))
    else:
        nb = b.shape[1] // tn
        b_spec = pl.BlockSpec((tm, tn), lambda p, q, m: (m, q))

    def body(a_ref, b_ref, o_ref, acc):
        m = pl.program_id(2)

        @pl.when(m == 0)
        def _():
            acc[...] = jnp.zeros_like(acc)

        acc[...] += _dot_tn(a_ref[...].astype(BF16), b_ref[...].astype(BF16))

        @pl.when(m == nm - 1)
        def _():
            o_ref[...] = acc[...].astype(out_dtype)

    return pl.pallas_call(
        body, name="wgrad", grid=(na, nb, nm),
        in_specs=[pl.BlockSpec((tm, ka), lambda p, q, m: (m, p)), b_spec],
        out_specs=pl.BlockSpec((None, None, ka, tn), lambda p, q, m: (p, q, 0, 0)),
        out_shape=jax.ShapeDtypeStruct((na, nb, ka, tn), out_dtype),
        scratch_shapes=[pltpu.VMEM((ka, tn), F32)],
        compiler_params=_cparams(("parallel", "parallel", "arbitrary")),
    )(a, b)


WIN_NEAR = (2, 4, 3, 5, 1)
WIN_FAR = (6, 7)
WIN_ORDER = WIN_FAR + WIN_NEAR + (0,)


def _w_in_relation_at(jj):
    k = 0
    for pos in reversed(range(len(WIN_ORDER) - 1)):
        k = jnp.where(jj == pos, WIN_ORDER[pos], k)
    return k


def _wgrad_w_in(u, dproj, me_arr):
    rows = u.shape[0]
    tm = _tile(rows, 1664)
    nm = rows // tm
    n_near = len(WIN_NEAR)

    def body(me_ref, a_ref, b_ref, far_ref, land_ref, acc, sbuf, send_sems, recv_sems, own_sem):
        del me_ref
        jj, m = pl.program_id(0), pl.program_id(1)

        def near_copy(n):
            k = WIN_NEAR[n]
            return pltpu.make_async_remote_copy(src_ref=sbuf.at[n], dst_ref=land_ref.at[k], send_sem=send_sems.at[n],
                                                recv_sem=recv_sems.at[n], device_id=_peer(k)[0], device_id_type=MESH_ID)

        own_copy = pltpu.make_async_copy(sbuf.at[n_near], land_ref.at[0], own_sem)

        @pl.when(m == 0)
        def _():
            acc[...] = jnp.zeros_like(acc)

        acc[...] += _dot_tn(a_ref[...], b_ref[...])

        for pos, k in enumerate(WIN_ORDER):
            @pl.when(jnp.logical_and(jj == pos, m == nm - 1))
            def _(k=k):
                block = acc[...].astype(BF16)
                if k in WIN_FAR:
                    far_ref[...] = block
                elif k == 0:
                    sbuf[n_near] = block
                    own_copy.start()
                else:
                    sbuf[WIN_NEAR.index(k)] = block
                    near_copy(WIN_NEAR.index(k)).start()

        @pl.when(jnp.logical_and(jj == N_DEV - 1, m == nm - 1))
        def _():
            for n in range(n_near):
                near_copy(n).wait_recv()
            for n in range(n_near):
                near_copy(n).wait_send()
            own_copy.wait()

    grid_spec = pltpu.PrefetchScalarGridSpec(
        num_scalar_prefetch=1, grid=(N_DEV, nm),
        in_specs=[pl.BlockSpec((tm, D_MODEL), lambda jj, m, me: (m, 0)),
                  pl.BlockSpec((tm, D_MODEL), lambda jj, m, me: (m, me[0] ^ _w_in_relation_at(jj)))],
        out_specs=[pl.BlockSpec((None, D_MODEL, D_MODEL), lambda jj, m, me: (jnp.minimum(jj, len(WIN_FAR) - 1), 0, 0)),
                   ANY],
        scratch_shapes=[pltpu.VMEM((D_MODEL, D_MODEL), F32), pltpu.VMEM((n_near + 1, D_MODEL, D_MODEL), BF16),
                        pltpu.SemaphoreType.DMA((n_near,)), pltpu.SemaphoreType.DMA((n_near,)),
                        pltpu.SemaphoreType.DMA(())])
    return pl.pallas_call(
        body, name="wgrad_w_in", grid_spec=grid_spec,
        out_shape=[jax.ShapeDtypeStruct((len(WIN_FAR), D_MODEL, D_MODEL), BF16),
                   jax.ShapeDtypeStruct((n_near + 1, D_MODEL, D_MODEL), BF16)],
        compiler_params=pltpu.CompilerParams(dimension_semantics=("arbitrary", "arbitrary"),
                                             vmem_limit_bytes=VMEM_LIMIT, has_side_effects=True),
    )(me_arr, u, dproj)


def _ffn_bwd(dh2, g, up, h1, norm_w, wfi_g, wfo_g):
    rows = h1.shape[0]
    tm = _tile(rows, 320)

    def body(dh2_ref, g_ref, up_ref, h1_ref, nw_ref, wfi_ref, wfo_ref, dgu_ref, dh1_ref, dw_ref):
        @pl.when(pl.program_id(0) == 0)
        def _():
            dw_ref[...] = jnp.zeros_like(dw_ref)

        dh2 = dh2_ref[...]
        dh2_b = dh2.astype(BF16)
        du2 = None
        for d in range(FFN_GROUPS):
            cols = slice(FFN_GROUP * d, FFN_GROUP * (d + 1))
            dact = _dot_nt(dh2_b, wfo_ref[d])
            gv, uv = g_ref[:, cols].astype(F32), up_ref[:, cols].astype(F32)
            sg = _sigmoid(gv)
            dg = (dact * uv * (sg * (1.0 + gv * (1.0 - sg)))).astype(BF16)
            dup = (dact * (gv * sg)).astype(BF16)
            dgu_ref[0, :, cols] = dg
            dgu_ref[1, :, cols] = dup
            part = _dot_nt(dg, wfi_ref[d]) + _dot_nt(dup, wfi_ref[d + FFN_GROUPS])
            du2 = part if du2 is None else du2 + part
        dx, dw = _rms_bwd(h1_ref[...], nw_ref[...], du2)
        dw_ref[0:1, :] += dw
        dh1_ref[...] = dh2 + dx

    row = pl.BlockSpec((tm, D_MODEL), lambda i: (i, 0))
    vec = pl.BlockSpec((1, D_MODEL), lambda i: (0, 0))
    hid = pl.BlockSpec((tm, FFN_GROUPS * FFN_GROUP), lambda i: (i, 0))
    resident = dict(pipeline_mode=pl.Buffered(1))
    return pl.pallas_call(
        body, name="ffn_bwd", grid=(rows // tm,),
        in_specs=[row, hid, hid, row, vec,
                  pl.BlockSpec((2 * FFN_GROUPS, D_MODEL, FFN_GROUP), lambda i: (0, 0, 0), **resident),
                  pl.BlockSpec((FFN_GROUPS, FFN_GROUP, D_MODEL), lambda i: (0, 0, 0), **resident)],
        out_specs=[pl.BlockSpec((2, tm, FFN_GROUPS * FFN_GROUP), lambda i: (0, i, 0)), row,
                   pl.BlockSpec((8, D_MODEL), lambda i: (0, 0))],
        out_shape=[jax.ShapeDtypeStruct((2, rows, FFN_GROUPS * FFN_GROUP), BF16),
                   jax.ShapeDtypeStruct((rows, D_MODEL), F32), jax.ShapeDtypeStruct((8, D_MODEL), F32)],
        compiler_params=_cparams(("arbitrary",)),
    )(dh2, g, up, h1, norm_w, wfi_g, wfo_g)


def _mix_bwd(dh1, yr, yl, proj, wbr, wbl, wout):
    rows = dh1.shape[0]
    tm = _tile(rows, 640)

    def body(dh1_ref, yr_ref, yl_ref, ga_ref, gb_ref, wbr_ref, wbl_ref, wo_ref,
             dyr_ref, dyl_ref, dseg_ref, dzr_ref, dzl_ref):
        dmix = _dot_nt(dh1_ref[...].astype(BF16), wo_ref[...])
        sa, sb = _sigmoid(ga_ref[...].astype(F32)), _sigmoid(gb_ref[...].astype(F32))
        dyr = (dmix * sa).astype(BF16)
        dyl = (dmix * sb).astype(BF16)
        dyr_ref[...] = dyr
        dyl_ref[...] = dyl
        dseg_ref[:, 0:D_MODEL] = (dmix * yr_ref[...].astype(F32) * (sa * (1.0 - sa))).astype(BF16)
        dseg_ref[:, D_MODEL:2 * D_MODEL] = (dmix * yl_ref[...].astype(F32) * (sb * (1.0 - sb))).astype(BF16)
        dzr_ref[...] = _dot_nt(dyr, wbr_ref[...]).astype(BF16)
        dzl_ref[...] = _dot_nt(dyl, wbl_ref[...]).astype(BF16)

    row = pl.BlockSpec((tm, D_MODEL), lambda i: (i, 0))
    wsp = pl.BlockSpec((D_MODEL, D_MODEL), lambda i: (0, 0))
    bshape = jax.ShapeDtypeStruct((rows, D_MODEL), BF16)
    return pl.pallas_call(
        body, name="mix_bwd", grid=(rows // tm,),
        in_specs=[row, row, row, _seg_spec(tm, 6), _seg_spec(tm, 7), wsp, wsp, wsp],
        out_specs=[row, row, pl.BlockSpec((tm, 2 * D_MODEL), lambda i: (i, 3)), row, row],
        out_shape=[bshape, bshape, jax.ShapeDtypeStruct((rows, N_DEV * D_MODEL), BF16), bshape, bshape],
        compiler_params=_cparams(("parallel",)),
    )(dh1, yr, yl, proj, proj, wbr, wbl, wout)


S1_SHAPES = [
    jax.ShapeDtypeStruct((N_DEV, D_MODEL, FFN_GROUP), BF16),
    jax.ShapeDtypeStruct((N_DEV, FFN_OUT_SHARD, D_MODEL), BF16),
]


def _s1_parts(ins, p):
    return [ins[0].at[p], ins[1].at[p // 2, _half_rows(p), :]]


def _lru_bwd(dzl, hs, cri, proj, dproj, conv_w, lam, wa_g, wx_g, s1_grads):
    rows = dzl.shape[0]
    tm = _tile(rows, 320)
    nt = rows // tm
    t8 = tm // 8
    n_s1 = len(s1_grads)

    def body(dzl_ref, hs_ref, hsp_ref, cri_ref, x_ref, gt_ref, cw_ref, lam_ref, wa_ref, wx_ref, dproj_in, *refs):
        del dproj_in
        s1_refs = refs[:n_s1]
        dseg_ref, dwa_ref, dwx_ref, sm_ref = refs[n_s1:n_s1 + 4]
        land_refs = refs[n_s1 + 4:2 * n_s1 + 4]
        (xbuf, abuf, mbuf, ibuf, dbuf, dcbuf, dpr_s, dpi_s, sums, conv_sums, anext, dhcar,
         send_sems, recv_sems, loc_sems) = refs[2 * n_s1 + 4:]
        step = pl.program_id(0)
        i = nt - 1 - step
        push = _Push(lambda p: _s1_parts(s1_refs, p), lambda s: [r.at[s] for r in land_refs],
                     (send_sems, recv_sems, loc_sems), n_s1)

        @pl.when(step == 0)
        def _():
            push.start()
            dwa_ref[...] = jnp.zeros_like(dwa_ref)
            dwx_ref[...] = jnp.zeros_like(dwx_ref)
            sm_ref[...] = jnp.zeros_like(sm_ref)
            anext[...] = jnp.zeros_like(anext)
            dhcar[...] = jnp.zeros_like(dhcar)
            dcbuf[tm:tm + 8, :] = jnp.zeros((8, D_MODEL), F32)

        slab, lanes = 16, 256
        lam_v = lam_ref[...]
        xbuf[0:8, :] = jnp.where(i == 0, 0.0, hsp_ref[8:16, :].astype(F32))
        sums[...] = jnp.zeros_like(sums)

        def before_scan(k, carry):
            rw = pl.ds(pl.multiple_of(k * slab, slab), slab)
            for q in range(D_MODEL // lanes):
                ln = slice(lanes * q, lanes * (q + 1))
                a, mult, inv_mult, _ = _lru_decay(cri_ref[1, rw, ln].astype(F32), lam_v[:, ln])
                abuf[rw, ln] = a
                mbuf[rw, ln] = mult
                ibuf[rw, ln] = inv_mult
                gl, dgl = _gelu_parts(gt_ref[rw, ln].astype(F32))
                dzl_v = dzl_ref[rw, ln].astype(F32)
                hs_v = hs_ref[rw, ln].astype(F32)
                dseg_ref[rw, D_MODEL + lanes * q:D_MODEL + lanes * (q + 1)] = (dzl_v * hs_v * dgl).astype(BF16)
                dbuf[rw, ln] = dzl_v * gl
                xbuf[pl.ds(pl.multiple_of(k * slab + 8, 8), slab), ln] = hs_v
            return carry

        lax.fori_loop(0, tm // slab, before_scan, 0)

        sub = lax.broadcasted_iota(jnp.int32, (8, D_MODEL), 0)

        def block(k, carry):
            dh_next, a_next = carry
            off = pl.multiple_of((t8 - 1 - k) * 8, 8)
            a_blk = abuf[pl.ds(off, 8), :]
            av = jnp.where(sub < 7, pltpu.roll(a_blk, 7, 0), a_next)
            uv = dbuf[pl.ds(off, 8), :]
            for s in (1, 2, 4):
                us = jnp.where(sub < 8 - s, pltpu.roll(uv, 8 - s, 0), 0.0)
                as_ = jnp.where(sub < 8 - s, pltpu.roll(av, 8 - s, 0), 1.0)
                uv = uv + av * us
                av = av * as_
            hv = uv + av * dh_next
            dbuf[pl.ds(off, 8), :] = hv
            return hv[0:1, :], a_blk[0:1, :]

        dh_first, a_first = lax.fori_loop(0, t8, block, (dhcar[...], anext[...]))
        dhcar[...] = dh_first
        anext[...] = a_first

        sp = jnp.maximum(-lam_v, 0.0) + jnp.log(1.0 + jnp.exp(-jnp.abs(lam_v)))
        sub_q = lax.broadcasted_iota(jnp.int32, (8, lanes), 0)
        row16 = lax.broadcasted_iota(jnp.int32, (slab, 1), 0)

        def after_scan(k, carry):
            off = pl.multiple_of(k * slab, slab)
            rw = pl.ds(off, slab)
            for q in range(D_MODEL // lanes):
                ln = slice(lanes * q, lanes * (q + 1))
                before = xbuf[pl.ds(off, 8), ln]
                h_lo = xbuf[pl.ds(pl.multiple_of(off + 8, 8), 8), ln]
                h_hi = xbuf[pl.ds(pl.multiple_of(off + 16, 8), 8), ln]
                hprev = jnp.concatenate([jnp.where(sub_q >= 1, pltpu.roll(h_lo, 1, 0), before[7:8, :]),
                                         jnp.where(sub_q >= 1, pltpu.roll(h_hi, 1, 0), h_lo[7:8, :])], axis=0)
                c, r, ig = (cri_ref[n, rw, ln].astype(F32) for n in range(3))
                a, mult, inv_mult = abuf[rw, ln], mbuf[rw, ln], ibuf[rw, ln]
                dh = dbuf[rw, ln]
                duu = jnp.where(i * tm + off + row16 >= PAD_ROWS, dh, 0.0)
                t_mult = duu * mult
                dlog_a = dh * hprev * a - duu * ig * c * (a * a) * inv_mult
                dpr = dlog_a * (-LRU_C * sp[:, ln]) * r * (1.0 - r)
                dpi = t_mult * c * ig * (1.0 - ig)
                dpr_s[rw, ln] = dpr.astype(BF16)
                dpi_s[rw, ln] = dpi.astype(BF16)
                dcbuf[rw, ln] = t_mult * ig
                sums[0, :, ln] += dlog_a * r
                sums[1, :, ln] += dpr
                sums[2, :, ln] += dpi
            return carry

        lax.fori_loop(0, tm // slab, after_scan, 0)

        dcs = []
        for g in range(LRU_BLOCKS):
            sl = slice(LRU_BLOCK * g, LRU_BLOCK * (g + 1))
            cg = cri_ref[0, :, sl]
            dpr_b, dpi_b = dpr_s[:, sl], dpi_s[:, sl]
            dwa_ref[g] += _dot_tn(cg, dpr_b)
            dwx_ref[g] += _dot_tn(cg, dpi_b)
            dcs.append(_dot_nt(dpr_b, wa_ref[g]) + _dot_nt(dpi_b, wx_ref[g]))
        dc = dcbuf[0:tm, :] + jnp.concatenate(dcs, axis=1)

        dcbuf[0:tm, :] = dc
        conv_sums[...] = jnp.zeros_like(conv_sums)

        def conv_back(k, carry):
            off = pl.multiple_of(k * slab, slab)
            rw = pl.ds(off, slab)
            for q in range(D_MODEL // lanes):
                ln = slice(lanes * q, lanes * (q + 1))
                blocks = [dcbuf[pl.ds(pl.multiple_of(off + 8 * b, 8), 8), ln] for b in range(3)]
                x_v = x_ref[rw, ln].astype(F32)
                now = jnp.concatenate(blocks[:2], axis=0)
                dlin = cw_ref[3:4, ln] * now
                conv_sums[3, :, ln] += now * x_v
                conv_sums[4, :, ln] += now
                for back in (1, 2, 3):
                    turned = [pltpu.roll(b, 8 - back, 0) for b in blocks]
                    later = jnp.concatenate([jnp.where(sub_q < 8 - back, turned[0], turned[1]),
                                             jnp.where(sub_q < 8 - back, turned[1], turned[2])], axis=0)
                    dlin = dlin + cw_ref[3 - back:4 - back, ln] * later
                    conv_sums[3 - back, :, ln] += later * x_v
                dseg_ref[rw, ln] = dlin.astype(BF16)
            return carry

        lax.fori_loop(0, tm // slab, conv_back, 0)
        dcbuf[tm:tm + 8, :] = dcbuf[0:8, :]
        for n in range(5):
            sm_ref[n:n + 1, :] += jnp.sum(conv_sums[n], axis=0, keepdims=True)
        sm_ref[5:6, :] += jnp.sum(sums[1], axis=0, keepdims=True)
        sm_ref[6:7, :] += jnp.sum(sums[2], axis=0, keepdims=True)
        sm_ref[7:8, :] += jnp.sum(sums[0], axis=0, keepdims=True) * (LRU_C * _sigmoid(-lam_v))

        @pl.when(step == nt - 1)
        def _():
            push.wait()

    rowb = pl.BlockSpec((tm, D_MODEL), lambda s: (nt - 1 - s, 0))
    t16 = tm // 16
    prev8 = pl.BlockSpec((16, D_MODEL), lambda s: (jnp.maximum((nt - 1 - s) * t16 - 1, 0), 0))
    seg = lambda k: pl.BlockSpec((tm, D_MODEL), lambda s, k=k: (nt - 1 - s, k))
    vec = pl.BlockSpec((1, D_MODEL), lambda s: (0, 0))
    mat = pl.BlockSpec((LRU_BLOCKS, LRU_BLOCK, LRU_BLOCK), lambda s: (0, 0, 0))
    mshape = jax.ShapeDtypeStruct((LRU_BLOCKS, LRU_BLOCK, LRU_BLOCK), F32)
    n_in = 10
    return pl.pallas_call(
        body, name="lru_bwd", grid=(nt,),
        in_specs=[rowb, rowb, prev8, pl.BlockSpec((3, tm, D_MODEL), lambda s: (0, nt - 1 - s, 0)), seg(4), seg(5),
                  pl.BlockSpec((4, D_MODEL), lambda s: (0, 0)), vec, mat, mat, ANY] + [ANY] * n_s1,
        out_specs=[pl.BlockSpec((tm, 2 * D_MODEL), lambda s: (nt - 1 - s, 2)), mat, mat,
                   pl.BlockSpec((8, D_MODEL), lambda s: (0, 0))] + [ANY] * n_s1,
        out_shape=[jax.ShapeDtypeStruct(dproj.shape, dproj.dtype), mshape, mshape,
                   jax.ShapeDtypeStruct((8, D_MODEL), F32)] + S1_SHAPES,
        input_output_aliases={n_in: 0},
        scratch_shapes=[pltpu.VMEM((tm + 8, D_MODEL), F32), pltpu.VMEM((tm, D_MODEL), F32),
                        pltpu.VMEM((tm, D_MODEL), F32), pltpu.VMEM((tm, D_MODEL), F32),
                        pltpu.VMEM((tm, D_MODEL), F32), pltpu.VMEM((tm + 8, D_MODEL), F32),
                        pltpu.VMEM((tm, D_MODEL), BF16), pltpu.VMEM((tm, D_MODEL), BF16),
                        pltpu.VMEM((3, 16, D_MODEL), F32), pltpu.VMEM((5, 16, D_MODEL), F32),
                        pltpu.VMEM((1, D_MODEL), F32), pltpu.VMEM((1, D_MODEL), F32)] + _push_sems(n_s1),
        compiler_params=pltpu.CompilerParams(dimension_semantics=("arbitrary",), vmem_limit_bytes=VMEM_LIMIT,
                                             has_side_effects=True),
    )(dzl, hs, hs, cri, proj, proj, conv_w, lam, wa_g, wx_g, dproj, *s1_grads)


def _retention_bwd(dzr, o, proj, states, cos2, sin2, dec, dproj, ride):
    rows = dzr.shape[0]
    n_chunks = rows // CHUNK
    per_step = _chunks_per_step(n_chunks)
    n_steps = n_chunks // per_step
    tm = per_step * CHUNK
    n_r = ride.n

    def body(dzr_ref, o_ref, q_ref, k_ref, v_ref, g_ref, st_ref, c_ref, s_ref, dec_ref, dproj_in, *refs):
        del dproj_in
        dseg_ref = refs[n_r]
        dstate = refs[2 * n_r + 1]
        push = ride.push(refs[:n_r], refs[n_r + 1:2 * n_r + 1], refs[2 * n_r + 2:])

        @pl.when(pl.program_id(0) == 0)
        def _():
            push.start()
            dstate[...] = jnp.zeros_like(dstate)

        for h in range(HEADS):
            sl = slice(HEAD_DIM * h, HEAD_DIM * (h + 1))
            intra, qd, kd, cd = dec_ref[0, h], dec_ref[1, h], dec_ref[2, h], dec_ref[3, h]
            dst = dstate[h]
            for c in reversed(range(per_step)):
                rw = slice(CHUNK * c, CHUNK * (c + 1))
                cos_t, sin_t = c_ref[rw, :], s_ref[rw, :]
                o = o_ref[rw, sl].astype(F32)
                g = g_ref[rw, sl].astype(F32)
                dzr_v = dzr_ref[rw, sl].astype(F32)
                sg = _sigmoid(g)
                r = lax.rsqrt(jnp.mean(o * o, axis=-1, keepdims=True) + NORM_EPS)
                on = o * r
                dseg_ref[rw, 3 * D_MODEL + HEAD_DIM * h:3 * D_MODEL + HEAD_DIM * (h + 1)] = (
                    dzr_v * on * (sg * (1.0 + g * (1.0 - sg)))).astype(BF16)
                don = dzr_v * (g * sg)
                do = r * (don - on * jnp.mean(don * on, axis=-1, keepdims=True))
                dob = do.astype(BF16)

                qh = _rot(q_ref[rw, sl].astype(F32), cos_t, sin_t)
                kh = _rot(k_ref[rw, sl].astype(F32), cos_t, sin_t) * QK_SCALE
                qb, kb, vb = qh.astype(BF16), kh.astype(BF16), v_ref[rw, sl]
                s = (_dot_nt(qb, kb) * intra).astype(BF16)
                ds = (_dot_nt(dob, vb) * intra).astype(BF16)
                st_b = st_ref[c, h].astype(BF16)
                dst_b = dst.astype(BF16)
                dv = _dot_tn(s, dob) + _dot((kh * kd).astype(BF16), dst_b)
                dq = _dot(ds, kb) + _dot_nt(dob, st_b) * qd
                dk = _dot_tn(ds, qb) + _dot_nt(vb, dst_b) * kd
                dst = dst * cd + _dot_tn((qh * qd).astype(BF16), dob)
                dseg_ref[rw, 2 * D_MODEL + HEAD_DIM * h:2 * D_MODEL + HEAD_DIM * (h + 1)] = dv.astype(BF16)
                dseg_ref[rw, sl] = _rot_t(dq, cos_t, sin_t).astype(BF16)
                dseg_ref[rw, D_MODEL + HEAD_DIM * h:D_MODEL + HEAD_DIM * (h + 1)] = (
                    _rot_t(dk, cos_t, sin_t) * QK_SCALE).astype(BF16)
            dstate[h] = dst

        @pl.when(pl.program_id(0) == n_steps - 1)
        def _():
            push.wait()

    rev = lambda s: n_steps - 1 - s
    rowb = pl.BlockSpec((tm, D_MODEL), lambda s: (rev(s), 0))
    seg = lambda k: pl.BlockSpec((tm, D_MODEL), lambda s, k=k: (rev(s), k))
    tab = pl.BlockSpec((tm, HEAD_DIM), lambda s: (rev(s), 0))
    return pl.pallas_call(
        body, name="retention_bwd", grid=(n_steps,),
        in_specs=[rowb, rowb, seg(0), seg(1), seg(2), seg(3),
                  pl.BlockSpec((per_step, HEADS, HEAD_DIM, HEAD_DIM), lambda s: (rev(s), 0, 0, 0)), tab, tab,
                  pl.BlockSpec((4, HEADS, CHUNK, CHUNK), lambda s: (0, 0, 0, 0)), ANY] + ride.specs(),
        out_specs=[pl.BlockSpec((tm, 4 * D_MODEL), lambda s: (rev(s), 0))] + ride.specs(),
        out_shape=[jax.ShapeDtypeStruct(dproj.shape, dproj.dtype)] + ride.out_shapes,
        input_output_aliases={10: 0},
        scratch_shapes=[pltpu.VMEM((HEADS, HEAD_DIM, HEAD_DIM), F32)] + ride.scratch(),
        compiler_params=pltpu.CompilerParams(dimension_semantics=("arbitrary",), vmem_limit_bytes=VMEM_LIMIT,
                                             has_side_effects=True),
    )(dzr, o, proj, proj, proj, proj, states, cos2, sin2, dec, dproj, *ride.arrays)


S2_SHAPES = [
    jax.ShapeDtypeStruct((N_DEV, LRU_BLOCKS, LRU_ROWS, LRU_BLOCK), F32),
    jax.ShapeDtypeStruct((N_DEV, LRU_BLOCKS, LRU_ROWS, LRU_BLOCK), F32),
]


def _s2_parts(ins, p):
    return [r.at[p] for r in ins]


def _in_proj_bwd(dproj, win_g, h0, norm_w, dh1, d_win_far, s2_grads):
    rows = h0.shape[0]
    tm = _tile(rows, 320)
    n_i = rows // tm
    n_s2 = len(s2_grads)
    n_far = len(WIN_FAR)

    def body(dseg_ref, w_ref, h0_ref, nw_ref, dh1_ref, far_ref, *refs):
        s2_refs = refs[:n_s2]
        dh0_ref, dw_ref, far_land = refs[n_s2:n_s2 + 3]
        land_refs = refs[n_s2 + 3:2 * n_s2 + 3]
        send_sems, recv_sems, loc_sems, far_send_sems, far_recv_sems = refs[2 * n_s2 + 3:]
        i = pl.program_id(0)
        push = _Push(lambda p: _s2_parts(s2_refs, p), lambda s: [r.at[s] for r in land_refs],
                     (send_sems, recv_sems, loc_sems), n_s2)

        def far_copy(n):
            return pltpu.make_async_remote_copy(src_ref=far_ref.at[n], dst_ref=far_land.at[n],
                                                send_sem=far_send_sems.at[n], recv_sem=far_recv_sems.at[n],
                                                device_id=_peer(WIN_FAR[n])[0], device_id_type=MESH_ID)

        @pl.when(i == 0)
        def _():
            for n in range(n_far):
                far_copy(n).start()
            push.start()
            dw_ref[...] = jnp.zeros_like(dw_ref)

        du = _dot_nt(dseg_ref[:, 0:D_MODEL], w_ref[0])
        for j in range(1, N_DEV):
            du = du + _dot_nt(dseg_ref[:, D_MODEL * j:D_MODEL * (j + 1)], w_ref[j])
        dx, dw = _rms_bwd(h0_ref[...], nw_ref[...], du)
        dw_ref[0:1, :] += dw
        dh0_ref[...] = dh1_ref[...] + dx

        @pl.when(i == n_i - 1)
        def _():
            for n in range(n_far):
                far_copy(n).wait_recv()
            for n in range(n_far):
                far_copy(n).wait_send()
            push.wait()

    row = pl.BlockSpec((tm, D_MODEL), lambda i: (i, 0))
    vec = pl.BlockSpec((1, D_MODEL), lambda i: (0, 0))
    return pl.pallas_call(
        body, name="in_proj_bwd", grid=(n_i,),
        in_specs=[pl.BlockSpec((tm, N_DEV * D_MODEL), lambda i: (i, 0)),
                  pl.BlockSpec((N_DEV, D_MODEL, D_MODEL), lambda i: (0, 0, 0), pipeline_mode=pl.Buffered(1)),
                  row, vec, row, ANY] + [ANY] * n_s2,
        out_specs=[row, pl.BlockSpec((8, D_MODEL), lambda i: (0, 0)), ANY] + [ANY] * n_s2,
        out_shape=[jax.ShapeDtypeStruct((rows, D_MODEL), F32), jax.ShapeDtypeStruct((8, D_MODEL), F32),
                   jax.ShapeDtypeStruct((n_far, D_MODEL, D_MODEL), BF16)] + S2_SHAPES,
        scratch_shapes=_push_sems(n_s2) + [pltpu.SemaphoreType.DMA((n_far,)), pltpu.SemaphoreType.DMA((n_far,))],
        compiler_params=pltpu.CompilerParams(dimension_semantics=("arbitrary",),
                                             vmem_limit_bytes=VMEM_LIMIT, has_side_effects=True),
    )(dproj, win_g, h0, norm_w, dh1, d_win_far, *s2_grads)


def _adamw(g_slots, w, m, v, more_slots=None):
    slots, rows, cols = g_slots.shape
    extra = [] if more_slots is None else [more_slots]
    tr = rows
    for cand in (256, 128, 64, 32, 16, 8):
        if rows % cand == 0 and rows > cand:
            tr = cand
            break

    def body(g_ref, *refs):
        w_ref, m_ref, v_ref, go_ref, d_ref, mo_ref, vo_ref = refs[len(extra):]
        g = g_ref[0].astype(F32)
        for s in range(1, slots):
            g = g + g_ref[s].astype(F32)
        for more_ref in refs[:len(extra)]:
            for s in range(more_ref.shape[0]):
                g = g + more_ref[s].astype(F32)
        m2 = ADAM_B1 * m_ref[...] + (1.0 - ADAM_B1) * g
        v2 = ADAM_B2 * v_ref[...] + (1.0 - ADAM_B2) * (g * g)
        m_hat = m2 / (1.0 - ADAM_B1 ** ADAM_STEP)
        v_hat = v2 / (1.0 - ADAM_B2 ** ADAM_STEP)
        go_ref[...] = g
        d_ref[...] = -ADAM_LR * (m_hat / (jnp.sqrt(v_hat) + ADAM_EPS) + ADAM_WD * w_ref[...])
        mo_ref[...] = m2
        vo_ref[...] = v2

    blk = pl.BlockSpec((tr, cols), lambda i: (i, 0))
    shape = jax.ShapeDtypeStruct((rows, cols), F32)
    return pl.pallas_call(
        body, name="adamw", grid=(rows // tr,),
        in_specs=[pl.BlockSpec((slots, tr, cols), lambda i: (0, i, 0))]
        + [pl.BlockSpec((t.shape[0], tr, cols), lambda i: (0, i, 0)) for t in extra] + [blk, blk, blk],
        out_specs=[blk] * 4, out_shape=[shape] * 4,
        compiler_params=_cparams(("parallel",)),
    )(g_slots, *extra, w, m, v)


def _sum_slots(packs):
    slots, rows, cols = packs.shape

    def body(p_ref, o_ref):
        acc = p_ref[0]
        for s in range(1, slots):
            acc = acc + p_ref[s]
        o_ref[...] = acc

    return pl.pallas_call(
        body, name="sum_slots", out_shape=jax.ShapeDtypeStruct((rows, cols), F32),
        compiler_params=pltpu.CompilerParams(vmem_limit_bytes=VMEM_LIMIT),
    )(packs)


def _gather_small(small):
    shapes = [jax.ShapeDtypeStruct((N_DEV,) + small.shape, F32)]
    return _push_call("gather_small", [small], shapes,
                      lambda ins, p: list(ins), lambda outs, s: [r.at[s] for r in outs])[0]


def _share_pack(pack):
    shapes = [jax.ShapeDtypeStruct((N_DEV,) + pack.shape, F32)]
    return _push_call("share_pack", [pack], shapes,
                      lambda ins, p: list(ins), lambda outs, s: [r.at[s] for r in outs])[0]


PACK_MIX_NORM, PACK_CONV_W, PACK_CONV_B, PACK_BA, PACK_BX, PACK_LAM = 0, 8, 12, 13, 14, 15
PACK_FFN_NORM, PACK_SQ_ERR, PACK_FINAL_NORM, PACK_META = 16, 24, 25, 32


def kernel(x, meta_tokens, mix_norm_w, w_in, conv_w, conv_b, lru_wa, lru_ba, lru_wx, lru_bx, lru_lambda, w_branch_ret, w_branch_lru, w_out, ffn_norm_w, w_ffn_in, w_ffn_out, final_norm_w, loss_target, m_meta_tokens, m_mix_norm_w, m_w_in, m_conv_w, m_conv_b, m_lru_wa, m_lru_ba, m_lru_wx, m_lru_bx, m_lru_lambda, m_w_branch_ret, m_w_branch_lru, m_w_out, m_ffn_norm_w, m_w_ffn_in, m_w_ffn_out, m_final_norm_w, v_meta_tokens, v_mix_norm_w, v_w_in, v_conv_w, v_conv_b, v_lru_wa, v_lru_ba, v_lru_wx, v_lru_bx, v_lru_lambda, v_w_branch_ret, v_w_branch_lru, v_w_out, v_ffn_norm_w, v_w_ffn_in, v_w_ffn_out, v_final_norm_w):
    me = _my_index()
    pad4 = ((0, 4), (0, 0))
    fw = final_norm_w.reshape(1, D_MODEL)

    small = jnp.concatenate([meta_tokens, jnp.pad(conv_w[0], pad4)], axis=0)
    small_g = _gather_small(small)
    meta_full = small_g[:, :N_META].transpose(1, 0, 2).reshape(N_META, D_MODEL)
    conv_w_full = small_g[:, N_META:N_META + 4].transpose(1, 0, 2).reshape(4, D_MODEL)
    mixer_shards = [w_branch_ret[0].astype(BF16), w_branch_lru[0].astype(BF16), w_out[0].astype(BF16),
                    lru_wa[0].astype(BF16), lru_wx[0].astype(BF16)]
    wfi_shard = jnp.pad(w_ffn_in[0].astype(BF16), ((0, 0), (0, FFN_GROUP - FFN_SHARD)))
    own_slot = lambda ins, p: list(ins)
    part_of_owner = lambda ins, p: [r.at[p] for r in ins]

    rows = x.shape[1] + CHUNK
    h0 = jnp.concatenate([jnp.zeros((PAD_ROWS, D_MODEL), F32), meta_full, x[0]], axis=0)
    cos2, sin2 = _rope_tables(rows)
    dec = _retention_consts()

    me_arr = me.astype(jnp.int32).reshape(1)
    proj, u, win_g = _in_proj(h0, mix_norm_w, w_in[0].astype(BF16), me_arr)
    o, zr, states, wbr_g, wbl_g, wout_g, wa_g, wx_g = _retention_fwd(
        proj, cos2, sin2, dec, _mixer_weights_ride(mixer_shards))
    wbr, wbl, wout = (t.reshape(D_MODEL, D_MODEL) for t in (wbr_g, wbl_g, wout_g))
    wa_g, wx_g = _from_owners(wa_g), _from_owners(wx_g)
    gather_wfi = _Ride([wfi_shard], [jax.ShapeDtypeStruct((N_DEV, D_MODEL, FFN_GROUP), BF16)],
                       own_slot, _slot_of_sender, gather_by_chip=True)
    hs, zl, cri, wfi_g = _lru_fwd(proj, conv_w_full, conv_b, lru_ba, lru_bx, lru_lambda, wa_g, wx_g, gather_wfi)
    h1, yr, yl, mixed, wfo_g = _mix_fwd(zr, zl, proj, h0, wbr, wbl, wout, _wfo_ride(w_ffn_out[0].astype(BF16)))
    u2, g, up, act, dh2, red = _ffn_fwd_loss(h1, ffn_norm_w, wfi_g, wfo_g, fw, loss_target[0])

    d_wfo = _wgrad(act, dh2, FFN_GROUP, D_MODEL, BF16)[:, 0]
    dgu, dh1, dw_ffn_norm = _ffn_bwd(dh2, g, up, h1, ffn_norm_w, wfi_g, wfo_g)
    d_wfi = _wgrad(u2, dgu, D_MODEL, FFN_GROUP, BF16, b_halves=True)[0]
    d_wout = _wgrad(mixed, dh1, D_MODEL, D_MODEL, BF16)[0, 0]
    dyr, dyl, dproj, dzr, dzl = _mix_bwd(dh1, yr, yl, proj, wbr, wbl, wout)
    d_wbr = _wgrad(zr, dyr, D_MODEL, D_MODEL, BF16)[0, 0]
    d_wbl = _wgrad(zl, dyl, D_MODEL, D_MODEL, BF16)[0, 0]
    dproj, d_wa, d_wx, lru_small, r_fi, r_fo = _lru_bwd(
        dzl, hs, cri, proj, dproj, conv_w_full, lru_lambda, wa_g, wx_g, [d_wfi, d_wfo])
    mix_shape = jax.ShapeDtypeStruct((N_DEV, D_MODEL // N_DEV, D_MODEL), BF16)
    scatter_mix = _Ride([t.reshape(mix_shape.shape) for t in (d_wbr, d_wbl, d_wout)], [mix_shape] * 3,
                        part_of_owner, _slot_of_sender)
    dproj, r_br, r_bl, r_out = _retention_bwd(dzr, o, proj, states, cos2, sin2, dec, dproj, scatter_mix)
    d_win_far, r_in = _wgrad_w_in(u, dproj, me_arr)
    dh0, dw_mix_norm, r_in_far, r_wa, r_wx = _in_proj_bwd(dproj, win_g, h0, mix_norm_w, dh1, d_win_far,
                                                          [_by_owner(d_wa), _by_owner(d_wx)])
    grad_x = dh0[CHUNK:]

    pack = jnp.concatenate([dw_mix_norm, lru_small, dw_ffn_norm, red, dh0[PAD_ROWS:CHUNK]], axis=0)
    small_sum = _sum_slots(_share_pack(pack))
    loss = (0.5 / D_MODEL) * jnp.sum(small_sum[PACK_SQ_ERR])

    def big_update(slots, w, m, v, more_slots=None):
        shape = w.shape
        w2, m2, v2 = (t.reshape(slots.shape[1:]) for t in (w, m, v))
        return [t.reshape(shape) for t in _adamw(slots, w2, m2, v2, more_slots)]

    res = {}
    res["w_in"] = big_update(r_in, w_in, m_w_in, v_w_in, r_in_far)
    res["w_branch_ret"] = big_update(r_br, w_branch_ret, m_w_branch_ret, v_w_branch_ret)
    res["w_branch_lru"] = big_update(r_bl, w_branch_lru, m_w_branch_lru, v_w_branch_lru)
    res["w_out"] = big_update(r_out, w_out, m_w_out, v_w_out)
    res["w_ffn_in"] = big_update(r_fi[:, :, :FFN_SHARD], w_ffn_in, m_w_ffn_in, v_w_ffn_in)
    res["w_ffn_out"] = big_update(r_fo, w_ffn_out, m_w_ffn_out, v_w_ffn_out)
    res["lru_wa"] = big_update(r_wa.reshape(N_DEV, LRU_BLOCKS * LRU_ROWS, LRU_BLOCK), lru_wa, m_lru_wa, v_lru_wa)
    res["lru_wx"] = big_update(r_wx.reshape(N_DEV, LRU_BLOCKS * LRU_ROWS, LRU_BLOCK), lru_wx, m_lru_wx, v_lru_wx)

    col = me * HEAD_DIM
    g_meta = lax.dynamic_slice(small_sum, (PACK_META, col), (N_META, HEAD_DIM))
    g_conv = lax.dynamic_slice(small_sum, (PACK_CONV_W, col), (8, HEAD_DIM))
    small_names = ["mix_norm_w", "conv_b", "lru_ba", "lru_bx", "lru_lambda", "ffn_norm_w", "final_norm_w"]
    small_rows = [PACK_MIX_NORM, PACK_CONV_B, PACK_BA, PACK_BX, PACK_LAM, PACK_FFN_NORM, PACK_FINAL_NORM]
    small_w = [mix_norm_w, conv_b, lru_ba, lru_bx, lru_lambda, ffn_norm_w, fw]
    small_m = [m_mix_norm_w, m_conv_b, m_lru_ba, m_lru_bx, m_lru_lambda, m_ffn_norm_w, m_final_norm_w.reshape(1, -1)]
    small_v = [v_mix_norm_w, v_conv_b, v_lru_ba, v_lru_bx, v_lru_lambda, v_ffn_norm_w, v_final_norm_w.reshape(1, -1)]

    def pack_small(vec_list, meta_t, conv_t):
        return jnp.concatenate([t.reshape(8, HEAD_DIM) for t in vec_list] + [meta_t, jnp.pad(conv_t[0], pad4)], axis=0)

    g_small = jnp.concatenate([small_sum[r].reshape(8, HEAD_DIM) for r in small_rows] + [g_meta, g_conv], axis=0)
    outs_small = _adamw(g_small[None], pack_small(small_w, meta_tokens, conv_w),
                        pack_small(small_m, m_meta_tokens, m_conv_w), pack_small(small_v, v_meta_tokens, v_conv_w))
    for idx, name in enumerate(small_names):
        shape = final_norm_w.shape if name == "final_norm_w" else (1, D_MODEL)
        res[name] = [t[8 * idx:8 * idx + 8].reshape(shape) for t in outs_small]
    res["meta_tokens"] = [t[56:72] for t in outs_small]
    res["conv_w"] = [t[72:76].reshape(1, 4, HEAD_DIM) for t in outs_small]

    order = ["meta_tokens", "mix_norm_w", "w_in", "conv_w", "conv_b", "lru_wa", "lru_ba", "lru_wx", "lru_bx",
             "lru_lambda", "w_branch_ret", "w_branch_lru", "w_out", "ffn_norm_w", "w_ffn_in", "w_ffn_out",
             "final_norm_w"]
    out = [loss, grad_x[None]]
    for kind in range(4):
        out += [res[name][kind] for name in order]
    return tuple(out)
```

```python
import functools

import numpy as np
import jax
import jax.numpy as jnp
from jax import lax
from jax.experimental import pallas as pl
from jax.experimental.pallas import tpu as pltpu

F32 = jnp.float32
BF16 = jnp.bfloat16

D_MODEL = 1024
N_META = 16
CHUNK = 128
PAD_ROWS = CHUNK - N_META
HEADS = 8
HEAD_DIM = 128
ROPE_BASE = 10000.0
QK_SCALE = HEAD_DIM ** -0.5
LRU_BLOCKS = 4
LRU_BLOCK = 256
LRU_C = 8.0
FFN_HIDDEN = 2816
N_DEV = 8
FFN_SHARD = 2 * FFN_HIDDEN // N_DEV
FFN_GROUP = 768
FFN_GROUPS = 4
FFN_OUT_SHARD = FFN_HIDDEN // N_DEV
NORM_EPS = 1e-6

ADAM_LR = 0.001
ADAM_B1 = 0.9
ADAM_B2 = 0.999
ADAM_EPS = 1e-08
ADAM_WD = 0.01
ADAM_STEP = 10

VMEM_LIMIT = 56 * 1024 * 1024
MESH_ID = pl.DeviceIdType.MESH
ANY = pl.BlockSpec(memory_space=pl.ANY)


def _cparams(sem):
    return pltpu.CompilerParams(dimension_semantics=sem, vmem_limit_bytes=VMEM_LIMIT)


def _tile(rows, cap):
    t = cap - cap % 64
    while rows % t:
        t -= 64
    return t


def _dot(a, b):
    return jnp.dot(a, b, preferred_element_type=F32)


def _dot_nt(a, b):
    return lax.dot_general(a, b, (((1,), (1,)), ((), ())), preferred_element_type=F32)


def _dot_tn(a, b):
    return lax.dot_general(a, b, (((0,), (0,)), ((), ())), preferred_element_type=F32)


def _sigmoid(x):
    return 0.5 * jnp.tanh(0.5 * x) + 0.5


def _gelu_parts(x):
    k = 0.7978845608028654
    inner = k * (x + 0.044715 * x * x * x)
    t = jnp.tanh(inner)
    g = 0.5 * x * (1.0 + t)
    dg = 0.5 * (1.0 + t) + 0.5 * x * (1.0 - t * t) * k * (1.0 + 3.0 * 0.044715 * x * x)
    return g, dg


def _rot(x, cos2, sin2):
    return x * cos2 + pltpu.roll(x, HEAD_DIM // 2, 1) * sin2


def _rot_t(dx, cos2, sin2):
    return dx * cos2 - pltpu.roll(dx, HEAD_DIM // 2, 1) * sin2


def _rms_bwd(x, w, dy):
    rs = lax.rsqrt(jnp.mean(x * x, axis=-1, keepdims=True) + NORM_EPS)
    nh = x * rs
    dw = jnp.sum(dy * nh, axis=0, keepdims=True)
    dn = dy * w
    dx = rs * (dn - nh * jnp.mean(dn * nh, axis=-1, keepdims=True))
    return dx, dw


def _retention_consts():
    h = jnp.arange(HEADS, dtype=F32)
    log_g = jnp.log(1.0 - 2.0 ** (-5.0 - h))
    idx = jnp.arange(CHUNK, dtype=F32)
    diff = idx[:, None] - idx[None, :]
    intra = jnp.where(diff[None] >= 0, jnp.exp(jnp.maximum(diff, 0.0)[None] * log_g[:, None, None]), 0.0)
    q_decay = jnp.exp((idx + 1.0)[:, None] * log_g[None, :])
    k_decay = jnp.exp((CHUNK - 1.0 - idx)[:, None] * log_g[None, :])
    chunk_decay = jnp.exp(CHUNK * log_g)
    shape = (HEADS, CHUNK, CHUNK)
    qd = jnp.broadcast_to(q_decay.T[:, :, None], shape)
    kd = jnp.broadcast_to(k_decay.T[:, :, None], shape)
    cd = jnp.broadcast_to(chunk_decay[:, None, None], shape)
    return jnp.stack([intra, qd, kd, cd])


def _rope_tables(rows):
    pos = jnp.maximum(jnp.arange(rows) - PAD_ROWS, 0).astype(F32)
    inv_freq = ROPE_BASE ** (-jnp.arange(0, HEAD_DIM, 2, dtype=F32) / HEAD_DIM)
    ang = pos[:, None] * inv_freq[None, :]
    cos, sin = jnp.cos(ang), jnp.sin(ang)
    return jnp.concatenate([cos, cos], axis=1), jnp.concatenate([-sin, sin], axis=1)


def _my_index():
    return 4 * lax.axis_index("x") + 2 * lax.axis_index("y") + lax.axis_index("c")


def _peer(k):
    x, y, c = lax.axis_index("x"), lax.axis_index("y"), lax.axis_index("c")
    px = 1 - x if k & 4 else x
    py = 1 - y if k & 2 else y
    pc = 1 - c if k & 1 else c
    return (px, py, pc), 4 * px + 2 * py + pc


def _push_sems(n_arr):
    n_rem = (N_DEV - 1) * n_arr
    return [pltpu.SemaphoreType.DMA((n_rem,)), pltpu.SemaphoreType.DMA((n_rem,)), pltpu.SemaphoreType.DMA((n_arr,))]


class _Push:
    def __init__(self, send_part, land_slot, sems, n_arr):
        self.send_part, self.land_slot, self.n_arr = send_part, land_slot, n_arr
        self.send_sems, self.recv_sems, self.loc_sems = sems

    def _remote(self, k, a, src, dst, pos):
        idx = (k - 1) * self.n_arr + a
        return pltpu.make_async_remote_copy(src_ref=src, dst_ref=dst, send_sem=self.send_sems.at[idx],
                                            recv_sem=self.recv_sems.at[idx], device_id=pos, device_id_type=MESH_ID)

    def _outgoing(self):
        me = _my_index()
        land = self.land_slot(me)
        remote = []
        for k in range(1, N_DEV):
            pos, p = _peer(k)
            src = self.send_part(p)
            remote += [self._remote(k, a, src[a], land[a], pos) for a in range(self.n_arr)]
        own = self.send_part(me)
        local = [pltpu.make_async_copy(own[a], land[a], self.loc_sems.at[a]) for a in range(self.n_arr)]
        return remote, local

    def start(self):
        remote, local = self._outgoing()
        for cp in remote + local:
            cp.start()

    def wait_recv_from(self, k):
        own = self.send_part(_my_index())
        pos, p = _peer(k)
        land = self.land_slot(p)
        for a in range(self.n_arr):
            self._remote(k, a, own[a], land[a], pos).wait_recv()

    def wait_sends(self):
        remote, local = self._outgoing()
        for cp in remote:
            cp.wait_send()
        for cp in local:
            cp.wait()

    def wait(self):
        for k in range(1, N_DEV):
            self.wait_recv_from(k)
        self.wait_sends()


DIRECT = (1, 2, 4, 6)
RELAYED = (2, 4, 6)


def _gather_by_chip_sems(n_arr):
    direct, relayed = len(DIRECT) * n_arr, len(RELAYED) * n_arr
    return [pltpu.SemaphoreType.DMA((direct,)), pltpu.SemaphoreType.DMA((direct,)),
            pltpu.SemaphoreType.DMA((relayed,)), pltpu.SemaphoreType.DMA((relayed,)), pltpu.SemaphoreType.DMA((n_arr,))]


class _GatherByChip:
    def __init__(self, srcs, land_slot, sems, n_arr):
        self.srcs, self.land_slot, self.n_arr = srcs, land_slot, n_arr
        self.send_sems, self.recv_sems, self.relay_send_sems, self.relay_recv_sems, self.loc_sems = sems

    def _direct(self, k, a, slot):
        idx = DIRECT.index(k) * self.n_arr + a
        return pltpu.make_async_remote_copy(src_ref=self.srcs[a], dst_ref=self.land_slot(slot)[a],
                                            send_sem=self.send_sems.at[idx], recv_sem=self.recv_sems.at[idx],
                                            device_id=_peer(k)[0], device_id_type=MESH_ID)

    def _relay(self, q, a, slot):
        idx = RELAYED.index(q) * self.n_arr + a
        block = self.land_slot(slot)[a]
        return pltpu.make_async_remote_copy(src_ref=block, dst_ref=block, send_sem=self.relay_send_sems.at[idx],
                                            recv_sem=self.relay_recv_sems.at[idx], device_id=_peer(1)[0],
                                            device_id_type=MESH_ID)

    def _own(self, a):
        return pltpu.make_async_copy(self.srcs[a], self.land_slot(_my_index())[a], self.loc_sems.at[a])

    def start(self):
        me = _my_index()
        for k in DIRECT:
            for a in range(self.n_arr):
                self._direct(k, a, me).start()
        for a in range(self.n_arr):
            self._own(a).start()

    def relay(self):
        for q in RELAYED:
            p = _peer(q)[1]
            for a in range(self.n_arr):
                self._direct(q, a, p).wait_recv()
                self._relay(q, a, p).start()

    def wait(self):
        me = _my_index()
        for a in range(self.n_arr):
            self._direct(1, a, _peer(1)[1]).wait_recv()
        for q in RELAYED:
            for a in range(self.n_arr):
                self._relay(q, a, _peer(q + 1)[1]).wait_recv()
        for k in DIRECT:
            for a in range(self.n_arr):
                self._direct(k, a, me).wait_send()
        for q in RELAYED:
            for a in range(self.n_arr):
                self._relay(q, a, _peer(q)[1]).wait_send()
        for a in range(self.n_arr):
            self._own(a).wait()


class _Ride:
    def __init__(self, arrays, out_shapes, send_part, land_slot, zero_dsts=None, zero_shape=None, n_zero=0,
                 gather_by_chip=False):
        self.arrays, self.out_shapes = list(arrays), list(out_shapes)
        self.send_part, self.land_slot, self.n = send_part, land_slot, len(arrays)
        self.zero_dsts, self.zero_shape, self.n_zero = zero_dsts, zero_shape, n_zero
        self.gather_by_chip = gather_by_chip

    def specs(self):
        return [ANY] * self.n

    def scratch(self):
        extra = [pltpu.SemaphoreType.DMA((self.n_zero,)), pltpu.VMEM(self.zero_shape, BF16)] if self.n_zero else []
        sems = _gather_by_chip_sems(self.n) if self.gather_by_chip else _push_sems(self.n)
        return sems + extra

    def push(self, in_refs, out_refs, scratch):
        ride = self
        n_sems = 5 if self.gather_by_chip else 3
        land = lambda s: ride.land_slot(out_refs, s)
        if self.gather_by_chip:
            push = _GatherByChip(list(in_refs), land, tuple(scratch[:n_sems]), self.n)
        else:
            push = _Push(lambda p: ride.send_part(in_refs, p), land, tuple(scratch[:n_sems]), self.n)

        class Both:
            def _fills(self):
                if not ride.n_zero:
                    return []
                zsems, zbuf = scratch[n_sems], scratch[n_sems + 1]
                return [pltpu.make_async_copy(zbuf, dst, zsems.at[z]) for z, dst in enumerate(ride.zero_dsts(out_refs))]

            def start(self):
                push.start()
                if ride.n_zero:
                    scratch[n_sems + 1][...] = jnp.zeros(ride.zero_shape, BF16)
                for cp in self._fills():
                    cp.start()

            def relay(self):
                if ride.gather_by_chip:
                    push.relay()

            def wait(self):
                push.wait()
                for cp in self._fills():
                    cp.wait()

        return Both()


def _slot_of_sender(out_refs, s):
    return [r.at[s] for r in out_refs]


def _push_call(name, arrays, out_shapes, send_part, land_slot):
    n_arr = len(arrays)

    def body(*refs):
        ins, outs, sems = refs[:n_arr], refs[n_arr:2 * n_arr], refs[2 * n_arr:]
        push = _Push(lambda p: send_part(ins, p), lambda s: land_slot(outs, s), sems, n_arr)
        push.start()
        push.wait()

    return pl.pallas_call(
        body, name=name, in_specs=[ANY] * n_arr, out_specs=[ANY] * n_arr, out_shape=out_shapes,
        scratch_shapes=_push_sems(n_arr), compiler_params=pltpu.CompilerParams(has_side_effects=True),
    )(*arrays)


LRU_ROWS = LRU_BLOCK // N_DEV
FFN_PAD_ROWS = FFN_GROUP - 2 * FFN_OUT_SHARD


def _half_rows(d):
    return pl.ds(pl.multiple_of((d % 2) * FFN_OUT_SHARD, 16), FFN_OUT_SHARD)


MIXER_SHAPES = [
    jax.ShapeDtypeStruct((N_DEV, D_MODEL // N_DEV, D_MODEL), BF16),
    jax.ShapeDtypeStruct((N_DEV, D_MODEL // N_DEV, D_MODEL), BF16),
    jax.ShapeDtypeStruct((N_DEV, D_MODEL // N_DEV, D_MODEL), BF16),
    jax.ShapeDtypeStruct((N_DEV, LRU_BLOCKS, LRU_ROWS, LRU_BLOCK), BF16),
    jax.ShapeDtypeStruct((N_DEV, LRU_BLOCKS, LRU_ROWS, LRU_BLOCK), BF16),
]


def _by_owner(t):
    return t.reshape(LRU_BLOCKS, N_DEV, LRU_ROWS, LRU_BLOCK).transpose(1, 0, 2, 3)


def _from_owners(t):
    return t.transpose(1, 0, 2, 3).reshape(LRU_BLOCKS, LRU_BLOCK, LRU_BLOCK)


def _mixer_weights_ride(shards):
    return _Ride(shards, MIXER_SHAPES, lambda ins, p: list(ins), _slot_of_sender, gather_by_chip=True)


def _wfo_ride(shard):
    zero_dsts = lambda outs: [outs[0].at[g, pl.ds(2 * FFN_OUT_SHARD, FFN_PAD_ROWS), :] for g in range(FFN_GROUPS)]
    return _Ride([shard], [jax.ShapeDtypeStruct((FFN_GROUPS, FFN_GROUP, D_MODEL), BF16)], lambda ins, p: list(ins),
                 lambda outs, d: [outs[0].at[d // 2, _half_rows(d), :]], zero_dsts, (FFN_PAD_ROWS, D_MODEL), FFN_GROUPS,
                 gather_by_chip=True)


def _arrival_rank_to_relation(jj):
    return jnp.where(jj == 3, 4, jnp.where(jj == 4, 3, jj))


def _in_proj(h0, norm_w, win_shard, me_arr):
    rows = h0.shape[0]
    tm = _tile(rows, 1664)
    n_i = rows // tm

    direct, relayed = DIRECT, RELAYED

    def body(me_ref, h_ref, nw_ref, wsh_ref, proj_ref, u_ref, wing_ref, u_all, wbuf, copy_sem,
             send_sems, recv_sems, relay_send_sems, relay_recv_sems, own_sem):
        del me_ref
        jj, i = pl.program_id(0), pl.program_id(1)
        me = _my_index()
        sibling = _peer(1)[0]

        def direct_copy(k, slot):
            n = direct.index(k)
            return pltpu.make_async_remote_copy(src_ref=wsh_ref, dst_ref=wing_ref.at[slot], send_sem=send_sems.at[n],
                                                recv_sem=recv_sems.at[n], device_id=_peer(k)[0], device_id_type=MESH_ID)

        def relay_copy(q, slot):
            n = relayed.index(q)
            return pltpu.make_async_remote_copy(src_ref=wing_ref.at[slot], dst_ref=wing_ref.at[slot],
                                                send_sem=relay_send_sems.at[n], recv_sem=relay_recv_sems.at[n],
                                                device_id=sibling, device_id_type=MESH_ID)

        own_slot = pltpu.make_async_copy(wsh_ref, wing_ref.at[me], own_sem)

        @pl.when(jnp.logical_and(jj == 0, i == 0))
        def _():
            for k in direct:
                direct_copy(k, me).start()
            own_slot.start()
            own = pltpu.make_async_copy(wsh_ref, wbuf, copy_sem)
            own.start()
            own.wait()

        for k in range(1, N_DEV):
            rank = {3: 4, 4: 3}.get(k, k)

            @pl.when(jnp.logical_and(jj == rank, i == 0))
            def _(k=k):
                p = _peer(k)[1]
                if k in direct:
                    direct_copy(k, p).wait_recv()
                    if k in relayed:
                        relay_copy(k, p).start()
                else:
                    relay_copy(k - 1, p).wait_recv()
                landed = pltpu.make_async_copy(wing_ref.at[p], wbuf, copy_sem)
                landed.start()
                landed.wait()

        rows_i = pl.ds(pl.multiple_of(i * tm, tm), tm)

        @pl.when(jj == 0)
        def _():
            x = h_ref[...]
            rs = lax.rsqrt(jnp.mean(x * x, axis=-1, keepdims=True) + NORM_EPS)
            u = (x * rs * nw_ref[...]).astype(BF16)
            u_all[rows_i, :] = u
            u_ref[...] = u
        proj_ref[...] = _dot(u_all[rows_i, :], wbuf[...]).astype(BF16)

        @pl.when(jnp.logical_and(jj == N_DEV - 1, i == n_i - 1))
        def _():
            for k in direct:
                direct_copy(k, me).wait_send()
            for q in relayed:
                relay_copy(q, _peer(q)[1]).wait_send()
            own_slot.wait()

    first_pass = lambda jj, i: jnp.where(jj == 0, i, n_i - 1)
    grid_spec = pltpu.PrefetchScalarGridSpec(
        num_scalar_prefetch=1, grid=(N_DEV, n_i),
        in_specs=[pl.BlockSpec((tm, D_MODEL), lambda jj, i, me: (first_pass(jj, i), 0)),
                  pl.BlockSpec((1, D_MODEL), lambda jj, i, me: (0, 0)), ANY],
        out_specs=[pl.BlockSpec((tm, D_MODEL), lambda jj, i, me: (i, me[0] ^ _arrival_rank_to_relation(jj))),
                   pl.BlockSpec((tm, D_MODEL), lambda jj, i, me: (first_pass(jj, i), 0)), ANY],
        scratch_shapes=[pltpu.VMEM((rows, D_MODEL), BF16), pltpu.VMEM((D_MODEL, D_MODEL), BF16),
                        pltpu.SemaphoreType.DMA(()),
                        pltpu.SemaphoreType.DMA((len(direct),)), pltpu.SemaphoreType.DMA((len(direct),)),
                        pltpu.SemaphoreType.DMA((len(relayed),)), pltpu.SemaphoreType.DMA((len(relayed),)),
                        pltpu.SemaphoreType.DMA(())])
    return pl.pallas_call(
        body, name="in_proj", grid_spec=grid_spec,
        out_shape=[jax.ShapeDtypeStruct((rows, N_DEV * D_MODEL), BF16),
                   jax.ShapeDtypeStruct((rows, D_MODEL), BF16),
                   jax.ShapeDtypeStruct((N_DEV, D_MODEL, D_MODEL), BF16)],
        compiler_params=pltpu.CompilerParams(dimension_semantics=("arbitrary", "arbitrary"),
                                             vmem_limit_bytes=VMEM_LIMIT, has_side_effects=True),
    )(me_arr, h0, norm_w, win_shard)


def _seg_spec(rows_per_block, seg):
    return pl.BlockSpec((rows_per_block, D_MODEL), lambda n, seg=seg: (n, seg))


def _chunks_per_step(n_chunks):
    return next(c for c in (5, 3, 2, 1) if n_chunks % c == 0)


def _retention_fwd(proj, cos2, sin2, dec, ride):
    rows = proj.shape[0]
    n_chunks = rows // CHUNK
    per_step = _chunks_per_step(n_chunks)
    n_steps = n_chunks // per_step
    tm = per_step * CHUNK
    n_r = ride.n

    def body(q_ref, k_ref, v_ref, g_ref, c_ref, s_ref, dec_ref, *refs):
        o_ref, zr_ref, st_ref = refs[n_r:n_r + 3]
        state = refs[2 * n_r + 3]
        push = ride.push(refs[:n_r], refs[n_r + 3:2 * n_r + 3], refs[2 * n_r + 4:])

        @pl.when(pl.program_id(0) == 0)
        def _():
            push.start()
            state[...] = jnp.zeros_like(state)

        for h in range(HEADS):
            sl = slice(HEAD_DIM * h, HEAD_DIM * (h + 1))
            st = state[h]
            for c in range(per_step):
                rw = slice(CHUNK * c, CHUNK * (c + 1))
                cos_t, sin_t = c_ref[rw, :], s_ref[rw, :]
                qh = _rot(q_ref[rw, sl].astype(F32), cos_t, sin_t)
                kh = _rot(k_ref[rw, sl].astype(F32), cos_t, sin_t) * QK_SCALE
                qb, kb, vb = qh.astype(BF16), kh.astype(BF16), v_ref[rw, sl]
                s = _dot_nt(qb, kb) * dec_ref[0, h]
                st_ref[c, h] = st
                o = _dot(s.astype(BF16), vb) + _dot(qb, st.astype(BF16)) * dec_ref[1, h]
                st = st * dec_ref[3, h] + _dot_tn((kh * dec_ref[2, h]).astype(BF16), vb)
                o_ref[rw, sl] = o.astype(BF16)
                r = lax.rsqrt(jnp.mean(o * o, axis=-1, keepdims=True) + NORM_EPS)
                g = g_ref[rw, sl].astype(F32)
                zr_ref[rw, sl] = (g * _sigmoid(g) * (o * r)).astype(BF16)
            state[h] = st

        @pl.when(pl.program_id(0) == n_steps // 2)
        def _():
            push.relay()

        @pl.when(pl.program_id(0) == n_steps - 1)
        def _():
            push.wait()

    tab = pl.BlockSpec((tm, HEAD_DIM), lambda n: (n, 0))
    return pl.pallas_call(
        body, name="retention_fwd", grid=(n_steps,),
        in_specs=[_seg_spec(tm, 0), _seg_spec(tm, 1), _seg_spec(tm, 2), _seg_spec(tm, 3), tab, tab,
                  pl.BlockSpec((4, HEADS, CHUNK, CHUNK), lambda n: (0, 0, 0, 0))] + ride.specs(),
        out_specs=[pl.BlockSpec((tm, D_MODEL), lambda n: (n, 0)),
                   pl.BlockSpec((tm, D_MODEL), lambda n: (n, 0)),
                   pl.BlockSpec((per_step, HEADS, HEAD_DIM, HEAD_DIM), lambda n: (n, 0, 0, 0))] + ride.specs(),
        out_shape=[jax.ShapeDtypeStruct((rows, D_MODEL), BF16),
                   jax.ShapeDtypeStruct((rows, D_MODEL), BF16),
                   jax.ShapeDtypeStruct((n_chunks, HEADS, HEAD_DIM, HEAD_DIM), F32)] + ride.out_shapes,
        scratch_shapes=[pltpu.VMEM((HEADS, HEAD_DIM, HEAD_DIM), F32)] + ride.scratch(),
        compiler_params=pltpu.CompilerParams(dimension_semantics=("arbitrary",), vmem_limit_bytes=VMEM_LIMIT,
                                             has_side_effects=True),
    )(proj, proj, proj, proj, cos2, sin2, dec, *ride.arrays)


def _lru_gates(c, ba, bx, wa_ref, wx_ref):
    pre_r, pre_i = [], []
    for g in range(LRU_BLOCKS):
        cg = c[:, LRU_BLOCK * g:LRU_BLOCK * (g + 1)].astype(BF16)
        pre_r.append(_dot(cg, wa_ref[g]))
        pre_i.append(_dot(cg, wx_ref[g]))
    return _sigmoid(jnp.concatenate(pre_r, axis=1) + ba), _sigmoid(jnp.concatenate(pre_i, axis=1) + bx)


def _lru_decay(r, lam):
    sp = jnp.maximum(-lam, 0.0) + jnp.log(1.0 + jnp.exp(-jnp.abs(lam)))
    log_a = -LRU_C * r * sp
    a = jnp.exp(log_a)
    one_minus_a2 = -jnp.tanh(log_a) * (a * a + 1.0)
    inv_mult = lax.rsqrt(jnp.maximum(one_minus_a2, 1e-30))
    return a, one_minus_a2 * inv_mult, inv_mult, sp


def _conv_taps(xbuf, tm, cw_ref, cb_ref):
    c = cb_ref[...] + cw_ref[3:4, :] * xbuf[8:8 + tm, :]
    for back in (1, 2, 3):
        c = c + cw_ref[3 - back:4 - back, :] * xbuf[8 - back:8 - back + tm, :]
    return c


def _lru_fwd(proj, conv_w, conv_b, ba, bx, lam, wa_g, wx_g, ride):
    rows = proj.shape[0]
    tm = _tile(rows, 320)
    n_t = rows // tm
    n_r = ride.n

    def body(x_ref, gt_ref, cw_ref, cb_ref, ba_ref, bx_ref, lam_ref, wa_ref, wx_ref, *refs):
        hs_ref, zl_ref, cri_ref = refs[n_r:n_r + 3]
        xbuf, abuf, ubuf, hcar = refs[2 * n_r + 3:2 * n_r + 7]
        push = ride.push(refs[:n_r], refs[n_r + 3:2 * n_r + 3], refs[2 * n_r + 7:])
        i = pl.program_id(0)

        @pl.when(i == 0)
        def _():
            push.start()
            xbuf[0:8, :] = jnp.zeros((8, D_MODEL), F32)
            hcar[...] = jnp.zeros_like(hcar)

        xbuf[8:8 + tm, :] = x_ref[...].astype(F32)
        c = _conv_taps(xbuf, tm, cw_ref, cb_ref)
        xbuf[0:8, :] = xbuf[tm:tm + 8, :]
        r, ig = _lru_gates(c, ba_ref[...], bx_ref[...], wa_ref, wx_ref)
        a, mult, _, _ = _lru_decay(r, lam_ref[...])
        cri_ref[0] = c.astype(BF16)
        cri_ref[1] = r.astype(BF16)
        cri_ref[2] = ig.astype(BF16)
        row = i * tm + lax.broadcasted_iota(jnp.int32, (tm, 1), 0)
        abuf[...] = a
        ubuf[...] = jnp.where(row >= PAD_ROWS, mult * (ig * c), 0.0)

        sub = lax.broadcasted_iota(jnp.int32, (8, D_MODEL), 0)

        def block(b, carry):
            off = pl.multiple_of(b * 8, 8)
            av, uv = abuf[pl.ds(off, 8), :], ubuf[pl.ds(off, 8), :]
            for s in (1, 2, 4):
                us = jnp.where(sub >= s, pltpu.roll(uv, s, 0), 0.0)
                as_ = jnp.where(sub >= s, pltpu.roll(av, s, 0), 1.0)
                uv = uv + av * us
                av = av * as_
            hv = uv + av * carry
            ubuf[pl.ds(off, 8), :] = hv
            return hv[7:8, :]

        hcar[...] = lax.fori_loop(0, tm // 8, block, hcar[...])
        gl, _ = _gelu_parts(gt_ref[...].astype(F32))
        hs = ubuf[...]
        hs_ref[...] = hs.astype(BF16)
        zl_ref[...] = (gl * hs).astype(BF16)

        @pl.when(i == n_t // 2)
        def _():
            push.relay()

        @pl.when(i == n_t - 1)
        def _():
            push.wait()

    vec = pl.BlockSpec((1, D_MODEL), lambda i: (0, 0))
    mat = pl.BlockSpec((LRU_BLOCKS, LRU_BLOCK, LRU_BLOCK), lambda i: (0, 0, 0))
    row = pl.BlockSpec((tm, D_MODEL), lambda i: (i, 0))
    return pl.pallas_call(
        body, name="lru_fwd", grid=(n_t,),
        in_specs=[_seg_spec(tm, 4), _seg_spec(tm, 5), pl.BlockSpec((4, D_MODEL), lambda i: (0, 0)),
                  vec, vec, vec, vec, mat, mat] + ride.specs(),
        out_specs=[row, row, pl.BlockSpec((3, tm, D_MODEL), lambda i: (0, i, 0))] + ride.specs(),
        out_shape=[jax.ShapeDtypeStruct((rows, D_MODEL), BF16)] * 2
        + [jax.ShapeDtypeStruct((3, rows, D_MODEL), BF16)] + ride.out_shapes,
        scratch_shapes=[pltpu.VMEM((tm + 8, D_MODEL), F32), pltpu.VMEM((tm, D_MODEL), F32),
                        pltpu.VMEM((tm, D_MODEL), F32), pltpu.VMEM((1, D_MODEL), F32)] + ride.scratch(),
        compiler_params=pltpu.CompilerParams(dimension_semantics=("arbitrary",), vmem_limit_bytes=VMEM_LIMIT,
                                             has_side_effects=True),
    )(proj, proj, conv_w, conv_b, ba, bx, lam, wa_g, wx_g, *ride.arrays)


def _mix_fwd(zr, zl, proj, h0, wbr, wbl, wout, ride):
    rows = h0.shape[0]
    tm = _tile(rows, 640)
    n_t = rows // tm
    n_r = ride.n

    def body(zr_ref, zl_ref, ga_ref, gb_ref, h0_ref, wbr_ref, wbl_ref, wo_ref, *refs):
        h1_ref, yr_ref, yl_ref, mx_ref = refs[n_r:n_r + 4]
        push = ride.push(refs[:n_r], refs[n_r + 4:2 * n_r + 4], refs[2 * n_r + 4:])

        @pl.when(pl.program_id(0) == 0)
        def _():
            push.start()

        yr = _dot(zr_ref[...], wbr_ref[...])
        yl = _dot(zl_ref[...], wbl_ref[...])
        mixed = (_sigmoid(ga_ref[...].astype(F32)) * yr + _sigmoid(gb_ref[...].astype(F32)) * yl).astype(BF16)
        yr_ref[...] = yr.astype(BF16)
        yl_ref[...] = yl.astype(BF16)
        mx_ref[...] = mixed
        h1_ref[...] = h0_ref[...] + _dot(mixed, wo_ref[...])

        @pl.when(pl.program_id(0) == n_t // 2)
        def _():
            push.relay()

        @pl.when(pl.program_id(0) == n_t - 1)
        def _():
            push.wait()

    row = pl.BlockSpec((tm, D_MODEL), lambda i: (i, 0))
    wsp = pl.BlockSpec((D_MODEL, D_MODEL), lambda i: (0, 0))
    return pl.pallas_call(
        body, name="mix_fwd", grid=(n_t,),
        in_specs=[row, row, _seg_spec(tm, 6), _seg_spec(tm, 7), row, wsp, wsp, wsp] + ride.specs(),
        out_specs=[row, row, row, row] + ride.specs(),
        out_shape=[jax.ShapeDtypeStruct((rows, D_MODEL), F32)] + [jax.ShapeDtypeStruct((rows, D_MODEL), BF16)] * 3
        + ride.out_shapes,
        scratch_shapes=ride.scratch(),
        compiler_params=pltpu.CompilerParams(dimension_semantics=("arbitrary",), vmem_limit_bytes=VMEM_LIMIT,
                                             has_side_effects=True),
    )(zr, zl, proj, proj, h0, wbr, wbl, wout, *ride.arrays)


def _ffn_fwd_loss(h1, norm_w, wfi_g, wfo_g, final_w, target):
    rows = h1.shape[0]
    tm = _tile(rows, 320)
    piece = 64
    n_piece = tm // piece

    def body(h1_ref, nw_ref, wfi_ref, wfo_ref, fw_ref, *refs):
        t_refs = refs[:n_piece]
        u2_ref, g_ref, up_ref, act_ref, dh2_ref, red_ref = refs[n_piece:]
        i = pl.program_id(0)

        @pl.when(i == 0)
        def _():
            red_ref[...] = jnp.zeros_like(red_ref)

        x = h1_ref[...]
        rs = lax.rsqrt(jnp.mean(x * x, axis=-1, keepdims=True) + NORM_EPS)
        u2 = (x * rs * nw_ref[...]).astype(BF16)
        u2_ref[...] = u2
        ffn = None
        for d in range(FFN_GROUPS):
            cols = slice(FFN_GROUP * d, FFN_GROUP * (d + 1))
            g = _dot(u2, wfi_ref[d])
            up = _dot(u2, wfi_ref[d + FFN_GROUPS])
            act = (g * _sigmoid(g) * up).astype(BF16)
            g_ref[:, cols] = g.astype(BF16)
            up_ref[:, cols] = up.astype(BF16)
            act_ref[:, cols] = act
            part = _dot(act, wfo_ref[d])
            ffn = part if ffn is None else ffn + part

        h2 = x + ffn
        rs = lax.rsqrt(jnp.mean(h2 * h2, axis=-1, keepdims=True) + NORM_EPS)
        nh = h2 * rs
        fw = fw_ref[...]
        row = i * tm + lax.broadcasted_iota(jnp.int32, (tm, 1), 0)
        tgt = jnp.concatenate([t[...] for t in t_refs], axis=0)
        diff = jnp.where(row >= CHUNK, nh * fw - tgt, 0.0)
        dy = diff * (1.0 / D_MODEL)
        red_ref[0:1, :] += jnp.sum(diff * diff, axis=0, keepdims=True)
        red_ref[1:2, :] += jnp.sum(dy * nh, axis=0, keepdims=True)
        dn = dy * fw
        dh2_ref[...] = rs * (dn - nh * jnp.mean(dn * nh, axis=-1, keepdims=True))

    row = pl.BlockSpec((tm, D_MODEL), lambda i: (i, 0))
    vec = pl.BlockSpec((1, D_MODEL), lambda i: (0, 0))
    hid = pl.BlockSpec((tm, FFN_GROUPS * FFN_GROUP), lambda i: (i, 0))
    hid_shape = jax.ShapeDtypeStruct((rows, FFN_GROUPS * FFN_GROUP), BF16)
    resident = dict(pipeline_mode=pl.Buffered(1))
    head_pieces = CHUNK // piece
    t_specs = [pl.BlockSpec((piece, D_MODEL), lambda i, k=k: (jnp.maximum(i * n_piece + k - head_pieces, 0), 0))
               for k in range(n_piece)]
    return pl.pallas_call(
        body, name="ffn_fwd_loss", grid=(rows // tm,),
        in_specs=[row, vec,
                  pl.BlockSpec((2 * FFN_GROUPS, D_MODEL, FFN_GROUP), lambda i: (0, 0, 0), **resident),
                  pl.BlockSpec((FFN_GROUPS, FFN_GROUP, D_MODEL), lambda i: (0, 0, 0), **resident),
                  vec] + t_specs,
        out_specs=[row, hid, hid, hid, row, pl.BlockSpec((8, D_MODEL), lambda i: (0, 0))],
        out_shape=[jax.ShapeDtypeStruct((rows, D_MODEL), BF16), hid_shape, hid_shape, hid_shape,
                   jax.ShapeDtypeStruct((rows, D_MODEL), F32), jax.ShapeDtypeStruct((8, D_MODEL), F32)],
        compiler_params=_cparams(("arbitrary",)),
    )(h1, norm_w, wfi_g, wfo_g, final_w, *([target] * n_piece))


def _wgrad(a, b, ka, tn, out_dtype, b_halves=False):
    rows = a.shape[0]
    na = a.shape[1] // ka
    tm = _tile(rows, 1664)
    nm = rows // tm
    if b_halves:
        per_half = b.shape[2] // tn
        nb = 2 * per_half
        b_spec = pl.BlockSpec((None, tm, tn), lambda p, q, m: (q // per_half, m, q % per_half))
    else:
        nb = b.shape[1] // tn
        b_spec = pl.BlockSpec((tm, tn), lambda p, q, m: (m, q))

    def body(a_ref, b_ref, o_ref, acc):
        m = pl.program_id(2)

        @pl.when(m == 0)
        def _():
            acc[...] = jnp.zeros_like(acc)

        acc[...] += _dot_tn(a_ref[...].astype(BF16), b_ref[...].astype(BF16))

        @pl.when(m == nm - 1)
        def _():
            o_ref[...] = acc[...].astype(out_dtype)

    return pl.pallas_call(
        body, name="wgrad", grid=(na, nb, nm),
        in_specs=[pl.BlockSpec((tm, ka), lambda p, q, m: (m, p)), b_spec],
        out_specs=pl.BlockSpec((None, None, ka, tn), lambda p, q, m: (p, q, 0, 0)),
        out_shape=jax.ShapeDtypeStruct((na, nb, ka, tn), out_dtype),
        scratch_shapes=[pltpu.VMEM((ka, tn), F32)],
        compiler_params=_cparams(("parallel", "parallel", "arbitrary")),
    )(a, b)


WIN_NEAR = (2, 4, 3, 5, 1)
WIN_FAR = (6, 7)
WIN_ORDER = WIN_FAR + WIN_NEAR + (0,)


def _w_in_relation_at(jj):
    k = 0
    for pos in reversed(range(len(WIN_ORDER) - 1)):
        k = jnp.where(jj == pos, WIN_ORDER[pos], k)
    return k


def _wgrad_w_in(u, dproj, me_arr):
    rows = u.shape[0]
    tm = _tile(rows, 1664)
    nm = rows // tm
    n_near = len(WIN_NEAR)

    def body(me_ref, a_ref, b_ref, far_ref, land_ref, acc, sbuf, send_sems, recv_sems, own_sem):
        del me_ref
        jj, m = pl.program_id(0), pl.program_id(1)

        def near_copy(n):
            k = WIN_NEAR[n]
            return pltpu.make_async_remote_copy(src_ref=sbuf.at[n], dst_ref=land_ref.at[k], send_sem=send_sems.at[n],
                                                recv_sem=recv_sems.at[n], device_id=_peer(k)[0], device_id_type=MESH_ID)

        own_copy = pltpu.make_async_copy(sbuf.at[n_near], land_ref.at[0], own_sem)

        @pl.when(m == 0)
        def _():
            acc[...] = jnp.zeros_like(acc)

        acc[...] += _dot_tn(a_ref[...], b_ref[...])

        for pos, k in enumerate(WIN_ORDER):
            @pl.when(jnp.logical_and(jj == pos, m == nm - 1))
            def _(k=k):
                block = acc[...].astype(BF16)
                if k in WIN_FAR:
                    far_ref[...] = block
                elif k == 0:
                    sbuf[n_near] = block
                    own_copy.start()
                else:
                    sbuf[WIN_NEAR.index(k)] = block
                    near_copy(WIN_NEAR.index(k)).start()

        @pl.when(jnp.logical_and(jj == N_DEV - 1, m == nm - 1))
        def _():
            for n in range(n_near):
                near_copy(n).wait_recv()
            for n in range(n_near):
                near_copy(n).wait_send()
            own_copy.wait()

    grid_spec = pltpu.PrefetchScalarGridSpec(
        num_scalar_prefetch=1, grid=(N_DEV, nm),
        in_specs=[pl.BlockSpec((tm, D_MODEL), lambda jj, m, me: (m, 0)),
                  pl.BlockSpec((tm, D_MODEL), lambda jj, m, me: (m, me[0] ^ _w_in_relation_at(jj)))],
        out_specs=[pl.BlockSpec((None, D_MODEL, D_MODEL), lambda jj, m, me: (jnp.minimum(jj, len(WIN_FAR) - 1), 0, 0)),
                   ANY],
        scratch_shapes=[pltpu.VMEM((D_MODEL, D_MODEL), F32), pltpu.VMEM((n_near + 1, D_MODEL, D_MODEL), BF16),
                        pltpu.SemaphoreType.DMA((n_near,)), pltpu.SemaphoreType.DMA((n_near,)),
                        pltpu.SemaphoreType.DMA(())])
    return pl.pallas_call(
        body, name="wgrad_w_in", grid_spec=grid_spec,
        out_shape=[jax.ShapeDtypeStruct((len(WIN_FAR), D_MODEL, D_MODEL), BF16),
                   jax.ShapeDtypeStruct((n_near + 1, D_MODEL, D_MODEL), BF16)],
        compiler_params=pltpu.CompilerParams(dimension_semantics=("arbitrary", "arbitrary"),
                                             vmem_limit_bytes=VMEM_LIMIT, has_side_effects=True),
    )(me_arr, u, dproj)


def _ffn_bwd(dh2, g, up, h1, norm_w, wfi_g, wfo_g):
    rows = h1.shape[0]
    tm = _tile(rows, 320)

    def body(dh2_ref, g_ref, up_ref, h1_ref, nw_ref, wfi_ref, wfo_ref, dgu_ref, dh1_ref, dw_ref):
        @pl.when(pl.program_id(0) == 0)
        def _():
            dw_ref[...] = jnp.zeros_like(dw_ref)

        dh2 = dh2_ref[...]
        dh2_b = dh2.astype(BF16)
        du2 = None
        for d in range(FFN_GROUPS):
            cols = slice(FFN_GROUP * d, FFN_GROUP * (d + 1))
            dact = _dot_nt(dh2_b, wfo_ref[d])
            gv, uv = g_ref[:, cols].astype(F32), up_ref[:, cols].astype(F32)
            sg = _sigmoid(gv)
            dg = (dact * uv * (sg * (1.0 + gv * (1.0 - sg)))).astype(BF16)
            dup = (dact * (gv * sg)).astype(BF16)
            dgu_ref[0, :, cols] = dg
            dgu_ref[1, :, cols] = dup
            part = _dot_nt(dg, wfi_ref[d]) + _dot_nt(dup, wfi_ref[d + FFN_GROUPS])
            du2 = part if du2 is None else du2 + part
        dx, dw = _rms_bwd(h1_ref[...], nw_ref[...], du2)
        dw_ref[0:1, :] += dw
        dh1_ref[...] = dh2 + dx

    row = pl.BlockSpec((tm, D_MODEL), lambda i: (i, 0))
    vec = pl.BlockSpec((1, D_MODEL), lambda i: (0, 0))
    hid = pl.BlockSpec((tm, FFN_GROUPS * FFN_GROUP), lambda i: (i, 0))
    resident = dict(pipeline_mode=pl.Buffered(1))
    return pl.pallas_call(
        body, name="ffn_bwd", grid=(rows // tm,),
        in_specs=[row, hid, hid, row, vec,
                  pl.BlockSpec((2 * FFN_GROUPS, D_MODEL, FFN_GROUP), lambda i: (0, 0, 0), **resident),
                  pl.BlockSpec((FFN_GROUPS, FFN_GROUP, D_MODEL), lambda i: (0, 0, 0), **resident)],
        out_specs=[pl.BlockSpec((2, tm, FFN_GROUPS * FFN_GROUP), lambda i: (0, i, 0)), row,
                   pl.BlockSpec((8, D_MODEL), lambda i: (0, 0))],
        out_shape=[jax.ShapeDtypeStruct((2, rows, FFN_GROUPS * FFN_GROUP), BF16),
                   jax.ShapeDtypeStruct((rows, D_MODEL), F32), jax.ShapeDtypeStruct((8, D_MODEL), F32)],
        compiler_params=_cparams(("arbitrary",)),
    )(dh2, g, up, h1, norm_w, wfi_g, wfo_g)


def _mix_bwd(dh1, yr, yl, proj, wbr, wbl, wout):
    rows = dh1.shape[0]
    tm = _tile(rows, 640)

    def body(dh1_ref, yr_ref, yl_ref, ga_ref, gb_ref, wbr_ref, wbl_ref, wo_ref,
             dyr_ref, dyl_ref, dseg_ref, dzr_ref, dzl_ref):
        dmix = _dot_nt(dh1_ref[...].astype(BF16), wo_ref[...])
        sa, sb = _sigmoid(ga_ref[...].astype(F32)), _sigmoid(gb_ref[...].astype(F32))
        dyr = (dmix * sa).astype(BF16)
        dyl = (dmix * sb).astype(BF16)
        dyr_ref[...] = dyr
        dyl_ref[...] = dyl
        dseg_ref[:, 0:D_MODEL] = (dmix * yr_ref[...].astype(F32) * (sa * (1.0 - sa))).astype(BF16)
        dseg_ref[:, D_MODEL:2 * D_MODEL] = (dmix * yl_ref[...].astype(F32) * (sb * (1.0 - sb))).astype(BF16)
        dzr_ref[...] = _dot_nt(dyr, wbr_ref[...]).astype(BF16)
        dzl_ref[...] = _dot_nt(dyl, wbl_ref[...]).astype(BF16)

    row = pl.BlockSpec((tm, D_MODEL), lambda i: (i, 0))
    wsp = pl.BlockSpec((D_MODEL, D_MODEL), lambda i: (0, 0))
    bshape = jax.ShapeDtypeStruct((rows, D_MODEL), BF16)
    return pl.pallas_call(
        body, name="mix_bwd", grid=(rows // tm,),
        in_specs=[row, row, row, _seg_spec(tm, 6), _seg_spec(tm, 7), wsp, wsp, wsp],
        out_specs=[row, row, pl.BlockSpec((tm, 2 * D_MODEL), lambda i: (i, 3)), row, row],
        out_shape=[bshape, bshape, jax.ShapeDtypeStruct((rows, N_DEV * D_MODEL), BF16), bshape, bshape],
        compiler_params=_cparams(("parallel",)),
    )(dh1, yr, yl, proj, proj, wbr, wbl, wout)


S1_SHAPES = [
    jax.ShapeDtypeStruct((N_DEV, D_MODEL, FFN_GROUP), BF16),
    jax.ShapeDtypeStruct((N_DEV, FFN_OUT_SHARD, D_MODEL), BF16),
]


def _s1_parts(ins, p):
    return [ins[0].at[p], ins[1].at[p // 2, _half_rows(p), :]]


def _lru_bwd(dzl, hs, cri, proj, dproj, conv_w, lam, wa_g, wx_g, s1_grads):
    rows = dzl.shape[0]
    tm = _tile(rows, 320)
    nt = rows // tm
    t8 = tm // 8
    n_s1 = len(s1_grads)

    def body(dzl_ref, hs_ref, hsp_ref, cri_ref, x_ref, gt_ref, cw_ref, lam_ref, wa_ref, wx_ref, dproj_in, *refs):
        del dproj_in
        s1_refs = refs[:n_s1]
        dseg_ref, dwa_ref, dwx_ref, sm_ref = refs[n_s1:n_s1 + 4]
        land_refs = refs[n_s1 + 4:2 * n_s1 + 4]
        (xbuf, abuf, mbuf, ibuf, dbuf, dcbuf, dpr_s, dpi_s, sums, conv_sums, anext, dhcar,
         send_sems, recv_sems, loc_sems) = refs[2 * n_s1 + 4:]
        step = pl.program_id(0)
        i = nt - 1 - step
        push = _Push(lambda p: _s1_parts(s1_refs, p), lambda s: [r.at[s] for r in land_refs],
                     (send_sems, recv_sems, loc_sems), n_s1)

        @pl.when(step == 0)
        def _():
            push.start()
            dwa_ref[...] = jnp.zeros_like(dwa_ref)
            dwx_ref[...] = jnp.zeros_like(dwx_ref)
            sm_ref[...] = jnp.zeros_like(sm_ref)
            anext[...] = jnp.zeros_like(anext)
            dhcar[...] = jnp.zeros_like(dhcar)
            dcbuf[tm:tm + 8, :] = jnp.zeros((8, D_MODEL), F32)

        slab, lanes = 16, 256
        lam_v = lam_ref[...]
        xbuf[0:8, :] = jnp.where(i == 0, 0.0, hsp_ref[8:16, :].astype(F32))
        sums[...] = jnp.zeros_like(sums)

        def before_scan(k, carry):
            rw = pl.ds(pl.multiple_of(k * slab, slab), slab)
            for q in range(D_MODEL // lanes):
                ln = slice(lanes * q, lanes * (q + 1))
                a, mult, inv_mult, _ = _lru_decay(cri_ref[1, rw, ln].astype(F32), lam_v[:, ln])
                abuf[rw, ln] = a
                mbuf[rw, ln] = mult
                ibuf[rw, ln] = inv_mult
                gl, dgl = _gelu_parts(gt_ref[rw, ln].astype(F32))
                dzl_v = dzl_ref[rw, ln].astype(F32)
                hs_v = hs_ref[rw, ln].astype(F32)
                dseg_ref[rw, D_MODEL + lanes * q:D_MODEL + lanes * (q + 1)] = (dzl_v * hs_v * dgl).astype(BF16)
                dbuf[rw, ln] = dzl_v * gl
                xbuf[pl.ds(pl.multiple_of(k * slab + 8, 8), slab), ln] = hs_v
            return carry

        lax.fori_loop(0, tm // slab, before_scan, 0)

        sub = lax.broadcasted_iota(jnp.int32, (8, D_MODEL), 0)

        def block(k, carry):
            dh_next, a_next = carry
            off = pl.multiple_of((t8 - 1 - k) * 8, 8)
            a_blk = abuf[pl.ds(off, 8), :]
            av = jnp.where(sub < 7, pltpu.roll(a_blk, 7, 0), a_next)
            uv = dbuf[pl.ds(off, 8), :]
            for s in (1, 2, 4):
                us = jnp.where(sub < 8 - s, pltpu.roll(uv, 8 - s, 0), 0.0)
                as_ = jnp.where(sub < 8 - s, pltpu.roll(av, 8 - s, 0), 1.0)
                uv = uv + av * us
                av = av * as_
            hv = uv + av * dh_next
            dbuf[pl.ds(off, 8), :] = hv
            return hv[0:1, :], a_blk[0:1, :]

        dh_first, a_first = lax.fori_loop(0, t8, block, (dhcar[...], anext[...]))
        dhcar[...] = dh_first
        anext[...] = a_first

        sp = jnp.maximum(-lam_v, 0.0) + jnp.log(1.0 + jnp.exp(-jnp.abs(lam_v)))
        sub_q = lax.broadcasted_iota(jnp.int32, (8, lanes), 0)
        row16 = lax.broadcasted_iota(jnp.int32, (slab, 1), 0)

        def after_scan(k, carry):
            off = pl.multiple_of(k * slab, slab)
            rw = pl.ds(off, slab)
            for q in range(D_MODEL // lanes):
                ln = slice(lanes * q, lanes * (q + 1))
                before = xbuf[pl.ds(off, 8), ln]
                h_lo = xbuf[pl.ds(pl.multiple_of(off + 8, 8), 8), ln]
                h_hi = xbuf[pl.ds(pl.multiple_of(off + 16, 8), 8), ln]
                hprev = jnp.concatenate([jnp.where(sub_q >= 1, pltpu.roll(h_lo, 1, 0), before[7:8, :]),
                                         jnp.where(sub_q >= 1, pltpu.roll(h_hi, 1, 0), h_lo[7:8, :])], axis=0)
                c, r, ig = (cri_ref[n, rw, ln].astype(F32) for n in range(3))
                a, mult, inv_mult = abuf[rw, ln], mbuf[rw, ln], ibuf[rw, ln]
                dh = dbuf[rw, ln]
                duu = jnp.where(i * tm + off + row16 >= PAD_ROWS, dh, 0.0)
                t_mult = duu * mult
                dlog_a = dh * hprev * a - duu * ig * c * (a * a) * inv_mult
                dpr = dlog_a * (-LRU_C * sp[:, ln]) * r * (1.0 - r)
                dpi = t_mult * c * ig * (1.0 - ig)
                dpr_s[rw, ln] = dpr.astype(BF16)
                dpi_s[rw, ln] = dpi.astype(BF16)
                dcbuf[rw, ln] = t_mult * ig
                sums[0, :, ln] += dlog_a * r
                sums[1, :, ln] += dpr
                sums[2, :, ln] += dpi
            return carry

        lax.fori_loop(0, tm // slab, after_scan, 0)

        dcs = []
        for g in range(LRU_BLOCKS):
            sl = slice(LRU_BLOCK * g, LRU_BLOCK * (g + 1))
            cg = cri_ref[0, :, sl]
            dpr_b, dpi_b = dpr_s[:, sl], dpi_s[:, sl]
            dwa_ref[g] += _dot_tn(cg, dpr_b)
            dwx_ref[g] += _dot_tn(cg, dpi_b)
            dcs.append(_dot_nt(dpr_b, wa_ref[g]) + _dot_nt(dpi_b, wx_ref[g]))
        dc = dcbuf[0:tm, :] + jnp.concatenate(dcs, axis=1)

        dcbuf[0:tm, :] = dc
        conv_sums[...] = jnp.zeros_like(conv_sums)

        def conv_back(k, carry):
            off = pl.multiple_of(k * slab, slab)
            rw = pl.ds(off, slab)
            for q in range(D_MODEL // lanes):
                ln = slice(lanes * q, lanes * (q + 1))
                blocks = [dcbuf[pl.ds(pl.multiple_of(off + 8 * b, 8), 8), ln] for b in range(3)]
                x_v = x_ref[rw, ln].astype(F32)
                now = jnp.concatenate(blocks[:2], axis=0)
                dlin = cw_ref[3:4, ln] * now
                conv_sums[3, :, ln] += now * x_v
                conv_sums[4, :, ln] += now
                for back in (1, 2, 3):
                    turned = [pltpu.roll(b, 8 - back, 0) for b in blocks]
                    later = jnp.concatenate([jnp.where(sub_q < 8 - back, turned[0], turned[1]),
                                             jnp.where(sub_q < 8 - back, turned[1], turned[2])], axis=0)
                    dlin = dlin + cw_ref[3 - back:4 - back, ln] * later
                    conv_sums[3 - back, :, ln] += later * x_v
                dseg_ref[rw, ln] = dlin.astype(BF16)
            return carry

        lax.fori_loop(0, tm // slab, conv_back, 0)
        dcbuf[tm:tm + 8, :] = dcbuf[0:8, :]
        for n in range(5):
            sm_ref[n:n + 1, :] += jnp.sum(conv_sums[n], axis=0, keepdims=True)
        sm_ref[5:6, :] += jnp.sum(sums[1], axis=0, keepdims=True)
        sm_ref[6:7, :] += jnp.sum(sums[2], axis=0, keepdims=True)
        sm_ref[7:8, :] += jnp.sum(sums[0], axis=0, keepdims=True) * (LRU_C * _sigmoid(-lam_v))

        @pl.when(step == nt - 1)
        def _():
            push.wait()

    rowb = pl.BlockSpec((tm, D_MODEL), lambda s: (nt - 1 - s, 0))
    t16 = tm // 16
    prev8 = pl.BlockSpec((16, D_MODEL), lambda s: (jnp.maximum((nt - 1 - s) * t16 - 1, 0), 0))
    seg = lambda k: pl.BlockSpec((tm, D_MODEL), lambda s, k=k: (nt - 1 - s, k))
    vec = pl.BlockSpec((1, D_MODEL), lambda s: (0, 0))
    mat = pl.BlockSpec((LRU_BLOCKS, LRU_BLOCK, LRU_BLOCK), lambda s: (0, 0, 0))
    mshape = jax.ShapeDtypeStruct((LRU_BLOCKS, LRU_BLOCK, LRU_BLOCK), F32)
    n_in = 10
    return pl.pallas_call(
        body, name="lru_bwd", grid=(nt,),
        in_specs=[rowb, rowb, prev8, pl.BlockSpec((3, tm, D_MODEL), lambda s: (0, nt - 1 - s, 0)), seg(4), seg(5),
                  pl.BlockSpec((4, D_MODEL), lambda s: (0, 0)), vec, mat, mat, ANY] + [ANY] * n_s1,
        out_specs=[pl.BlockSpec((tm, 2 * D_MODEL), lambda s: (nt - 1 - s, 2)), mat, mat,
                   pl.BlockSpec((8, D_MODEL), lambda s: (0, 0))] + [ANY] * n_s1,
        out_shape=[jax.ShapeDtypeStruct(dproj.shape, dproj.dtype), mshape, mshape,
                   jax.ShapeDtypeStruct((8, D_MODEL), F32)] + S1_SHAPES,
        input_output_aliases={n_in: 0},
        scratch_shapes=[pltpu.VMEM((tm + 8, D_MODEL), F32), pltpu.VMEM((tm, D_MODEL), F32),
                        pltpu.VMEM((tm, D_MODEL), F32), pltpu.VMEM((tm, D_MODEL), F32),
                        pltpu.VMEM((tm, D_MODEL), F32), pltpu.VMEM((tm + 8, D_MODEL), F32),
                        pltpu.VMEM((tm, D_MODEL), BF16), pltpu.VMEM((tm, D_MODEL), BF16),
                        pltpu.VMEM((3, 16, D_MODEL), F32), pltpu.VMEM((5, 16, D_MODEL), F32),
                        pltpu.VMEM((1, D_MODEL), F32), pltpu.VMEM((1, D_MODEL), F32)] + _push_sems(n_s1),
        compiler_params=pltpu.CompilerParams(dimension_semantics=("arbitrary",), vmem_limit_bytes=VMEM_LIMIT,
                                             has_side_effects=True),
    )(dzl, hs, hs, cri, proj, proj, conv_w, lam, wa_g, wx_g, dproj, *s1_grads)


def _retention_bwd(dzr, o, proj, states, cos2, sin2, dec, dproj):
    rows = dzr.shape[0]
    n_chunks = rows // CHUNK
    per_step = _chunks_per_step(n_chunks)
    n_steps = n_chunks // per_step
    tm = per_step * CHUNK

    def body(dzr_ref, o_ref, q_ref, k_ref, v_ref, g_ref, st_ref, c_ref, s_ref, dec_ref, dproj_in, dseg_ref, dstate):
        del dproj_in

        @pl.when(pl.program_id(0) == 0)
        def _():
            dstate[...] = jnp.zeros_like(dstate)

        for h in range(HEADS):
            sl = slice(HEAD_DIM * h, HEAD_DIM * (h + 1))
            intra, qd, kd, cd = dec_ref[0, h], dec_ref[1, h], dec_ref[2, h], dec_ref[3, h]
            dst = dstate[h]
            for c in reversed(range(per_step)):
                rw = slice(CHUNK * c, CHUNK * (c + 1))
                cos_t, sin_t = c_ref[rw, :], s_ref[rw, :]
                o = o_ref[rw, sl].astype(F32)
                g = g_ref[rw, sl].astype(F32)
                dzr_v = dzr_ref[rw, sl].astype(F32)
                sg = _sigmoid(g)
                r = lax.rsqrt(jnp.mean(o * o, axis=-1, keepdims=True) + NORM_EPS)
                on = o * r
                dseg_ref[rw, 3 * D_MODEL + HEAD_DIM * h:3 * D_MODEL + HEAD_DIM * (h + 1)] = (
                    dzr_v * on * (sg * (1.0 + g * (1.0 - sg)))).astype(BF16)
                don = dzr_v * (g * sg)
                do = r * (don - on * jnp.mean(don * on, axis=-1, keepdims=True))
                dob = do.astype(BF16)

                qh = _rot(q_ref[rw, sl].astype(F32), cos_t, sin_t)
                kh = _rot(k_ref[rw, sl].astype(F32), cos_t, sin_t) * QK_SCALE
                qb, kb, vb = qh.astype(BF16), kh.astype(BF16), v_ref[rw, sl]
                s = (_dot_nt(qb, kb) * intra).astype(BF16)
                ds = (_dot_nt(dob, vb) * intra).astype(BF16)
                st_b = st_ref[c, h].astype(BF16)
                dst_b = dst.astype(BF16)
                dv = _dot_tn(s, dob) + _dot((kh * kd).astype(BF16), dst_b)
                dq = _dot(ds, kb) + _dot_nt(dob, st_b) * qd
                dk = _dot_tn(ds, qb) + _dot_nt(vb, dst_b) * kd
                dst = dst * cd + _dot_tn((qh * qd).astype(BF16), dob)
                dseg_ref[rw, 2 * D_MODEL + HEAD_DIM * h:2 * D_MODEL + HEAD_DIM * (h + 1)] = dv.astype(BF16)
                dseg_ref[rw, sl] = _rot_t(dq, cos_t, sin_t).astype(BF16)
                dseg_ref[rw, D_MODEL + HEAD_DIM * h:D_MODEL + HEAD_DIM * (h + 1)] = (
                    _rot_t(dk, cos_t, sin_t) * QK_SCALE).astype(BF16)
            dstate[h] = dst

    rev = lambda s: n_steps - 1 - s
    rowb = pl.BlockSpec((tm, D_MODEL), lambda s: (rev(s), 0))
    seg = lambda k: pl.BlockSpec((tm, D_MODEL), lambda s, k=k: (rev(s), k))
    tab = pl.BlockSpec((tm, HEAD_DIM), lambda s: (rev(s), 0))
    return pl.pallas_call(
        body, name="retention_bwd", grid=(n_steps,),
        in_specs=[rowb, rowb, seg(0), seg(1), seg(2), seg(3),
                  pl.BlockSpec((per_step, HEADS, HEAD_DIM, HEAD_DIM), lambda s: (rev(s), 0, 0, 0)), tab, tab,
                  pl.BlockSpec((4, HEADS, CHUNK, CHUNK), lambda s: (0, 0, 0, 0)), ANY],
        out_specs=pl.BlockSpec((tm, 4 * D_MODEL), lambda s: (rev(s), 0)),
        out_shape=jax.ShapeDtypeStruct(dproj.shape, dproj.dtype),
        input_output_aliases={10: 0},
        scratch_shapes=[pltpu.VMEM((HEADS, HEAD_DIM, HEAD_DIM), F32)],
        compiler_params=_cparams(("arbitrary",)),
    )(dzr, o, proj, proj, proj, proj, states, cos2, sin2, dec, dproj)


S2_SHAPES = [
    jax.ShapeDtypeStruct((N_DEV, D_MODEL // N_DEV, D_MODEL), BF16),
    jax.ShapeDtypeStruct((N_DEV, D_MODEL // N_DEV, D_MODEL), BF16),
    jax.ShapeDtypeStruct((N_DEV, D_MODEL // N_DEV, D_MODEL), BF16),
    jax.ShapeDtypeStruct((N_DEV, LRU_BLOCKS, LRU_ROWS, LRU_BLOCK), F32),
    jax.ShapeDtypeStruct((N_DEV, LRU_BLOCKS, LRU_ROWS, LRU_BLOCK), F32),
]


def _s2_parts(ins, p):
    return [r.at[p] for r in ins]


def _in_proj_bwd(dproj, win_g, h0, norm_w, dh1, d_win_far, s2_grads):
    rows = h0.shape[0]
    tm = _tile(rows, 320)
    n_i = rows // tm
    n_s2 = len(s2_grads)
    n_far = len(WIN_FAR)

    def body(dseg_ref, w_ref, h0_ref, nw_ref, dh1_ref, far_ref, *refs):
        s2_refs = refs[:n_s2]
        dh0_ref, dw_ref, far_land = refs[n_s2:n_s2 + 3]
        land_refs = refs[n_s2 + 3:2 * n_s2 + 3]
        send_sems, recv_sems, loc_sems, far_send_sems, far_recv_sems = refs[2 * n_s2 + 3:]
        i = pl.program_id(0)
        push = _Push(lambda p: _s2_parts(s2_refs, p), lambda s: [r.at[s] for r in land_refs],
                     (send_sems, recv_sems, loc_sems), n_s2)

        def far_copy(n):
            return pltpu.make_async_remote_copy(src_ref=far_ref.at[n], dst_ref=far_land.at[n],
                                                send_sem=far_send_sems.at[n], recv_sem=far_recv_sems.at[n],
                                                device_id=_peer(WIN_FAR[n])[0], device_id_type=MESH_ID)

        @pl.when(i == 0)
        def _():
            for n in range(n_far):
                far_copy(n).start()
            push.start()
            dw_ref[...] = jnp.zeros_like(dw_ref)

        du = _dot_nt(dseg_ref[:, 0:D_MODEL], w_ref[0])
        for j in range(1, N_DEV):
            du = du + _dot_nt(dseg_ref[:, D_MODEL * j:D_MODEL * (j + 1)], w_ref[j])
        dx, dw = _rms_bwd(h0_ref[...], nw_ref[...], du)
        dw_ref[0:1, :] += dw
        dh0_ref[...] = dh1_ref[...] + dx

        @pl.when(i == n_i - 1)
        def _():
            for n in range(n_far):
                far_copy(n).wait_recv()
            for n in range(n_far):
                far_copy(n).wait_send()
            push.wait()

    row = pl.BlockSpec((tm, D_MODEL), lambda i: (i, 0))
    vec = pl.BlockSpec((1, D_MODEL), lambda i: (0, 0))
    return pl.pallas_call(
        body, name="in_proj_bwd", grid=(n_i,),
        in_specs=[pl.BlockSpec((tm, N_DEV * D_MODEL), lambda i: (i, 0)),
                  pl.BlockSpec((N_DEV, D_MODEL, D_MODEL), lambda i: (0, 0, 0), pipeline_mode=pl.Buffered(1)),
                  row, vec, row, ANY] + [ANY] * n_s2,
        out_specs=[row, pl.BlockSpec((8, D_MODEL), lambda i: (0, 0)), ANY] + [ANY] * n_s2,
        out_shape=[jax.ShapeDtypeStruct((rows, D_MODEL), F32), jax.ShapeDtypeStruct((8, D_MODEL), F32),
                   jax.ShapeDtypeStruct((n_far, D_MODEL, D_MODEL), BF16)] + S2_SHAPES,
        scratch_shapes=_push_sems(n_s2) + [pltpu.SemaphoreType.DMA((n_far,)), pltpu.SemaphoreType.DMA((n_far,))],
        compiler_params=pltpu.CompilerParams(dimension_semantics=("arbitrary",),
                                             vmem_limit_bytes=VMEM_LIMIT, has_side_effects=True),
    )(dproj, win_g, h0, norm_w, dh1, d_win_far, *s2_grads)


def _adamw(g_slots, w, m, v, more_slots=None):
    slots, rows, cols = g_slots.shape
    extra = [] if more_slots is None else [more_slots]
    tr = rows
    for cand in (256, 128, 64, 32, 16, 8):
        if rows % cand == 0 and rows > cand:
            tr = cand
            break

    def body(g_ref, *refs):
        w_ref, m_ref, v_ref, go_ref, d_ref, mo_ref, vo_ref = refs[len(extra):]
        g = g_ref[0].astype(F32)
        for s in range(1, slots):
            g = g + g_ref[s].astype(F32)
        for more_ref in refs[:len(extra)]:
            for s in range(more_ref.shape[0]):
                g = g + more_ref[s].astype(F32)
        m2 = ADAM_B1 * m_ref[...] + (1.0 - ADAM_B1) * g
        v2 = ADAM_B2 * v_ref[...] + (1.0 - ADAM_B2) * (g * g)
        m_hat = m2 / (1.0 - ADAM_B1 ** ADAM_STEP)
        v_hat = v2 / (1.0 - ADAM_B2 ** ADAM_STEP)
        go_ref[...] = g
        d_ref[...] = -ADAM_LR * (m_hat / (jnp.sqrt(v_hat) + ADAM_EPS) + ADAM_WD * w_ref[...])
        mo_ref[...] = m2
        vo_ref[...] = v2

    blk = pl.BlockSpec((tr, cols), lambda i: (i, 0))
    shape = jax.ShapeDtypeStruct((rows, cols), F32)
    return pl.pallas_call(
        body, name="adamw", grid=(rows // tr,),
        in_specs=[pl.BlockSpec((slots, tr, cols), lambda i: (0, i, 0))]
        + [pl.BlockSpec((t.shape[0], tr, cols), lambda i: (0, i, 0)) for t in extra] + [blk, blk, blk],
        out_specs=[blk] * 4, out_shape=[shape] * 4,
        compiler_params=_cparams(("parallel",)),
    )(g_slots, *extra, w, m, v)


def _sum_slots(packs):
    slots, rows, cols = packs.shape

    def body(p_ref, o_ref):
        acc = p_ref[0]
        for s in range(1, slots):
            acc = acc + p_ref[s]
        o_ref[...] = acc

    return pl.pallas_call(
        body, name="sum_slots", out_shape=jax.ShapeDtypeStruct((rows, cols), F32),
        compiler_params=pltpu.CompilerParams(vmem_limit_bytes=VMEM_LIMIT),
    )(packs)


def _gather_small(small):
    shapes = [jax.ShapeDtypeStruct((N_DEV,) + small.shape, F32)]
    return _push_call("gather_small", [small], shapes,
                      lambda ins, p: list(ins), lambda outs, s: [r.at[s] for r in outs])[0]


def _share_pack(pack):
    shapes = [jax.ShapeDtypeStruct((N_DEV,) + pack.shape, F32)]
    return _push_call("share_pack", [pack], shapes,
                      lambda ins, p: list(ins), lambda outs, s: [r.at[s] for r in outs])[0]


PACK_MIX_NORM, PACK_CONV_W, PACK_CONV_B, PACK_BA, PACK_BX, PACK_LAM = 0, 8, 12, 13, 14, 15
PACK_FFN_NORM, PACK_SQ_ERR, PACK_FINAL_NORM, PACK_META = 16, 24, 25, 32


def kernel(x, meta_tokens, mix_norm_w, w_in, conv_w, conv_b, lru_wa, lru_ba, lru_wx, lru_bx, lru_lambda, w_branch_ret, w_branch_lru, w_out, ffn_norm_w, w_ffn_in, w_ffn_out, final_norm_w, loss_target, m_meta_tokens, m_mix_norm_w, m_w_in, m_conv_w, m_conv_b, m_lru_wa, m_lru_ba, m_lru_wx, m_lru_bx, m_lru_lambda, m_w_branch_ret, m_w_branch_lru, m_w_out, m_ffn_norm_w, m_w_ffn_in, m_w_ffn_out, m_final_norm_w, v_meta_tokens, v_mix_norm_w, v_w_in, v_conv_w, v_conv_b, v_lru_wa, v_lru_ba, v_lru_wx, v_lru_bx, v_lru_lambda, v_w_branch_ret, v_w_branch_lru, v_w_out, v_ffn_norm_w, v_w_ffn_in, v_w_ffn_out, v_final_norm_w):
    me = _my_index()
    pad4 = ((0, 4), (0, 0))
    fw = final_norm_w.reshape(1, D_MODEL)

    small = jnp.concatenate([meta_tokens, jnp.pad(conv_w[0], pad4)], axis=0)
    small_g = _gather_small(small)
    meta_full = small_g[:, :N_META].transpose(1, 0, 2).reshape(N_META, D_MODEL)
    conv_w_full = small_g[:, N_META:N_META + 4].transpose(1, 0, 2).reshape(4, D_MODEL)
    mixer_shards = [w_branch_ret[0].astype(BF16), w_branch_lru[0].astype(BF16), w_out[0].astype(BF16),
                    lru_wa[0].astype(BF16), lru_wx[0].astype(BF16)]
    wfi_shard = jnp.pad(w_ffn_in[0].astype(BF16), ((0, 0), (0, FFN_GROUP - FFN_SHARD)))
    own_slot = lambda ins, p: list(ins)

    rows = x.shape[1] + CHUNK
    h0 = jnp.concatenate([jnp.zeros((PAD_ROWS, D_MODEL), F32), meta_full, x[0]], axis=0)
    cos2, sin2 = _rope_tables(rows)
    dec = _retention_consts()

    me_arr = me.astype(jnp.int32).reshape(1)
    proj, u, win_g = _in_proj(h0, mix_norm_w, w_in[0].astype(BF16), me_arr)
    o, zr, states, wbr_g, wbl_g, wout_g, wa_g, wx_g = _retention_fwd(
        proj, cos2, sin2, dec, _mixer_weights_ride(mixer_shards))
    wbr, wbl, wout = (t.reshape(D_MODEL, D_MODEL) for t in (wbr_g, wbl_g, wout_g))
    wa_g, wx_g = _from_owners(wa_g), _from_owners(wx_g)
    gather_wfi = _Ride([wfi_shard], [jax.ShapeDtypeStruct((N_DEV, D_MODEL, FFN_GROUP), BF16)],
                       own_slot, _slot_of_sender, gather_by_chip=True)
    hs, zl, cri, wfi_g = _lru_fwd(proj, conv_w_full, conv_b, lru_ba, lru_bx, lru_lambda, wa_g, wx_g, gather_wfi)
    h1, yr, yl, mixed, wfo_g = _mix_fwd(zr, zl, proj, h0, wbr, wbl, wout, _wfo_ride(w_ffn_out[0].astype(BF16)))
    u2, g, up, act, dh2, red = _ffn_fwd_loss(h1, ffn_norm_w, wfi_g, wfo_g, fw, loss_target[0])

    d_wfo = _wgrad(act, dh2, FFN_GROUP, D_MODEL, BF16)[:, 0]
    dgu, dh1, dw_ffn_norm = _ffn_bwd(dh2, g, up, h1, ffn_norm_w, wfi_g, wfo_g)
    d_wfi = _wgrad(u2, dgu, D_MODEL, FFN_GROUP, BF16, b_halves=True)[0]
    d_wout = _wgrad(mixed, dh1, D_MODEL, D_MODEL, BF16)[0, 0]
    dyr, dyl, dproj, dzr, dzl = _mix_bwd(dh1, yr, yl, proj, wbr, wbl, wout)
    d_wbr = _wgrad(zr, dyr, D_MODEL, D_MODEL, BF16)[0, 0]
    d_wbl = _wgrad(zl, dyl, D_MODEL, D_MODEL, BF16)[0, 0]
    dproj, d_wa, d_wx, lru_small, r_fi, r_fo = _lru_bwd(
        dzl, hs, cri, proj, dproj, conv_w_full, lru_lambda, wa_g, wx_g, [d_wfi, d_wfo])
    dproj = _retention_bwd(dzr, o, proj, states, cos2, sin2, dec, dproj)
    d_win_far, r_in = _wgrad_w_in(u, dproj, me_arr)
    mix_grads = [t.reshape(N_DEV, D_MODEL // N_DEV, D_MODEL) for t in (d_wbr, d_wbl, d_wout)]
    dh0, dw_mix_norm, r_in_far, r_br, r_bl, r_out, r_wa, r_wx = _in_proj_bwd(
        dproj, win_g, h0, mix_norm_w, dh1, d_win_far, mix_grads + [_by_owner(d_wa), _by_owner(d_wx)])
    grad_x = dh0[CHUNK:]

    pack = jnp.concatenate([dw_mix_norm, lru_small, dw_ffn_norm, red, dh0[PAD_ROWS:CHUNK]], axis=0)
    small_sum = _sum_slots(_share_pack(pack))
    loss = (0.5 / D_MODEL) * jnp.sum(small_sum[PACK_SQ_ERR])

    def big_update(slots, w, m, v, more_slots=None):
        shape = w.shape
        w2, m2, v2 = (t.reshape(slots.shape[1:]) for t in (w, m, v))
        return [t.reshape(shape) for t in _adamw(slots, w2, m2, v2, more_slots)]

    res = {}
    res["w_in"] = big_update(r_in, w_in, m_w_in, v_w_in, r_in_far)
    res["w_branch_ret"] = big_update(r_br, w_branch_ret, m_w_branch_ret, v_w_branch_ret)
    res["w_branch_lru"] = big_update(r_bl, w_branch_lru, m_w_branch_lru, v_w_branch_lru)
    res["w_out"] = big_update(r_out, w_out, m_w_out, v_w_out)
    res["w_ffn_in"] = big_update(r_fi[:, :, :FFN_SHARD], w_ffn_in, m_w_ffn_in, v_w_ffn_in)
    res["w_ffn_out"] = big_update(r_fo, w_ffn_out, m_w_ffn_out, v_w_ffn_out)
    res["lru_wa"] = big_update(r_wa.reshape(N_DEV, LRU_BLOCKS * LRU_ROWS, LRU_BLOCK), lru_wa, m_lru_wa, v_lru_wa)
    res["lru_wx"] = big_update(r_wx.reshape(N_DEV, LRU_BLOCKS * LRU_ROWS, LRU_BLOCK), lru_wx, m_lru_wx, v_lru_wx)

    col = me * HEAD_DIM
    g_meta = lax.dynamic_slice(small_sum, (PACK_META, col), (N_META, HEAD_DIM))
    g_conv = lax.dynamic_slice(small_sum, (PACK_CONV_W, col), (8, HEAD_DIM))
    small_names = ["mix_norm_w", "conv_b", "lru_ba", "lru_bx", "lru_lambda", "ffn_norm_w", "final_norm_w"]
    small_rows = [PACK_MIX_NORM, PACK_CONV_B, PACK_BA, PACK_BX, PACK_LAM, PACK_FFN_NORM, PACK_FINAL_NORM]
    small_w = [mix_norm_w, conv_b, lru_ba, lru_bx, lru_lambda, ffn_norm_w, fw]
    small_m = [m_mix_norm_w, m_conv_b, m_lru_ba, m_lru_bx, m_lru_lambda, m_ffn_norm_w, m_final_norm_w.reshape(1, -1)]
    small_v = [v_mix_norm_w, v_conv_b, v_lru_ba, v_lru_bx, v_lru_lambda, v_ffn_norm_w, v_final_norm_w.reshape(1, -1)]

    def pack_small(vec_list, meta_t, conv_t):
        return jnp.concatenate([t.reshape(8, HEAD_DIM) for t in vec_list] + [meta_t, jnp.pad(conv_t[0], pad4)], axis=0)

    g_small = jnp.concatenate([small_sum[r].reshape(8, HEAD_DIM) for r in small_rows] + [g_meta, g_conv], axis=0)
    outs_small = _adamw(g_small[None], pack_small(small_w, meta_tokens, conv_w),
                        pack_small(small_m, m_meta_tokens, m_conv_w), pack_small(small_v, v_meta_tokens, v_conv_w))
    for idx, name in enumerate(small_names):
        shape = final_norm_w.shape if name == "final_norm_w" else (1, D_MODEL)
        res[name] = [t[8 * idx:8 * idx + 8].reshape(shape) for t in outs_small]
    res["meta_tokens"] = [t[56:72] for t in outs_small]
    res["conv_w"] = [t[72:76].reshape(1, 4, HEAD_DIM) for t in outs_small]

    order = ["meta_tokens", "mix_norm_w", "w_in", "conv_w", "conv_b", "lru_wa", "lru_ba", "lru_wx", "lru_bx",
             "lru_lambda", "w_branch_ret", "w_branch_lru", "w_out", "ffn_norm_w", "w_ffn_in", "w_ffn_out",
             "final_norm_w"]
    out = [loss, grad_x[None]]
    for kind in range(4):
        out += [res[name][kind] for name in order]
    return tuple(out)
```

```python
import jax
import jax.numpy as jnp
from jax import lax
from jax.experimental import pallas as pl
from jax.experimental.pallas import tpu as pltpu

F32 = jnp.float32
BF16 = jnp.bfloat16

D_MODEL = 1024
N_META = 16
CHUNK = 128
PAD_ROWS = CHUNK - N_META
HEADS = 8
HEAD_DIM = 128
ROPE_BASE = 10000.0
QK_SCALE = HEAD_DIM ** -0.5
LRU_BLOCKS = 4
LRU_BLOCK = 256
LRU_C = 8.0
FFN_HIDDEN = 2816
N_DEV = 8
FFN_SHARD = 2 * FFN_HIDDEN // N_DEV
FFN_GROUP = 768
FFN_GROUPS = 4
FFN_OUT_SHARD = FFN_HIDDEN // N_DEV
NORM_EPS = 1e-6

ADAM_LR = 0.001
ADAM_B1 = 0.9
ADAM_B2 = 0.999
ADAM_EPS = 1e-08
ADAM_WD = 0.01
ADAM_STEP = 10

VMEM_LIMIT = 56 * 1024 * 1024
MESH_ID = pl.DeviceIdType.MESH
ANY = pl.BlockSpec(memory_space=pl.ANY)


def _cparams(sem):
    return pltpu.CompilerParams(dimension_semantics=sem, vmem_limit_bytes=VMEM_LIMIT)


def _tile(rows, cap):
    t = cap - cap % 64
    while rows % t:
        t -= 64
    return t


def _dot(a, b):
    return jnp.dot(a, b, preferred_element_type=F32)


def _dot_nt(a, b):
    return lax.dot_general(a, b, (((1,), (1,)), ((), ())), preferred_element_type=F32)


def _dot_tn(a, b):
    return lax.dot_general(a, b, (((0,), (0,)), ((), ())), preferred_element_type=F32)


def _sigmoid(x):
    return 0.5 * jnp.tanh(0.5 * x) + 0.5


def _gelu_parts(x):
    k = 0.7978845608028654
    inner = k * (x + 0.044715 * x * x * x)
    t = jnp.tanh(inner)
    g = 0.5 * x * (1.0 + t)
    dg = 0.5 * (1.0 + t) + 0.5 * x * (1.0 - t * t) * k * (1.0 + 3.0 * 0.044715 * x * x)
    return g, dg


def _rot(x, cos2, sin2):
    return x * cos2 + pltpu.roll(x, HEAD_DIM // 2, 1) * sin2


def _rot_t(dx, cos2, sin2):
    return dx * cos2 - pltpu.roll(dx, HEAD_DIM // 2, 1) * sin2


def _rms_bwd(x, w, dy):
    rs = lax.rsqrt(jnp.mean(x * x, axis=-1, keepdims=True) + NORM_EPS)
    nh = x * rs
    dw = jnp.sum(dy * nh, axis=0, keepdims=True)
    dn = dy * w
    dx = rs * (dn - nh * jnp.mean(dn * nh, axis=-1, keepdims=True))
    return dx, dw


def _retention_consts():
    h = jnp.arange(HEADS, dtype=F32)
    log_g = jnp.log(1.0 - 2.0 ** (-5.0 - h))
    idx = jnp.arange(CHUNK, dtype=F32)
    diff = idx[:, None] - idx[None, :]
    intra = jnp.where(diff[None] >= 0, jnp.exp(jnp.maximum(diff, 0.0)[None] * log_g[:, None, None]), 0.0)
    q_decay = jnp.exp((idx + 1.0)[:, None] * log_g[None, :])
    k_decay = jnp.exp((CHUNK - 1.0 - idx)[:, None] * log_g[None, :])
    chunk_decay = jnp.exp(CHUNK * log_g)
    shape = (HEADS, CHUNK, CHUNK)
    qd = jnp.broadcast_to(q_decay.T[:, :, None], shape)
    kd = jnp.broadcast_to(k_decay.T[:, :, None], shape)
    cd = jnp.broadcast_to(chunk_decay[:, None, None], shape)
    return jnp.stack([intra, qd, kd, cd])


def _rope_tables(rows):
    pos = jnp.maximum(jnp.arange(rows) - PAD_ROWS, 0).astype(F32)
    inv_freq = ROPE_BASE ** (-jnp.arange(0, HEAD_DIM, 2, dtype=F32) / HEAD_DIM)
    ang = pos[:, None] * inv_freq[None, :]
    cos, sin = jnp.cos(ang), jnp.sin(ang)
    return jnp.concatenate([cos, cos], axis=1), jnp.concatenate([-sin, sin], axis=1)


def _my_index():
    return 4 * lax.axis_index("x") + 2 * lax.axis_index("y") + lax.axis_index("c")


def _peer(k):
    x, y, c = lax.axis_index("x"), lax.axis_index("y"), lax.axis_index("c")
    px = 1 - x if k & 4 else x
    py = 1 - y if k & 2 else y
    pc = 1 - c if k & 1 else c
    return (px, py, pc), 4 * px + 2 * py + pc


def _push_sems(n_arr):
    n_rem = (N_DEV - 1) * n_arr
    return [pltpu.SemaphoreType.DMA((n_rem,)), pltpu.SemaphoreType.DMA((n_rem,)), pltpu.SemaphoreType.DMA((n_arr,))]


class _Push:
    def __init__(self, send_part, land_slot, sems, n_arr):
        self.send_part, self.land_slot, self.n_arr = send_part, land_slot, n_arr
        self.send_sems, self.recv_sems, self.loc_sems = sems

    def _remote(self, k, a, src, dst, pos):
        idx = (k - 1) * self.n_arr + a
        return pltpu.make_async_remote_copy(src_ref=src, dst_ref=dst, send_sem=self.send_sems.at[idx],
                                            recv_sem=self.recv_sems.at[idx], device_id=pos, device_id_type=MESH_ID)

    def _outgoing(self):
        me = _my_index()
        land = self.land_slot(me)
        remote = []
        for k in range(1, N_DEV):
            pos, p = _peer(k)
            src = self.send_part(p)
            remote += [self._remote(k, a, src[a], land[a], pos) for a in range(self.n_arr)]
        own = self.send_part(me)
        local = [pltpu.make_async_copy(own[a], land[a], self.loc_sems.at[a]) for a in range(self.n_arr)]
        return remote, local

    def start(self):
        remote, local = self._outgoing()
        for cp in remote + local:
            cp.start()

    def wait_recv_from(self, k):
        own = self.send_part(_my_index())
        pos, p = _peer(k)
        land = self.land_slot(p)
        for a in range(self.n_arr):
            self._remote(k, a, own[a], land[a], pos).wait_recv()

    def wait_sends(self):
        remote, local = self._outgoing()
        for cp in remote:
            cp.wait_send()
        for cp in local:
            cp.wait()

    def wait(self):
        for k in range(1, N_DEV):
            self.wait_recv_from(k)
        self.wait_sends()


DIRECT = (1, 2, 4, 6)
RELAYED = (2, 4, 6)


def _gather_by_chip_sems(n_arr):
    direct, relayed = len(DIRECT) * n_arr, len(RELAYED) * n_arr
    return [pltpu.SemaphoreType.DMA((direct,)), pltpu.SemaphoreType.DMA((direct,)),
            pltpu.SemaphoreType.DMA((relayed,)), pltpu.SemaphoreType.DMA((relayed,)), pltpu.SemaphoreType.DMA((n_arr,))]


class _GatherByChip:
    def __init__(self, srcs, land_slot, sems, n_arr):
        self.srcs, self.land_slot, self.n_arr = srcs, land_slot, n_arr
        self.send_sems, self.recv_sems, self.relay_send_sems, self.relay_recv_sems, self.loc_sems = sems

    def _direct(self, k, a, slot):
        idx = DIRECT.index(k) * self.n_arr + a
        return pltpu.make_async_remote_copy(src_ref=self.srcs[a], dst_ref=self.land_slot(slot)[a],
                                            send_sem=self.send_sems.at[idx], recv_sem=self.recv_sems.at[idx],
                                            device_id=_peer(k)[0], device_id_type=MESH_ID)

    def _relay(self, q, a, slot):
        idx = RELAYED.index(q) * self.n_arr + a
        block = self.land_slot(slot)[a]
        return pltpu.make_async_remote_copy(src_ref=block, dst_ref=block, send_sem=self.relay_send_sems.at[idx],
                                            recv_sem=self.relay_recv_sems.at[idx], device_id=_peer(1)[0],
                                            device_id_type=MESH_ID)

    def _own(self, a):
        return pltpu.make_async_copy(self.srcs[a], self.land_slot(_my_index())[a], self.loc_sems.at[a])

    def start(self):
        me = _my_index()
        for k in DIRECT:
            for a in range(self.n_arr):
                self._direct(k, a, me).start()
        for a in range(self.n_arr):
            self._own(a).start()

    def relay(self):
        for q in RELAYED:
            p = _peer(q)[1]
            for a in range(self.n_arr):
                self._direct(q, a, p).wait_recv()
                self._relay(q, a, p).start()

    def wait(self):
        me = _my_index()
        for a in range(self.n_arr):
            self._direct(1, a, _peer(1)[1]).wait_recv()
        for q in RELAYED:
            for a in range(self.n_arr):
                self._relay(q, a, _peer(q + 1)[1]).wait_recv()
        for k in DIRECT:
            for a in range(self.n_arr):
                self._direct(k, a, me).wait_send()
        for q in RELAYED:
            for a in range(self.n_arr):
                self._relay(q, a, _peer(q)[1]).wait_send()
        for a in range(self.n_arr):
            self._own(a).wait()


class _Ride:
    def __init__(self, arrays, out_shapes, send_part, land_slot, zero_dsts=None, zero_shape=None, n_zero=0,
                 gather_by_chip=False):
        self.arrays, self.out_shapes = list(arrays), list(out_shapes)
        self.send_part, self.land_slot, self.n = send_part, land_slot, len(arrays)
        self.zero_dsts, self.zero_shape, self.n_zero = zero_dsts, zero_shape, n_zero
        self.gather_by_chip = gather_by_chip

    def specs(self):
        return [ANY] * self.n

    def scratch(self):
        extra = [pltpu.SemaphoreType.DMA((self.n_zero,)), pltpu.VMEM(self.zero_shape, BF16)] if self.n_zero else []
        sems = _gather_by_chip_sems(self.n) if self.gather_by_chip else _push_sems(self.n)
        return sems + extra

    def push(self, in_refs, out_refs, scratch):
        ride = self
        n_sems = 5 if self.gather_by_chip else 3
        land = lambda s: ride.land_slot(out_refs, s)
        if self.gather_by_chip:
            push = _GatherByChip(list(in_refs), land, tuple(scratch[:n_sems]), self.n)
        else:
            push = _Push(lambda p: ride.send_part(in_refs, p), land, tuple(scratch[:n_sems]), self.n)

        class Both:
            def _fills(self):
                if not ride.n_zero:
                    return []
                zsems, zbuf = scratch[n_sems], scratch[n_sems + 1]
                return [pltpu.make_async_copy(zbuf, dst, zsems.at[z]) for z, dst in enumerate(ride.zero_dsts(out_refs))]

            def start(self):
                push.start()
                if ride.n_zero:
                    scratch[n_sems + 1][...] = jnp.zeros(ride.zero_shape, BF16)
                for cp in self._fills():
                    cp.start()

            def relay(self):
                if ride.gather_by_chip:
                    push.relay()

            def wait(self):
                push.wait()
                for cp in self._fills():
                    cp.wait()

        return Both()


def _slot_of_sender(out_refs, s):
    return [r.at[s] for r in out_refs]


def _push_call(name, arrays, out_shapes, send_part, land_slot):
    n_arr = len(arrays)

    def body(*refs):
        ins, outs, sems = refs[:n_arr], refs[n_arr:2 * n_arr], refs[2 * n_arr:]
        push = _Push(lambda p: send_part(ins, p), lambda s: land_slot(outs, s), sems, n_arr)
        push.start()
        push.wait()

    return pl.pallas_call(
        body, name=name, in_specs=[ANY] * n_arr, out_specs=[ANY] * n_arr, out_shape=out_shapes,
        scratch_shapes=_push_sems(n_arr), compiler_params=pltpu.CompilerParams(has_side_effects=True),
    )(*arrays)


LRU_ROWS = LRU_BLOCK // N_DEV
FFN_PAD_ROWS = FFN_GROUP - 2 * FFN_OUT_SHARD


def _half_rows(d):
    return pl.ds(pl.multiple_of((d % 2) * FFN_OUT_SHARD, 16), FFN_OUT_SHARD)


MIXER_SHAPES = [
    jax.ShapeDtypeStruct((N_DEV, D_MODEL // N_DEV, D_MODEL), BF16),
    jax.ShapeDtypeStruct((N_DEV, D_MODEL // N_DEV, D_MODEL), BF16),
    jax.ShapeDtypeStruct((N_DEV, D_MODEL // N_DEV, D_MODEL), BF16),
    jax.ShapeDtypeStruct((N_DEV, LRU_BLOCKS, LRU_ROWS, LRU_BLOCK), BF16),
    jax.ShapeDtypeStruct((N_DEV, LRU_BLOCKS, LRU_ROWS, LRU_BLOCK), BF16),
]


def _by_owner(t):
    return t.reshape(LRU_BLOCKS, N_DEV, LRU_ROWS, LRU_BLOCK).transpose(1, 0, 2, 3)


def _from_owners(t):
    return t.transpose(1, 0, 2, 3).reshape(LRU_BLOCKS, LRU_BLOCK, LRU_BLOCK)


def _mixer_weights_ride(shards):
    return _Ride(shards, MIXER_SHAPES, lambda ins, p: list(ins), _slot_of_sender, gather_by_chip=True)


def _wfo_ride(shard):
    zero_dsts = lambda outs: [outs[0].at[g, pl.ds(2 * FFN_OUT_SHARD, FFN_PAD_ROWS), :] for g in range(FFN_GROUPS)]
    return _Ride([shard], [jax.ShapeDtypeStruct((FFN_GROUPS, FFN_GROUP, D_MODEL), BF16)], lambda ins, p: list(ins),
                 lambda outs, d: [outs[0].at[d // 2, _half_rows(d), :]], zero_dsts, (FFN_PAD_ROWS, D_MODEL), FFN_GROUPS,
                 gather_by_chip=True)


def _arrival_rank_to_relation(jj):
    return jnp.where(jj == 3, 4, jnp.where(jj == 4, 3, jj))


def _in_proj(h0, norm_w, win_shard, me_arr):
    rows = h0.shape[0]
    tm = _tile(rows, 1664)
    n_i = rows // tm

    direct, relayed = DIRECT, RELAYED

    def body(me_ref, h_ref, nw_ref, wsh_ref, proj_ref, u_ref, wing_ref, u_all, wbuf, copy_sem,
             send_sems, recv_sems, relay_send_sems, relay_recv_sems, own_sem):
        del me_ref
        jj, i = pl.program_id(0), pl.program_id(1)
        me = _my_index()
        sibling = _peer(1)[0]

        def direct_copy(k, slot):
            n = direct.index(k)
            return pltpu.make_async_remote_copy(src_ref=wsh_ref, dst_ref=wing_ref.at[slot], send_sem=send_sems.at[n],
                                                recv_sem=recv_sems.at[n], device_id=_peer(k)[0], device_id_type=MESH_ID)

        def relay_copy(q, slot):
            n = relayed.index(q)
            return pltpu.make_async_remote_copy(src_ref=wing_ref.at[slot], dst_ref=wing_ref.at[slot],
                                                send_sem=relay_send_sems.at[n], recv_sem=relay_recv_sems.at[n],
                                                device_id=sibling, device_id_type=MESH_ID)

        own_slot = pltpu.make_async_copy(wsh_ref, wing_ref.at[me], own_sem)

        @pl.when(jnp.logical_and(jj == 0, i == 0))
        def _():
            for k in direct:
                direct_copy(k, me).start()
            own_slot.start()
            own = pltpu.make_async_copy(wsh_ref, wbuf, copy_sem)
            own.start()
            own.wait()

        for k in range(1, N_DEV):
            rank = {3: 4, 4: 3}.get(k, k)

            @pl.when(jnp.logical_and(jj == rank, i == 0))
            def _(k=k):
                p = _peer(k)[1]
                if k in direct:
                    direct_copy(k, p).wait_recv()
                    if k in relayed:
                        relay_copy(k, p).start()
                else:
                    relay_copy(k - 1, p).wait_recv()
                landed = pltpu.make_async_copy(wing_ref.at[p], wbuf, copy_sem)
                landed.start()
                landed.wait()

        rows_i = pl.ds(pl.multiple_of(i * tm, tm), tm)

        @pl.when(jj == 0)
        def _():
            x = h_ref[...]
            rs = lax.rsqrt(jnp.mean(x * x, axis=-1, keepdims=True) + NORM_EPS)
            u = (x * rs * nw_ref[...]).astype(BF16)
            u_all[rows_i, :] = u
            u_ref[...] = u
        proj_ref[...] = _dot(u_all[rows_i, :], wbuf[...]).astype(BF16)

        @pl.when(jnp.logical_and(jj == N_DEV - 1, i == n_i - 1))
        def _():
            for k in direct:
                direct_copy(k, me).wait_send()
            for q in relayed:
                relay_copy(q, _peer(q)[1]).wait_send()
            own_slot.wait()

    first_pass = lambda jj, i: jnp.where(jj == 0, i, n_i - 1)
    grid_spec = pltpu.PrefetchScalarGridSpec(
        num_scalar_prefetch=1, grid=(N_DEV, n_i),
        in_specs=[pl.BlockSpec((tm, D_MODEL), lambda jj, i, me: (first_pass(jj, i), 0)),
                  pl.BlockSpec((1, D_MODEL), lambda jj, i, me: (0, 0)), ANY],
        out_specs=[pl.BlockSpec((tm, D_MODEL), lambda jj, i, me: (i, me[0] ^ _arrival_rank_to_relation(jj))),
                   pl.BlockSpec((tm, D_MODEL), lambda jj, i, me: (first_pass(jj, i), 0)), ANY],
        scratch_shapes=[pltpu.VMEM((rows, D_MODEL), BF16), pltpu.VMEM((D_MODEL, D_MODEL), BF16),
                        pltpu.SemaphoreType.DMA(()),
                        pltpu.SemaphoreType.DMA((len(direct),)), pltpu.SemaphoreType.DMA((len(direct),)),
                        pltpu.SemaphoreType.DMA((len(relayed),)), pltpu.SemaphoreType.DMA((len(relayed),)),
                        pltpu.SemaphoreType.DMA(())])
    return pl.pallas_call(
        body, name="in_proj", grid_spec=grid_spec,
        out_shape=[jax.ShapeDtypeStruct((rows, N_DEV * D_MODEL), BF16),
                   jax.ShapeDtypeStruct((rows, D_MODEL), BF16),
                   jax.ShapeDtypeStruct((N_DEV, D_MODEL, D_MODEL), BF16)],
        compiler_params=pltpu.CompilerParams(dimension_semantics=("arbitrary", "arbitrary"),
                                             vmem_limit_bytes=VMEM_LIMIT, has_side_effects=True),
    )(me_arr, h0, norm_w, win_shard)


def _seg_spec(rows_per_block, seg):
    return pl.BlockSpec((rows_per_block, D_MODEL), lambda n, seg=seg: (n, seg))


def _chunks_per_step(n_chunks):
    return next(c for c in (5, 3, 2, 1) if n_chunks % c == 0)


def _retention_fwd(proj, cos2, sin2, dec, ride):
    rows = proj.shape[0]
    n_chunks = rows // CHUNK
    per_step = _chunks_per_step(n_chunks)
    n_steps = n_chunks // per_step
    tm = per_step * CHUNK
    n_r = ride.n

    def body(q_ref, k_ref, v_ref, g_ref, c_ref, s_ref, dec_ref, *refs):
        o_ref, zr_ref, st_ref = refs[n_r:n_r + 3]
        state = refs[2 * n_r + 3]
        push = ride.push(refs[:n_r], refs[n_r + 3:2 * n_r + 3], refs[2 * n_r + 4:])

        @pl.when(pl.program_id(0) == 0)
        def _():
            push.start()
            state[...] = jnp.zeros_like(state)

        for h in range(HEADS):
            sl = slice(HEAD_DIM * h, HEAD_DIM * (h + 1))
            st = state[h]
            for c in range(per_step):
                rw = slice(CHUNK * c, CHUNK * (c + 1))
                cos_t, sin_t = c_ref[rw, :], s_ref[rw, :]
                qh = _rot(q_ref[rw, sl].astype(F32), cos_t, sin_t)
                kh = _rot(k_ref[rw, sl].astype(F32), cos_t, sin_t) * QK_SCALE
                qb, kb, vb = qh.astype(BF16), kh.astype(BF16), v_ref[rw, sl]
                s = _dot_nt(qb, kb) * dec_ref[0, h]
                st_ref[c, h] = st
                o = _dot(s.astype(BF16), vb) + _dot(qb, st.astype(BF16)) * dec_ref[1, h]
                st = st * dec_ref[3, h] + _dot_tn((kh * dec_ref[2, h]).astype(BF16), vb)
                o_ref[rw, sl] = o.astype(BF16)
                r = lax.rsqrt(jnp.mean(o * o, axis=-1, keepdims=True) + NORM_EPS)
                g = g_ref[rw, sl].astype(F32)
                zr_ref[rw, sl] = (g * _sigmoid(g) * (o * r)).astype(BF16)
            state[h] = st

        @pl.when(pl.program_id(0) == n_steps // 2)
        def _():
            push.relay()

        @pl.when(pl.program_id(0) == n_steps - 1)
        def _():
            push.wait()

    tab = pl.BlockSpec((tm, HEAD_DIM), lambda n: (n, 0))
    return pl.pallas_call(
        body, name="retention_fwd", grid=(n_steps,),
        in_specs=[_seg_spec(tm, 0), _seg_spec(tm, 1), _seg_spec(tm, 2), _seg_spec(tm, 3), tab, tab,
                  pl.BlockSpec((4, HEADS, CHUNK, CHUNK), lambda n: (0, 0, 0, 0))] + ride.specs(),
        out_specs=[pl.BlockSpec((tm, D_MODEL), lambda n: (n, 0)),
                   pl.BlockSpec((tm, D_MODEL), lambda n: (n, 0)),
                   pl.BlockSpec((per_step, HEADS, HEAD_DIM, HEAD_DIM), lambda n: (n, 0, 0, 0))] + ride.specs(),
        out_shape=[jax.ShapeDtypeStruct((rows, D_MODEL), BF16),
                   jax.ShapeDtypeStruct((rows, D_MODEL), BF16),
                   jax.ShapeDtypeStruct((n_chunks, HEADS, HEAD_DIM, HEAD_DIM), F32)] + ride.out_shapes,
        scratch_shapes=[pltpu.VMEM((HEADS, HEAD_DIM, HEAD_DIM), F32)] + ride.scratch(),
        compiler_params=pltpu.CompilerParams(dimension_semantics=("arbitrary",), vmem_limit_bytes=VMEM_LIMIT,
                                             has_side_effects=True),
    )(proj, proj, proj, proj, cos2, sin2, dec, *ride.arrays)


def _lru_gates(c, ba, bx, wa_ref, wx_ref):
    pre_r, pre_i = [], []
    for g in range(LRU_BLOCKS):
        cg = c[:, LRU_BLOCK * g:LRU_BLOCK * (g + 1)].astype(BF16)
        pre_r.append(_dot(cg, wa_ref[g]))
        pre_i.append(_dot(cg, wx_ref[g]))
    return _sigmoid(jnp.concatenate(pre_r, axis=1) + ba), _sigmoid(jnp.concatenate(pre_i, axis=1) + bx)


def _lru_decay(r, lam):
    sp = jnp.maximum(-lam, 0.0) + jnp.log(1.0 + jnp.exp(-jnp.abs(lam)))
    log_a = -LRU_C * r * sp
    a = jnp.exp(log_a)
    one_minus_a2 = -jnp.tanh(log_a) * (a * a + 1.0)
    inv_mult = lax.rsqrt(jnp.maximum(one_minus_a2, 1e-30))
    return a, one_minus_a2 * inv_mult, inv_mult, sp


def _conv_taps(xbuf, tm, cw_ref, cb_ref):
    c = cb_ref[...] + cw_ref[3:4, :] * xbuf[8:8 + tm, :]
    for back in (1, 2, 3):
        c = c + cw_ref[3 - back:4 - back, :] * xbuf[8 - back:8 - back + tm, :]
    return c


def _lru_fwd(proj, conv_w, conv_b, ba, bx, lam, wa_g, wx_g, ride):
    rows = proj.shape[0]
    tm = _tile(rows, 320)
    n_t = rows // tm
    n_r = ride.n

    def body(x_ref, gt_ref, cw_ref, cb_ref, ba_ref, bx_ref, lam_ref, wa_ref, wx_ref, *refs):
        hs_ref, zl_ref, cri_ref = refs[n_r:n_r + 3]
        xbuf, abuf, ubuf, hcar = refs[2 * n_r + 3:2 * n_r + 7]
        push = ride.push(refs[:n_r], refs[n_r + 3:2 * n_r + 3], refs[2 * n_r + 7:])
        i = pl.program_id(0)

        @pl.when(i == 0)
        def _():
            push.start()
            xbuf[0:8, :] = jnp.zeros((8, D_MODEL), F32)
            hcar[...] = jnp.zeros_like(hcar)

        xbuf[8:8 + tm, :] = x_ref[...].astype(F32)
        c = _conv_taps(xbuf, tm, cw_ref, cb_ref)
        xbuf[0:8, :] = xbuf[tm:tm + 8, :]
        r, ig = _lru_gates(c, ba_ref[...], bx_ref[...], wa_ref, wx_ref)
        a, mult, _, _ = _lru_decay(r, lam_ref[...])
        cri_ref[0] = c.astype(BF16)
        cri_ref[1] = r.astype(BF16)
        cri_ref[2] = ig.astype(BF16)
        row = i * tm + lax.broadcasted_iota(jnp.int32, (tm, 1), 0)
        abuf[...] = a
        ubuf[...] = jnp.where(row >= PAD_ROWS, mult * (ig * c), 0.0)

        sub = lax.broadcasted_iota(jnp.int32, (8, D_MODEL), 0)

        def block(b, carry):
            off = pl.multiple_of(b * 8, 8)
            av, uv = abuf[pl.ds(off, 8), :], ubuf[pl.ds(off, 8), :]
            for s in (1, 2, 4):
                us = jnp.where(sub >= s, pltpu.roll(uv, s, 0), 0.0)
                as_ = jnp.where(sub >= s, pltpu.roll(av, s, 0), 1.0)
                uv = uv + av * us
                av = av * as_
            hv = uv + av * carry
            ubuf[pl.ds(off, 8), :] = hv
            return hv[7:8, :]

        hcar[...] = lax.fori_loop(0, tm // 8, block, hcar[...])
        gl, _ = _gelu_parts(gt_ref[...].astype(F32))
        hs = ubuf[...]
        hs_ref[...] = hs.astype(BF16)
        zl_ref[...] = (gl * hs).astype(BF16)

        @pl.when(i == n_t // 2)
        def _():
            push.relay()

        @pl.when(i == n_t - 1)
        def _():
            push.wait()

    vec = pl.BlockSpec((1, D_MODEL), lambda i: (0, 0))
    mat = pl.BlockSpec((LRU_BLOCKS, LRU_BLOCK, LRU_BLOCK), lambda i: (0, 0, 0))
    row = pl.BlockSpec((tm, D_MODEL), lambda i: (i, 0))
    return pl.pallas_call(
        body, name="lru_fwd", grid=(n_t,),
        in_specs=[_seg_spec(tm, 4), _seg_spec(tm, 5), pl.BlockSpec((4, D_MODEL), lambda i: (0, 0)),
                  vec, vec, vec, vec, mat, mat] + ride.specs(),
        out_specs=[row, row, pl.BlockSpec((3, tm, D_MODEL), lambda i: (0, i, 0))] + ride.specs(),
        out_shape=[jax.ShapeDtypeStruct((rows, D_MODEL), BF16)] * 2
        + [jax.ShapeDtypeStruct((3, rows, D_MODEL), BF16)] + ride.out_shapes,
        scratch_shapes=[pltpu.VMEM((tm + 8, D_MODEL), F32), pltpu.VMEM((tm, D_MODEL), F32),
                        pltpu.VMEM((tm, D_MODEL), F32), pltpu.VMEM((1, D_MODEL), F32)] + ride.scratch(),
        compiler_params=pltpu.CompilerParams(dimension_semantics=("arbitrary",), vmem_limit_bytes=VMEM_LIMIT,
                                             has_side_effects=True),
    )(proj, proj, conv_w, conv_b, ba, bx, lam, wa_g, wx_g, *ride.arrays)


def _mix_fwd(zr, zl, proj, h0, wbr, wbl, wout, ride):
    rows = h0.shape[0]
    tm = _tile(rows, 640)
    n_t = rows // tm
    n_r = ride.n

    def body(zr_ref, zl_ref, ga_ref, gb_ref, h0_ref, wbr_ref, wbl_ref, wo_ref, *refs):
        h1_ref, yr_ref, yl_ref, mx_ref = refs[n_r:n_r + 4]
        push = ride.push(refs[:n_r], refs[n_r + 4:2 * n_r + 4], refs[2 * n_r + 4:])

        @pl.when(pl.program_id(0) == 0)
        def _():
            push.start()

        yr = _dot(zr_ref[...], wbr_ref[...])
        yl = _dot(zl_ref[...], wbl_ref[...])
        mixed = (_sigmoid(ga_ref[...].astype(F32)) * yr + _sigmoid(gb_ref[...].astype(F32)) * yl).astype(BF16)
        yr_ref[...] = yr.astype(BF16)
        yl_ref[...] = yl.astype(BF16)
        mx_ref[...] = mixed
        h1_ref[...] = h0_ref[...] + _dot(mixed, wo_ref[...])

        @pl.when(pl.program_id(0) == n_t // 2)
        def _():
            push.relay()

        @pl.when(pl.program_id(0) == n_t - 1)
        def _():
            push.wait()

    row = pl.BlockSpec((tm, D_MODEL), lambda i: (i, 0))
    wsp = pl.BlockSpec((D_MODEL, D_MODEL), lambda i: (0, 0))
    return pl.pallas_call(
        body, name="mix_fwd", grid=(n_t,),
        in_specs=[row, row, _seg_spec(tm, 6), _seg_spec(tm, 7), row, wsp, wsp, wsp] + ride.specs(),
        out_specs=[row, row, row, row] + ride.specs(),
        out_shape=[jax.ShapeDtypeStruct((rows, D_MODEL), F32)] + [jax.ShapeDtypeStruct((rows, D_MODEL), BF16)] * 3
        + ride.out_shapes,
        scratch_shapes=ride.scratch(),
        compiler_params=pltpu.CompilerParams(dimension_semantics=("arbitrary",), vmem_limit_bytes=VMEM_LIMIT,
                                             has_side_effects=True),
    )(zr, zl, proj, proj, h0, wbr, wbl, wout, *ride.arrays)


def _ffn_fwd_loss(h1, norm_w, wfi_g, wfo_g, final_w, target):
    rows = h1.shape[0]
    tm = _tile(rows, 320)
    piece = 64
    n_piece = tm // piece

    def body(h1_ref, nw_ref, wfi_ref, wfo_ref, fw_ref, *refs):
        t_refs = refs[:n_piece]
        u2_ref, g_ref, up_ref, act_ref, dh2_ref, red_ref = refs[n_piece:]
        i = pl.program_id(0)

        @pl.when(i == 0)
        def _():
            red_ref[...] = jnp.zeros_like(red_ref)

        x = h1_ref[...]
        rs = lax.rsqrt(jnp.mean(x * x, axis=-1, keepdims=True) + NORM_EPS)
        u2 = (x * rs * nw_ref[...]).astype(BF16)
        u2_ref[...] = u2
        ffn = None
        for d in range(FFN_GROUPS):
            cols = slice(FFN_GROUP * d, FFN_GROUP * (d + 1))
            g = _dot(u2, wfi_ref[d])
            up = _dot(u2, wfi_ref[d + FFN_GROUPS])
            act = (g * _sigmoid(g) * up).astype(BF16)
            g_ref[:, cols] = g.astype(BF16)
            up_ref[:, cols] = up.astype(BF16)
            act_ref[:, cols] = act
            part = _dot(act, wfo_ref[d])
            ffn = part if ffn is None else ffn + part

        h2 = x + ffn
        rs = lax.rsqrt(jnp.mean(h2 * h2, axis=-1, keepdims=True) + NORM_EPS)
        nh = h2 * rs
        fw = fw_ref[...]
        row = i * tm + lax.broadcasted_iota(jnp.int32, (tm, 1), 0)
        tgt = jnp.concatenate([t[...] for t in t_refs], axis=0)
        diff = jnp.where(row >= CHUNK, nh * fw - tgt, 0.0)
        dy = diff * (1.0 / D_MODEL)
        red_ref[0:1, :] += jnp.sum(diff * diff, axis=0, keepdims=True)
        red_ref[1:2, :] += jnp.sum(dy * nh, axis=0, keepdims=True)
        dn = dy * fw
        dh2_ref[...] = rs * (dn - nh * jnp.mean(dn * nh, axis=-1, keepdims=True))

    row = pl.BlockSpec((tm, D_MODEL), lambda i: (i, 0))
    vec = pl.BlockSpec((1, D_MODEL), lambda i: (0, 0))
    hid = pl.BlockSpec((tm, FFN_GROUPS * FFN_GROUP), lambda i: (i, 0))
    hid_shape = jax.ShapeDtypeStruct((rows, FFN_GROUPS * FFN_GROUP), BF16)
    resident = dict(pipeline_mode=pl.Buffered(1))
    head_pieces = CHUNK // piece
    t_specs = [pl.BlockSpec((piece, D_MODEL), lambda i, k=k: (jnp.maximum(i * n_piece + k - head_pieces, 0), 0))
               for k in range(n_piece)]
    return pl.pallas_call(
        body, name="ffn_fwd_loss", grid=(rows // tm,),
        in_specs=[row, vec,
                  pl.BlockSpec((2 * FFN_GROUPS, D_MODEL, FFN_GROUP), lambda i: (0, 0, 0), **resident),
                  pl.BlockSpec((FFN_GROUPS, FFN_GROUP, D_MODEL), lambda i: (0, 0, 0), **resident),
                  vec] + t_specs,
        out_specs=[row, hid, hid, hid, row, pl.BlockSpec((8, D_MODEL), lambda i: (0, 0))],
        out_shape=[jax.ShapeDtypeStruct((rows, D_MODEL), BF16), hid_shape, hid_shape, hid_shape,
                   jax.ShapeDtypeStruct((rows, D_MODEL), F32), jax.ShapeDtypeStruct((8, D_MODEL), F32)],
        compiler_params=_cparams(("arbitrary",)),
    )(h1, norm_w, wfi_g, wfo_g, final_w, *([target] * n_piece))


def _wgrad(a, b, ka, tn, out_dtype, b_halves=False):
    rows = a.shape[0]
    na = a.shape[1] // ka
    tm = _tile(rows, 1664)
    nm = rows // tm
    if b_halves:
        per_half = b.shape[2] // tn
        nb = 2 * per_half
        b_spec = pl.BlockSpec((None, tm, tn), lambda p, q, m: (q // per_half, m, q % per_half))
    else:
        nb = b.shape[1] // tn
        b_spec = pl.BlockSpec((tm, tn), lambda p, q, m: (m, q))

    def body(a_ref, b_ref, o_ref, acc):
        m = pl.program_id(2)

        @pl.when(m == 0)
        def _():
            acc[...] = jnp.zeros_like(acc)

        acc[...] += _dot_tn(a_ref[...].astype(BF16), b_ref[...].astype(BF16))

        @pl.when(m == nm - 1)
        def _():
            o_ref[...] = acc[...].astype(out_dtype)

    return pl.pallas_call(
        body, name="wgrad", grid=(na, nb, nm),
        in_specs=[pl.BlockSpec((tm, ka), lambda p, q, m: (m, p)), b_spec],
        out_specs=pl.BlockSpec((None, None, ka, tn), lambda p, q, m: (p, q, 0, 0)),
        out_shape=jax.ShapeDtypeStruct((na, nb, ka, tn), out_dtype),
        scratch_shapes=[pltpu.VMEM((ka, tn), F32)],
        compiler_params=_cparams(("parallel", "parallel", "arbitrary")),
    )(a, b)


WIN_NEAR = (2, 4, 3, 5, 1)
WIN_FAR = (6, 7)
WIN_ORDER = WIN_FAR + WIN_NEAR + (0,)


def _w_in_relation_at(jj):
    k = 0
    for pos in reversed(range(len(WIN_ORDER) - 1)):
        k = jnp.where(jj == pos, WIN_ORDER[pos], k)
    return k


def _wgrad_w_in(u, dproj, me_arr):
    rows = u.shape[0]
    tm = _tile(rows, 1664)
    nm = rows // tm
    n_near = len(WIN_NEAR)

    def body(me_ref, a_ref, b_ref, far_ref, land_ref, acc, sbuf, send_sems, recv_sems, own_sem):
        del me_ref
        jj, m = pl.program_id(0), pl.program_id(1)

        def near_copy(n):
            k = WIN_NEAR[n]
            return pltpu.make_async_remote_copy(src_ref=sbuf.at[n], dst_ref=land_ref.at[k], send_sem=send_sems.at[n],
                                                recv_sem=recv_sems.at[n], device_id=_peer(k)[0], device_id_type=MESH_ID)

        own_copy = pltpu.make_async_copy(sbuf.at[n_near], land_ref.at[0], own_sem)

        @pl.when(m == 0)
        def _():
            acc[...] = jnp.zeros_like(acc)

        acc[...] += _dot_tn(a_ref[...], b_ref[...])

        for pos, k in enumerate(WIN_ORDER):
            @pl.when(jnp.logical_and(jj == pos, m == nm - 1))
            def _(k=k):
                block = acc[...].astype(BF16)
                if k in WIN_FAR:
                    far_ref[...] = block
                elif k == 0:
                    sbuf[n_near] = block
                    own_copy.start()
                else:
                    sbuf[WIN_NEAR.index(k)] = block
                    near_copy(WIN_NEAR.index(k)).start()

        @pl.when(jnp.logical_and(jj == N_DEV - 1, m == nm - 1))
        def _():
            for n in range(n_near):
                near_copy(n).wait_recv()
            for n in range(n_near):
                near_copy(n).wait_send()
            own_copy.wait()

    grid_spec = pltpu.PrefetchScalarGridSpec(
        num_scalar_prefetch=1, grid=(N_DEV, nm),
        in_specs=[pl.BlockSpec((tm, D_MODEL), lambda jj, m, me: (m, 0)),
                  pl.BlockSpec((tm, D_MODEL), lambda jj, m, me: (m, me[0] ^ _w_in_relation_at(jj)))],
        out_specs=[pl.BlockSpec((None, D_MODEL, D_MODEL), lambda jj, m, me: (jnp.minimum(jj, len(WIN_FAR) - 1), 0, 0)),
                   ANY],
        scratch_shapes=[pltpu.VMEM((D_MODEL, D_MODEL), F32), pltpu.VMEM((n_near + 1, D_MODEL, D_MODEL), BF16),
                        pltpu.SemaphoreType.DMA((n_near,)), pltpu.SemaphoreType.DMA((n_near,)),
                        pltpu.SemaphoreType.DMA(())])
    return pl.pallas_call(
        body, name="wgrad_w_in", grid_spec=grid_spec,
        out_shape=[jax.ShapeDtypeStruct((len(WIN_FAR), D_MODEL, D_MODEL), BF16),
                   jax.ShapeDtypeStruct((n_near + 1, D_MODEL, D_MODEL), BF16)],
        compiler_params=pltpu.CompilerParams(dimension_semantics=("arbitrary", "arbitrary"),
                                             vmem_limit_bytes=VMEM_LIMIT, has_side_effects=True),
    )(me_arr, u, dproj)


def _ffn_bwd(dh2, g, up, h1, norm_w, wfi_g, wfo_g):
    rows = h1.shape[0]
    tm = _tile(rows, 320)

    def body(dh2_ref, g_ref, up_ref, h1_ref, nw_ref, wfi_ref, wfo_ref, dgu_ref, dh1_ref, dw_ref):
        @pl.when(pl.program_id(0) == 0)
        def _():
            dw_ref[...] = jnp.zeros_like(dw_ref)

        dh2 = dh2_ref[...]
        dh2_b = dh2.astype(BF16)
        du2 = None
        for d in range(FFN_GROUPS):
            cols = slice(FFN_GROUP * d, FFN_GROUP * (d + 1))
            dact = _dot_nt(dh2_b, wfo_ref[d])
            gv, uv = g_ref[:, cols].astype(F32), up_ref[:, cols].astype(F32)
            sg = _sigmoid(gv)
            dg = (dact * uv * (sg * (1.0 + gv * (1.0 - sg)))).astype(BF16)
            dup = (dact * (gv * sg)).astype(BF16)
            dgu_ref[0, :, cols] = dg
            dgu_ref[1, :, cols] = dup
            part = _dot_nt(dg, wfi_ref[d]) + _dot_nt(dup, wfi_ref[d + FFN_GROUPS])
            du2 = part if du2 is None else du2 + part
        dx, dw = _rms_bwd(h1_ref[...], nw_ref[...], du2)
        dw_ref[0:1, :] += dw
        dh1_ref[...] = dh2 + dx

    row = pl.BlockSpec((tm, D_MODEL), lambda i: (i, 0))
    vec = pl.BlockSpec((1, D_MODEL), lambda i: (0, 0))
    hid = pl.BlockSpec((tm, FFN_GROUPS * FFN_GROUP), lambda i: (i, 0))
    resident = dict(pipeline_mode=pl.Buffered(1))
    return pl.pallas_call(
        body, name="ffn_bwd", grid=(rows // tm,),
        in_specs=[row, hid, hid, row, vec,
                  pl.BlockSpec((2 * FFN_GROUPS, D_MODEL, FFN_GROUP), lambda i: (0, 0, 0), **resident),
                  pl.BlockSpec((FFN_GROUPS, FFN_GROUP, D_MODEL), lambda i: (0, 0, 0), **resident)],
        out_specs=[pl.BlockSpec((2, tm, FFN_GROUPS * FFN_GROUP), lambda i: (0, i, 0)), row,
                   pl.BlockSpec((8, D_MODEL), lambda i: (0, 0))],
        out_shape=[jax.ShapeDtypeStruct((2, rows, FFN_GROUPS * FFN_GROUP), BF16),
                   jax.ShapeDtypeStruct((rows, D_MODEL), F32), jax.ShapeDtypeStruct((8, D_MODEL), F32)],
        compiler_params=_cparams(("arbitrary",)),
    )(dh2, g, up, h1, norm_w, wfi_g, wfo_g)


def _mix_bwd(dh1, yr, yl, proj, wbr, wbl, wout):
    rows = dh1.shape[0]
    tm = _tile(rows, 640)

    def body(dh1_ref, yr_ref, yl_ref, ga_ref, gb_ref, wbr_ref, wbl_ref, wo_ref,
             dyr_ref, dyl_ref, dseg_ref, dzr_ref, dzl_ref):
        dmix = _dot_nt(dh1_ref[...].astype(BF16), wo_ref[...])
        sa, sb = _sigmoid(ga_ref[...].astype(F32)), _sigmoid(gb_ref[...].astype(F32))
        dyr = (dmix * sa).astype(BF16)
        dyl = (dmix * sb).astype(BF16)
        dyr_ref[...] = dyr
        dyl_ref[...] = dyl
        dseg_ref[:, 0:D_MODEL] = (dmix * yr_ref[...].astype(F32) * (sa * (1.0 - sa))).astype(BF16)
        dseg_ref[:, D_MODEL:2 * D_MODEL] = (dmix * yl_ref[...].astype(F32) * (sb * (1.0 - sb))).astype(BF16)
        dzr_ref[...] = _dot_nt(dyr, wbr_ref[...]).astype(BF16)
        dzl_ref[...] = _dot_nt(dyl, wbl_ref[...]).astype(BF16)

    row = pl.BlockSpec((tm, D_MODEL), lambda i: (i, 0))
    wsp = pl.BlockSpec((D_MODEL, D_MODEL), lambda i: (0, 0))
    bshape = jax.ShapeDtypeStruct((rows, D_MODEL), BF16)
    return pl.pallas_call(
        body, name="mix_bwd", grid=(rows // tm,),
        in_specs=[row, row, row, _seg_spec(tm, 6), _seg_spec(tm, 7), wsp, wsp, wsp],
        out_specs=[row, row, pl.BlockSpec((tm, 2 * D_MODEL), lambda i: (i, 3)), row, row],
        out_shape=[bshape, bshape, jax.ShapeDtypeStruct((rows, N_DEV * D_MODEL), BF16), bshape, bshape],
        compiler_params=_cparams(("parallel",)),
    )(dh1, yr, yl, proj, proj, wbr, wbl, wout)


S1_SHAPES = [
    jax.ShapeDtypeStruct((N_DEV, D_MODEL, FFN_GROUP), BF16),
    jax.ShapeDtypeStruct((N_DEV, FFN_OUT_SHARD, D_MODEL), BF16),
]


def _s1_parts(ins, p):
    return [ins[0].at[p], ins[1].at[p // 2, _half_rows(p), :]]


def _lru_bwd(dzl, hs, cri, proj, dproj, conv_w, lam, wa_g, wx_g, s1_grads):
    rows = dzl.shape[0]
    tm = _tile(rows, 320)
    nt = rows // tm
    t8 = tm // 8
    n_s1 = len(s1_grads)

    def body(dzl_ref, hs_ref, hsp_ref, cri_ref, x_ref, gt_ref, cw_ref, lam_ref, wa_ref, wx_ref, dproj_in, *refs):
        del dproj_in
        s1_refs = refs[:n_s1]
        dseg_ref, dwa_ref, dwx_ref, sm_ref = refs[n_s1:n_s1 + 4]
        land_refs = refs[n_s1 + 4:2 * n_s1 + 4]
        (xbuf, abuf, mbuf, ibuf, dbuf, dcbuf, dpr_s, dpi_s, sums, conv_sums, anext, dhcar,
         send_sems, recv_sems, loc_sems) = refs[2 * n_s1 + 4:]
        step = pl.program_id(0)
        i = nt - 1 - step
        push = _Push(lambda p: _s1_parts(s1_refs, p), lambda s: [r.at[s] for r in land_refs],
                     (send_sems, recv_sems, loc_sems), n_s1)

        @pl.when(step == 0)
        def _():
            push.start()
            dwa_ref[...] = jnp.zeros_like(dwa_ref)
            dwx_ref[...] = jnp.zeros_like(dwx_ref)
            sm_ref[...] = jnp.zeros_like(sm_ref)
            anext[...] = jnp.zeros_like(anext)
            dhcar[...] = jnp.zeros_like(dhcar)
            dcbuf[tm:tm + 8, :] = jnp.zeros((8, D_MODEL), F32)

        slab, lanes = 16, 256
        lam_v = lam_ref[...]
        xbuf[0:8, :] = jnp.where(i == 0, 0.0, hsp_ref[8:16, :].astype(F32))
        sums[...] = jnp.zeros_like(sums)

        def before_scan(k, carry):
            rw = pl.ds(pl.multiple_of(k * slab, slab), slab)
            for q in range(D_MODEL // lanes):
                ln = slice(lanes * q, lanes * (q + 1))
                a, mult, inv_mult, _ = _lru_decay(cri_ref[1, rw, ln].astype(F32), lam_v[:, ln])
                abuf[rw, ln] = a
                mbuf[rw, ln] = mult
                ibuf[rw, ln] = inv_mult
                gl, dgl = _gelu_parts(gt_ref[rw, ln].astype(F32))
                dzl_v = dzl_ref[rw, ln].astype(F32)
                hs_v = hs_ref[rw, ln].astype(F32)
                dseg_ref[rw, D_MODEL + lanes * q:D_MODEL + lanes * (q + 1)] = (dzl_v * hs_v * dgl).astype(BF16)
                dbuf[rw, ln] = dzl_v * gl
                xbuf[pl.ds(pl.multiple_of(k * slab + 8, 8), slab), ln] = hs_v
            return carry

        lax.fori_loop(0, tm // slab, before_scan, 0)

        sub = lax.broadcasted_iota(jnp.int32, (8, D_MODEL), 0)

        def block(k, carry):
            dh_next, a_next = carry
            off = pl.multiple_of((t8 - 1 - k) * 8, 8)
            a_blk = abuf[pl.ds(off, 8), :]
            av = jnp.where(sub < 7, pltpu.roll(a_blk, 7, 0), a_next)
            uv = dbuf[pl.ds(off, 8), :]
            for s in (1, 2, 4):
                us = jnp.where(sub < 8 - s, pltpu.roll(uv, 8 - s, 0), 0.0)
                as_ = jnp.where(sub < 8 - s, pltpu.roll(av, 8 - s, 0), 1.0)
                uv = uv + av * us
                av = av * as_
            hv = uv + av * dh_next
            dbuf[pl.ds(off, 8), :] = hv
            return hv[0:1, :], a_blk[0:1, :]

        dh_first, a_first = lax.fori_loop(0, t8, block, (dhcar[...], anext[...]))
        dhcar[...] = dh_first
        anext[...] = a_first

        sp = jnp.maximum(-lam_v, 0.0) + jnp.log(1.0 + jnp.exp(-jnp.abs(lam_v)))
        sub_q = lax.broadcasted_iota(jnp.int32, (8, lanes), 0)
        row16 = lax.broadcasted_iota(jnp.int32, (slab, 1), 0)

        def after_scan(k, carry):
            off = pl.multiple_of(k * slab, slab)
            rw = pl.ds(off, slab)
            for q in range(D_MODEL // lanes):
                ln = slice(lanes * q, lanes * (q + 1))
                before = xbuf[pl.ds(off, 8), ln]
                h_lo = xbuf[pl.ds(pl.multiple_of(off + 8, 8), 8), ln]
                h_hi = xbuf[pl.ds(pl.multiple_of(off + 16, 8), 8), ln]
                hprev = jnp.concatenate([jnp.where(sub_q >= 1, pltpu.roll(h_lo, 1, 0), before[7:8, :]),
                                         jnp.where(sub_q >= 1, pltpu.roll(h_hi, 1, 0), h_lo[7:8, :])], axis=0)
                c, r, ig = (cri_ref[n, rw, ln].astype(F32) for n in range(3))
                a, mult, inv_mult = abuf[rw, ln], mbuf[rw, ln], ibuf[rw, ln]
                dh = dbuf[rw, ln]
                duu = jnp.where(i * tm + off + row16 >= PAD_ROWS, dh, 0.0)
                t_mult = duu * mult
                dlog_a = dh * hprev * a - duu * ig * c * (a * a) * inv_mult
                dpr = dlog_a * (-LRU_C * sp[:, ln]) * r * (1.0 - r)
                dpi = t_mult * c * ig * (1.0 - ig)
                dpr_s[rw, ln] = dpr.astype(BF16)
                dpi_s[rw, ln] = dpi.astype(BF16)
                dcbuf[rw, ln] = t_mult * ig
                sums[0, :, ln] += dlog_a * r
                sums[1, :, ln] += dpr
                sums[2, :, ln] += dpi
            return carry

        lax.fori_loop(0, tm // slab, after_scan, 0)

        dcs = []
        for g in range(LRU_BLOCKS):
            sl = slice(LRU_BLOCK * g, LRU_BLOCK * (g + 1))
            cg = cri_ref[0, :, sl]
            dpr_b, dpi_b = dpr_s[:, sl], dpi_s[:, sl]
            dwa_ref[g] += _dot_tn(cg, dpr_b)
            dwx_ref[g] += _dot_tn(cg, dpi_b)
            dcs.append(_dot_nt(dpr_b, wa_ref[g]) + _dot_nt(dpi_b, wx_ref[g]))
        dc = dcbuf[0:tm, :] + jnp.concatenate(dcs, axis=1)

        dcbuf[0:tm, :] = dc
        conv_sums[...] = jnp.zeros_like(conv_sums)

        def conv_back(k, carry):
            off = pl.multiple_of(k * slab, slab)
            rw = pl.ds(off, slab)
            for q in range(D_MODEL // lanes):
                ln = slice(lanes * q, lanes * (q + 1))
                blocks = [dcbuf[pl.ds(pl.multiple_of(off + 8 * b, 8), 8), ln] for b in range(3)]
                x_v = x_ref[rw, ln].astype(F32)
                now = jnp.concatenate(blocks[:2], axis=0)
                dlin = cw_ref[3:4, ln] * now
                conv_sums[3, :, ln] += now * x_v
                conv_sums[4, :, ln] += now
                for back in (1, 2, 3):
                    turned = [pltpu.roll(b, 8 - back, 0) for b in blocks]
                    later = jnp.concatenate([jnp.where(sub_q < 8 - back, turned[0], turned[1]),
                                             jnp.where(sub_q < 8 - back, turned[1], turned[2])], axis=0)
                    dlin = dlin + cw_ref[3 - back:4 - back, ln] * later
                    conv_sums[3 - back, :, ln] += later * x_v
                dseg_ref[rw, ln] = dlin.astype(BF16)
            return carry

        lax.fori_loop(0, tm // slab, conv_back, 0)
        dcbuf[tm:tm + 8, :] = dcbuf[0:8, :]
        for n in range(5):
            sm_ref[n:n + 1, :] += jnp.sum(conv_sums[n], axis=0, keepdims=True)
        sm_ref[5:6, :] += jnp.sum(sums[1], axis=0, keepdims=True)
        sm_ref[6:7, :] += jnp.sum(sums[2], axis=0, keepdims=True)
        sm_ref[7:8, :] += jnp.sum(sums[0], axis=0, keepdims=True) * (LRU_C * _sigmoid(-lam_v))

        @pl.when(step == nt - 1)
        def _():
            push.wait()

    rowb = pl.BlockSpec((tm, D_MODEL), lambda s: (nt - 1 - s, 0))
    t16 = tm // 16
    prev8 = pl.BlockSpec((16, D_MODEL), lambda s: (jnp.maximum((nt - 1 - s) * t16 - 1, 0), 0))
    seg = lambda k: pl.BlockSpec((tm, D_MODEL), lambda s, k=k: (nt - 1 - s, k))
    vec = pl.BlockSpec((1, D_MODEL), lambda s: (0, 0))
    mat = pl.BlockSpec((LRU_BLOCKS, LRU_BLOCK, LRU_BLOCK), lambda s: (0, 0, 0))
    mshape = jax.ShapeDtypeStruct((LRU_BLOCKS, LRU_BLOCK, LRU_BLOCK), F32)
    n_in = 10
    return pl.pallas_call(
        body, name="lru_bwd", grid=(nt,),
        in_specs=[rowb, rowb, prev8, pl.BlockSpec((3, tm, D_MODEL), lambda s: (0, nt - 1 - s, 0)), seg(4), seg(5),
                  pl.BlockSpec((4, D_MODEL), lambda s: (0, 0)), vec, mat, mat, ANY] + [ANY] * n_s1,
        out_specs=[pl.BlockSpec((tm, 2 * D_MODEL), lambda s: (nt - 1 - s, 2)), mat, mat,
                   pl.BlockSpec((8, D_MODEL), lambda s: (0, 0))] + [ANY] * n_s1,
        out_shape=[jax.ShapeDtypeStruct(dproj.shape, dproj.dtype), mshape, mshape,
                   jax.ShapeDtypeStruct((8, D_MODEL), F32)] + S1_SHAPES,
        input_output_aliases={n_in: 0},
        scratch_shapes=[pltpu.VMEM((tm + 8, D_MODEL), F32), pltpu.VMEM((tm, D_MODEL), F32),
                        pltpu.VMEM((tm, D_MODEL), F32), pltpu.VMEM((tm, D_MODEL), F32),
                        pltpu.VMEM((tm, D_MODEL), F32), pltpu.VMEM((tm + 8, D_MODEL), F32),
                        pltpu.VMEM((tm, D_MODEL), BF16), pltpu.VMEM((tm, D_MODEL), BF16),
                        pltpu.VMEM((3, 16, D_MODEL), F32), pltpu.VMEM((5, 16, D_MODEL), F32),
                        pltpu.VMEM((1, D_MODEL), F32), pltpu.VMEM((1, D_MODEL), F32)] + _push_sems(n_s1),
        compiler_params=pltpu.CompilerParams(dimension_semantics=("arbitrary",), vmem_limit_bytes=VMEM_LIMIT,
                                             has_side_effects=True),
    )(dzl, hs, hs, cri, proj, proj, conv_w, lam, wa_g, wx_g, dproj, *s1_grads)


def _retention_bwd(dzr, o, proj, states, cos2, sin2, dec, dproj, ride):
    rows = dzr.shape[0]
    n_chunks = rows // CHUNK
    per_step = _chunks_per_step(n_chunks)
    n_steps = n_chunks // per_step
    tm = per_step * CHUNK
    n_r = ride.n

    def body(dzr_ref, o_ref, q_ref, k_ref, v_ref, g_ref, st_ref, c_ref, s_ref, dec_ref, dproj_in, *refs):
        del dproj_in
        dseg_ref = refs[n_r]
        dstate = refs[2 * n_r + 1]
        push = ride.push(refs[:n_r], refs[n_r + 1:2 * n_r + 1], refs[2 * n_r + 2:])

        @pl.when(pl.program_id(0) == 0)
        def _():
            push.start()
            dstate[...] = jnp.zeros_like(dstate)

        for h in range(HEADS):
            sl = slice(HEAD_DIM * h, HEAD_DIM * (h + 1))
            intra, qd, kd, cd = dec_ref[0, h], dec_ref[1, h], dec_ref[2, h], dec_ref[3, h]
            dst = dstate[h]
            for c in reversed(range(per_step)):
                rw = slice(CHUNK * c, CHUNK * (c + 1))
                cos_t, sin_t = c_ref[rw, :], s_ref[rw, :]
                o = o_ref[rw, sl].astype(F32)
                g = g_ref[rw, sl].astype(F32)
                dzr_v = dzr_ref[rw, sl].astype(F32)
                sg = _sigmoid(g)
                r = lax.rsqrt(jnp.mean(o * o, axis=-1, keepdims=True) + NORM_EPS)
                on = o * r
                dseg_ref[rw, 3 * D_MODEL + HEAD_DIM * h:3 * D_MODEL + HEAD_DIM * (h + 1)] = (
                    dzr_v * on * (sg * (1.0 + g * (1.0 - sg)))).astype(BF16)
                don = dzr_v * (g * sg)
                do = r * (don - on * jnp.mean(don * on, axis=-1, keepdims=True))
                dob = do.astype(BF16)

                qh = _rot(q_ref[rw, sl].astype(F32), cos_t, sin_t)
                kh = _rot(k_ref[rw, sl].astype(F32), cos_t, sin_t) * QK_SCALE
                qb, kb, vb = qh.astype(BF16), kh.astype(BF16), v_ref[rw, sl]
                s = (_dot_nt(qb, kb) * intra).astype(BF16)
                ds = (_dot_nt(dob, vb) * intra).astype(BF16)
                st_b = st_ref[c, h].astype(BF16)
                dst_b = dst.astype(BF16)
                dv = _dot_tn(s, dob) + _dot((kh * kd).astype(BF16), dst_b)
                dq = _dot(ds, kb) + _dot_nt(dob, st_b) * qd
                dk = _dot_tn(ds, qb) + _dot_nt(vb, dst_b) * kd
                dst = dst * cd + _dot_tn((qh * qd).astype(BF16), dob)
                dseg_ref[rw, 2 * D_MODEL + HEAD_DIM * h:2 * D_MODEL + HEAD_DIM * (h + 1)] = dv.astype(BF16)
                dseg_ref[rw, sl] = _rot_t(dq, cos_t, sin_t).astype(BF16)
                dseg_ref[rw, D_MODEL + HEAD_DIM * h:D_MODEL + HEAD_DIM * (h + 1)] = (
                    _rot_t(dk, cos_t, sin_t) * QK_SCALE).astype(BF16)
            dstate[h] = dst

        @pl.when(pl.program_id(0) == n_steps - 1)
        def _():
            push.wait()

    rev = lambda s: n_steps - 1 - s
    rowb = pl.BlockSpec((tm, D_MODEL), lambda s: (rev(s), 0))
    seg = lambda k: pl.BlockSpec((tm, D_MODEL), lambda s, k=k: (rev(s), k))
    tab = pl.BlockSpec((tm, HEAD_DIM), lambda s: (rev(s), 0))
    return pl.pallas_call(
        body, name="retention_bwd", grid=(n_steps,),
        in_specs=[rowb, rowb, seg(0), seg(1), seg(2), seg(3),
                  pl.BlockSpec((per_step, HEADS, HEAD_DIM, HEAD_DIM), lambda s: (rev(s), 0, 0, 0)), tab, tab,
                  pl.BlockSpec((4, HEADS, CHUNK, CHUNK), lambda s: (0, 0, 0, 0)), ANY] + ride.specs(),
        out_specs=[pl.BlockSpec((tm, 4 * D_MODEL), lambda s: (rev(s), 0))] + ride.specs(),
        out_shape=[jax.ShapeDtypeStruct(dproj.shape, dproj.dtype)] + ride.out_shapes,
        input_output_aliases={10: 0},
        scratch_shapes=[pltpu.VMEM((HEADS, HEAD_DIM, HEAD_DIM), F32)] + ride.scratch(),
        compiler_params=pltpu.CompilerParams(dimension_semantics=("arbitrary",), vmem_limit_bytes=VMEM_LIMIT,
                                             has_side_effects=True),
    )(dzr, o, proj, proj, proj, proj, states, cos2, sin2, dec, dproj, *ride.arrays)


S2_SHAPES = [
    jax.ShapeDtypeStruct((N_DEV, LRU_BLOCKS, LRU_ROWS, LRU_BLOCK), F32),
    jax.ShapeDtypeStruct((N_DEV, LRU_BLOCKS, LRU_ROWS, LRU_BLOCK), F32),
]


def _s2_parts(ins, p):
    return [r.at[p] for r in ins]


def _in_proj_bwd(dproj, win_g, h0, norm_w, dh1, d_win_far, s2_grads, pack_early):
    rows = h0.shape[0]
    tm = _tile(rows, 320)
    n_i = rows // tm
    n_s2 = len(s2_grads)
    n_far = len(WIN_FAR)
    pack_rows = pack_early.shape[0]

    def body(dseg_ref, w_ref, h0_ref, nw_ref, dh1_ref, far_ref, early_ref, *refs):
        s2_refs = refs[:n_s2]
        dh0_ref, dw_ref, far_land = refs[n_s2:n_s2 + 3]
        land_refs = refs[n_s2 + 3:2 * n_s2 + 3]
        early_land, late_land = refs[2 * n_s2 + 3:2 * n_s2 + 5]
        (send_sems, recv_sems, loc_sems, far_send_sems, far_recv_sems, late_buf) = refs[2 * n_s2 + 5:2 * n_s2 + 11]
        early_sems, late_sems = refs[2 * n_s2 + 11:2 * n_s2 + 14], refs[2 * n_s2 + 14:]
        i = pl.program_id(0)
        push = _Push(lambda p: _s2_parts(s2_refs, p), lambda s: [r.at[s] for r in land_refs],
                     (send_sems, recv_sems, loc_sems), n_s2)
        early = _Push(lambda p: [early_ref], lambda s: [early_land.at[s]], tuple(early_sems), 1)
        late = _Push(lambda p: [late_buf], lambda s: [late_land.at[s]], tuple(late_sems), 1)

        def far_copy(n):
            return pltpu.make_async_remote_copy(src_ref=far_ref.at[n], dst_ref=far_land.at[n],
                                                send_sem=far_send_sems.at[n], recv_sem=far_recv_sems.at[n],
                                                device_id=_peer(WIN_FAR[n])[0], device_id_type=MESH_ID)

        @pl.when(i == 0)
        def _():
            for n in range(n_far):
                far_copy(n).start()
            push.start()
            early.start()
            dw_ref[...] = jnp.zeros_like(dw_ref)

        du = _dot_nt(dseg_ref[:, 0:D_MODEL], w_ref[0])
        for j in range(1, N_DEV):
            du = du + _dot_nt(dseg_ref[:, D_MODEL * j:D_MODEL * (j + 1)], w_ref[j])
        dx, dw = _rms_bwd(h0_ref[...], nw_ref[...], du)
        dw_ref[0:1, :] += dw
        dh0 = dh1_ref[...] + dx
        dh0_ref[...] = dh0

        @pl.when(i == 0)
        def _():
            late_buf[8:8 + N_META, :] = dh0[PAD_ROWS:CHUNK, :]

        @pl.when(i == n_i - 1)
        def _():
            late_buf[0:8, :] = dw_ref[...]
            late.start()
            for n in range(n_far):
                far_copy(n).wait_recv()
            for n in range(n_far):
                far_copy(n).wait_send()
            push.wait()
            early.wait()
            late.wait()

    row = pl.BlockSpec((tm, D_MODEL), lambda i: (i, 0))
    vec = pl.BlockSpec((1, D_MODEL), lambda i: (0, 0))
    return pl.pallas_call(
        body, name="in_proj_bwd", grid=(n_i,),
        in_specs=[pl.BlockSpec((tm, N_DEV * D_MODEL), lambda i: (i, 0)),
                  pl.BlockSpec((N_DEV, D_MODEL, D_MODEL), lambda i: (0, 0, 0), pipeline_mode=pl.Buffered(1)),
                  row, vec, row, ANY, ANY] + [ANY] * n_s2,
        out_specs=[row, pl.BlockSpec((8, D_MODEL), lambda i: (0, 0)), ANY] + [ANY] * n_s2 + [ANY, ANY],
        out_shape=[jax.ShapeDtypeStruct((rows, D_MODEL), F32), jax.ShapeDtypeStruct((8, D_MODEL), F32),
                   jax.ShapeDtypeStruct((n_far, D_MODEL, D_MODEL), BF16)] + S2_SHAPES
        + [jax.ShapeDtypeStruct((N_DEV, pack_rows, D_MODEL), F32)] * 2,
        scratch_shapes=_push_sems(n_s2) + [pltpu.SemaphoreType.DMA((n_far,)), pltpu.SemaphoreType.DMA((n_far,)),
                                           pltpu.VMEM((pack_rows, D_MODEL), F32)] + _push_sems(1) + _push_sems(1),
        compiler_params=pltpu.CompilerParams(dimension_semantics=("arbitrary",),
                                             vmem_limit_bytes=VMEM_LIMIT, has_side_effects=True),
    )(dproj, win_g, h0, norm_w, dh1, d_win_far, pack_early, *s2_grads)


def _adamw(g_slots, w, m, v, more_slots=None):
    slots, rows, cols = g_slots.shape
    extra = [] if more_slots is None else [more_slots]
    tr = rows
    for cand in (256, 128, 64, 32, 16, 8):
        if rows % cand == 0 and rows > cand:
            tr = cand
            break

    def body(g_ref, *refs):
        w_ref, m_ref, v_ref, go_ref, d_ref, mo_ref, vo_ref = refs[len(extra):]
        g = g_ref[0].astype(F32)
        for s in range(1, slots):
            g = g + g_ref[s].astype(F32)
        for more_ref in refs[:len(extra)]:
            for s in range(more_ref.shape[0]):
                g = g + more_ref[s].astype(F32)
        m2 = ADAM_B1 * m_ref[...] + (1.0 - ADAM_B1) * g
        v2 = ADAM_B2 * v_ref[...] + (1.0 - ADAM_B2) * (g * g)
        m_hat = m2 / (1.0 - ADAM_B1 ** ADAM_STEP)
        v_hat = v2 / (1.0 - ADAM_B2 ** ADAM_STEP)
        go_ref[...] = g
        d_ref[...] = -ADAM_LR * (m_hat / (jnp.sqrt(v_hat) + ADAM_EPS) + ADAM_WD * w_ref[...])
        mo_ref[...] = m2
        vo_ref[...] = v2

    blk = pl.BlockSpec((tr, cols), lambda i: (i, 0))
    shape = jax.ShapeDtypeStruct((rows, cols), F32)
    return pl.pallas_call(
        body, name="adamw", grid=(rows // tr,),
        in_specs=[pl.BlockSpec((slots, tr, cols), lambda i: (0, i, 0))]
        + [pl.BlockSpec((t.shape[0], tr, cols), lambda i: (0, i, 0)) for t in extra] + [blk, blk, blk],
        out_specs=[blk] * 4, out_shape=[shape] * 4,
        compiler_params=_cparams(("parallel",)),
    )(g_slots, *extra, w, m, v)


def _sum_slots(packs):
    slots, rows, cols = packs.shape

    def body(p_ref, o_ref):
        acc = p_ref[0]
        for s in range(1, slots):
            acc = acc + p_ref[s]
        o_ref[...] = acc

    return pl.pallas_call(
        body, name="sum_slots", out_shape=jax.ShapeDtypeStruct((rows, cols), F32),
        compiler_params=pltpu.CompilerParams(vmem_limit_bytes=VMEM_LIMIT),
    )(packs)


def _gather_small(small):
    shapes = [jax.ShapeDtypeStruct((N_DEV,) + small.shape, F32)]
    return _push_call("gather_small", [small], shapes,
                      lambda ins, p: list(ins), lambda outs, s: [r.at[s] for r in outs])[0]


PACK_CONV_W, PACK_CONV_B, PACK_BA, PACK_BX, PACK_LAM = 0, 4, 5, 6, 7
PACK_FFN_NORM, PACK_SQ_ERR, PACK_FINAL_NORM, PACK_MIX_NORM, PACK_META = 8, 16, 17, 24, 32


def kernel(x, meta_tokens, mix_norm_w, w_in, conv_w, conv_b, lru_wa, lru_ba, lru_wx, lru_bx, lru_lambda, w_branch_ret, w_branch_lru, w_out, ffn_norm_w, w_ffn_in, w_ffn_out, final_norm_w, loss_target, m_meta_tokens, m_mix_norm_w, m_w_in, m_conv_w, m_conv_b, m_lru_wa, m_lru_ba, m_lru_wx, m_lru_bx, m_lru_lambda, m_w_branch_ret, m_w_branch_lru, m_w_out, m_ffn_norm_w, m_w_ffn_in, m_w_ffn_out, m_final_norm_w, v_meta_tokens, v_mix_norm_w, v_w_in, v_conv_w, v_conv_b, v_lru_wa, v_lru_ba, v_lru_wx, v_lru_bx, v_lru_lambda, v_w_branch_ret, v_w_branch_lru, v_w_out, v_ffn_norm_w, v_w_ffn_in, v_w_ffn_out, v_final_norm_w):
    me = _my_index()
    pad4 = ((0, 4), (0, 0))
    fw = final_norm_w.reshape(1, D_MODEL)

    small = jnp.concatenate([meta_tokens, jnp.pad(conv_w[0], pad4)], axis=0)
    small_g = _gather_small(small)
    meta_full = small_g[:, :N_META].transpose(1, 0, 2).reshape(N_META, D_MODEL)
    conv_w_full = small_g[:, N_META:N_META + 4].transpose(1, 0, 2).reshape(4, D_MODEL)
    mixer_shards = [w_branch_ret[0].astype(BF16), w_branch_lru[0].astype(BF16), w_out[0].astype(BF16),
                    lru_wa[0].astype(BF16), lru_wx[0].astype(BF16)]
    wfi_shard = jnp.pad(w_ffn_in[0].astype(BF16), ((0, 0), (0, FFN_GROUP - FFN_SHARD)))
    own_slot = lambda ins, p: list(ins)
    part_of_owner = lambda ins, p: [r.at[p] for r in ins]

    rows = x.shape[1] + CHUNK
    h0 = jnp.concatenate([jnp.zeros((PAD_ROWS, D_MODEL), F32), meta_full, x[0]], axis=0)
    cos2, sin2 = _rope_tables(rows)
    dec = _retention_consts()

    me_arr = me.astype(jnp.int32).reshape(1)
    proj, u, win_g = _in_proj(h0, mix_norm_w, w_in[0].astype(BF16), me_arr)
    o, zr, states, wbr_g, wbl_g, wout_g, wa_g, wx_g = _retention_fwd(
        proj, cos2, sin2, dec, _mixer_weights_ride(mixer_shards))
    wbr, wbl, wout = (t.reshape(D_MODEL, D_MODEL) for t in (wbr_g, wbl_g, wout_g))
    wa_g, wx_g = _from_owners(wa_g), _from_owners(wx_g)
    gather_wfi = _Ride([wfi_shard], [jax.ShapeDtypeStruct((N_DEV, D_MODEL, FFN_GROUP), BF16)],
                       own_slot, _slot_of_sender, gather_by_chip=True)
    hs, zl, cri, wfi_g = _lru_fwd(proj, conv_w_full, conv_b, lru_ba, lru_bx, lru_lambda, wa_g, wx_g, gather_wfi)
    h1, yr, yl, mixed, wfo_g = _mix_fwd(zr, zl, proj, h0, wbr, wbl, wout, _wfo_ride(w_ffn_out[0].astype(BF16)))
    u2, g, up, act, dh2, red = _ffn_fwd_loss(h1, ffn_norm_w, wfi_g, wfo_g, fw, loss_target[0])

    d_wfo = _wgrad(act, dh2, FFN_GROUP, D_MODEL, BF16)[:, 0]
    dgu, dh1, dw_ffn_norm = _ffn_bwd(dh2, g, up, h1, ffn_norm_w, wfi_g, wfo_g)
    d_wfi = _wgrad(u2, dgu, D_MODEL, FFN_GROUP, BF16, b_halves=True)[0]
    d_wout = _wgrad(mixed, dh1, D_MODEL, D_MODEL, BF16)[0, 0]
    dyr, dyl, dproj, dzr, dzl = _mix_bwd(dh1, yr, yl, proj, wbr, wbl, wout)
    d_wbr = _wgrad(zr, dyr, D_MODEL, D_MODEL, BF16)[0, 0]
    d_wbl = _wgrad(zl, dyl, D_MODEL, D_MODEL, BF16)[0, 0]
    dproj, d_wa, d_wx, lru_small, r_fi, r_fo = _lru_bwd(
        dzl, hs, cri, proj, dproj, conv_w_full, lru_lambda, wa_g, wx_g, [d_wfi, d_wfo])
    mix_shape = jax.ShapeDtypeStruct((N_DEV, D_MODEL // N_DEV, D_MODEL), BF16)
    scatter_mix = _Ride([t.reshape(mix_shape.shape) for t in (d_wbr, d_wbl, d_wout)], [mix_shape] * 3,
                        part_of_owner, _slot_of_sender)
    dproj, r_br, r_bl, r_out = _retention_bwd(dzr, o, proj, states, cos2, sin2, dec, dproj, scatter_mix)
    d_win_far, r_in = _wgrad_w_in(u, dproj, me_arr)
    pack_early = jnp.concatenate([lru_small, dw_ffn_norm, red], axis=0)
    dh0, _, r_in_far, r_wa, r_wx, packs_early, packs_late = _in_proj_bwd(
        dproj, win_g, h0, mix_norm_w, dh1, d_win_far, [_by_owner(d_wa), _by_owner(d_wx)], pack_early)
    grad_x = dh0[CHUNK:]

    small_sum = jnp.concatenate([_sum_slots(packs_early), _sum_slots(packs_late)], axis=0)
    loss = (0.5 / D_MODEL) * jnp.sum(small_sum[PACK_SQ_ERR])

    def big_update(slots, w, m, v, more_slots=None):
        shape = w.shape
        w2, m2, v2 = (t.reshape(slots.shape[1:]) for t in (w, m, v))
        return [t.reshape(shape) for t in _adamw(slots, w2, m2, v2, more_slots)]

    res = {}
    res["w_in"] = big_update(r_in, w_in, m_w_in, v_w_in, r_in_far)
    res["w_branch_ret"] = big_update(r_br, w_branch_ret, m_w_branch_ret, v_w_branch_ret)
    res["w_branch_lru"] = big_update(r_bl, w_branch_lru, m_w_branch_lru, v_w_branch_lru)
    res["w_out"] = big_update(r_out, w_out, m_w_out, v_w_out)
    res["w_ffn_in"] = big_update(r_fi[:, :, :FFN_SHARD], w_ffn_in, m_w_ffn_in, v_w_ffn_in)
    res["w_ffn_out"] = big_update(r_fo, w_ffn_out, m_w_ffn_out, v_w_ffn_out)
    res["lru_wa"] = big_update(r_wa.reshape(N_DEV, LRU_BLOCKS * LRU_ROWS, LRU_BLOCK), lru_wa, m_lru_wa, v_lru_wa)
    res["lru_wx"] = big_update(r_wx.reshape(N_DEV, LRU_BLOCKS * LRU_ROWS, LRU_BLOCK), lru_wx, m_lru_wx, v_lru_wx)

    col = me * HEAD_DIM
    g_meta = lax.dynamic_slice(small_sum, (PACK_META, col), (N_META, HEAD_DIM))
    g_conv = lax.dynamic_slice(small_sum, (PACK_CONV_W, col), (8, HEAD_DIM))
    small_names = ["mix_norm_w", "conv_b", "lru_ba", "lru_bx", "lru_lambda", "ffn_norm_w", "final_norm_w"]
    small_rows = [PACK_MIX_NORM, PACK_CONV_B, PACK_BA, PACK_BX, PACK_LAM, PACK_FFN_NORM, PACK_FINAL_NORM]
    small_w = [mix_norm_w, conv_b, lru_ba, lru_bx, lru_lambda, ffn_norm_w, fw]
    small_m = [m_mix_norm_w, m_conv_b, m_lru_ba, m_lru_bx, m_lru_lambda, m_ffn_norm_w, m_final_norm_w.reshape(1, -1)]
    small_v = [v_mix_norm_w, v_conv_b, v_lru_ba, v_lru_bx, v_lru_lambda, v_ffn_norm_w, v_final_norm_w.reshape(1, -1)]

    def pack_small(vec_list, meta_t, conv_t):
        return jnp.concatenate([t.reshape(8, HEAD_DIM) for t in vec_list] + [meta_t, jnp.pad(conv_t[0], pad4)], axis=0)

    g_small = jnp.concatenate([small_sum[r].reshape(8, HEAD_DIM) for r in small_rows] + [g_meta, g_conv], axis=0)
    outs_small = _adamw(g_small[None], pack_small(small_w, meta_tokens, conv_w),
                        pack_small(small_m, m_meta_tokens, m_conv_w), pack_small(small_v, v_meta_tokens, v_conv_w))
    for idx, name in enumerate(small_names):
        shape = final_norm_w.shape if name == "final_norm_w" else (1, D_MODEL)
        res[name] = [t[8 * idx:8 * idx + 8].reshape(shape) for t in outs_small]
    res["meta_tokens"] = [t[56:72] for t in outs_small]
    res["conv_w"] = [t[72:76].reshape(1, 4, HEAD_DIM) for t in outs_small]

    order = ["meta_tokens", "mix_norm_w", "w_in", "conv_w", "conv_b", "lru_wa", "lru_ba", "lru_wx", "lru_bx",
             "lru_lambda", "w_branch_ret", "w_branch_lru", "w_out", "ffn_norm_w", "w_ffn_in", "w_ffn_out",
             "final_norm_w"]
    out = [loss, grad_x[None]]
    for kind in range(4):
        out += [res[name][kind] for name in order]
    return tuple(out)
```

```python
import jax
import jax.numpy as jnp
from jax import lax
from jax.experimental import pallas as pl
from jax.experimental.pallas import tpu as pltpu

F32 = jnp.float32
BF16 = jnp.bfloat16

D_MODEL = 1024
N_META = 16
CHUNK = 128
PAD_ROWS = CHUNK - N_META
HEADS = 8
HEAD_DIM = 128
ROPE_BASE = 10000.0
QK_SCALE = HEAD_DIM ** -0.5
LRU_BLOCKS = 4
LRU_BLOCK = 256
LRU_C = 8.0
FFN_HIDDEN = 2816
N_DEV = 8
FFN_SHARD = 2 * FFN_HIDDEN // N_DEV
FFN_GROUP = 768
FFN_GROUPS = 4
FFN_OUT_SHARD = FFN_HIDDEN // N_DEV
NORM_EPS = 1e-6

ADAM_LR = 0.001
ADAM_B1 = 0.9
ADAM_B2 = 0.999
ADAM_EPS = 1e-08
ADAM_WD = 0.01
ADAM_STEP = 10

VMEM_LIMIT = 56 * 1024 * 1024
MESH_ID = pl.DeviceIdType.MESH
ANY = pl.BlockSpec(memory_space=pl.ANY)


def _cparams(sem):
    return pltpu.CompilerParams(dimension_semantics=sem, vmem_limit_bytes=VMEM_LIMIT)


def _tile(rows, cap):
    t = cap - cap % 64
    while rows % t:
        t -= 64
    return t


def _dot(a, b):
    return jnp.dot(a, b, preferred_element_type=F32)


def _dot_nt(a, b):
    return lax.dot_general(a, b, (((1,), (1,)), ((), ())), preferred_element_type=F32)


def _dot_tn(a, b):
    return lax.dot_general(a, b, (((0,), (0,)), ((), ())), preferred_element_type=F32)


def _sigmoid(x):
    return 0.5 * jnp.tanh(0.5 * x) + 0.5


def _gelu_parts(x):
    k = 0.7978845608028654
    inner = k * (x + 0.044715 * x * x * x)
    t = jnp.tanh(inner)
    g = 0.5 * x * (1.0 + t)
    dg = 0.5 * (1.0 + t) + 0.5 * x * (1.0 - t * t) * k * (1.0 + 3.0 * 0.044715 * x * x)
    return g, dg


def _rot(x, cos2, sin2):
    return x * cos2 + pltpu.roll(x, HEAD_DIM // 2, 1) * sin2


def _rot_t(dx, cos2, sin2):
    return dx * cos2 - pltpu.roll(dx, HEAD_DIM // 2, 1) * sin2


def _rms_bwd(x, w, dy):
    rs = lax.rsqrt(jnp.mean(x * x, axis=-1, keepdims=True) + NORM_EPS)
    nh = x * rs
    dw = jnp.sum(dy * nh, axis=0, keepdims=True)
    dn = dy * w
    dx = rs * (dn - nh * jnp.mean(dn * nh, axis=-1, keepdims=True))
    return dx, dw


def _retention_consts():
    h = jnp.arange(HEADS, dtype=F32)
    log_g = jnp.log(1.0 - 2.0 ** (-5.0 - h))
    idx = jnp.arange(CHUNK, dtype=F32)
    diff = idx[:, None] - idx[None, :]
    intra = jnp.where(diff[None] >= 0, jnp.exp(jnp.maximum(diff, 0.0)[None] * log_g[:, None, None]), 0.0)
    q_decay = jnp.exp((idx + 1.0)[:, None] * log_g[None, :])
    k_decay = jnp.exp((CHUNK - 1.0 - idx)[:, None] * log_g[None, :])
    chunk_decay = jnp.exp(CHUNK * log_g)
    shape = (HEADS, CHUNK, CHUNK)
    qd = jnp.broadcast_to(q_decay.T[:, :, None], shape)
    kd = jnp.broadcast_to(k_decay.T[:, :, None], shape)
    cd = jnp.broadcast_to(chunk_decay[:, None, None], shape)
    return jnp.stack([intra, qd, kd, cd])


def _rope_tables(rows):
    pos = jnp.maximum(jnp.arange(rows) - PAD_ROWS, 0).astype(F32)
    inv_freq = ROPE_BASE ** (-jnp.arange(0, HEAD_DIM, 2, dtype=F32) / HEAD_DIM)
    ang = pos[:, None] * inv_freq[None, :]
    cos, sin = jnp.cos(ang), jnp.sin(ang)
    return jnp.concatenate([cos, cos], axis=1), jnp.concatenate([-sin, sin], axis=1)


def _my_index():
    return 4 * lax.axis_index("x") + 2 * lax.axis_index("y") + lax.axis_index("c")


def _peer(k):
    x, y, c = lax.axis_index("x"), lax.axis_index("y"), lax.axis_index("c")
    px = 1 - x if k & 4 else x
    py = 1 - y if k & 2 else y
    pc = 1 - c if k & 1 else c
    return (px, py, pc), 4 * px + 2 * py + pc


def _push_sems(n_arr):
    n_rem = (N_DEV - 1) * n_arr
    return [pltpu.SemaphoreType.DMA((n_rem,)), pltpu.SemaphoreType.DMA((n_rem,)), pltpu.SemaphoreType.DMA((n_arr,))]


class _Push:
    def __init__(self, send_part, land_slot, sems, n_arr):
        self.send_part, self.land_slot, self.n_arr = send_part, land_slot, n_arr
        self.send_sems, self.recv_sems, self.loc_sems = sems

    def _remote(self, k, a, src, dst, pos):
        idx = (k - 1) * self.n_arr + a
        return pltpu.make_async_remote_copy(src_ref=src, dst_ref=dst, send_sem=self.send_sems.at[idx],
                                            recv_sem=self.recv_sems.at[idx], device_id=pos, device_id_type=MESH_ID)

    def _outgoing(self):
        me = _my_index()
        land = self.land_slot(me)
        remote = []
        for k in range(1, N_DEV):
            pos, p = _peer(k)
            src = self.send_part(p)
            remote += [self._remote(k, a, src[a], land[a], pos) for a in range(self.n_arr)]
        own = self.send_part(me)
        local = [pltpu.make_async_copy(own[a], land[a], self.loc_sems.at[a]) for a in range(self.n_arr)]
        return remote, local

    def start(self):
        remote, local = self._outgoing()
        for cp in remote + local:
            cp.start()

    def wait_recv_from(self, k):
        own = self.send_part(_my_index())
        pos, p = _peer(k)
        land = self.land_slot(p)
        for a in range(self.n_arr):
            self._remote(k, a, own[a], land[a], pos).wait_recv()

    def wait_sends(self):
        remote, local = self._outgoing()
        for cp in remote:
            cp.wait_send()
        for cp in local:
            cp.wait()

    def wait(self):
        for k in range(1, N_DEV):
            self.wait_recv_from(k)
        self.wait_sends()


DIRECT = (1, 2, 4, 6)
RELAYED = (2, 4, 6)


def _gather_by_chip_sems(n_arr):
    direct, relayed = len(DIRECT) * n_arr, len(RELAYED) * n_arr
    return [pltpu.SemaphoreType.DMA((direct,)), pltpu.SemaphoreType.DMA((direct,)),
            pltpu.SemaphoreType.DMA((relayed,)), pltpu.SemaphoreType.DMA((relayed,)), pltpu.SemaphoreType.DMA((n_arr,))]


class _GatherByChip:
    def __init__(self, srcs, land_slot, sems, n_arr):
        self.srcs, self.land_slot, self.n_arr = srcs, land_slot, n_arr
        self.send_sems, self.recv_sems, self.relay_send_sems, self.relay_recv_sems, self.loc_sems = sems

    def _direct(self, k, a, slot):
        idx = DIRECT.index(k) * self.n_arr + a
        return pltpu.make_async_remote_copy(src_ref=self.srcs[a], dst_ref=self.land_slot(slot)[a],
                                            send_sem=self.send_sems.at[idx], recv_sem=self.recv_sems.at[idx],
                                            device_id=_peer(k)[0], device_id_type=MESH_ID)

    def _relay(self, q, a, slot):
        idx = RELAYED.index(q) * self.n_arr + a
        block = self.land_slot(slot)[a]
        return pltpu.make_async_remote_copy(src_ref=block, dst_ref=block, send_sem=self.relay_send_sems.at[idx],
                                            recv_sem=self.relay_recv_sems.at[idx], device_id=_peer(1)[0],
                                            device_id_type=MESH_ID)

    def _own(self, a):
        return pltpu.make_async_copy(self.srcs[a], self.land_slot(_my_index())[a], self.loc_sems.at[a])

    def start(self):
        me = _my_index()
        for k in DIRECT:
            for a in range(self.n_arr):
                self._direct(k, a, me).start()
        for a in range(self.n_arr):
            self._own(a).start()

    def relay(self):
        for q in RELAYED:
            p = _peer(q)[1]
            for a in range(self.n_arr):
                self._direct(q, a, p).wait_recv()
                self._relay(q, a, p).start()

    def wait(self):
        me = _my_index()
        for a in range(self.n_arr):
            self._direct(1, a, _peer(1)[1]).wait_recv()
        for q in RELAYED:
            for a in range(self.n_arr):
                self._relay(q, a, _peer(q + 1)[1]).wait_recv()
        for k in DIRECT:
            for a in range(self.n_arr):
                self._direct(k, a, me).wait_send()
        for q in RELAYED:
            for a in range(self.n_arr):
                self._relay(q, a, _peer(q)[1]).wait_send()
        for a in range(self.n_arr):
            self._own(a).wait()


class _Ride:
    def __init__(self, arrays, out_shapes, send_part, land_slot, zero_dsts=None, zero_shape=None, n_zero=0,
                 gather_by_chip=False):
        self.arrays, self.out_shapes = list(arrays), list(out_shapes)
        self.send_part, self.land_slot, self.n = send_part, land_slot, len(arrays)
        self.zero_dsts, self.zero_shape, self.n_zero = zero_dsts, zero_shape, n_zero
        self.gather_by_chip = gather_by_chip

    def specs(self):
        return [ANY] * self.n

    def scratch(self):
        extra = [pltpu.SemaphoreType.DMA((self.n_zero,)), pltpu.VMEM(self.zero_shape, BF16)] if self.n_zero else []
        sems = _gather_by_chip_sems(self.n) if self.gather_by_chip else _push_sems(self.n)
        return sems + extra

    def push(self, in_refs, out_refs, scratch):
        ride = self
        n_sems = 5 if self.gather_by_chip else 3
        land = lambda s: ride.land_slot(out_refs, s)
        if self.gather_by_chip:
            push = _GatherByChip(list(in_refs), land, tuple(scratch[:n_sems]), self.n)
        else:
            push = _Push(lambda p: ride.send_part(in_refs, p), land, tuple(scratch[:n_sems]), self.n)

        class Both:
            def _fills(self):
                if not ride.n_zero:
                    return []
                zsems, zbuf = scratch[n_sems], scratch[n_sems + 1]
                return [pltpu.make_async_copy(zbuf, dst, zsems.at[z]) for z, dst in enumerate(ride.zero_dsts(out_refs))]

            def start(self):
                push.start()
                if ride.n_zero:
                    scratch[n_sems + 1][...] = jnp.zeros(ride.zero_shape, BF16)
                for cp in self._fills():
                    cp.start()

            def relay(self):
                if ride.gather_by_chip:
                    push.relay()

            def wait(self):
                push.wait()
                for cp in self._fills():
                    cp.wait()

        return Both()


def _slot_of_sender(out_refs, s):
    return [r.at[s] for r in out_refs]


def _push_call(name, arrays, out_shapes, send_part, land_slot):
    n_arr = len(arrays)

    def body(*refs):
        ins, outs, sems = refs[:n_arr], refs[n_arr:2 * n_arr], refs[2 * n_arr:]
        push = _Push(lambda p: send_part(ins, p), lambda s: land_slot(outs, s), sems, n_arr)
        push.start()
        push.wait()

    return pl.pallas_call(
        body, name=name, in_specs=[ANY] * n_arr, out_specs=[ANY] * n_arr, out_shape=out_shapes,
        scratch_shapes=_push_sems(n_arr), compiler_params=pltpu.CompilerParams(has_side_effects=True),
    )(*arrays)


LRU_ROWS = LRU_BLOCK // N_DEV
FFN_PAD_ROWS = FFN_GROUP - 2 * FFN_OUT_SHARD


def _half_rows(d):
    return pl.ds(pl.multiple_of((d % 2) * FFN_OUT_SHARD, 16), FFN_OUT_SHARD)


MIXER_SHAPES = [
    jax.ShapeDtypeStruct((N_DEV, D_MODEL // N_DEV, D_MODEL), BF16),
    jax.ShapeDtypeStruct((N_DEV, D_MODEL // N_DEV, D_MODEL), BF16),
    jax.ShapeDtypeStruct((N_DEV, D_MODEL // N_DEV, D_MODEL), BF16),
    jax.ShapeDtypeStruct((N_DEV, LRU_BLOCKS, LRU_ROWS, LRU_BLOCK), BF16),
    jax.ShapeDtypeStruct((N_DEV, LRU_BLOCKS, LRU_ROWS, LRU_BLOCK), BF16),
]


def _by_owner(t):
    return t.reshape(LRU_BLOCKS, N_DEV, LRU_ROWS, LRU_BLOCK).transpose(1, 0, 2, 3)


def _from_owners(t):
    return t.transpose(1, 0, 2, 3).reshape(LRU_BLOCKS, LRU_BLOCK, LRU_BLOCK)


def _mixer_weights_ride(shards):
    return _Ride(shards, MIXER_SHAPES, lambda ins, p: list(ins), _slot_of_sender, gather_by_chip=True)


def _wfo_ride(shard):
    zero_dsts = lambda outs: [outs[0].at[g, pl.ds(2 * FFN_OUT_SHARD, FFN_PAD_ROWS), :] for g in range(FFN_GROUPS)]
    return _Ride([shard], [jax.ShapeDtypeStruct((FFN_GROUPS, FFN_GROUP, D_MODEL), BF16)], lambda ins, p: list(ins),
                 lambda outs, d: [outs[0].at[d // 2, _half_rows(d), :]], zero_dsts, (FFN_PAD_ROWS, D_MODEL), FFN_GROUPS,
                 gather_by_chip=True)


W_IN_USE_ORDER = (0, 1, 2, 4, 6, 3, 5, 7)


def _arrival_rank_to_relation(jj):
    k = W_IN_USE_ORDER[-1]
    for pos in reversed(range(N_DEV - 1)):
        k = jnp.where(jj == pos, W_IN_USE_ORDER[pos], k)
    return k


def _in_proj(h0, norm_w, win_shard, me_arr):
    rows = h0.shape[0]
    tm = _tile(rows, 1664)
    n_i = rows // tm

    direct, relayed = DIRECT, RELAYED

    def body(me_ref, h_ref, nw_ref, wsh_ref, proj_ref, u_ref, wing_ref, u_all, wbuf, copy_sem,
             send_sems, recv_sems, relay_send_sems, relay_recv_sems, own_sem):
        del me_ref
        jj, i = pl.program_id(0), pl.program_id(1)
        me = _my_index()
        sibling = _peer(1)[0]

        def direct_copy(k, slot):
            n = direct.index(k)
            return pltpu.make_async_remote_copy(src_ref=wsh_ref, dst_ref=wing_ref.at[slot], send_sem=send_sems.at[n],
                                                recv_sem=recv_sems.at[n], device_id=_peer(k)[0], device_id_type=MESH_ID)

        def relay_copy(q, slot):
            n = relayed.index(q)
            return pltpu.make_async_remote_copy(src_ref=wing_ref.at[slot], dst_ref=wing_ref.at[slot],
                                                send_sem=relay_send_sems.at[n], recv_sem=relay_recv_sems.at[n],
                                                device_id=sibling, device_id_type=MESH_ID)

        own_slot = pltpu.make_async_copy(wsh_ref, wing_ref.at[me], own_sem)

        @pl.when(jnp.logical_and(jj == 0, i == 0))
        def _():
            for k in direct:
                direct_copy(k, me).start()
            own_slot.start()
            own = pltpu.make_async_copy(wsh_ref, wbuf, copy_sem)
            own.start()
            own.wait()

        for k in range(1, N_DEV):
            rank = W_IN_USE_ORDER.index(k)

            @pl.when(jnp.logical_and(jj == rank, i == 0))
            def _(k=k):
                p = _peer(k)[1]
                if k in direct:
                    direct_copy(k, p).wait_recv()
                    if k in relayed:
                        relay_copy(k, p).start()
                else:
                    relay_copy(k - 1, p).wait_recv()
                landed = pltpu.make_async_copy(wing_ref.at[p], wbuf, copy_sem)
                landed.start()
                landed.wait()

        rows_i = pl.ds(pl.multiple_of(i * tm, tm), tm)

        @pl.when(jj == 0)
        def _():
            x = h_ref[...]
            rs = lax.rsqrt(jnp.mean(x * x, axis=-1, keepdims=True) + NORM_EPS)
            u = (x * rs * nw_ref[...]).astype(BF16)
            u_all[rows_i, :] = u
            u_ref[...] = u
        proj_ref[...] = _dot(u_all[rows_i, :], wbuf[...]).astype(BF16)

        @pl.when(jnp.logical_and(jj == N_DEV - 1, i == n_i - 1))
        def _():
            for k in direct:
                direct_copy(k, me).wait_send()
            for q in relayed:
                relay_copy(q, _peer(q)[1]).wait_send()
            own_slot.wait()

    first_pass = lambda jj, i: jnp.where(jj == 0, i, n_i - 1)
    grid_spec = pltpu.PrefetchScalarGridSpec(
        num_scalar_prefetch=1, grid=(N_DEV, n_i),
        in_specs=[pl.BlockSpec((tm, D_MODEL), lambda jj, i, me: (first_pass(jj, i), 0)),
                  pl.BlockSpec((1, D_MODEL), lambda jj, i, me: (0, 0)), ANY],
        out_specs=[pl.BlockSpec((tm, D_MODEL), lambda jj, i, me: (i, me[0] ^ _arrival_rank_to_relation(jj))),
                   pl.BlockSpec((tm, D_MODEL), lambda jj, i, me: (first_pass(jj, i), 0)), ANY],
        scratch_shapes=[pltpu.VMEM((rows, D_MODEL), BF16), pltpu.VMEM((D_MODEL, D_MODEL), BF16),
                        pltpu.SemaphoreType.DMA(()),
                        pltpu.SemaphoreType.DMA((len(direct),)), pltpu.SemaphoreType.DMA((len(direct),)),
                        pltpu.SemaphoreType.DMA((len(relayed),)), pltpu.SemaphoreType.DMA((len(relayed),)),
                        pltpu.SemaphoreType.DMA(())])
    return pl.pallas_call(
        body, name="in_proj", grid_spec=grid_spec,
        out_shape=[jax.ShapeDtypeStruct((rows, N_DEV * D_MODEL), BF16),
                   jax.ShapeDtypeStruct((rows, D_MODEL), BF16),
                   jax.ShapeDtypeStruct((N_DEV, D_MODEL, D_MODEL), BF16)],
        compiler_params=pltpu.CompilerParams(dimension_semantics=("arbitrary", "arbitrary"),
                                             vmem_limit_bytes=VMEM_LIMIT, has_side_effects=True),
    )(me_arr, h0, norm_w, win_shard)


def _seg_spec(rows_per_block, seg):
    return pl.BlockSpec((rows_per_block, D_MODEL), lambda n, seg=seg: (n, seg))


def _chunks_per_step(n_chunks):
    return next(c for c in (5, 3, 2, 1) if n_chunks % c == 0)


def _retention_fwd(proj, cos2, sin2, dec, ride):
    rows = proj.shape[0]
    n_chunks = rows // CHUNK
    per_step = _chunks_per_step(n_chunks)
    n_steps = n_chunks // per_step
    tm = per_step * CHUNK
    n_r = ride.n

    def body(q_ref, k_ref, v_ref, g_ref, c_ref, s_ref, dec_ref, *refs):
        o_ref, zr_ref, st_ref = refs[n_r:n_r + 3]
        state = refs[2 * n_r + 3]
        push = ride.push(refs[:n_r], refs[n_r + 3:2 * n_r + 3], refs[2 * n_r + 4:])

        @pl.when(pl.program_id(0) == 0)
        def _():
            push.start()
            state[...] = jnp.zeros_like(state)

        for h in range(HEADS):
            sl = slice(HEAD_DIM * h, HEAD_DIM * (h + 1))
            st = state[h]
            for c in range(per_step):
                rw = slice(CHUNK * c, CHUNK * (c + 1))
                cos_t, sin_t = c_ref[rw, :], s_ref[rw, :]
                qh = _rot(q_ref[rw, sl].astype(F32), cos_t, sin_t)
                kh = _rot(k_ref[rw, sl].astype(F32), cos_t, sin_t) * QK_SCALE
                qb, kb, vb = qh.astype(BF16), kh.astype(BF16), v_ref[rw, sl]
                s = _dot_nt(qb, kb) * dec_ref[0, h]
                st_ref[c, h] = st
                o = _dot(s.astype(BF16), vb) + _dot(qb, st.astype(BF16)) * dec_ref[1, h]
                st = st * dec_ref[3, h] + _dot_tn((kh * dec_ref[2, h]).astype(BF16), vb)
                o_ref[rw, sl] = o.astype(BF16)
                r = lax.rsqrt(jnp.mean(o * o, axis=-1, keepdims=True) + NORM_EPS)
                g = g_ref[rw, sl].astype(F32)
                zr_ref[rw, sl] = (g * _sigmoid(g) * (o * r)).astype(BF16)
            state[h] = st

        @pl.when(pl.program_id(0) == n_steps // 2)
        def _():
            push.relay()

        @pl.when(pl.program_id(0) == n_steps - 1)
        def _():
            push.wait()

    tab = pl.BlockSpec((tm, HEAD_DIM), lambda n: (n, 0))
    return pl.pallas_call(
        body, name="retention_fwd", grid=(n_steps,),
        in_specs=[_seg_spec(tm, 0), _seg_spec(tm, 1), _seg_spec(tm, 2), _seg_spec(tm, 3), tab, tab,
                  pl.BlockSpec((4, HEADS, CHUNK, CHUNK), lambda n: (0, 0, 0, 0))] + ride.specs(),
        out_specs=[pl.BlockSpec((tm, D_MODEL), lambda n: (n, 0)),
                   pl.BlockSpec((tm, D_MODEL), lambda n: (n, 0)),
                   pl.BlockSpec((per_step, HEADS, HEAD_DIM, HEAD_DIM), lambda n: (n, 0, 0, 0))] + ride.specs(),
        out_shape=[jax.ShapeDtypeStruct((rows, D_MODEL), BF16),
                   jax.ShapeDtypeStruct((rows, D_MODEL), BF16),
                   jax.ShapeDtypeStruct((n_chunks, HEADS, HEAD_DIM, HEAD_DIM), F32)] + ride.out_shapes,
        scratch_shapes=[pltpu.VMEM((HEADS, HEAD_DIM, HEAD_DIM), F32)] + ride.scratch(),
        compiler_params=pltpu.CompilerParams(dimension_semantics=("arbitrary",), vmem_limit_bytes=VMEM_LIMIT,
                                             has_side_effects=True),
    )(proj, proj, proj, proj, cos2, sin2, dec, *ride.arrays)


def _lru_gates(c, ba, bx, wa_ref, wx_ref):
    pre_r, pre_i = [], []
    for g in range(LRU_BLOCKS):
        cg = c[:, LRU_BLOCK * g:LRU_BLOCK * (g + 1)].astype(BF16)
        pre_r.append(_dot(cg, wa_ref[g]))
        pre_i.append(_dot(cg, wx_ref[g]))
    return _sigmoid(jnp.concatenate(pre_r, axis=1) + ba), _sigmoid(jnp.concatenate(pre_i, axis=1) + bx)


def _lru_decay(r, lam):
    sp = jnp.maximum(-lam, 0.0) + jnp.log(1.0 + jnp.exp(-jnp.abs(lam)))
    log_a = -LRU_C * r * sp
    a = jnp.exp(log_a)
    one_minus_a2 = -jnp.tanh(log_a) * (a * a + 1.0)
    inv_mult = lax.rsqrt(jnp.maximum(one_minus_a2, 1e-30))
    return a, one_minus_a2 * inv_mult, inv_mult, sp


def _conv_taps(xbuf, tm, cw_ref, cb_ref):
    c = cb_ref[...] + cw_ref[3:4, :] * xbuf[8:8 + tm, :]
    for back in (1, 2, 3):
        c = c + cw_ref[3 - back:4 - back, :] * xbuf[8 - back:8 - back + tm, :]
    return c


def _lru_fwd(proj, conv_w, conv_b, ba, bx, lam, wa_g, wx_g, ride):
    rows = proj.shape[0]
    tm = _tile(rows, 320)
    n_t = rows // tm
    n_r = ride.n

    def body(x_ref, gt_ref, cw_ref, cb_ref, ba_ref, bx_ref, lam_ref, wa_ref, wx_ref, *refs):
        hs_ref, zl_ref, cri_ref = refs[n_r:n_r + 3]
        xbuf, abuf, ubuf, hcar = refs[2 * n_r + 3:2 * n_r + 7]
        push = ride.push(refs[:n_r], refs[n_r + 3:2 * n_r + 3], refs[2 * n_r + 7:])
        i = pl.program_id(0)

        @pl.when(i == 0)
        def _():
            push.start()
            xbuf[0:8, :] = jnp.zeros((8, D_MODEL), F32)
            hcar[...] = jnp.zeros_like(hcar)

        xbuf[8:8 + tm, :] = x_ref[...].astype(F32)
        c = _conv_taps(xbuf, tm, cw_ref, cb_ref)
        xbuf[0:8, :] = xbuf[tm:tm + 8, :]
        r, ig = _lru_gates(c, ba_ref[...], bx_ref[...], wa_ref, wx_ref)
        a, mult, _, _ = _lru_decay(r, lam_ref[...])
        cri_ref[0] = c.astype(BF16)
        cri_ref[1] = r.astype(BF16)
        cri_ref[2] = ig.astype(BF16)
        row = i * tm + lax.broadcasted_iota(jnp.int32, (tm, 1), 0)
        abuf[...] = a
        ubuf[...] = jnp.where(row >= PAD_ROWS, mult * (ig * c), 0.0)

        sub = lax.broadcasted_iota(jnp.int32, (8, D_MODEL), 0)

        def block(b, carry):
            off = pl.multiple_of(b * 8, 8)
            av, uv = abuf[pl.ds(off, 8), :], ubuf[pl.ds(off, 8), :]
            for s in (1, 2, 4):
                us = jnp.where(sub >= s, pltpu.roll(uv, s, 0), 0.0)
                as_ = jnp.where(sub >= s, pltpu.roll(av, s, 0), 1.0)
                uv = uv + av * us
                av = av * as_
            hv = uv + av * carry
            ubuf[pl.ds(off, 8), :] = hv
            return hv[7:8, :]

        hcar[...] = lax.fori_loop(0, tm // 8, block, hcar[...])
        gl, _ = _gelu_parts(gt_ref[...].astype(F32))
        hs = ubuf[...]
        hs_ref[...] = hs.astype(BF16)
        zl_ref[...] = (gl * hs).astype(BF16)

        @pl.when(i == n_t // 2)
        def _():
            push.relay()

        @pl.when(i == n_t - 1)
        def _():
            push.wait()

    vec = pl.BlockSpec((1, D_MODEL), lambda i: (0, 0))
    mat = pl.BlockSpec((LRU_BLOCKS, LRU_BLOCK, LRU_BLOCK), lambda i: (0, 0, 0))
    row = pl.BlockSpec((tm, D_MODEL), lambda i: (i, 0))
    return pl.pallas_call(
        body, name="lru_fwd", grid=(n_t,),
        in_specs=[_seg_spec(tm, 4), _seg_spec(tm, 5), pl.BlockSpec((4, D_MODEL), lambda i: (0, 0)),
                  vec, vec, vec, vec, mat, mat] + ride.specs(),
        out_specs=[row, row, pl.BlockSpec((3, tm, D_MODEL), lambda i: (0, i, 0))] + ride.specs(),
        out_shape=[jax.ShapeDtypeStruct((rows, D_MODEL), BF16)] * 2
        + [jax.ShapeDtypeStruct((3, rows, D_MODEL), BF16)] + ride.out_shapes,
        scratch_shapes=[pltpu.VMEM((tm + 8, D_MODEL), F32), pltpu.VMEM((tm, D_MODEL), F32),
                        pltpu.VMEM((tm, D_MODEL), F32), pltpu.VMEM((1, D_MODEL), F32)] + ride.scratch(),
        compiler_params=pltpu.CompilerParams(dimension_semantics=("arbitrary",), vmem_limit_bytes=VMEM_LIMIT,
                                             has_side_effects=True),
    )(proj, proj, conv_w, conv_b, ba, bx, lam, wa_g, wx_g, *ride.arrays)


def _mix_fwd(zr, zl, proj, h0, wbr, wbl, wout, ride):
    rows = h0.shape[0]
    tm = _tile(rows, 640)
    n_t = rows // tm
    n_r = ride.n

    def body(zr_ref, zl_ref, ga_ref, gb_ref, h0_ref, wbr_ref, wbl_ref, wo_ref, *refs):
        h1_ref, yr_ref, yl_ref, mx_ref = refs[n_r:n_r + 4]
        push = ride.push(refs[:n_r], refs[n_r + 4:2 * n_r + 4], refs[2 * n_r + 4:])

        @pl.when(pl.program_id(0) == 0)
        def _():
            push.start()

        yr = _dot(zr_ref[...], wbr_ref[...])
        yl = _dot(zl_ref[...], wbl_ref[...])
        mixed = (_sigmoid(ga_ref[...].astype(F32)) * yr + _sigmoid(gb_ref[...].astype(F32)) * yl).astype(BF16)
        yr_ref[...] = yr.astype(BF16)
        yl_ref[...] = yl.astype(BF16)
        mx_ref[...] = mixed
        h1_ref[...] = h0_ref[...] + _dot(mixed, wo_ref[...])

        @pl.when(pl.program_id(0) == n_t // 2)
        def _():
            push.relay()

        @pl.when(pl.program_id(0) == n_t - 1)
        def _():
            push.wait()

    row = pl.BlockSpec((tm, D_MODEL), lambda i: (i, 0))
    wsp = pl.BlockSpec((D_MODEL, D_MODEL), lambda i: (0, 0))
    return pl.pallas_call(
        body, name="mix_fwd", grid=(n_t,),
        in_specs=[row, row, _seg_spec(tm, 6), _seg_spec(tm, 7), row, wsp, wsp, wsp] + ride.specs(),
        out_specs=[row, row, row, row] + ride.specs(),
        out_shape=[jax.ShapeDtypeStruct((rows, D_MODEL), F32)] + [jax.ShapeDtypeStruct((rows, D_MODEL), BF16)] * 3
        + ride.out_shapes,
        scratch_shapes=ride.scratch(),
        compiler_params=pltpu.CompilerParams(dimension_semantics=("arbitrary",), vmem_limit_bytes=VMEM_LIMIT,
                                             has_side_effects=True),
    )(zr, zl, proj, proj, h0, wbr, wbl, wout, *ride.arrays)


def _ffn_fwd_loss(h1, norm_w, wfi_g, wfo_g, final_w, target):
    rows = h1.shape[0]
    tm = _tile(rows, 320)
    piece = 64
    n_piece = tm // piece

    def body(h1_ref, nw_ref, wfi_ref, wfo_ref, fw_ref, *refs):
        t_refs = refs[:n_piece]
        u2_ref, g_ref, up_ref, act_ref, dh2_ref, red_ref = refs[n_piece:]
        i = pl.program_id(0)

        @pl.when(i == 0)
        def _():
            red_ref[...] = jnp.zeros_like(red_ref)

        x = h1_ref[...]
        rs = lax.rsqrt(jnp.mean(x * x, axis=-1, keepdims=True) + NORM_EPS)
        u2 = (x * rs * nw_ref[...]).astype(BF16)
        u2_ref[...] = u2
        ffn = None
        for d in range(FFN_GROUPS):
            cols = slice(FFN_GROUP * d, FFN_GROUP * (d + 1))
            g = _dot(u2, wfi_ref[d])
            up = _dot(u2, wfi_ref[d + FFN_GROUPS])
            act = (g * _sigmoid(g) * up).astype(BF16)
            g_ref[:, cols] = g.astype(BF16)
            up_ref[:, cols] = up.astype(BF16)
            act_ref[:, cols] = act
            part = _dot(act, wfo_ref[d])
            ffn = part if ffn is None else ffn + part

        h2 = x + ffn
        rs = lax.rsqrt(jnp.mean(h2 * h2, axis=-1, keepdims=True) + NORM_EPS)
        nh = h2 * rs
        fw = fw_ref[...]
        row = i * tm + lax.broadcasted_iota(jnp.int32, (tm, 1), 0)
        tgt = jnp.concatenate([t[...] for t in t_refs], axis=0)
        diff = jnp.where(row >= CHUNK, nh * fw - tgt, 0.0)
        dy = diff * (1.0 / D_MODEL)
        red_ref[0:1, :] += jnp.sum(diff * diff, axis=0, keepdims=True)
        red_ref[1:2, :] += jnp.sum(dy * nh, axis=0, keepdims=True)
        dn = dy * fw
        dh2_ref[...] = rs * (dn - nh * jnp.mean(dn * nh, axis=-1, keepdims=True))

    row = pl.BlockSpec((tm, D_MODEL), lambda i: (i, 0))
    vec = pl.BlockSpec((1, D_MODEL), lambda i: (0, 0))
    hid = pl.BlockSpec((tm, FFN_GROUPS * FFN_GROUP), lambda i: (i, 0))
    hid_shape = jax.ShapeDtypeStruct((rows, FFN_GROUPS * FFN_GROUP), BF16)
    resident = dict(pipeline_mode=pl.Buffered(1))
    head_pieces = CHUNK // piece
    t_specs = [pl.BlockSpec((piece, D_MODEL), lambda i, k=k: (jnp.maximum(i * n_piece + k - head_pieces, 0), 0))
               for k in range(n_piece)]
    return pl.pallas_call(
        body, name="ffn_fwd_loss", grid=(rows // tm,),
        in_specs=[row, vec,
                  pl.BlockSpec((2 * FFN_GROUPS, D_MODEL, FFN_GROUP), lambda i: (0, 0, 0), **resident),
                  pl.BlockSpec((FFN_GROUPS, FFN_GROUP, D_MODEL), lambda i: (0, 0, 0), **resident),
                  vec] + t_specs,
        out_specs=[row, hid, hid, hid, row, pl.BlockSpec((8, D_MODEL), lambda i: (0, 0))],
        out_shape=[jax.ShapeDtypeStruct((rows, D_MODEL), BF16), hid_shape, hid_shape, hid_shape,
                   jax.ShapeDtypeStruct((rows, D_MODEL), F32), jax.ShapeDtypeStruct((8, D_MODEL), F32)],
        compiler_params=_cparams(("arbitrary",)),
    )(h1, norm_w, wfi_g, wfo_g, final_w, *([target] * n_piece))


def _wgrad(a, b, ka, tn, out_dtype, b_halves=False):
    rows = a.shape[0]
    na = a.shape[1] // ka
    tm = _tile(rows, 1664)
    nm = rows // tm
    if b_halves:
        per_half = b.shape[2] // tn
        nb = 2 * per_half
        b_spec = pl.BlockSpec((None, tm, tn), lambda p, q, m: (q // per_half, m, q % per_half))
    else:
        nb = b.shape[1] // tn
        b_spec = pl.BlockSpec((tm, tn), lambda p, q, m: (m, q))

    def body(a_ref, b_ref, o_ref, acc):
        m = pl.program_id(2)

        @pl.when(m == 0)
        def _():
            acc[...] = jnp.zeros_like(acc)

        acc[...] += _dot_tn(a_ref[...].astype(BF16), b_ref[...].astype(BF16))

        @pl.when(m == nm - 1)
        def _():
            o_ref[...] = acc[...].astype(out_dtype)

    return pl.pallas_call(
        body, name="wgrad", grid=(na, nb, nm),
        in_specs=[pl.BlockSpec((tm, ka), lambda p, q, m: (m, p)), b_spec],
        out_specs=pl.BlockSpec((None, None, ka, tn), lambda p, q, m: (p, q, 0, 0)),
        out_shape=jax.ShapeDtypeStruct((na, nb, ka, tn), out_dtype),
        scratch_shapes=[pltpu.VMEM((ka, tn), F32)],
        compiler_params=_cparams(("parallel", "parallel", "arbitrary")),
    )(a, b)


WIN_NEAR = (2, 4, 3, 5, 1)
WIN_FAR = (6, 7)
WIN_ORDER = WIN_FAR + WIN_NEAR + (0,)


def _w_in_relation_at(jj):
    k = 0
    for pos in reversed(range(len(WIN_ORDER) - 1)):
        k = jnp.where(jj == pos, WIN_ORDER[pos], k)
    return k


def _wgrad_w_in(u, dproj, me_arr):
    rows = u.shape[0]
    tm = _tile(rows, 1664)
    nm = rows // tm
    n_near = len(WIN_NEAR)

    def body(me_ref, a_ref, b_ref, far_ref, land_ref, acc, sbuf, send_sems, recv_sems, own_sem):
        del me_ref
        jj, m = pl.program_id(0), pl.program_id(1)

        def near_copy(n):
            k = WIN_NEAR[n]
            return pltpu.make_async_remote_copy(src_ref=sbuf.at[n], dst_ref=land_ref.at[k], send_sem=send_sems.at[n],
                                                recv_sem=recv_sems.at[n], device_id=_peer(k)[0], device_id_type=MESH_ID)

        own_copy = pltpu.make_async_copy(sbuf.at[n_near], land_ref.at[0], own_sem)

        @pl.when(m == 0)
        def _():
            acc[...] = jnp.zeros_like(acc)

        acc[...] += _dot_tn(a_ref[...], b_ref[...])

        for pos, k in enumerate(WIN_ORDER):
            @pl.when(jnp.logical_and(jj == pos, m == nm - 1))
            def _(k=k):
                block = acc[...].astype(BF16)
                if k in WIN_FAR:
                    far_ref[...] = block
                elif k == 0:
                    sbuf[n_near] = block
                    own_copy.start()
                else:
                    sbuf[WIN_NEAR.index(k)] = block
                    near_copy(WIN_NEAR.index(k)).start()

        @pl.when(jnp.logical_and(jj == N_DEV - 1, m == nm - 1))
        def _():
            for n in range(n_near):
                near_copy(n).wait_recv()
            for n in range(n_near):
                near_copy(n).wait_send()
            own_copy.wait()

    grid_spec = pltpu.PrefetchScalarGridSpec(
        num_scalar_prefetch=1, grid=(N_DEV, nm),
        in_specs=[pl.BlockSpec((tm, D_MODEL), lambda jj, m, me: (m, 0)),
                  pl.BlockSpec((tm, D_MODEL), lambda jj, m, me: (m, me[0] ^ _w_in_relation_at(jj)))],
        out_specs=[pl.BlockSpec((None, D_MODEL, D_MODEL), lambda jj, m, me: (jnp.minimum(jj, len(WIN_FAR) - 1), 0, 0)),
                   ANY],
        scratch_shapes=[pltpu.VMEM((D_MODEL, D_MODEL), F32), pltpu.VMEM((n_near + 1, D_MODEL, D_MODEL), BF16),
                        pltpu.SemaphoreType.DMA((n_near,)), pltpu.SemaphoreType.DMA((n_near,)),
                        pltpu.SemaphoreType.DMA(())])
    return pl.pallas_call(
        body, name="wgrad_w_in", grid_spec=grid_spec,
        out_shape=[jax.ShapeDtypeStruct((len(WIN_FAR), D_MODEL, D_MODEL), BF16),
                   jax.ShapeDtypeStruct((n_near + 1, D_MODEL, D_MODEL), BF16)],
        compiler_params=pltpu.CompilerParams(dimension_semantics=("arbitrary", "arbitrary"),
                                             vmem_limit_bytes=VMEM_LIMIT, has_side_effects=True),
    )(me_arr, u, dproj)


def _ffn_bwd(dh2, g, up, h1, norm_w, wfi_g, wfo_g):
    rows = h1.shape[0]
    tm = _tile(rows, 320)

    def body(dh2_ref, g_ref, up_ref, h1_ref, nw_ref, wfi_ref, wfo_ref, dgu_ref, dh1_ref, dw_ref):
        @pl.when(pl.program_id(0) == 0)
        def _():
            dw_ref[...] = jnp.zeros_like(dw_ref)

        dh2 = dh2_ref[...]
        dh2_b = dh2.astype(BF16)
        du2 = None
        for d in range(FFN_GROUPS):
            cols = slice(FFN_GROUP * d, FFN_GROUP * (d + 1))
            dact = _dot_nt(dh2_b, wfo_ref[d])
            gv, uv = g_ref[:, cols].astype(F32), up_ref[:, cols].astype(F32)
            sg = _sigmoid(gv)
            dg = (dact * uv * (sg * (1.0 + gv * (1.0 - sg)))).astype(BF16)
            dup = (dact * (gv * sg)).astype(BF16)
            dgu_ref[0, :, cols] = dg
            dgu_ref[1, :, cols] = dup
            part = _dot_nt(dg, wfi_ref[d]) + _dot_nt(dup, wfi_ref[d + FFN_GROUPS])
            du2 = part if du2 is None else du2 + part
        dx, dw = _rms_bwd(h1_ref[...], nw_ref[...], du2)
        dw_ref[0:1, :] += dw
        dh1_ref[...] = dh2 + dx

    row = pl.BlockSpec((tm, D_MODEL), lambda i: (i, 0))
    vec = pl.BlockSpec((1, D_MODEL), lambda i: (0, 0))
    hid = pl.BlockSpec((tm, FFN_GROUPS * FFN_GROUP), lambda i: (i, 0))
    resident = dict(pipeline_mode=pl.Buffered(1))
    return pl.pallas_call(
        body, name="ffn_bwd", grid=(rows // tm,),
        in_specs=[row, hid, hid, row, vec,
                  pl.BlockSpec((2 * FFN_GROUPS, D_MODEL, FFN_GROUP), lambda i: (0, 0, 0), **resident),
                  pl.BlockSpec((FFN_GROUPS, FFN_GROUP, D_MODEL), lambda i: (0, 0, 0), **resident)],
        out_specs=[pl.BlockSpec((2, tm, FFN_GROUPS * FFN_GROUP), lambda i: (0, i, 0)), row,
                   pl.BlockSpec((8, D_MODEL), lambda i: (0, 0))],
        out_shape=[jax.ShapeDtypeStruct((2, rows, FFN_GROUPS * FFN_GROUP), BF16),
                   jax.ShapeDtypeStruct((rows, D_MODEL), F32), jax.ShapeDtypeStruct((8, D_MODEL), F32)],
        compiler_params=_cparams(("arbitrary",)),
    )(dh2, g, up, h1, norm_w, wfi_g, wfo_g)


def _mix_bwd(dh1, yr, yl, proj, wbr, wbl, wout):
    rows = dh1.shape[0]
    tm = _tile(rows, 640)

    def body(dh1_ref, yr_ref, yl_ref, ga_ref, gb_ref, wbr_ref, wbl_ref, wo_ref,
             dyr_ref, dyl_ref, dseg_ref, dzr_ref, dzl_ref):
        dmix = _dot_nt(dh1_ref[...].astype(BF16), wo_ref[...])
        sa, sb = _sigmoid(ga_ref[...].astype(F32)), _sigmoid(gb_ref[...].astype(F32))
        dyr = (dmix * sa).astype(BF16)
        dyl = (dmix * sb).astype(BF16)
        dyr_ref[...] = dyr
        dyl_ref[...] = dyl
        dseg_ref[:, 0:D_MODEL] = (dmix * yr_ref[...].astype(F32) * (sa * (1.0 - sa))).astype(BF16)
        dseg_ref[:, D_MODEL:2 * D_MODEL] = (dmix * yl_ref[...].astype(F32) * (sb * (1.0 - sb))).astype(BF16)
        dzr_ref[...] = _dot_nt(dyr, wbr_ref[...]).astype(BF16)
        dzl_ref[...] = _dot_nt(dyl, wbl_ref[...]).astype(BF16)

    row = pl.BlockSpec((tm, D_MODEL), lambda i: (i, 0))
    wsp = pl.BlockSpec((D_MODEL, D_MODEL), lambda i: (0, 0))
    bshape = jax.ShapeDtypeStruct((rows, D_MODEL), BF16)
    return pl.pallas_call(
        body, name="mix_bwd", grid=(rows // tm,),
        in_specs=[row, row, row, _seg_spec(tm, 6), _seg_spec(tm, 7), wsp, wsp, wsp],
        out_specs=[row, row, pl.BlockSpec((tm, 2 * D_MODEL), lambda i: (i, 3)), row, row],
        out_shape=[bshape, bshape, jax.ShapeDtypeStruct((rows, N_DEV * D_MODEL), BF16), bshape, bshape],
        compiler_params=_cparams(("parallel",)),
    )(dh1, yr, yl, proj, proj, wbr, wbl, wout)


S1_SHAPES = [
    jax.ShapeDtypeStruct((N_DEV, D_MODEL, FFN_GROUP), BF16),
    jax.ShapeDtypeStruct((N_DEV, FFN_OUT_SHARD, D_MODEL), BF16),
]


def _s1_parts(ins, p):
    return [ins[0].at[p], ins[1].at[p // 2, _half_rows(p), :]]


def _lru_bwd(dzl, hs, cri, proj, dproj, conv_w, lam, wa_g, wx_g, s1_grads):
    rows = dzl.shape[0]
    tm = _tile(rows, 320)
    nt = rows // tm
    t8 = tm // 8
    n_s1 = len(s1_grads)

    def body(dzl_ref, hs_ref, hsp_ref, cri_ref, x_ref, gt_ref, cw_ref, lam_ref, wa_ref, wx_ref, dproj_in, *refs):
        del dproj_in
        s1_refs = refs[:n_s1]
        dseg_ref, dwa_ref, dwx_ref, sm_ref = refs[n_s1:n_s1 + 4]
        land_refs = refs[n_s1 + 4:2 * n_s1 + 4]
        (xbuf, abuf, mbuf, ibuf, dbuf, dcbuf, dpr_s, dpi_s, sums, conv_sums, anext, dhcar,
         send_sems, recv_sems, loc_sems) = refs[2 * n_s1 + 4:]
        step = pl.program_id(0)
        i = nt - 1 - step
        push = _Push(lambda p: _s1_parts(s1_refs, p), lambda s: [r.at[s] for r in land_refs],
                     (send_sems, recv_sems, loc_sems), n_s1)

        @pl.when(step == 0)
        def _():
            push.start()
            dwa_ref[...] = jnp.zeros_like(dwa_ref)
            dwx_ref[...] = jnp.zeros_like(dwx_ref)
            sm_ref[...] = jnp.zeros_like(sm_ref)
            anext[...] = jnp.zeros_like(anext)
            dhcar[...] = jnp.zeros_like(dhcar)
            dcbuf[tm:tm + 8, :] = jnp.zeros((8, D_MODEL), F32)

        slab, lanes = 16, 256
        lam_v = lam_ref[...]
        xbuf[0:8, :] = jnp.where(i == 0, 0.0, hsp_ref[8:16, :].astype(F32))
        sums[...] = jnp.zeros_like(sums)

        def before_scan(k, carry):
            rw = pl.ds(pl.multiple_of(k * slab, slab), slab)
            for q in range(D_MODEL // lanes):
                ln = slice(lanes * q, lanes * (q + 1))
                a, mult, inv_mult, _ = _lru_decay(cri_ref[1, rw, ln].astype(F32), lam_v[:, ln])
                abuf[rw, ln] = a
                mbuf[rw, ln] = mult
                ibuf[rw, ln] = inv_mult
                gl, dgl = _gelu_parts(gt_ref[rw, ln].astype(F32))
                dzl_v = dzl_ref[rw, ln].astype(F32)
                hs_v = hs_ref[rw, ln].astype(F32)
                dseg_ref[rw, D_MODEL + lanes * q:D_MODEL + lanes * (q + 1)] = (dzl_v * hs_v * dgl).astype(BF16)
                dbuf[rw, ln] = dzl_v * gl
                xbuf[pl.ds(pl.multiple_of(k * slab + 8, 8), slab), ln] = hs_v
            return carry

        lax.fori_loop(0, tm // slab, before_scan, 0)

        sub = lax.broadcasted_iota(jnp.int32, (8, D_MODEL), 0)

        def block(k, carry):
            dh_next, a_next = carry
            off = pl.multiple_of((t8 - 1 - k) * 8, 8)
            a_blk = abuf[pl.ds(off, 8), :]
            av = jnp.where(sub < 7, pltpu.roll(a_blk, 7, 0), a_next)
            uv = dbuf[pl.ds(off, 8), :]
            for s in (1, 2, 4):
                us = jnp.where(sub < 8 - s, pltpu.roll(uv, 8 - s, 0), 0.0)
                as_ = jnp.where(sub < 8 - s, pltpu.roll(av, 8 - s, 0), 1.0)
                uv = uv + av * us
                av = av * as_
            hv = uv + av * dh_next
            dbuf[pl.ds(off, 8), :] = hv
            return hv[0:1, :], a_blk[0:1, :]

        dh_first, a_first = lax.fori_loop(0, t8, block, (dhcar[...], anext[...]))
        dhcar[...] = dh_first
        anext[...] = a_first

        sp = jnp.maximum(-lam_v, 0.0) + jnp.log(1.0 + jnp.exp(-jnp.abs(lam_v)))
        sub_q = lax.broadcasted_iota(jnp.int32, (8, lanes), 0)
        row16 = lax.broadcasted_iota(jnp.int32, (slab, 1), 0)

        def after_scan(k, carry):
            off = pl.multiple_of(k * slab, slab)
            rw = pl.ds(off, slab)
            for q in range(D_MODEL // lanes):
                ln = slice(lanes * q, lanes * (q + 1))
                before = xbuf[pl.ds(off, 8), ln]
                h_lo = xbuf[pl.ds(pl.multiple_of(off + 8, 8), 8), ln]
                h_hi = xbuf[pl.ds(pl.multiple_of(off + 16, 8), 8), ln]
                hprev = jnp.concatenate([jnp.where(sub_q >= 1, pltpu.roll(h_lo, 1, 0), before[7:8, :]),
                                         jnp.where(sub_q >= 1, pltpu.roll(h_hi, 1, 0), h_lo[7:8, :])], axis=0)
                c, r, ig = (cri_ref[n, rw, ln].astype(F32) for n in range(3))
                a, mult, inv_mult = abuf[rw, ln], mbuf[rw, ln], ibuf[rw, ln]
                dh = dbuf[rw, ln]
                duu = jnp.where(i * tm + off + row16 >= PAD_ROWS, dh, 0.0)
                t_mult = duu * mult
                dlog_a = dh * hprev * a - duu * ig * c * (a * a) * inv_mult
                dpr = dlog_a * (-LRU_C * sp[:, ln]) * r * (1.0 - r)
                dpi = t_mult * c * ig * (1.0 - ig)
                dpr_s[rw, ln] = dpr.astype(BF16)
                dpi_s[rw, ln] = dpi.astype(BF16)
                dcbuf[rw, ln] = t_mult * ig
                sums[0, :, ln] += dlog_a * r
                sums[1, :, ln] += dpr
                sums[2, :, ln] += dpi
            return carry

        lax.fori_loop(0, tm // slab, after_scan, 0)

        dcs = []
        for g in range(LRU_BLOCKS):
            sl = slice(LRU_BLOCK * g, LRU_BLOCK * (g + 1))
            cg = cri_ref[0, :, sl]
            dpr_b, dpi_b = dpr_s[:, sl], dpi_s[:, sl]
            dwa_ref[g] += _dot_tn(cg, dpr_b)
            dwx_ref[g] += _dot_tn(cg, dpi_b)
            dcs.append(_dot_nt(dpr_b, wa_ref[g]) + _dot_nt(dpi_b, wx_ref[g]))
        dc = dcbuf[0:tm, :] + jnp.concatenate(dcs, axis=1)

        dcbuf[0:tm, :] = dc
        conv_sums[...] = jnp.zeros_like(conv_sums)

        def conv_back(k, carry):
            off = pl.multiple_of(k * slab, slab)
            rw = pl.ds(off, slab)
            for q in range(D_MODEL // lanes):
                ln = slice(lanes * q, lanes * (q + 1))
                blocks = [dcbuf[pl.ds(pl.multiple_of(off + 8 * b, 8), 8), ln] for b in range(3)]
                x_v = x_ref[rw, ln].astype(F32)
                now = jnp.concatenate(blocks[:2], axis=0)
                dlin = cw_ref[3:4, ln] * now
                conv_sums[3, :, ln] += now * x_v
                conv_sums[4, :, ln] += now
                for back in (1, 2, 3):
                    turned = [pltpu.roll(b, 8 - back, 0) for b in blocks]
                    later = jnp.concatenate([jnp.where(sub_q < 8 - back, turned[0], turned[1]),
                                             jnp.where(sub_q < 8 - back, turned[1], turned[2])], axis=0)
                    dlin = dlin + cw_ref[3 - back:4 - back, ln] * later
                    conv_sums[3 - back, :, ln] += later * x_v
                dseg_ref[rw, ln] = dlin.astype(BF16)
            return carry

        lax.fori_loop(0, tm // slab, conv_back, 0)
        dcbuf[tm:tm + 8, :] = dcbuf[0:8, :]
        for n in range(5):
            sm_ref[n:n + 1, :] += jnp.sum(conv_sums[n], axis=0, keepdims=True)
        sm_ref[5:6, :] += jnp.sum(sums[1], axis=0, keepdims=True)
        sm_ref[6:7, :] += jnp.sum(sums[2], axis=0, keepdims=True)
        sm_ref[7:8, :] += jnp.sum(sums[0], axis=0, keepdims=True) * (LRU_C * _sigmoid(-lam_v))

        @pl.when(step == nt - 1)
        def _():
            push.wait()

    rowb = pl.BlockSpec((tm, D_MODEL), lambda s: (nt - 1 - s, 0))
    t16 = tm // 16
    prev8 = pl.BlockSpec((16, D_MODEL), lambda s: (jnp.maximum((nt - 1 - s) * t16 - 1, 0), 0))
    seg = lambda k: pl.BlockSpec((tm, D_MODEL), lambda s, k=k: (nt - 1 - s, k))
    vec = pl.BlockSpec((1, D_MODEL), lambda s: (0, 0))
    mat = pl.BlockSpec((LRU_BLOCKS, LRU_BLOCK, LRU_BLOCK), lambda s: (0, 0, 0))
    mshape = jax.ShapeDtypeStruct((LRU_BLOCKS, LRU_BLOCK, LRU_BLOCK), F32)
    n_in = 10
    return pl.pallas_call(
        body, name="lru_bwd", grid=(nt,),
        in_specs=[rowb, rowb, prev8, pl.BlockSpec((3, tm, D_MODEL), lambda s: (0, nt - 1 - s, 0)), seg(4), seg(5),
                  pl.BlockSpec((4, D_MODEL), lambda s: (0, 0)), vec, mat, mat, ANY] + [ANY] * n_s1,
        out_specs=[pl.BlockSpec((tm, 2 * D_MODEL), lambda s: (nt - 1 - s, 2)), mat, mat,
                   pl.BlockSpec((8, D_MODEL), lambda s: (0, 0))] + [ANY] * n_s1,
        out_shape=[jax.ShapeDtypeStruct(dproj.shape, dproj.dtype), mshape, mshape,
                   jax.ShapeDtypeStruct((8, D_MODEL), F32)] + S1_SHAPES,
        input_output_aliases={n_in: 0},
        scratch_shapes=[pltpu.VMEM((tm + 8, D_MODEL), F32), pltpu.VMEM((tm, D_MODEL), F32),
                        pltpu.VMEM((tm, D_MODEL), F32), pltpu.VMEM((tm, D_MODEL), F32),
                        pltpu.VMEM((tm, D_MODEL), F32), pltpu.VMEM((tm + 8, D_MODEL), F32),
                        pltpu.VMEM((tm, D_MODEL), BF16), pltpu.VMEM((tm, D_MODEL), BF16),
                        pltpu.VMEM((3, 16, D_MODEL), F32), pltpu.VMEM((5, 16, D_MODEL), F32),
                        pltpu.VMEM((1, D_MODEL), F32), pltpu.VMEM((1, D_MODEL), F32)] + _push_sems(n_s1),
        compiler_params=pltpu.CompilerParams(dimension_semantics=("arbitrary",), vmem_limit_bytes=VMEM_LIMIT,
                                             has_side_effects=True),
    )(dzl, hs, hs, cri, proj, proj, conv_w, lam, wa_g, wx_g, dproj, *s1_grads)


def _retention_bwd(dzr, o, proj, states, cos2, sin2, dec, dproj, ride):
    rows = dzr.shape[0]
    n_chunks = rows // CHUNK
    per_step = _chunks_per_step(n_chunks)
    n_steps = n_chunks // per_step
    tm = per_step * CHUNK
    n_r = ride.n

    def body(dzr_ref, o_ref, q_ref, k_ref, v_ref, g_ref, st_ref, c_ref, s_ref, dec_ref, dproj_in, *refs):
        del dproj_in
        dseg_ref = refs[n_r]
        dstate = refs[2 * n_r + 1]
        push = ride.push(refs[:n_r], refs[n_r + 1:2 * n_r + 1], refs[2 * n_r + 2:])

        @pl.when(pl.program_id(0) == 0)
        def _():
            push.start()
            dstate[...] = jnp.zeros_like(dstate)

        for h in range(HEADS):
            sl = slice(HEAD_DIM * h, HEAD_DIM * (h + 1))
            intra, qd, kd, cd = dec_ref[0, h], dec_ref[1, h], dec_ref[2, h], dec_ref[3, h]
            dst = dstate[h]
            for c in reversed(range(per_step)):
                rw = slice(CHUNK * c, CHUNK * (c + 1))
                cos_t, sin_t = c_ref[rw, :], s_ref[rw, :]
                o = o_ref[rw, sl].astype(F32)
                g = g_ref[rw, sl].astype(F32)
                dzr_v = dzr_ref[rw, sl].astype(F32)
                sg = _sigmoid(g)
                r = lax.rsqrt(jnp.mean(o * o, axis=-1, keepdims=True) + NORM_EPS)
                on = o * r
                dseg_ref[rw, 3 * D_MODEL + HEAD_DIM * h:3 * D_MODEL + HEAD_DIM * (h + 1)] = (
                    dzr_v * on * (sg * (1.0 + g * (1.0 - sg)))).astype(BF16)
                don = dzr_v * (g * sg)
                do = r * (don - on * jnp.mean(don * on, axis=-1, keepdims=True))
                dob = do.astype(BF16)

                qh = _rot(q_ref[rw, sl].astype(F32), cos_t, sin_t)
                kh = _rot(k_ref[rw, sl].astype(F32), cos_t, sin_t) * QK_SCALE
                qb, kb, vb = qh.astype(BF16), kh.astype(BF16), v_ref[rw, sl]
                s = (_dot_nt(qb, kb) * intra).astype(BF16)
                ds = (_dot_nt(dob, vb) * intra).astype(BF16)
                st_b = st_ref[c, h].astype(BF16)
                dst_b = dst.astype(BF16)
                dv = _dot_tn(s, dob) + _dot((kh * kd).astype(BF16), dst_b)
                dq = _dot(ds, kb) + _dot_nt(dob, st_b) * qd
                dk = _dot_tn(ds, qb) + _dot_nt(vb, dst_b) * kd
                dst = dst * cd + _dot_tn((qh * qd).astype(BF16), dob)
                dseg_ref[rw, 2 * D_MODEL + HEAD_DIM * h:2 * D_MODEL + HEAD_DIM * (h + 1)] = dv.astype(BF16)
                dseg_ref[rw, sl] = _rot_t(dq, cos_t, sin_t).astype(BF16)
                dseg_ref[rw, D_MODEL + HEAD_DIM * h:D_MODEL + HEAD_DIM * (h + 1)] = (
                    _rot_t(dk, cos_t, sin_t) * QK_SCALE).astype(BF16)
            dstate[h] = dst

        @pl.when(pl.program_id(0) == n_steps - 1)
        def _():
            push.wait()

    rev = lambda s: n_steps - 1 - s
    rowb = pl.BlockSpec((tm, D_MODEL), lambda s: (rev(s), 0))
    seg = lambda k: pl.BlockSpec((tm, D_MODEL), lambda s, k=k: (rev(s), k))
    tab = pl.BlockSpec((tm, HEAD_DIM), lambda s: (rev(s), 0))
    return pl.pallas_call(
        body, name="retention_bwd", grid=(n_steps,),
        in_specs=[rowb, rowb, seg(0), seg(1), seg(2), seg(3),
                  pl.BlockSpec((per_step, HEADS, HEAD_DIM, HEAD_DIM), lambda s: (rev(s), 0, 0, 0)), tab, tab,
                  pl.BlockSpec((4, HEADS, CHUNK, CHUNK), lambda s: (0, 0, 0, 0)), ANY] + ride.specs(),
        out_specs=[pl.BlockSpec((tm, 4 * D_MODEL), lambda s: (rev(s), 0))] + ride.specs(),
        out_shape=[jax.ShapeDtypeStruct(dproj.shape, dproj.dtype)] + ride.out_shapes,
        input_output_aliases={10: 0},
        scratch_shapes=[pltpu.VMEM((HEADS, HEAD_DIM, HEAD_DIM), F32)] + ride.scratch(),
        compiler_params=pltpu.CompilerParams(dimension_semantics=("arbitrary",), vmem_limit_bytes=VMEM_LIMIT,
                                             has_side_effects=True),
    )(dzr, o, proj, proj, proj, proj, states, cos2, sin2, dec, dproj, *ride.arrays)


S2_SHAPES = [
    jax.ShapeDtypeStruct((N_DEV, LRU_BLOCKS, LRU_ROWS, LRU_BLOCK), F32),
    jax.ShapeDtypeStruct((N_DEV, LRU_BLOCKS, LRU_ROWS, LRU_BLOCK), F32),
]


def _s2_parts(ins, p):
    return [r.at[p] for r in ins]


def _in_proj_bwd(dproj, win_g, h0, norm_w, dh1, d_win_far, s2_grads, pack_early):
    rows = h0.shape[0]
    tm = _tile(rows, 320)
    n_i = rows // tm
    n_s2 = len(s2_grads)
    n_far = len(WIN_FAR)
    pack_rows = pack_early.shape[0]

    def body(dseg_ref, w_ref, h0_ref, nw_ref, dh1_ref, far_ref, early_ref, *refs):
        s2_refs = refs[:n_s2]
        dh0_ref, dw_ref, far_land = refs[n_s2:n_s2 + 3]
        land_refs = refs[n_s2 + 3:2 * n_s2 + 3]
        early_land, late_land = refs[2 * n_s2 + 3:2 * n_s2 + 5]
        (send_sems, recv_sems, loc_sems, far_send_sems, far_recv_sems, late_buf) = refs[2 * n_s2 + 5:2 * n_s2 + 11]
        early_sems, late_sems = refs[2 * n_s2 + 11:2 * n_s2 + 14], refs[2 * n_s2 + 14:]
        i = pl.program_id(0)
        push = _Push(lambda p: _s2_parts(s2_refs, p), lambda s: [r.at[s] for r in land_refs],
                     (send_sems, recv_sems, loc_sems), n_s2)
        early = _Push(lambda p: [early_ref], lambda s: [early_land.at[s]], tuple(early_sems), 1)
        late = _Push(lambda p: [late_buf], lambda s: [late_land.at[s]], tuple(late_sems), 1)

        def far_copy(n):
            return pltpu.make_async_remote_copy(src_ref=far_ref.at[n], dst_ref=far_land.at[n],
                                                send_sem=far_send_sems.at[n], recv_sem=far_recv_sems.at[n],
                                                device_id=_peer(WIN_FAR[n])[0], device_id_type=MESH_ID)

        @pl.when(i == 0)
        def _():
            for n in range(n_far):
                far_copy(n).start()
            push.start()
            early.start()
            dw_ref[...] = jnp.zeros_like(dw_ref)

        du = _dot_nt(dseg_ref[:, 0:D_MODEL], w_ref[0])
        for j in range(1, N_DEV):
            du = du + _dot_nt(dseg_ref[:, D_MODEL * j:D_MODEL * (j + 1)], w_ref[j])
        dx, dw = _rms_bwd(h0_ref[...], nw_ref[...], du)
        dw_ref[0:1, :] += dw
        dh0 = dh1_ref[...] + dx
        dh0_ref[...] = dh0

        @pl.when(i == 0)
        def _():
            late_buf[8:8 + N_META, :] = dh0[PAD_ROWS:CHUNK, :]

        @pl.when(i == n_i - 1)
        def _():
            late_buf[0:8, :] = dw_ref[...]
            late.start()
            for n in range(n_far):
                far_copy(n).wait_recv()
            for n in range(n_far):
                far_copy(n).wait_send()
            push.wait()
            early.wait()
            late.wait()

    row = pl.BlockSpec((tm, D_MODEL), lambda i: (i, 0))
    vec = pl.BlockSpec((1, D_MODEL), lambda i: (0, 0))
    return pl.pallas_call(
        body, name="in_proj_bwd", grid=(n_i,),
        in_specs=[pl.BlockSpec((tm, N_DEV * D_MODEL), lambda i: (i, 0)),
                  pl.BlockSpec((N_DEV, D_MODEL, D_MODEL), lambda i: (0, 0, 0), pipeline_mode=pl.Buffered(1)),
                  row, vec, row, ANY, ANY] + [ANY] * n_s2,
        out_specs=[row, pl.BlockSpec((8, D_MODEL), lambda i: (0, 0)), ANY] + [ANY] * n_s2 + [ANY, ANY],
        out_shape=[jax.ShapeDtypeStruct((rows, D_MODEL), F32), jax.ShapeDtypeStruct((8, D_MODEL), F32),
                   jax.ShapeDtypeStruct((n_far, D_MODEL, D_MODEL), BF16)] + S2_SHAPES
        + [jax.ShapeDtypeStruct((N_DEV, pack_rows, D_MODEL), F32)] * 2,
        scratch_shapes=_push_sems(n_s2) + [pltpu.SemaphoreType.DMA((n_far,)), pltpu.SemaphoreType.DMA((n_far,)),
                                           pltpu.VMEM((pack_rows, D_MODEL), F32)] + _push_sems(1) + _push_sems(1),
        compiler_params=pltpu.CompilerParams(dimension_semantics=("arbitrary",),
                                             vmem_limit_bytes=VMEM_LIMIT, has_side_effects=True),
    )(dproj, win_g, h0, norm_w, dh1, d_win_far, pack_early, *s2_grads)


def _adamw(g_slots, w, m, v, more_slots=None):
    slots, rows, cols = g_slots.shape
    extra = [] if more_slots is None else [more_slots]
    tr = rows
    for cand in (256, 128, 64, 32, 16, 8):
        if rows % cand == 0 and rows > cand:
            tr = cand
            break

    def body(g_ref, *refs):
        w_ref, m_ref, v_ref, go_ref, d_ref, mo_ref, vo_ref = refs[len(extra):]
        g = g_ref[0].astype(F32)
        for s in range(1, slots):
            g = g + g_ref[s].astype(F32)
        for more_ref in refs[:len(extra)]:
            for s in range(more_ref.shape[0]):
                g = g + more_ref[s].astype(F32)
        m2 = ADAM_B1 * m_ref[...] + (1.0 - ADAM_B1) * g
        v2 = ADAM_B2 * v_ref[...] + (1.0 - ADAM_B2) * (g * g)
        m_hat = m2 / (1.0 - ADAM_B1 ** ADAM_STEP)
        v_hat = v2 / (1.0 - ADAM_B2 ** ADAM_STEP)
        go_ref[...] = g
        d_ref[...] = -ADAM_LR * (m_hat / (jnp.sqrt(v_hat) + ADAM_EPS) + ADAM_WD * w_ref[...])
        mo_ref[...] = m2
        vo_ref[...] = v2

    blk = pl.BlockSpec((tr, cols), lambda i: (i, 0))
    shape = jax.ShapeDtypeStruct((rows, cols), F32)
    return pl.pallas_call(
        body, name="adamw", grid=(rows // tr,),
        in_specs=[pl.BlockSpec((slots, tr, cols), lambda i: (0, i, 0))]
        + [pl.BlockSpec((t.shape[0], tr, cols), lambda i: (0, i, 0)) for t in extra] + [blk, blk, blk],
        out_specs=[blk] * 4, out_shape=[shape] * 4,
        compiler_params=_cparams(("parallel",)),
    )(g_slots, *extra, w, m, v)


def _sum_slots(packs):
    slots, rows, cols = packs.shape

    def body(p_ref, o_ref):
        acc = p_ref[0]
        for s in range(1, slots):
            acc = acc + p_ref[s]
        o_ref[...] = acc

    return pl.pallas_call(
        body, name="sum_slots", out_shape=jax.ShapeDtypeStruct((rows, cols), F32),
        compiler_params=pltpu.CompilerParams(vmem_limit_bytes=VMEM_LIMIT),
    )(packs)


def _gather_small(small):
    shapes = [jax.ShapeDtypeStruct((N_DEV,) + small.shape, F32)]
    return _push_call("gather_small", [small], shapes,
                      lambda ins, p: list(ins), lambda outs, s: [r.at[s] for r in outs])[0]


PACK_CONV_W, PACK_CONV_B, PACK_BA, PACK_BX, PACK_LAM = 0, 4, 5, 6, 7
PACK_FFN_NORM, PACK_SQ_ERR, PACK_FINAL_NORM, PACK_MIX_NORM, PACK_META = 8, 16, 17, 24, 32


def kernel(x, meta_tokens, mix_norm_w, w_in, conv_w, conv_b, lru_wa, lru_ba, lru_wx, lru_bx, lru_lambda, w_branch_ret, w_branch_lru, w_out, ffn_norm_w, w_ffn_in, w_ffn_out, final_norm_w, loss_target, m_meta_tokens, m_mix_norm_w, m_w_in, m_conv_w, m_conv_b, m_lru_wa, m_lru_ba, m_lru_wx, m_lru_bx, m_lru_lambda, m_w_branch_ret, m_w_branch_lru, m_w_out, m_ffn_norm_w, m_w_ffn_in, m_w_ffn_out, m_final_norm_w, v_meta_tokens, v_mix_norm_w, v_w_in, v_conv_w, v_conv_b, v_lru_wa, v_lru_ba, v_lru_wx, v_lru_bx, v_lru_lambda, v_w_branch_ret, v_w_branch_lru, v_w_out, v_ffn_norm_w, v_w_ffn_in, v_w_ffn_out, v_final_norm_w):
    me = _my_index()
    pad4 = ((0, 4), (0, 0))
    fw = final_norm_w.reshape(1, D_MODEL)

    small = jnp.concatenate([meta_tokens, jnp.pad(conv_w[0], pad4)], axis=0)
    small_g = _gather_small(small)
    meta_full = small_g[:, :N_META].transpose(1, 0, 2).reshape(N_META, D_MODEL)
    conv_w_full = small_g[:, N_META:N_META + 4].transpose(1, 0, 2).reshape(4, D_MODEL)
    mixer_shards = [w_branch_ret[0].astype(BF16), w_branch_lru[0].astype(BF16), w_out[0].astype(BF16),
                    lru_wa[0].astype(BF16), lru_wx[0].astype(BF16)]
    wfi_shard = jnp.pad(w_ffn_in[0].astype(BF16), ((0, 0), (0, FFN_GROUP - FFN_SHARD)))
    own_slot = lambda ins, p: list(ins)
    part_of_owner = lambda ins, p: [r.at[p] for r in ins]

    rows = x.shape[1] + CHUNK
    h0 = jnp.concatenate([jnp.zeros((PAD_ROWS, D_MODEL), F32), meta_full, x[0]], axis=0)
    cos2, sin2 = _rope_tables(rows)
    dec = _retention_consts()

    me_arr = me.astype(jnp.int32).reshape(1)
    proj, u, win_g = _in_proj(h0, mix_norm_w, w_in[0].astype(BF16), me_arr)
    o, zr, states, wbr_g, wbl_g, wout_g, wa_g, wx_g = _retention_fwd(
        proj, cos2, sin2, dec, _mixer_weights_ride(mixer_shards))
    wbr, wbl, wout = (t.reshape(D_MODEL, D_MODEL) for t in (wbr_g, wbl_g, wout_g))
    wa_g, wx_g = _from_owners(wa_g), _from_owners(wx_g)
    gather_wfi = _Ride([wfi_shard], [jax.ShapeDtypeStruct((N_DEV, D_MODEL, FFN_GROUP), BF16)],
                       own_slot, _slot_of_sender, gather_by_chip=True)
    hs, zl, cri, wfi_g = _lru_fwd(proj, conv_w_full, conv_b, lru_ba, lru_bx, lru_lambda, wa_g, wx_g, gather_wfi)
    h1, yr, yl, mixed, wfo_g = _mix_fwd(zr, zl, proj, h0, wbr, wbl, wout, _wfo_ride(w_ffn_out[0].astype(BF16)))
    u2, g, up, act, dh2, red = _ffn_fwd_loss(h1, ffn_norm_w, wfi_g, wfo_g, fw, loss_target[0])

    d_wfo = _wgrad(act, dh2, FFN_GROUP, D_MODEL, BF16)[:, 0]
    dgu, dh1, dw_ffn_norm = _ffn_bwd(dh2, g, up, h1, ffn_norm_w, wfi_g, wfo_g)
    d_wfi = _wgrad(u2, dgu, D_MODEL, FFN_GROUP, BF16, b_halves=True)[0]
    d_wout = _wgrad(mixed, dh1, D_MODEL, D_MODEL, BF16)[0, 0]
    dyr, dyl, dproj, dzr, dzl = _mix_bwd(dh1, yr, yl, proj, wbr, wbl, wout)
    d_wbr = _wgrad(zr, dyr, D_MODEL, D_MODEL, BF16)[0, 0]
    d_wbl = _wgrad(zl, dyl, D_MODEL, D_MODEL, BF16)[0, 0]
    dproj, d_wa, d_wx, lru_small, r_fi, r_fo = _lru_bwd(
        dzl, hs, cri, proj, dproj, conv_w_full, lru_lambda, wa_g, wx_g, [d_wfi, d_wfo])
    mix_shape = jax.ShapeDtypeStruct((N_DEV, D_MODEL // N_DEV, D_MODEL), BF16)
    scatter_mix = _Ride([t.reshape(mix_shape.shape) for t in (d_wbr, d_wbl, d_wout)], [mix_shape] * 3,
                        part_of_owner, _slot_of_sender)
    dproj, r_br, r_bl, r_out = _retention_bwd(dzr, o, proj, states, cos2, sin2, dec, dproj, scatter_mix)
    d_win_far, r_in = _wgrad_w_in(u, dproj, me_arr)
    pack_early = jnp.concatenate([lru_small, dw_ffn_norm, red], axis=0)
    dh0, _, r_in_far, r_wa, r_wx, packs_early, packs_late = _in_proj_bwd(
        dproj, win_g, h0, mix_norm_w, dh1, d_win_far, [_by_owner(d_wa), _by_owner(d_wx)], pack_early)
    grad_x = dh0[CHUNK:]

    small_sum = jnp.concatenate([_sum_slots(packs_early), _sum_slots(packs_late)], axis=0)
    loss = (0.5 / D_MODEL) * jnp.sum(small_sum[PACK_SQ_ERR])

    def big_update(slots, w, m, v, more_slots=None):
        shape = w.shape
        w2, m2, v2 = (t.reshape(slots.shape[1:]) for t in (w, m, v))
        return [t.reshape(shape) for t in _adamw(slots, w2, m2, v2, more_slots)]

    res = {}
    res["w_in"] = big_update(r_in, w_in, m_w_in, v_w_in, r_in_far)
    res["w_branch_ret"] = big_update(r_br, w_branch_ret, m_w_branch_ret, v_w_branch_ret)
    res["w_branch_lru"] = big_update(r_bl, w_branch_lru, m_w_branch_lru, v_w_branch_lru)
    res["w_out"] = big_update(r_out, w_out, m_w_out, v_w_out)
    res["w_ffn_in"] = big_update(r_fi[:, :, :FFN_SHARD], w_ffn_in, m_w_ffn_in, v_w_ffn_in)
    res["w_ffn_out"] = big_update(r_fo, w_ffn_out, m_w_ffn_out, v_w_ffn_out)
    res["lru_wa"] = big_update(r_wa.reshape(N_DEV, LRU_BLOCKS * LRU_ROWS, LRU_BLOCK), lru_wa, m_lru_wa, v_lru_wa)
    res["lru_wx"] = big_update(r_wx.reshape(N_DEV, LRU_BLOCKS * LRU_ROWS, LRU_BLOCK), lru_wx, m_lru_wx, v_lru_wx)

    col = me * HEAD_DIM
    g_meta = lax.dynamic_slice(small_sum, (PACK_META, col), (N_META, HEAD_DIM))
    g_conv = lax.dynamic_slice(small_sum, (PACK_CONV_W, col), (8, HEAD_DIM))
    small_names = ["mix_norm_w", "conv_b", "lru_ba", "lru_bx", "lru_lambda", "ffn_norm_w", "final_norm_w"]
    small_rows = [PACK_MIX_NORM, PACK_CONV_B, PACK_BA, PACK_BX, PACK_LAM, PACK_FFN_NORM, PACK_FINAL_NORM]
    small_w = [mix_norm_w, conv_b, lru_ba, lru_bx, lru_lambda, ffn_norm_w, fw]
    small_m = [m_mix_norm_w, m_conv_b, m_lru_ba, m_lru_bx, m_lru_lambda, m_ffn_norm_w, m_final_norm_w.reshape(1, -1)]
    small_v = [v_mix_norm_w, v_conv_b, v_lru_ba, v_lru_bx, v_lru_lambda, v_ffn_norm_w, v_final_norm_w.reshape(1, -1)]

    def pack_small(vec_list, meta_t, conv_t):
        return jnp.concatenate([t.reshape(8, HEAD_DIM) for t in vec_list] + [meta_t, jnp.pad(conv_t[0], pad4)], axis=0)

    g_small = jnp.concatenate([small_sum[r].reshape(8, HEAD_DIM) for r in small_rows] + [g_meta, g_conv], axis=0)
    outs_small = _adamw(g_small[None], pack_small(small_w, meta_tokens, conv_w),
                        pack_small(small_m, m_meta_tokens, m_conv_w), pack_small(small_v, v_meta_tokens, v_conv_w))
    for idx, name in enumerate(small_names):
        shape = final_norm_w.shape if name == "final_norm_w" else (1, D_MODEL)
        res[name] = [t[8 * idx:8 * idx + 8].reshape(shape) for t in outs_small]
    res["meta_tokens"] = [t[56:72] for t in outs_small]
    res["conv_w"] = [t[72:76].reshape(1, 4, HEAD_DIM) for t in outs_small]

    order = ["meta_tokens", "mix_norm_w", "w_in", "conv_w", "conv_b", "lru_wa", "lru_ba", "lru_wx", "lru_bx",
             "lru_lambda", "w_branch_ret", "w_branch_lru", "w_out", "ffn_norm_w", "w_ffn_in", "w_ffn_out",
             "final_norm_w"]
    out = [loss, grad_x[None]]
    for kind in range(4):
        out += [res[name][kind] for name in order]
    return tuple(out)
```

```python
import jax
import jax.numpy as jnp
from jax import lax
from jax.experimental import pallas as pl
from jax.experimental.pallas import tpu as pltpu

F32 = jnp.float32
BF16 = jnp.bfloat16

D_MODEL = 1024
N_META = 16
CHUNK = 128
PAD_ROWS = CHUNK - N_META
HEADS = 8
HEAD_DIM = 128
ROPE_BASE = 10000.0
QK_SCALE = HEAD_DIM ** -0.5
LRU_BLOCKS = 4
LRU_BLOCK = 256
LRU_C = 8.0
FFN_HIDDEN = 2816
N_DEV = 8
FFN_SHARD = 2 * FFN_HIDDEN // N_DEV
FFN_GROUP = 768
FFN_GROUPS = 4
FFN_OUT_SHARD = FFN_HIDDEN // N_DEV
NORM_EPS = 1e-6

ADAM_LR = 0.001
ADAM_B1 = 0.9
ADAM_B2 = 0.999
ADAM_EPS = 1e-08
ADAM_WD = 0.01
ADAM_STEP = 10

VMEM_LIMIT = 56 * 1024 * 1024
MESH_ID = pl.DeviceIdType.MESH
ANY = pl.BlockSpec(memory_space=pl.ANY)


def _cparams(sem):
    return pltpu.CompilerParams(dimension_semantics=sem, vmem_limit_bytes=VMEM_LIMIT)


def _tile(rows, cap):
    t = cap - cap % 64
    while rows % t:
        t -= 64
    return t


def _dot(a, b):
    return jnp.dot(a, b, preferred_element_type=F32)


def _dot_nt(a, b):
    return lax.dot_general(a, b, (((1,), (1,)), ((), ())), preferred_element_type=F32)


def _dot_tn(a, b):
    return lax.dot_general(a, b, (((0,), (0,)), ((), ())), preferred_element_type=F32)


def _sigmoid(x):
    return 0.5 * jnp.tanh(0.5 * x) + 0.5


def _gelu_parts(x):
    k = 0.7978845608028654
    inner = k * (x + 0.044715 * x * x * x)
    t = jnp.tanh(inner)
    g = 0.5 * x * (1.0 + t)
    dg = 0.5 * (1.0 + t) + 0.5 * x * (1.0 - t * t) * k * (1.0 + 3.0 * 0.044715 * x * x)
    return g, dg


def _rot(x, cos2, sin2):
    return x * cos2 + pltpu.roll(x, HEAD_DIM // 2, 1) * sin2


def _rot_t(dx, cos2, sin2):
    return dx * cos2 - pltpu.roll(dx, HEAD_DIM // 2, 1) * sin2


def _rms_bwd(x, w, dy):
    rs = lax.rsqrt(jnp.mean(x * x, axis=-1, keepdims=True) + NORM_EPS)
    nh = x * rs
    dw = jnp.sum(dy * nh, axis=0, keepdims=True)
    dn = dy * w
    dx = rs * (dn - nh * jnp.mean(dn * nh, axis=-1, keepdims=True))
    return dx, dw


def _retention_consts():
    h = jnp.arange(HEADS, dtype=F32)
    log_g = jnp.log(1.0 - 2.0 ** (-5.0 - h))
    idx = jnp.arange(CHUNK, dtype=F32)
    diff = idx[:, None] - idx[None, :]
    intra = jnp.where(diff[None] >= 0, jnp.exp(jnp.maximum(diff, 0.0)[None] * log_g[:, None, None]), 0.0)
    q_decay = jnp.exp((idx + 1.0)[:, None] * log_g[None, :])
    k_decay = jnp.exp((CHUNK - 1.0 - idx)[:, None] * log_g[None, :])
    chunk_decay = jnp.exp(CHUNK * log_g)
    shape = (HEADS, CHUNK, CHUNK)
    qd = jnp.broadcast_to(q_decay.T[:, :, None], shape)
    kd = jnp.broadcast_to(k_decay.T[:, :, None], shape)
    cd = jnp.broadcast_to(chunk_decay[:, None, None], shape)
    return jnp.stack([intra, qd, kd, cd])


def _rope_tables(rows):
    pos = jnp.maximum(jnp.arange(rows) - PAD_ROWS, 0).astype(F32)
    inv_freq = ROPE_BASE ** (-jnp.arange(0, HEAD_DIM, 2, dtype=F32) / HEAD_DIM)
    ang = pos[:, None] * inv_freq[None, :]
    cos, sin = jnp.cos(ang), jnp.sin(ang)
    return jnp.concatenate([cos, cos], axis=1), jnp.concatenate([-sin, sin], axis=1)


def _my_index():
    return 4 * lax.axis_index("x") + 2 * lax.axis_index("y") + lax.axis_index("c")


def _peer(k):
    x, y, c = lax.axis_index("x"), lax.axis_index("y"), lax.axis_index("c")
    px = 1 - x if k & 4 else x
    py = 1 - y if k & 2 else y
    pc = 1 - c if k & 1 else c
    return (px, py, pc), 4 * px + 2 * py + pc


def _push_sems(n_arr):
    n_rem = (N_DEV - 1) * n_arr
    return [pltpu.SemaphoreType.DMA((n_rem,)), pltpu.SemaphoreType.DMA((n_rem,)), pltpu.SemaphoreType.DMA((n_arr,))]


class _Push:
    def __init__(self, send_part, land_slot, sems, n_arr):
        self.send_part, self.land_slot, self.n_arr = send_part, land_slot, n_arr
        self.send_sems, self.recv_sems, self.loc_sems = sems

    def _remote(self, k, a, src, dst, pos):
        idx = (k - 1) * self.n_arr + a
        return pltpu.make_async_remote_copy(src_ref=src, dst_ref=dst, send_sem=self.send_sems.at[idx],
                                            recv_sem=self.recv_sems.at[idx], device_id=pos, device_id_type=MESH_ID)

    def _outgoing(self):
        me = _my_index()
        land = self.land_slot(me)
        remote = []
        for k in range(1, N_DEV):
            pos, p = _peer(k)
            src = self.send_part(p)
            remote += [self._remote(k, a, src[a], land[a], pos) for a in range(self.n_arr)]
        own = self.send_part(me)
        local = [pltpu.make_async_copy(own[a], land[a], self.loc_sems.at[a]) for a in range(self.n_arr)]
        return remote, local

    def start(self):
        remote, local = self._outgoing()
        for cp in remote + local:
            cp.start()

    def wait_recv_from(self, k):
        own = self.send_part(_my_index())
        pos, p = _peer(k)
        land = self.land_slot(p)
        for a in range(self.n_arr):
            self._remote(k, a, own[a], land[a], pos).wait_recv()

    def wait_sends(self):
        remote, local = self._outgoing()
        for cp in remote:
            cp.wait_send()
        for cp in local:
            cp.wait()

    def wait(self):
        for k in range(1, N_DEV):
            self.wait_recv_from(k)
        self.wait_sends()


DIRECT = (1, 2, 4, 6)
RELAYED = (2, 4, 6)


def _gather_by_chip_sems(n_arr):
    direct, relayed = len(DIRECT) * n_arr, len(RELAYED) * n_arr
    return [pltpu.SemaphoreType.DMA((direct,)), pltpu.SemaphoreType.DMA((direct,)),
            pltpu.SemaphoreType.DMA((relayed,)), pltpu.SemaphoreType.DMA((relayed,)), pltpu.SemaphoreType.DMA((n_arr,))]


class _GatherByChip:
    def __init__(self, srcs, land_slot, sems, n_arr):
        self.srcs, self.land_slot, self.n_arr = srcs, land_slot, n_arr
        self.send_sems, self.recv_sems, self.relay_send_sems, self.relay_recv_sems, self.loc_sems = sems

    def _direct(self, k, a, slot):
        idx = DIRECT.index(k) * self.n_arr + a
        return pltpu.make_async_remote_copy(src_ref=self.srcs[a], dst_ref=self.land_slot(slot)[a],
                                            send_sem=self.send_sems.at[idx], recv_sem=self.recv_sems.at[idx],
                                            device_id=_peer(k)[0], device_id_type=MESH_ID)

    def _relay(self, q, a, slot):
        idx = RELAYED.index(q) * self.n_arr + a
        block = self.land_slot(slot)[a]
        return pltpu.make_async_remote_copy(src_ref=block, dst_ref=block, send_sem=self.relay_send_sems.at[idx],
                                            recv_sem=self.relay_recv_sems.at[idx], device_id=_peer(1)[0],
                                            device_id_type=MESH_ID)

    def _own(self, a):
        return pltpu.make_async_copy(self.srcs[a], self.land_slot(_my_index())[a], self.loc_sems.at[a])

    def start(self):
        me = _my_index()
        for k in DIRECT:
            for a in range(self.n_arr):
                self._direct(k, a, me).start()
        for a in range(self.n_arr):
            self._own(a).start()

    def relay(self):
        for q in RELAYED:
            p = _peer(q)[1]
            for a in range(self.n_arr):
                self._direct(q, a, p).wait_recv()
                self._relay(q, a, p).start()

    def wait(self):
        me = _my_index()
        for a in range(self.n_arr):
            self._direct(1, a, _peer(1)[1]).wait_recv()
        for q in RELAYED:
            for a in range(self.n_arr):
                self._relay(q, a, _peer(q + 1)[1]).wait_recv()
        for k in DIRECT:
            for a in range(self.n_arr):
                self._direct(k, a, me).wait_send()
        for q in RELAYED:
            for a in range(self.n_arr):
                self._relay(q, a, _peer(q)[1]).wait_send()
        for a in range(self.n_arr):
            self._own(a).wait()


class _Ride:
    def __init__(self, arrays, out_shapes, send_part, land_slot, zero_dsts=None, zero_shape=None, n_zero=0,
                 gather_by_chip=False):
        self.arrays, self.out_shapes = list(arrays), list(out_shapes)
        self.send_part, self.land_slot, self.n = send_part, land_slot, len(arrays)
        self.zero_dsts, self.zero_shape, self.n_zero = zero_dsts, zero_shape, n_zero
        self.gather_by_chip = gather_by_chip

    def specs(self):
        return [ANY] * self.n

    def scratch(self):
        extra = [pltpu.SemaphoreType.DMA((self.n_zero,)), pltpu.VMEM(self.zero_shape, BF16)] if self.n_zero else []
        sems = _gather_by_chip_sems(self.n) if self.gather_by_chip else _push_sems(self.n)
        return sems + extra

    def push(self, in_refs, out_refs, scratch):
        ride = self
        n_sems = 5 if self.gather_by_chip else 3
        land = lambda s: ride.land_slot(out_refs, s)
        if self.gather_by_chip:
            push = _GatherByChip(list(in_refs), land, tuple(scratch[:n_sems]), self.n)
        else:
            push = _Push(lambda p: ride.send_part(in_refs, p), land, tuple(scratch[:n_sems]), self.n)

        class Both:
            def _fills(self):
                if not ride.n_zero:
                    return []
                zsems, zbuf = scratch[n_sems], scratch[n_sems + 1]
                return [pltpu.make_async_copy(zbuf, dst, zsems.at[z]) for z, dst in enumerate(ride.zero_dsts(out_refs))]

            def start(self):
                push.start()
                if ride.n_zero:
                    scratch[n_sems + 1][...] = jnp.zeros(ride.zero_shape, BF16)
                for cp in self._fills():
                    cp.start()

            def relay(self):
                if ride.gather_by_chip:
                    push.relay()

            def wait(self):
                push.wait()
                for cp in self._fills():
                    cp.wait()

        return Both()


def _slot_of_sender(out_refs, s):
    return [r.at[s] for r in out_refs]


def _push_call(name, arrays, out_shapes, send_part, land_slot):
    n_arr = len(arrays)

    def body(*refs):
        ins, outs, sems = refs[:n_arr], refs[n_arr:2 * n_arr], refs[2 * n_arr:]
        push = _Push(lambda p: send_part(ins, p), lambda s: land_slot(outs, s), sems, n_arr)
        push.start()
        push.wait()

    return pl.pallas_call(
        body, name=name, in_specs=[ANY] * n_arr, out_specs=[ANY] * n_arr, out_shape=out_shapes,
        scratch_shapes=_push_sems(n_arr), compiler_params=pltpu.CompilerParams(has_side_effects=True),
    )(*arrays)


LRU_ROWS = LRU_BLOCK // N_DEV
FFN_PAD_ROWS = FFN_GROUP - 2 * FFN_OUT_SHARD


def _half_rows(d):
    return pl.ds(pl.multiple_of((d % 2) * FFN_OUT_SHARD, 16), FFN_OUT_SHARD)


MIXER_SHAPES = [
    jax.ShapeDtypeStruct((N_DEV, D_MODEL // N_DEV, D_MODEL), BF16),
    jax.ShapeDtypeStruct((N_DEV, D_MODEL // N_DEV, D_MODEL), BF16),
    jax.ShapeDtypeStruct((N_DEV, D_MODEL // N_DEV, D_MODEL), BF16),
    jax.ShapeDtypeStruct((N_DEV, LRU_BLOCKS, LRU_ROWS, LRU_BLOCK), BF16),
    jax.ShapeDtypeStruct((N_DEV, LRU_BLOCKS, LRU_ROWS, LRU_BLOCK), BF16),
]


def _by_owner(t):
    return t.reshape(LRU_BLOCKS, N_DEV, LRU_ROWS, LRU_BLOCK).transpose(1, 0, 2, 3)


def _from_owners(t):
    return t.transpose(1, 0, 2, 3).reshape(LRU_BLOCKS, LRU_BLOCK, LRU_BLOCK)


def _mixer_weights_ride(shards):
    return _Ride(shards, MIXER_SHAPES, lambda ins, p: list(ins), _slot_of_sender, gather_by_chip=True)


def _wfo_ride(shard):
    zero_dsts = lambda outs: [outs[0].at[g, pl.ds(2 * FFN_OUT_SHARD, FFN_PAD_ROWS), :] for g in range(FFN_GROUPS)]
    return _Ride([shard], [jax.ShapeDtypeStruct((FFN_GROUPS, FFN_GROUP, D_MODEL), BF16)], lambda ins, p: list(ins),
                 lambda outs, d: [outs[0].at[d // 2, _half_rows(d), :]], zero_dsts, (FFN_PAD_ROWS, D_MODEL), FFN_GROUPS,
                 gather_by_chip=True)


W_IN_USE_ORDER = (0, 1, 2, 4, 6, 3, 5, 7)


def _arrival_rank_to_relation(jj):
    k = W_IN_USE_ORDER[-1]
    for pos in reversed(range(N_DEV - 1)):
        k = jnp.where(jj == pos, W_IN_USE_ORDER[pos], k)
    return k


def _in_proj(h0, norm_w, win_shard, me_arr):
    rows = h0.shape[0]
    tm = _tile(rows, 1664)
    n_i = rows // tm

    direct, relayed = DIRECT, RELAYED

    def body(me_ref, h_ref, nw_ref, wsh_ref, proj_ref, u_ref, wing_ref, u_all, wbuf, copy_sem,
             send_sems, recv_sems, relay_send_sems, relay_recv_sems, own_sem):
        del me_ref
        jj, i = pl.program_id(0), pl.program_id(1)
        me = _my_index()
        sibling = _peer(1)[0]

        def direct_copy(k, slot):
            n = direct.index(k)
            return pltpu.make_async_remote_copy(src_ref=wsh_ref, dst_ref=wing_ref.at[slot], send_sem=send_sems.at[n],
                                                recv_sem=recv_sems.at[n], device_id=_peer(k)[0], device_id_type=MESH_ID)

        def relay_copy(q, slot):
            n = relayed.index(q)
            return pltpu.make_async_remote_copy(src_ref=wing_ref.at[slot], dst_ref=wing_ref.at[slot],
                                                send_sem=relay_send_sems.at[n], recv_sem=relay_recv_sems.at[n],
                                                device_id=sibling, device_id_type=MESH_ID)

        own_slot = pltpu.make_async_copy(wsh_ref, wing_ref.at[me], own_sem)

        @pl.when(jnp.logical_and(jj == 0, i == 0))
        def _():
            for k in direct:
                direct_copy(k, me).start()
            own_slot.start()
            own = pltpu.make_async_copy(wsh_ref, wbuf, copy_sem)
            own.start()
            own.wait()

        for k in range(1, N_DEV):
            rank = W_IN_USE_ORDER.index(k)

            @pl.when(jnp.logical_and(jj == rank, i == 0))
            def _(k=k):
                p = _peer(k)[1]
                if k in direct:
                    direct_copy(k, p).wait_recv()
                    if k in relayed:
                        relay_copy(k, p).start()
                else:
                    relay_copy(k - 1, p).wait_recv()
                landed = pltpu.make_async_copy(wing_ref.at[p], wbuf, copy_sem)
                landed.start()
                landed.wait()

        rows_i = pl.ds(pl.multiple_of(i * tm, tm), tm)

        @pl.when(jj == 0)
        def _():
            x = h_ref[...]
            rs = lax.rsqrt(jnp.mean(x * x, axis=-1, keepdims=True) + NORM_EPS)
            u = (x * rs * nw_ref[...]).astype(BF16)
            u_all[rows_i, :] = u
            u_ref[...] = u
        proj_ref[...] = _dot(u_all[rows_i, :], wbuf[...]).astype(BF16)

        @pl.when(jnp.logical_and(jj == N_DEV - 1, i == n_i - 1))
        def _():
            for k in direct:
                direct_copy(k, me).wait_send()
            for q in relayed:
                relay_copy(q, _peer(q)[1]).wait_send()
            own_slot.wait()

    first_pass = lambda jj, i: jnp.where(jj == 0, i, n_i - 1)
    grid_spec = pltpu.PrefetchScalarGridSpec(
        num_scalar_prefetch=1, grid=(N_DEV, n_i),
        in_specs=[pl.BlockSpec((tm, D_MODEL), lambda jj, i, me: (first_pass(jj, i), 0)),
                  pl.BlockSpec((1, D_MODEL), lambda jj, i, me: (0, 0)), ANY],
        out_specs=[pl.BlockSpec((tm, D_MODEL), lambda jj, i, me: (i, me[0] ^ _arrival_rank_to_relation(jj))),
                   pl.BlockSpec((tm, D_MODEL), lambda jj, i, me: (first_pass(jj, i), 0)), ANY],
        scratch_shapes=[pltpu.VMEM((rows, D_MODEL), BF16), pltpu.VMEM((D_MODEL, D_MODEL), BF16),
                        pltpu.SemaphoreType.DMA(()),
                        pltpu.SemaphoreType.DMA((len(direct),)), pltpu.SemaphoreType.DMA((len(direct),)),
                        pltpu.SemaphoreType.DMA((len(relayed),)), pltpu.SemaphoreType.DMA((len(relayed),)),
                        pltpu.SemaphoreType.DMA(())])
    return pl.pallas_call(
        body, name="in_proj", grid_spec=grid_spec,
        out_shape=[jax.ShapeDtypeStruct((rows, N_DEV * D_MODEL), BF16),
                   jax.ShapeDtypeStruct((rows, D_MODEL), BF16),
                   jax.ShapeDtypeStruct((N_DEV, D_MODEL, D_MODEL), BF16)],
        compiler_params=pltpu.CompilerParams(dimension_semantics=("arbitrary", "arbitrary"),
                                             vmem_limit_bytes=VMEM_LIMIT, has_side_effects=True),
    )(me_arr, h0, norm_w, win_shard)


def _seg_spec(rows_per_block, seg):
    return pl.BlockSpec((rows_per_block, D_MODEL), lambda n, seg=seg: (n, seg))


def _chunks_per_step(n_chunks):
    return next(c for c in (5, 3, 2, 1) if n_chunks % c == 0)


def _retention_fwd(proj, cos2, sin2, dec, ride):
    rows = proj.shape[0]
    n_chunks = rows // CHUNK
    per_step = _chunks_per_step(n_chunks)
    n_steps = n_chunks // per_step
    tm = per_step * CHUNK
    n_r = ride.n

    def body(q_ref, k_ref, v_ref, g_ref, c_ref, s_ref, dec_ref, *refs):
        o_ref, zr_ref, st_ref = refs[n_r:n_r + 3]
        state = refs[2 * n_r + 3]
        push = ride.push(refs[:n_r], refs[n_r + 3:2 * n_r + 3], refs[2 * n_r + 4:])

        @pl.when(pl.program_id(0) == 0)
        def _():
            push.start()
            state[...] = jnp.zeros_like(state)

        for h in range(HEADS):
            sl = slice(HEAD_DIM * h, HEAD_DIM * (h + 1))
            st = state[h]
            for c in range(per_step):
                rw = slice(CHUNK * c, CHUNK * (c + 1))
                cos_t, sin_t = c_ref[rw, :], s_ref[rw, :]
                qh = _rot(q_ref[rw, sl].astype(F32), cos_t, sin_t)
                kh = _rot(k_ref[rw, sl].astype(F32), cos_t, sin_t) * QK_SCALE
                qb, kb, vb = qh.astype(BF16), kh.astype(BF16), v_ref[rw, sl]
                s = _dot_nt(qb, kb) * dec_ref[0, h]
                st_ref[c, h] = st
                o = _dot(s.astype(BF16), vb) + _dot(qb, st.astype(BF16)) * dec_ref[1, h]
                st = st * dec_ref[3, h] + _dot_tn((kh * dec_ref[2, h]).astype(BF16), vb)
                o_ref[rw, sl] = o.astype(BF16)
                r = lax.rsqrt(jnp.mean(o * o, axis=-1, keepdims=True) + NORM_EPS)
                g = g_ref[rw, sl].astype(F32)
                zr_ref[rw, sl] = (g * _sigmoid(g) * (o * r)).astype(BF16)
            state[h] = st

        @pl.when(pl.program_id(0) == n_steps // 2)
        def _():
            push.relay()

        @pl.when(pl.program_id(0) == n_steps - 1)
        def _():
            push.wait()

    tab = pl.BlockSpec((tm, HEAD_DIM), lambda n: (n, 0))
    return pl.pallas_call(
        body, name="retention_fwd", grid=(n_steps,),
        in_specs=[_seg_spec(tm, 0), _seg_spec(tm, 1), _seg_spec(tm, 2), _seg_spec(tm, 3), tab, tab,
                  pl.BlockSpec((4, HEADS, CHUNK, CHUNK), lambda n: (0, 0, 0, 0))] + ride.specs(),
        out_specs=[pl.BlockSpec((tm, D_MODEL), lambda n: (n, 0)),
                   pl.BlockSpec((tm, D_MODEL), lambda n: (n, 0)),
                   pl.BlockSpec((per_step, HEADS, HEAD_DIM, HEAD_DIM), lambda n: (n, 0, 0, 0))] + ride.specs(),
        out_shape=[jax.ShapeDtypeStruct((rows, D_MODEL), BF16),
                   jax.ShapeDtypeStruct((rows, D_MODEL), BF16),
                   jax.ShapeDtypeStruct((n_chunks, HEADS, HEAD_DIM, HEAD_DIM), F32)] + ride.out_shapes,
        scratch_shapes=[pltpu.VMEM((HEADS, HEAD_DIM, HEAD_DIM), F32)] + ride.scratch(),
        compiler_params=pltpu.CompilerParams(dimension_semantics=("arbitrary",), vmem_limit_bytes=VMEM_LIMIT,
                                             has_side_effects=True),
    )(proj, proj, proj, proj, cos2, sin2, dec, *ride.arrays)


def _lru_gates(c, ba, bx, wa_ref, wx_ref):
    pre_r, pre_i = [], []
    for g in range(LRU_BLOCKS):
        cg = c[:, LRU_BLOCK * g:LRU_BLOCK * (g + 1)].astype(BF16)
        pre_r.append(_dot(cg, wa_ref[g]))
        pre_i.append(_dot(cg, wx_ref[g]))
    return _sigmoid(jnp.concatenate(pre_r, axis=1) + ba), _sigmoid(jnp.concatenate(pre_i, axis=1) + bx)


def _lru_decay(r, lam):
    sp = jnp.maximum(-lam, 0.0) + jnp.log(1.0 + jnp.exp(-jnp.abs(lam)))
    log_a = -LRU_C * r * sp
    a = jnp.exp(log_a)
    one_minus_a2 = -jnp.tanh(log_a) * (a * a + 1.0)
    inv_mult = lax.rsqrt(jnp.maximum(one_minus_a2, 1e-30))
    return a, one_minus_a2 * inv_mult, inv_mult, sp


def _conv_taps(xbuf, tm, cw_ref, cb_ref):
    c = cb_ref[...] + cw_ref[3:4, :] * xbuf[8:8 + tm, :]
    for back in (1, 2, 3):
        c = c + cw_ref[3 - back:4 - back, :] * xbuf[8 - back:8 - back + tm, :]
    return c


def _lru_fwd(proj, conv_w, conv_b, ba, bx, lam, wa_g, wx_g, ride):
    rows = proj.shape[0]
    tm = _tile(rows, 320)
    n_t = rows // tm
    n_r = ride.n

    def body(x_ref, gt_ref, cw_ref, cb_ref, ba_ref, bx_ref, lam_ref, wa_ref, wx_ref, *refs):
        hs_ref, zl_ref, cri_ref = refs[n_r:n_r + 3]
        xbuf, abuf, ubuf, hcar = refs[2 * n_r + 3:2 * n_r + 7]
        push = ride.push(refs[:n_r], refs[n_r + 3:2 * n_r + 3], refs[2 * n_r + 7:])
        i = pl.program_id(0)

        @pl.when(i == 0)
        def _():
            push.start()
            xbuf[0:8, :] = jnp.zeros((8, D_MODEL), F32)
            hcar[...] = jnp.zeros_like(hcar)

        xbuf[8:8 + tm, :] = x_ref[...].astype(F32)
        c = _conv_taps(xbuf, tm, cw_ref, cb_ref)
        xbuf[0:8, :] = xbuf[tm:tm + 8, :]
        r, ig = _lru_gates(c, ba_ref[...], bx_ref[...], wa_ref, wx_ref)
        a, mult, _, _ = _lru_decay(r, lam_ref[...])
        cri_ref[0] = c.astype(BF16)
        cri_ref[1] = r.astype(BF16)
        cri_ref[2] = ig.astype(BF16)
        row = i * tm + lax.broadcasted_iota(jnp.int32, (tm, 1), 0)
        abuf[...] = a
        ubuf[...] = jnp.where(row >= PAD_ROWS, mult * (ig * c), 0.0)

        sub = lax.broadcasted_iota(jnp.int32, (8, D_MODEL), 0)

        def block(b, carry):
            off = pl.multiple_of(b * 8, 8)
            av, uv = abuf[pl.ds(off, 8), :], ubuf[pl.ds(off, 8), :]
            for s in (1, 2, 4):
                us = jnp.where(sub >= s, pltpu.roll(uv, s, 0), 0.0)
                as_ = jnp.where(sub >= s, pltpu.roll(av, s, 0), 1.0)
                uv = uv + av * us
                av = av * as_
            hv = uv + av * carry
            ubuf[pl.ds(off, 8), :] = hv
            return hv[7:8, :]

        hcar[...] = lax.fori_loop(0, tm // 8, block, hcar[...])
        gl, _ = _gelu_parts(gt_ref[...].astype(F32))
        hs = ubuf[...]
        hs_ref[...] = hs.astype(BF16)
        zl_ref[...] = (gl * hs).astype(BF16)

        @pl.when(i == n_t // 2)
        def _():
            push.relay()

        @pl.when(i == n_t - 1)
        def _():
            push.wait()

    vec = pl.BlockSpec((1, D_MODEL), lambda i: (0, 0))
    mat = pl.BlockSpec((LRU_BLOCKS, LRU_BLOCK, LRU_BLOCK), lambda i: (0, 0, 0))
    row = pl.BlockSpec((tm, D_MODEL), lambda i: (i, 0))
    return pl.pallas_call(
        body, name="lru_fwd", grid=(n_t,),
        in_specs=[_seg_spec(tm, 4), _seg_spec(tm, 5), pl.BlockSpec((4, D_MODEL), lambda i: (0, 0)),
                  vec, vec, vec, vec, mat, mat] + ride.specs(),
        out_specs=[row, row, pl.BlockSpec((3, tm, D_MODEL), lambda i: (0, i, 0))] + ride.specs(),
        out_shape=[jax.ShapeDtypeStruct((rows, D_MODEL), BF16)] * 2
        + [jax.ShapeDtypeStruct((3, rows, D_MODEL), BF16)] + ride.out_shapes,
        scratch_shapes=[pltpu.VMEM((tm + 8, D_MODEL), F32), pltpu.VMEM((tm, D_MODEL), F32),
                        pltpu.VMEM((tm, D_MODEL), F32), pltpu.VMEM((1, D_MODEL), F32)] + ride.scratch(),
        compiler_params=pltpu.CompilerParams(dimension_semantics=("arbitrary",), vmem_limit_bytes=VMEM_LIMIT,
                                             has_side_effects=True),
    )(proj, proj, conv_w, conv_b, ba, bx, lam, wa_g, wx_g, *ride.arrays)


def _mix_fwd(zr, zl, proj, h0, wbr, wbl, wout, ride):
    rows = h0.shape[0]
    tm = _tile(rows, 640)
    n_t = rows // tm
    n_r = ride.n

    def body(zr_ref, zl_ref, ga_ref, gb_ref, h0_ref, wbr_ref, wbl_ref, wo_ref, *refs):
        h1_ref, yr_ref, yl_ref, mx_ref = refs[n_r:n_r + 4]
        push = ride.push(refs[:n_r], refs[n_r + 4:2 * n_r + 4], refs[2 * n_r + 4:])

        @pl.when(pl.program_id(0) == 0)
        def _():
            push.start()

        yr = _dot(zr_ref[...], wbr_ref[...])
        yl = _dot(zl_ref[...], wbl_ref[...])
        mixed = (_sigmoid(ga_ref[...].astype(F32)) * yr + _sigmoid(gb_ref[...].astype(F32)) * yl).astype(BF16)
        yr_ref[...] = yr.astype(BF16)
        yl_ref[...] = yl.astype(BF16)
        mx_ref[...] = mixed
        h1_ref[...] = h0_ref[...] + _dot(mixed, wo_ref[...])

        @pl.when(pl.program_id(0) == n_t // 2)
        def _():
            push.relay()

        @pl.when(pl.program_id(0) == n_t - 1)
        def _():
            push.wait()

    row = pl.BlockSpec((tm, D_MODEL), lambda i: (i, 0))
    wsp = pl.BlockSpec((D_MODEL, D_MODEL), lambda i: (0, 0))
    return pl.pallas_call(
        body, name="mix_fwd", grid=(n_t,),
        in_specs=[row, row, _seg_spec(tm, 6), _seg_spec(tm, 7), row, wsp, wsp, wsp] + ride.specs(),
        out_specs=[row, row, row, row] + ride.specs(),
        out_shape=[jax.ShapeDtypeStruct((rows, D_MODEL), F32)] + [jax.ShapeDtypeStruct((rows, D_MODEL), BF16)] * 3
        + ride.out_shapes,
        scratch_shapes=ride.scratch(),
        compiler_params=pltpu.CompilerParams(dimension_semantics=("arbitrary",), vmem_limit_bytes=VMEM_LIMIT,
                                             has_side_effects=True),
    )(zr, zl, proj, proj, h0, wbr, wbl, wout, *ride.arrays)


def _ffn_fwd_loss(h1, norm_w, wfi_g, wfo_g, final_w, target):
    rows = h1.shape[0]
    tm = _tile(rows, 320)
    piece = 64
    n_piece = tm // piece

    def body(h1_ref, nw_ref, wfi_ref, wfo_ref, fw_ref, *refs):
        t_refs = refs[:n_piece]
        u2_ref, g_ref, up_ref, act_ref, dh2_ref, red_ref = refs[n_piece:]
        i = pl.program_id(0)

        @pl.when(i == 0)
        def _():
            red_ref[...] = jnp.zeros_like(red_ref)

        x = h1_ref[...]
        rs = lax.rsqrt(jnp.mean(x * x, axis=-1, keepdims=True) + NORM_EPS)
        u2 = (x * rs * nw_ref[...]).astype(BF16)
        u2_ref[...] = u2
        ffn = None
        for d in range(FFN_GROUPS):
            cols = slice(FFN_GROUP * d, FFN_GROUP * (d + 1))
            g = _dot(u2, wfi_ref[d])
            up = _dot(u2, wfi_ref[d + FFN_GROUPS])
            act = (g * _sigmoid(g) * up).astype(BF16)
            g_ref[:, cols] = g.astype(BF16)
            up_ref[:, cols] = up.astype(BF16)
            act_ref[:, cols] = act
            part = _dot(act, wfo_ref[d])
            ffn = part if ffn is None else ffn + part

        h2 = x + ffn
        rs = lax.rsqrt(jnp.mean(h2 * h2, axis=-1, keepdims=True) + NORM_EPS)
        nh = h2 * rs
        fw = fw_ref[...]
        row = i * tm + lax.broadcasted_iota(jnp.int32, (tm, 1), 0)
        tgt = jnp.concatenate([t[...] for t in t_refs], axis=0)
        diff = jnp.where(row >= CHUNK, nh * fw - tgt, 0.0)
        dy = diff * (1.0 / D_MODEL)
        red_ref[0:1, :] += jnp.sum(diff * diff, axis=0, keepdims=True)
        red_ref[1:2, :] += jnp.sum(dy * nh, axis=0, keepdims=True)
        dn = dy * fw
        dh2_ref[...] = rs * (dn - nh * jnp.mean(dn * nh, axis=-1, keepdims=True))

    row = pl.BlockSpec((tm, D_MODEL), lambda i: (i, 0))
    vec = pl.BlockSpec((1, D_MODEL), lambda i: (0, 0))
    hid = pl.BlockSpec((tm, FFN_GROUPS * FFN_GROUP), lambda i: (i, 0))
    hid_shape = jax.ShapeDtypeStruct((rows, FFN_GROUPS * FFN_GROUP), BF16)
    resident = dict(pipeline_mode=pl.Buffered(1))
    head_pieces = CHUNK // piece
    t_specs = [pl.BlockSpec((piece, D_MODEL), lambda i, k=k: (jnp.maximum(i * n_piece + k - head_pieces, 0), 0))
               for k in range(n_piece)]
    return pl.pallas_call(
        body, name="ffn_fwd_loss", grid=(rows // tm,),
        in_specs=[row, vec,
                  pl.BlockSpec((2 * FFN_GROUPS, D_MODEL, FFN_GROUP), lambda i: (0, 0, 0), **resident),
                  pl.BlockSpec((FFN_GROUPS, FFN_GROUP, D_MODEL), lambda i: (0, 0, 0), **resident),
                  vec] + t_specs,
        out_specs=[row, hid, hid, hid, row, pl.BlockSpec((8, D_MODEL), lambda i: (0, 0))],
        out_shape=[jax.ShapeDtypeStruct((rows, D_MODEL), BF16), hid_shape, hid_shape, hid_shape,
                   jax.ShapeDtypeStruct((rows, D_MODEL), F32), jax.ShapeDtypeStruct((8, D_MODEL), F32)],
        compiler_params=_cparams(("arbitrary",)),
    )(h1, norm_w, wfi_g, wfo_g, final_w, *([target] * n_piece))


def _wgrad(a, b, ka, tn, out_dtype, b_halves=False):
    rows = a.shape[0]
    na = a.shape[1] // ka
    tm = _tile(rows, 1664)
    nm = rows // tm
    if b_halves:
        per_half = b.shape[2] // tn
        nb = 2 * per_half
        b_spec = pl.BlockSpec((None, tm, tn), lambda p, q, m: (q // per_half, m, q % per_half))
    else:
        nb = b.shape[1] // tn
        b_spec = pl.BlockSpec((tm, tn), lambda p, q, m: (m, q))

    def body(a_ref, b_ref, o_ref, acc):
        m = pl.program_id(2)

        @pl.when(m == 0)
        def _():
            acc[...] = jnp.zeros_like(acc)

        acc[...] += _dot_tn(a_ref[...].astype(BF16), b_ref[...].astype(BF16))

        @pl.when(m == nm - 1)
        def _():
            o_ref[...] = acc[...].astype(out_dtype)

    return pl.pallas_call(
        body, name="wgrad", grid=(na, nb, nm),
        in_specs=[pl.BlockSpec((tm, ka), lambda p, q, m: (m, p)), b_spec],
        out_specs=pl.BlockSpec((None, None, ka, tn), lambda p, q, m: (p, q, 0, 0)),
        out_shape=jax.ShapeDtypeStruct((na, nb, ka, tn), out_dtype),
        scratch_shapes=[pltpu.VMEM((ka, tn), F32)],
        compiler_params=_cparams(("parallel", "parallel", "arbitrary")),
    )(a, b)


WIN_NEAR = (2, 4, 3, 5, 1)
WIN_FAR = (6, 7)
WIN_ORDER = WIN_FAR + WIN_NEAR + (0,)


def _w_in_relation_at(jj):
    k = 0
    for pos in reversed(range(len(WIN_ORDER) - 1)):
        k = jnp.where(jj == pos, WIN_ORDER[pos], k)
    return k


def _wgrad_w_in(u, dproj, me_arr):
    rows = u.shape[0]
    tm = _tile(rows, 1664)
    nm = rows // tm
    n_near = len(WIN_NEAR)

    def body(me_ref, a_ref, b_ref, far_ref, land_ref, acc, sbuf, send_sems, recv_sems, own_sem):
        del me_ref
        jj, m = pl.program_id(0), pl.program_id(1)

        def near_copy(n):
            k = WIN_NEAR[n]
            return pltpu.make_async_remote_copy(src_ref=sbuf.at[n], dst_ref=land_ref.at[k], send_sem=send_sems.at[n],
                                                recv_sem=recv_sems.at[n], device_id=_peer(k)[0], device_id_type=MESH_ID)

        own_copy = pltpu.make_async_copy(sbuf.at[n_near], land_ref.at[0], own_sem)

        @pl.when(m == 0)
        def _():
            acc[...] = jnp.zeros_like(acc)

        acc[...] += _dot_tn(a_ref[...], b_ref[...])

        for pos, k in enumerate(WIN_ORDER):
            @pl.when(jnp.logical_and(jj == pos, m == nm - 1))
            def _(k=k):
                block = acc[...].astype(BF16)
                if k in WIN_FAR:
                    far_ref[...] = block
                elif k == 0:
                    sbuf[n_near] = block
                    own_copy.start()
                else:
                    sbuf[WIN_NEAR.index(k)] = block
                    near_copy(WIN_NEAR.index(k)).start()

        @pl.when(jnp.logical_and(jj == N_DEV - 1, m == nm - 1))
        def _():
            for n in range(n_near):
                near_copy(n).wait_recv()
            for n in range(n_near):
                near_copy(n).wait_send()
            own_copy.wait()

    grid_spec = pltpu.PrefetchScalarGridSpec(
        num_scalar_prefetch=1, grid=(N_DEV, nm),
        in_specs=[pl.BlockSpec((tm, D_MODEL), lambda jj, m, me: (m, 0)),
                  pl.BlockSpec((tm, D_MODEL), lambda jj, m, me: (m, me[0] ^ _w_in_relation_at(jj)))],
        out_specs=[pl.BlockSpec((None, D_MODEL, D_MODEL), lambda jj, m, me: (jnp.minimum(jj, len(WIN_FAR) - 1), 0, 0)),
                   ANY],
        scratch_shapes=[pltpu.VMEM((D_MODEL, D_MODEL), F32), pltpu.VMEM((n_near + 1, D_MODEL, D_MODEL), BF16),
                        pltpu.SemaphoreType.DMA((n_near,)), pltpu.SemaphoreType.DMA((n_near,)),
                        pltpu.SemaphoreType.DMA(())])
    return pl.pallas_call(
        body, name="wgrad_w_in", grid_spec=grid_spec,
        out_shape=[jax.ShapeDtypeStruct((len(WIN_FAR), D_MODEL, D_MODEL), BF16),
                   jax.ShapeDtypeStruct((n_near + 1, D_MODEL, D_MODEL), BF16)],
        compiler_params=pltpu.CompilerParams(dimension_semantics=("arbitrary", "arbitrary"),
                                             vmem_limit_bytes=VMEM_LIMIT, has_side_effects=True),
    )(me_arr, u, dproj)


def _ffn_bwd(dh2, g, up, h1, norm_w, wfi_g, wfo_g):
    rows = h1.shape[0]
    tm = _tile(rows, 320)

    def body(dh2_ref, g_ref, up_ref, h1_ref, nw_ref, wfi_ref, wfo_ref, dgu_ref, dh1_ref, dw_ref):
        @pl.when(pl.program_id(0) == 0)
        def _():
            dw_ref[...] = jnp.zeros_like(dw_ref)

        dh2 = dh2_ref[...]
        dh2_b = dh2.astype(BF16)
        du2 = None
        for d in range(FFN_GROUPS):
            cols = slice(FFN_GROUP * d, FFN_GROUP * (d + 1))
            dact = _dot_nt(dh2_b, wfo_ref[d])
            gv, uv = g_ref[:, cols].astype(F32), up_ref[:, cols].astype(F32)
            sg = _sigmoid(gv)
            dg = (dact * uv * (sg * (1.0 + gv * (1.0 - sg)))).astype(BF16)
            dup = (dact * (gv * sg)).astype(BF16)
            dgu_ref[0, :, cols] = dg
            dgu_ref[1, :, cols] = dup
            part = _dot_nt(dg, wfi_ref[d]) + _dot_nt(dup, wfi_ref[d + FFN_GROUPS])
            du2 = part if du2 is None else du2 + part
        dx, dw = _rms_bwd(h1_ref[...], nw_ref[...], du2)
        dw_ref[0:1, :] += dw
        dh1_ref[...] = dh2 + dx

    row = pl.BlockSpec((tm, D_MODEL), lambda i: (i, 0))
    vec = pl.BlockSpec((1, D_MODEL), lambda i: (0, 0))
    hid = pl.BlockSpec((tm, FFN_GROUPS * FFN_GROUP), lambda i: (i, 0))
    resident = dict(pipeline_mode=pl.Buffered(1))
    return pl.pallas_call(
        body, name="ffn_bwd", grid=(rows // tm,),
        in_specs=[row, hid, hid, row, vec,
                  pl.BlockSpec((2 * FFN_GROUPS, D_MODEL, FFN_GROUP), lambda i: (0, 0, 0), **resident),
                  pl.BlockSpec((FFN_GROUPS, FFN_GROUP, D_MODEL), lambda i: (0, 0, 0), **resident)],
        out_specs=[pl.BlockSpec((2, tm, FFN_GROUPS * FFN_GROUP), lambda i: (0, i, 0)), row,
                   pl.BlockSpec((8, D_MODEL), lambda i: (0, 0))],
        out_shape=[jax.ShapeDtypeStruct((2, rows, FFN_GROUPS * FFN_GROUP), BF16),
                   jax.ShapeDtypeStruct((rows, D_MODEL), F32), jax.ShapeDtypeStruct((8, D_MODEL), F32)],
        compiler_params=_cparams(("arbitrary",)),
    )(dh2, g, up, h1, norm_w, wfi_g, wfo_g)


def _mix_bwd(dh1, yr, yl, proj, wbr, wbl, wout):
    rows = dh1.shape[0]
    tm = _tile(rows, 640)

    def body(dh1_ref, yr_ref, yl_ref, ga_ref, gb_ref, wbr_ref, wbl_ref, wo_ref,
             dyr_ref, dyl_ref, dseg_ref, dzr_ref, dzl_ref):
        dmix = _dot_nt(dh1_ref[...].astype(BF16), wo_ref[...])
        sa, sb = _sigmoid(ga_ref[...].astype(F32)), _sigmoid(gb_ref[...].astype(F32))
        dyr = (dmix * sa).astype(BF16)
        dyl = (dmix * sb).astype(BF16)
        dyr_ref[...] = dyr
        dyl_ref[...] = dyl
        dseg_ref[:, 0:D_MODEL] = (dmix * yr_ref[...].astype(F32) * (sa * (1.0 - sa))).astype(BF16)
        dseg_ref[:, D_MODEL:2 * D_MODEL] = (dmix * yl_ref[...].astype(F32) * (sb * (1.0 - sb))).astype(BF16)
        dzr_ref[...] = _dot_nt(dyr, wbr_ref[...]).astype(BF16)
        dzl_ref[...] = _dot_nt(dyl, wbl_ref[...]).astype(BF16)

    row = pl.BlockSpec((tm, D_MODEL), lambda i: (i, 0))
    wsp = pl.BlockSpec((D_MODEL, D_MODEL), lambda i: (0, 0))
    bshape = jax.ShapeDtypeStruct((rows, D_MODEL), BF16)
    return pl.pallas_call(
        body, name="mix_bwd", grid=(rows // tm,),
        in_specs=[row, row, row, _seg_spec(tm, 6), _seg_spec(tm, 7), wsp, wsp, wsp],
        out_specs=[row, row, pl.BlockSpec((tm, 2 * D_MODEL), lambda i: (i, 3)), row, row],
        out_shape=[bshape, bshape, jax.ShapeDtypeStruct((rows, N_DEV * D_MODEL), BF16), bshape, bshape],
        compiler_params=_cparams(("parallel",)),
    )(dh1, yr, yl, proj, proj, wbr, wbl, wout)


S1_SHAPES = [
    jax.ShapeDtypeStruct((N_DEV, D_MODEL, FFN_GROUP), BF16),
    jax.ShapeDtypeStruct((N_DEV, FFN_OUT_SHARD, D_MODEL), BF16),
]


def _s1_parts(ins, p):
    return [ins[0].at[p], ins[1].at[p // 2, _half_rows(p), :]]


def _lru_bwd(dzl, hs, cri, proj, dproj, conv_w, lam, wa_g, wx_g, s1_grads):
    rows = dzl.shape[0]
    tm = _tile(rows, 640)
    nt = rows // tm
    t8 = tm // 8
    n_s1 = len(s1_grads)

    def body(dzl_ref, hs_ref, hsp_ref, cri_ref, x_ref, gt_ref, cw_ref, lam_ref, wa_ref, wx_ref, dproj_in, *refs):
        del dproj_in
        s1_refs = refs[:n_s1]
        dseg_ref, dwa_ref, dwx_ref, sm_ref = refs[n_s1:n_s1 + 4]
        land_refs = refs[n_s1 + 4:2 * n_s1 + 4]
        (xbuf, abuf, mbuf, ibuf, dbuf, dcbuf, dpr_s, dpi_s, sums, conv_sums, anext, dhcar,
         send_sems, recv_sems, loc_sems) = refs[2 * n_s1 + 4:]
        step = pl.program_id(0)
        i = nt - 1 - step
        push = _Push(lambda p: _s1_parts(s1_refs, p), lambda s: [r.at[s] for r in land_refs],
                     (send_sems, recv_sems, loc_sems), n_s1)

        @pl.when(step == 0)
        def _():
            push.start()
            dwa_ref[...] = jnp.zeros_like(dwa_ref)
            dwx_ref[...] = jnp.zeros_like(dwx_ref)
            sm_ref[...] = jnp.zeros_like(sm_ref)
            anext[...] = jnp.zeros_like(anext)
            dhcar[...] = jnp.zeros_like(dhcar)
            dcbuf[tm:tm + 8, :] = jnp.zeros((8, D_MODEL), F32)

        slab, lanes = 16, 256
        lam_v = lam_ref[...]
        xbuf[0:8, :] = jnp.where(i == 0, 0.0, hsp_ref[8:16, :].astype(F32))
        sums[...] = jnp.zeros_like(sums)

        def before_scan(k, carry):
            rw = pl.ds(pl.multiple_of(k * slab, slab), slab)
            for q in range(D_MODEL // lanes):
                ln = slice(lanes * q, lanes * (q + 1))
                a, mult, inv_mult, _ = _lru_decay(cri_ref[1, rw, ln].astype(F32), lam_v[:, ln])
                abuf[rw, ln] = a
                mbuf[rw, ln] = mult
                ibuf[rw, ln] = inv_mult
                gl, dgl = _gelu_parts(gt_ref[rw, ln].astype(F32))
                dzl_v = dzl_ref[rw, ln].astype(F32)
                hs_v = hs_ref[rw, ln].astype(F32)
                dseg_ref[rw, D_MODEL + lanes * q:D_MODEL + lanes * (q + 1)] = (dzl_v * hs_v * dgl).astype(BF16)
                dbuf[rw, ln] = dzl_v * gl
                xbuf[pl.ds(pl.multiple_of(k * slab + 8, 8), slab), ln] = hs_v
            return carry

        lax.fori_loop(0, tm // slab, before_scan, 0)

        sub = lax.broadcasted_iota(jnp.int32, (8, D_MODEL), 0)

        def block(k, carry):
            dh_next, a_next = carry
            off = pl.multiple_of((t8 - 1 - k) * 8, 8)
            a_blk = abuf[pl.ds(off, 8), :]
            av = jnp.where(sub < 7, pltpu.roll(a_blk, 7, 0), a_next)
            uv = dbuf[pl.ds(off, 8), :]
            for s in (1, 2, 4):
                us = jnp.where(sub < 8 - s, pltpu.roll(uv, 8 - s, 0), 0.0)
                as_ = jnp.where(sub < 8 - s, pltpu.roll(av, 8 - s, 0), 1.0)
                uv = uv + av * us
                av = av * as_
            hv = uv + av * dh_next
            dbuf[pl.ds(off, 8), :] = hv
            return hv[0:1, :], a_blk[0:1, :]

        dh_first, a_first = lax.fori_loop(0, t8, block, (dhcar[...], anext[...]))
        dhcar[...] = dh_first
        anext[...] = a_first

        sp = jnp.maximum(-lam_v, 0.0) + jnp.log(1.0 + jnp.exp(-jnp.abs(lam_v)))
        sub_q = lax.broadcasted_iota(jnp.int32, (8, lanes), 0)
        row16 = lax.broadcasted_iota(jnp.int32, (slab, 1), 0)

        def after_scan(k, carry):
            off = pl.multiple_of(k * slab, slab)
            rw = pl.ds(off, slab)
            for q in range(D_MODEL // lanes):
                ln = slice(lanes * q, lanes * (q + 1))
                before = xbuf[pl.ds(off, 8), ln]
                h_lo = xbuf[pl.ds(pl.multiple_of(off + 8, 8), 8), ln]
                h_hi = xbuf[pl.ds(pl.multiple_of(off + 16, 8), 8), ln]
                hprev = jnp.concatenate([jnp.where(sub_q >= 1, pltpu.roll(h_lo, 1, 0), before[7:8, :]),
                                         jnp.where(sub_q >= 1, pltpu.roll(h_hi, 1, 0), h_lo[7:8, :])], axis=0)
                c, r, ig = (cri_ref[n, rw, ln].astype(F32) for n in range(3))
                a, mult, inv_mult = abuf[rw, ln], mbuf[rw, ln], ibuf[rw, ln]
                dh = dbuf[rw, ln]
                duu = jnp.where(i * tm + off + row16 >= PAD_ROWS, dh, 0.0)
                t_mult = duu * mult
                dlog_a = dh * hprev * a - duu * ig * c * (a * a) * inv_mult
                dpr = dlog_a * (-LRU_C * sp[:, ln]) * r * (1.0 - r)
                dpi = t_mult * c * ig * (1.0 - ig)
                dpr_s[rw, ln] = dpr.astype(BF16)
                dpi_s[rw, ln] = dpi.astype(BF16)
                dcbuf[rw, ln] = t_mult * ig
                sums[0, :, ln] += dlog_a * r
                sums[1, :, ln] += dpr
                sums[2, :, ln] += dpi
            return carry

        lax.fori_loop(0, tm // slab, after_scan, 0)

        dcs = []
        for g in range(LRU_BLOCKS):
            sl = slice(LRU_BLOCK * g, LRU_BLOCK * (g + 1))
            cg = cri_ref[0, :, sl]
            dpr_b, dpi_b = dpr_s[:, sl], dpi_s[:, sl]
            dwa_ref[g] += _dot_tn(cg, dpr_b)
            dwx_ref[g] += _dot_tn(cg, dpi_b)
            dcs.append(_dot_nt(dpr_b, wa_ref[g]) + _dot_nt(dpi_b, wx_ref[g]))
        dc = dcbuf[0:tm, :] + jnp.concatenate(dcs, axis=1)

        dcbuf[0:tm, :] = dc
        conv_sums[...] = jnp.zeros_like(conv_sums)

        def conv_back(k, carry):
            off = pl.multiple_of(k * slab, slab)
            rw = pl.ds(off, slab)
            for q in range(D_MODEL // lanes):
                ln = slice(lanes * q, lanes * (q + 1))
                blocks = [dcbuf[pl.ds(pl.multiple_of(off + 8 * b, 8), 8), ln] for b in range(3)]
                x_v = x_ref[rw, ln].astype(F32)
                now = jnp.concatenate(blocks[:2], axis=0)
                dlin = cw_ref[3:4, ln] * now
                conv_sums[3, :, ln] += now * x_v
                conv_sums[4, :, ln] += now
                for back in (1, 2, 3):
                    turned = [pltpu.roll(b, 8 - back, 0) for b in blocks]
                    later = jnp.concatenate([jnp.where(sub_q < 8 - back, turned[0], turned[1]),
                                             jnp.where(sub_q < 8 - back, turned[1], turned[2])], axis=0)
                    dlin = dlin + cw_ref[3 - back:4 - back, ln] * later
                    conv_sums[3 - back, :, ln] += later * x_v
                dseg_ref[rw, ln] = dlin.astype(BF16)
            return carry

        lax.fori_loop(0, tm // slab, conv_back, 0)
        dcbuf[tm:tm + 8, :] = dcbuf[0:8, :]
        for n in range(5):
            sm_ref[n:n + 1, :] += jnp.sum(conv_sums[n], axis=0, keepdims=True)
        sm_ref[5:6, :] += jnp.sum(sums[1], axis=0, keepdims=True)
        sm_ref[6:7, :] += jnp.sum(sums[2], axis=0, keepdims=True)
        sm_ref[7:8, :] += jnp.sum(sums[0], axis=0, keepdims=True) * (LRU_C * _sigmoid(-lam_v))

        @pl.when(step == nt - 1)
        def _():
            push.wait()

    rowb = pl.BlockSpec((tm, D_MODEL), lambda s: (nt - 1 - s, 0))
    t16 = tm // 16
    prev8 = pl.BlockSpec((16, D_MODEL), lambda s: (jnp.maximum((nt - 1 - s) * t16 - 1, 0), 0))
    seg = lambda k: pl.BlockSpec((tm, D_MODEL), lambda s, k=k: (nt - 1 - s, k))
    vec = pl.BlockSpec((1, D_MODEL), lambda s: (0, 0))
    mat = pl.BlockSpec((LRU_BLOCKS, LRU_BLOCK, LRU_BLOCK), lambda s: (0, 0, 0))
    mshape = jax.ShapeDtypeStruct((LRU_BLOCKS, LRU_BLOCK, LRU_BLOCK), F32)
    n_in = 10
    return pl.pallas_call(
        body, name="lru_bwd", grid=(nt,),
        in_specs=[rowb, rowb, prev8, pl.BlockSpec((3, tm, D_MODEL), lambda s: (0, nt - 1 - s, 0)), seg(4), seg(5),
                  pl.BlockSpec((4, D_MODEL), lambda s: (0, 0)), vec, mat, mat, ANY] + [ANY] * n_s1,
        out_specs=[pl.BlockSpec((tm, 2 * D_MODEL), lambda s: (nt - 1 - s, 2)), mat, mat,
                   pl.BlockSpec((8, D_MODEL), lambda s: (0, 0))] + [ANY] * n_s1,
        out_shape=[jax.ShapeDtypeStruct(dproj.shape, dproj.dtype), mshape, mshape,
                   jax.ShapeDtypeStruct((8, D_MODEL), F32)] + S1_SHAPES,
        input_output_aliases={n_in: 0},
        scratch_shapes=[pltpu.VMEM((tm + 8, D_MODEL), F32), pltpu.VMEM((tm, D_MODEL), F32),
                        pltpu.VMEM((tm, D_MODEL), F32), pltpu.VMEM((tm, D_MODEL), F32),
                        pltpu.VMEM((tm, D_MODEL), F32), pltpu.VMEM((tm + 8, D_MODEL), F32),
                        pltpu.VMEM((tm, D_MODEL), BF16), pltpu.VMEM((tm, D_MODEL), BF16),
                        pltpu.VMEM((3, 16, D_MODEL), F32), pltpu.VMEM((5, 16, D_MODEL), F32),
                        pltpu.VMEM((1, D_MODEL), F32), pltpu.VMEM((1, D_MODEL), F32)] + _push_sems(n_s1),
        compiler_params=pltpu.CompilerParams(dimension_semantics=("arbitrary",), vmem_limit_bytes=VMEM_LIMIT,
                                             has_side_effects=True),
    )(dzl, hs, hs, cri, proj, proj, conv_w, lam, wa_g, wx_g, dproj, *s1_grads)


def _retention_bwd(dzr, o, proj, states, cos2, sin2, dec, dproj, ride):
    rows = dzr.shape[0]
    n_chunks = rows // CHUNK
    per_step = _chunks_per_step(n_chunks)
    n_steps = n_chunks // per_step
    tm = per_step * CHUNK
    n_r = ride.n

    def body(dzr_ref, o_ref, q_ref, k_ref, v_ref, g_ref, st_ref, c_ref, s_ref, dec_ref, dproj_in, *refs):
        del dproj_in
        dseg_ref = refs[n_r]
        dstate = refs[2 * n_r + 1]
        push = ride.push(refs[:n_r], refs[n_r + 1:2 * n_r + 1], refs[2 * n_r + 2:])

        @pl.when(pl.program_id(0) == 0)
        def _():
            push.start()
            dstate[...] = jnp.zeros_like(dstate)

        for h in range(HEADS):
            sl = slice(HEAD_DIM * h, HEAD_DIM * (h + 1))
            intra, qd, kd, cd = dec_ref[0, h], dec_ref[1, h], dec_ref[2, h], dec_ref[3, h]
            dst = dstate[h]
            for c in reversed(range(per_step)):
                rw = slice(CHUNK * c, CHUNK * (c + 1))
                cos_t, sin_t = c_ref[rw, :], s_ref[rw, :]
                o = o_ref[rw, sl].astype(F32)
                g = g_ref[rw, sl].astype(F32)
                dzr_v = dzr_ref[rw, sl].astype(F32)
                sg = _sigmoid(g)
                r = lax.rsqrt(jnp.mean(o * o, axis=-1, keepdims=True) + NORM_EPS)
                on = o * r
                dseg_ref[rw, 3 * D_MODEL + HEAD_DIM * h:3 * D_MODEL + HEAD_DIM * (h + 1)] = (
                    dzr_v * on * (sg * (1.0 + g * (1.0 - sg)))).astype(BF16)
                don = dzr_v * (g * sg)
                do = r * (don - on * jnp.mean(don * on, axis=-1, keepdims=True))
                dob = do.astype(BF16)

                qh = _rot(q_ref[rw, sl].astype(F32), cos_t, sin_t)
                kh = _rot(k_ref[rw, sl].astype(F32), cos_t, sin_t) * QK_SCALE
                qb, kb, vb = qh.astype(BF16), kh.astype(BF16), v_ref[rw, sl]
                s = (_dot_nt(qb, kb) * intra).astype(BF16)
                ds = (_dot_nt(dob, vb) * intra).astype(BF16)
                st_b = st_ref[c, h].astype(BF16)
                dst_b = dst.astype(BF16)
                dv = _dot_tn(s, dob) + _dot((kh * kd).astype(BF16), dst_b)
                dq = _dot(ds, kb) + _dot_nt(dob, st_b) * qd
                dk = _dot_tn(ds, qb) + _dot_nt(vb, dst_b) * kd
                dst = dst * cd + _dot_tn((qh * qd).astype(BF16), dob)
                dseg_ref[rw, 2 * D_MODEL + HEAD_DIM * h:2 * D_MODEL + HEAD_DIM * (h + 1)] = dv.astype(BF16)
                dseg_ref[rw, sl] = _rot_t(dq, cos_t, sin_t).astype(BF16)
                dseg_ref[rw, D_MODEL + HEAD_DIM * h:D_MODEL + HEAD_DIM * (h + 1)] = (
                    _rot_t(dk, cos_t, sin_t) * QK_SCALE).astype(BF16)
            dstate[h] = dst

        @pl.when(pl.program_id(0) == n_steps - 1)
        def _():
            push.wait()

    rev = lambda s: n_steps - 1 - s
    rowb = pl.BlockSpec((tm, D_MODEL), lambda s: (rev(s), 0))
    seg = lambda k: pl.BlockSpec((tm, D_MODEL), lambda s, k=k: (rev(s), k))
    tab = pl.BlockSpec((tm, HEAD_DIM), lambda s: (rev(s), 0))
    return pl.pallas_call(
        body, name="retention_bwd", grid=(n_steps,),
        in_specs=[rowb, rowb, seg(0), seg(1), seg(2), seg(3),
                  pl.BlockSpec((per_step, HEADS, HEAD_DIM, HEAD_DIM), lambda s: (rev(s), 0, 0, 0)), tab, tab,
                  pl.BlockSpec((4, HEADS, CHUNK, CHUNK), lambda s: (0, 0, 0, 0)), ANY] + ride.specs(),
        out_specs=[pl.BlockSpec((tm, 4 * D_MODEL), lambda s: (rev(s), 0))] + ride.specs(),
        out_shape=[jax.ShapeDtypeStruct(dproj.shape, dproj.dtype)] + ride.out_shapes,
        input_output_aliases={10: 0},
        scratch_shapes=[pltpu.VMEM((HEADS, HEAD_DIM, HEAD_DIM), F32)] + ride.scratch(),
        compiler_params=pltpu.CompilerParams(dimension_semantics=("arbitrary",), vmem_limit_bytes=VMEM_LIMIT,
                                             has_side_effects=True),
    )(dzr, o, proj, proj, proj, proj, states, cos2, sin2, dec, dproj, *ride.arrays)


S2_SHAPES = [
    jax.ShapeDtypeStruct((N_DEV, LRU_BLOCKS, LRU_ROWS, LRU_BLOCK), F32),
    jax.ShapeDtypeStruct((N_DEV, LRU_BLOCKS, LRU_ROWS, LRU_BLOCK), F32),
]


def _s2_parts(ins, p):
    return [r.at[p] for r in ins]


def _in_proj_bwd(dproj, win_g, h0, norm_w, dh1, d_win_far, s2_grads, pack_early):
    rows = h0.shape[0]
    tm = _tile(rows, 320)
    n_i = rows // tm
    n_s2 = len(s2_grads)
    n_far = len(WIN_FAR)
    pack_rows = pack_early.shape[0]

    def body(dseg_ref, w_ref, h0_ref, nw_ref, dh1_ref, far_ref, early_ref, *refs):
        s2_refs = refs[:n_s2]
        dh0_ref, dw_ref, far_land = refs[n_s2:n_s2 + 3]
        land_refs = refs[n_s2 + 3:2 * n_s2 + 3]
        early_land, late_land = refs[2 * n_s2 + 3:2 * n_s2 + 5]
        (send_sems, recv_sems, loc_sems, far_send_sems, far_recv_sems, late_buf) = refs[2 * n_s2 + 5:2 * n_s2 + 11]
        early_sems, late_sems = refs[2 * n_s2 + 11:2 * n_s2 + 14], refs[2 * n_s2 + 14:]
        i = pl.program_id(0)
        push = _Push(lambda p: _s2_parts(s2_refs, p), lambda s: [r.at[s] for r in land_refs],
                     (send_sems, recv_sems, loc_sems), n_s2)
        early = _Push(lambda p: [early_ref], lambda s: [early_land.at[s]], tuple(early_sems), 1)
        late = _Push(lambda p: [late_buf], lambda s: [late_land.at[s]], tuple(late_sems), 1)

        def far_copy(n):
            return pltpu.make_async_remote_copy(src_ref=far_ref.at[n], dst_ref=far_land.at[n],
                                                send_sem=far_send_sems.at[n], recv_sem=far_recv_sems.at[n],
                                                device_id=_peer(WIN_FAR[n])[0], device_id_type=MESH_ID)

        @pl.when(i == 0)
        def _():
            for n in range(n_far):
                far_copy(n).start()
            push.start()
            early.start()
            dw_ref[...] = jnp.zeros_like(dw_ref)

        du = _dot_nt(dseg_ref[:, 0:D_MODEL], w_ref[0])
        for j in range(1, N_DEV):
            du = du + _dot_nt(dseg_ref[:, D_MODEL * j:D_MODEL * (j + 1)], w_ref[j])
        dx, dw = _rms_bwd(h0_ref[...], nw_ref[...], du)
        dw_ref[0:1, :] += dw
        dh0 = dh1_ref[...] + dx
        dh0_ref[...] = dh0

        @pl.when(i == 0)
        def _():
            late_buf[8:8 + N_META, :] = dh0[PAD_ROWS:CHUNK, :]

        @pl.when(i == n_i - 1)
        def _():
            late_buf[0:8, :] = dw_ref[...]
            late.start()
            for n in range(n_far):
                far_copy(n).wait_recv()
            for n in range(n_far):
                far_copy(n).wait_send()
            push.wait()
            early.wait()
            late.wait()

    row = pl.BlockSpec((tm, D_MODEL), lambda i: (i, 0))
    vec = pl.BlockSpec((1, D_MODEL), lambda i: (0, 0))
    return pl.pallas_call(
        body, name="in_proj_bwd", grid=(n_i,),
        in_specs=[pl.BlockSpec((tm, N_DEV * D_MODEL), lambda i: (i, 0)),
                  pl.BlockSpec((N_DEV, D_MODEL, D_MODEL), lambda i: (0, 0, 0), pipeline_mode=pl.Buffered(1)),
                  row, vec, row, ANY, ANY] + [ANY] * n_s2,
        out_specs=[row, pl.BlockSpec((8, D_MODEL), lambda i: (0, 0)), ANY] + [ANY] * n_s2 + [ANY, ANY],
        out_shape=[jax.ShapeDtypeStruct((rows, D_MODEL), F32), jax.ShapeDtypeStruct((8, D_MODEL), F32),
                   jax.ShapeDtypeStruct((n_far, D_MODEL, D_MODEL), BF16)] + S2_SHAPES
        + [jax.ShapeDtypeStruct((N_DEV, pack_rows, D_MODEL), F32)] * 2,
        scratch_shapes=_push_sems(n_s2) + [pltpu.SemaphoreType.DMA((n_far,)), pltpu.SemaphoreType.DMA((n_far,)),
                                           pltpu.VMEM((pack_rows, D_MODEL), F32)] + _push_sems(1) + _push_sems(1),
        compiler_params=pltpu.CompilerParams(dimension_semantics=("arbitrary",),
                                             vmem_limit_bytes=VMEM_LIMIT, has_side_effects=True),
    )(dproj, win_g, h0, norm_w, dh1, d_win_far, pack_early, *s2_grads)


def _adamw(g_slots, w, m, v, more_slots=None):
    slots, rows, cols = g_slots.shape
    extra = [] if more_slots is None else [more_slots]
    tr = rows
    for cand in (256, 128, 64, 32, 16, 8):
        if rows % cand == 0 and rows > cand:
            tr = cand
            break

    def body(g_ref, *refs):
        w_ref, m_ref, v_ref, go_ref, d_ref, mo_ref, vo_ref = refs[len(extra):]
        g = g_ref[0].astype(F32)
        for s in range(1, slots):
            g = g + g_ref[s].astype(F32)
        for more_ref in refs[:len(extra)]:
            for s in range(more_ref.shape[0]):
                g = g + more_ref[s].astype(F32)
        m2 = ADAM_B1 * m_ref[...] + (1.0 - ADAM_B1) * g
        v2 = ADAM_B2 * v_ref[...] + (1.0 - ADAM_B2) * (g * g)
        m_hat = m2 / (1.0 - ADAM_B1 ** ADAM_STEP)
        v_hat = v2 / (1.0 - ADAM_B2 ** ADAM_STEP)
        go_ref[...] = g
        d_ref[...] = -ADAM_LR * (m_hat / (jnp.sqrt(v_hat) + ADAM_EPS) + ADAM_WD * w_ref[...])
        mo_ref[...] = m2
        vo_ref[...] = v2

    blk = pl.BlockSpec((tr, cols), lambda i: (i, 0))
    shape = jax.ShapeDtypeStruct((rows, cols), F32)
    return pl.pallas_call(
        body, name="adamw", grid=(rows // tr,),
        in_specs=[pl.BlockSpec((slots, tr, cols), lambda i: (0, i, 0))]
        + [pl.BlockSpec((t.shape[0], tr, cols), lambda i: (0, i, 0)) for t in extra] + [blk, blk, blk],
        out_specs=[blk] * 4, out_shape=[shape] * 4,
        compiler_params=_cparams(("parallel",)),
    )(g_slots, *extra, w, m, v)


def _sum_slots(packs):
    slots, rows, cols = packs.shape

    def body(p_ref, o_ref):
        acc = p_ref[0]
        for s in range(1, slots):
            acc = acc + p_ref[s]
        o_ref[...] = acc

    return pl.pallas_call(
        body, name="sum_slots", out_shape=jax.ShapeDtypeStruct((rows, cols), F32),
        compiler_params=pltpu.CompilerParams(vmem_limit_bytes=VMEM_LIMIT),
    )(packs)


def _gather_small(small):
    shapes = [jax.ShapeDtypeStruct((N_DEV,) + small.shape, F32)]
    return _push_call("gather_small", [small], shapes,
                      lambda ins, p: list(ins), lambda outs, s: [r.at[s] for r in outs])[0]


PACK_CONV_W, PACK_CONV_B, PACK_BA, PACK_BX, PACK_LAM = 0, 4, 5, 6, 7
PACK_FFN_NORM, PACK_SQ_ERR, PACK_FINAL_NORM, PACK_MIX_NORM, PACK_META = 8, 16, 17, 24, 32


def kernel(x, meta_tokens, mix_norm_w, w_in, conv_w, conv_b, lru_wa, lru_ba, lru_wx, lru_bx, lru_lambda, w_branch_ret, w_branch_lru, w_out, ffn_norm_w, w_ffn_in, w_ffn_out, final_norm_w, loss_target, m_meta_tokens, m_mix_norm_w, m_w_in, m_conv_w, m_conv_b, m_lru_wa, m_lru_ba, m_lru_wx, m_lru_bx, m_lru_lambda, m_w_branch_ret, m_w_branch_lru, m_w_out, m_ffn_norm_w, m_w_ffn_in, m_w_ffn_out, m_final_norm_w, v_meta_tokens, v_mix_norm_w, v_w_in, v_conv_w, v_conv_b, v_lru_wa, v_lru_ba, v_lru_wx, v_lru_bx, v_lru_lambda, v_w_branch_ret, v_w_branch_lru, v_w_out, v_ffn_norm_w, v_w_ffn_in, v_w_ffn_out, v_final_norm_w):
    me = _my_index()
    pad4 = ((0, 4), (0, 0))
    fw = final_norm_w.reshape(1, D_MODEL)

    small = jnp.concatenate([meta_tokens, jnp.pad(conv_w[0], pad4)], axis=0)
    small_g = _gather_small(small)
    meta_full = small_g[:, :N_META].transpose(1, 0, 2).reshape(N_META, D_MODEL)
    conv_w_full = small_g[:, N_META:N_META + 4].transpose(1, 0, 2).reshape(4, D_MODEL)
    mixer_shards = [w_branch_ret[0].astype(BF16), w_branch_lru[0].astype(BF16), w_out[0].astype(BF16),
                    lru_wa[0].astype(BF16), lru_wx[0].astype(BF16)]
    wfi_shard = jnp.pad(w_ffn_in[0].astype(BF16), ((0, 0), (0, FFN_GROUP - FFN_SHARD)))
    own_slot = lambda ins, p: list(ins)
    part_of_owner = lambda ins, p: [r.at[p] for r in ins]

    rows = x.shape[1] + CHUNK
    h0 = jnp.concatenate([jnp.zeros((PAD_ROWS, D_MODEL), F32), meta_full, x[0]], axis=0)
    cos2, sin2 = _rope_tables(rows)
    dec = _retention_consts()

    me_arr = me.astype(jnp.int32).reshape(1)
    proj, u, win_g = _in_proj(h0, mix_norm_w, w_in[0].astype(BF16), me_arr)
    o, zr, states, wbr_g, wbl_g, wout_g, wa_g, wx_g = _retention_fwd(
        proj, cos2, sin2, dec, _mixer_weights_ride(mixer_shards))
    wbr, wbl, wout = (t.reshape(D_MODEL, D_MODEL) for t in (wbr_g, wbl_g, wout_g))
    wa_g, wx_g = _from_owners(wa_g), _from_owners(wx_g)
    gather_wfi = _Ride([wfi_shard], [jax.ShapeDtypeStruct((N_DEV, D_MODEL, FFN_GROUP), BF16)],
                       own_slot, _slot_of_sender, gather_by_chip=True)
    hs, zl, cri, wfi_g = _lru_fwd(proj, conv_w_full, conv_b, lru_ba, lru_bx, lru_lambda, wa_g, wx_g, gather_wfi)
    h1, yr, yl, mixed, wfo_g = _mix_fwd(zr, zl, proj, h0, wbr, wbl, wout, _wfo_ride(w_ffn_out[0].astype(BF16)))
    u2, g, up, act, dh2, red = _ffn_fwd_loss(h1, ffn_norm_w, wfi_g, wfo_g, fw, loss_target[0])

    d_wfo = _wgrad(act, dh2, FFN_GROUP, D_MODEL, BF16)[:, 0]
    dgu, dh1, dw_ffn_norm = _ffn_bwd(dh2, g, up, h1, ffn_norm_w, wfi_g, wfo_g)
    d_wfi = _wgrad(u2, dgu, D_MODEL, FFN_GROUP, BF16, b_halves=True)[0]
    d_wout = _wgrad(mixed, dh1, D_MODEL, D_MODEL, BF16)[0, 0]
    dyr, dyl, dproj, dzr, dzl = _mix_bwd(dh1, yr, yl, proj, wbr, wbl, wout)
    d_wbr = _wgrad(zr, dyr, D_MODEL, D_MODEL, BF16)[0, 0]
    d_wbl = _wgrad(zl, dyl, D_MODEL, D_MODEL, BF16)[0, 0]
    dproj, d_wa, d_wx, lru_small, r_fi, r_fo = _lru_bwd(
        dzl, hs, cri, proj, dproj, conv_w_full, lru_lambda, wa_g, wx_g, [d_wfi, d_wfo])
    mix_shape = jax.ShapeDtypeStruct((N_DEV, D_MODEL // N_DEV, D_MODEL), BF16)
    scatter_mix = _Ride([t.reshape(mix_shape.shape) for t in (d_wbr, d_wbl, d_wout)], [mix_shape] * 3,
                        part_of_owner, _slot_of_sender)
    dproj, r_br, r_bl, r_out = _retention_bwd(dzr, o, proj, states, cos2, sin2, dec, dproj, scatter_mix)
    d_win_far, r_in = _wgrad_w_in(u, dproj, me_arr)
    pack_early = jnp.concatenate([lru_small, dw_ffn_norm, red], axis=0)
    dh0, _, r_in_far, r_wa, r_wx, packs_early, packs_late = _in_proj_bwd(
        dproj, win_g, h0, mix_norm_w, dh1, d_win_far, [_by_owner(d_wa), _by_owner(d_wx)], pack_early)
    grad_x = dh0[CHUNK:]

    small_sum = jnp.concatenate([_sum_slots(packs_early), _sum_slots(packs_late)], axis=0)
    loss = (0.5 / D_MODEL) * jnp.sum(small_sum[PACK_SQ_ERR])

    def big_update(slots, w, m, v, more_slots=None):
        shape = w.shape
        w2, m2, v2 = (t.reshape(slots.shape[1:]) for t in (w, m, v))
        return [t.reshape(shape) for t in _adamw(slots, w2, m2, v2, more_slots)]

    res = {}
    res["w_in"] = big_update(r_in, w_in, m_w_in, v_w_in, r_in_far)
    res["w_branch_ret"] = big_update(r_br, w_branch_ret, m_w_branch_ret, v_w_branch_ret)
    res["w_branch_lru"] = big_update(r_bl, w_branch_lru, m_w_branch_lru, v_w_branch_lru)
    res["w_out"] = big_update(r_out, w_out, m_w_out, v_w_out)
    res["w_ffn_in"] = big_update(r_fi[:, :, :FFN_SHARD], w_ffn_in, m_w_ffn_in, v_w_ffn_in)
    res["w_ffn_out"] = big_update(r_fo, w_ffn_out, m_w_ffn_out, v_w_ffn_out)
    res["lru_wa"] = big_update(r_wa.reshape(N_DEV, LRU_BLOCKS * LRU_ROWS, LRU_BLOCK), lru_wa, m_lru_wa, v_lru_wa)
    res["lru_wx"] = big_update(r_wx.reshape(N_DEV, LRU_BLOCKS * LRU_ROWS, LRU_BLOCK), lru_wx, m_lru_wx, v_lru_wx)

    col = me * HEAD_DIM
    g_meta = lax.dynamic_slice(small_sum, (PACK_META, col), (N_META, HEAD_DIM))
    g_conv = lax.dynamic_slice(small_sum, (PACK_CONV_W, col), (8, HEAD_DIM))
    small_names = ["mix_norm_w", "conv_b", "lru_ba", "lru_bx", "lru_lambda", "ffn_norm_w", "final_norm_w"]
    small_rows = [PACK_MIX_NORM, PACK_CONV_B, PACK_BA, PACK_BX, PACK_LAM, PACK_FFN_NORM, PACK_FINAL_NORM]
    small_w = [mix_norm_w, conv_b, lru_ba, lru_bx, lru_lambda, ffn_norm_w, fw]
    small_m = [m_mix_norm_w, m_conv_b, m_lru_ba, m_lru_bx, m_lru_lambda, m_ffn_norm_w, m_final_norm_w.reshape(1, -1)]
    small_v = [v_mix_norm_w, v_conv_b, v_lru_ba, v_lru_bx, v_lru_lambda, v_ffn_norm_w, v_final_norm_w.reshape(1, -1)]

    def pack_small(vec_list, meta_t, conv_t):
        return jnp.concatenate([t.reshape(8, HEAD_DIM) for t in vec_list] + [meta_t, jnp.pad(conv_t[0], pad4)], axis=0)

    g_small = jnp.concatenate([small_sum[r].reshape(8, HEAD_DIM) for r in small_rows] + [g_meta, g_conv], axis=0)
    outs_small = _adamw(g_small[None], pack_small(small_w, meta_tokens, conv_w),
                        pack_small(small_m, m_meta_tokens, m_conv_w), pack_small(small_v, v_meta_tokens, v_conv_w))
    for idx, name in enumerate(small_names):
        shape = final_norm_w.shape if name == "final_norm_w" else (1, D_MODEL)
        res[name] = [t[8 * idx:8 * idx + 8].reshape(shape) for t in outs_small]
    res["meta_tokens"] = [t[56:72] for t in outs_small]
    res["conv_w"] = [t[72:76].reshape(1, 4, HEAD_DIM) for t in outs_small]

    order = ["meta_tokens", "mix_norm_w", "w_in", "conv_w", "conv_b", "lru_wa", "lru_ba", "lru_wx", "lru_bx",
             "lru_lambda", "w_branch_ret", "w_branch_lru", "w_out", "ffn_norm_w", "w_ffn_in", "w_ffn_out",
             "final_norm_w"]
    out = [loss, grad_x[None]]
    for kind in range(4):
        out += [res[name][kind] for name in order]
    return tuple(out)
```

```python
import jax
import jax.numpy as jnp
from jax import lax
from jax.experimental import pallas as pl
from jax.experimental.pallas import tpu as pltpu

F32 = jnp.float32
BF16 = jnp.bfloat16

D_MODEL = 1024
N_META = 16
CHUNK = 128
PAD_ROWS = CHUNK - N_META
HEADS = 8
HEAD_DIM = 128
ROPE_BASE = 10000.0
QK_SCALE = HEAD_DIM ** -0.5
LRU_BLOCKS = 4
LRU_BLOCK = 256
LRU_C = 8.0
FFN_HIDDEN = 2816
N_DEV = 8
FFN_SHARD = 2 * FFN_HIDDEN // N_DEV
FFN_GROUP = 768
FFN_GROUPS = 4
FFN_OUT_SHARD = FFN_HIDDEN // N_DEV
NORM_EPS = 1e-6

ADAM_LR = 0.001
ADAM_B1 = 0.9
ADAM_B2 = 0.999
ADAM_EPS = 1e-08
ADAM_WD = 0.01
ADAM_STEP = 10

VMEM_LIMIT = 56 * 1024 * 1024
MESH_ID = pl.DeviceIdType.MESH
ANY = pl.BlockSpec(memory_space=pl.ANY)


def _cparams(sem):
    return pltpu.CompilerParams(dimension_semantics=sem, vmem_limit_bytes=VMEM_LIMIT)


def _tile(rows, cap):
    t = cap - cap % 64
    while rows % t:
        t -= 64
    return t


def _dot(a, b):
    return jnp.dot(a, b, preferred_element_type=F32)


def _dot_nt(a, b):
    return lax.dot_general(a, b, (((1,), (1,)), ((), ())), preferred_element_type=F32)


def _dot_tn(a, b):
    return lax.dot_general(a, b, (((0,), (0,)), ((), ())), preferred_element_type=F32)


def _sigmoid(x):
    return 0.5 * jnp.tanh(0.5 * x) + 0.5


def _gelu_parts(x):
    k = 0.7978845608028654
    inner = k * (x + 0.044715 * x * x * x)
    t = jnp.tanh(inner)
    g = 0.5 * x * (1.0 + t)
    dg = 0.5 * (1.0 + t) + 0.5 * x * (1.0 - t * t) * k * (1.0 + 3.0 * 0.044715 * x * x)
    return g, dg


def _rot(x, cos2, sin2):
    return x * cos2 + pltpu.roll(x, HEAD_DIM // 2, 1) * sin2


def _rot_t(dx, cos2, sin2):
    return dx * cos2 - pltpu.roll(dx, HEAD_DIM // 2, 1) * sin2


def _rms_bwd(x, w, dy):
    rs = lax.rsqrt(jnp.mean(x * x, axis=-1, keepdims=True) + NORM_EPS)
    nh = x * rs
    dw = jnp.sum(dy * nh, axis=0, keepdims=True)
    dn = dy * w
    dx = rs * (dn - nh * jnp.mean(dn * nh, axis=-1, keepdims=True))
    return dx, dw


def _retention_consts():
    h = jnp.arange(HEADS, dtype=F32)
    log_g = jnp.log(1.0 - 2.0 ** (-5.0 - h))
    idx = jnp.arange(CHUNK, dtype=F32)
    diff = idx[:, None] - idx[None, :]
    intra = jnp.where(diff[None] >= 0, jnp.exp(jnp.maximum(diff, 0.0)[None] * log_g[:, None, None]), 0.0)
    q_decay = jnp.exp((idx + 1.0)[:, None] * log_g[None, :])
    k_decay = jnp.exp((CHUNK - 1.0 - idx)[:, None] * log_g[None, :])
    chunk_decay = jnp.exp(CHUNK * log_g)
    shape = (HEADS, CHUNK, CHUNK)
    qd = jnp.broadcast_to(q_decay.T[:, :, None], shape)
    kd = jnp.broadcast_to(k_decay.T[:, :, None], shape)
    cd = jnp.broadcast_to(chunk_decay[:, None, None], shape)
    return jnp.stack([intra, qd, kd, cd])


def _rope_tables(rows):
    pos = jnp.maximum(jnp.arange(rows) - PAD_ROWS, 0).astype(F32)
    inv_freq = ROPE_BASE ** (-jnp.arange(0, HEAD_DIM, 2, dtype=F32) / HEAD_DIM)
    ang = pos[:, None] * inv_freq[None, :]
    cos, sin = jnp.cos(ang), jnp.sin(ang)
    return jnp.concatenate([cos, cos], axis=1), jnp.concatenate([-sin, sin], axis=1)


def _my_index():
    return 4 * lax.axis_index("x") + 2 * lax.axis_index("y") + lax.axis_index("c")


def _peer(k):
    x, y, c = lax.axis_index("x"), lax.axis_index("y"), lax.axis_index("c")
    px = 1 - x if k & 4 else x
    py = 1 - y if k & 2 else y
    pc = 1 - c if k & 1 else c
    return (px, py, pc), 4 * px + 2 * py + pc


def _push_sems(n_arr):
    n_rem = (N_DEV - 1) * n_arr
    return [pltpu.SemaphoreType.DMA((n_rem,)), pltpu.SemaphoreType.DMA((n_rem,)), pltpu.SemaphoreType.DMA((n_arr,))]


class _Push:
    def __init__(self, send_part, land_slot, sems, n_arr):
        self.send_part, self.land_slot, self.n_arr = send_part, land_slot, n_arr
        self.send_sems, self.recv_sems, self.loc_sems = sems

    def _remote(self, k, a, src, dst, pos):
        idx = (k - 1) * self.n_arr + a
        return pltpu.make_async_remote_copy(src_ref=src, dst_ref=dst, send_sem=self.send_sems.at[idx],
                                            recv_sem=self.recv_sems.at[idx], device_id=pos, device_id_type=MESH_ID)

    def _outgoing(self):
        me = _my_index()
        land = self.land_slot(me)
        remote = []
        for k in range(1, N_DEV):
            pos, p = _peer(k)
            src = self.send_part(p)
            remote += [self._remote(k, a, src[a], land[a], pos) for a in range(self.n_arr)]
        own = self.send_part(me)
        local = [pltpu.make_async_copy(own[a], land[a], self.loc_sems.at[a]) for a in range(self.n_arr)]
        return remote, local

    def start(self):
        remote, local = self._outgoing()
        for cp in remote + local:
            cp.start()

    def wait_recv_from(self, k):
        own = self.send_part(_my_index())
        pos, p = _peer(k)
        land = self.land_slot(p)
        for a in range(self.n_arr):
            self._remote(k, a, own[a], land[a], pos).wait_recv()

    def wait_sends(self):
        remote, local = self._outgoing()
        for cp in remote:
            cp.wait_send()
        for cp in local:
            cp.wait()

    def wait(self):
        for k in range(1, N_DEV):
            self.wait_recv_from(k)
        self.wait_sends()


DIRECT = (1, 2, 4, 6)
RELAYED = (2, 4, 6)


def _gather_by_chip_sems(n_arr):
    direct, relayed = len(DIRECT) * n_arr, len(RELAYED) * n_arr
    return [pltpu.SemaphoreType.DMA((direct,)), pltpu.SemaphoreType.DMA((direct,)),
            pltpu.SemaphoreType.DMA((relayed,)), pltpu.SemaphoreType.DMA((relayed,)), pltpu.SemaphoreType.DMA((n_arr,))]


class _GatherByChip:
    def __init__(self, srcs, land_slot, sems, n_arr):
        self.srcs, self.land_slot, self.n_arr = srcs, land_slot, n_arr
        self.send_sems, self.recv_sems, self.relay_send_sems, self.relay_recv_sems, self.loc_sems = sems

    def _direct(self, k, a, slot):
        idx = DIRECT.index(k) * self.n_arr + a
        return pltpu.make_async_remote_copy(src_ref=self.srcs[a], dst_ref=self.land_slot(slot)[a],
                                            send_sem=self.send_sems.at[idx], recv_sem=self.recv_sems.at[idx],
                                            device_id=_peer(k)[0], device_id_type=MESH_ID)

    def _relay(self, q, a, slot):
        idx = RELAYED.index(q) * self.n_arr + a
        block = self.land_slot(slot)[a]
        return pltpu.make_async_remote_copy(src_ref=block, dst_ref=block, send_sem=self.relay_send_sems.at[idx],
                                            recv_sem=self.relay_recv_sems.at[idx], device_id=_peer(1)[0],
                                            device_id_type=MESH_ID)

    def _own(self, a):
        return pltpu.make_async_copy(self.srcs[a], self.land_slot(_my_index())[a], self.loc_sems.at[a])

    def start(self):
        me = _my_index()
        for k in DIRECT:
            for a in range(self.n_arr):
                self._direct(k, a, me).start()
        for a in range(self.n_arr):
            self._own(a).start()

    def relay(self):
        for q in RELAYED:
            p = _peer(q)[1]
            for a in range(self.n_arr):
                self._direct(q, a, p).wait_recv()
                self._relay(q, a, p).start()

    def wait(self):
        me = _my_index()
        for a in range(self.n_arr):
            self._direct(1, a, _peer(1)[1]).wait_recv()
        for q in RELAYED:
            for a in range(self.n_arr):
                self._relay(q, a, _peer(q + 1)[1]).wait_recv()
        for k in DIRECT:
            for a in range(self.n_arr):
                self._direct(k, a, me).wait_send()
        for q in RELAYED:
            for a in range(self.n_arr):
                self._relay(q, a, _peer(q)[1]).wait_send()
        for a in range(self.n_arr):
            self._own(a).wait()


class _Ride:
    def __init__(self, arrays, out_shapes, send_part, land_slot, zero_dsts=None, zero_shape=None, n_zero=0,
                 gather_by_chip=False):
        self.arrays, self.out_shapes = list(arrays), list(out_shapes)
        self.send_part, self.land_slot, self.n = send_part, land_slot, len(arrays)
        self.zero_dsts, self.zero_shape, self.n_zero = zero_dsts, zero_shape, n_zero
        self.gather_by_chip = gather_by_chip

    def specs(self):
        return [ANY] * self.n

    def scratch(self):
        extra = [pltpu.SemaphoreType.DMA((self.n_zero,)), pltpu.VMEM(self.zero_shape, BF16)] if self.n_zero else []
        sems = _gather_by_chip_sems(self.n) if self.gather_by_chip else _push_sems(self.n)
        return sems + extra

    def push(self, in_refs, out_refs, scratch):
        ride = self
        n_sems = 5 if self.gather_by_chip else 3
        land = lambda s: ride.land_slot(out_refs, s)
        if self.gather_by_chip:
            push = _GatherByChip(list(in_refs), land, tuple(scratch[:n_sems]), self.n)
        else:
            push = _Push(lambda p: ride.send_part(in_refs, p), land, tuple(scratch[:n_sems]), self.n)

        class Both:
            def _fills(self):
                if not ride.n_zero:
                    return []
                zsems, zbuf = scratch[n_sems], scratch[n_sems + 1]
                return [pltpu.make_async_copy(zbuf, dst, zsems.at[z]) for z, dst in enumerate(ride.zero_dsts(out_refs))]

            def start(self):
                push.start()
                if ride.n_zero:
                    scratch[n_sems + 1][...] = jnp.zeros(ride.zero_shape, BF16)
                for cp in self._fills():
                    cp.start()

            def relay(self):
                if ride.gather_by_chip:
                    push.relay()

            def wait(self):
                push.wait()
                for cp in self._fills():
                    cp.wait()

        return Both()


def _slot_of_sender(out_refs, s):
    return [r.at[s] for r in out_refs]


def _push_call(name, arrays, out_shapes, send_part, land_slot):
    n_arr = len(arrays)

    def body(*refs):
        ins, outs, sems = refs[:n_arr], refs[n_arr:2 * n_arr], refs[2 * n_arr:]
        push = _Push(lambda p: send_part(ins, p), lambda s: land_slot(outs, s), sems, n_arr)
        push.start()
        push.wait()

    return pl.pallas_call(
        body, name=name, in_specs=[ANY] * n_arr, out_specs=[ANY] * n_arr, out_shape=out_shapes,
        scratch_shapes=_push_sems(n_arr), compiler_params=pltpu.CompilerParams(has_side_effects=True),
    )(*arrays)


LRU_ROWS = LRU_BLOCK // N_DEV
FFN_PAD_ROWS = FFN_GROUP - 2 * FFN_OUT_SHARD


def _half_rows(d):
    return pl.ds(pl.multiple_of((d % 2) * FFN_OUT_SHARD, 16), FFN_OUT_SHARD)


MIXER_SHAPES = [
    jax.ShapeDtypeStruct((N_DEV, D_MODEL // N_DEV, D_MODEL), BF16),
    jax.ShapeDtypeStruct((N_DEV, D_MODEL // N_DEV, D_MODEL), BF16),
    jax.ShapeDtypeStruct((N_DEV, D_MODEL // N_DEV, D_MODEL), BF16),
    jax.ShapeDtypeStruct((N_DEV, LRU_BLOCKS, LRU_ROWS, LRU_BLOCK), BF16),
    jax.ShapeDtypeStruct((N_DEV, LRU_BLOCKS, LRU_ROWS, LRU_BLOCK), BF16),
]


def _by_owner(t):
    return t.reshape(LRU_BLOCKS, N_DEV, LRU_ROWS, LRU_BLOCK).transpose(1, 0, 2, 3)


def _from_owners(t):
    return t.transpose(1, 0, 2, 3).reshape(LRU_BLOCKS, LRU_BLOCK, LRU_BLOCK)


def _mixer_weights_ride(shards):
    return _Ride(shards, MIXER_SHAPES, lambda ins, p: list(ins), _slot_of_sender, gather_by_chip=True)


def _wfo_ride(shard):
    zero_dsts = lambda outs: [outs[0].at[g, pl.ds(2 * FFN_OUT_SHARD, FFN_PAD_ROWS), :] for g in range(FFN_GROUPS)]
    return _Ride([shard], [jax.ShapeDtypeStruct((FFN_GROUPS, FFN_GROUP, D_MODEL), BF16)], lambda ins, p: list(ins),
                 lambda outs, d: [outs[0].at[d // 2, _half_rows(d), :]], zero_dsts, (FFN_PAD_ROWS, D_MODEL), FFN_GROUPS,
                 gather_by_chip=True)


W_IN_USE_ORDER = (0, 1, 2, 4, 6, 3, 5, 7)


def _arrival_rank_to_relation(jj):
    k = W_IN_USE_ORDER[-1]
    for pos in reversed(range(N_DEV - 1)):
        k = jnp.where(jj == pos, W_IN_USE_ORDER[pos], k)
    return k


def _in_proj(h0, norm_w, win_shard, me_arr):
    rows = h0.shape[0]
    tm = _tile(rows, 1664)
    n_i = rows // tm

    direct, relayed = DIRECT, RELAYED

    def body(me_ref, h_ref, nw_ref, wsh_ref, proj_ref, u_ref, wing_ref, u_all, wbuf, copy_sem,
             send_sems, recv_sems, relay_send_sems, relay_recv_sems, own_sem):
        del me_ref
        jj, i = pl.program_id(0), pl.program_id(1)
        me = _my_index()
        sibling = _peer(1)[0]

        def direct_copy(k, slot):
            n = direct.index(k)
            return pltpu.make_async_remote_copy(src_ref=wsh_ref, dst_ref=wing_ref.at[slot], send_sem=send_sems.at[n],
                                                recv_sem=recv_sems.at[n], device_id=_peer(k)[0], device_id_type=MESH_ID)

        def relay_copy(q, slot):
            n = relayed.index(q)
            return pltpu.make_async_remote_copy(src_ref=wing_ref.at[slot], dst_ref=wing_ref.at[slot],
                                                send_sem=relay_send_sems.at[n], recv_sem=relay_recv_sems.at[n],
                                                device_id=sibling, device_id_type=MESH_ID)

        own_slot = pltpu.make_async_copy(wsh_ref, wing_ref.at[me], own_sem)

        @pl.when(jnp.logical_and(jj == 0, i == 0))
        def _():
            for k in direct:
                direct_copy(k, me).start()
            own_slot.start()
            own = pltpu.make_async_copy(wsh_ref, wbuf, copy_sem)
            own.start()
            own.wait()

        for k in range(1, N_DEV):
            rank = W_IN_USE_ORDER.index(k)

            @pl.when(jnp.logical_and(jj == rank, i == 0))
            def _(k=k):
                p = _peer(k)[1]
                if k in direct:
                    direct_copy(k, p).wait_recv()
                    if k in relayed:
                        relay_copy(k, p).start()
                else:
                    relay_copy(k - 1, p).wait_recv()
                landed = pltpu.make_async_copy(wing_ref.at[p], wbuf, copy_sem)
                landed.start()
                landed.wait()

        rows_i = pl.ds(pl.multiple_of(i * tm, tm), tm)

        @pl.when(jj == 0)
        def _():
            x = h_ref[...]
            rs = lax.rsqrt(jnp.mean(x * x, axis=-1, keepdims=True) + NORM_EPS)
            u = (x * rs * nw_ref[...]).astype(BF16)
            u_all[rows_i, :] = u
            u_ref[...] = u
        proj_ref[...] = _dot(u_all[rows_i, :], wbuf[...]).astype(BF16)

        @pl.when(jnp.logical_and(jj == N_DEV - 1, i == n_i - 1))
        def _():
            for k in direct:
                direct_copy(k, me).wait_send()
            for q in relayed:
                relay_copy(q, _peer(q)[1]).wait_send()
            own_slot.wait()

    first_pass = lambda jj, i: jnp.where(jj == 0, i, n_i - 1)
    grid_spec = pltpu.PrefetchScalarGridSpec(
        num_scalar_prefetch=1, grid=(N_DEV, n_i),
        in_specs=[pl.BlockSpec((tm, D_MODEL), lambda jj, i, me: (first_pass(jj, i), 0)),
                  pl.BlockSpec((1, D_MODEL), lambda jj, i, me: (0, 0)), ANY],
        out_specs=[pl.BlockSpec((tm, D_MODEL), lambda jj, i, me: (i, me[0] ^ _arrival_rank_to_relation(jj))),
                   pl.BlockSpec((tm, D_MODEL), lambda jj, i, me: (first_pass(jj, i), 0)), ANY],
        scratch_shapes=[pltpu.VMEM((rows, D_MODEL), BF16), pltpu.VMEM((D_MODEL, D_MODEL), BF16),
                        pltpu.SemaphoreType.DMA(()),
                        pltpu.SemaphoreType.DMA((len(direct),)), pltpu.SemaphoreType.DMA((len(direct),)),
                        pltpu.SemaphoreType.DMA((len(relayed),)), pltpu.SemaphoreType.DMA((len(relayed),)),
                        pltpu.SemaphoreType.DMA(())])
    return pl.pallas_call(
        body, name="in_proj", grid_spec=grid_spec,
        out_shape=[jax.ShapeDtypeStruct((rows, N_DEV * D_MODEL), BF16),
                   jax.ShapeDtypeStruct((rows, D_MODEL), BF16),
                   jax.ShapeDtypeStruct((N_DEV, D_MODEL, D_MODEL), BF16)],
        compiler_params=pltpu.CompilerParams(dimension_semantics=("arbitrary", "arbitrary"),
                                             vmem_limit_bytes=VMEM_LIMIT, has_side_effects=True),
    )(me_arr, h0, norm_w, win_shard)


def _seg_spec(rows_per_block, seg):
    return pl.BlockSpec((rows_per_block, D_MODEL), lambda n, seg=seg: (n, seg))


def _chunks_per_step(n_chunks):
    return next(c for c in (5, 3, 2, 1) if n_chunks % c == 0)


def _retention_fwd(proj, cos2, sin2, dec, ride):
    rows = proj.shape[0]
    n_chunks = rows // CHUNK
    per_step = _chunks_per_step(n_chunks)
    n_steps = n_chunks // per_step
    tm = per_step * CHUNK
    n_r = ride.n

    def body(q_ref, k_ref, v_ref, g_ref, c_ref, s_ref, dec_ref, *refs):
        o_ref, zr_ref, st_ref = refs[n_r:n_r + 3]
        state = refs[2 * n_r + 3]
        push = ride.push(refs[:n_r], refs[n_r + 3:2 * n_r + 3], refs[2 * n_r + 4:])

        @pl.when(pl.program_id(0) == 0)
        def _():
            push.start()
            state[...] = jnp.zeros_like(state)

        for h in range(HEADS):
            sl = slice(HEAD_DIM * h, HEAD_DIM * (h + 1))
            st = state[h]
            for c in range(per_step):
                rw = slice(CHUNK * c, CHUNK * (c + 1))
                cos_t, sin_t = c_ref[rw, :], s_ref[rw, :]
                qh = _rot(q_ref[rw, sl].astype(F32), cos_t, sin_t)
                kh = _rot(k_ref[rw, sl].astype(F32), cos_t, sin_t) * QK_SCALE
                qb, kb, vb = qh.astype(BF16), kh.astype(BF16), v_ref[rw, sl]
                s = _dot_nt(qb, kb) * dec_ref[0, h]
                st_ref[c, h] = st
                o = _dot(s.astype(BF16), vb) + _dot(qb, st.astype(BF16)) * dec_ref[1, h]
                st = st * dec_ref[3, h] + _dot_tn((kh * dec_ref[2, h]).astype(BF16), vb)
                o_ref[rw, sl] = o.astype(BF16)
                r = lax.rsqrt(jnp.mean(o * o, axis=-1, keepdims=True) + NORM_EPS)
                g = g_ref[rw, sl].astype(F32)
                zr_ref[rw, sl] = (g * _sigmoid(g) * (o * r)).astype(BF16)
            state[h] = st

        @pl.when(pl.program_id(0) == n_steps // 2)
        def _():
            push.relay()

        @pl.when(pl.program_id(0) == n_steps - 1)
        def _():
            push.wait()

    tab = pl.BlockSpec((tm, HEAD_DIM), lambda n: (n, 0))
    return pl.pallas_call(
        body, name="retention_fwd", grid=(n_steps,),
        in_specs=[_seg_spec(tm, 0), _seg_spec(tm, 1), _seg_spec(tm, 2), _seg_spec(tm, 3), tab, tab,
                  pl.BlockSpec((4, HEADS, CHUNK, CHUNK), lambda n: (0, 0, 0, 0))] + ride.specs(),
        out_specs=[pl.BlockSpec((tm, D_MODEL), lambda n: (n, 0)),
                   pl.BlockSpec((tm, D_MODEL), lambda n: (n, 0)),
                   pl.BlockSpec((per_step, HEADS, HEAD_DIM, HEAD_DIM), lambda n: (n, 0, 0, 0))] + ride.specs(),
        out_shape=[jax.ShapeDtypeStruct((rows, D_MODEL), BF16),
                   jax.ShapeDtypeStruct((rows, D_MODEL), BF16),
                   jax.ShapeDtypeStruct((n_chunks, HEADS, HEAD_DIM, HEAD_DIM), F32)] + ride.out_shapes,
        scratch_shapes=[pltpu.VMEM((HEADS, HEAD_DIM, HEAD_DIM), F32)] + ride.scratch(),
        compiler_params=pltpu.CompilerParams(dimension_semantics=("arbitrary",), vmem_limit_bytes=VMEM_LIMIT,
                                             has_side_effects=True),
    )(proj, proj, proj, proj, cos2, sin2, dec, *ride.arrays)


def _lru_gates(c, ba, bx, wa_ref, wx_ref):
    pre_r, pre_i = [], []
    for g in range(LRU_BLOCKS):
        cg = c[:, LRU_BLOCK * g:LRU_BLOCK * (g + 1)].astype(BF16)
        pre_r.append(_dot(cg, wa_ref[g]))
        pre_i.append(_dot(cg, wx_ref[g]))
    return _sigmoid(jnp.concatenate(pre_r, axis=1) + ba), _sigmoid(jnp.concatenate(pre_i, axis=1) + bx)


def _lru_decay(r, lam):
    sp = jnp.maximum(-lam, 0.0) + jnp.log(1.0 + jnp.exp(-jnp.abs(lam)))
    log_a = -LRU_C * r * sp
    a = jnp.exp(log_a)
    one_minus_a2 = -jnp.tanh(log_a) * (a * a + 1.0)
    inv_mult = lax.rsqrt(jnp.maximum(one_minus_a2, 1e-30))
    return a, one_minus_a2 * inv_mult, inv_mult, sp


def _conv_taps(xbuf, tm, cw_ref, cb_ref):
    c = cb_ref[...] + cw_ref[3:4, :] * xbuf[8:8 + tm, :]
    for back in (1, 2, 3):
        c = c + cw_ref[3 - back:4 - back, :] * xbuf[8 - back:8 - back + tm, :]
    return c


def _lru_fwd(proj, conv_w, conv_b, ba, bx, lam, wa_g, wx_g, ride):
    rows = proj.shape[0]
    tm = _tile(rows, 320)
    n_t = rows // tm
    n_r = ride.n

    def body(x_ref, gt_ref, cw_ref, cb_ref, ba_ref, bx_ref, lam_ref, wa_ref, wx_ref, *refs):
        hs_ref, zl_ref, cri_ref = refs[n_r:n_r + 3]
        xbuf, abuf, ubuf, hcar = refs[2 * n_r + 3:2 * n_r + 7]
        push = ride.push(refs[:n_r], refs[n_r + 3:2 * n_r + 3], refs[2 * n_r + 7:])
        i = pl.program_id(0)

        @pl.when(i == 0)
        def _():
            push.start()
            xbuf[0:8, :] = jnp.zeros((8, D_MODEL), F32)
            hcar[...] = jnp.zeros_like(hcar)

        xbuf[8:8 + tm, :] = x_ref[...].astype(F32)
        c = _conv_taps(xbuf, tm, cw_ref, cb_ref)
        xbuf[0:8, :] = xbuf[tm:tm + 8, :]
        r, ig = _lru_gates(c, ba_ref[...], bx_ref[...], wa_ref, wx_ref)
        a, mult, _, _ = _lru_decay(r, lam_ref[...])
        cri_ref[0] = c.astype(BF16)
        cri_ref[1] = r.astype(BF16)
        cri_ref[2] = ig.astype(BF16)
        row = i * tm + lax.broadcasted_iota(jnp.int32, (tm, 1), 0)
        abuf[...] = a
        ubuf[...] = jnp.where(row >= PAD_ROWS, mult * (ig * c), 0.0)

        sub = lax.broadcasted_iota(jnp.int32, (8, D_MODEL), 0)

        def block(b, carry):
            off = pl.multiple_of(b * 8, 8)
            av, uv = abuf[pl.ds(off, 8), :], ubuf[pl.ds(off, 8), :]
            for s in (1, 2, 4):
                us = jnp.where(sub >= s, pltpu.roll(uv, s, 0), 0.0)
                as_ = jnp.where(sub >= s, pltpu.roll(av, s, 0), 1.0)
                uv = uv + av * us
                av = av * as_
            hv = uv + av * carry
            ubuf[pl.ds(off, 8), :] = hv
            return hv[7:8, :]

        hcar[...] = lax.fori_loop(0, tm // 8, block, hcar[...])
        gl, _ = _gelu_parts(gt_ref[...].astype(F32))
        hs = ubuf[...]
        hs_ref[...] = hs.astype(BF16)
        zl_ref[...] = (gl * hs).astype(BF16)

        @pl.when(i == n_t // 2)
        def _():
            push.relay()

        @pl.when(i == n_t - 1)
        def _():
            push.wait()

    vec = pl.BlockSpec((1, D_MODEL), lambda i: (0, 0))
    mat = pl.BlockSpec((LRU_BLOCKS, LRU_BLOCK, LRU_BLOCK), lambda i: (0, 0, 0))
    row = pl.BlockSpec((tm, D_MODEL), lambda i: (i, 0))
    return pl.pallas_call(
        body, name="lru_fwd", grid=(n_t,),
        in_specs=[_seg_spec(tm, 4), _seg_spec(tm, 5), pl.BlockSpec((4, D_MODEL), lambda i: (0, 0)),
                  vec, vec, vec, vec, mat, mat] + ride.specs(),
        out_specs=[row, row, pl.BlockSpec((3, tm, D_MODEL), lambda i: (0, i, 0))] + ride.specs(),
        out_shape=[jax.ShapeDtypeStruct((rows, D_MODEL), BF16)] * 2
        + [jax.ShapeDtypeStruct((3, rows, D_MODEL), BF16)] + ride.out_shapes,
        scratch_shapes=[pltpu.VMEM((tm + 8, D_MODEL), F32), pltpu.VMEM((tm, D_MODEL), F32),
                        pltpu.VMEM((tm, D_MODEL), F32), pltpu.VMEM((1, D_MODEL), F32)] + ride.scratch(),
        compiler_params=pltpu.CompilerParams(dimension_semantics=("arbitrary",), vmem_limit_bytes=VMEM_LIMIT,
                                             has_side_effects=True),
    )(proj, proj, conv_w, conv_b, ba, bx, lam, wa_g, wx_g, *ride.arrays)


def _mix_fwd(zr, zl, proj, h0, wbr, wbl, wout, ride):
    rows = h0.shape[0]
    tm = _tile(rows, 640)
    n_t = rows // tm
    n_r = ride.n

    def body(zr_ref, zl_ref, ga_ref, gb_ref, h0_ref, wbr_ref, wbl_ref, wo_ref, *refs):
        h1_ref, yr_ref, yl_ref, mx_ref = refs[n_r:n_r + 4]
        push = ride.push(refs[:n_r], refs[n_r + 4:2 * n_r + 4], refs[2 * n_r + 4:])

        @pl.when(pl.program_id(0) == 0)
        def _():
            push.start()

        yr = _dot(zr_ref[...], wbr_ref[...])
        yl = _dot(zl_ref[...], wbl_ref[...])
        mixed = (_sigmoid(ga_ref[...].astype(F32)) * yr + _sigmoid(gb_ref[...].astype(F32)) * yl).astype(BF16)
        yr_ref[...] = yr.astype(BF16)
        yl_ref[...] = yl.astype(BF16)
        mx_ref[...] = mixed
        h1_ref[...] = h0_ref[...] + _dot(mixed, wo_ref[...])

        @pl.when(pl.program_id(0) == n_t // 2)
        def _():
            push.relay()

        @pl.when(pl.program_id(0) == n_t - 1)
        def _():
            push.wait()

    row = pl.BlockSpec((tm, D_MODEL), lambda i: (i, 0))
    wsp = pl.BlockSpec((D_MODEL, D_MODEL), lambda i: (0, 0))
    return pl.pallas_call(
        body, name="mix_fwd", grid=(n_t,),
        in_specs=[row, row, _seg_spec(tm, 6), _seg_spec(tm, 7), row, wsp, wsp, wsp] + ride.specs(),
        out_specs=[row, row, row, row] + ride.specs(),
        out_shape=[jax.ShapeDtypeStruct((rows, D_MODEL), F32)] + [jax.ShapeDtypeStruct((rows, D_MODEL), BF16)] * 3
        + ride.out_shapes,
        scratch_shapes=ride.scratch(),
        compiler_params=pltpu.CompilerParams(dimension_semantics=("arbitrary",), vmem_limit_bytes=VMEM_LIMIT,
                                             has_side_effects=True),
    )(zr, zl, proj, proj, h0, wbr, wbl, wout, *ride.arrays)


def _ffn_fwd_loss(h1, norm_w, wfi_g, wfo_g, final_w, target):
    rows = h1.shape[0]
    tm = _tile(rows, 320)
    piece = 64
    n_piece = tm // piece

    def body(h1_ref, nw_ref, wfi_ref, wfo_ref, fw_ref, *refs):
        t_refs = refs[:n_piece]
        u2_ref, g_ref, up_ref, act_ref, dh2_ref, red_ref = refs[n_piece:]
        i = pl.program_id(0)

        @pl.when(i == 0)
        def _():
            red_ref[...] = jnp.zeros_like(red_ref)

        x = h1_ref[...]
        rs = lax.rsqrt(jnp.mean(x * x, axis=-1, keepdims=True) + NORM_EPS)
        u2 = (x * rs * nw_ref[...]).astype(BF16)
        u2_ref[...] = u2
        ffn = None
        for d in range(FFN_GROUPS):
            cols = slice(FFN_GROUP * d, FFN_GROUP * (d + 1))
            g = _dot(u2, wfi_ref[d])
            up = _dot(u2, wfi_ref[d + FFN_GROUPS])
            act = (g * _sigmoid(g) * up).astype(BF16)
            g_ref[:, cols] = g.astype(BF16)
            up_ref[:, cols] = up.astype(BF16)
            act_ref[:, cols] = act
            part = _dot(act, wfo_ref[d])
            ffn = part if ffn is None else ffn + part

        h2 = x + ffn
        rs = lax.rsqrt(jnp.mean(h2 * h2, axis=-1, keepdims=True) + NORM_EPS)
        nh = h2 * rs
        fw = fw_ref[...]
        row = i * tm + lax.broadcasted_iota(jnp.int32, (tm, 1), 0)
        tgt = jnp.concatenate([t[...] for t in t_refs], axis=0)
        diff = jnp.where(row >= CHUNK, nh * fw - tgt, 0.0)
        dy = diff * (1.0 / D_MODEL)
        red_ref[0:1, :] += jnp.sum(diff * diff, axis=0, keepdims=True)
        red_ref[1:2, :] += jnp.sum(dy * nh, axis=0, keepdims=True)
        dn = dy * fw
        dh2_ref[...] = rs * (dn - nh * jnp.mean(dn * nh, axis=-1, keepdims=True))

    row = pl.BlockSpec((tm, D_MODEL), lambda i: (i, 0))
    vec = pl.BlockSpec((1, D_MODEL), lambda i: (0, 0))
    hid = pl.BlockSpec((tm, FFN_GROUPS * FFN_GROUP), lambda i: (i, 0))
    hid_shape = jax.ShapeDtypeStruct((rows, FFN_GROUPS * FFN_GROUP), BF16)
    resident = dict(pipeline_mode=pl.Buffered(1))
    head_pieces = CHUNK // piece
    t_specs = [pl.BlockSpec((piece, D_MODEL), lambda i, k=k: (jnp.maximum(i * n_piece + k - head_pieces, 0), 0))
               for k in range(n_piece)]
    return pl.pallas_call(
        body, name="ffn_fwd_loss", grid=(rows // tm,),
        in_specs=[row, vec,
                  pl.BlockSpec((2 * FFN_GROUPS, D_MODEL, FFN_GROUP), lambda i: (0, 0, 0), **resident),
                  pl.BlockSpec((FFN_GROUPS, FFN_GROUP, D_MODEL), lambda i: (0, 0, 0), **resident),
                  vec] + t_specs,
        out_specs=[row, hid, hid, hid, row, pl.BlockSpec((8, D_MODEL), lambda i: (0, 0))],
        out_shape=[jax.ShapeDtypeStruct((rows, D_MODEL), BF16), hid_shape, hid_shape, hid_shape,
                   jax.ShapeDtypeStruct((rows, D_MODEL), F32), jax.ShapeDtypeStruct((8, D_MODEL), F32)],
        compiler_params=_cparams(("arbitrary",)),
    )(h1, norm_w, wfi_g, wfo_g, final_w, *([target] * n_piece))


def _wgrad(a, b, ka, tn, out_dtype, b_halves=False):
    rows = a.shape[0]
    na = a.shape[1] // ka
    tm = _tile(rows, 1664)
    nm = rows // tm
    if b_halves:
        per_half = b.shape[2] // tn
        nb = 2 * per_half
        b_spec = pl.BlockSpec((None, tm, tn), lambda p, q, m: (q // per_half, m, q % per_half))
    else:
        nb = b.shape[1] // tn
        b_spec = pl.BlockSpec((tm, tn), lambda p, q, m: (m, q))

    def body(a_ref, b_ref, o_ref, acc):
        m = pl.program_id(2)

        @pl.when(m == 0)
        def _():
            acc[...] = jnp.zeros_like(acc)

        acc[...] += _dot_tn(a_ref[...].astype(BF16), b_ref[...].astype(BF16))

        @pl.when(m == nm - 1)
        def _():
            o_ref[...] = acc[...].astype(out_dtype)

    return pl.pallas_call(
        body, name="wgrad", grid=(na, nb, nm),
        in_specs=[pl.BlockSpec((tm, ka), lambda p, q, m: (m, p)), b_spec],
        out_specs=pl.BlockSpec((None, None, ka, tn), lambda p, q, m: (p, q, 0, 0)),
        out_shape=jax.ShapeDtypeStruct((na, nb, ka, tn), out_dtype),
        scratch_shapes=[pltpu.VMEM((ka, tn), F32)],
        compiler_params=_cparams(("parallel", "parallel", "arbitrary")),
    )(a, b)


WIN_NEAR = (2, 4, 3, 5, 1)
WIN_FAR = (6, 7)
WIN_ORDER = WIN_FAR + WIN_NEAR + (0,)


def _w_in_relation_at(jj):
    k = 0
    for pos in reversed(range(len(WIN_ORDER) - 1)):
        k = jnp.where(jj == pos, WIN_ORDER[pos], k)
    return k


def _wgrad_w_in(u, dproj, me_arr):
    rows = u.shape[0]
    tm = _tile(rows, 1664)
    nm = rows // tm
    n_near = len(WIN_NEAR)

    def body(me_ref, a_ref, b_ref, far_ref, land_ref, acc, sbuf, send_sems, recv_sems, own_sem):
        del me_ref
        jj, m = pl.program_id(0), pl.program_id(1)

        def near_copy(n):
            k = WIN_NEAR[n]
            return pltpu.make_async_remote_copy(src_ref=sbuf.at[n], dst_ref=land_ref.at[k], send_sem=send_sems.at[n],
                                                recv_sem=recv_sems.at[n], device_id=_peer(k)[0], device_id_type=MESH_ID)

        own_copy = pltpu.make_async_copy(sbuf.at[n_near], land_ref.at[0], own_sem)

        @pl.when(m == 0)
        def _():
            acc[...] = jnp.zeros_like(acc)

        acc[...] += _dot_tn(a_ref[...], b_ref[...])

        for pos, k in enumerate(WIN_ORDER):
            @pl.when(jnp.logical_and(jj == pos, m == nm - 1))
            def _(k=k):
                block = acc[...].astype(BF16)
                if k in WIN_FAR:
                    far_ref[...] = block
                elif k == 0:
                    sbuf[n_near] = block
                    own_copy.start()
                else:
                    sbuf[WIN_NEAR.index(k)] = block
                    near_copy(WIN_NEAR.index(k)).start()

        @pl.when(jnp.logical_and(jj == N_DEV - 1, m == nm - 1))
        def _():
            for n in range(n_near):
                near_copy(n).wait_recv()
            for n in range(n_near):
                near_copy(n).wait_send()
            own_copy.wait()

    grid_spec = pltpu.PrefetchScalarGridSpec(
        num_scalar_prefetch=1, grid=(N_DEV, nm),
        in_specs=[pl.BlockSpec((tm, D_MODEL), lambda jj, m, me: (m, 0)),
                  pl.BlockSpec((tm, D_MODEL), lambda jj, m, me: (m, me[0] ^ _w_in_relation_at(jj)))],
        out_specs=[pl.BlockSpec((None, D_MODEL, D_MODEL), lambda jj, m, me: (jnp.minimum(jj, len(WIN_FAR) - 1), 0, 0)),
                   ANY],
        scratch_shapes=[pltpu.VMEM((D_MODEL, D_MODEL), F32), pltpu.VMEM((n_near + 1, D_MODEL, D_MODEL), BF16),
                        pltpu.SemaphoreType.DMA((n_near,)), pltpu.SemaphoreType.DMA((n_near,)),
                        pltpu.SemaphoreType.DMA(())])
    return pl.pallas_call(
        body, name="wgrad_w_in", grid_spec=grid_spec,
        out_shape=[jax.ShapeDtypeStruct((len(WIN_FAR), D_MODEL, D_MODEL), BF16),
                   jax.ShapeDtypeStruct((n_near + 1, D_MODEL, D_MODEL), BF16)],
        compiler_params=pltpu.CompilerParams(dimension_semantics=("arbitrary", "arbitrary"),
                                             vmem_limit_bytes=VMEM_LIMIT, has_side_effects=True),
    )(me_arr, u, dproj)


def _ffn_bwd(dh2, g, up, h1, norm_w, wfi_g, wfo_g):
    rows = h1.shape[0]
    tm = _tile(rows, 320)

    def body(dh2_ref, g_ref, up_ref, h1_ref, nw_ref, wfi_ref, wfo_ref, dgu_ref, dh1_ref, dw_ref):
        @pl.when(pl.program_id(0) == 0)
        def _():
            dw_ref[...] = jnp.zeros_like(dw_ref)

        dh2 = dh2_ref[...]
        dh2_b = dh2.astype(BF16)
        du2 = None
        for d in range(FFN_GROUPS):
            cols = slice(FFN_GROUP * d, FFN_GROUP * (d + 1))
            dact = _dot_nt(dh2_b, wfo_ref[d])
            gv, uv = g_ref[:, cols].astype(F32), up_ref[:, cols].astype(F32)
            sg = _sigmoid(gv)
            dg = (dact * uv * (sg * (1.0 + gv * (1.0 - sg)))).astype(BF16)
            dup = (dact * (gv * sg)).astype(BF16)
            dgu_ref[0, :, cols] = dg
            dgu_ref[1, :, cols] = dup
            part = _dot_nt(dg, wfi_ref[d]) + _dot_nt(dup, wfi_ref[d + FFN_GROUPS])
            du2 = part if du2 is None else du2 + part
        dx, dw = _rms_bwd(h1_ref[...], nw_ref[...], du2)
        dw_ref[0:1, :] += dw
        dh1_ref[...] = dh2 + dx

    row = pl.BlockSpec((tm, D_MODEL), lambda i: (i, 0))
    vec = pl.BlockSpec((1, D_MODEL), lambda i: (0, 0))
    hid = pl.BlockSpec((tm, FFN_GROUPS * FFN_GROUP), lambda i: (i, 0))
    resident = dict(pipeline_mode=pl.Buffered(1))
    return pl.pallas_call(
        body, name="ffn_bwd", grid=(rows // tm,),
        in_specs=[row, hid, hid, row, vec,
                  pl.BlockSpec((2 * FFN_GROUPS, D_MODEL, FFN_GROUP), lambda i: (0, 0, 0), **resident),
                  pl.BlockSpec((FFN_GROUPS, FFN_GROUP, D_MODEL), lambda i: (0, 0, 0), **resident)],
        out_specs=[pl.BlockSpec((2, tm, FFN_GROUPS * FFN_GROUP), lambda i: (0, i, 0)), row,
                   pl.BlockSpec((8, D_MODEL), lambda i: (0, 0))],
        out_shape=[jax.ShapeDtypeStruct((2, rows, FFN_GROUPS * FFN_GROUP), BF16),
                   jax.ShapeDtypeStruct((rows, D_MODEL), F32), jax.ShapeDtypeStruct((8, D_MODEL), F32)],
        compiler_params=_cparams(("arbitrary",)),
    )(dh2, g, up, h1, norm_w, wfi_g, wfo_g)


def _mix_bwd(dh1, yr, yl, proj, wbr, wbl, wout):
    rows = dh1.shape[0]
    tm = _tile(rows, 640)

    def body(dh1_ref, yr_ref, yl_ref, ga_ref, gb_ref, wbr_ref, wbl_ref, wo_ref,
             dyr_ref, dyl_ref, dseg_ref, dzr_ref, dzl_ref):
        dmix = _dot_nt(dh1_ref[...].astype(BF16), wo_ref[...])
        sa, sb = _sigmoid(ga_ref[...].astype(F32)), _sigmoid(gb_ref[...].astype(F32))
        dyr = (dmix * sa).astype(BF16)
        dyl = (dmix * sb).astype(BF16)
        dyr_ref[...] = dyr
        dyl_ref[...] = dyl
        dseg_ref[:, 0:D_MODEL] = (dmix * yr_ref[...].astype(F32) * (sa * (1.0 - sa))).astype(BF16)
        dseg_ref[:, D_MODEL:2 * D_MODEL] = (dmix * yl_ref[...].astype(F32) * (sb * (1.0 - sb))).astype(BF16)
        dzr_ref[...] = _dot_nt(dyr, wbr_ref[...]).astype(BF16)
        dzl_ref[...] = _dot_nt(dyl, wbl_ref[...]).astype(BF16)

    row = pl.BlockSpec((tm, D_MODEL), lambda i: (i, 0))
    wsp = pl.BlockSpec((D_MODEL, D_MODEL), lambda i: (0, 0))
    bshape = jax.ShapeDtypeStruct((rows, D_MODEL), BF16)
    return pl.pallas_call(
        body, name="mix_bwd", grid=(rows // tm,),
        in_specs=[row, row, row, _seg_spec(tm, 6), _seg_spec(tm, 7), wsp, wsp, wsp],
        out_specs=[row, row, pl.BlockSpec((tm, 2 * D_MODEL), lambda i: (i, 3)), row, row],
        out_shape=[bshape, bshape, jax.ShapeDtypeStruct((rows, N_DEV * D_MODEL), BF16), bshape, bshape],
        compiler_params=_cparams(("parallel",)),
    )(dh1, yr, yl, proj, proj, wbr, wbl, wout)


S1_SHAPES = [jax.ShapeDtypeStruct((N_DEV, D_MODEL, FFN_GROUP), BF16)]


def _s1_parts(ins, p):
    return [ins[0].at[p]]


def _lru_bwd(dzl, hs, cri, proj, dproj, conv_w, lam, wa_g, wx_g, s1_grads):
    rows = dzl.shape[0]
    tm = _tile(rows, 640)
    nt = rows // tm
    t8 = tm // 8
    n_s1 = len(s1_grads)

    def body(dzl_ref, hs_ref, hsp_ref, cri_ref, x_ref, gt_ref, cw_ref, lam_ref, wa_ref, wx_ref, dproj_in, *refs):
        del dproj_in
        s1_refs = refs[:n_s1]
        dseg_ref, dwa_ref, dwx_ref, sm_ref = refs[n_s1:n_s1 + 4]
        land_refs = refs[n_s1 + 4:2 * n_s1 + 4]
        (xbuf, abuf, mbuf, ibuf, dbuf, dcbuf, dpr_s, dpi_s, sums, conv_sums, anext, dhcar,
         send_sems, recv_sems, loc_sems) = refs[2 * n_s1 + 4:]
        step = pl.program_id(0)
        i = nt - 1 - step
        push = _Push(lambda p: _s1_parts(s1_refs, p), lambda s: [r.at[s] for r in land_refs],
                     (send_sems, recv_sems, loc_sems), n_s1)

        @pl.when(step == 0)
        def _():
            push.start()
            dwa_ref[...] = jnp.zeros_like(dwa_ref)
            dwx_ref[...] = jnp.zeros_like(dwx_ref)
            sm_ref[...] = jnp.zeros_like(sm_ref)
            anext[...] = jnp.zeros_like(anext)
            dhcar[...] = jnp.zeros_like(dhcar)
            dcbuf[tm:tm + 8, :] = jnp.zeros((8, D_MODEL), F32)

        slab, lanes = 16, 256
        lam_v = lam_ref[...]
        xbuf[0:8, :] = jnp.where(i == 0, 0.0, hsp_ref[8:16, :].astype(F32))
        sums[...] = jnp.zeros_like(sums)

        def before_scan(k, carry):
            rw = pl.ds(pl.multiple_of(k * slab, slab), slab)
            for q in range(D_MODEL // lanes):
                ln = slice(lanes * q, lanes * (q + 1))
                a, mult, inv_mult, _ = _lru_decay(cri_ref[1, rw, ln].astype(F32), lam_v[:, ln])
                abuf[rw, ln] = a
                mbuf[rw, ln] = mult
                ibuf[rw, ln] = inv_mult
                gl, dgl = _gelu_parts(gt_ref[rw, ln].astype(F32))
                dzl_v = dzl_ref[rw, ln].astype(F32)
                hs_v = hs_ref[rw, ln].astype(F32)
                dseg_ref[rw, D_MODEL + lanes * q:D_MODEL + lanes * (q + 1)] = (dzl_v * hs_v * dgl).astype(BF16)
                dbuf[rw, ln] = dzl_v * gl
                xbuf[pl.ds(pl.multiple_of(k * slab + 8, 8), slab), ln] = hs_v
            return carry

        lax.fori_loop(0, tm // slab, before_scan, 0)

        sub = lax.broadcasted_iota(jnp.int32, (8, D_MODEL), 0)

        def block(k, carry):
            dh_next, a_next = carry
            off = pl.multiple_of((t8 - 1 - k) * 8, 8)
            a_blk = abuf[pl.ds(off, 8), :]
            av = jnp.where(sub < 7, pltpu.roll(a_blk, 7, 0), a_next)
            uv = dbuf[pl.ds(off, 8), :]
            for s in (1, 2, 4):
                us = jnp.where(sub < 8 - s, pltpu.roll(uv, 8 - s, 0), 0.0)
                as_ = jnp.where(sub < 8 - s, pltpu.roll(av, 8 - s, 0), 1.0)
                uv = uv + av * us
                av = av * as_
            hv = uv + av * dh_next
            dbuf[pl.ds(off, 8), :] = hv
            return hv[0:1, :], a_blk[0:1, :]

        dh_first, a_first = lax.fori_loop(0, t8, block, (dhcar[...], anext[...]))
        dhcar[...] = dh_first
        anext[...] = a_first

        sp = jnp.maximum(-lam_v, 0.0) + jnp.log(1.0 + jnp.exp(-jnp.abs(lam_v)))
        sub_q = lax.broadcasted_iota(jnp.int32, (8, lanes), 0)
        row16 = lax.broadcasted_iota(jnp.int32, (slab, 1), 0)

        def after_scan(k, carry):
            off = pl.multiple_of(k * slab, slab)
            rw = pl.ds(off, slab)
            for q in range(D_MODEL // lanes):
                ln = slice(lanes * q, lanes * (q + 1))
                before = xbuf[pl.ds(off, 8), ln]
                h_lo = xbuf[pl.ds(pl.multiple_of(off + 8, 8), 8), ln]
                h_hi = xbuf[pl.ds(pl.multiple_of(off + 16, 8), 8), ln]
                hprev = jnp.concatenate([jnp.where(sub_q >= 1, pltpu.roll(h_lo, 1, 0), before[7:8, :]),
                                         jnp.where(sub_q >= 1, pltpu.roll(h_hi, 1, 0), h_lo[7:8, :])], axis=0)
                c, r, ig = (cri_ref[n, rw, ln].astype(F32) for n in range(3))
                a, mult, inv_mult = abuf[rw, ln], mbuf[rw, ln], ibuf[rw, ln]
                dh = dbuf[rw, ln]
                duu = jnp.where(i * tm + off + row16 >= PAD_ROWS, dh, 0.0)
                t_mult = duu * mult
                dlog_a = dh * hprev * a - duu * ig * c * (a * a) * inv_mult
                dpr = dlog_a * (-LRU_C * sp[:, ln]) * r * (1.0 - r)
                dpi = t_mult * c * ig * (1.0 - ig)
                dpr_s[rw, ln] = dpr.astype(BF16)
                dpi_s[rw, ln] = dpi.astype(BF16)
                dcbuf[rw, ln] = t_mult * ig
                sums[0, :, ln] += dlog_a * r
                sums[1, :, ln] += dpr
                sums[2, :, ln] += dpi
            return carry

        lax.fori_loop(0, tm // slab, after_scan, 0)

        dcs = []
        for g in range(LRU_BLOCKS):
            sl = slice(LRU_BLOCK * g, LRU_BLOCK * (g + 1))
            cg = cri_ref[0, :, sl]
            dpr_b, dpi_b = dpr_s[:, sl], dpi_s[:, sl]
            dwa_ref[g] += _dot_tn(cg, dpr_b)
            dwx_ref[g] += _dot_tn(cg, dpi_b)
            dcs.append(_dot_nt(dpr_b, wa_ref[g]) + _dot_nt(dpi_b, wx_ref[g]))
        dc = dcbuf[0:tm, :] + jnp.concatenate(dcs, axis=1)

        dcbuf[0:tm, :] = dc
        conv_sums[...] = jnp.zeros_like(conv_sums)

        def conv_back(k, carry):
            off = pl.multiple_of(k * slab, slab)
            rw = pl.ds(off, slab)
            for q in range(D_MODEL // lanes):
                ln = slice(lanes * q, lanes * (q + 1))
                blocks = [dcbuf[pl.ds(pl.multiple_of(off + 8 * b, 8), 8), ln] for b in range(3)]
                x_v = x_ref[rw, ln].astype(F32)
                now = jnp.concatenate(blocks[:2], axis=0)
                dlin = cw_ref[3:4, ln] * now
                conv_sums[3, :, ln] += now * x_v
                conv_sums[4, :, ln] += now
                for back in (1, 2, 3):
                    turned = [pltpu.roll(b, 8 - back, 0) for b in blocks]
                    later = jnp.concatenate([jnp.where(sub_q < 8 - back, turned[0], turned[1]),
                                             jnp.where(sub_q < 8 - back, turned[1], turned[2])], axis=0)
                    dlin = dlin + cw_ref[3 - back:4 - back, ln] * later
                    conv_sums[3 - back, :, ln] += later * x_v
                dseg_ref[rw, ln] = dlin.astype(BF16)
            return carry

        lax.fori_loop(0, tm // slab, conv_back, 0)
        dcbuf[tm:tm + 8, :] = dcbuf[0:8, :]
        for n in range(5):
            sm_ref[n:n + 1, :] += jnp.sum(conv_sums[n], axis=0, keepdims=True)
        sm_ref[5:6, :] += jnp.sum(sums[1], axis=0, keepdims=True)
        sm_ref[6:7, :] += jnp.sum(sums[2], axis=0, keepdims=True)
        sm_ref[7:8, :] += jnp.sum(sums[0], axis=0, keepdims=True) * (LRU_C * _sigmoid(-lam_v))

        @pl.when(step == nt - 1)
        def _():
            push.wait()

    rowb = pl.BlockSpec((tm, D_MODEL), lambda s: (nt - 1 - s, 0))
    t16 = tm // 16
    prev8 = pl.BlockSpec((16, D_MODEL), lambda s: (jnp.maximum((nt - 1 - s) * t16 - 1, 0), 0))
    seg = lambda k: pl.BlockSpec((tm, D_MODEL), lambda s, k=k: (nt - 1 - s, k))
    vec = pl.BlockSpec((1, D_MODEL), lambda s: (0, 0))
    mat = pl.BlockSpec((LRU_BLOCKS, LRU_BLOCK, LRU_BLOCK), lambda s: (0, 0, 0))
    mshape = jax.ShapeDtypeStruct((LRU_BLOCKS, LRU_BLOCK, LRU_BLOCK), F32)
    n_in = 10
    return pl.pallas_call(
        body, name="lru_bwd", grid=(nt,),
        in_specs=[rowb, rowb, prev8, pl.BlockSpec((3, tm, D_MODEL), lambda s: (0, nt - 1 - s, 0)), seg(4), seg(5),
                  pl.BlockSpec((4, D_MODEL), lambda s: (0, 0)), vec, mat, mat, ANY] + [ANY] * n_s1,
        out_specs=[pl.BlockSpec((tm, 2 * D_MODEL), lambda s: (nt - 1 - s, 2)), mat, mat,
                   pl.BlockSpec((8, D_MODEL), lambda s: (0, 0))] + [ANY] * n_s1,
        out_shape=[jax.ShapeDtypeStruct(dproj.shape, dproj.dtype), mshape, mshape,
                   jax.ShapeDtypeStruct((8, D_MODEL), F32)] + S1_SHAPES,
        input_output_aliases={n_in: 0},
        scratch_shapes=[pltpu.VMEM((tm + 8, D_MODEL), F32), pltpu.VMEM((tm, D_MODEL), F32),
                        pltpu.VMEM((tm, D_MODEL), F32), pltpu.VMEM((tm, D_MODEL), F32),
                        pltpu.VMEM((tm, D_MODEL), F32), pltpu.VMEM((tm + 8, D_MODEL), F32),
                        pltpu.VMEM((tm, D_MODEL), BF16), pltpu.VMEM((tm, D_MODEL), BF16),
                        pltpu.VMEM((3, 16, D_MODEL), F32), pltpu.VMEM((5, 16, D_MODEL), F32),
                        pltpu.VMEM((1, D_MODEL), F32), pltpu.VMEM((1, D_MODEL), F32)] + _push_sems(n_s1),
        compiler_params=pltpu.CompilerParams(dimension_semantics=("arbitrary",), vmem_limit_bytes=VMEM_LIMIT,
                                             has_side_effects=True),
    )(dzl, hs, hs, cri, proj, proj, conv_w, lam, wa_g, wx_g, dproj, *s1_grads)


def _retention_bwd(dzr, o, proj, states, cos2, sin2, dec, dproj, ride):
    rows = dzr.shape[0]
    n_chunks = rows // CHUNK
    per_step = _chunks_per_step(n_chunks)
    n_steps = n_chunks // per_step
    tm = per_step * CHUNK
    n_r = ride.n

    def body(dzr_ref, o_ref, q_ref, k_ref, v_ref, g_ref, st_ref, c_ref, s_ref, dec_ref, dproj_in, *refs):
        del dproj_in
        dseg_ref = refs[n_r]
        dstate = refs[2 * n_r + 1]
        push = ride.push(refs[:n_r], refs[n_r + 1:2 * n_r + 1], refs[2 * n_r + 2:])

        @pl.when(pl.program_id(0) == 0)
        def _():
            push.start()
            dstate[...] = jnp.zeros_like(dstate)

        for h in range(HEADS):
            sl = slice(HEAD_DIM * h, HEAD_DIM * (h + 1))
            intra, qd, kd, cd = dec_ref[0, h], dec_ref[1, h], dec_ref[2, h], dec_ref[3, h]
            dst = dstate[h]
            for c in reversed(range(per_step)):
                rw = slice(CHUNK * c, CHUNK * (c + 1))
                cos_t, sin_t = c_ref[rw, :], s_ref[rw, :]
                o = o_ref[rw, sl].astype(F32)
                g = g_ref[rw, sl].astype(F32)
                dzr_v = dzr_ref[rw, sl].astype(F32)
                sg = _sigmoid(g)
                r = lax.rsqrt(jnp.mean(o * o, axis=-1, keepdims=True) + NORM_EPS)
                on = o * r
                dseg_ref[rw, 3 * D_MODEL + HEAD_DIM * h:3 * D_MODEL + HEAD_DIM * (h + 1)] = (
                    dzr_v * on * (sg * (1.0 + g * (1.0 - sg)))).astype(BF16)
                don = dzr_v * (g * sg)
                do = r * (don - on * jnp.mean(don * on, axis=-1, keepdims=True))
                dob = do.astype(BF16)

                qh = _rot(q_ref[rw, sl].astype(F32), cos_t, sin_t)
                kh = _rot(k_ref[rw, sl].astype(F32), cos_t, sin_t) * QK_SCALE
                qb, kb, vb = qh.astype(BF16), kh.astype(BF16), v_ref[rw, sl]
                s = (_dot_nt(qb, kb) * intra).astype(BF16)
                ds = (_dot_nt(dob, vb) * intra).astype(BF16)
                st_b = st_ref[c, h].astype(BF16)
                dst_b = dst.astype(BF16)
                dv = _dot_tn(s, dob) + _dot((kh * kd).astype(BF16), dst_b)
                dq = _dot(ds, kb) + _dot_nt(dob, st_b) * qd
                dk = _dot_tn(ds, qb) + _dot_nt(vb, dst_b) * kd
                dst = dst * cd + _dot_tn((qh * qd).astype(BF16), dob)
                dseg_ref[rw, 2 * D_MODEL + HEAD_DIM * h:2 * D_MODEL + HEAD_DIM * (h + 1)] = dv.astype(BF16)
                dseg_ref[rw, sl] = _rot_t(dq, cos_t, sin_t).astype(BF16)
                dseg_ref[rw, D_MODEL + HEAD_DIM * h:D_MODEL + HEAD_DIM * (h + 1)] = (
                    _rot_t(dk, cos_t, sin_t) * QK_SCALE).astype(BF16)
            dstate[h] = dst

        @pl.when(pl.program_id(0) == n_steps - 1)
        def _():
            push.wait()

    rev = lambda s: n_steps - 1 - s
    rowb = pl.BlockSpec((tm, D_MODEL), lambda s: (rev(s), 0))
    seg = lambda k: pl.BlockSpec((tm, D_MODEL), lambda s, k=k: (rev(s), k))
    tab = pl.BlockSpec((tm, HEAD_DIM), lambda s: (rev(s), 0))
    return pl.pallas_call(
        body, name="retention_bwd", grid=(n_steps,),
        in_specs=[rowb, rowb, seg(0), seg(1), seg(2), seg(3),
                  pl.BlockSpec((per_step, HEADS, HEAD_DIM, HEAD_DIM), lambda s: (rev(s), 0, 0, 0)), tab, tab,
                  pl.BlockSpec((4, HEADS, CHUNK, CHUNK), lambda s: (0, 0, 0, 0)), ANY] + ride.specs(),
        out_specs=[pl.BlockSpec((tm, 4 * D_MODEL), lambda s: (rev(s), 0))] + ride.specs(),
        out_shape=[jax.ShapeDtypeStruct(dproj.shape, dproj.dtype)] + ride.out_shapes,
        input_output_aliases={10: 0},
        scratch_shapes=[pltpu.VMEM((HEADS, HEAD_DIM, HEAD_DIM), F32)] + ride.scratch(),
        compiler_params=pltpu.CompilerParams(dimension_semantics=("arbitrary",), vmem_limit_bytes=VMEM_LIMIT,
                                             has_side_effects=True),
    )(dzr, o, proj, proj, proj, proj, states, cos2, sin2, dec, dproj, *ride.arrays)


S2_SHAPES = [
    jax.ShapeDtypeStruct((N_DEV, LRU_BLOCKS, LRU_ROWS, LRU_BLOCK), F32),
    jax.ShapeDtypeStruct((N_DEV, LRU_BLOCKS, LRU_ROWS, LRU_BLOCK), F32),
]


def _s2_parts(ins, p):
    return [r.at[p] for r in ins]


def _in_proj_bwd(dproj, win_g, h0, norm_w, dh1, d_win_far, s2_grads, pack_early):
    rows = h0.shape[0]
    tm = _tile(rows, 320)
    n_i = rows // tm
    n_s2 = len(s2_grads)
    n_far = len(WIN_FAR)
    pack_rows = pack_early.shape[0]

    def body(dseg_ref, w_ref, h0_ref, nw_ref, dh1_ref, far_ref, early_ref, *refs):
        s2_refs = refs[:n_s2]
        dh0_ref, dw_ref, far_land = refs[n_s2:n_s2 + 3]
        land_refs = refs[n_s2 + 3:2 * n_s2 + 3]
        early_land, late_land = refs[2 * n_s2 + 3:2 * n_s2 + 5]
        (send_sems, recv_sems, loc_sems, far_send_sems, far_recv_sems, late_buf) = refs[2 * n_s2 + 5:2 * n_s2 + 11]
        early_sems, late_sems = refs[2 * n_s2 + 11:2 * n_s2 + 14], refs[2 * n_s2 + 14:]
        i = pl.program_id(0)
        push = _Push(lambda p: _s2_parts(s2_refs, p), lambda s: [r.at[s] for r in land_refs],
                     (send_sems, recv_sems, loc_sems), n_s2)
        early = _Push(lambda p: [early_ref], lambda s: [early_land.at[s]], tuple(early_sems), 1)
        late = _Push(lambda p: [late_buf], lambda s: [late_land.at[s]], tuple(late_sems), 1)

        def far_copy(n):
            return pltpu.make_async_remote_copy(src_ref=far_ref.at[n], dst_ref=far_land.at[n],
                                                send_sem=far_send_sems.at[n], recv_sem=far_recv_sems.at[n],
                                                device_id=_peer(WIN_FAR[n])[0], device_id_type=MESH_ID)

        @pl.when(i == 0)
        def _():
            for n in range(n_far):
                far_copy(n).start()
            push.start()
            early.start()
            dw_ref[...] = jnp.zeros_like(dw_ref)

        du = _dot_nt(dseg_ref[:, 0:D_MODEL], w_ref[0])
        for j in range(1, N_DEV):
            du = du + _dot_nt(dseg_ref[:, D_MODEL * j:D_MODEL * (j + 1)], w_ref[j])
        dx, dw = _rms_bwd(h0_ref[...], nw_ref[...], du)
        dw_ref[0:1, :] += dw
        dh0 = dh1_ref[...] + dx
        dh0_ref[...] = dh0

        @pl.when(i == 0)
        def _():
            late_buf[8:8 + N_META, :] = dh0[PAD_ROWS:CHUNK, :]

        @pl.when(i == n_i - 1)
        def _():
            late_buf[0:8, :] = dw_ref[...]
            late.start()
            for n in range(n_far):
                far_copy(n).wait_recv()
            for n in range(n_far):
                far_copy(n).wait_send()
            push.wait()
            early.wait()
            late.wait()

    row = pl.BlockSpec((tm, D_MODEL), lambda i: (i, 0))
    vec = pl.BlockSpec((1, D_MODEL), lambda i: (0, 0))
    return pl.pallas_call(
        body, name="in_proj_bwd", grid=(n_i,),
        in_specs=[pl.BlockSpec((tm, N_DEV * D_MODEL), lambda i: (i, 0)),
                  pl.BlockSpec((N_DEV, D_MODEL, D_MODEL), lambda i: (0, 0, 0), pipeline_mode=pl.Buffered(1)),
                  row, vec, row, ANY, ANY] + [ANY] * n_s2,
        out_specs=[row, pl.BlockSpec((8, D_MODEL), lambda i: (0, 0)), ANY] + [ANY] * n_s2 + [ANY, ANY],
        out_shape=[jax.ShapeDtypeStruct((rows, D_MODEL), F32), jax.ShapeDtypeStruct((8, D_MODEL), F32),
                   jax.ShapeDtypeStruct((n_far, D_MODEL, D_MODEL), BF16)] + S2_SHAPES
        + [jax.ShapeDtypeStruct((N_DEV, pack_rows, D_MODEL), F32)] * 2,
        scratch_shapes=_push_sems(n_s2) + [pltpu.SemaphoreType.DMA((n_far,)), pltpu.SemaphoreType.DMA((n_far,)),
                                           pltpu.VMEM((pack_rows, D_MODEL), F32)] + _push_sems(1) + _push_sems(1),
        compiler_params=pltpu.CompilerParams(dimension_semantics=("arbitrary",),
                                             vmem_limit_bytes=VMEM_LIMIT, has_side_effects=True),
    )(dproj, win_g, h0, norm_w, dh1, d_win_far, pack_early, *s2_grads)


def _adamw(g_slots, w, m, v, more_slots=None):
    slots, rows, cols = g_slots.shape
    extra = [] if more_slots is None else [more_slots]
    tr = rows
    for cand in (256, 128, 64, 32, 16, 8):
        if rows % cand == 0 and rows > cand:
            tr = cand
            break

    def body(g_ref, *refs):
        w_ref, m_ref, v_ref, go_ref, d_ref, mo_ref, vo_ref = refs[len(extra):]
        g = g_ref[0].astype(F32)
        for s in range(1, slots):
            g = g + g_ref[s].astype(F32)
        for more_ref in refs[:len(extra)]:
            for s in range(more_ref.shape[0]):
                g = g + more_ref[s].astype(F32)
        m2 = ADAM_B1 * m_ref[...] + (1.0 - ADAM_B1) * g
        v2 = ADAM_B2 * v_ref[...] + (1.0 - ADAM_B2) * (g * g)
        m_hat = m2 / (1.0 - ADAM_B1 ** ADAM_STEP)
        v_hat = v2 / (1.0 - ADAM_B2 ** ADAM_STEP)
        go_ref[...] = g
        d_ref[...] = -ADAM_LR * (m_hat / (jnp.sqrt(v_hat) + ADAM_EPS) + ADAM_WD * w_ref[...])
        mo_ref[...] = m2
        vo_ref[...] = v2

    blk = pl.BlockSpec((tr, cols), lambda i: (i, 0))
    shape = jax.ShapeDtypeStruct((rows, cols), F32)
    return pl.pallas_call(
        body, name="adamw", grid=(rows // tr,),
        in_specs=[pl.BlockSpec((slots, tr, cols), lambda i: (0, i, 0))]
        + [pl.BlockSpec((t.shape[0], tr, cols), lambda i: (0, i, 0)) for t in extra] + [blk, blk, blk],
        out_specs=[blk] * 4, out_shape=[shape] * 4,
        compiler_params=_cparams(("parallel",)),
    )(g_slots, *extra, w, m, v)


def _sum_slots(packs):
    slots, rows, cols = packs.shape

    def body(p_ref, o_ref):
        acc = p_ref[0]
        for s in range(1, slots):
            acc = acc + p_ref[s]
        o_ref[...] = acc

    return pl.pallas_call(
        body, name="sum_slots", out_shape=jax.ShapeDtypeStruct((rows, cols), F32),
        compiler_params=pltpu.CompilerParams(vmem_limit_bytes=VMEM_LIMIT),
    )(packs)


def _gather_small(small):
    shapes = [jax.ShapeDtypeStruct((N_DEV,) + small.shape, F32)]
    return _push_call("gather_small", [small], shapes,
                      lambda ins, p: list(ins), lambda outs, s: [r.at[s] for r in outs])[0]


PACK_CONV_W, PACK_CONV_B, PACK_BA, PACK_BX, PACK_LAM = 0, 4, 5, 6, 7
PACK_FFN_NORM, PACK_SQ_ERR, PACK_FINAL_NORM, PACK_MIX_NORM, PACK_META = 8, 16, 17, 24, 32


def kernel(x, meta_tokens, mix_norm_w, w_in, conv_w, conv_b, lru_wa, lru_ba, lru_wx, lru_bx, lru_lambda, w_branch_ret, w_branch_lru, w_out, ffn_norm_w, w_ffn_in, w_ffn_out, final_norm_w, loss_target, m_meta_tokens, m_mix_norm_w, m_w_in, m_conv_w, m_conv_b, m_lru_wa, m_lru_ba, m_lru_wx, m_lru_bx, m_lru_lambda, m_w_branch_ret, m_w_branch_lru, m_w_out, m_ffn_norm_w, m_w_ffn_in, m_w_ffn_out, m_final_norm_w, v_meta_tokens, v_mix_norm_w, v_w_in, v_conv_w, v_conv_b, v_lru_wa, v_lru_ba, v_lru_wx, v_lru_bx, v_lru_lambda, v_w_branch_ret, v_w_branch_lru, v_w_out, v_ffn_norm_w, v_w_ffn_in, v_w_ffn_out, v_final_norm_w):
    me = _my_index()
    pad4 = ((0, 4), (0, 0))
    fw = final_norm_w.reshape(1, D_MODEL)

    small = jnp.concatenate([meta_tokens, jnp.pad(conv_w[0], pad4)], axis=0)
    small_g = _gather_small(small)
    meta_full = small_g[:, :N_META].transpose(1, 0, 2).reshape(N_META, D_MODEL)
    conv_w_full = small_g[:, N_META:N_META + 4].transpose(1, 0, 2).reshape(4, D_MODEL)
    mixer_shards = [w_branch_ret[0].astype(BF16), w_branch_lru[0].astype(BF16), w_out[0].astype(BF16),
                    lru_wa[0].astype(BF16), lru_wx[0].astype(BF16)]
    wfi_shard = jnp.pad(w_ffn_in[0].astype(BF16), ((0, 0), (0, FFN_GROUP - FFN_SHARD)))
    own_slot = lambda ins, p: list(ins)

    rows = x.shape[1] + CHUNK
    h0 = jnp.concatenate([jnp.zeros((PAD_ROWS, D_MODEL), F32), meta_full, x[0]], axis=0)
    cos2, sin2 = _rope_tables(rows)
    dec = _retention_consts()

    me_arr = me.astype(jnp.int32).reshape(1)
    proj, u, win_g = _in_proj(h0, mix_norm_w, w_in[0].astype(BF16), me_arr)
    o, zr, states, wbr_g, wbl_g, wout_g, wa_g, wx_g = _retention_fwd(
        proj, cos2, sin2, dec, _mixer_weights_ride(mixer_shards))
    wbr, wbl, wout = (t.reshape(D_MODEL, D_MODEL) for t in (wbr_g, wbl_g, wout_g))
    wa_g, wx_g = _from_owners(wa_g), _from_owners(wx_g)
    gather_wfi = _Ride([wfi_shard], [jax.ShapeDtypeStruct((N_DEV, D_MODEL, FFN_GROUP), BF16)],
                       own_slot, _slot_of_sender, gather_by_chip=True)
    hs, zl, cri, wfi_g = _lru_fwd(proj, conv_w_full, conv_b, lru_ba, lru_bx, lru_lambda, wa_g, wx_g, gather_wfi)
    h1, yr, yl, mixed, wfo_g = _mix_fwd(zr, zl, proj, h0, wbr, wbl, wout, _wfo_ride(w_ffn_out[0].astype(BF16)))
    u2, g, up, act, dh2, red = _ffn_fwd_loss(h1, ffn_norm_w, wfi_g, wfo_g, fw, loss_target[0])

    d_wfo = _wgrad(act, dh2, FFN_GROUP, D_MODEL, BF16)[:, 0]
    dgu, dh1, dw_ffn_norm = _ffn_bwd(dh2, g, up, h1, ffn_norm_w, wfi_g, wfo_g)
    d_wfi = _wgrad(u2, dgu, D_MODEL, FFN_GROUP, BF16, b_halves=True)[0]
    d_wout = _wgrad(mixed, dh1, D_MODEL, D_MODEL, BF16)[0, 0]
    dyr, dyl, dproj, dzr, dzl = _mix_bwd(dh1, yr, yl, proj, wbr, wbl, wout)
    d_wbr = _wgrad(zr, dyr, D_MODEL, D_MODEL, BF16)[0, 0]
    d_wbl = _wgrad(zl, dyl, D_MODEL, D_MODEL, BF16)[0, 0]
    dproj, d_wa, d_wx, lru_small, r_fi = _lru_bwd(
        dzl, hs, cri, proj, dproj, conv_w_full, lru_lambda, wa_g, wx_g, [d_wfi])
    mix_shape = jax.ShapeDtypeStruct((N_DEV, D_MODEL // N_DEV, D_MODEL), BF16)
    wfo_shape = jax.ShapeDtypeStruct((N_DEV, FFN_OUT_SHARD, D_MODEL), BF16)
    part_of_owner = lambda ins, p: [r.at[p] for r in ins[:3]] + [ins[3].at[p // 2, _half_rows(p), :]]
    scatter_mix = _Ride([t.reshape(mix_shape.shape) for t in (d_wbr, d_wbl, d_wout)] + [d_wfo],
                        [mix_shape] * 3 + [wfo_shape], part_of_owner, _slot_of_sender)
    dproj, r_br, r_bl, r_out, r_fo = _retention_bwd(dzr, o, proj, states, cos2, sin2, dec, dproj, scatter_mix)
    d_win_far, r_in = _wgrad_w_in(u, dproj, me_arr)
    pack_early = jnp.concatenate([lru_small, dw_ffn_norm, red], axis=0)
    dh0, _, r_in_far, r_wa, r_wx, packs_early, packs_late = _in_proj_bwd(
        dproj, win_g, h0, mix_norm_w, dh1, d_win_far, [_by_owner(d_wa), _by_owner(d_wx)], pack_early)
    grad_x = dh0[CHUNK:]

    small_sum = jnp.concatenate([_sum_slots(packs_early), _sum_slots(packs_late)], axis=0)
    loss = (0.5 / D_MODEL) * jnp.sum(small_sum[PACK_SQ_ERR])

    def big_update(slots, w, m, v, more_slots=None):
        shape = w.shape
        w2, m2, v2 = (t.reshape(slots.shape[1:]) for t in (w, m, v))
        return [t.reshape(shape) for t in _adamw(slots, w2, m2, v2, more_slots)]

    res = {}
    res["w_in"] = big_update(r_in, w_in, m_w_in, v_w_in, r_in_far)
    res["w_branch_ret"] = big_update(r_br, w_branch_ret, m_w_branch_ret, v_w_branch_ret)
    res["w_branch_lru"] = big_update(r_bl, w_branch_lru, m_w_branch_lru, v_w_branch_lru)
    res["w_out"] = big_update(r_out, w_out, m_w_out, v_w_out)
    res["w_ffn_in"] = big_update(r_fi[:, :, :FFN_SHARD], w_ffn_in, m_w_ffn_in, v_w_ffn_in)
    res["w_ffn_out"] = big_update(r_fo, w_ffn_out, m_w_ffn_out, v_w_ffn_out)
    res["lru_wa"] = big_update(r_wa.reshape(N_DEV, LRU_BLOCKS * LRU_ROWS, LRU_BLOCK), lru_wa, m_lru_wa, v_lru_wa)
    res["lru_wx"] = big_update(r_wx.reshape(N_DEV, LRU_BLOCKS * LRU_ROWS, LRU_BLOCK), lru_wx, m_lru_wx, v_lru_wx)

    col = me * HEAD_DIM
    g_meta = lax.dynamic_slice(small_sum, (PACK_META, col), (N_META, HEAD_DIM))
    g_conv = lax.dynamic_slice(small_sum, (PACK_CONV_W, col), (8, HEAD_DIM))
    small_names = ["mix_norm_w", "conv_b", "lru_ba", "lru_bx", "lru_lambda", "ffn_norm_w", "final_norm_w"]
    small_rows = [PACK_MIX_NORM, PACK_CONV_B, PACK_BA, PACK_BX, PACK_LAM, PACK_FFN_NORM, PACK_FINAL_NORM]
    small_w = [mix_norm_w, conv_b, lru_ba, lru_bx, lru_lambda, ffn_norm_w, fw]
    small_m = [m_mix_norm_w, m_conv_b, m_lru_ba, m_lru_bx, m_lru_lambda, m_ffn_norm_w, m_final_norm_w.reshape(1, -1)]
    small_v = [v_mix_norm_w, v_conv_b, v_lru_ba, v_lru_bx, v_lru_lambda, v_ffn_norm_w, v_final_norm_w.reshape(1, -1)]

    def pack_small(vec_list, meta_t, conv_t):
        return jnp.concatenate([t.reshape(8, HEAD_DIM) for t in vec_list] + [meta_t, jnp.pad(conv_t[0], pad4)], axis=0)

    g_small = jnp.concatenate([small_sum[r].reshape(8, HEAD_DIM) for r in small_rows] + [g_meta, g_conv], axis=0)
    outs_small = _adamw(g_small[None], pack_small(small_w, meta_tokens, conv_w),
                        pack_small(small_m, m_meta_tokens, m_conv_w), pack_small(small_v, v_meta_tokens, v_conv_w))
    for idx, name in enumerate(small_names):
        shape = final_norm_w.shape if name == "final_norm_w" else (1, D_MODEL)
        res[name] = [t[8 * idx:8 * idx + 8].reshape(shape) for t in outs_small]
    res["meta_tokens"] = [t[56:72] for t in outs_small]
    res["conv_w"] = [t[72:76].reshape(1, 4, HEAD_DIM) for t in outs_small]

    order = ["meta_tokens", "mix_norm_w", "w_in", "conv_w", "conv_b", "lru_wa", "lru_ba", "lru_wx", "lru_bx",
             "lru_lambda", "w_branch_ret", "w_branch_lru", "w_out", "ffn_norm_w", "w_ffn_in", "w_ffn_out",
             "final_norm_w"]
    out = [loss, grad_x[None]]
    for kind in range(4):
        out += [res[name][kind] for name in order]
    return tuple(out)
```

```python
import jax
import jax.numpy as jnp
from jax import lax
from jax.experimental import pallas as pl
from jax.experimental.pallas import tpu as pltpu

F32 = jnp.float32
BF16 = jnp.bfloat16

D_MODEL = 1024
N_META = 16
CHUNK = 128
PAD_ROWS = CHUNK - N_META
HEADS = 8
HEAD_DIM = 128
ROPE_BASE = 10000.0
QK_SCALE = HEAD_DIM ** -0.5
LRU_BLOCKS = 4
LRU_BLOCK = 256
LRU_C = 8.0
FFN_HIDDEN = 2816
N_DEV = 8
FFN_SHARD = 2 * FFN_HIDDEN // N_DEV
FFN_GROUP = 768
FFN_GROUPS = 4
FFN_OUT_SHARD = FFN_HIDDEN // N_DEV
NORM_EPS = 1e-6

ADAM_LR = 0.001
ADAM_B1 = 0.9
ADAM_B2 = 0.999
ADAM_EPS = 1e-08
ADAM_WD = 0.01
ADAM_STEP = 10

VMEM_LIMIT = 56 * 1024 * 1024
MESH_ID = pl.DeviceIdType.MESH
ANY = pl.BlockSpec(memory_space=pl.ANY)


def _cparams(sem):
    return pltpu.CompilerParams(dimension_semantics=sem, vmem_limit_bytes=VMEM_LIMIT)


def _tile(rows, cap):
    t = cap - cap % 64
    while rows % t:
        t -= 64
    return t


def _dot(a, b):
    return jnp.dot(a, b, preferred_element_type=F32)


def _dot_nt(a, b):
    return lax.dot_general(a, b, (((1,), (1,)), ((), ())), preferred_element_type=F32)


def _dot_tn(a, b):
    return lax.dot_general(a, b, (((0,), (0,)), ((), ())), preferred_element_type=F32)


def _sigmoid(x):
    return 0.5 * jnp.tanh(0.5 * x) + 0.5


def _gelu_parts(x):
    k = 0.7978845608028654
    inner = k * (x + 0.044715 * x * x * x)
    t = jnp.tanh(inner)
    g = 0.5 * x * (1.0 + t)
    dg = 0.5 * (1.0 + t) + 0.5 * x * (1.0 - t * t) * k * (1.0 + 3.0 * 0.044715 * x * x)
    return g, dg


def _rot(x, cos2, sin2):
    return x * cos2 + pltpu.roll(x, HEAD_DIM // 2, 1) * sin2


def _rot_t(dx, cos2, sin2):
    return dx * cos2 - pltpu.roll(dx, HEAD_DIM // 2, 1) * sin2


def _rms_bwd(x, w, dy):
    rs = lax.rsqrt(jnp.mean(x * x, axis=-1, keepdims=True) + NORM_EPS)
    nh = x * rs
    dw = jnp.sum(dy * nh, axis=0, keepdims=True)
    dn = dy * w
    dx = rs * (dn - nh * jnp.mean(dn * nh, axis=-1, keepdims=True))
    return dx, dw


def _retention_consts():
    h = jnp.arange(HEADS, dtype=F32)
    log_g = jnp.log(1.0 - 2.0 ** (-5.0 - h))
    idx = jnp.arange(CHUNK, dtype=F32)
    diff = idx[:, None] - idx[None, :]
    intra = jnp.where(diff[None] >= 0, jnp.exp(jnp.maximum(diff, 0.0)[None] * log_g[:, None, None]), 0.0)
    q_decay = jnp.exp((idx + 1.0)[:, None] * log_g[None, :])
    k_decay = jnp.exp((CHUNK - 1.0 - idx)[:, None] * log_g[None, :])
    chunk_decay = jnp.exp(CHUNK * log_g)
    shape = (HEADS, CHUNK, CHUNK)
    qd = jnp.broadcast_to(q_decay.T[:, :, None], shape)
    kd = jnp.broadcast_to(k_decay.T[:, :, None], shape)
    cd = jnp.broadcast_to(chunk_decay[:, None, None], shape)
    return jnp.stack([intra, qd, kd, cd])


def _rope_tables(rows):
    pos = jnp.maximum(jnp.arange(rows) - PAD_ROWS, 0).astype(F32)
    inv_freq = ROPE_BASE ** (-jnp.arange(0, HEAD_DIM, 2, dtype=F32) / HEAD_DIM)
    ang = pos[:, None] * inv_freq[None, :]
    cos, sin = jnp.cos(ang), jnp.sin(ang)
    return jnp.concatenate([cos, cos], axis=1), jnp.concatenate([-sin, sin], axis=1)


def _my_index():
    return 4 * lax.axis_index("x") + 2 * lax.axis_index("y") + lax.axis_index("c")


def _peer(k):
    x, y, c = lax.axis_index("x"), lax.axis_index("y"), lax.axis_index("c")
    px = 1 - x if k & 4 else x
    py = 1 - y if k & 2 else y
    pc = 1 - c if k & 1 else c
    return (px, py, pc), 4 * px + 2 * py + pc


def _push_sems(n_arr):
    n_rem = (N_DEV - 1) * n_arr
    return [pltpu.SemaphoreType.DMA((n_rem,)), pltpu.SemaphoreType.DMA((n_rem,)), pltpu.SemaphoreType.DMA((n_arr,))]


class _Push:
    def __init__(self, send_part, land_slot, sems, n_arr):
        self.send_part, self.land_slot, self.n_arr = send_part, land_slot, n_arr
        self.send_sems, self.recv_sems, self.loc_sems = sems

    def _remote(self, k, a, src, dst, pos):
        idx = (k - 1) * self.n_arr + a
        return pltpu.make_async_remote_copy(src_ref=src, dst_ref=dst, send_sem=self.send_sems.at[idx],
                                            recv_sem=self.recv_sems.at[idx], device_id=pos, device_id_type=MESH_ID)

    def _outgoing(self):
        me = _my_index()
        land = self.land_slot(me)
        remote = []
        for k in range(1, N_DEV):
            pos, p = _peer(k)
            src = self.send_part(p)
            remote += [self._remote(k, a, src[a], land[a], pos) for a in range(self.n_arr)]
        own = self.send_part(me)
        local = [pltpu.make_async_copy(own[a], land[a], self.loc_sems.at[a]) for a in range(self.n_arr)]
        return remote, local

    def start(self):
        remote, local = self._outgoing()
        for cp in remote + local:
            cp.start()

    def wait_recv_from(self, k):
        own = self.send_part(_my_index())
        pos, p = _peer(k)
        land = self.land_slot(p)
        for a in range(self.n_arr):
            self._remote(k, a, own[a], land[a], pos).wait_recv()

    def wait_sends(self):
        remote, local = self._outgoing()
        for cp in remote:
            cp.wait_send()
        for cp in local:
            cp.wait()

    def wait(self):
        for k in range(1, N_DEV):
            self.wait_recv_from(k)
        self.wait_sends()


DIRECT = (1, 2, 4, 6)
RELAYED = (2, 4, 6)


def _gather_by_chip_sems(n_arr):
    direct, relayed = len(DIRECT) * n_arr, len(RELAYED) * n_arr
    return [pltpu.SemaphoreType.DMA((direct,)), pltpu.SemaphoreType.DMA((direct,)),
            pltpu.SemaphoreType.DMA((relayed,)), pltpu.SemaphoreType.DMA((relayed,)), pltpu.SemaphoreType.DMA((n_arr,))]


class _GatherByChip:
    def __init__(self, srcs, land_slot, sems, n_arr):
        self.srcs, self.land_slot, self.n_arr = srcs, land_slot, n_arr
        self.send_sems, self.recv_sems, self.relay_send_sems, self.relay_recv_sems, self.loc_sems = sems

    def _direct(self, k, a, slot):
        idx = DIRECT.index(k) * self.n_arr + a
        return pltpu.make_async_remote_copy(src_ref=self.srcs[a], dst_ref=self.land_slot(slot)[a],
                                            send_sem=self.send_sems.at[idx], recv_sem=self.recv_sems.at[idx],
                                            device_id=_peer(k)[0], device_id_type=MESH_ID)

    def _relay(self, q, a, slot):
        idx = RELAYED.index(q) * self.n_arr + a
        block = self.land_slot(slot)[a]
        return pltpu.make_async_remote_copy(src_ref=block, dst_ref=block, send_sem=self.relay_send_sems.at[idx],
                                            recv_sem=self.relay_recv_sems.at[idx], device_id=_peer(1)[0],
                                            device_id_type=MESH_ID)

    def _own(self, a):
        return pltpu.make_async_copy(self.srcs[a], self.land_slot(_my_index())[a], self.loc_sems.at[a])

    def start(self):
        me = _my_index()
        for k in DIRECT:
            for a in range(self.n_arr):
                self._direct(k, a, me).start()
        for a in range(self.n_arr):
            self._own(a).start()

    def relay(self):
        for q in RELAYED:
            p = _peer(q)[1]
            for a in range(self.n_arr):
                self._direct(q, a, p).wait_recv()
                self._relay(q, a, p).start()

    def wait(self):
        me = _my_index()
        for a in range(self.n_arr):
            self._direct(1, a, _peer(1)[1]).wait_recv()
        for q in RELAYED:
            for a in range(self.n_arr):
                self._relay(q, a, _peer(q + 1)[1]).wait_recv()
        for k in DIRECT:
            for a in range(self.n_arr):
                self._direct(k, a, me).wait_send()
        for q in RELAYED:
            for a in range(self.n_arr):
                self._relay(q, a, _peer(q)[1]).wait_send()
        for a in range(self.n_arr):
            self._own(a).wait()


class _Ride:
    def __init__(self, arrays, out_shapes, send_part, land_slot, zero_dsts=None, zero_shape=None, n_zero=0,
                 gather_by_chip=False):
        self.arrays, self.out_shapes = list(arrays), list(out_shapes)
        self.send_part, self.land_slot, self.n = send_part, land_slot, len(arrays)
        self.zero_dsts, self.zero_shape, self.n_zero = zero_dsts, zero_shape, n_zero
        self.gather_by_chip = gather_by_chip

    def specs(self):
        return [ANY] * self.n

    def scratch(self):
        extra = [pltpu.SemaphoreType.DMA((self.n_zero,)), pltpu.VMEM(self.zero_shape, BF16)] if self.n_zero else []
        sems = _gather_by_chip_sems(self.n) if self.gather_by_chip else _push_sems(self.n)
        return sems + extra

    def push(self, in_refs, out_refs, scratch):
        ride = self
        n_sems = 5 if self.gather_by_chip else 3
        land = lambda s: ride.land_slot(out_refs, s)
        if self.gather_by_chip:
            push = _GatherByChip(list(in_refs), land, tuple(scratch[:n_sems]), self.n)
        else:
            push = _Push(lambda p: ride.send_part(in_refs, p), land, tuple(scratch[:n_sems]), self.n)

        class Both:
            def _fills(self):
                if not ride.n_zero:
                    return []
                zsems, zbuf = scratch[n_sems], scratch[n_sems + 1]
                return [pltpu.make_async_copy(zbuf, dst, zsems.at[z]) for z, dst in enumerate(ride.zero_dsts(out_refs))]

            def start(self):
                push.start()
                if ride.n_zero:
                    scratch[n_sems + 1][...] = jnp.zeros(ride.zero_shape, BF16)
                for cp in self._fills():
                    cp.start()

            def relay(self):
                if ride.gather_by_chip:
                    push.relay()

            def wait(self):
                push.wait()
                for cp in self._fills():
                    cp.wait()

        return Both()


def _slot_of_sender(out_refs, s):
    return [r.at[s] for r in out_refs]


def _push_call(name, arrays, out_shapes, send_part, land_slot):
    n_arr = len(arrays)

    def body(*refs):
        ins, outs, sems = refs[:n_arr], refs[n_arr:2 * n_arr], refs[2 * n_arr:]
        push = _Push(lambda p: send_part(ins, p), lambda s: land_slot(outs, s), sems, n_arr)
        push.start()
        push.wait()

    return pl.pallas_call(
        body, name=name, in_specs=[ANY] * n_arr, out_specs=[ANY] * n_arr, out_shape=out_shapes,
        scratch_shapes=_push_sems(n_arr), compiler_params=pltpu.CompilerParams(has_side_effects=True),
    )(*arrays)


LRU_ROWS = LRU_BLOCK // N_DEV
FFN_PAD_ROWS = FFN_GROUP - 2 * FFN_OUT_SHARD


def _half_rows(d):
    return pl.ds(pl.multiple_of((d % 2) * FFN_OUT_SHARD, 16), FFN_OUT_SHARD)


MIXER_SHAPES = [
    jax.ShapeDtypeStruct((N_DEV, D_MODEL // N_DEV, D_MODEL), BF16),
    jax.ShapeDtypeStruct((N_DEV, D_MODEL // N_DEV, D_MODEL), BF16),
    jax.ShapeDtypeStruct((N_DEV, D_MODEL // N_DEV, D_MODEL), BF16),
    jax.ShapeDtypeStruct((N_DEV, LRU_BLOCKS, LRU_ROWS, LRU_BLOCK), BF16),
    jax.ShapeDtypeStruct((N_DEV, LRU_BLOCKS, LRU_ROWS, LRU_BLOCK), BF16),
]


def _by_owner(t):
    return t.reshape(LRU_BLOCKS, N_DEV, LRU_ROWS, LRU_BLOCK).transpose(1, 0, 2, 3)


def _from_owners(t):
    return t.transpose(1, 0, 2, 3).reshape(LRU_BLOCKS, LRU_BLOCK, LRU_BLOCK)


def _mixer_weights_ride(shards):
    return _Ride(shards, MIXER_SHAPES, lambda ins, p: list(ins), _slot_of_sender, gather_by_chip=True)


def _wfo_ride(shard):
    zero_dsts = lambda outs: [outs[0].at[g, pl.ds(2 * FFN_OUT_SHARD, FFN_PAD_ROWS), :] for g in range(FFN_GROUPS)]
    return _Ride([shard], [jax.ShapeDtypeStruct((FFN_GROUPS, FFN_GROUP, D_MODEL), BF16)], lambda ins, p: list(ins),
                 lambda outs, d: [outs[0].at[d // 2, _half_rows(d), :]], zero_dsts, (FFN_PAD_ROWS, D_MODEL), FFN_GROUPS,
                 gather_by_chip=True)


W_IN_USE_ORDER = (0, 1, 2, 4, 6, 3, 5, 7)


def _arrival_rank_to_relation(jj):
    k = W_IN_USE_ORDER[-1]
    for pos in reversed(range(N_DEV - 1)):
        k = jnp.where(jj == pos, W_IN_USE_ORDER[pos], k)
    return k


def _in_proj(h0, norm_w, win_shard, me_arr):
    rows = h0.shape[0]
    tm = _tile(rows, 1664)
    n_i = rows // tm

    direct, relayed = DIRECT, RELAYED

    def body(me_ref, h_ref, nw_ref, wsh_ref, proj_ref, u_ref, wing_ref, u_all, wbuf, copy_sem,
             send_sems, recv_sems, relay_send_sems, relay_recv_sems, own_sem):
        del me_ref
        jj, i = pl.program_id(0), pl.program_id(1)
        me = _my_index()
        sibling = _peer(1)[0]

        def direct_copy(k, slot):
            n = direct.index(k)
            return pltpu.make_async_remote_copy(src_ref=wsh_ref, dst_ref=wing_ref.at[slot], send_sem=send_sems.at[n],
                                                recv_sem=recv_sems.at[n], device_id=_peer(k)[0], device_id_type=MESH_ID)

        def relay_copy(q, slot):
            n = relayed.index(q)
            return pltpu.make_async_remote_copy(src_ref=wing_ref.at[slot], dst_ref=wing_ref.at[slot],
                                                send_sem=relay_send_sems.at[n], recv_sem=relay_recv_sems.at[n],
                                                device_id=sibling, device_id_type=MESH_ID)

        own_slot = pltpu.make_async_copy(wsh_ref, wing_ref.at[me], own_sem)

        @pl.when(jnp.logical_and(jj == 0, i == 0))
        def _():
            for k in direct:
                direct_copy(k, me).start()
            own_slot.start()
            own = pltpu.make_async_copy(wsh_ref, wbuf, copy_sem)
            own.start()
            own.wait()

        for k in range(1, N_DEV):
            rank = W_IN_USE_ORDER.index(k)

            @pl.when(jnp.logical_and(jj == rank, i == 0))
            def _(k=k):
                p = _peer(k)[1]
                if k in direct:
                    direct_copy(k, p).wait_recv()
                    if k in relayed:
                        relay_copy(k, p).start()
                else:
                    relay_copy(k - 1, p).wait_recv()
                landed = pltpu.make_async_copy(wing_ref.at[p], wbuf, copy_sem)
                landed.start()
                landed.wait()

        rows_i = pl.ds(pl.multiple_of(i * tm, tm), tm)

        @pl.when(jj == 0)
        def _():
            x = h_ref[...]
            rs = lax.rsqrt(jnp.mean(x * x, axis=-1, keepdims=True) + NORM_EPS)
            u = (x * rs * nw_ref[...]).astype(BF16)
            u_all[rows_i, :] = u
            u_ref[...] = u
        proj_ref[...] = _dot(u_all[rows_i, :], wbuf[...]).astype(BF16)

        @pl.when(jnp.logical_and(jj == N_DEV - 1, i == n_i - 1))
        def _():
            for k in direct:
                direct_copy(k, me).wait_send()
            for q in relayed:
                relay_copy(q, _peer(q)[1]).wait_send()
            own_slot.wait()

    first_pass = lambda jj, i: jnp.where(jj == 0, i, n_i - 1)
    grid_spec = pltpu.PrefetchScalarGridSpec(
        num_scalar_prefetch=1, grid=(N_DEV, n_i),
        in_specs=[pl.BlockSpec((tm, D_MODEL), lambda jj, i, me: (first_pass(jj, i), 0)),
                  pl.BlockSpec((1, D_MODEL), lambda jj, i, me: (0, 0)), ANY],
        out_specs=[pl.BlockSpec((tm, D_MODEL), lambda jj, i, me: (i, me[0] ^ _arrival_rank_to_relation(jj))),
                   pl.BlockSpec((tm, D_MODEL), lambda jj, i, me: (first_pass(jj, i), 0)), ANY],
        scratch_shapes=[pltpu.VMEM((rows, D_MODEL), BF16), pltpu.VMEM((D_MODEL, D_MODEL), BF16),
                        pltpu.SemaphoreType.DMA(()),
                        pltpu.SemaphoreType.DMA((len(direct),)), pltpu.SemaphoreType.DMA((len(direct),)),
                        pltpu.SemaphoreType.DMA((len(relayed),)), pltpu.SemaphoreType.DMA((len(relayed),)),
                        pltpu.SemaphoreType.DMA(())])
    return pl.pallas_call(
        body, name="in_proj", grid_spec=grid_spec,
        out_shape=[jax.ShapeDtypeStruct((rows, N_DEV * D_MODEL), BF16),
                   jax.ShapeDtypeStruct((rows, D_MODEL), BF16),
                   jax.ShapeDtypeStruct((N_DEV, D_MODEL, D_MODEL), BF16)],
        compiler_params=pltpu.CompilerParams(dimension_semantics=("arbitrary", "arbitrary"),
                                             vmem_limit_bytes=VMEM_LIMIT, has_side_effects=True),
    )(me_arr, h0, norm_w, win_shard)


def _seg_spec(rows_per_block, seg):
    return pl.BlockSpec((rows_per_block, D_MODEL), lambda n, seg=seg: (n, seg))


def _chunks_per_step(n_chunks):
    return next(c for c in (5, 3, 2, 1) if n_chunks % c == 0)


def _retention_fwd(proj, cos2, sin2, dec, ride):
    rows = proj.shape[0]
    n_chunks = rows // CHUNK
    per_step = _chunks_per_step(n_chunks)
    n_steps = n_chunks // per_step
    tm = per_step * CHUNK
    n_r = ride.n

    def body(q_ref, k_ref, v_ref, g_ref, c_ref, s_ref, dec_ref, *refs):
        o_ref, zr_ref, st_ref = refs[n_r:n_r + 3]
        state = refs[2 * n_r + 3]
        push = ride.push(refs[:n_r], refs[n_r + 3:2 * n_r + 3], refs[2 * n_r + 4:])

        @pl.when(pl.program_id(0) == 0)
        def _():
            push.start()
            state[...] = jnp.zeros_like(state)

        for h in range(HEADS):
            sl = slice(HEAD_DIM * h, HEAD_DIM * (h + 1))
            st = state[h]
            for c in range(per_step):
                rw = slice(CHUNK * c, CHUNK * (c + 1))
                cos_t, sin_t = c_ref[rw, :], s_ref[rw, :]
                qh = _rot(q_ref[rw, sl].astype(F32), cos_t, sin_t)
                kh = _rot(k_ref[rw, sl].astype(F32), cos_t, sin_t) * QK_SCALE
                qb, kb, vb = qh.astype(BF16), kh.astype(BF16), v_ref[rw, sl]
                s = _dot_nt(qb, kb) * dec_ref[0, h]
                st_ref[c, h] = st
                o = _dot(s.astype(BF16), vb) + _dot(qb, st.astype(BF16)) * dec_ref[1, h]
                st = st * dec_ref[3, h] + _dot_tn((kh * dec_ref[2, h]).astype(BF16), vb)
                o_ref[rw, sl] = o.astype(BF16)
                r = lax.rsqrt(jnp.mean(o * o, axis=-1, keepdims=True) + NORM_EPS)
                g = g_ref[rw, sl].astype(F32)
                zr_ref[rw, sl] = (g * _sigmoid(g) * (o * r)).astype(BF16)
            state[h] = st

        @pl.when(pl.program_id(0) == n_steps // 2)
        def _():
            push.relay()

        @pl.when(pl.program_id(0) == n_steps - 1)
        def _():
            push.wait()

    tab = pl.BlockSpec((tm, HEAD_DIM), lambda n: (n, 0))
    return pl.pallas_call(
        body, name="retention_fwd", grid=(n_steps,),
        in_specs=[_seg_spec(tm, 0), _seg_spec(tm, 1), _seg_spec(tm, 2), _seg_spec(tm, 3), tab, tab,
                  pl.BlockSpec((4, HEADS, CHUNK, CHUNK), lambda n: (0, 0, 0, 0))] + ride.specs(),
        out_specs=[pl.BlockSpec((tm, D_MODEL), lambda n: (n, 0)),
                   pl.BlockSpec((tm, D_MODEL), lambda n: (n, 0)),
                   pl.BlockSpec((per_step, HEADS, HEAD_DIM, HEAD_DIM), lambda n: (n, 0, 0, 0))] + ride.specs(),
        out_shape=[jax.ShapeDtypeStruct((rows, D_MODEL), BF16),
                   jax.ShapeDtypeStruct((rows, D_MODEL), BF16),
                   jax.ShapeDtypeStruct((n_chunks, HEADS, HEAD_DIM, HEAD_DIM), F32)] + ride.out_shapes,
        scratch_shapes=[pltpu.VMEM((HEADS, HEAD_DIM, HEAD_DIM), F32)] + ride.scratch(),
        compiler_params=pltpu.CompilerParams(dimension_semantics=("arbitrary",), vmem_limit_bytes=VMEM_LIMIT,
                                             has_side_effects=True),
    )(proj, proj, proj, proj, cos2, sin2, dec, *ride.arrays)


def _lru_gates(c, ba, bx, wa_ref, wx_ref):
    pre_r, pre_i = [], []
    for g in range(LRU_BLOCKS):
        cg = c[:, LRU_BLOCK * g:LRU_BLOCK * (g + 1)].astype(BF16)
        pre_r.append(_dot(cg, wa_ref[g]))
        pre_i.append(_dot(cg, wx_ref[g]))
    return _sigmoid(jnp.concatenate(pre_r, axis=1) + ba), _sigmoid(jnp.concatenate(pre_i, axis=1) + bx)


def _lru_decay(r, lam):
    sp = jnp.maximum(-lam, 0.0) + jnp.log(1.0 + jnp.exp(-jnp.abs(lam)))
    log_a = -LRU_C * r * sp
    a = jnp.exp(log_a)
    one_minus_a2 = -jnp.tanh(log_a) * (a * a + 1.0)
    inv_mult = lax.rsqrt(jnp.maximum(one_minus_a2, 1e-30))
    return a, one_minus_a2 * inv_mult, inv_mult, sp


def _conv_taps(xbuf, tm, cw_ref, cb_ref):
    c = cb_ref[...] + cw_ref[3:4, :] * xbuf[8:8 + tm, :]
    for back in (1, 2, 3):
        c = c + cw_ref[3 - back:4 - back, :] * xbuf[8 - back:8 - back + tm, :]
    return c


def _lru_fwd(proj, conv_w, conv_b, ba, bx, lam, wa_g, wx_g, ride):
    rows = proj.shape[0]
    tm = _tile(rows, 640)
    n_t = rows // tm
    n_r = ride.n

    def body(x_ref, gt_ref, cw_ref, cb_ref, ba_ref, bx_ref, lam_ref, wa_ref, wx_ref, *refs):
        hs_ref, zl_ref, cri_ref = refs[n_r:n_r + 3]
        xbuf, abuf, ubuf, hcar = refs[2 * n_r + 3:2 * n_r + 7]
        push = ride.push(refs[:n_r], refs[n_r + 3:2 * n_r + 3], refs[2 * n_r + 7:])
        i = pl.program_id(0)

        @pl.when(i == 0)
        def _():
            push.start()
            xbuf[0:8, :] = jnp.zeros((8, D_MODEL), F32)
            hcar[...] = jnp.zeros_like(hcar)

        xbuf[8:8 + tm, :] = x_ref[...].astype(F32)
        c = _conv_taps(xbuf, tm, cw_ref, cb_ref)
        xbuf[0:8, :] = xbuf[tm:tm + 8, :]
        r, ig = _lru_gates(c, ba_ref[...], bx_ref[...], wa_ref, wx_ref)
        a, mult, _, _ = _lru_decay(r, lam_ref[...])
        cri_ref[0] = c.astype(BF16)
        cri_ref[1] = r.astype(BF16)
        cri_ref[2] = ig.astype(BF16)
        row = i * tm + lax.broadcasted_iota(jnp.int32, (tm, 1), 0)
        abuf[...] = a
        ubuf[...] = jnp.where(row >= PAD_ROWS, mult * (ig * c), 0.0)

        sub = lax.broadcasted_iota(jnp.int32, (8, D_MODEL), 0)

        def block(b, carry):
            off = pl.multiple_of(b * 8, 8)
            av, uv = abuf[pl.ds(off, 8), :], ubuf[pl.ds(off, 8), :]
            for s in (1, 2, 4):
                us = jnp.where(sub >= s, pltpu.roll(uv, s, 0), 0.0)
                as_ = jnp.where(sub >= s, pltpu.roll(av, s, 0), 1.0)
                uv = uv + av * us
                av = av * as_
            hv = uv + av * carry
            ubuf[pl.ds(off, 8), :] = hv
            return hv[7:8, :]

        hcar[...] = lax.fori_loop(0, tm // 8, block, hcar[...])
        gl, _ = _gelu_parts(gt_ref[...].astype(F32))
        hs = ubuf[...]
        hs_ref[...] = hs.astype(BF16)
        zl_ref[...] = (gl * hs).astype(BF16)

        @pl.when(i == n_t // 2)
        def _():
            push.relay()

        @pl.when(i == n_t - 1)
        def _():
            push.wait()

    vec = pl.BlockSpec((1, D_MODEL), lambda i: (0, 0))
    mat = pl.BlockSpec((LRU_BLOCKS, LRU_BLOCK, LRU_BLOCK), lambda i: (0, 0, 0))
    row = pl.BlockSpec((tm, D_MODEL), lambda i: (i, 0))
    return pl.pallas_call(
        body, name="lru_fwd", grid=(n_t,),
        in_specs=[_seg_spec(tm, 4), _seg_spec(tm, 5), pl.BlockSpec((4, D_MODEL), lambda i: (0, 0)),
                  vec, vec, vec, vec, mat, mat] + ride.specs(),
        out_specs=[row, row, pl.BlockSpec((3, tm, D_MODEL), lambda i: (0, i, 0))] + ride.specs(),
        out_shape=[jax.ShapeDtypeStruct((rows, D_MODEL), BF16)] * 2
        + [jax.ShapeDtypeStruct((3, rows, D_MODEL), BF16)] + ride.out_shapes,
        scratch_shapes=[pltpu.VMEM((tm + 8, D_MODEL), F32), pltpu.VMEM((tm, D_MODEL), F32),
                        pltpu.VMEM((tm, D_MODEL), F32), pltpu.VMEM((1, D_MODEL), F32)] + ride.scratch(),
        compiler_params=pltpu.CompilerParams(dimension_semantics=("arbitrary",), vmem_limit_bytes=VMEM_LIMIT,
                                             has_side_effects=True),
    )(proj, proj, conv_w, conv_b, ba, bx, lam, wa_g, wx_g, *ride.arrays)


def _mix_fwd(zr, zl, proj, h0, wbr, wbl, wout, ride):
    rows = h0.shape[0]
    tm = _tile(rows, 640)
    n_t = rows // tm
    n_r = ride.n

    def body(zr_ref, zl_ref, ga_ref, gb_ref, h0_ref, wbr_ref, wbl_ref, wo_ref, *refs):
        h1_ref, yr_ref, yl_ref, mx_ref = refs[n_r:n_r + 4]
        push = ride.push(refs[:n_r], refs[n_r + 4:2 * n_r + 4], refs[2 * n_r + 4:])

        @pl.when(pl.program_id(0) == 0)
        def _():
            push.start()

        yr = _dot(zr_ref[...], wbr_ref[...])
        yl = _dot(zl_ref[...], wbl_ref[...])
        mixed = (_sigmoid(ga_ref[...].astype(F32)) * yr + _sigmoid(gb_ref[...].astype(F32)) * yl).astype(BF16)
        yr_ref[...] = yr.astype(BF16)
        yl_ref[...] = yl.astype(BF16)
        mx_ref[...] = mixed
        h1_ref[...] = h0_ref[...] + _dot(mixed, wo_ref[...])

        @pl.when(pl.program_id(0) == n_t // 2)
        def _():
            push.relay()

        @pl.when(pl.program_id(0) == n_t - 1)
        def _():
            push.wait()

    row = pl.BlockSpec((tm, D_MODEL), lambda i: (i, 0))
    wsp = pl.BlockSpec((D_MODEL, D_MODEL), lambda i: (0, 0))
    return pl.pallas_call(
        body, name="mix_fwd", grid=(n_t,),
        in_specs=[row, row, _seg_spec(tm, 6), _seg_spec(tm, 7), row, wsp, wsp, wsp] + ride.specs(),
        out_specs=[row, row, row, row] + ride.specs(),
        out_shape=[jax.ShapeDtypeStruct((rows, D_MODEL), F32)] + [jax.ShapeDtypeStruct((rows, D_MODEL), BF16)] * 3
        + ride.out_shapes,
        scratch_shapes=ride.scratch(),
        compiler_params=pltpu.CompilerParams(dimension_semantics=("arbitrary",), vmem_limit_bytes=VMEM_LIMIT,
                                             has_side_effects=True),
    )(zr, zl, proj, proj, h0, wbr, wbl, wout, *ride.arrays)


def _ffn_fwd_loss(h1, norm_w, wfi_g, wfo_g, final_w, target):
    rows = h1.shape[0]
    tm = _tile(rows, 320)
    piece = 64
    n_piece = tm // piece

    def body(h1_ref, nw_ref, wfi_ref, wfo_ref, fw_ref, *refs):
        t_refs = refs[:n_piece]
        u2_ref, g_ref, up_ref, act_ref, dh2_ref, red_ref = refs[n_piece:]
        i = pl.program_id(0)

        @pl.when(i == 0)
        def _():
            red_ref[...] = jnp.zeros_like(red_ref)

        x = h1_ref[...]
        rs = lax.rsqrt(jnp.mean(x * x, axis=-1, keepdims=True) + NORM_EPS)
        u2 = (x * rs * nw_ref[...]).astype(BF16)
        u2_ref[...] = u2
        ffn = None
        for d in range(FFN_GROUPS):
            cols = slice(FFN_GROUP * d, FFN_GROUP * (d + 1))
            g = _dot(u2, wfi_ref[d])
            up = _dot(u2, wfi_ref[d + FFN_GROUPS])
            act = (g * _sigmoid(g) * up).astype(BF16)
            g_ref[:, cols] = g.astype(BF16)
            up_ref[:, cols] = up.astype(BF16)
            act_ref[:, cols] = act
            part = _dot(act, wfo_ref[d])
            ffn = part if ffn is None else ffn + part

        h2 = x + ffn
        rs = lax.rsqrt(jnp.mean(h2 * h2, axis=-1, keepdims=True) + NORM_EPS)
        nh = h2 * rs
        fw = fw_ref[...]
        row = i * tm + lax.broadcasted_iota(jnp.int32, (tm, 1), 0)
        tgt = jnp.concatenate([t[...] for t in t_refs], axis=0)
        diff = jnp.where(row >= CHUNK, nh * fw - tgt, 0.0)
        dy = diff * (1.0 / D_MODEL)
        red_ref[0:1, :] += jnp.sum(diff * diff, axis=0, keepdims=True)
        red_ref[1:2, :] += jnp.sum(dy * nh, axis=0, keepdims=True)
        dn = dy * fw
        dh2_ref[...] = rs * (dn - nh * jnp.mean(dn * nh, axis=-1, keepdims=True))

    row = pl.BlockSpec((tm, D_MODEL), lambda i: (i, 0))
    vec = pl.BlockSpec((1, D_MODEL), lambda i: (0, 0))
    hid = pl.BlockSpec((tm, FFN_GROUPS * FFN_GROUP), lambda i: (i, 0))
    hid_shape = jax.ShapeDtypeStruct((rows, FFN_GROUPS * FFN_GROUP), BF16)
    resident = dict(pipeline_mode=pl.Buffered(1))
    head_pieces = CHUNK // piece
    t_specs = [pl.BlockSpec((piece, D_MODEL), lambda i, k=k: (jnp.maximum(i * n_piece + k - head_pieces, 0), 0))
               for k in range(n_piece)]
    return pl.pallas_call(
        body, name="ffn_fwd_loss", grid=(rows // tm,),
        in_specs=[row, vec,
                  pl.BlockSpec((2 * FFN_GROUPS, D_MODEL, FFN_GROUP), lambda i: (0, 0, 0), **resident),
                  pl.BlockSpec((FFN_GROUPS, FFN_GROUP, D_MODEL), lambda i: (0, 0, 0), **resident),
                  vec] + t_specs,
        out_specs=[row, hid, hid, hid, row, pl.BlockSpec((8, D_MODEL), lambda i: (0, 0))],
        out_shape=[jax.ShapeDtypeStruct((rows, D_MODEL), BF16), hid_shape, hid_shape, hid_shape,
                   jax.ShapeDtypeStruct((rows, D_MODEL), F32), jax.ShapeDtypeStruct((8, D_MODEL), F32)],
        compiler_params=_cparams(("arbitrary",)),
    )(h1, norm_w, wfi_g, wfo_g, final_w, *([target] * n_piece))


def _wgrad(a, b, ka, tn, out_dtype, b_halves=False):
    rows = a.shape[0]
    na = a.shape[1] // ka
    tm = _tile(rows, 1664)
    nm = rows // tm
    if b_halves:
        per_half = b.shape[2] // tn
        nb = 2 * per_half
        b_spec = pl.BlockSpec((None, tm, tn), lambda p, q, m: (q // per_half, m, q % per_half))
    else:
        nb = b.shape[1] // tn
        b_spec = pl.BlockSpec((tm, tn), lambda p, q, m: (m, q))

    def body(a_ref, b_ref, o_ref, acc):
        m = pl.program_id(2)

        @pl.when(m == 0)
        def _():
            acc[...] = jnp.zeros_like(acc)

        acc[...] += _dot_tn(a_ref[...].astype(BF16), b_ref[...].astype(BF16))

        @pl.when(m == nm - 1)
        def _():
            o_ref[...] = acc[...].astype(out_dtype)

    return pl.pallas_call(
        body, name="wgrad", grid=(na, nb, nm),
        in_specs=[pl.BlockSpec((tm, ka), lambda p, q, m: (m, p)), b_spec],
        out_specs=pl.BlockSpec((None, None, ka, tn), lambda p, q, m: (p, q, 0, 0)),
        out_shape=jax.ShapeDtypeStruct((na, nb, ka, tn), out_dtype),
        scratch_shapes=[pltpu.VMEM((ka, tn), F32)],
        compiler_params=_cparams(("parallel", "parallel", "arbitrary")),
    )(a, b)


WIN_NEAR = (2, 4, 3, 5, 1)
WIN_FAR = (6, 7)
WIN_ORDER = WIN_FAR + WIN_NEAR + (0,)


def _w_in_relation_at(jj):
    k = 0
    for pos in reversed(range(len(WIN_ORDER) - 1)):
        k = jnp.where(jj == pos, WIN_ORDER[pos], k)
    return k


def _wgrad_w_in(u, dproj, me_arr):
    rows = u.shape[0]
    tm = _tile(rows, 1664)
    nm = rows // tm
    n_near = len(WIN_NEAR)

    def body(me_ref, a_ref, b_ref, far_ref, land_ref, acc, sbuf, send_sems, recv_sems, own_sem):
        del me_ref
        jj, m = pl.program_id(0), pl.program_id(1)

        def near_copy(n):
            k = WIN_NEAR[n]
            return pltpu.make_async_remote_copy(src_ref=sbuf.at[n], dst_ref=land_ref.at[k], send_sem=send_sems.at[n],
                                                recv_sem=recv_sems.at[n], device_id=_peer(k)[0], device_id_type=MESH_ID)

        own_copy = pltpu.make_async_copy(sbuf.at[n_near], land_ref.at[0], own_sem)

        @pl.when(m == 0)
        def _():
            acc[...] = jnp.zeros_like(acc)

        acc[...] += _dot_tn(a_ref[...], b_ref[...])

        for pos, k in enumerate(WIN_ORDER):
            @pl.when(jnp.logical_and(jj == pos, m == nm - 1))
            def _(k=k):
                block = acc[...].astype(BF16)
                if k in WIN_FAR:
                    far_ref[...] = block
                elif k == 0:
                    sbuf[n_near] = block
                    own_copy.start()
                else:
                    sbuf[WIN_NEAR.index(k)] = block
                    near_copy(WIN_NEAR.index(k)).start()

        @pl.when(jnp.logical_and(jj == N_DEV - 1, m == nm - 1))
        def _():
            for n in range(n_near):
                near_copy(n).wait_recv()
            for n in range(n_near):
                near_copy(n).wait_send()
            own_copy.wait()

    grid_spec = pltpu.PrefetchScalarGridSpec(
        num_scalar_prefetch=1, grid=(N_DEV, nm),
        in_specs=[pl.BlockSpec((tm, D_MODEL), lambda jj, m, me: (m, 0)),
                  pl.BlockSpec((tm, D_MODEL), lambda jj, m, me: (m, me[0] ^ _w_in_relation_at(jj)))],
        out_specs=[pl.BlockSpec((None, D_MODEL, D_MODEL), lambda jj, m, me: (jnp.minimum(jj, len(WIN_FAR) - 1), 0, 0)),
                   ANY],
        scratch_shapes=[pltpu.VMEM((D_MODEL, D_MODEL), F32), pltpu.VMEM((n_near + 1, D_MODEL, D_MODEL), BF16),
                        pltpu.SemaphoreType.DMA((n_near,)), pltpu.SemaphoreType.DMA((n_near,)),
                        pltpu.SemaphoreType.DMA(())])
    return pl.pallas_call(
        body, name="wgrad_w_in", grid_spec=grid_spec,
        out_shape=[jax.ShapeDtypeStruct((len(WIN_FAR), D_MODEL, D_MODEL), BF16),
                   jax.ShapeDtypeStruct((n_near + 1, D_MODEL, D_MODEL), BF16)],
        compiler_params=pltpu.CompilerParams(dimension_semantics=("arbitrary", "arbitrary"),
                                             vmem_limit_bytes=VMEM_LIMIT, has_side_effects=True),
    )(me_arr, u, dproj)


def _ffn_bwd(dh2, g, up, h1, norm_w, wfi_g, wfo_g):
    rows = h1.shape[0]
    tm = _tile(rows, 320)

    def body(dh2_ref, g_ref, up_ref, h1_ref, nw_ref, wfi_ref, wfo_ref, dgu_ref, dh1_ref, dw_ref):
        @pl.when(pl.program_id(0) == 0)
        def _():
            dw_ref[...] = jnp.zeros_like(dw_ref)

        dh2 = dh2_ref[...]
        dh2_b = dh2.astype(BF16)
        du2 = None
        for d in range(FFN_GROUPS):
            cols = slice(FFN_GROUP * d, FFN_GROUP * (d + 1))
            dact = _dot_nt(dh2_b, wfo_ref[d])
            gv, uv = g_ref[:, cols].astype(F32), up_ref[:, cols].astype(F32)
            sg = _sigmoid(gv)
            dg = (dact * uv * (sg * (1.0 + gv * (1.0 - sg)))).astype(BF16)
            dup = (dact * (gv * sg)).astype(BF16)
            dgu_ref[0, :, cols] = dg
            dgu_ref[1, :, cols] = dup
            part = _dot_nt(dg, wfi_ref[d]) + _dot_nt(dup, wfi_ref[d + FFN_GROUPS])
            du2 = part if du2 is None else du2 + part
        dx, dw = _rms_bwd(h1_ref[...], nw_ref[...], du2)
        dw_ref[0:1, :] += dw
        dh1_ref[...] = dh2 + dx

    row = pl.BlockSpec((tm, D_MODEL), lambda i: (i, 0))
    vec = pl.BlockSpec((1, D_MODEL), lambda i: (0, 0))
    hid = pl.BlockSpec((tm, FFN_GROUPS * FFN_GROUP), lambda i: (i, 0))
    resident = dict(pipeline_mode=pl.Buffered(1))
    return pl.pallas_call(
        body, name="ffn_bwd", grid=(rows // tm,),
        in_specs=[row, hid, hid, row, vec,
                  pl.BlockSpec((2 * FFN_GROUPS, D_MODEL, FFN_GROUP), lambda i: (0, 0, 0), **resident),
                  pl.BlockSpec((FFN_GROUPS, FFN_GROUP, D_MODEL), lambda i: (0, 0, 0), **resident)],
        out_specs=[pl.BlockSpec((2, tm, FFN_GROUPS * FFN_GROUP), lambda i: (0, i, 0)), row,
                   pl.BlockSpec((8, D_MODEL), lambda i: (0, 0))],
        out_shape=[jax.ShapeDtypeStruct((2, rows, FFN_GROUPS * FFN_GROUP), BF16),
                   jax.ShapeDtypeStruct((rows, D_MODEL), F32), jax.ShapeDtypeStruct((8, D_MODEL), F32)],
        compiler_params=_cparams(("arbitrary",)),
    )(dh2, g, up, h1, norm_w, wfi_g, wfo_g)


def _mix_bwd(dh1, yr, yl, proj, wbr, wbl, wout):
    rows = dh1.shape[0]
    tm = _tile(rows, 640)

    def body(dh1_ref, yr_ref, yl_ref, ga_ref, gb_ref, wbr_ref, wbl_ref, wo_ref,
             dyr_ref, dyl_ref, dseg_ref, dzr_ref, dzl_ref):
        dmix = _dot_nt(dh1_ref[...].astype(BF16), wo_ref[...])
        sa, sb = _sigmoid(ga_ref[...].astype(F32)), _sigmoid(gb_ref[...].astype(F32))
        dyr = (dmix * sa).astype(BF16)
        dyl = (dmix * sb).astype(BF16)
        dyr_ref[...] = dyr
        dyl_ref[...] = dyl
        dseg_ref[:, 0:D_MODEL] = (dmix * yr_ref[...].astype(F32) * (sa * (1.0 - sa))).astype(BF16)
        dseg_ref[:, D_MODEL:2 * D_MODEL] = (dmix * yl_ref[...].astype(F32) * (sb * (1.0 - sb))).astype(BF16)
        dzr_ref[...] = _dot_nt(dyr, wbr_ref[...]).astype(BF16)
        dzl_ref[...] = _dot_nt(dyl, wbl_ref[...]).astype(BF16)

    row = pl.BlockSpec((tm, D_MODEL), lambda i: (i, 0))
    wsp = pl.BlockSpec((D_MODEL, D_MODEL), lambda i: (0, 0))
    bshape = jax.ShapeDtypeStruct((rows, D_MODEL), BF16)
    return pl.pallas_call(
        body, name="mix_bwd", grid=(rows // tm,),
        in_specs=[row, row, row, _seg_spec(tm, 6), _seg_spec(tm, 7), wsp, wsp, wsp],
        out_specs=[row, row, pl.BlockSpec((tm, 2 * D_MODEL), lambda i: (i, 3)), row, row],
        out_shape=[bshape, bshape, jax.ShapeDtypeStruct((rows, N_DEV * D_MODEL), BF16), bshape, bshape],
        compiler_params=_cparams(("parallel",)),
    )(dh1, yr, yl, proj, proj, wbr, wbl, wout)


S1_SHAPES = [jax.ShapeDtypeStruct((N_DEV, D_MODEL, FFN_GROUP), BF16)]


def _s1_parts(ins, p):
    return [ins[0].at[p]]


def _lru_bwd(dzl, hs, cri, proj, dproj, conv_w, lam, wa_g, wx_g, s1_grads):
    rows = dzl.shape[0]
    tm = _tile(rows, 640)
    nt = rows // tm
    t8 = tm // 8
    n_s1 = len(s1_grads)

    def body(dzl_ref, hs_ref, hsp_ref, cri_ref, x_ref, gt_ref, cw_ref, lam_ref, wa_ref, wx_ref, dproj_in, *refs):
        del dproj_in
        s1_refs = refs[:n_s1]
        dseg_ref, dwa_ref, dwx_ref, sm_ref = refs[n_s1:n_s1 + 4]
        land_refs = refs[n_s1 + 4:2 * n_s1 + 4]
        (xbuf, abuf, mbuf, ibuf, dbuf, dcbuf, dpr_s, dpi_s, sums, conv_sums, anext, dhcar,
         send_sems, recv_sems, loc_sems) = refs[2 * n_s1 + 4:]
        step = pl.program_id(0)
        i = nt - 1 - step
        push = _Push(lambda p: _s1_parts(s1_refs, p), lambda s: [r.at[s] for r in land_refs],
                     (send_sems, recv_sems, loc_sems), n_s1)

        @pl.when(step == 0)
        def _():
            push.start()
            dwa_ref[...] = jnp.zeros_like(dwa_ref)
            dwx_ref[...] = jnp.zeros_like(dwx_ref)
            sm_ref[...] = jnp.zeros_like(sm_ref)
            anext[...] = jnp.zeros_like(anext)
            dhcar[...] = jnp.zeros_like(dhcar)
            dcbuf[tm:tm + 8, :] = jnp.zeros((8, D_MODEL), F32)

        slab, lanes = 16, 256
        lam_v = lam_ref[...]
        xbuf[0:8, :] = jnp.where(i == 0, 0.0, hsp_ref[8:16, :].astype(F32))
        sums[...] = jnp.zeros_like(sums)

        def before_scan(k, carry):
            rw = pl.ds(pl.multiple_of(k * slab, slab), slab)
            for q in range(D_MODEL // lanes):
                ln = slice(lanes * q, lanes * (q + 1))
                a, mult, inv_mult, _ = _lru_decay(cri_ref[1, rw, ln].astype(F32), lam_v[:, ln])
                abuf[rw, ln] = a
                mbuf[rw, ln] = mult
                ibuf[rw, ln] = inv_mult
                gl, dgl = _gelu_parts(gt_ref[rw, ln].astype(F32))
                dzl_v = dzl_ref[rw, ln].astype(F32)
                hs_v = hs_ref[rw, ln].astype(F32)
                dseg_ref[rw, D_MODEL + lanes * q:D_MODEL + lanes * (q + 1)] = (dzl_v * hs_v * dgl).astype(BF16)
                dbuf[rw, ln] = dzl_v * gl
                xbuf[pl.ds(pl.multiple_of(k * slab + 8, 8), slab), ln] = hs_v
            return carry

        lax.fori_loop(0, tm // slab, before_scan, 0)

        sub = lax.broadcasted_iota(jnp.int32, (8, D_MODEL), 0)

        def block(k, carry):
            dh_next, a_next = carry
            off = pl.multiple_of((t8 - 1 - k) * 8, 8)
            a_blk = abuf[pl.ds(off, 8), :]
            av = jnp.where(sub < 7, pltpu.roll(a_blk, 7, 0), a_next)
            uv = dbuf[pl.ds(off, 8), :]
            for s in (1, 2, 4):
                us = jnp.where(sub < 8 - s, pltpu.roll(uv, 8 - s, 0), 0.0)
                as_ = jnp.where(sub < 8 - s, pltpu.roll(av, 8 - s, 0), 1.0)
                uv = uv + av * us
                av = av * as_
            hv = uv + av * dh_next
            dbuf[pl.ds(off, 8), :] = hv
            return hv[0:1, :], a_blk[0:1, :]

        dh_first, a_first = lax.fori_loop(0, t8, block, (dhcar[...], anext[...]))
        dhcar[...] = dh_first
        anext[...] = a_first

        sp = jnp.maximum(-lam_v, 0.0) + jnp.log(1.0 + jnp.exp(-jnp.abs(lam_v)))
        sub_q = lax.broadcasted_iota(jnp.int32, (8, lanes), 0)
        row16 = lax.broadcasted_iota(jnp.int32, (slab, 1), 0)

        def after_scan(k, carry):
            off = pl.multiple_of(k * slab, slab)
            rw = pl.ds(off, slab)
            for q in range(D_MODEL // lanes):
                ln = slice(lanes * q, lanes * (q + 1))
                before = xbuf[pl.ds(off, 8), ln]
                h_lo = xbuf[pl.ds(pl.multiple_of(off + 8, 8), 8), ln]
                h_hi = xbuf[pl.ds(pl.multiple_of(off + 16, 8), 8), ln]
                hprev = jnp.concatenate([jnp.where(sub_q >= 1, pltpu.roll(h_lo, 1, 0), before[7:8, :]),
                                         jnp.where(sub_q >= 1, pltpu.roll(h_hi, 1, 0), h_lo[7:8, :])], axis=0)
                c, r, ig = (cri_ref[n, rw, ln].astype(F32) for n in range(3))
                a, mult, inv_mult = abuf[rw, ln], mbuf[rw, ln], ibuf[rw, ln]
                dh = dbuf[rw, ln]
                duu = jnp.where(i * tm + off + row16 >= PAD_ROWS, dh, 0.0)
                t_mult = duu * mult
                dlog_a = dh * hprev * a - duu * ig * c * (a * a) * inv_mult
                dpr = dlog_a * (-LRU_C * sp[:, ln]) * r * (1.0 - r)
                dpi = t_mult * c * ig * (1.0 - ig)
                dpr_s[rw, ln] = dpr.astype(BF16)
                dpi_s[rw, ln] = dpi.astype(BF16)
                dcbuf[rw, ln] = t_mult * ig
                sums[0, :, ln] += dlog_a * r
                sums[1, :, ln] += dpr
                sums[2, :, ln] += dpi
            return carry

        lax.fori_loop(0, tm // slab, after_scan, 0)

        dcs = []
        for g in range(LRU_BLOCKS):
            sl = slice(LRU_BLOCK * g, LRU_BLOCK * (g + 1))
            cg = cri_ref[0, :, sl]
            dpr_b, dpi_b = dpr_s[:, sl], dpi_s[:, sl]
            dwa_ref[g] += _dot_tn(cg, dpr_b)
            dwx_ref[g] += _dot_tn(cg, dpi_b)
            dcs.append(_dot_nt(dpr_b, wa_ref[g]) + _dot_nt(dpi_b, wx_ref[g]))
        dc = dcbuf[0:tm, :] + jnp.concatenate(dcs, axis=1)

        dcbuf[0:tm, :] = dc
        conv_sums[...] = jnp.zeros_like(conv_sums)

        def conv_back(k, carry):
            off = pl.multiple_of(k * slab, slab)
            rw = pl.ds(off, slab)
            for q in range(D_MODEL // lanes):
                ln = slice(lanes * q, lanes * (q + 1))
                blocks = [dcbuf[pl.ds(pl.multiple_of(off + 8 * b, 8), 8), ln] for b in range(3)]
                x_v = x_ref[rw, ln].astype(F32)
                now = jnp.concatenate(blocks[:2], axis=0)
                dlin = cw_ref[3:4, ln] * now
                conv_sums[3, :, ln] += now * x_v
                conv_sums[4, :, ln] += now
                for back in (1, 2, 3):
                    turned = [pltpu.roll(b, 8 - back, 0) for b in blocks]
                    later = jnp.concatenate([jnp.where(sub_q < 8 - back, turned[0], turned[1]),
                                             jnp.where(sub_q < 8 - back, turned[1], turned[2])], axis=0)
                    dlin = dlin + cw_ref[3 - back:4 - back, ln] * later
                    conv_sums[3 - back, :, ln] += later * x_v
                dseg_ref[rw, ln] = dlin.astype(BF16)
            return carry

        lax.fori_loop(0, tm // slab, conv_back, 0)
        dcbuf[tm:tm + 8, :] = dcbuf[0:8, :]
        for n in range(5):
            sm_ref[n:n + 1, :] += jnp.sum(conv_sums[n], axis=0, keepdims=True)
        sm_ref[5:6, :] += jnp.sum(sums[1], axis=0, keepdims=True)
        sm_ref[6:7, :] += jnp.sum(sums[2], axis=0, keepdims=True)
        sm_ref[7:8, :] += jnp.sum(sums[0], axis=0, keepdims=True) * (LRU_C * _sigmoid(-lam_v))

        @pl.when(step == nt - 1)
        def _():
            push.wait()

    rowb = pl.BlockSpec((tm, D_MODEL), lambda s: (nt - 1 - s, 0))
    t16 = tm // 16
    prev8 = pl.BlockSpec((16, D_MODEL), lambda s: (jnp.maximum((nt - 1 - s) * t16 - 1, 0), 0))
    seg = lambda k: pl.BlockSpec((tm, D_MODEL), lambda s, k=k: (nt - 1 - s, k))
    vec = pl.BlockSpec((1, D_MODEL), lambda s: (0, 0))
    mat = pl.BlockSpec((LRU_BLOCKS, LRU_BLOCK, LRU_BLOCK), lambda s: (0, 0, 0))
    mshape = jax.ShapeDtypeStruct((LRU_BLOCKS, LRU_BLOCK, LRU_BLOCK), F32)
    n_in = 10
    return pl.pallas_call(
        body, name="lru_bwd", grid=(nt,),
        in_specs=[rowb, rowb, prev8, pl.BlockSpec((3, tm, D_MODEL), lambda s: (0, nt - 1 - s, 0)), seg(4), seg(5),
                  pl.BlockSpec((4, D_MODEL), lambda s: (0, 0)), vec, mat, mat, ANY] + [ANY] * n_s1,
        out_specs=[pl.BlockSpec((tm, 2 * D_MODEL), lambda s: (nt - 1 - s, 2)), mat, mat,
                   pl.BlockSpec((8, D_MODEL), lambda s: (0, 0))] + [ANY] * n_s1,
        out_shape=[jax.ShapeDtypeStruct(dproj.shape, dproj.dtype), mshape, mshape,
                   jax.ShapeDtypeStruct((8, D_MODEL), F32)] + S1_SHAPES,
        input_output_aliases={n_in: 0},
        scratch_shapes=[pltpu.VMEM((tm + 8, D_MODEL), F32), pltpu.VMEM((tm, D_MODEL), F32),
                        pltpu.VMEM((tm, D_MODEL), F32), pltpu.VMEM((tm, D_MODEL), F32),
                        pltpu.VMEM((tm, D_MODEL), F32), pltpu.VMEM((tm + 8, D_MODEL), F32),
                        pltpu.VMEM((tm, D_MODEL), BF16), pltpu.VMEM((tm, D_MODEL), BF16),
                        pltpu.VMEM((3, 16, D_MODEL), F32), pltpu.VMEM((5, 16, D_MODEL), F32),
                        pltpu.VMEM((1, D_MODEL), F32), pltpu.VMEM((1, D_MODEL), F32)] + _push_sems(n_s1),
        compiler_params=pltpu.CompilerParams(dimension_semantics=("arbitrary",), vmem_limit_bytes=VMEM_LIMIT,
                                             has_side_effects=True),
    )(dzl, hs, hs, cri, proj, proj, conv_w, lam, wa_g, wx_g, dproj, *s1_grads)


def _retention_bwd(dzr, o, proj, states, cos2, sin2, dec, dproj, ride):
    rows = dzr.shape[0]
    n_chunks = rows // CHUNK
    per_step = _chunks_per_step(n_chunks)
    n_steps = n_chunks // per_step
    tm = per_step * CHUNK
    n_r = ride.n

    def body(dzr_ref, o_ref, q_ref, k_ref, v_ref, g_ref, st_ref, c_ref, s_ref, dec_ref, dproj_in, *refs):
        del dproj_in
        dseg_ref = refs[n_r]
        dstate = refs[2 * n_r + 1]
        push = ride.push(refs[:n_r], refs[n_r + 1:2 * n_r + 1], refs[2 * n_r + 2:])

        @pl.when(pl.program_id(0) == 0)
        def _():
            push.start()
            dstate[...] = jnp.zeros_like(dstate)

        for h in range(HEADS):
            sl = slice(HEAD_DIM * h, HEAD_DIM * (h + 1))
            intra, qd, kd, cd = dec_ref[0, h], dec_ref[1, h], dec_ref[2, h], dec_ref[3, h]
            dst = dstate[h]
            for c in reversed(range(per_step)):
                rw = slice(CHUNK * c, CHUNK * (c + 1))
                cos_t, sin_t = c_ref[rw, :], s_ref[rw, :]
                o = o_ref[rw, sl].astype(F32)
                g = g_ref[rw, sl].astype(F32)
                dzr_v = dzr_ref[rw, sl].astype(F32)
                sg = _sigmoid(g)
                r = lax.rsqrt(jnp.mean(o * o, axis=-1, keepdims=True) + NORM_EPS)
                on = o * r
                dseg_ref[rw, 3 * D_MODEL + HEAD_DIM * h:3 * D_MODEL + HEAD_DIM * (h + 1)] = (
                    dzr_v * on * (sg * (1.0 + g * (1.0 - sg)))).astype(BF16)
                don = dzr_v * (g * sg)
                do = r * (don - on * jnp.mean(don * on, axis=-1, keepdims=True))
                dob = do.astype(BF16)

                qh = _rot(q_ref[rw, sl].astype(F32), cos_t, sin_t)
                kh = _rot(k_ref[rw, sl].astype(F32), cos_t, sin_t) * QK_SCALE
                qb, kb, vb = qh.astype(BF16), kh.astype(BF16), v_ref[rw, sl]
                s = (_dot_nt(qb, kb) * intra).astype(BF16)
                ds = (_dot_nt(dob, vb) * intra).astype(BF16)
                st_b = st_ref[c, h].astype(BF16)
                dst_b = dst.astype(BF16)
                dv = _dot_tn(s, dob) + _dot((kh * kd).astype(BF16), dst_b)
                dq = _dot(ds, kb) + _dot_nt(dob, st_b) * qd
                dk = _dot_tn(ds, qb) + _dot_nt(vb, dst_b) * kd
                dst = dst * cd + _dot_tn((qh * qd).astype(BF16), dob)
                dseg_ref[rw, 2 * D_MODEL + HEAD_DIM * h:2 * D_MODEL + HEAD_DIM * (h + 1)] = dv.astype(BF16)
                dseg_ref[rw, sl] = _rot_t(dq, cos_t, sin_t).astype(BF16)
                dseg_ref[rw, D_MODEL + HEAD_DIM * h:D_MODEL + HEAD_DIM * (h + 1)] = (
                    _rot_t(dk, cos_t, sin_t) * QK_SCALE).astype(BF16)
            dstate[h] = dst

        @pl.when(pl.program_id(0) == n_steps - 1)
        def _():
            push.wait()

    rev = lambda s: n_steps - 1 - s
    rowb = pl.BlockSpec((tm, D_MODEL), lambda s: (rev(s), 0))
    seg = lambda k: pl.BlockSpec((tm, D_MODEL), lambda s, k=k: (rev(s), k))
    tab = pl.BlockSpec((tm, HEAD_DIM), lambda s: (rev(s), 0))
    return pl.pallas_call(
        body, name="retention_bwd", grid=(n_steps,),
        in_specs=[rowb, rowb, seg(0), seg(1), seg(2), seg(3),
                  pl.BlockSpec((per_step, HEADS, HEAD_DIM, HEAD_DIM), lambda s: (rev(s), 0, 0, 0)), tab, tab,
                  pl.BlockSpec((4, HEADS, CHUNK, CHUNK), lambda s: (0, 0, 0, 0)), ANY] + ride.specs(),
        out_specs=[pl.BlockSpec((tm, 4 * D_MODEL), lambda s: (rev(s), 0))] + ride.specs(),
        out_shape=[jax.ShapeDtypeStruct(dproj.shape, dproj.dtype)] + ride.out_shapes,
        input_output_aliases={10: 0},
        scratch_shapes=[pltpu.VMEM((HEADS, HEAD_DIM, HEAD_DIM), F32)] + ride.scratch(),
        compiler_params=pltpu.CompilerParams(dimension_semantics=("arbitrary",), vmem_limit_bytes=VMEM_LIMIT,
                                             has_side_effects=True),
    )(dzr, o, proj, proj, proj, proj, states, cos2, sin2, dec, dproj, *ride.arrays)


S2_SHAPES = [
    jax.ShapeDtypeStruct((N_DEV, LRU_BLOCKS, LRU_ROWS, LRU_BLOCK), F32),
    jax.ShapeDtypeStruct((N_DEV, LRU_BLOCKS, LRU_ROWS, LRU_BLOCK), F32),
]


def _s2_parts(ins, p):
    return [r.at[p] for r in ins]


def _in_proj_bwd(dproj, win_g, h0, norm_w, dh1, d_win_far, s2_grads, pack_early):
    rows = h0.shape[0]
    tm = _tile(rows, 320)
    n_i = rows // tm
    n_s2 = len(s2_grads)
    n_far = len(WIN_FAR)
    pack_rows = pack_early.shape[0]

    def body(dseg_ref, w_ref, h0_ref, nw_ref, dh1_ref, far_ref, early_ref, *refs):
        s2_refs = refs[:n_s2]
        dh0_ref, dw_ref, far_land = refs[n_s2:n_s2 + 3]
        land_refs = refs[n_s2 + 3:2 * n_s2 + 3]
        early_land, late_land = refs[2 * n_s2 + 3:2 * n_s2 + 5]
        (send_sems, recv_sems, loc_sems, far_send_sems, far_recv_sems, late_buf) = refs[2 * n_s2 + 5:2 * n_s2 + 11]
        early_sems, late_sems = refs[2 * n_s2 + 11:2 * n_s2 + 14], refs[2 * n_s2 + 14:]
        i = pl.program_id(0)
        push = _Push(lambda p: _s2_parts(s2_refs, p), lambda s: [r.at[s] for r in land_refs],
                     (send_sems, recv_sems, loc_sems), n_s2)
        early = _Push(lambda p: [early_ref], lambda s: [early_land.at[s]], tuple(early_sems), 1)
        late = _Push(lambda p: [late_buf], lambda s: [late_land.at[s]], tuple(late_sems), 1)

        def far_copy(n):
            return pltpu.make_async_remote_copy(src_ref=far_ref.at[n], dst_ref=far_land.at[n],
                                                send_sem=far_send_sems.at[n], recv_sem=far_recv_sems.at[n],
                                                device_id=_peer(WIN_FAR[n])[0], device_id_type=MESH_ID)

        @pl.when(i == 0)
        def _():
            for n in range(n_far):
                far_copy(n).start()
            push.start()
            early.start()
            dw_ref[...] = jnp.zeros_like(dw_ref)

        du = _dot_nt(dseg_ref[:, 0:D_MODEL], w_ref[0])
        for j in range(1, N_DEV):
            du = du + _dot_nt(dseg_ref[:, D_MODEL * j:D_MODEL * (j + 1)], w_ref[j])
        dx, dw = _rms_bwd(h0_ref[...], nw_ref[...], du)
        dw_ref[0:1, :] += dw
        dh0 = dh1_ref[...] + dx
        dh0_ref[...] = dh0

        @pl.when(i == 0)
        def _():
            late_buf[8:8 + N_META, :] = dh0[PAD_ROWS:CHUNK, :]

        @pl.when(i == n_i - 1)
        def _():
            late_buf[0:8, :] = dw_ref[...]
            late.start()
            for n in range(n_far):
                far_copy(n).wait_recv()
            for n in range(n_far):
                far_copy(n).wait_send()
            push.wait()
            early.wait()
            late.wait()

    row = pl.BlockSpec((tm, D_MODEL), lambda i: (i, 0))
    vec = pl.BlockSpec((1, D_MODEL), lambda i: (0, 0))
    return pl.pallas_call(
        body, name="in_proj_bwd", grid=(n_i,),
        in_specs=[pl.BlockSpec((tm, N_DEV * D_MODEL), lambda i: (i, 0)),
                  pl.BlockSpec((N_DEV, D_MODEL, D_MODEL), lambda i: (0, 0, 0), pipeline_mode=pl.Buffered(1)),
                  row, vec, row, ANY, ANY] + [ANY] * n_s2,
        out_specs=[row, pl.BlockSpec((8, D_MODEL), lambda i: (0, 0)), ANY] + [ANY] * n_s2 + [ANY, ANY],
        out_shape=[jax.ShapeDtypeStruct((rows, D_MODEL), F32), jax.ShapeDtypeStruct((8, D_MODEL), F32),
                   jax.ShapeDtypeStruct((n_far, D_MODEL, D_MODEL), BF16)] + S2_SHAPES
        + [jax.ShapeDtypeStruct((N_DEV, pack_rows, D_MODEL), F32)] * 2,
        scratch_shapes=_push_sems(n_s2) + [pltpu.SemaphoreType.DMA((n_far,)), pltpu.SemaphoreType.DMA((n_far,)),
                                           pltpu.VMEM((pack_rows, D_MODEL), F32)] + _push_sems(1) + _push_sems(1),
        compiler_params=pltpu.CompilerParams(dimension_semantics=("arbitrary",),
                                             vmem_limit_bytes=VMEM_LIMIT, has_side_effects=True),
    )(dproj, win_g, h0, norm_w, dh1, d_win_far, pack_early, *s2_grads)


def _adamw(g_slots, w, m, v, more_slots=None):
    slots, rows, cols = g_slots.shape
    extra = [] if more_slots is None else [more_slots]
    tr = rows
    for cand in (256, 128, 64, 32, 16, 8):
        if rows % cand == 0 and rows > cand:
            tr = cand
            break

    def body(g_ref, *refs):
        w_ref, m_ref, v_ref, go_ref, d_ref, mo_ref, vo_ref = refs[len(extra):]
        g = g_ref[0].astype(F32)
        for s in range(1, slots):
            g = g + g_ref[s].astype(F32)
        for more_ref in refs[:len(extra)]:
            for s in range(more_ref.shape[0]):
                g = g + more_ref[s].astype(F32)
        m2 = ADAM_B1 * m_ref[...] + (1.0 - ADAM_B1) * g
        v2 = ADAM_B2 * v_ref[...] + (1.0 - ADAM_B2) * (g * g)
        m_hat = m2 / (1.0 - ADAM_B1 ** ADAM_STEP)
        v_hat = v2 / (1.0 - ADAM_B2 ** ADAM_STEP)
        go_ref[...] = g
        d_ref[...] = -ADAM_LR * (m_hat / (jnp.sqrt(v_hat) + ADAM_EPS) + ADAM_WD * w_ref[...])
        mo_ref[...] = m2
        vo_ref[...] = v2

    blk = pl.BlockSpec((tr, cols), lambda i: (i, 0))
    shape = jax.ShapeDtypeStruct((rows, cols), F32)
    return pl.pallas_call(
        body, name="adamw", grid=(rows // tr,),
        in_specs=[pl.BlockSpec((slots, tr, cols), lambda i: (0, i, 0))]
        + [pl.BlockSpec((t.shape[0], tr, cols), lambda i: (0, i, 0)) for t in extra] + [blk, blk, blk],
        out_specs=[blk] * 4, out_shape=[shape] * 4,
        compiler_params=_cparams(("parallel",)),
    )(g_slots, *extra, w, m, v)


def _sum_slots(packs):
    slots, rows, cols = packs.shape

    def body(p_ref, o_ref):
        acc = p_ref[0]
        for s in range(1, slots):
            acc = acc + p_ref[s]
        o_ref[...] = acc

    return pl.pallas_call(
        body, name="sum_slots", out_shape=jax.ShapeDtypeStruct((rows, cols), F32),
        compiler_params=pltpu.CompilerParams(vmem_limit_bytes=VMEM_LIMIT),
    )(packs)


def _gather_small(small):
    shapes = [jax.ShapeDtypeStruct((N_DEV,) + small.shape, F32)]
    return _push_call("gather_small", [small], shapes,
                      lambda ins, p: list(ins), lambda outs, s: [r.at[s] for r in outs])[0]


PACK_CONV_W, PACK_CONV_B, PACK_BA, PACK_BX, PACK_LAM = 0, 4, 5, 6, 7
PACK_FFN_NORM, PACK_SQ_ERR, PACK_FINAL_NORM, PACK_MIX_NORM, PACK_META = 8, 16, 17, 24, 32


def kernel(x, meta_tokens, mix_norm_w, w_in, conv_w, conv_b, lru_wa, lru_ba, lru_wx, lru_bx, lru_lambda, w_branch_ret, w_branch_lru, w_out, ffn_norm_w, w_ffn_in, w_ffn_out, final_norm_w, loss_target, m_meta_tokens, m_mix_norm_w, m_w_in, m_conv_w, m_conv_b, m_lru_wa, m_lru_ba, m_lru_wx, m_lru_bx, m_lru_lambda, m_w_branch_ret, m_w_branch_lru, m_w_out, m_ffn_norm_w, m_w_ffn_in, m_w_ffn_out, m_final_norm_w, v_meta_tokens, v_mix_norm_w, v_w_in, v_conv_w, v_conv_b, v_lru_wa, v_lru_ba, v_lru_wx, v_lru_bx, v_lru_lambda, v_w_branch_ret, v_w_branch_lru, v_w_out, v_ffn_norm_w, v_w_ffn_in, v_w_ffn_out, v_final_norm_w):
    me = _my_index()
    pad4 = ((0, 4), (0, 0))
    fw = final_norm_w.reshape(1, D_MODEL)

    small = jnp.concatenate([meta_tokens, jnp.pad(conv_w[0], pad4)], axis=0)
    small_g = _gather_small(small)
    meta_full = small_g[:, :N_META].transpose(1, 0, 2).reshape(N_META, D_MODEL)
    conv_w_full = small_g[:, N_META:N_META + 4].transpose(1, 0, 2).reshape(4, D_MODEL)
    mixer_shards = [w_branch_ret[0].astype(BF16), w_branch_lru[0].astype(BF16), w_out[0].astype(BF16),
                    lru_wa[0].astype(BF16), lru_wx[0].astype(BF16)]
    wfi_shard = jnp.pad(w_ffn_in[0].astype(BF16), ((0, 0), (0, FFN_GROUP - FFN_SHARD)))
    own_slot = lambda ins, p: list(ins)

    rows = x.shape[1] + CHUNK
    h0 = jnp.concatenate([jnp.zeros((PAD_ROWS, D_MODEL), F32), meta_full, x[0]], axis=0)
    cos2, sin2 = _rope_tables(rows)
    dec = _retention_consts()

    me_arr = me.astype(jnp.int32).reshape(1)
    proj, u, win_g = _in_proj(h0, mix_norm_w, w_in[0].astype(BF16), me_arr)
    o, zr, states, wbr_g, wbl_g, wout_g, wa_g, wx_g = _retention_fwd(
        proj, cos2, sin2, dec, _mixer_weights_ride(mixer_shards))
    wbr, wbl, wout = (t.reshape(D_MODEL, D_MODEL) for t in (wbr_g, wbl_g, wout_g))
    wa_g, wx_g = _from_owners(wa_g), _from_owners(wx_g)
    gather_wfi = _Ride([wfi_shard], [jax.ShapeDtypeStruct((N_DEV, D_MODEL, FFN_GROUP), BF16)],
                       own_slot, _slot_of_sender, gather_by_chip=True)
    hs, zl, cri, wfi_g = _lru_fwd(proj, conv_w_full, conv_b, lru_ba, lru_bx, lru_lambda, wa_g, wx_g, gather_wfi)
    h1, yr, yl, mixed, wfo_g = _mix_fwd(zr, zl, proj, h0, wbr, wbl, wout, _wfo_ride(w_ffn_out[0].astype(BF16)))
    u2, g, up, act, dh2, red = _ffn_fwd_loss(h1, ffn_norm_w, wfi_g, wfo_g, fw, loss_target[0])

    d_wfo = _wgrad(act, dh2, FFN_GROUP, D_MODEL, BF16)[:, 0]
    dgu, dh1, dw_ffn_norm = _ffn_bwd(dh2, g, up, h1, ffn_norm_w, wfi_g, wfo_g)
    d_wfi = _wgrad(u2, dgu, D_MODEL, FFN_GROUP, BF16, b_halves=True)[0]
    d_wout = _wgrad(mixed, dh1, D_MODEL, D_MODEL, BF16)[0, 0]
    dyr, dyl, dproj, dzr, dzl = _mix_bwd(dh1, yr, yl, proj, wbr, wbl, wout)
    d_wbr = _wgrad(zr, dyr, D_MODEL, D_MODEL, BF16)[0, 0]
    d_wbl = _wgrad(zl, dyl, D_MODEL, D_MODEL, BF16)[0, 0]
    dproj, d_wa, d_wx, lru_small, r_fi = _lru_bwd(
        dzl, hs, cri, proj, dproj, conv_w_full, lru_lambda, wa_g, wx_g, [d_wfi])
    mix_shape = jax.ShapeDtypeStruct((N_DEV, D_MODEL // N_DEV, D_MODEL), BF16)
    wfo_shape = jax.ShapeDtypeStruct((N_DEV, FFN_OUT_SHARD, D_MODEL), BF16)
    part_of_owner = lambda ins, p: [r.at[p] for r in ins[:3]] + [ins[3].at[p // 2, _half_rows(p), :]]
    scatter_mix = _Ride([t.reshape(mix_shape.shape) for t in (d_wbr, d_wbl, d_wout)] + [d_wfo],
                        [mix_shape] * 3 + [wfo_shape], part_of_owner, _slot_of_sender)
    dproj, r_br, r_bl, r_out, r_fo = _retention_bwd(dzr, o, proj, states, cos2, sin2, dec, dproj, scatter_mix)
    d_win_far, r_in = _wgrad_w_in(u, dproj, me_arr)
    pack_early = jnp.concatenate([lru_small, dw_ffn_norm, red], axis=0)
    dh0, _, r_in_far, r_wa, r_wx, packs_early, packs_late = _in_proj_bwd(
        dproj, win_g, h0, mix_norm_w, dh1, d_win_far, [_by_owner(d_wa), _by_owner(d_wx)], pack_early)
    grad_x = dh0[CHUNK:]

    small_sum = jnp.concatenate([_sum_slots(packs_early), _sum_slots(packs_late)], axis=0)
    loss = (0.5 / D_MODEL) * jnp.sum(small_sum[PACK_SQ_ERR])

    def big_update(slots, w, m, v, more_slots=None):
        shape = w.shape
        w2, m2, v2 = (t.reshape(slots.shape[1:]) for t in (w, m, v))
        return [t.reshape(shape) for t in _adamw(slots, w2, m2, v2, more_slots)]

    res = {}
    res["w_in"] = big_update(r_in, w_in, m_w_in, v_w_in, r_in_far)
    res["w_branch_ret"] = big_update(r_br, w_branch_ret, m_w_branch_ret, v_w_branch_ret)
    res["w_branch_lru"] = big_update(r_bl, w_branch_lru, m_w_branch_lru, v_w_branch_lru)
    res["w_out"] = big_update(r_out, w_out, m_w_out, v_w_out)
    res["w_ffn_in"] = big_update(r_fi[:, :, :FFN_SHARD], w_ffn_in, m_w_ffn_in, v_w_ffn_in)
    res["w_ffn_out"] = big_update(r_fo, w_ffn_out, m_w_ffn_out, v_w_ffn_out)
    res["lru_wa"] = big_update(r_wa.reshape(N_DEV, LRU_BLOCKS * LRU_ROWS, LRU_BLOCK), lru_wa, m_lru_wa, v_lru_wa)
    res["lru_wx"] = big_update(r_wx.reshape(N_DEV, LRU_BLOCKS * LRU_ROWS, LRU_BLOCK), lru_wx, m_lru_wx, v_lru_wx)

    col = me * HEAD_DIM
    g_meta = lax.dynamic_slice(small_sum, (PACK_META, col), (N_META, HEAD_DIM))
    g_conv = lax.dynamic_slice(small_sum, (PACK_CONV_W, col), (8, HEAD_DIM))
    small_names = ["mix_norm_w", "conv_b", "lru_ba", "lru_bx", "lru_lambda", "ffn_norm_w", "final_norm_w"]
    small_rows = [PACK_MIX_NORM, PACK_CONV_B, PACK_BA, PACK_BX, PACK_LAM, PACK_FFN_NORM, PACK_FINAL_NORM]
    small_w = [mix_norm_w, conv_b, lru_ba, lru_bx, lru_lambda, ffn_norm_w, fw]
    small_m = [m_mix_norm_w, m_conv_b, m_lru_ba, m_lru_bx, m_lru_lambda, m_ffn_norm_w, m_final_norm_w.reshape(1, -1)]
    small_v = [v_mix_norm_w, v_conv_b, v_lru_ba, v_lru_bx, v_lru_lambda, v_ffn_norm_w, v_final_norm_w.reshape(1, -1)]

    def pack_small(vec_list, meta_t, conv_t):
        return jnp.concatenate([t.reshape(8, HEAD_DIM) for t in vec_list] + [meta_t, jnp.pad(conv_t[0], pad4)], axis=0)

    g_small = jnp.concatenate([small_sum[r].reshape(8, HEAD_DIM) for r in small_rows] + [g_meta, g_conv], axis=0)
    outs_small = _adamw(g_small[None], pack_small(small_w, meta_tokens, conv_w),
                        pack_small(small_m, m_meta_tokens, m_conv_w), pack_small(small_v, v_meta_tokens, v_conv_w))
    for idx, name in enumerate(small_names):
        shape = final_norm_w.shape if name == "final_norm_w" else (1, D_MODEL)
        res[name] = [t[8 * idx:8 * idx + 8].reshape(shape) for t in outs_small]
    res["meta_tokens"] = [t[56:72] for t in outs_small]
    res["conv_w"] = [t[72:76].reshape(1, 4, HEAD_DIM) for t in outs_small]

    order = ["meta_tokens", "mix_norm_w", "w_in", "conv_w", "conv_b", "lru_wa", "lru_ba", "lru_wx", "lru_bx",
             "lru_lambda", "w_branch_ret", "w_branch_lru", "w_out", "ffn_norm_w", "w_ffn_in", "w_ffn_out",
             "final_norm_w"]
    out = [loss, grad_x[None]]
    for kind in range(4):
        out += [res[name][kind] for name in order]
    return tuple(out)
```

```python
import jax
import jax.numpy as jnp
from jax import lax
from jax.experimental import pallas as pl
from jax.experimental.pallas import tpu as pltpu

F32 = jnp.float32
BF16 = jnp.bfloat16

D_MODEL = 1024
N_META = 16
CHUNK = 128
PAD_ROWS = CHUNK - N_META
HEADS = 8
HEAD_DIM = 128
ROPE_BASE = 10000.0
QK_SCALE = HEAD_DIM ** -0.5
LRU_BLOCKS = 4
LRU_BLOCK = 256
LRU_C = 8.0
FFN_HIDDEN = 2816
N_DEV = 8
FFN_SHARD = 2 * FFN_HIDDEN // N_DEV
FFN_GROUP = 768
FFN_GROUPS = 4
FFN_OUT_SHARD = FFN_HIDDEN // N_DEV
NORM_EPS = 1e-6

ADAM_LR = 0.001
ADAM_B1 = 0.9
ADAM_B2 = 0.999
ADAM_EPS = 1e-08
ADAM_WD = 0.01
ADAM_STEP = 10

VMEM_LIMIT = 56 * 1024 * 1024
MESH_ID = pl.DeviceIdType.MESH
ANY = pl.BlockSpec(memory_space=pl.ANY)


def _cparams(sem):
    return pltpu.CompilerParams(dimension_semantics=sem, vmem_limit_bytes=VMEM_LIMIT)


def _tile(rows, cap):
    t = cap - cap % 64
    while rows % t:
        t -= 64
    return t


def _dot(a, b):
    return jnp.dot(a, b, preferred_element_type=F32)


def _dot_nt(a, b):
    return lax.dot_general(a, b, (((1,), (1,)), ((), ())), preferred_element_type=F32)


def _dot_tn(a, b):
    return lax.dot_general(a, b, (((0,), (0,)), ((), ())), preferred_element_type=F32)


def _sigmoid(x):
    return 0.5 * jnp.tanh(0.5 * x) + 0.5


def _gelu_parts(x):
    k = 0.7978845608028654
    inner = k * (x + 0.044715 * x * x * x)
    t = jnp.tanh(inner)
    g = 0.5 * x * (1.0 + t)
    dg = 0.5 * (1.0 + t) + 0.5 * x * (1.0 - t * t) * k * (1.0 + 3.0 * 0.044715 * x * x)
    return g, dg


def _rot(x, cos2, sin2):
    return x * cos2 + pltpu.roll(x, HEAD_DIM // 2, 1) * sin2


def _rot_t(dx, cos2, sin2):
    return dx * cos2 - pltpu.roll(dx, HEAD_DIM // 2, 1) * sin2


def _rms_bwd(x, w, dy):
    rs = lax.rsqrt(jnp.mean(x * x, axis=-1, keepdims=True) + NORM_EPS)
    nh = x * rs
    dw = jnp.sum(dy * nh, axis=0, keepdims=True)
    dn = dy * w
    dx = rs * (dn - nh * jnp.mean(dn * nh, axis=-1, keepdims=True))
    return dx, dw


def _retention_consts():
    h = jnp.arange(HEADS, dtype=F32)
    log_g = jnp.log(1.0 - 2.0 ** (-5.0 - h))
    idx = jnp.arange(CHUNK, dtype=F32)
    diff = idx[:, None] - idx[None, :]
    intra = jnp.where(diff[None] >= 0, jnp.exp(jnp.maximum(diff, 0.0)[None] * log_g[:, None, None]), 0.0)
    q_decay = jnp.exp((idx + 1.0)[:, None] * log_g[None, :])
    k_decay = jnp.exp((CHUNK - 1.0 - idx)[:, None] * log_g[None, :])
    chunk_decay = jnp.exp(CHUNK * log_g)
    shape = (HEADS, CHUNK, CHUNK)
    qd = jnp.broadcast_to(q_decay.T[:, :, None], shape)
    kd = jnp.broadcast_to(k_decay.T[:, :, None], shape)
    cd = jnp.broadcast_to(chunk_decay[:, None, None], shape)
    return jnp.stack([intra, qd, kd, cd])


def _rope_tables(rows):
    pos = jnp.maximum(jnp.arange(rows) - PAD_ROWS, 0).astype(F32)
    inv_freq = ROPE_BASE ** (-jnp.arange(0, HEAD_DIM, 2, dtype=F32) / HEAD_DIM)
    ang = pos[:, None] * inv_freq[None, :]
    cos, sin = jnp.cos(ang), jnp.sin(ang)
    return jnp.concatenate([cos, cos], axis=1), jnp.concatenate([-sin, sin], axis=1)


def _my_index():
    return 4 * lax.axis_index("x") + 2 * lax.axis_index("y") + lax.axis_index("c")


def _peer(k):
    x, y, c = lax.axis_index("x"), lax.axis_index("y"), lax.axis_index("c")
    px = 1 - x if k & 4 else x
    py = 1 - y if k & 2 else y
    pc = 1 - c if k & 1 else c
    return (px, py, pc), 4 * px + 2 * py + pc


def _push_sems(n_arr):
    n_rem = (N_DEV - 1) * n_arr
    return [pltpu.SemaphoreType.DMA((n_rem,)), pltpu.SemaphoreType.DMA((n_rem,)), pltpu.SemaphoreType.DMA((n_arr,))]


class _Push:
    def __init__(self, send_part, land_slot, sems, n_arr):
        self.send_part, self.land_slot, self.n_arr = send_part, land_slot, n_arr
        self.send_sems, self.recv_sems, self.loc_sems = sems

    def _remote(self, k, a, src, dst, pos):
        idx = (k - 1) * self.n_arr + a
        return pltpu.make_async_remote_copy(src_ref=src, dst_ref=dst, send_sem=self.send_sems.at[idx],
                                            recv_sem=self.recv_sems.at[idx], device_id=pos, device_id_type=MESH_ID)

    def _outgoing(self):
        me = _my_index()
        land = self.land_slot(me)
        remote = []
        for k in range(1, N_DEV):
            pos, p = _peer(k)
            src = self.send_part(p)
            remote += [self._remote(k, a, src[a], land[a], pos) for a in range(self.n_arr)]
        own = self.send_part(me)
        local = [pltpu.make_async_copy(own[a], land[a], self.loc_sems.at[a]) for a in range(self.n_arr)]
        return remote, local

    def start(self):
        remote, local = self._outgoing()
        for cp in remote + local:
            cp.start()

    def wait_recv_from(self, k):
        own = self.send_part(_my_index())
        pos, p = _peer(k)
        land = self.land_slot(p)
        for a in range(self.n_arr):
            self._remote(k, a, own[a], land[a], pos).wait_recv()

    def wait_sends(self):
        remote, local = self._outgoing()
        for cp in remote:
            cp.wait_send()
        for cp in local:
            cp.wait()

    def wait(self):
        for k in range(1, N_DEV):
            self.wait_recv_from(k)
        self.wait_sends()


DIRECT = (1, 2, 4, 6)
RELAYED = (2, 4, 6)


def _gather_by_chip_sems(n_arr):
    direct, relayed = len(DIRECT) * n_arr, len(RELAYED) * n_arr
    return [pltpu.SemaphoreType.DMA((direct,)), pltpu.SemaphoreType.DMA((direct,)),
            pltpu.SemaphoreType.DMA((relayed,)), pltpu.SemaphoreType.DMA((relayed,)), pltpu.SemaphoreType.DMA((n_arr,))]


class _GatherByChip:
    def __init__(self, srcs, land_slot, sems, n_arr):
        self.srcs, self.land_slot, self.n_arr = srcs, land_slot, n_arr
        self.send_sems, self.recv_sems, self.relay_send_sems, self.relay_recv_sems, self.loc_sems = sems

    def _direct(self, k, a, slot):
        idx = DIRECT.index(k) * self.n_arr + a
        return pltpu.make_async_remote_copy(src_ref=self.srcs[a], dst_ref=self.land_slot(slot)[a],
                                            send_sem=self.send_sems.at[idx], recv_sem=self.recv_sems.at[idx],
                                            device_id=_peer(k)[0], device_id_type=MESH_ID)

    def _relay(self, q, a, slot):
        idx = RELAYED.index(q) * self.n_arr + a
        block = self.land_slot(slot)[a]
        return pltpu.make_async_remote_copy(src_ref=block, dst_ref=block, send_sem=self.relay_send_sems.at[idx],
                                            recv_sem=self.relay_recv_sems.at[idx], device_id=_peer(1)[0],
                                            device_id_type=MESH_ID)

    def _own(self, a):
        return pltpu.make_async_copy(self.srcs[a], self.land_slot(_my_index())[a], self.loc_sems.at[a])

    def start(self):
        me = _my_index()
        for k in DIRECT:
            for a in range(self.n_arr):
                self._direct(k, a, me).start()
        for a in range(self.n_arr):
            self._own(a).start()

    def relay(self):
        for q in RELAYED:
            p = _peer(q)[1]
            for a in range(self.n_arr):
                self._direct(q, a, p).wait_recv()
                self._relay(q, a, p).start()

    def wait(self):
        me = _my_index()
        for a in range(self.n_arr):
            self._direct(1, a, _peer(1)[1]).wait_recv()
        for q in RELAYED:
            for a in range(self.n_arr):
                self._relay(q, a, _peer(q + 1)[1]).wait_recv()
        for k in DIRECT:
            for a in range(self.n_arr):
                self._direct(k, a, me).wait_send()
        for q in RELAYED:
            for a in range(self.n_arr):
                self._relay(q, a, _peer(q)[1]).wait_send()
        for a in range(self.n_arr):
            self._own(a).wait()


class _Ride:
    def __init__(self, arrays, out_shapes, send_part, land_slot, zero_dsts=None, zero_shape=None, n_zero=0,
                 gather_by_chip=False):
        self.arrays, self.out_shapes = list(arrays), list(out_shapes)
        self.send_part, self.land_slot, self.n = send_part, land_slot, len(arrays)
        self.zero_dsts, self.zero_shape, self.n_zero = zero_dsts, zero_shape, n_zero
        self.gather_by_chip = gather_by_chip

    def specs(self):
        return [ANY] * self.n

    def scratch(self):
        extra = [pltpu.SemaphoreType.DMA((self.n_zero,)), pltpu.VMEM(self.zero_shape, BF16)] if self.n_zero else []
        sems = _gather_by_chip_sems(self.n) if self.gather_by_chip else _push_sems(self.n)
        return sems + extra

    def push(self, in_refs, out_refs, scratch):
        ride = self
        n_sems = 5 if self.gather_by_chip else 3
        land = lambda s: ride.land_slot(out_refs, s)
        if self.gather_by_chip:
            push = _GatherByChip(list(in_refs), land, tuple(scratch[:n_sems]), self.n)
        else:
            push = _Push(lambda p: ride.send_part(in_refs, p), land, tuple(scratch[:n_sems]), self.n)

        class Both:
            def _fills(self):
                if not ride.n_zero:
                    return []
                zsems, zbuf = scratch[n_sems], scratch[n_sems + 1]
                return [pltpu.make_async_copy(zbuf, dst, zsems.at[z]) for z, dst in enumerate(ride.zero_dsts(out_refs))]

            def start(self):
                push.start()
                if ride.n_zero:
                    scratch[n_sems + 1][...] = jnp.zeros(ride.zero_shape, BF16)
                for cp in self._fills():
                    cp.start()

            def relay(self):
                if ride.gather_by_chip:
                    push.relay()

            def wait(self):
                push.wait()
                for cp in self._fills():
                    cp.wait()

        return Both()


def _slot_of_sender(out_refs, s):
    return [r.at[s] for r in out_refs]


def _push_call(name, arrays, out_shapes, send_part, land_slot):
    n_arr = len(arrays)

    def body(*refs):
        ins, outs, sems = refs[:n_arr], refs[n_arr:2 * n_arr], refs[2 * n_arr:]
        push = _Push(lambda p: send_part(ins, p), lambda s: land_slot(outs, s), sems, n_arr)
        push.start()
        push.wait()

    return pl.pallas_call(
        body, name=name, in_specs=[ANY] * n_arr, out_specs=[ANY] * n_arr, out_shape=out_shapes,
        scratch_shapes=_push_sems(n_arr), compiler_params=pltpu.CompilerParams(has_side_effects=True),
    )(*arrays)


LRU_ROWS = LRU_BLOCK // N_DEV
FFN_PAD_ROWS = FFN_GROUP - 2 * FFN_OUT_SHARD


def _half_rows(d):
    return pl.ds(pl.multiple_of((d % 2) * FFN_OUT_SHARD, 16), FFN_OUT_SHARD)


MIXER_SHAPES = [
    jax.ShapeDtypeStruct((N_DEV, D_MODEL // N_DEV, D_MODEL), BF16),
    jax.ShapeDtypeStruct((N_DEV, D_MODEL // N_DEV, D_MODEL), BF16),
    jax.ShapeDtypeStruct((N_DEV, D_MODEL // N_DEV, D_MODEL), BF16),
    jax.ShapeDtypeStruct((N_DEV, LRU_BLOCKS, LRU_ROWS, LRU_BLOCK), BF16),
    jax.ShapeDtypeStruct((N_DEV, LRU_BLOCKS, LRU_ROWS, LRU_BLOCK), BF16),
]


def _by_owner(t):
    return t.reshape(LRU_BLOCKS, N_DEV, LRU_ROWS, LRU_BLOCK).transpose(1, 0, 2, 3)


def _from_owners(t):
    return t.transpose(1, 0, 2, 3).reshape(LRU_BLOCKS, LRU_BLOCK, LRU_BLOCK)


def _mixer_weights_ride(shards):
    return _Ride(shards, MIXER_SHAPES, lambda ins, p: list(ins), _slot_of_sender, gather_by_chip=True)


def _wfo_ride(shard):
    zero_dsts = lambda outs: [outs[0].at[g, pl.ds(2 * FFN_OUT_SHARD, FFN_PAD_ROWS), :] for g in range(FFN_GROUPS)]
    return _Ride([shard], [jax.ShapeDtypeStruct((FFN_GROUPS, FFN_GROUP, D_MODEL), BF16)], lambda ins, p: list(ins),
                 lambda outs, d: [outs[0].at[d // 2, _half_rows(d), :]], zero_dsts, (FFN_PAD_ROWS, D_MODEL), FFN_GROUPS,
                 gather_by_chip=True)


W_IN_USE_ORDER = (0, 1, 2, 4, 6, 3, 5, 7)


def _arrival_rank_to_relation(jj):
    k = W_IN_USE_ORDER[-1]
    for pos in reversed(range(N_DEV - 1)):
        k = jnp.where(jj == pos, W_IN_USE_ORDER[pos], k)
    return k


def _in_proj(h0, norm_w, win_shard, me_arr):
    rows = h0.shape[0]
    tm = _tile(rows, 1664)
    n_i = rows // tm

    direct, relayed = DIRECT, RELAYED

    def body(me_ref, h_ref, nw_ref, wsh_ref, proj_ref, u_ref, wing_ref, u_all, wbuf, copy_sem,
             send_sems, recv_sems, relay_send_sems, relay_recv_sems, own_sem):
        del me_ref
        jj, i = pl.program_id(0), pl.program_id(1)
        me = _my_index()
        sibling = _peer(1)[0]

        def direct_copy(k, slot):
            n = direct.index(k)
            return pltpu.make_async_remote_copy(src_ref=wsh_ref, dst_ref=wing_ref.at[slot], send_sem=send_sems.at[n],
                                                recv_sem=recv_sems.at[n], device_id=_peer(k)[0], device_id_type=MESH_ID)

        def relay_copy(q, slot):
            n = relayed.index(q)
            return pltpu.make_async_remote_copy(src_ref=wing_ref.at[slot], dst_ref=wing_ref.at[slot],
                                                send_sem=relay_send_sems.at[n], recv_sem=relay_recv_sems.at[n],
                                                device_id=sibling, device_id_type=MESH_ID)

        own_slot = pltpu.make_async_copy(wsh_ref, wing_ref.at[me], own_sem)

        @pl.when(jnp.logical_and(jj == 0, i == 0))
        def _():
            for k in direct:
                direct_copy(k, me).start()
            own_slot.start()
            own = pltpu.make_async_copy(wsh_ref, wbuf, copy_sem)
            own.start()
            own.wait()

        for k in range(1, N_DEV):
            rank = W_IN_USE_ORDER.index(k)

            @pl.when(jnp.logical_and(jj == rank, i == 0))
            def _(k=k):
                p = _peer(k)[1]
                if k in direct:
                    direct_copy(k, p).wait_recv()
                    if k in relayed:
                        relay_copy(k, p).start()
                else:
                    relay_copy(k - 1, p).wait_recv()
                landed = pltpu.make_async_copy(wing_ref.at[p], wbuf, copy_sem)
                landed.start()
                landed.wait()

        rows_i = pl.ds(pl.multiple_of(i * tm, tm), tm)

        @pl.when(jj == 0)
        def _():
            x = h_ref[...]
            rs = lax.rsqrt(jnp.mean(x * x, axis=-1, keepdims=True) + NORM_EPS)
            u = (x * rs * nw_ref[...]).astype(BF16)
            u_all[rows_i, :] = u
            u_ref[...] = u
        proj_ref[...] = _dot(u_all[rows_i, :], wbuf[...]).astype(BF16)

        @pl.when(jnp.logical_and(jj == N_DEV - 1, i == n_i - 1))
        def _():
            for k in direct:
                direct_copy(k, me).wait_send()
            for q in relayed:
                relay_copy(q, _peer(q)[1]).wait_send()
            own_slot.wait()

    first_pass = lambda jj, i: jnp.where(jj == 0, i, n_i - 1)
    grid_spec = pltpu.PrefetchScalarGridSpec(
        num_scalar_prefetch=1, grid=(N_DEV, n_i),
        in_specs=[pl.BlockSpec((tm, D_MODEL), lambda jj, i, me: (first_pass(jj, i), 0)),
                  pl.BlockSpec((1, D_MODEL), lambda jj, i, me: (0, 0)), ANY],
        out_specs=[pl.BlockSpec((tm, D_MODEL), lambda jj, i, me: (i, me[0] ^ _arrival_rank_to_relation(jj))),
                   pl.BlockSpec((tm, D_MODEL), lambda jj, i, me: (first_pass(jj, i), 0)), ANY],
        scratch_shapes=[pltpu.VMEM((rows, D_MODEL), BF16), pltpu.VMEM((D_MODEL, D_MODEL), BF16),
                        pltpu.SemaphoreType.DMA(()),
                        pltpu.SemaphoreType.DMA((len(direct),)), pltpu.SemaphoreType.DMA((len(direct),)),
                        pltpu.SemaphoreType.DMA((len(relayed),)), pltpu.SemaphoreType.DMA((len(relayed),)),
                        pltpu.SemaphoreType.DMA(())])
    return pl.pallas_call(
        body, name="in_proj", grid_spec=grid_spec,
        out_shape=[jax.ShapeDtypeStruct((rows, N_DEV * D_MODEL), BF16),
                   jax.ShapeDtypeStruct((rows, D_MODEL), BF16),
                   jax.ShapeDtypeStruct((N_DEV, D_MODEL, D_MODEL), BF16)],
        compiler_params=pltpu.CompilerParams(dimension_semantics=("arbitrary", "arbitrary"),
                                             vmem_limit_bytes=VMEM_LIMIT, has_side_effects=True),
    )(me_arr, h0, norm_w, win_shard)


def _seg_spec(rows_per_block, seg):
    return pl.BlockSpec((rows_per_block, D_MODEL), lambda n, seg=seg: (n, seg))


def _chunks_per_step(n_chunks):
    return next(c for c in (5, 3, 2, 1) if n_chunks % c == 0)


def _retention_fwd(proj, cos2, sin2, dec, ride):
    rows = proj.shape[0]
    n_chunks = rows // CHUNK
    per_step = _chunks_per_step(n_chunks)
    n_steps = n_chunks // per_step
    tm = per_step * CHUNK
    n_r = ride.n

    def body(q_ref, k_ref, v_ref, g_ref, c_ref, s_ref, dec_ref, *refs):
        o_ref, zr_ref, st_ref = refs[n_r:n_r + 3]
        state = refs[2 * n_r + 3]
        push = ride.push(refs[:n_r], refs[n_r + 3:2 * n_r + 3], refs[2 * n_r + 4:])

        @pl.when(pl.program_id(0) == 0)
        def _():
            push.start()
            state[...] = jnp.zeros_like(state)

        for h in range(HEADS):
            sl = slice(HEAD_DIM * h, HEAD_DIM * (h + 1))
            st = state[h]
            for c in range(per_step):
                rw = slice(CHUNK * c, CHUNK * (c + 1))
                cos_t, sin_t = c_ref[rw, :], s_ref[rw, :]
                qh = _rot(q_ref[rw, sl].astype(F32), cos_t, sin_t)
                kh = _rot(k_ref[rw, sl].astype(F32), cos_t, sin_t) * QK_SCALE
                qb, kb, vb = qh.astype(BF16), kh.astype(BF16), v_ref[rw, sl]
                s = _dot_nt(qb, kb) * dec_ref[0, h]
                st_ref[c, h] = st
                o = _dot(s.astype(BF16), vb) + _dot(qb, st.astype(BF16)) * dec_ref[1, h]
                st = st * dec_ref[3, h] + _dot_tn((kh * dec_ref[2, h]).astype(BF16), vb)
                o_ref[rw, sl] = o.astype(BF16)
                r = lax.rsqrt(jnp.mean(o * o, axis=-1, keepdims=True) + NORM_EPS)
                g = g_ref[rw, sl].astype(F32)
                zr_ref[rw, sl] = (g * _sigmoid(g) * (o * r)).astype(BF16)
            state[h] = st

        @pl.when(pl.program_id(0) == n_steps // 2)
        def _():
            push.relay()

        @pl.when(pl.program_id(0) == n_steps - 1)
        def _():
            push.wait()

    tab = pl.BlockSpec((tm, HEAD_DIM), lambda n: (n, 0))
    return pl.pallas_call(
        body, name="retention_fwd", grid=(n_steps,),
        in_specs=[_seg_spec(tm, 0), _seg_spec(tm, 1), _seg_spec(tm, 2), _seg_spec(tm, 3), tab, tab,
                  pl.BlockSpec((4, HEADS, CHUNK, CHUNK), lambda n: (0, 0, 0, 0))] + ride.specs(),
        out_specs=[pl.BlockSpec((tm, D_MODEL), lambda n: (n, 0)),
                   pl.BlockSpec((tm, D_MODEL), lambda n: (n, 0)),
                   pl.BlockSpec((per_step, HEADS, HEAD_DIM, HEAD_DIM), lambda n: (n, 0, 0, 0))] + ride.specs(),
        out_shape=[jax.ShapeDtypeStruct((rows, D_MODEL), BF16),
                   jax.ShapeDtypeStruct((rows, D_MODEL), BF16),
                   jax.ShapeDtypeStruct((n_chunks, HEADS, HEAD_DIM, HEAD_DIM), F32)] + ride.out_shapes,
        scratch_shapes=[pltpu.VMEM((HEADS, HEAD_DIM, HEAD_DIM), F32)] + ride.scratch(),
        compiler_params=pltpu.CompilerParams(dimension_semantics=("arbitrary",), vmem_limit_bytes=VMEM_LIMIT,
                                             has_side_effects=True),
    )(proj, proj, proj, proj, cos2, sin2, dec, *ride.arrays)


def _lru_gates(c, ba, bx, wa_ref, wx_ref):
    pre_r, pre_i = [], []
    for g in range(LRU_BLOCKS):
        cg = c[:, LRU_BLOCK * g:LRU_BLOCK * (g + 1)].astype(BF16)
        pre_r.append(_dot(cg, wa_ref[g]))
        pre_i.append(_dot(cg, wx_ref[g]))
    return _sigmoid(jnp.concatenate(pre_r, axis=1) + ba), _sigmoid(jnp.concatenate(pre_i, axis=1) + bx)


def _lru_decay(r, lam):
    sp = jnp.maximum(-lam, 0.0) + jnp.log(1.0 + jnp.exp(-jnp.abs(lam)))
    log_a = -LRU_C * r * sp
    a = jnp.exp(log_a)
    one_minus_a2 = -jnp.tanh(log_a) * (a * a + 1.0)
    inv_mult = lax.rsqrt(jnp.maximum(one_minus_a2, 1e-30))
    return a, one_minus_a2 * inv_mult, inv_mult, sp


def _conv_taps(xbuf, tm, cw_ref, cb_ref):
    c = cb_ref[...] + cw_ref[3:4, :] * xbuf[8:8 + tm, :]
    for back in (1, 2, 3):
        c = c + cw_ref[3 - back:4 - back, :] * xbuf[8 - back:8 - back + tm, :]
    return c


def _lru_fwd(proj, conv_w, conv_b, ba, bx, lam, wa_g, wx_g, ride):
    rows = proj.shape[0]
    tm = _tile(rows, 320)
    n_t = rows // tm
    n_r = ride.n

    def body(x_ref, gt_ref, cw_ref, cb_ref, ba_ref, bx_ref, lam_ref, wa_ref, wx_ref, *refs):
        hs_ref, zl_ref, cri_ref = refs[n_r:n_r + 3]
        xbuf, abuf, ubuf, hcar = refs[2 * n_r + 3:2 * n_r + 7]
        push = ride.push(refs[:n_r], refs[n_r + 3:2 * n_r + 3], refs[2 * n_r + 7:])
        i = pl.program_id(0)

        @pl.when(i == 0)
        def _():
            push.start()
            xbuf[0:8, :] = jnp.zeros((8, D_MODEL), F32)
            hcar[...] = jnp.zeros_like(hcar)

        xbuf[8:8 + tm, :] = x_ref[...].astype(F32)
        c = _conv_taps(xbuf, tm, cw_ref, cb_ref)
        xbuf[0:8, :] = xbuf[tm:tm + 8, :]
        r, ig = _lru_gates(c, ba_ref[...], bx_ref[...], wa_ref, wx_ref)
        a, mult, _, _ = _lru_decay(r, lam_ref[...])
        cri_ref[0] = c.astype(BF16)
        cri_ref[1] = r.astype(BF16)
        cri_ref[2] = ig.astype(BF16)
        row = i * tm + lax.broadcasted_iota(jnp.int32, (tm, 1), 0)
        abuf[...] = a
        ubuf[...] = jnp.where(row >= PAD_ROWS, mult * (ig * c), 0.0)

        sub = lax.broadcasted_iota(jnp.int32, (8, D_MODEL), 0)

        def block(b, carry):
            off = pl.multiple_of(b * 8, 8)
            av, uv = abuf[pl.ds(off, 8), :], ubuf[pl.ds(off, 8), :]
            for s in (1, 2, 4):
                us = jnp.where(sub >= s, pltpu.roll(uv, s, 0), 0.0)
                as_ = jnp.where(sub >= s, pltpu.roll(av, s, 0), 1.0)
                uv = uv + av * us
                av = av * as_
            hv = uv + av * carry
            ubuf[pl.ds(off, 8), :] = hv
            return hv[7:8, :]

        hcar[...] = lax.fori_loop(0, tm // 8, block, hcar[...])
        gl, _ = _gelu_parts(gt_ref[...].astype(F32))
        hs = ubuf[...]
        hs_ref[...] = hs.astype(BF16)
        zl_ref[...] = (gl * hs).astype(BF16)

        @pl.when(i == n_t // 2)
        def _():
            push.relay()

        @pl.when(i == n_t - 1)
        def _():
            push.wait()

    vec = pl.BlockSpec((1, D_MODEL), lambda i: (0, 0))
    mat = pl.BlockSpec((LRU_BLOCKS, LRU_BLOCK, LRU_BLOCK), lambda i: (0, 0, 0))
    row = pl.BlockSpec((tm, D_MODEL), lambda i: (i, 0))
    return pl.pallas_call(
        body, name="lru_fwd", grid=(n_t,),
        in_specs=[_seg_spec(tm, 4), _seg_spec(tm, 5), pl.BlockSpec((4, D_MODEL), lambda i: (0, 0)),
                  vec, vec, vec, vec, mat, mat] + ride.specs(),
        out_specs=[row, row, pl.BlockSpec((3, tm, D_MODEL), lambda i: (0, i, 0))] + ride.specs(),
        out_shape=[jax.ShapeDtypeStruct((rows, D_MODEL), BF16)] * 2
        + [jax.ShapeDtypeStruct((3, rows, D_MODEL), BF16)] + ride.out_shapes,
        scratch_shapes=[pltpu.VMEM((tm + 8, D_MODEL), F32), pltpu.VMEM((tm, D_MODEL), F32),
                        pltpu.VMEM((tm, D_MODEL), F32), pltpu.VMEM((1, D_MODEL), F32)] + ride.scratch(),
        compiler_params=pltpu.CompilerParams(dimension_semantics=("arbitrary",), vmem_limit_bytes=VMEM_LIMIT,
                                             has_side_effects=True),
    )(proj, proj, conv_w, conv_b, ba, bx, lam, wa_g, wx_g, *ride.arrays)


def _mix_fwd(zr, zl, proj, h0, wbr, wbl, wout, ride):
    rows = h0.shape[0]
    tm = _tile(rows, 640)
    n_t = rows // tm
    n_r = ride.n

    def body(zr_ref, zl_ref, ga_ref, gb_ref, h0_ref, wbr_ref, wbl_ref, wo_ref, *refs):
        h1_ref, yr_ref, yl_ref, mx_ref = refs[n_r:n_r + 4]
        push = ride.push(refs[:n_r], refs[n_r + 4:2 * n_r + 4], refs[2 * n_r + 4:])

        @pl.when(pl.program_id(0) == 0)
        def _():
            push.start()

        yr = _dot(zr_ref[...], wbr_ref[...])
        yl = _dot(zl_ref[...], wbl_ref[...])
        mixed = (_sigmoid(ga_ref[...].astype(F32)) * yr + _sigmoid(gb_ref[...].astype(F32)) * yl).astype(BF16)
        yr_ref[...] = yr.astype(BF16)
        yl_ref[...] = yl.astype(BF16)
        mx_ref[...] = mixed
        h1_ref[...] = h0_ref[...] + _dot(mixed, wo_ref[...])

        @pl.when(pl.program_id(0) == n_t // 2)
        def _():
            push.relay()

        @pl.when(pl.program_id(0) == n_t - 1)
        def _():
            push.wait()

    row = pl.BlockSpec((tm, D_MODEL), lambda i: (i, 0))
    wsp = pl.BlockSpec((D_MODEL, D_MODEL), lambda i: (0, 0))
    return pl.pallas_call(
        body, name="mix_fwd", grid=(n_t,),
        in_specs=[row, row, _seg_spec(tm, 6), _seg_spec(tm, 7), row, wsp, wsp, wsp] + ride.specs(),
        out_specs=[row, row, row, row] + ride.specs(),
        out_shape=[jax.ShapeDtypeStruct((rows, D_MODEL), F32)] + [jax.ShapeDtypeStruct((rows, D_MODEL), BF16)] * 3
        + ride.out_shapes,
        scratch_shapes=ride.scratch(),
        compiler_params=pltpu.CompilerParams(dimension_semantics=("arbitrary",), vmem_limit_bytes=VMEM_LIMIT,
                                             has_side_effects=True),
    )(zr, zl, proj, proj, h0, wbr, wbl, wout, *ride.arrays)


def _ffn_fwd_loss(h1, norm_w, wfi_g, wfo_g, final_w, target):
    rows = h1.shape[0]
    tm = _tile(rows, 320)
    piece = 64
    n_piece = tm // piece

    def body(h1_ref, nw_ref, wfi_ref, wfo_ref, fw_ref, *refs):
        t_refs = refs[:n_piece]
        u2_ref, g_ref, up_ref, act_ref, dh2_ref, red_ref = refs[n_piece:]
        i = pl.program_id(0)

        @pl.when(i == 0)
        def _():
            red_ref[...] = jnp.zeros_like(red_ref)

        x = h1_ref[...]
        rs = lax.rsqrt(jnp.mean(x * x, axis=-1, keepdims=True) + NORM_EPS)
        u2 = (x * rs * nw_ref[...]).astype(BF16)
        u2_ref[...] = u2
        ffn = None
        for d in range(FFN_GROUPS):
            cols = slice(FFN_GROUP * d, FFN_GROUP * (d + 1))
            g = _dot(u2, wfi_ref[d])
            up = _dot(u2, wfi_ref[d + FFN_GROUPS])
            act = (g * _sigmoid(g) * up).astype(BF16)
            g_ref[:, cols] = g.astype(BF16)
            up_ref[:, cols] = up.astype(BF16)
            act_ref[:, cols] = act
            part = _dot(act, wfo_ref[d])
            ffn = part if ffn is None else ffn + part

        h2 = x + ffn
        rs = lax.rsqrt(jnp.mean(h2 * h2, axis=-1, keepdims=True) + NORM_EPS)
        nh = h2 * rs
        fw = fw_ref[...]
        row = i * tm + lax.broadcasted_iota(jnp.int32, (tm, 1), 0)
        tgt = jnp.concatenate([t[...] for t in t_refs], axis=0)
        diff = jnp.where(row >= CHUNK, nh * fw - tgt, 0.0)
        dy = diff * (1.0 / D_MODEL)
        red_ref[0:1, :] += jnp.sum(diff * diff, axis=0, keepdims=True)
        red_ref[1:2, :] += jnp.sum(dy * nh, axis=0, keepdims=True)
        dn = dy * fw
        dh2_ref[...] = rs * (dn - nh * jnp.mean(dn * nh, axis=-1, keepdims=True))

    row = pl.BlockSpec((tm, D_MODEL), lambda i: (i, 0))
    vec = pl.BlockSpec((1, D_MODEL), lambda i: (0, 0))
    hid = pl.BlockSpec((tm, FFN_GROUPS * FFN_GROUP), lambda i: (i, 0))
    hid_shape = jax.ShapeDtypeStruct((rows, FFN_GROUPS * FFN_GROUP), BF16)
    resident = dict(pipeline_mode=pl.Buffered(1))
    head_pieces = CHUNK // piece
    t_specs = [pl.BlockSpec((piece, D_MODEL), lambda i, k=k: (jnp.maximum(i * n_piece + k - head_pieces, 0), 0))
               for k in range(n_piece)]
    return pl.pallas_call(
        body, name="ffn_fwd_loss", grid=(rows // tm,),
        in_specs=[row, vec,
                  pl.BlockSpec((2 * FFN_GROUPS, D_MODEL, FFN_GROUP), lambda i: (0, 0, 0), **resident),
                  pl.BlockSpec((FFN_GROUPS, FFN_GROUP, D_MODEL), lambda i: (0, 0, 0), **resident),
                  vec] + t_specs,
        out_specs=[row, hid, hid, hid, row, pl.BlockSpec((8, D_MODEL), lambda i: (0, 0))],
        out_shape=[jax.ShapeDtypeStruct((rows, D_MODEL), BF16), hid_shape, hid_shape, hid_shape,
                   jax.ShapeDtypeStruct((rows, D_MODEL), F32), jax.ShapeDtypeStruct((8, D_MODEL), F32)],
        compiler_params=_cparams(("arbitrary",)),
    )(h1, norm_w, wfi_g, wfo_g, final_w, *([target] * n_piece))


def _wgrad(a, b, ka, tn, out_dtype, b_halves=False, row_cap=1664):
    rows = a.shape[0]
    na = a.shape[1] // ka
    tm = _tile(rows, row_cap)
    nm = rows // tm
    if b_halves:
        per_half = b.shape[2] // tn
        nb = 2 * per_half
        b_spec = pl.BlockSpec((None, tm, tn), lambda p, q, m: (q // per_half, m, q % per_half))
    else:
        nb = b.shape[1] // tn
        b_spec = pl.BlockSpec((tm, tn), lambda p, q, m: (m, q))

    def body(a_ref, b_ref, o_ref, acc):
        m = pl.program_id(2)

        @pl.when(m == 0)
        def _():
            acc[...] = jnp.zeros_like(acc)

        acc[...] += _dot_tn(a_ref[...].astype(BF16), b_ref[...].astype(BF16))

        @pl.when(m == nm - 1)
        def _():
            o_ref[...] = acc[...].astype(out_dtype)

    return pl.pallas_call(
        body, name="wgrad", grid=(na, nb, nm),
        in_specs=[pl.BlockSpec((tm, ka), lambda p, q, m: (m, p)), b_spec],
        out_specs=pl.BlockSpec((None, None, ka, tn), lambda p, q, m: (p, q, 0, 0)),
        out_shape=jax.ShapeDtypeStruct((na, nb, ka, tn), out_dtype),
        scratch_shapes=[pltpu.VMEM((ka, tn), F32)],
        compiler_params=_cparams(("parallel", "parallel", "arbitrary")),
    )(a, b)


WIN_NEAR = (2, 4, 3, 5, 1)
WIN_FAR = (6, 7)
WIN_ORDER = WIN_FAR + WIN_NEAR + (0,)


def _w_in_relation_at(jj):
    k = 0
    for pos in reversed(range(len(WIN_ORDER) - 1)):
        k = jnp.where(jj == pos, WIN_ORDER[pos], k)
    return k


def _wgrad_w_in(u, dproj, me_arr):
    rows = u.shape[0]
    tm = _tile(rows, 1664)
    nm = rows // tm
    n_near = len(WIN_NEAR)

    def body(me_ref, a_ref, b_ref, far_ref, land_ref, acc, sbuf, send_sems, recv_sems, own_sem):
        del me_ref
        jj, m = pl.program_id(0), pl.program_id(1)

        def near_copy(n):
            k = WIN_NEAR[n]
            return pltpu.make_async_remote_copy(src_ref=sbuf.at[n], dst_ref=land_ref.at[k], send_sem=send_sems.at[n],
                                                recv_sem=recv_sems.at[n], device_id=_peer(k)[0], device_id_type=MESH_ID)

        own_copy = pltpu.make_async_copy(sbuf.at[n_near], land_ref.at[0], own_sem)

        @pl.when(m == 0)
        def _():
            acc[...] = jnp.zeros_like(acc)

        acc[...] += _dot_tn(a_ref[...], b_ref[...])

        for pos, k in enumerate(WIN_ORDER):
            @pl.when(jnp.logical_and(jj == pos, m == nm - 1))
            def _(k=k):
                block = acc[...].astype(BF16)
                if k in WIN_FAR:
                    far_ref[...] = block
                elif k == 0:
                    sbuf[n_near] = block
                    own_copy.start()
                else:
                    sbuf[WIN_NEAR.index(k)] = block
                    near_copy(WIN_NEAR.index(k)).start()

        @pl.when(jnp.logical_and(jj == N_DEV - 1, m == nm - 1))
        def _():
            for n in range(n_near):
                near_copy(n).wait_recv()
            for n in range(n_near):
                near_copy(n).wait_send()
            own_copy.wait()

    grid_spec = pltpu.PrefetchScalarGridSpec(
        num_scalar_prefetch=1, grid=(N_DEV, nm),
        in_specs=[pl.BlockSpec((tm, D_MODEL), lambda jj, m, me: (m, 0)),
                  pl.BlockSpec((tm, D_MODEL), lambda jj, m, me: (m, me[0] ^ _w_in_relation_at(jj)))],
        out_specs=[pl.BlockSpec((None, D_MODEL, D_MODEL), lambda jj, m, me: (jnp.minimum(jj, len(WIN_FAR) - 1), 0, 0)),
                   ANY],
        scratch_shapes=[pltpu.VMEM((D_MODEL, D_MODEL), F32), pltpu.VMEM((n_near + 1, D_MODEL, D_MODEL), BF16),
                        pltpu.SemaphoreType.DMA((n_near,)), pltpu.SemaphoreType.DMA((n_near,)),
                        pltpu.SemaphoreType.DMA(())])
    return pl.pallas_call(
        body, name="wgrad_w_in", grid_spec=grid_spec,
        out_shape=[jax.ShapeDtypeStruct((len(WIN_FAR), D_MODEL, D_MODEL), BF16),
                   jax.ShapeDtypeStruct((n_near + 1, D_MODEL, D_MODEL), BF16)],
        compiler_params=pltpu.CompilerParams(dimension_semantics=("arbitrary", "arbitrary"),
                                             vmem_limit_bytes=VMEM_LIMIT, has_side_effects=True),
    )(me_arr, u, dproj)


def _ffn_bwd(dh2, g, up, h1, norm_w, wfi_g, wfo_g):
    rows = h1.shape[0]
    tm = _tile(rows, 320)

    def body(dh2_ref, g_ref, up_ref, h1_ref, nw_ref, wfi_ref, wfo_ref, dgu_ref, dh1_ref, dw_ref):
        @pl.when(pl.program_id(0) == 0)
        def _():
            dw_ref[...] = jnp.zeros_like(dw_ref)

        dh2 = dh2_ref[...]
        dh2_b = dh2.astype(BF16)
        du2 = None
        for d in range(FFN_GROUPS):
            cols = slice(FFN_GROUP * d, FFN_GROUP * (d + 1))
            dact = _dot_nt(dh2_b, wfo_ref[d])
            gv, uv = g_ref[:, cols].astype(F32), up_ref[:, cols].astype(F32)
            sg = _sigmoid(gv)
            dg = (dact * uv * (sg * (1.0 + gv * (1.0 - sg)))).astype(BF16)
            dup = (dact * (gv * sg)).astype(BF16)
            dgu_ref[0, :, cols] = dg
            dgu_ref[1, :, cols] = dup
            part = _dot_nt(dg, wfi_ref[d]) + _dot_nt(dup, wfi_ref[d + FFN_GROUPS])
            du2 = part if du2 is None else du2 + part
        dx, dw = _rms_bwd(h1_ref[...], nw_ref[...], du2)
        dw_ref[0:1, :] += dw
        dh1_ref[...] = dh2 + dx

    row = pl.BlockSpec((tm, D_MODEL), lambda i: (i, 0))
    vec = pl.BlockSpec((1, D_MODEL), lambda i: (0, 0))
    hid = pl.BlockSpec((tm, FFN_GROUPS * FFN_GROUP), lambda i: (i, 0))
    resident = dict(pipeline_mode=pl.Buffered(1))
    return pl.pallas_call(
        body, name="ffn_bwd", grid=(rows // tm,),
        in_specs=[row, hid, hid, row, vec,
                  pl.BlockSpec((2 * FFN_GROUPS, D_MODEL, FFN_GROUP), lambda i: (0, 0, 0), **resident),
                  pl.BlockSpec((FFN_GROUPS, FFN_GROUP, D_MODEL), lambda i: (0, 0, 0), **resident)],
        out_specs=[pl.BlockSpec((2, tm, FFN_GROUPS * FFN_GROUP), lambda i: (0, i, 0)), row,
                   pl.BlockSpec((8, D_MODEL), lambda i: (0, 0))],
        out_shape=[jax.ShapeDtypeStruct((2, rows, FFN_GROUPS * FFN_GROUP), BF16),
                   jax.ShapeDtypeStruct((rows, D_MODEL), F32), jax.ShapeDtypeStruct((8, D_MODEL), F32)],
        compiler_params=_cparams(("arbitrary",)),
    )(dh2, g, up, h1, norm_w, wfi_g, wfo_g)


def _mix_bwd(dh1, yr, yl, proj, wbr, wbl, wout):
    rows = dh1.shape[0]
    tm = _tile(rows, 640)

    def body(dh1_ref, yr_ref, yl_ref, ga_ref, gb_ref, wbr_ref, wbl_ref, wo_ref,
             dyr_ref, dyl_ref, dseg_ref, dzr_ref, dzl_ref):
        dmix = _dot_nt(dh1_ref[...].astype(BF16), wo_ref[...])
        sa, sb = _sigmoid(ga_ref[...].astype(F32)), _sigmoid(gb_ref[...].astype(F32))
        dyr = (dmix * sa).astype(BF16)
        dyl = (dmix * sb).astype(BF16)
        dyr_ref[...] = dyr
        dyl_ref[...] = dyl
        dseg_ref[:, 0:D_MODEL] = (dmix * yr_ref[...].astype(F32) * (sa * (1.0 - sa))).astype(BF16)
        dseg_ref[:, D_MODEL:2 * D_MODEL] = (dmix * yl_ref[...].astype(F32) * (sb * (1.0 - sb))).astype(BF16)
        dzr_ref[...] = _dot_nt(dyr, wbr_ref[...]).astype(BF16)
        dzl_ref[...] = _dot_nt(dyl, wbl_ref[...]).astype(BF16)

    row = pl.BlockSpec((tm, D_MODEL), lambda i: (i, 0))
    wsp = pl.BlockSpec((D_MODEL, D_MODEL), lambda i: (0, 0))
    bshape = jax.ShapeDtypeStruct((rows, D_MODEL), BF16)
    return pl.pallas_call(
        body, name="mix_bwd", grid=(rows // tm,),
        in_specs=[row, row, row, _seg_spec(tm, 6), _seg_spec(tm, 7), wsp, wsp, wsp],
        out_specs=[row, row, pl.BlockSpec((tm, 2 * D_MODEL), lambda i: (i, 3)), row, row],
        out_shape=[bshape, bshape, jax.ShapeDtypeStruct((rows, N_DEV * D_MODEL), BF16), bshape, bshape],
        compiler_params=_cparams(("parallel",)),
    )(dh1, yr, yl, proj, proj, wbr, wbl, wout)


S1_SHAPES = [jax.ShapeDtypeStruct((N_DEV, D_MODEL, FFN_GROUP), BF16)]


def _s1_parts(ins, p):
    return [ins[0].at[p]]


def _lru_bwd(dzl, hs, cri, proj, dproj, conv_w, lam, wa_g, wx_g, s1_grads):
    rows = dzl.shape[0]
    tm = _tile(rows, 640)
    nt = rows // tm
    t8 = tm // 8
    n_s1 = len(s1_grads)

    def body(dzl_ref, hs_ref, hsp_ref, cri_ref, x_ref, gt_ref, cw_ref, lam_ref, wa_ref, wx_ref, dproj_in, *refs):
        del dproj_in
        s1_refs = refs[:n_s1]
        dseg_ref, dwa_ref, dwx_ref, sm_ref = refs[n_s1:n_s1 + 4]
        land_refs = refs[n_s1 + 4:2 * n_s1 + 4]
        (xbuf, abuf, mbuf, ibuf, dbuf, dcbuf, dpr_s, dpi_s, sums, conv_sums, anext, dhcar,
         send_sems, recv_sems, loc_sems) = refs[2 * n_s1 + 4:]
        step = pl.program_id(0)
        i = nt - 1 - step
        push = _Push(lambda p: _s1_parts(s1_refs, p), lambda s: [r.at[s] for r in land_refs],
                     (send_sems, recv_sems, loc_sems), n_s1)

        @pl.when(step == 0)
        def _():
            push.start()
            dwa_ref[...] = jnp.zeros_like(dwa_ref)
            dwx_ref[...] = jnp.zeros_like(dwx_ref)
            sm_ref[...] = jnp.zeros_like(sm_ref)
            anext[...] = jnp.zeros_like(anext)
            dhcar[...] = jnp.zeros_like(dhcar)
            dcbuf[tm:tm + 8, :] = jnp.zeros((8, D_MODEL), F32)

        slab, lanes = 16, 256
        lam_v = lam_ref[...]
        xbuf[0:8, :] = jnp.where(i == 0, 0.0, hsp_ref[8:16, :].astype(F32))
        sums[...] = jnp.zeros_like(sums)

        def before_scan(k, carry):
            rw = pl.ds(pl.multiple_of(k * slab, slab), slab)
            for q in range(D_MODEL // lanes):
                ln = slice(lanes * q, lanes * (q + 1))
                a, mult, inv_mult, _ = _lru_decay(cri_ref[1, rw, ln].astype(F32), lam_v[:, ln])
                abuf[rw, ln] = a
                mbuf[rw, ln] = mult
                ibuf[rw, ln] = inv_mult
                gl, dgl = _gelu_parts(gt_ref[rw, ln].astype(F32))
                dzl_v = dzl_ref[rw, ln].astype(F32)
                hs_v = hs_ref[rw, ln].astype(F32)
                dseg_ref[rw, D_MODEL + lanes * q:D_MODEL + lanes * (q + 1)] = (dzl_v * hs_v * dgl).astype(BF16)
                dbuf[rw, ln] = dzl_v * gl
                xbuf[pl.ds(pl.multiple_of(k * slab + 8, 8), slab), ln] = hs_v
            return carry

        lax.fori_loop(0, tm // slab, before_scan, 0)

        sub = lax.broadcasted_iota(jnp.int32, (8, D_MODEL), 0)

        def block(k, carry):
            dh_next, a_next = carry
            off = pl.multiple_of((t8 - 1 - k) * 8, 8)
            a_blk = abuf[pl.ds(off, 8), :]
            av = jnp.where(sub < 7, pltpu.roll(a_blk, 7, 0), a_next)
            uv = dbuf[pl.ds(off, 8), :]
            for s in (1, 2, 4):
                us = jnp.where(sub < 8 - s, pltpu.roll(uv, 8 - s, 0), 0.0)
                as_ = jnp.where(sub < 8 - s, pltpu.roll(av, 8 - s, 0), 1.0)
                uv = uv + av * us
                av = av * as_
            hv = uv + av * dh_next
            dbuf[pl.ds(off, 8), :] = hv
            return hv[0:1, :], a_blk[0:1, :]

        dh_first, a_first = lax.fori_loop(0, t8, block, (dhcar[...], anext[...]))
        dhcar[...] = dh_first
        anext[...] = a_first

        sp = jnp.maximum(-lam_v, 0.0) + jnp.log(1.0 + jnp.exp(-jnp.abs(lam_v)))
        sub_q = lax.broadcasted_iota(jnp.int32, (8, lanes), 0)
        row16 = lax.broadcasted_iota(jnp.int32, (slab, 1), 0)

        def after_scan(k, carry):
            off = pl.multiple_of(k * slab, slab)
            rw = pl.ds(off, slab)
            for q in range(D_MODEL // lanes):
                ln = slice(lanes * q, lanes * (q + 1))
                before = xbuf[pl.ds(off, 8), ln]
                h_lo = xbuf[pl.ds(pl.multiple_of(off + 8, 8), 8), ln]
                h_hi = xbuf[pl.ds(pl.multiple_of(off + 16, 8), 8), ln]
                hprev = jnp.concatenate([jnp.where(sub_q >= 1, pltpu.roll(h_lo, 1, 0), before[7:8, :]),
                                         jnp.where(sub_q >= 1, pltpu.roll(h_hi, 1, 0), h_lo[7:8, :])], axis=0)
                c, r, ig = (cri_ref[n, rw, ln].astype(F32) for n in range(3))
                a, mult, inv_mult = abuf[rw, ln], mbuf[rw, ln], ibuf[rw, ln]
                dh = dbuf[rw, ln]
                duu = jnp.where(i * tm + off + row16 >= PAD_ROWS, dh, 0.0)
                t_mult = duu * mult
                dlog_a = dh * hprev * a - duu * ig * c * (a * a) * inv_mult
                dpr = dlog_a * (-LRU_C * sp[:, ln]) * r * (1.0 - r)
                dpi = t_mult * c * ig * (1.0 - ig)
                dpr_s[rw, ln] = dpr.astype(BF16)
                dpi_s[rw, ln] = dpi.astype(BF16)
                dcbuf[rw, ln] = t_mult * ig
                sums[0, :, ln] += dlog_a * r
                sums[1, :, ln] += dpr
                sums[2, :, ln] += dpi
            return carry

        lax.fori_loop(0, tm // slab, after_scan, 0)

        dcs = []
        for g in range(LRU_BLOCKS):
            sl = slice(LRU_BLOCK * g, LRU_BLOCK * (g + 1))
            cg = cri_ref[0, :, sl]
            dpr_b, dpi_b = dpr_s[:, sl], dpi_s[:, sl]
            dwa_ref[g] += _dot_tn(cg, dpr_b)
            dwx_ref[g] += _dot_tn(cg, dpi_b)
            dcs.append(_dot_nt(dpr_b, wa_ref[g]) + _dot_nt(dpi_b, wx_ref[g]))
        dc = dcbuf[0:tm, :] + jnp.concatenate(dcs, axis=1)

        dcbuf[0:tm, :] = dc
        conv_sums[...] = jnp.zeros_like(conv_sums)

        def conv_back(k, carry):
            off = pl.multiple_of(k * slab, slab)
            rw = pl.ds(off, slab)
            for q in range(D_MODEL // lanes):
                ln = slice(lanes * q, lanes * (q + 1))
                blocks = [dcbuf[pl.ds(pl.multiple_of(off + 8 * b, 8), 8), ln] for b in range(3)]
                x_v = x_ref[rw, ln].astype(F32)
                now = jnp.concatenate(blocks[:2], axis=0)
                dlin = cw_ref[3:4, ln] * now
                conv_sums[3, :, ln] += now * x_v
                conv_sums[4, :, ln] += now
                for back in (1, 2, 3):
                    turned = [pltpu.roll(b, 8 - back, 0) for b in blocks]
                    later = jnp.concatenate([jnp.where(sub_q < 8 - back, turned[0], turned[1]),
                                             jnp.where(sub_q < 8 - back, turned[1], turned[2])], axis=0)
                    dlin = dlin + cw_ref[3 - back:4 - back, ln] * later
                    conv_sums[3 - back, :, ln] += later * x_v
                dseg_ref[rw, ln] = dlin.astype(BF16)
            return carry

        lax.fori_loop(0, tm // slab, conv_back, 0)
        dcbuf[tm:tm + 8, :] = dcbuf[0:8, :]
        for n in range(5):
            sm_ref[n:n + 1, :] += jnp.sum(conv_sums[n], axis=0, keepdims=True)
        sm_ref[5:6, :] += jnp.sum(sums[1], axis=0, keepdims=True)
        sm_ref[6:7, :] += jnp.sum(sums[2], axis=0, keepdims=True)
        sm_ref[7:8, :] += jnp.sum(sums[0], axis=0, keepdims=True) * (LRU_C * _sigmoid(-lam_v))

        @pl.when(step == nt - 1)
        def _():
            push.wait()

    rowb = pl.BlockSpec((tm, D_MODEL), lambda s: (nt - 1 - s, 0))
    t16 = tm // 16
    prev8 = pl.BlockSpec((16, D_MODEL), lambda s: (jnp.maximum((nt - 1 - s) * t16 - 1, 0), 0))
    seg = lambda k: pl.BlockSpec((tm, D_MODEL), lambda s, k=k: (nt - 1 - s, k))
    vec = pl.BlockSpec((1, D_MODEL), lambda s: (0, 0))
    mat = pl.BlockSpec((LRU_BLOCKS, LRU_BLOCK, LRU_BLOCK), lambda s: (0, 0, 0))
    mshape = jax.ShapeDtypeStruct((LRU_BLOCKS, LRU_BLOCK, LRU_BLOCK), F32)
    n_in = 10
    return pl.pallas_call(
        body, name="lru_bwd", grid=(nt,),
        in_specs=[rowb, rowb, prev8, pl.BlockSpec((3, tm, D_MODEL), lambda s: (0, nt - 1 - s, 0)), seg(4), seg(5),
                  pl.BlockSpec((4, D_MODEL), lambda s: (0, 0)), vec, mat, mat, ANY] + [ANY] * n_s1,
        out_specs=[pl.BlockSpec((tm, 2 * D_MODEL), lambda s: (nt - 1 - s, 2)), mat, mat,
                   pl.BlockSpec((8, D_MODEL), lambda s: (0, 0))] + [ANY] * n_s1,
        out_shape=[jax.ShapeDtypeStruct(dproj.shape, dproj.dtype), mshape, mshape,
                   jax.ShapeDtypeStruct((8, D_MODEL), F32)] + S1_SHAPES,
        input_output_aliases={n_in: 0},
        scratch_shapes=[pltpu.VMEM((tm + 8, D_MODEL), F32), pltpu.VMEM((tm, D_MODEL), F32),
                        pltpu.VMEM((tm, D_MODEL), F32), pltpu.VMEM((tm, D_MODEL), F32),
                        pltpu.VMEM((tm, D_MODEL), F32), pltpu.VMEM((tm + 8, D_MODEL), F32),
                        pltpu.VMEM((tm, D_MODEL), BF16), pltpu.VMEM((tm, D_MODEL), BF16),
                        pltpu.VMEM((3, 16, D_MODEL), F32), pltpu.VMEM((5, 16, D_MODEL), F32),
                        pltpu.VMEM((1, D_MODEL), F32), pltpu.VMEM((1, D_MODEL), F32)] + _push_sems(n_s1),
        compiler_params=pltpu.CompilerParams(dimension_semantics=("arbitrary",), vmem_limit_bytes=VMEM_LIMIT,
                                             has_side_effects=True),
    )(dzl, hs, hs, cri, proj, proj, conv_w, lam, wa_g, wx_g, dproj, *s1_grads)


def _retention_bwd(dzr, o, proj, states, cos2, sin2, dec, dproj, ride):
    rows = dzr.shape[0]
    n_chunks = rows // CHUNK
    per_step = _chunks_per_step(n_chunks)
    n_steps = n_chunks // per_step
    tm = per_step * CHUNK
    n_r = ride.n

    def body(dzr_ref, o_ref, q_ref, k_ref, v_ref, g_ref, st_ref, c_ref, s_ref, dec_ref, dproj_in, *refs):
        del dproj_in
        dseg_ref = refs[n_r]
        dstate = refs[2 * n_r + 1]
        push = ride.push(refs[:n_r], refs[n_r + 1:2 * n_r + 1], refs[2 * n_r + 2:])

        @pl.when(pl.program_id(0) == 0)
        def _():
            push.start()
            dstate[...] = jnp.zeros_like(dstate)

        for h in range(HEADS):
            sl = slice(HEAD_DIM * h, HEAD_DIM * (h + 1))
            intra, qd, kd, cd = dec_ref[0, h], dec_ref[1, h], dec_ref[2, h], dec_ref[3, h]
            dst = dstate[h]
            for c in reversed(range(per_step)):
                rw = slice(CHUNK * c, CHUNK * (c + 1))
                cos_t, sin_t = c_ref[rw, :], s_ref[rw, :]
                o = o_ref[rw, sl].astype(F32)
                g = g_ref[rw, sl].astype(F32)
                dzr_v = dzr_ref[rw, sl].astype(F32)
                sg = _sigmoid(g)
                r = lax.rsqrt(jnp.mean(o * o, axis=-1, keepdims=True) + NORM_EPS)
                on = o * r
                dseg_ref[rw, 3 * D_MODEL + HEAD_DIM * h:3 * D_MODEL + HEAD_DIM * (h + 1)] = (
                    dzr_v * on * (sg * (1.0 + g * (1.0 - sg)))).astype(BF16)
                don = dzr_v * (g * sg)
                do = r * (don - on * jnp.mean(don * on, axis=-1, keepdims=True))
                dob = do.astype(BF16)

                qh = _rot(q_ref[rw, sl].astype(F32), cos_t, sin_t)
                kh = _rot(k_ref[rw, sl].astype(F32), cos_t, sin_t) * QK_SCALE
                qb, kb, vb = qh.astype(BF16), kh.astype(BF16), v_ref[rw, sl]
                s = (_dot_nt(qb, kb) * intra).astype(BF16)
                ds = (_dot_nt(dob, vb) * intra).astype(BF16)
                st_b = st_ref[c, h].astype(BF16)
                dst_b = dst.astype(BF16)
                dv = _dot_tn(s, dob) + _dot((kh * kd).astype(BF16), dst_b)
                dq = _dot(ds, kb) + _dot_nt(dob, st_b) * qd
                dk = _dot_tn(ds, qb) + _dot_nt(vb, dst_b) * kd
                dst = dst * cd + _dot_tn((qh * qd).astype(BF16), dob)
                dseg_ref[rw, 2 * D_MODEL + HEAD_DIM * h:2 * D_MODEL + HEAD_DIM * (h + 1)] = dv.astype(BF16)
                dseg_ref[rw, sl] = _rot_t(dq, cos_t, sin_t).astype(BF16)
                dseg_ref[rw, D_MODEL + HEAD_DIM * h:D_MODEL + HEAD_DIM * (h + 1)] = (
                    _rot_t(dk, cos_t, sin_t) * QK_SCALE).astype(BF16)
            dstate[h] = dst

        @pl.when(pl.program_id(0) == n_steps - 1)
        def _():
            push.wait()

    rev = lambda s: n_steps - 1 - s
    rowb = pl.BlockSpec((tm, D_MODEL), lambda s: (rev(s), 0))
    seg = lambda k: pl.BlockSpec((tm, D_MODEL), lambda s, k=k: (rev(s), k))
    tab = pl.BlockSpec((tm, HEAD_DIM), lambda s: (rev(s), 0))
    return pl.pallas_call(
        body, name="retention_bwd", grid=(n_steps,),
        in_specs=[rowb, rowb, seg(0), seg(1), seg(2), seg(3),
                  pl.BlockSpec((per_step, HEADS, HEAD_DIM, HEAD_DIM), lambda s: (rev(s), 0, 0, 0)), tab, tab,
                  pl.BlockSpec((4, HEADS, CHUNK, CHUNK), lambda s: (0, 0, 0, 0)), ANY] + ride.specs(),
        out_specs=[pl.BlockSpec((tm, 4 * D_MODEL), lambda s: (rev(s), 0))] + ride.specs(),
        out_shape=[jax.ShapeDtypeStruct(dproj.shape, dproj.dtype)] + ride.out_shapes,
        input_output_aliases={10: 0},
        scratch_shapes=[pltpu.VMEM((HEADS, HEAD_DIM, HEAD_DIM), F32)] + ride.scratch(),
        compiler_params=pltpu.CompilerParams(dimension_semantics=("arbitrary",), vmem_limit_bytes=VMEM_LIMIT,
                                             has_side_effects=True),
    )(dzr, o, proj, proj, proj, proj, states, cos2, sin2, dec, dproj, *ride.arrays)


S2_SHAPES = [
    jax.ShapeDtypeStruct((N_DEV, LRU_BLOCKS, LRU_ROWS, LRU_BLOCK), F32),
    jax.ShapeDtypeStruct((N_DEV, LRU_BLOCKS, LRU_ROWS, LRU_BLOCK), F32),
]


def _s2_parts(ins, p):
    return [r.at[p] for r in ins]


def _in_proj_bwd(dproj, win_g, h0, norm_w, dh1, d_win_far, s2_grads, pack_early):
    rows = h0.shape[0]
    tm = _tile(rows, 320)
    n_i = rows // tm
    n_s2 = len(s2_grads)
    n_far = len(WIN_FAR)
    pack_rows = pack_early.shape[0]

    def body(dseg_ref, w_ref, h0_ref, nw_ref, dh1_ref, far_ref, early_ref, *refs):
        s2_refs = refs[:n_s2]
        dh0_ref, dw_ref, far_land = refs[n_s2:n_s2 + 3]
        land_refs = refs[n_s2 + 3:2 * n_s2 + 3]
        early_land, late_land = refs[2 * n_s2 + 3:2 * n_s2 + 5]
        (send_sems, recv_sems, loc_sems, far_send_sems, far_recv_sems, late_buf) = refs[2 * n_s2 + 5:2 * n_s2 + 11]
        early_sems, late_sems = refs[2 * n_s2 + 11:2 * n_s2 + 14], refs[2 * n_s2 + 14:]
        i = pl.program_id(0)
        push = _Push(lambda p: _s2_parts(s2_refs, p), lambda s: [r.at[s] for r in land_refs],
                     (send_sems, recv_sems, loc_sems), n_s2)
        early = _Push(lambda p: [early_ref], lambda s: [early_land.at[s]], tuple(early_sems), 1)
        late = _Push(lambda p: [late_buf], lambda s: [late_land.at[s]], tuple(late_sems), 1)

        def far_copy(n):
            return pltpu.make_async_remote_copy(src_ref=far_ref.at[n], dst_ref=far_land.at[n],
                                                send_sem=far_send_sems.at[n], recv_sem=far_recv_sems.at[n],
                                                device_id=_peer(WIN_FAR[n])[0], device_id_type=MESH_ID)

        @pl.when(i == 0)
        def _():
            for n in range(n_far):
                far_copy(n).start()
            push.start()
            early.start()
            dw_ref[...] = jnp.zeros_like(dw_ref)

        du = _dot_nt(dseg_ref[:, 0:D_MODEL], w_ref[0])
        for j in range(1, N_DEV):
            du = du + _dot_nt(dseg_ref[:, D_MODEL * j:D_MODEL * (j + 1)], w_ref[j])
        dx, dw = _rms_bwd(h0_ref[...], nw_ref[...], du)
        dw_ref[0:1, :] += dw
        dh0 = dh1_ref[...] + dx
        dh0_ref[...] = dh0

        @pl.when(i == 0)
        def _():
            late_buf[8:8 + N_META, :] = dh0[PAD_ROWS:CHUNK, :]

        @pl.when(i == n_i - 1)
        def _():
            late_buf[0:8, :] = dw_ref[...]
            late.start()
            for n in range(n_far):
                far_copy(n).wait_recv()
            for n in range(n_far):
                far_copy(n).wait_send()
            push.wait()
            early.wait()
            late.wait()

    row = pl.BlockSpec((tm, D_MODEL), lambda i: (i, 0))
    vec = pl.BlockSpec((1, D_MODEL), lambda i: (0, 0))
    return pl.pallas_call(
        body, name="in_proj_bwd", grid=(n_i,),
        in_specs=[pl.BlockSpec((tm, N_DEV * D_MODEL), lambda i: (i, 0)),
                  pl.BlockSpec((N_DEV, D_MODEL, D_MODEL), lambda i: (0, 0, 0), pipeline_mode=pl.Buffered(1)),
                  row, vec, row, ANY, ANY] + [ANY] * n_s2,
        out_specs=[row, pl.BlockSpec((8, D_MODEL), lambda i: (0, 0)), ANY] + [ANY] * n_s2 + [ANY, ANY],
        out_shape=[jax.ShapeDtypeStruct((rows, D_MODEL), F32), jax.ShapeDtypeStruct((8, D_MODEL), F32),
                   jax.ShapeDtypeStruct((n_far, D_MODEL, D_MODEL), BF16)] + S2_SHAPES
        + [jax.ShapeDtypeStruct((N_DEV, pack_rows, D_MODEL), F32)] * 2,
        scratch_shapes=_push_sems(n_s2) + [pltpu.SemaphoreType.DMA((n_far,)), pltpu.SemaphoreType.DMA((n_far,)),
                                           pltpu.VMEM((pack_rows, D_MODEL), F32)] + _push_sems(1) + _push_sems(1),
        compiler_params=pltpu.CompilerParams(dimension_semantics=("arbitrary",),
                                             vmem_limit_bytes=VMEM_LIMIT, has_side_effects=True),
    )(dproj, win_g, h0, norm_w, dh1, d_win_far, pack_early, *s2_grads)


def _adamw(g_slots, w, m, v, more_slots=None):
    slots, rows, cols = g_slots.shape
    extra = [] if more_slots is None else [more_slots]
    tr = rows
    for cand in (256, 128, 64, 32, 16, 8):
        if rows % cand == 0 and rows > cand:
            tr = cand
            break

    def body(g_ref, *refs):
        w_ref, m_ref, v_ref, go_ref, d_ref, mo_ref, vo_ref = refs[len(extra):]
        g = g_ref[0].astype(F32)
        for s in range(1, slots):
            g = g + g_ref[s].astype(F32)
        for more_ref in refs[:len(extra)]:
            for s in range(more_ref.shape[0]):
                g = g + more_ref[s].astype(F32)
        m2 = ADAM_B1 * m_ref[...] + (1.0 - ADAM_B1) * g
        v2 = ADAM_B2 * v_ref[...] + (1.0 - ADAM_B2) * (g * g)
        m_hat = m2 / (1.0 - ADAM_B1 ** ADAM_STEP)
        v_hat = v2 / (1.0 - ADAM_B2 ** ADAM_STEP)
        go_ref[...] = g
        d_ref[...] = -ADAM_LR * (m_hat / (jnp.sqrt(v_hat) + ADAM_EPS) + ADAM_WD * w_ref[...])
        mo_ref[...] = m2
        vo_ref[...] = v2

    blk = pl.BlockSpec((tr, cols), lambda i: (i, 0))
    shape = jax.ShapeDtypeStruct((rows, cols), F32)
    return pl.pallas_call(
        body, name="adamw", grid=(rows // tr,),
        in_specs=[pl.BlockSpec((slots, tr, cols), lambda i: (0, i, 0))]
        + [pl.BlockSpec((t.shape[0], tr, cols), lambda i: (0, i, 0)) for t in extra] + [blk, blk, blk],
        out_specs=[blk] * 4, out_shape=[shape] * 4,
        compiler_params=_cparams(("parallel",)),
    )(g_slots, *extra, w, m, v)


def _sum_slots(packs):
    slots, rows, cols = packs.shape

    def body(p_ref, o_ref):
        acc = p_ref[0]
        for s in range(1, slots):
            acc = acc + p_ref[s]
        o_ref[...] = acc

    return pl.pallas_call(
        body, name="sum_slots", out_shape=jax.ShapeDtypeStruct((rows, cols), F32),
        compiler_params=pltpu.CompilerParams(vmem_limit_bytes=VMEM_LIMIT),
    )(packs)


def _gather_small(small):
    shapes = [jax.ShapeDtypeStruct((N_DEV,) + small.shape, F32)]
    return _push_call("gather_small", [small], shapes,
                      lambda ins, p: list(ins), lambda outs, s: [r.at[s] for r in outs])[0]


PACK_CONV_W, PACK_CONV_B, PACK_BA, PACK_BX, PACK_LAM = 0, 4, 5, 6, 7
PACK_FFN_NORM, PACK_SQ_ERR, PACK_FINAL_NORM, PACK_MIX_NORM, PACK_META = 8, 16, 17, 24, 32


def kernel(x, meta_tokens, mix_norm_w, w_in, conv_w, conv_b, lru_wa, lru_ba, lru_wx, lru_bx, lru_lambda, w_branch_ret, w_branch_lru, w_out, ffn_norm_w, w_ffn_in, w_ffn_out, final_norm_w, loss_target, m_meta_tokens, m_mix_norm_w, m_w_in, m_conv_w, m_conv_b, m_lru_wa, m_lru_ba, m_lru_wx, m_lru_bx, m_lru_lambda, m_w_branch_ret, m_w_branch_lru, m_w_out, m_ffn_norm_w, m_w_ffn_in, m_w_ffn_out, m_final_norm_w, v_meta_tokens, v_mix_norm_w, v_w_in, v_conv_w, v_conv_b, v_lru_wa, v_lru_ba, v_lru_wx, v_lru_bx, v_lru_lambda, v_w_branch_ret, v_w_branch_lru, v_w_out, v_ffn_norm_w, v_w_ffn_in, v_w_ffn_out, v_final_norm_w):
    me = _my_index()
    pad4 = ((0, 4), (0, 0))
    fw = final_norm_w.reshape(1, D_MODEL)

    small = jnp.concatenate([meta_tokens, jnp.pad(conv_w[0], pad4)], axis=0)
    small_g = _gather_small(small)
    meta_full = small_g[:, :N_META].transpose(1, 0, 2).reshape(N_META, D_MODEL)
    conv_w_full = small_g[:, N_META:N_META + 4].transpose(1, 0, 2).reshape(4, D_MODEL)
    mixer_shards = [w_branch_ret[0].astype(BF16), w_branch_lru[0].astype(BF16), w_out[0].astype(BF16),
                    lru_wa[0].astype(BF16), lru_wx[0].astype(BF16)]
    wfi_shard = jnp.pad(w_ffn_in[0].astype(BF16), ((0, 0), (0, FFN_GROUP - FFN_SHARD)))
    own_slot = lambda ins, p: list(ins)

    rows = x.shape[1] + CHUNK
    h0 = jnp.concatenate([jnp.zeros((PAD_ROWS, D_MODEL), F32), meta_full, x[0]], axis=0)
    cos2, sin2 = _rope_tables(rows)
    dec = _retention_consts()

    me_arr = me.astype(jnp.int32).reshape(1)
    proj, u, win_g = _in_proj(h0, mix_norm_w, w_in[0].astype(BF16), me_arr)
    o, zr, states, wbr_g, wbl_g, wout_g, wa_g, wx_g = _retention_fwd(
        proj, cos2, sin2, dec, _mixer_weights_ride(mixer_shards))
    wbr, wbl, wout = (t.reshape(D_MODEL, D_MODEL) for t in (wbr_g, wbl_g, wout_g))
    wa_g, wx_g = _from_owners(wa_g), _from_owners(wx_g)
    gather_wfi = _Ride([wfi_shard], [jax.ShapeDtypeStruct((N_DEV, D_MODEL, FFN_GROUP), BF16)],
                       own_slot, _slot_of_sender, gather_by_chip=True)
    hs, zl, cri, wfi_g = _lru_fwd(proj, conv_w_full, conv_b, lru_ba, lru_bx, lru_lambda, wa_g, wx_g, gather_wfi)
    h1, yr, yl, mixed, wfo_g = _mix_fwd(zr, zl, proj, h0, wbr, wbl, wout, _wfo_ride(w_ffn_out[0].astype(BF16)))
    u2, g, up, act, dh2, red = _ffn_fwd_loss(h1, ffn_norm_w, wfi_g, wfo_g, fw, loss_target[0])

    d_wfo = _wgrad(act, dh2, FFN_GROUP, D_MODEL, BF16)[:, 0]
    dgu, dh1, dw_ffn_norm = _ffn_bwd(dh2, g, up, h1, ffn_norm_w, wfi_g, wfo_g)
    d_wfi = _wgrad(u2, dgu, D_MODEL, FFN_GROUP, BF16, b_halves=True, row_cap=4160)[0]
    d_wout = _wgrad(mixed, dh1, D_MODEL, D_MODEL, BF16)[0, 0]
    dyr, dyl, dproj, dzr, dzl = _mix_bwd(dh1, yr, yl, proj, wbr, wbl, wout)
    d_wbr = _wgrad(zr, dyr, D_MODEL, D_MODEL, BF16)[0, 0]
    d_wbl = _wgrad(zl, dyl, D_MODEL, D_MODEL, BF16)[0, 0]
    dproj, d_wa, d_wx, lru_small, r_fi = _lru_bwd(
        dzl, hs, cri, proj, dproj, conv_w_full, lru_lambda, wa_g, wx_g, [d_wfi])
    mix_shape = jax.ShapeDtypeStruct((N_DEV, D_MODEL // N_DEV, D_MODEL), BF16)
    wfo_shape = jax.ShapeDtypeStruct((N_DEV, FFN_OUT_SHARD, D_MODEL), BF16)
    part_of_owner = lambda ins, p: [r.at[p] for r in ins[:3]] + [ins[3].at[p // 2, _half_rows(p), :]]
    scatter_mix = _Ride([t.reshape(mix_shape.shape) for t in (d_wbr, d_wbl, d_wout)] + [d_wfo],
                        [mix_shape] * 3 + [wfo_shape], part_of_owner, _slot_of_sender)
    dproj, r_br, r_bl, r_out, r_fo = _retention_bwd(dzr, o, proj, states, cos2, sin2, dec, dproj, scatter_mix)
    d_win_far, r_in = _wgrad_w_in(u, dproj, me_arr)
    pack_early = jnp.concatenate([lru_small, dw_ffn_norm, red], axis=0)
    dh0, _, r_in_far, r_wa, r_wx, packs_early, packs_late = _in_proj_bwd(
        dproj, win_g, h0, mix_norm_w, dh1, d_win_far, [_by_owner(d_wa), _by_owner(d_wx)], pack_early)
    grad_x = dh0[CHUNK:]

    small_sum = jnp.concatenate([_sum_slots(packs_early), _sum_slots(packs_late)], axis=0)
    loss = (0.5 / D_MODEL) * jnp.sum(small_sum[PACK_SQ_ERR])

    def big_update(slots, w, m, v, more_slots=None):
        shape = w.shape
        w2, m2, v2 = (t.reshape(slots.shape[1:]) for t in (w, m, v))
        return [t.reshape(shape) for t in _adamw(slots, w2, m2, v2, more_slots)]

    res = {}
    res["w_in"] = big_update(r_in, w_in, m_w_in, v_w_in, r_in_far)
    res["w_branch_ret"] = big_update(r_br, w_branch_ret, m_w_branch_ret, v_w_branch_ret)
    res["w_branch_lru"] = big_update(r_bl, w_branch_lru, m_w_branch_lru, v_w_branch_lru)
    res["w_out"] = big_update(r_out, w_out, m_w_out, v_w_out)
    res["w_ffn_in"] = big_update(r_fi[:, :, :FFN_SHARD], w_ffn_in, m_w_ffn_in, v_w_ffn_in)
    res["w_ffn_out"] = big_update(r_fo, w_ffn_out, m_w_ffn_out, v_w_ffn_out)
    res["lru_wa"] = big_update(r_wa.reshape(N_DEV, LRU_BLOCKS * LRU_ROWS, LRU_BLOCK), lru_wa, m_lru_wa, v_lru_wa)
    res["lru_wx"] = big_update(r_wx.reshape(N_DEV, LRU_BLOCKS * LRU_ROWS, LRU_BLOCK), lru_wx, m_lru_wx, v_lru_wx)

    col = me * HEAD_DIM
    g_meta = lax.dynamic_slice(small_sum, (PACK_META, col), (N_META, HEAD_DIM))
    g_conv = lax.dynamic_slice(small_sum, (PACK_CONV_W, col), (8, HEAD_DIM))
    small_names = ["mix_norm_w", "conv_b", "lru_ba", "lru_bx", "lru_lambda", "ffn_norm_w", "final_norm_w"]
    small_rows = [PACK_MIX_NORM, PACK_CONV_B, PACK_BA, PACK_BX, PACK_LAM, PACK_FFN_NORM, PACK_FINAL_NORM]
    small_w = [mix_norm_w, conv_b, lru_ba, lru_bx, lru_lambda, ffn_norm_w, fw]
    small_m = [m_mix_norm_w, m_conv_b, m_lru_ba, m_lru_bx, m_lru_lambda, m_ffn_norm_w, m_final_norm_w.reshape(1, -1)]
    small_v = [v_mix_norm_w, v_conv_b, v_lru_ba, v_lru_bx, v_lru_lambda, v_ffn_norm_w, v_final_norm_w.reshape(1, -1)]

    def pack_small(vec_list, meta_t, conv_t):
        return jnp.concatenate([t.reshape(8, HEAD_DIM) for t in vec_list] + [meta_t, jnp.pad(conv_t[0], pad4)], axis=0)

    g_small = jnp.concatenate([small_sum[r].reshape(8, HEAD_DIM) for r in small_rows] + [g_meta, g_conv], axis=0)
    outs_small = _adamw(g_small[None], pack_small(small_w, meta_tokens, conv_w),
                        pack_small(small_m, m_meta_tokens, m_conv_w), pack_small(small_v, v_meta_tokens, v_conv_w))
    for idx, name in enumerate(small_names):
        shape = final_norm_w.shape if name == "final_norm_w" else (1, D_MODEL)
        res[name] = [t[8 * idx:8 * idx + 8].reshape(shape) for t in outs_small]
    res["meta_tokens"] = [t[56:72] for t in outs_small]
    res["conv_w"] = [t[72:76].reshape(1, 4, HEAD_DIM) for t in outs_small]

    order = ["meta_tokens", "mix_norm_w", "w_in", "conv_w", "conv_b", "lru_wa", "lru_ba", "lru_wx", "lru_bx",
             "lru_lambda", "w_branch_ret", "w_branch_lru", "w_out", "ffn_norm_w", "w_ffn_in", "w_ffn_out",
             "final_norm_w"]
    out = [loss, grad_x[None]]
    for kind in range(4):
        out += [res[name][kind] for name in order]
    return tuple(out)
```

```python
import jax
import jax.numpy as jnp
from jax import lax
from jax.experimental import pallas as pl
from jax.experimental.pallas import tpu as pltpu

F32 = jnp.float32
BF16 = jnp.bfloat16

D_MODEL = 1024
N_META = 16
CHUNK = 128
PAD_ROWS = CHUNK - N_META
HEADS = 8
HEAD_DIM = 128
ROPE_BASE = 10000.0
QK_SCALE = HEAD_DIM ** -0.5
LRU_BLOCKS = 4
LRU_BLOCK = 256
LRU_C = 8.0
FFN_HIDDEN = 2816
N_DEV = 8
FFN_SHARD = 2 * FFN_HIDDEN // N_DEV
FFN_GROUP = 768
FFN_GROUPS = 4
FFN_OUT_SHARD = FFN_HIDDEN // N_DEV
NORM_EPS = 1e-6

ADAM_LR = 0.001
ADAM_B1 = 0.9
ADAM_B2 = 0.999
ADAM_EPS = 1e-08
ADAM_WD = 0.01
ADAM_STEP = 10

VMEM_LIMIT = 56 * 1024 * 1024
MESH_ID = pl.DeviceIdType.MESH
ANY = pl.BlockSpec(memory_space=pl.ANY)


def _cparams(sem):
    return pltpu.CompilerParams(dimension_semantics=sem, vmem_limit_bytes=VMEM_LIMIT)


def _tile(rows, cap):
    t = cap - cap % 64
    while rows % t:
        t -= 64
    return t


def _dot(a, b):
    return jnp.dot(a, b, preferred_element_type=F32)


def _dot_nt(a, b):
    return lax.dot_general(a, b, (((1,), (1,)), ((), ())), preferred_element_type=F32)


def _dot_tn(a, b):
    return lax.dot_general(a, b, (((0,), (0,)), ((), ())), preferred_element_type=F32)


def _sigmoid(x):
    return 0.5 * jnp.tanh(0.5 * x) + 0.5


def _gelu_parts(x):
    k = 0.7978845608028654
    inner = k * (x + 0.044715 * x * x * x)
    t = jnp.tanh(inner)
    g = 0.5 * x * (1.0 + t)
    dg = 0.5 * (1.0 + t) + 0.5 * x * (1.0 - t * t) * k * (1.0 + 3.0 * 0.044715 * x * x)
    return g, dg


def _rot(x, cos2, sin2):
    return x * cos2 + pltpu.roll(x, HEAD_DIM // 2, 1) * sin2


def _rot_t(dx, cos2, sin2):
    return dx * cos2 - pltpu.roll(dx, HEAD_DIM // 2, 1) * sin2


def _rms_bwd(x, w, dy):
    rs = lax.rsqrt(jnp.mean(x * x, axis=-1, keepdims=True) + NORM_EPS)
    nh = x * rs
    dw = jnp.sum(dy * nh, axis=0, keepdims=True)
    dn = dy * w
    dx = rs * (dn - nh * jnp.mean(dn * nh, axis=-1, keepdims=True))
    return dx, dw


def _retention_consts():
    h = jnp.arange(HEADS, dtype=F32)
    log_g = jnp.log(1.0 - 2.0 ** (-5.0 - h))
    idx = jnp.arange(CHUNK, dtype=F32)
    diff = idx[:, None] - idx[None, :]
    intra = jnp.where(diff[None] >= 0, jnp.exp(jnp.maximum(diff, 0.0)[None] * log_g[:, None, None]), 0.0)
    q_decay = jnp.exp((idx + 1.0)[:, None] * log_g[None, :])
    k_decay = jnp.exp((CHUNK - 1.0 - idx)[:, None] * log_g[None, :])
    chunk_decay = jnp.exp(CHUNK * log_g)
    shape = (HEADS, CHUNK, CHUNK)
    qd = jnp.broadcast_to(q_decay.T[:, :, None], shape)
    kd = jnp.broadcast_to(k_decay.T[:, :, None], shape)
    cd = jnp.broadcast_to(chunk_decay[:, None, None], shape)
    return jnp.stack([intra, qd, kd, cd])


def _rope_tables(rows):
    pos = jnp.maximum(jnp.arange(rows) - PAD_ROWS, 0).astype(F32)
    inv_freq = ROPE_BASE ** (-jnp.arange(0, HEAD_DIM, 2, dtype=F32) / HEAD_DIM)
    ang = pos[:, None] * inv_freq[None, :]
    cos, sin = jnp.cos(ang), jnp.sin(ang)
    return jnp.concatenate([cos, cos], axis=1), jnp.concatenate([-sin, sin], axis=1)


def _my_index():
    return 4 * lax.axis_index("x") + 2 * lax.axis_index("y") + lax.axis_index("c")


def _peer(k):
    x, y, c = lax.axis_index("x"), lax.axis_index("y"), lax.axis_index("c")
    px = 1 - x if k & 4 else x
    py = 1 - y if k & 2 else y
    pc = 1 - c if k & 1 else c
    return (px, py, pc), 4 * px + 2 * py + pc


def _push_sems(n_arr):
    n_rem = (N_DEV - 1) * n_arr
    return [pltpu.SemaphoreType.DMA((n_rem,)), pltpu.SemaphoreType.DMA((n_rem,)), pltpu.SemaphoreType.DMA((n_arr,))]


class _Push:
    def __init__(self, send_part, land_slot, sems, n_arr):
        self.send_part, self.land_slot, self.n_arr = send_part, land_slot, n_arr
        self.send_sems, self.recv_sems, self.loc_sems = sems

    def _remote(self, k, a, src, dst, pos):
        idx = (k - 1) * self.n_arr + a
        return pltpu.make_async_remote_copy(src_ref=src, dst_ref=dst, send_sem=self.send_sems.at[idx],
                                            recv_sem=self.recv_sems.at[idx], device_id=pos, device_id_type=MESH_ID)

    def _outgoing(self):
        me = _my_index()
        land = self.land_slot(me)
        remote = []
        for k in range(1, N_DEV):
            pos, p = _peer(k)
            src = self.send_part(p)
            remote += [self._remote(k, a, src[a], land[a], pos) for a in range(self.n_arr)]
        own = self.send_part(me)
        local = [pltpu.make_async_copy(own[a], land[a], self.loc_sems.at[a]) for a in range(self.n_arr)]
        return remote, local

    def start(self):
        remote, local = self._outgoing()
        for cp in remote + local:
            cp.start()

    def wait_recv_from(self, k):
        own = self.send_part(_my_index())
        pos, p = _peer(k)
        land = self.land_slot(p)
        for a in range(self.n_arr):
            self._remote(k, a, own[a], land[a], pos).wait_recv()

    def wait_sends(self):
        remote, local = self._outgoing()
        for cp in remote:
            cp.wait_send()
        for cp in local:
            cp.wait()

    def wait(self):
        for k in range(1, N_DEV):
            self.wait_recv_from(k)
        self.wait_sends()


DIRECT = (1, 2, 4, 6)
RELAYED = (2, 4, 6)


def _gather_by_chip_sems(n_arr):
    direct, relayed = len(DIRECT) * n_arr, len(RELAYED) * n_arr
    return [pltpu.SemaphoreType.DMA((direct,)), pltpu.SemaphoreType.DMA((direct,)),
            pltpu.SemaphoreType.DMA((relayed,)), pltpu.SemaphoreType.DMA((relayed,)), pltpu.SemaphoreType.DMA((n_arr,))]


class _GatherByChip:
    def __init__(self, srcs, land_slot, sems, n_arr):
        self.srcs, self.land_slot, self.n_arr = srcs, land_slot, n_arr
        self.send_sems, self.recv_sems, self.relay_send_sems, self.relay_recv_sems, self.loc_sems = sems

    def _direct(self, k, a, slot):
        idx = DIRECT.index(k) * self.n_arr + a
        return pltpu.make_async_remote_copy(src_ref=self.srcs[a], dst_ref=self.land_slot(slot)[a],
                                            send_sem=self.send_sems.at[idx], recv_sem=self.recv_sems.at[idx],
                                            device_id=_peer(k)[0], device_id_type=MESH_ID)

    def _relay(self, q, a, slot):
        idx = RELAYED.index(q) * self.n_arr + a
        block = self.land_slot(slot)[a]
        return pltpu.make_async_remote_copy(src_ref=block, dst_ref=block, send_sem=self.relay_send_sems.at[idx],
                                            recv_sem=self.relay_recv_sems.at[idx], device_id=_peer(1)[0],
                                            device_id_type=MESH_ID)

    def _own(self, a):
        return pltpu.make_async_copy(self.srcs[a], self.land_slot(_my_index())[a], self.loc_sems.at[a])

    def start(self):
        me = _my_index()
        for k in DIRECT:
            for a in range(self.n_arr):
                self._direct(k, a, me).start()
        for a in range(self.n_arr):
            self._own(a).start()

    def relay(self):
        for q in RELAYED:
            p = _peer(q)[1]
            for a in range(self.n_arr):
                self._direct(q, a, p).wait_recv()
                self._relay(q, a, p).start()

    def wait(self):
        me = _my_index()
        for a in range(self.n_arr):
            self._direct(1, a, _peer(1)[1]).wait_recv()
        for q in RELAYED:
            for a in range(self.n_arr):
                self._relay(q, a, _peer(q + 1)[1]).wait_recv()
        for k in DIRECT:
            for a in range(self.n_arr):
                self._direct(k, a, me).wait_send()
        for q in RELAYED:
            for a in range(self.n_arr):
                self._relay(q, a, _peer(q)[1]).wait_send()
        for a in range(self.n_arr):
            self._own(a).wait()


class _Ride:
    def __init__(self, arrays, out_shapes, send_part, land_slot, zero_dsts=None, zero_shape=None, n_zero=0,
                 gather_by_chip=False):
        self.arrays, self.out_shapes = list(arrays), list(out_shapes)
        self.send_part, self.land_slot, self.n = send_part, land_slot, len(arrays)
        self.zero_dsts, self.zero_shape, self.n_zero = zero_dsts, zero_shape, n_zero
        self.gather_by_chip = gather_by_chip

    def specs(self):
        return [ANY] * self.n

    def scratch(self):
        extra = [pltpu.SemaphoreType.DMA((self.n_zero,)), pltpu.VMEM(self.zero_shape, BF16)] if self.n_zero else []
        sems = _gather_by_chip_sems(self.n) if self.gather_by_chip else _push_sems(self.n)
        return sems + extra

    def push(self, in_refs, out_refs, scratch):
        ride = self
        n_sems = 5 if self.gather_by_chip else 3
        land = lambda s: ride.land_slot(out_refs, s)
        if self.gather_by_chip:
            push = _GatherByChip(list(in_refs), land, tuple(scratch[:n_sems]), self.n)
        else:
            push = _Push(lambda p: ride.send_part(in_refs, p), land, tuple(scratch[:n_sems]), self.n)

        class Both:
            def _fills(self):
                if not ride.n_zero:
                    return []
                zsems, zbuf = scratch[n_sems], scratch[n_sems + 1]
                return [pltpu.make_async_copy(zbuf, dst, zsems.at[z]) for z, dst in enumerate(ride.zero_dsts(out_refs))]

            def start(self):
                push.start()
                if ride.n_zero:
                    scratch[n_sems + 1][...] = jnp.zeros(ride.zero_shape, BF16)
                for cp in self._fills():
                    cp.start()

            def relay(self):
                if ride.gather_by_chip:
                    push.relay()

            def wait(self):
                push.wait()
                for cp in self._fills():
                    cp.wait()

        return Both()


def _slot_of_sender(out_refs, s):
    return [r.at[s] for r in out_refs]


def _push_call(name, arrays, out_shapes, send_part, land_slot):
    n_arr = len(arrays)

    def body(*refs):
        ins, outs, sems = refs[:n_arr], refs[n_arr:2 * n_arr], refs[2 * n_arr:]
        push = _Push(lambda p: send_part(ins, p), lambda s: land_slot(outs, s), sems, n_arr)
        push.start()
        push.wait()

    return pl.pallas_call(
        body, name=name, in_specs=[ANY] * n_arr, out_specs=[ANY] * n_arr, out_shape=out_shapes,
        scratch_shapes=_push_sems(n_arr), compiler_params=pltpu.CompilerParams(has_side_effects=True),
    )(*arrays)


LRU_ROWS = LRU_BLOCK // N_DEV
FFN_PAD_ROWS = FFN_GROUP - 2 * FFN_OUT_SHARD


def _half_rows(d):
    return pl.ds(pl.multiple_of((d % 2) * FFN_OUT_SHARD, 16), FFN_OUT_SHARD)


MIXER_SHAPES = [
    jax.ShapeDtypeStruct((N_DEV, D_MODEL // N_DEV, D_MODEL), BF16),
    jax.ShapeDtypeStruct((N_DEV, D_MODEL // N_DEV, D_MODEL), BF16),
    jax.ShapeDtypeStruct((N_DEV, D_MODEL // N_DEV, D_MODEL), BF16),
    jax.ShapeDtypeStruct((N_DEV, LRU_BLOCKS, LRU_ROWS, LRU_BLOCK), BF16),
    jax.ShapeDtypeStruct((N_DEV, LRU_BLOCKS, LRU_ROWS, LRU_BLOCK), BF16),
]


def _by_owner(t):
    return t.reshape(LRU_BLOCKS, N_DEV, LRU_ROWS, LRU_BLOCK).transpose(1, 0, 2, 3)


def _from_owners(t):
    return t.transpose(1, 0, 2, 3).reshape(LRU_BLOCKS, LRU_BLOCK, LRU_BLOCK)


def _mixer_weights_ride(shards):
    return _Ride(shards, MIXER_SHAPES, lambda ins, p: list(ins), _slot_of_sender, gather_by_chip=True)


def _wfo_ride(shard):
    zero_dsts = lambda outs: [outs[0].at[g, pl.ds(2 * FFN_OUT_SHARD, FFN_PAD_ROWS), :] for g in range(FFN_GROUPS)]
    return _Ride([shard], [jax.ShapeDtypeStruct((FFN_GROUPS, FFN_GROUP, D_MODEL), BF16)], lambda ins, p: list(ins),
                 lambda outs, d: [outs[0].at[d // 2, _half_rows(d), :]], zero_dsts, (FFN_PAD_ROWS, D_MODEL), FFN_GROUPS,
                 gather_by_chip=True)


W_IN_USE_ORDER = (0, 1, 2, 4, 6, 3, 5, 7)


def _arrival_rank_to_relation(jj):
    k = W_IN_USE_ORDER[-1]
    for pos in reversed(range(N_DEV - 1)):
        k = jnp.where(jj == pos, W_IN_USE_ORDER[pos], k)
    return k


def _in_proj(h0, norm_w, win_shard, me_arr):
    rows = h0.shape[0]
    tm = _tile(rows, 1664)
    n_i = rows // tm

    direct, relayed = DIRECT, RELAYED

    def body(me_ref, h_ref, nw_ref, wsh_ref, proj_ref, u_ref, wing_ref, u_all, wbuf, copy_sem,
             send_sems, recv_sems, relay_send_sems, relay_recv_sems, own_sem):
        del me_ref
        jj, i = pl.program_id(0), pl.program_id(1)
        me = _my_index()
        sibling = _peer(1)[0]

        def direct_copy(k, slot):
            n = direct.index(k)
            return pltpu.make_async_remote_copy(src_ref=wsh_ref, dst_ref=wing_ref.at[slot], send_sem=send_sems.at[n],
                                                recv_sem=recv_sems.at[n], device_id=_peer(k)[0], device_id_type=MESH_ID)

        def relay_copy(q, slot):
            n = relayed.index(q)
            return pltpu.make_async_remote_copy(src_ref=wing_ref.at[slot], dst_ref=wing_ref.at[slot],
                                                send_sem=relay_send_sems.at[n], recv_sem=relay_recv_sems.at[n],
                                                device_id=sibling, device_id_type=MESH_ID)

        own_slot = pltpu.make_async_copy(wsh_ref, wing_ref.at[me], own_sem)

        @pl.when(jnp.logical_and(jj == 0, i == 0))
        def _():
            for k in direct:
                direct_copy(k, me).start()
            own_slot.start()
            own = pltpu.make_async_copy(wsh_ref, wbuf, copy_sem)
            own.start()
            own.wait()

        for k in range(1, N_DEV):
            rank = W_IN_USE_ORDER.index(k)

            @pl.when(jnp.logical_and(jj == rank, i == 0))
            def _(k=k):
                p = _peer(k)[1]
                if k in direct:
                    direct_copy(k, p).wait_recv()
                    if k in relayed:
                        relay_copy(k, p).start()
                else:
                    relay_copy(k - 1, p).wait_recv()
                landed = pltpu.make_async_copy(wing_ref.at[p], wbuf, copy_sem)
                landed.start()
                landed.wait()

        rows_i = pl.ds(pl.multiple_of(i * tm, tm), tm)

        @pl.when(jj == 0)
        def _():
            x = h_ref[...]
            rs = lax.rsqrt(jnp.mean(x * x, axis=-1, keepdims=True) + NORM_EPS)
            u = (x * rs * nw_ref[...]).astype(BF16)
            u_all[rows_i, :] = u
            u_ref[...] = u
        proj_ref[...] = _dot(u_all[rows_i, :], wbuf[...]).astype(BF16)

        @pl.when(jnp.logical_and(jj == N_DEV - 1, i == n_i - 1))
        def _():
            for k in direct:
                direct_copy(k, me).wait_send()
            for q in relayed:
                relay_copy(q, _peer(q)[1]).wait_send()
            own_slot.wait()

    first_pass = lambda jj, i: jnp.where(jj == 0, i, n_i - 1)
    grid_spec = pltpu.PrefetchScalarGridSpec(
        num_scalar_prefetch=1, grid=(N_DEV, n_i),
        in_specs=[pl.BlockSpec((tm, D_MODEL), lambda jj, i, me: (first_pass(jj, i), 0)),
                  pl.BlockSpec((1, D_MODEL), lambda jj, i, me: (0, 0)), ANY],
        out_specs=[pl.BlockSpec((tm, D_MODEL), lambda jj, i, me: (i, me[0] ^ _arrival_rank_to_relation(jj))),
                   pl.BlockSpec((tm, D_MODEL), lambda jj, i, me: (first_pass(jj, i), 0)), ANY],
        scratch_shapes=[pltpu.VMEM((rows, D_MODEL), BF16), pltpu.VMEM((D_MODEL, D_MODEL), BF16),
                        pltpu.SemaphoreType.DMA(()),
                        pltpu.SemaphoreType.DMA((len(direct),)), pltpu.SemaphoreType.DMA((len(direct),)),
                        pltpu.SemaphoreType.DMA((len(relayed),)), pltpu.SemaphoreType.DMA((len(relayed),)),
                        pltpu.SemaphoreType.DMA(())])
    return pl.pallas_call(
        body, name="in_proj", grid_spec=grid_spec,
        out_shape=[jax.ShapeDtypeStruct((rows, N_DEV * D_MODEL), BF16),
                   jax.ShapeDtypeStruct((rows, D_MODEL), BF16),
                   jax.ShapeDtypeStruct((N_DEV, D_MODEL, D_MODEL), BF16)],
        compiler_params=pltpu.CompilerParams(dimension_semantics=("arbitrary", "arbitrary"),
                                             vmem_limit_bytes=VMEM_LIMIT, has_side_effects=True),
    )(me_arr, h0, norm_w, win_shard)


def _seg_spec(rows_per_block, seg):
    return pl.BlockSpec((rows_per_block, D_MODEL), lambda n, seg=seg: (n, seg))


def _chunks_per_step(n_chunks):
    return next(c for c in (5, 3, 2, 1) if n_chunks % c == 0)


def _retention_fwd(proj, cos2, sin2, dec, ride):
    rows = proj.shape[0]
    n_chunks = rows // CHUNK
    per_step = _chunks_per_step(n_chunks)
    n_steps = n_chunks // per_step
    tm = per_step * CHUNK
    n_r = ride.n

    def body(q_ref, k_ref, v_ref, g_ref, c_ref, s_ref, dec_ref, *refs):
        o_ref, zr_ref, st_ref = refs[n_r:n_r + 3]
        state = refs[2 * n_r + 3]
        push = ride.push(refs[:n_r], refs[n_r + 3:2 * n_r + 3], refs[2 * n_r + 4:])

        @pl.when(pl.program_id(0) == 0)
        def _():
            push.start()
            state[...] = jnp.zeros_like(state)

        for h in range(HEADS):
            sl = slice(HEAD_DIM * h, HEAD_DIM * (h + 1))
            st = state[h]
            for c in range(per_step):
                rw = slice(CHUNK * c, CHUNK * (c + 1))
                cos_t, sin_t = c_ref[rw, :], s_ref[rw, :]
                qh = _rot(q_ref[rw, sl].astype(F32), cos_t, sin_t)
                kh = _rot(k_ref[rw, sl].astype(F32), cos_t, sin_t) * QK_SCALE
                qb, kb, vb = qh.astype(BF16), kh.astype(BF16), v_ref[rw, sl]
                s = _dot_nt(qb, kb) * dec_ref[0, h]
                st_ref[c, h] = st
                o = _dot(s.astype(BF16), vb) + _dot(qb, st.astype(BF16)) * dec_ref[1, h]
                st = st * dec_ref[3, h] + _dot_tn((kh * dec_ref[2, h]).astype(BF16), vb)
                o_ref[rw, sl] = o.astype(BF16)
                r = lax.rsqrt(jnp.mean(o * o, axis=-1, keepdims=True) + NORM_EPS)
                g = g_ref[rw, sl].astype(F32)
                zr_ref[rw, sl] = (g * _sigmoid(g) * (o * r)).astype(BF16)
            state[h] = st

        @pl.when(pl.program_id(0) == n_steps // 2)
        def _():
            push.relay()

        @pl.when(pl.program_id(0) == n_steps - 1)
        def _():
            push.wait()

    tab = pl.BlockSpec((tm, HEAD_DIM), lambda n: (n, 0))
    return pl.pallas_call(
        body, name="retention_fwd", grid=(n_steps,),
        in_specs=[_seg_spec(tm, 0), _seg_spec(tm, 1), _seg_spec(tm, 2), _seg_spec(tm, 3), tab, tab,
                  pl.BlockSpec((4, HEADS, CHUNK, CHUNK), lambda n: (0, 0, 0, 0))] + ride.specs(),
        out_specs=[pl.BlockSpec((tm, D_MODEL), lambda n: (n, 0)),
                   pl.BlockSpec((tm, D_MODEL), lambda n: (n, 0)),
                   pl.BlockSpec((per_step, HEADS, HEAD_DIM, HEAD_DIM), lambda n: (n, 0, 0, 0))] + ride.specs(),
        out_shape=[jax.ShapeDtypeStruct((rows, D_MODEL), BF16),
                   jax.ShapeDtypeStruct((rows, D_MODEL), BF16),
                   jax.ShapeDtypeStruct((n_chunks, HEADS, HEAD_DIM, HEAD_DIM), F32)] + ride.out_shapes,
        scratch_shapes=[pltpu.VMEM((HEADS, HEAD_DIM, HEAD_DIM), F32)] + ride.scratch(),
        compiler_params=pltpu.CompilerParams(dimension_semantics=("arbitrary",), vmem_limit_bytes=VMEM_LIMIT,
                                             has_side_effects=True),
    )(proj, proj, proj, proj, cos2, sin2, dec, *ride.arrays)


def _lru_gates(c, ba, bx, wa_ref, wx_ref):
    pre_r, pre_i = [], []
    for g in range(LRU_BLOCKS):
        cg = c[:, LRU_BLOCK * g:LRU_BLOCK * (g + 1)].astype(BF16)
        pre_r.append(_dot(cg, wa_ref[g]))
        pre_i.append(_dot(cg, wx_ref[g]))
    return _sigmoid(jnp.concatenate(pre_r, axis=1) + ba), _sigmoid(jnp.concatenate(pre_i, axis=1) + bx)


def _lru_decay(r, lam):
    sp = jnp.maximum(-lam, 0.0) + jnp.log(1.0 + jnp.exp(-jnp.abs(lam)))
    log_a = -LRU_C * r * sp
    a = jnp.exp(log_a)
    one_minus_a2 = -jnp.tanh(log_a) * (a * a + 1.0)
    inv_mult = lax.rsqrt(jnp.maximum(one_minus_a2, 1e-30))
    return a, one_minus_a2 * inv_mult, inv_mult, sp


def _conv_taps(xbuf, tm, cw_ref, cb_ref):
    c = cb_ref[...] + cw_ref[3:4, :] * xbuf[8:8 + tm, :]
    for back in (1, 2, 3):
        c = c + cw_ref[3 - back:4 - back, :] * xbuf[8 - back:8 - back + tm, :]
    return c


def _lru_fwd(proj, conv_w, conv_b, ba, bx, lam, wa_g, wx_g, ride):
    rows = proj.shape[0]
    tm = _tile(rows, 320)
    n_t = rows // tm
    n_r = ride.n

    def body(x_ref, gt_ref, cw_ref, cb_ref, ba_ref, bx_ref, lam_ref, wa_ref, wx_ref, *refs):
        hs_ref, zl_ref, cri_ref = refs[n_r:n_r + 3]
        xbuf, abuf, ubuf, hcar = refs[2 * n_r + 3:2 * n_r + 7]
        push = ride.push(refs[:n_r], refs[n_r + 3:2 * n_r + 3], refs[2 * n_r + 7:])
        i = pl.program_id(0)

        @pl.when(i == 0)
        def _():
            push.start()
            xbuf[0:8, :] = jnp.zeros((8, D_MODEL), F32)
            hcar[...] = jnp.zeros_like(hcar)

        xbuf[8:8 + tm, :] = x_ref[...].astype(F32)
        c = _conv_taps(xbuf, tm, cw_ref, cb_ref)
        xbuf[0:8, :] = xbuf[tm:tm + 8, :]
        r, ig = _lru_gates(c, ba_ref[...], bx_ref[...], wa_ref, wx_ref)
        a, mult, _, _ = _lru_decay(r, lam_ref[...])
        cri_ref[0] = c.astype(BF16)
        cri_ref[1] = r.astype(BF16)
        cri_ref[2] = ig.astype(BF16)
        row = i * tm + lax.broadcasted_iota(jnp.int32, (tm, 1), 0)
        abuf[...] = a
        ubuf[...] = jnp.where(row >= PAD_ROWS, mult * (ig * c), 0.0)

        sub = lax.broadcasted_iota(jnp.int32, (8, D_MODEL), 0)

        def block(b, carry):
            off = pl.multiple_of(b * 8, 8)
            av, uv = abuf[pl.ds(off, 8), :], ubuf[pl.ds(off, 8), :]
            for s in (1, 2, 4):
                us = jnp.where(sub >= s, pltpu.roll(uv, s, 0), 0.0)
                as_ = jnp.where(sub >= s, pltpu.roll(av, s, 0), 1.0)
                uv = uv + av * us
                av = av * as_
            hv = uv + av * carry
            ubuf[pl.ds(off, 8), :] = hv
            return hv[7:8, :]

        hcar[...] = lax.fori_loop(0, tm // 8, block, hcar[...])
        gl, _ = _gelu_parts(gt_ref[...].astype(F32))
        hs = ubuf[...]
        hs_ref[...] = hs.astype(BF16)
        zl_ref[...] = (gl * hs).astype(BF16)

        @pl.when(i == n_t // 2)
        def _():
            push.relay()

        @pl.when(i == n_t - 1)
        def _():
            push.wait()

    vec = pl.BlockSpec((1, D_MODEL), lambda i: (0, 0))
    mat = pl.BlockSpec((LRU_BLOCKS, LRU_BLOCK, LRU_BLOCK), lambda i: (0, 0, 0))
    row = pl.BlockSpec((tm, D_MODEL), lambda i: (i, 0))
    return pl.pallas_call(
        body, name="lru_fwd", grid=(n_t,),
        in_specs=[_seg_spec(tm, 4), _seg_spec(tm, 5), pl.BlockSpec((4, D_MODEL), lambda i: (0, 0)),
                  vec, vec, vec, vec, mat, mat] + ride.specs(),
        out_specs=[row, row, pl.BlockSpec((3, tm, D_MODEL), lambda i: (0, i, 0))] + ride.specs(),
        out_shape=[jax.ShapeDtypeStruct((rows, D_MODEL), BF16)] * 2
        + [jax.ShapeDtypeStruct((3, rows, D_MODEL), BF16)] + ride.out_shapes,
        scratch_shapes=[pltpu.VMEM((tm + 8, D_MODEL), F32), pltpu.VMEM((tm, D_MODEL), F32),
                        pltpu.VMEM((tm, D_MODEL), F32), pltpu.VMEM((1, D_MODEL), F32)] + ride.scratch(),
        compiler_params=pltpu.CompilerParams(dimension_semantics=("arbitrary",), vmem_limit_bytes=VMEM_LIMIT,
                                             has_side_effects=True),
    )(proj, proj, conv_w, conv_b, ba, bx, lam, wa_g, wx_g, *ride.arrays)


def _mix_fwd(zr, zl, proj, h0, wbr, wbl, wout, ride):
    rows = h0.shape[0]
    tm = _tile(rows, 640)
    n_t = rows // tm
    n_r = ride.n

    def body(zr_ref, zl_ref, ga_ref, gb_ref, h0_ref, wbr_ref, wbl_ref, wo_ref, *refs):
        h1_ref, yr_ref, yl_ref, mx_ref = refs[n_r:n_r + 4]
        push = ride.push(refs[:n_r], refs[n_r + 4:2 * n_r + 4], refs[2 * n_r + 4:])

        @pl.when(pl.program_id(0) == 0)
        def _():
            push.start()

        yr = _dot(zr_ref[...], wbr_ref[...])
        yl = _dot(zl_ref[...], wbl_ref[...])
        mixed = (_sigmoid(ga_ref[...].astype(F32)) * yr + _sigmoid(gb_ref[...].astype(F32)) * yl).astype(BF16)
        yr_ref[...] = yr.astype(BF16)
        yl_ref[...] = yl.astype(BF16)
        mx_ref[...] = mixed
        h1_ref[...] = h0_ref[...] + _dot(mixed, wo_ref[...])

        @pl.when(pl.program_id(0) == n_t // 2)
        def _():
            push.relay()

        @pl.when(pl.program_id(0) == n_t - 1)
        def _():
            push.wait()

    row = pl.BlockSpec((tm, D_MODEL), lambda i: (i, 0))
    wsp = pl.BlockSpec((D_MODEL, D_MODEL), lambda i: (0, 0))
    return pl.pallas_call(
        body, name="mix_fwd", grid=(n_t,),
        in_specs=[row, row, _seg_spec(tm, 6), _seg_spec(tm, 7), row, wsp, wsp, wsp] + ride.specs(),
        out_specs=[row, row, row, row] + ride.specs(),
        out_shape=[jax.ShapeDtypeStruct((rows, D_MODEL), F32)] + [jax.ShapeDtypeStruct((rows, D_MODEL), BF16)] * 3
        + ride.out_shapes,
        scratch_shapes=ride.scratch(),
        compiler_params=pltpu.CompilerParams(dimension_semantics=("arbitrary",), vmem_limit_bytes=VMEM_LIMIT,
                                             has_side_effects=True),
    )(zr, zl, proj, proj, h0, wbr, wbl, wout, *ride.arrays)


def _ffn_fwd_loss(h1, norm_w, wfi_g, wfo_g, final_w, target):
    rows = h1.shape[0]
    tm = _tile(rows, 320)
    piece = 64
    n_piece = tm // piece

    def body(h1_ref, nw_ref, wfi_ref, wfo_ref, fw_ref, *refs):
        t_refs = refs[:n_piece]
        u2_ref, g_ref, up_ref, act_ref, dh2_ref, red_ref = refs[n_piece:]
        i = pl.program_id(0)

        @pl.when(i == 0)
        def _():
            red_ref[...] = jnp.zeros_like(red_ref)

        x = h1_ref[...]
        rs = lax.rsqrt(jnp.mean(x * x, axis=-1, keepdims=True) + NORM_EPS)
        u2 = (x * rs * nw_ref[...]).astype(BF16)
        u2_ref[...] = u2
        ffn = None
        for d in range(FFN_GROUPS):
            cols = slice(FFN_GROUP * d, FFN_GROUP * (d + 1))
            g = _dot(u2, wfi_ref[d])
            up = _dot(u2, wfi_ref[d + FFN_GROUPS])
            act = (g * _sigmoid(g) * up).astype(BF16)
            g_ref[:, cols] = g.astype(BF16)
            up_ref[:, cols] = up.astype(BF16)
            act_ref[:, cols] = act
            part = _dot(act, wfo_ref[d])
            ffn = part if ffn is None else ffn + part

        h2 = h1_ref[...] + ffn
        rs = lax.rsqrt(jnp.mean(h2 * h2, axis=-1, keepdims=True) + NORM_EPS)
        nh = h2 * rs
        fw = fw_ref[...]
        row = i * tm + lax.broadcasted_iota(jnp.int32, (tm, 1), 0)
        tgt = jnp.concatenate([t[...] for t in t_refs], axis=0)
        diff = jnp.where(row >= CHUNK, nh * fw - tgt, 0.0)
        dy = diff * (1.0 / D_MODEL)
        red_ref[0:1, :] += jnp.sum(diff * diff, axis=0, keepdims=True)
        red_ref[1:2, :] += jnp.sum(dy * nh, axis=0, keepdims=True)
        dn = dy * fw
        dh2_ref[...] = rs * (dn - nh * jnp.mean(dn * nh, axis=-1, keepdims=True))

    row = pl.BlockSpec((tm, D_MODEL), lambda i: (i, 0))
    vec = pl.BlockSpec((1, D_MODEL), lambda i: (0, 0))
    hid = pl.BlockSpec((tm, FFN_GROUPS * FFN_GROUP), lambda i: (i, 0))
    hid_shape = jax.ShapeDtypeStruct((rows, FFN_GROUPS * FFN_GROUP), BF16)
    resident = dict(pipeline_mode=pl.Buffered(1))
    head_pieces = CHUNK // piece
    t_specs = [pl.BlockSpec((piece, D_MODEL), lambda i, k=k: (jnp.maximum(i * n_piece + k - head_pieces, 0), 0))
               for k in range(n_piece)]
    return pl.pallas_call(
        body, name="ffn_fwd_loss", grid=(rows // tm,),
        in_specs=[row, vec,
                  pl.BlockSpec((2 * FFN_GROUPS, D_MODEL, FFN_GROUP), lambda i: (0, 0, 0), **resident),
                  pl.BlockSpec((FFN_GROUPS, FFN_GROUP, D_MODEL), lambda i: (0, 0, 0), **resident),
                  vec] + t_specs,
        out_specs=[row, hid, hid, hid, row, pl.BlockSpec((8, D_MODEL), lambda i: (0, 0))],
        out_shape=[jax.ShapeDtypeStruct((rows, D_MODEL), BF16), hid_shape, hid_shape, hid_shape,
                   jax.ShapeDtypeStruct((rows, D_MODEL), F32), jax.ShapeDtypeStruct((8, D_MODEL), F32)],
        compiler_params=_cparams(("arbitrary",)),
    )(h1, norm_w, wfi_g, wfo_g, final_w, *([target] * n_piece))


def _wgrad(a, b, ka, tn, out_dtype, b_halves=False, row_cap=1664):
    rows = a.shape[0]
    na = a.shape[1] // ka
    tm = _tile(rows, row_cap)
    nm = rows // tm
    if b_halves:
        per_half = b.shape[2] // tn
        nb = 2 * per_half
        b_spec = pl.BlockSpec((None, tm, tn), lambda p, q, m: (q // per_half, m, q % per_half))
    else:
        nb = b.shape[1] // tn
        b_spec = pl.BlockSpec((tm, tn), lambda p, q, m: (m, q))

    def body(a_ref, b_ref, o_ref, acc):
        m = pl.program_id(2)

        @pl.when(m == 0)
        def _():
            acc[...] = jnp.zeros_like(acc)

        acc[...] += _dot_tn(a_ref[...].astype(BF16), b_ref[...].astype(BF16))

        @pl.when(m == nm - 1)
        def _():
            o_ref[...] = acc[...].astype(out_dtype)

    return pl.pallas_call(
        body, name="wgrad", grid=(na, nb, nm),
        in_specs=[pl.BlockSpec((tm, ka), lambda p, q, m: (m, p)), b_spec],
        out_specs=pl.BlockSpec((None, None, ka, tn), lambda p, q, m: (p, q, 0, 0)),
        out_shape=jax.ShapeDtypeStruct((na, nb, ka, tn), out_dtype),
        scratch_shapes=[pltpu.VMEM((ka, tn), F32)],
        compiler_params=_cparams(("parallel", "parallel", "arbitrary")),
    )(a, b)


WIN_NEAR = (2, 4, 3, 5, 1)
WIN_FAR = (6, 7)
WIN_ORDER = WIN_FAR + WIN_NEAR + (0,)


def _w_in_relation_at(jj):
    k = 0
    for pos in reversed(range(len(WIN_ORDER) - 1)):
        k = jnp.where(jj == pos, WIN_ORDER[pos], k)
    return k


def _wgrad_w_in(u, dproj, me_arr):
    rows = u.shape[0]
    tm = _tile(rows, 1664)
    nm = rows // tm
    n_near = len(WIN_NEAR)

    def body(me_ref, a_ref, b_ref, far_ref, land_ref, acc, sbuf, send_sems, recv_sems, own_sem):
        del me_ref
        jj, m = pl.program_id(0), pl.program_id(1)

        def near_copy(n):
            k = WIN_NEAR[n]
            return pltpu.make_async_remote_copy(src_ref=sbuf.at[n], dst_ref=land_ref.at[k], send_sem=send_sems.at[n],
                                                recv_sem=recv_sems.at[n], device_id=_peer(k)[0], device_id_type=MESH_ID)

        own_copy = pltpu.make_async_copy(sbuf.at[n_near], land_ref.at[0], own_sem)

        @pl.when(m == 0)
        def _():
            acc[...] = jnp.zeros_like(acc)

        acc[...] += _dot_tn(a_ref[...], b_ref[...])

        for pos, k in enumerate(WIN_ORDER):
            @pl.when(jnp.logical_and(jj == pos, m == nm - 1))
            def _(k=k):
                block = acc[...].astype(BF16)
                if k in WIN_FAR:
                    far_ref[...] = block
                elif k == 0:
                    sbuf[n_near] = block
                    own_copy.start()
                else:
                    sbuf[WIN_NEAR.index(k)] = block
                    near_copy(WIN_NEAR.index(k)).start()

        @pl.when(jnp.logical_and(jj == N_DEV - 1, m == nm - 1))
        def _():
            for n in range(n_near):
                near_copy(n).wait_recv()
            for n in range(n_near):
                near_copy(n).wait_send()
            own_copy.wait()

    grid_spec = pltpu.PrefetchScalarGridSpec(
        num_scalar_prefetch=1, grid=(N_DEV, nm),
        in_specs=[pl.BlockSpec((tm, D_MODEL), lambda jj, m, me: (m, 0)),
                  pl.BlockSpec((tm, D_MODEL), lambda jj, m, me: (m, me[0] ^ _w_in_relation_at(jj)))],
        out_specs=[pl.BlockSpec((None, D_MODEL, D_MODEL), lambda jj, m, me: (jnp.minimum(jj, len(WIN_FAR) - 1), 0, 0)),
                   ANY],
        scratch_shapes=[pltpu.VMEM((D_MODEL, D_MODEL), F32), pltpu.VMEM((n_near + 1, D_MODEL, D_MODEL), BF16),
                        pltpu.SemaphoreType.DMA((n_near,)), pltpu.SemaphoreType.DMA((n_near,)),
                        pltpu.SemaphoreType.DMA(())])
    return pl.pallas_call(
        body, name="wgrad_w_in", grid_spec=grid_spec,
        out_shape=[jax.ShapeDtypeStruct((len(WIN_FAR), D_MODEL, D_MODEL), BF16),
                   jax.ShapeDtypeStruct((n_near + 1, D_MODEL, D_MODEL), BF16)],
        compiler_params=pltpu.CompilerParams(dimension_semantics=("arbitrary", "arbitrary"),
                                             vmem_limit_bytes=VMEM_LIMIT, has_side_effects=True),
    )(me_arr, u, dproj)


def _ffn_bwd(dh2, g, up, h1, norm_w, wfi_g, wfo_g):
    rows = h1.shape[0]
    tm = _tile(rows, 320)

    def body(dh2_ref, g_ref, up_ref, h1_ref, nw_ref, wfi_ref, wfo_ref, dgu_ref, dh1_ref, dw_ref):
        @pl.when(pl.program_id(0) == 0)
        def _():
            dw_ref[...] = jnp.zeros_like(dw_ref)

        dh2_b = dh2_ref[...].astype(BF16)
        du2 = None
        for d in range(FFN_GROUPS):
            cols = slice(FFN_GROUP * d, FFN_GROUP * (d + 1))
            dact = _dot_nt(dh2_b, wfo_ref[d])
            gv, uv = g_ref[:, cols].astype(F32), up_ref[:, cols].astype(F32)
            sg = _sigmoid(gv)
            dg = (dact * uv * (sg * (1.0 + gv * (1.0 - sg)))).astype(BF16)
            dup = (dact * (gv * sg)).astype(BF16)
            dgu_ref[0, :, cols] = dg
            dgu_ref[1, :, cols] = dup
            part = _dot_nt(dg, wfi_ref[d]) + _dot_nt(dup, wfi_ref[d + FFN_GROUPS])
            du2 = part if du2 is None else du2 + part
        dx, dw = _rms_bwd(h1_ref[...], nw_ref[...], du2)
        dw_ref[0:1, :] += dw
        dh1_ref[...] = dh2_ref[...] + dx

    row = pl.BlockSpec((tm, D_MODEL), lambda i: (i, 0))
    vec = pl.BlockSpec((1, D_MODEL), lambda i: (0, 0))
    hid = pl.BlockSpec((tm, FFN_GROUPS * FFN_GROUP), lambda i: (i, 0))
    resident = dict(pipeline_mode=pl.Buffered(1))
    return pl.pallas_call(
        body, name="ffn_bwd", grid=(rows // tm,),
        in_specs=[row, hid, hid, row, vec,
                  pl.BlockSpec((2 * FFN_GROUPS, D_MODEL, FFN_GROUP), lambda i: (0, 0, 0), **resident),
                  pl.BlockSpec((FFN_GROUPS, FFN_GROUP, D_MODEL), lambda i: (0, 0, 0), **resident)],
        out_specs=[pl.BlockSpec((2, tm, FFN_GROUPS * FFN_GROUP), lambda i: (0, i, 0)), row,
                   pl.BlockSpec((8, D_MODEL), lambda i: (0, 0))],
        out_shape=[jax.ShapeDtypeStruct((2, rows, FFN_GROUPS * FFN_GROUP), BF16),
                   jax.ShapeDtypeStruct((rows, D_MODEL), F32), jax.ShapeDtypeStruct((8, D_MODEL), F32)],
        compiler_params=_cparams(("arbitrary",)),
    )(dh2, g, up, h1, norm_w, wfi_g, wfo_g)


def _mix_bwd(dh1, yr, yl, proj, wbr, wbl, wout):
    rows = dh1.shape[0]
    tm = _tile(rows, 640)

    def body(dh1_ref, yr_ref, yl_ref, ga_ref, gb_ref, wbr_ref, wbl_ref, wo_ref,
             dyr_ref, dyl_ref, dseg_ref, dzr_ref, dzl_ref):
        dmix = _dot_nt(dh1_ref[...].astype(BF16), wo_ref[...])
        sa, sb = _sigmoid(ga_ref[...].astype(F32)), _sigmoid(gb_ref[...].astype(F32))
        dyr = (dmix * sa).astype(BF16)
        dyl = (dmix * sb).astype(BF16)
        dyr_ref[...] = dyr
        dyl_ref[...] = dyl
        dseg_ref[:, 0:D_MODEL] = (dmix * yr_ref[...].astype(F32) * (sa * (1.0 - sa))).astype(BF16)
        dseg_ref[:, D_MODEL:2 * D_MODEL] = (dmix * yl_ref[...].astype(F32) * (sb * (1.0 - sb))).astype(BF16)
        dzr_ref[...] = _dot_nt(dyr, wbr_ref[...]).astype(BF16)
        dzl_ref[...] = _dot_nt(dyl, wbl_ref[...]).astype(BF16)

    row = pl.BlockSpec((tm, D_MODEL), lambda i: (i, 0))
    wsp = pl.BlockSpec((D_MODEL, D_MODEL), lambda i: (0, 0))
    bshape = jax.ShapeDtypeStruct((rows, D_MODEL), BF16)
    return pl.pallas_call(
        body, name="mix_bwd", grid=(rows // tm,),
        in_specs=[row, row, row, _seg_spec(tm, 6), _seg_spec(tm, 7), wsp, wsp, wsp],
        out_specs=[row, row, pl.BlockSpec((tm, 2 * D_MODEL), lambda i: (i, 3)), row, row],
        out_shape=[bshape, bshape, jax.ShapeDtypeStruct((rows, N_DEV * D_MODEL), BF16), bshape, bshape],
        compiler_params=_cparams(("parallel",)),
    )(dh1, yr, yl, proj, proj, wbr, wbl, wout)


S1_SHAPES = [jax.ShapeDtypeStruct((N_DEV, D_MODEL, FFN_GROUP), BF16)]


def _s1_parts(ins, p):
    return [ins[0].at[p]]


def _lru_bwd(dzl, hs, cri, proj, dproj, conv_w, lam, wa_g, wx_g, s1_grads):
    rows = dzl.shape[0]
    tm = _tile(rows, 640)
    nt = rows // tm
    t8 = tm // 8
    n_s1 = len(s1_grads)

    def body(dzl_ref, hs_ref, hsp_ref, cri_ref, x_ref, gt_ref, cw_ref, lam_ref, wa_ref, wx_ref, dproj_in, *refs):
        del dproj_in
        s1_refs = refs[:n_s1]
        dseg_ref, dwa_ref, dwx_ref, sm_ref = refs[n_s1:n_s1 + 4]
        land_refs = refs[n_s1 + 4:2 * n_s1 + 4]
        (xbuf, abuf, mbuf, ibuf, dbuf, dcbuf, dpr_s, dpi_s, sums, conv_sums, anext, dhcar,
         send_sems, recv_sems, loc_sems) = refs[2 * n_s1 + 4:]
        step = pl.program_id(0)
        i = nt - 1 - step
        push = _Push(lambda p: _s1_parts(s1_refs, p), lambda s: [r.at[s] for r in land_refs],
                     (send_sems, recv_sems, loc_sems), n_s1)

        @pl.when(step == 0)
        def _():
            push.start()
            dwa_ref[...] = jnp.zeros_like(dwa_ref)
            dwx_ref[...] = jnp.zeros_like(dwx_ref)
            sm_ref[...] = jnp.zeros_like(sm_ref)
            anext[...] = jnp.zeros_like(anext)
            dhcar[...] = jnp.zeros_like(dhcar)
            dcbuf[tm:tm + 8, :] = jnp.zeros((8, D_MODEL), F32)

        slab, lanes = 16, 256
        lam_v = lam_ref[...]
        xbuf[0:8, :] = jnp.where(i == 0, 0.0, hsp_ref[8:16, :].astype(F32))
        sums[...] = jnp.zeros_like(sums)

        def before_scan(k, carry):
            rw = pl.ds(pl.multiple_of(k * slab, slab), slab)
            for q in range(D_MODEL // lanes):
                ln = slice(lanes * q, lanes * (q + 1))
                a, mult, inv_mult, _ = _lru_decay(cri_ref[1, rw, ln].astype(F32), lam_v[:, ln])
                abuf[rw, ln] = a
                mbuf[rw, ln] = mult
                ibuf[rw, ln] = inv_mult
                gl, dgl = _gelu_parts(gt_ref[rw, ln].astype(F32))
                dzl_v = dzl_ref[rw, ln].astype(F32)
                hs_v = hs_ref[rw, ln].astype(F32)
                dseg_ref[rw, D_MODEL + lanes * q:D_MODEL + lanes * (q + 1)] = (dzl_v * hs_v * dgl).astype(BF16)
                dbuf[rw, ln] = dzl_v * gl
                xbuf[pl.ds(pl.multiple_of(k * slab + 8, 8), slab), ln] = hs_v
            return carry

        lax.fori_loop(0, tm // slab, before_scan, 0)

        sub = lax.broadcasted_iota(jnp.int32, (8, D_MODEL), 0)

        def block(k, carry):
            dh_next, a_next = carry
            off = pl.multiple_of((t8 - 1 - k) * 8, 8)
            a_blk = abuf[pl.ds(off, 8), :]
            av = jnp.where(sub < 7, pltpu.roll(a_blk, 7, 0), a_next)
            uv = dbuf[pl.ds(off, 8), :]
            for s in (1, 2, 4):
                us = jnp.where(sub < 8 - s, pltpu.roll(uv, 8 - s, 0), 0.0)
                as_ = jnp.where(sub < 8 - s, pltpu.roll(av, 8 - s, 0), 1.0)
                uv = uv + av * us
                av = av * as_
            hv = uv + av * dh_next
            dbuf[pl.ds(off, 8), :] = hv
            return hv[0:1, :], a_blk[0:1, :]

        dh_first, a_first = lax.fori_loop(0, t8, block, (dhcar[...], anext[...]))
        dhcar[...] = dh_first
        anext[...] = a_first

        sp = jnp.maximum(-lam_v, 0.0) + jnp.log(1.0 + jnp.exp(-jnp.abs(lam_v)))
        sub_q = lax.broadcasted_iota(jnp.int32, (8, lanes), 0)
        row16 = lax.broadcasted_iota(jnp.int32, (slab, 1), 0)

        def after_scan(k, carry):
            off = pl.multiple_of(k * slab, slab)
            rw = pl.ds(off, slab)
            for q in range(D_MODEL // lanes):
                ln = slice(lanes * q, lanes * (q + 1))
                before = xbuf[pl.ds(off, 8), ln]
                h_lo = xbuf[pl.ds(pl.multiple_of(off + 8, 8), 8), ln]
                h_hi = xbuf[pl.ds(pl.multiple_of(off + 16, 8), 8), ln]
                hprev = jnp.concatenate([jnp.where(sub_q >= 1, pltpu.roll(h_lo, 1, 0), before[7:8, :]),
                                         jnp.where(sub_q >= 1, pltpu.roll(h_hi, 1, 0), h_lo[7:8, :])], axis=0)
                c, r, ig = (cri_ref[n, rw, ln].astype(F32) for n in range(3))
                a, mult, inv_mult = abuf[rw, ln], mbuf[rw, ln], ibuf[rw, ln]
                dh = dbuf[rw, ln]
                duu = jnp.where(i * tm + off + row16 >= PAD_ROWS, dh, 0.0)
                t_mult = duu * mult
                dlog_a = dh * hprev * a - duu * ig * c * (a * a) * inv_mult
                dpr = dlog_a * (-LRU_C * sp[:, ln]) * r * (1.0 - r)
                dpi = t_mult * c * ig * (1.0 - ig)
                dpr_s[rw, ln] = dpr.astype(BF16)
                dpi_s[rw, ln] = dpi.astype(BF16)
                dcbuf[rw, ln] = t_mult * ig
                sums[0, :, ln] += dlog_a * r
                sums[1, :, ln] += dpr
                sums[2, :, ln] += dpi
            return carry

        lax.fori_loop(0, tm // slab, after_scan, 0)

        dcs = []
        for g in range(LRU_BLOCKS):
            sl = slice(LRU_BLOCK * g, LRU_BLOCK * (g + 1))
            cg = cri_ref[0, :, sl]
            dpr_b, dpi_b = dpr_s[:, sl], dpi_s[:, sl]
            dwa_ref[g] += _dot_tn(cg, dpr_b)
            dwx_ref[g] += _dot_tn(cg, dpi_b)
            dcs.append(_dot_nt(dpr_b, wa_ref[g]) + _dot_nt(dpi_b, wx_ref[g]))
        dc = dcbuf[0:tm, :] + jnp.concatenate(dcs, axis=1)

        dcbuf[0:tm, :] = dc
        conv_sums[...] = jnp.zeros_like(conv_sums)

        def conv_back(k, carry):
            off = pl.multiple_of(k * slab, slab)
            rw = pl.ds(off, slab)
            for q in range(D_MODEL // lanes):
                ln = slice(lanes * q, lanes * (q + 1))
                blocks = [dcbuf[pl.ds(pl.multiple_of(off + 8 * b, 8), 8), ln] for b in range(3)]
                x_v = x_ref[rw, ln].astype(F32)
                now = jnp.concatenate(blocks[:2], axis=0)
                dlin = cw_ref[3:4, ln] * now
                conv_sums[3, :, ln] += now * x_v
                conv_sums[4, :, ln] += now
                for back in (1, 2, 3):
                    turned = [pltpu.roll(b, 8 - back, 0) for b in blocks]
                    later = jnp.concatenate([jnp.where(sub_q < 8 - back, turned[0], turned[1]),
                                             jnp.where(sub_q < 8 - back, turned[1], turned[2])], axis=0)
                    dlin = dlin + cw_ref[3 - back:4 - back, ln] * later
                    conv_sums[3 - back, :, ln] += later * x_v
                dseg_ref[rw, ln] = dlin.astype(BF16)
            return carry

        lax.fori_loop(0, tm // slab, conv_back, 0)
        dcbuf[tm:tm + 8, :] = dcbuf[0:8, :]
        for n in range(5):
            sm_ref[n:n + 1, :] += jnp.sum(conv_sums[n], axis=0, keepdims=True)
        sm_ref[5:6, :] += jnp.sum(sums[1], axis=0, keepdims=True)
        sm_ref[6:7, :] += jnp.sum(sums[2], axis=0, keepdims=True)
        sm_ref[7:8, :] += jnp.sum(sums[0], axis=0, keepdims=True) * (LRU_C * _sigmoid(-lam_v))

        @pl.when(step == nt - 1)
        def _():
            push.wait()

    rowb = pl.BlockSpec((tm, D_MODEL), lambda s: (nt - 1 - s, 0))
    t16 = tm // 16
    prev8 = pl.BlockSpec((16, D_MODEL), lambda s: (jnp.maximum((nt - 1 - s) * t16 - 1, 0), 0))
    seg = lambda k: pl.BlockSpec((tm, D_MODEL), lambda s, k=k: (nt - 1 - s, k))
    vec = pl.BlockSpec((1, D_MODEL), lambda s: (0, 0))
    mat = pl.BlockSpec((LRU_BLOCKS, LRU_BLOCK, LRU_BLOCK), lambda s: (0, 0, 0))
    mshape = jax.ShapeDtypeStruct((LRU_BLOCKS, LRU_BLOCK, LRU_BLOCK), F32)
    n_in = 10
    return pl.pallas_call(
        body, name="lru_bwd", grid=(nt,),
        in_specs=[rowb, rowb, prev8, pl.BlockSpec((3, tm, D_MODEL), lambda s: (0, nt - 1 - s, 0)), seg(4), seg(5),
                  pl.BlockSpec((4, D_MODEL), lambda s: (0, 0)), vec, mat, mat, ANY] + [ANY] * n_s1,
        out_specs=[pl.BlockSpec((tm, 2 * D_MODEL), lambda s: (nt - 1 - s, 2)), mat, mat,
                   pl.BlockSpec((8, D_MODEL), lambda s: (0, 0))] + [ANY] * n_s1,
        out_shape=[jax.ShapeDtypeStruct(dproj.shape, dproj.dtype), mshape, mshape,
                   jax.ShapeDtypeStruct((8, D_MODEL), F32)] + S1_SHAPES,
        input_output_aliases={n_in: 0},
        scratch_shapes=[pltpu.VMEM((tm + 8, D_MODEL), F32), pltpu.VMEM((tm, D_MODEL), F32),
                        pltpu.VMEM((tm, D_MODEL), F32), pltpu.VMEM((tm, D_MODEL), F32),
                        pltpu.VMEM((tm, D_MODEL), F32), pltpu.VMEM((tm + 8, D_MODEL), F32),
                        pltpu.VMEM((tm, D_MODEL), BF16), pltpu.VMEM((tm, D_MODEL), BF16),
                        pltpu.VMEM((3, 16, D_MODEL), F32), pltpu.VMEM((5, 16, D_MODEL), F32),
                        pltpu.VMEM((1, D_MODEL), F32), pltpu.VMEM((1, D_MODEL), F32)] + _push_sems(n_s1),
        compiler_params=pltpu.CompilerParams(dimension_semantics=("arbitrary",), vmem_limit_bytes=VMEM_LIMIT,
                                             has_side_effects=True),
    )(dzl, hs, hs, cri, proj, proj, conv_w, lam, wa_g, wx_g, dproj, *s1_grads)


def _retention_bwd(dzr, o, proj, states, cos2, sin2, dec, dproj, ride):
    rows = dzr.shape[0]
    n_chunks = rows // CHUNK
    per_step = _chunks_per_step(n_chunks)
    n_steps = n_chunks // per_step
    tm = per_step * CHUNK
    n_r = ride.n

    def body(dzr_ref, o_ref, q_ref, k_ref, v_ref, g_ref, st_ref, c_ref, s_ref, dec_ref, dproj_in, *refs):
        del dproj_in
        dseg_ref = refs[n_r]
        dstate = refs[2 * n_r + 1]
        push = ride.push(refs[:n_r], refs[n_r + 1:2 * n_r + 1], refs[2 * n_r + 2:])

        @pl.when(pl.program_id(0) == 0)
        def _():
            push.start()
            dstate[...] = jnp.zeros_like(dstate)

        for h in range(HEADS):
            sl = slice(HEAD_DIM * h, HEAD_DIM * (h + 1))
            intra, qd, kd, cd = dec_ref[0, h], dec_ref[1, h], dec_ref[2, h], dec_ref[3, h]
            dst = dstate[h]
            for c in reversed(range(per_step)):
                rw = slice(CHUNK * c, CHUNK * (c + 1))
                cos_t, sin_t = c_ref[rw, :], s_ref[rw, :]
                o = o_ref[rw, sl].astype(F32)
                g = g_ref[rw, sl].astype(F32)
                dzr_v = dzr_ref[rw, sl].astype(F32)
                sg = _sigmoid(g)
                r = lax.rsqrt(jnp.mean(o * o, axis=-1, keepdims=True) + NORM_EPS)
                on = o * r
                dseg_ref[rw, 3 * D_MODEL + HEAD_DIM * h:3 * D_MODEL + HEAD_DIM * (h + 1)] = (
                    dzr_v * on * (sg * (1.0 + g * (1.0 - sg)))).astype(BF16)
                don = dzr_v * (g * sg)
                do = r * (don - on * jnp.mean(don * on, axis=-1, keepdims=True))
                dob = do.astype(BF16)

                qh = _rot(q_ref[rw, sl].astype(F32), cos_t, sin_t)
                kh = _rot(k_ref[rw, sl].astype(F32), cos_t, sin_t) * QK_SCALE
                qb, kb, vb = qh.astype(BF16), kh.astype(BF16), v_ref[rw, sl]
                s = (_dot_nt(qb, kb) * intra).astype(BF16)
                ds = (_dot_nt(dob, vb) * intra).astype(BF16)
                st_b = st_ref[c, h].astype(BF16)
                dst_b = dst.astype(BF16)
                dv = _dot_tn(s, dob) + _dot((kh * kd).astype(BF16), dst_b)
                dq = _dot(ds, kb) + _dot_nt(dob, st_b) * qd
                dk = _dot_tn(ds, qb) + _dot_nt(vb, dst_b) * kd
                dst = dst * cd + _dot_tn((qh * qd).astype(BF16), dob)
                dseg_ref[rw, 2 * D_MODEL + HEAD_DIM * h:2 * D_MODEL + HEAD_DIM * (h + 1)] = dv.astype(BF16)
                dseg_ref[rw, sl] = _rot_t(dq, cos_t, sin_t).astype(BF16)
                dseg_ref[rw, D_MODEL + HEAD_DIM * h:D_MODEL + HEAD_DIM * (h + 1)] = (
                    _rot_t(dk, cos_t, sin_t) * QK_SCALE).astype(BF16)
            dstate[h] = dst

        @pl.when(pl.program_id(0) == n_steps - 1)
        def _():
            push.wait()

    rev = lambda s: n_steps - 1 - s
    rowb = pl.BlockSpec((tm, D_MODEL), lambda s: (rev(s), 0))
    seg = lambda k: pl.BlockSpec((tm, D_MODEL), lambda s, k=k: (rev(s), k))
    tab = pl.BlockSpec((tm, HEAD_DIM), lambda s: (rev(s), 0))
    return pl.pallas_call(
        body, name="retention_bwd", grid=(n_steps,),
        in_specs=[rowb, rowb, seg(0), seg(1), seg(2), seg(3),
                  pl.BlockSpec((per_step, HEADS, HEAD_DIM, HEAD_DIM), lambda s: (rev(s), 0, 0, 0)), tab, tab,
                  pl.BlockSpec((4, HEADS, CHUNK, CHUNK), lambda s: (0, 0, 0, 0)), ANY] + ride.specs(),
        out_specs=[pl.BlockSpec((tm, 4 * D_MODEL), lambda s: (rev(s), 0))] + ride.specs(),
        out_shape=[jax.ShapeDtypeStruct(dproj.shape, dproj.dtype)] + ride.out_shapes,
        input_output_aliases={10: 0},
        scratch_shapes=[pltpu.VMEM((HEADS, HEAD_DIM, HEAD_DIM), F32)] + ride.scratch(),
        compiler_params=pltpu.CompilerParams(dimension_semantics=("arbitrary",), vmem_limit_bytes=VMEM_LIMIT,
                                             has_side_effects=True),
    )(dzr, o, proj, proj, proj, proj, states, cos2, sin2, dec, dproj, *ride.arrays)


S2_SHAPES = [
    jax.ShapeDtypeStruct((N_DEV, LRU_BLOCKS, LRU_ROWS, LRU_BLOCK), F32),
    jax.ShapeDtypeStruct((N_DEV, LRU_BLOCKS, LRU_ROWS, LRU_BLOCK), F32),
]


def _s2_parts(ins, p):
    return [r.at[p] for r in ins]


def _in_proj_bwd(dproj, win_g, h0, norm_w, dh1, d_win_far, s2_grads, pack_early):
    rows = h0.shape[0]
    tm = _tile(rows, 320)
    n_i = rows // tm
    n_s2 = len(s2_grads)
    n_far = len(WIN_FAR)
    pack_rows = pack_early.shape[0]

    def body(dseg_ref, w_ref, h0_ref, nw_ref, dh1_ref, far_ref, early_ref, *refs):
        s2_refs = refs[:n_s2]
        dh0_ref, dw_ref, far_land = refs[n_s2:n_s2 + 3]
        land_refs = refs[n_s2 + 3:2 * n_s2 + 3]
        early_land, late_land = refs[2 * n_s2 + 3:2 * n_s2 + 5]
        (send_sems, recv_sems, loc_sems, far_send_sems, far_recv_sems, late_buf) = refs[2 * n_s2 + 5:2 * n_s2 + 11]
        early_sems, late_sems = refs[2 * n_s2 + 11:2 * n_s2 + 14], refs[2 * n_s2 + 14:]
        i = pl.program_id(0)
        push = _Push(lambda p: _s2_parts(s2_refs, p), lambda s: [r.at[s] for r in land_refs],
                     (send_sems, recv_sems, loc_sems), n_s2)
        early = _Push(lambda p: [early_ref], lambda s: [early_land.at[s]], tuple(early_sems), 1)
        late = _Push(lambda p: [late_buf], lambda s: [late_land.at[s]], tuple(late_sems), 1)

        def far_copy(n):
            return pltpu.make_async_remote_copy(src_ref=far_ref.at[n], dst_ref=far_land.at[n],
                                                send_sem=far_send_sems.at[n], recv_sem=far_recv_sems.at[n],
                                                device_id=_peer(WIN_FAR[n])[0], device_id_type=MESH_ID)

        @pl.when(i == 0)
        def _():
            for n in range(n_far):
                far_copy(n).start()
            push.start()
            early.start()
            dw_ref[...] = jnp.zeros_like(dw_ref)

        du = _dot_nt(dseg_ref[:, 0:D_MODEL], w_ref[0])
        for j in range(1, N_DEV):
            du = du + _dot_nt(dseg_ref[:, D_MODEL * j:D_MODEL * (j + 1)], w_ref[j])
        dx, dw = _rms_bwd(h0_ref[...], nw_ref[...], du)
        dw_ref[0:1, :] += dw
        dh0 = dh1_ref[...] + dx
        dh0_ref[...] = dh0

        @pl.when(i == 0)
        def _():
            late_buf[8:8 + N_META, :] = dh0[PAD_ROWS:CHUNK, :]

        @pl.when(i == n_i - 1)
        def _():
            late_buf[0:8, :] = dw_ref[...]
            late.start()
            for n in range(n_far):
                far_copy(n).wait_recv()
            for n in range(n_far):
                far_copy(n).wait_send()
            push.wait()
            early.wait()
            late.wait()

    row = pl.BlockSpec((tm, D_MODEL), lambda i: (i, 0))
    vec = pl.BlockSpec((1, D_MODEL), lambda i: (0, 0))
    return pl.pallas_call(
        body, name="in_proj_bwd", grid=(n_i,),
        in_specs=[pl.BlockSpec((tm, N_DEV * D_MODEL), lambda i: (i, 0)),
                  pl.BlockSpec((N_DEV, D_MODEL, D_MODEL), lambda i: (0, 0, 0), pipeline_mode=pl.Buffered(1)),
                  row, vec, row, ANY, ANY] + [ANY] * n_s2,
        out_specs=[row, pl.BlockSpec((8, D_MODEL), lambda i: (0, 0)), ANY] + [ANY] * n_s2 + [ANY, ANY],
        out_shape=[jax.ShapeDtypeStruct((rows, D_MODEL), F32), jax.ShapeDtypeStruct((8, D_MODEL), F32),
                   jax.ShapeDtypeStruct((n_far, D_MODEL, D_MODEL), BF16)] + S2_SHAPES
        + [jax.ShapeDtypeStruct((N_DEV, pack_rows, D_MODEL), F32)] * 2,
        scratch_shapes=_push_sems(n_s2) + [pltpu.SemaphoreType.DMA((n_far,)), pltpu.SemaphoreType.DMA((n_far,)),
                                           pltpu.VMEM((pack_rows, D_MODEL), F32)] + _push_sems(1) + _push_sems(1),
        compiler_params=pltpu.CompilerParams(dimension_semantics=("arbitrary",),
                                             vmem_limit_bytes=VMEM_LIMIT, has_side_effects=True),
    )(dproj, win_g, h0, norm_w, dh1, d_win_far, pack_early, *s2_grads)


def _adamw(g_slots, w, m, v, more_slots=None):
    slots, rows, cols = g_slots.shape
    extra = [] if more_slots is None else [more_slots]
    tr = rows
    for cand in (256, 128, 64, 32, 16, 8):
        if rows % cand == 0 and rows > cand:
            tr = cand
            break

    def body(g_ref, *refs):
        w_ref, m_ref, v_ref, go_ref, d_ref, mo_ref, vo_ref = refs[len(extra):]
        g = g_ref[0].astype(F32)
        for s in range(1, slots):
            g = g + g_ref[s].astype(F32)
        for more_ref in refs[:len(extra)]:
            for s in range(more_ref.shape[0]):
                g = g + more_ref[s].astype(F32)
        m2 = ADAM_B1 * m_ref[...] + (1.0 - ADAM_B1) * g
        v2 = ADAM_B2 * v_ref[...] + (1.0 - ADAM_B2) * (g * g)
        m_hat = m2 / (1.0 - ADAM_B1 ** ADAM_STEP)
        v_hat = v2 / (1.0 - ADAM_B2 ** ADAM_STEP)
        go_ref[...] = g
        d_ref[...] = -ADAM_LR * (m_hat / (jnp.sqrt(v_hat) + ADAM_EPS) + ADAM_WD * w_ref[...])
        mo_ref[...] = m2
        vo_ref[...] = v2

    blk = pl.BlockSpec((tr, cols), lambda i: (i, 0))
    shape = jax.ShapeDtypeStruct((rows, cols), F32)
    return pl.pallas_call(
        body, name="adamw", grid=(rows // tr,),
        in_specs=[pl.BlockSpec((slots, tr, cols), lambda i: (0, i, 0))]
        + [pl.BlockSpec((t.shape[0], tr, cols), lambda i: (0, i, 0)) for t in extra] + [blk, blk, blk],
        out_specs=[blk] * 4, out_shape=[shape] * 4,
        compiler_params=_cparams(("parallel",)),
    )(g_slots, *extra, w, m, v)


def _sum_slots(packs):
    slots, rows, cols = packs.shape

    def body(p_ref, o_ref):
        acc = p_ref[0]
        for s in range(1, slots):
            acc = acc + p_ref[s]
        o_ref[...] = acc

    return pl.pallas_call(
        body, name="sum_slots", out_shape=jax.ShapeDtypeStruct((rows, cols), F32),
        compiler_params=pltpu.CompilerParams(vmem_limit_bytes=VMEM_LIMIT),
    )(packs)


def _gather_small(small):
    shapes = [jax.ShapeDtypeStruct((N_DEV,) + small.shape, F32)]
    return _push_call("gather_small", [small], shapes,
                      lambda ins, p: list(ins), lambda outs, s: [r.at[s] for r in outs])[0]


PACK_CONV_W, PACK_CONV_B, PACK_BA, PACK_BX, PACK_LAM = 0, 4, 5, 6, 7
PACK_FFN_NORM, PACK_SQ_ERR, PACK_FINAL_NORM, PACK_MIX_NORM, PACK_META = 8, 16, 17, 24, 32


def kernel(x, meta_tokens, mix_norm_w, w_in, conv_w, conv_b, lru_wa, lru_ba, lru_wx, lru_bx, lru_lambda, w_branch_ret, w_branch_lru, w_out, ffn_norm_w, w_ffn_in, w_ffn_out, final_norm_w, loss_target, m_meta_tokens, m_mix_norm_w, m_w_in, m_conv_w, m_conv_b, m_lru_wa, m_lru_ba, m_lru_wx, m_lru_bx, m_lru_lambda, m_w_branch_ret, m_w_branch_lru, m_w_out, m_ffn_norm_w, m_w_ffn_in, m_w_ffn_out, m_final_norm_w, v_meta_tokens, v_mix_norm_w, v_w_in, v_conv_w, v_conv_b, v_lru_wa, v_lru_ba, v_lru_wx, v_lru_bx, v_lru_lambda, v_w_branch_ret, v_w_branch_lru, v_w_out, v_ffn_norm_w, v_w_ffn_in, v_w_ffn_out, v_final_norm_w):
    me = _my_index()
    pad4 = ((0, 4), (0, 0))
    fw = final_norm_w.reshape(1, D_MODEL)

    small = jnp.concatenate([meta_tokens, jnp.pad(conv_w[0], pad4)], axis=0)
    small_g = _gather_small(small)
    meta_full = small_g[:, :N_META].transpose(1, 0, 2).reshape(N_META, D_MODEL)
    conv_w_full = small_g[:, N_META:N_META + 4].transpose(1, 0, 2).reshape(4, D_MODEL)
    mixer_shards = [w_branch_ret[0].astype(BF16), w_branch_lru[0].astype(BF16), w_out[0].astype(BF16),
                    lru_wa[0].astype(BF16), lru_wx[0].astype(BF16)]
    wfi_shard = jnp.pad(w_ffn_in[0].astype(BF16), ((0, 0), (0, FFN_GROUP - FFN_SHARD)))
    own_slot = lambda ins, p: list(ins)

    rows = x.shape[1] + CHUNK
    h0 = jnp.concatenate([jnp.zeros((PAD_ROWS, D_MODEL), F32), meta_full, x[0]], axis=0)
    cos2, sin2 = _rope_tables(rows)
    dec = _retention_consts()

    me_arr = me.astype(jnp.int32).reshape(1)
    proj, u, win_g = _in_proj(h0, mix_norm_w, w_in[0].astype(BF16), me_arr)
    o, zr, states, wbr_g, wbl_g, wout_g, wa_g, wx_g = _retention_fwd(
        proj, cos2, sin2, dec, _mixer_weights_ride(mixer_shards))
    wbr, wbl, wout = (t.reshape(D_MODEL, D_MODEL) for t in (wbr_g, wbl_g, wout_g))
    wa_g, wx_g = _from_owners(wa_g), _from_owners(wx_g)
    gather_wfi = _Ride([wfi_shard], [jax.ShapeDtypeStruct((N_DEV, D_MODEL, FFN_GROUP), BF16)],
                       own_slot, _slot_of_sender, gather_by_chip=True)
    hs, zl, cri, wfi_g = _lru_fwd(proj, conv_w_full, conv_b, lru_ba, lru_bx, lru_lambda, wa_g, wx_g, gather_wfi)
    h1, yr, yl, mixed, wfo_g = _mix_fwd(zr, zl, proj, h0, wbr, wbl, wout, _wfo_ride(w_ffn_out[0].astype(BF16)))
    u2, g, up, act, dh2, red = _ffn_fwd_loss(h1, ffn_norm_w, wfi_g, wfo_g, fw, loss_target[0])

    d_wfo = _wgrad(act, dh2, FFN_GROUP, D_MODEL, BF16)[:, 0]
    dgu, dh1, dw_ffn_norm = _ffn_bwd(dh2, g, up, h1, ffn_norm_w, wfi_g, wfo_g)
    d_wfi = _wgrad(u2, dgu, D_MODEL, FFN_GROUP, BF16, b_halves=True, row_cap=4160)[0]
    d_wout = _wgrad(mixed, dh1, D_MODEL, D_MODEL, BF16)[0, 0]
    dyr, dyl, dproj, dzr, dzl = _mix_bwd(dh1, yr, yl, proj, wbr, wbl, wout)
    d_wbr = _wgrad(zr, dyr, D_MODEL, D_MODEL, BF16)[0, 0]
    d_wbl = _wgrad(zl, dyl, D_MODEL, D_MODEL, BF16)[0, 0]
    dproj, d_wa, d_wx, lru_small, r_fi = _lru_bwd(
        dzl, hs, cri, proj, dproj, conv_w_full, lru_lambda, wa_g, wx_g, [d_wfi])
    mix_shape = jax.ShapeDtypeStruct((N_DEV, D_MODEL // N_DEV, D_MODEL), BF16)
    wfo_shape = jax.ShapeDtypeStruct((N_DEV, FFN_OUT_SHARD, D_MODEL), BF16)
    part_of_owner = lambda ins, p: [r.at[p] for r in ins[:3]] + [ins[3].at[p // 2, _half_rows(p), :]]
    scatter_mix = _Ride([t.reshape(mix_shape.shape) for t in (d_wbr, d_wbl, d_wout)] + [d_wfo],
                        [mix_shape] * 3 + [wfo_shape], part_of_owner, _slot_of_sender)
    dproj, r_br, r_bl, r_out, r_fo = _retention_bwd(dzr, o, proj, states, cos2, sin2, dec, dproj, scatter_mix)
    d_win_far, r_in = _wgrad_w_in(u, dproj, me_arr)
    pack_early = jnp.concatenate([lru_small, dw_ffn_norm, red], axis=0)
    dh0, _, r_in_far, r_wa, r_wx, packs_early, packs_late = _in_proj_bwd(
        dproj, win_g, h0, mix_norm_w, dh1, d_win_far, [_by_owner(d_wa), _by_owner(d_wx)], pack_early)
    grad_x = dh0[CHUNK:]

    small_sum = jnp.concatenate([_sum_slots(packs_early), _sum_slots(packs_late)], axis=0)
    loss = (0.5 / D_MODEL) * jnp.sum(small_sum[PACK_SQ_ERR])

    def big_update(slots, w, m, v, more_slots=None):
        shape = w.shape
        w2, m2, v2 = (t.reshape(slots.shape[1:]) for t in (w, m, v))
        return [t.reshape(shape) for t in _adamw(slots, w2, m2, v2, more_slots)]

    res = {}
    res["w_in"] = big_update(r_in, w_in, m_w_in, v_w_in, r_in_far)
    res["w_branch_ret"] = big_update(r_br, w_branch_ret, m_w_branch_ret, v_w_branch_ret)
    res["w_branch_lru"] = big_update(r_bl, w_branch_lru, m_w_branch_lru, v_w_branch_lru)
    res["w_out"] = big_update(r_out, w_out, m_w_out, v_w_out)
    res["w_ffn_in"] = big_update(r_fi[:, :, :FFN_SHARD], w_ffn_in, m_w_ffn_in, v_w_ffn_in)
    res["w_ffn_out"] = big_update(r_fo, w_ffn_out, m_w_ffn_out, v_w_ffn_out)
    res["lru_wa"] = big_update(r_wa.reshape(N_DEV, LRU_BLOCKS * LRU_ROWS, LRU_BLOCK), lru_wa, m_lru_wa, v_lru_wa)
    res["lru_wx"] = big_update(r_wx.reshape(N_DEV, LRU_BLOCKS * LRU_ROWS, LRU_BLOCK), lru_wx, m_lru_wx, v_lru_wx)

    col = me * HEAD_DIM
    g_meta = lax.dynamic_slice(small_sum, (PACK_META, col), (N_META, HEAD_DIM))
    g_conv = lax.dynamic_slice(small_sum, (PACK_CONV_W, col), (8, HEAD_DIM))
    small_names = ["mix_norm_w", "conv_b", "lru_ba", "lru_bx", "lru_lambda", "ffn_norm_w", "final_norm_w"]
    small_rows = [PACK_MIX_NORM, PACK_CONV_B, PACK_BA, PACK_BX, PACK_LAM, PACK_FFN_NORM, PACK_FINAL_NORM]
    small_w = [mix_norm_w, conv_b, lru_ba, lru_bx, lru_lambda, ffn_norm_w, fw]
    small_m = [m_mix_norm_w, m_conv_b, m_lru_ba, m_lru_bx, m_lru_lambda, m_ffn_norm_w, m_final_norm_w.reshape(1, -1)]
    small_v = [v_mix_norm_w, v_conv_b, v_lru_ba, v_lru_bx, v_lru_lambda, v_ffn_norm_w, v_final_norm_w.reshape(1, -1)]

    def pack_small(vec_list, meta_t, conv_t):
        return jnp.concatenate([t.reshape(8, HEAD_DIM) for t in vec_list] + [meta_t, jnp.pad(conv_t[0], pad4)], axis=0)

    g_small = jnp.concatenate([small_sum[r].reshape(8, HEAD_DIM) for r in small_rows] + [g_meta, g_conv], axis=0)
    outs_small = _adamw(g_small[None], pack_small(small_w, meta_tokens, conv_w),
                        pack_small(small_m, m_meta_tokens, m_conv_w), pack_small(small_v, v_meta_tokens, v_conv_w))
    for idx, name in enumerate(small_names):
        shape = final_norm_w.shape if name == "final_norm_w" else (1, D_MODEL)
        res[name] = [t[8 * idx:8 * idx + 8].reshape(shape) for t in outs_small]
    res["meta_tokens"] = [t[56:72] for t in outs_small]
    res["conv_w"] = [t[72:76].reshape(1, 4, HEAD_DIM) for t in outs_small]

    order = ["meta_tokens", "mix_norm_w", "w_in", "conv_w", "conv_b", "lru_wa", "lru_ba", "lru_wx", "lru_bx",
             "lru_lambda", "w_branch_ret", "w_branch_lru", "w_out", "ffn_norm_w", "w_ffn_in", "w_ffn_out",
             "final_norm_w"]
    out = [loss, grad_x[None]]
    for kind in range(4):
        out += [res[name][kind] for name in order]
    return tuple(out)
```
